```python
import jax
import jax.numpy as jnp
from jax import lax
import numpy as np

D_MODEL = 1024
BATCH = 8
SEQ = 2048
DEPTH = 1
DEC_BATCH = 128
DEC_SEQ = 1
PAST_LEN = 16384
PAGE_SIZE = 128

MIX_WIDTH = D_MODEL
RWKV_HEAD_DIM = 64
RWKV_WIDTH = MIX_WIDTH // 2
RWKV_HEADS = RWKV_WIDTH // RWKV_HEAD_DIM
RWKV_DECAY_LORA = 64
RWKV_A_LORA = 64
RWKV_GATE_LORA = 128
RWKV_PROJ = 3 * RWKV_WIDTH + RWKV_DECAY_LORA + RWKV_A_LORA + RWKV_GATE_LORA
RWKV_GN_EPS = 64e-5
GLA_WIDTH = MIX_WIDTH - RWKV_WIDTH
GLA_HEADS = 4
GLA_DK = GLA_WIDTH // 2 // GLA_HEADS
GLA_DV = GLA_WIDTH // GLA_HEADS
GLA_GATE_LORA = 16
GLA_GATE_NORMALIZER = 16.0
GLA_CHUNK = 64
GLA_PROJ = 2 * GLA_HEADS * GLA_DK + 2 * GLA_WIDTH + GLA_GATE_LORA
PROJ_TOTAL = RWKV_PROJ + GLA_PROJ
N_EXPERTS = 32
TOP_K = 4
MOE_FF = D_MODEL
SWIGLU_LIMIT = 7.0
SWIGLU_ALPHA = 1.702
MOE_BLOCK = 128
NORM_EPS = 1e-5

kernel_name = 'hybrid_rwkv7_gla_moe_step'


def rms_norm(x, gain, eps=NORM_EPS):
    xf = x.astype(jnp.float32)
    y = xf * lax.rsqrt(jnp.mean(xf * xf, axis=-1, keepdims=True) + eps)
    return (y * gain.astype(jnp.float32)).astype(x.dtype)


def rwkv7_step(S, inp):
    r, w, k, v, kk, a = inp
    sa = jnp.einsum('bhvk,bhk->bhv', S, -kk)
    S = S * w[:, :, None, :] + sa[..., None] * (kk * a)[:, :, None, :] + v[..., None] * k[:, :, None, :]
    return S, jnp.einsum('bhvk,bhk->bhv', S, r)


def rwkv7_mix(z, shift0, S0, mu, w0, w2, a0, a2, g2, k_k, k_a, r_k, ln_w, ln_b):
    B, L, _ = z.shape
    f32 = jnp.float32
    W = RWKV_WIDTH
    z_prev = jnp.concatenate([shift0[:, None, :].astype(z.dtype), z[:, :-1]], axis=1)
    zs = z + mu * (z_prev - z)
    r, k, v = zs[..., :W], zs[..., W:2 * W], zs[..., 2 * W:3 * W]
    o_w = 3 * W
    o_a = o_w + RWKV_DECAY_LORA
    o_g = o_a + RWKV_A_LORA
    zw, za, zg = zs[..., o_w:o_a], zs[..., o_a:o_g], zs[..., o_g:]
    w = -jax.nn.softplus(-(w0 + jnp.tanh(zw) @ w2).astype(f32)) - 0.5
    decay = jnp.exp(-jnp.exp(w))
    a = jax.nn.sigmoid((a0 + za @ a2).astype(f32))
    g = jax.nn.sigmoid(zg) @ g2
    heads = lambda t: t.astype(f32).reshape(B, L, RWKV_HEADS, RWKV_HEAD_DIM)
    kk = heads(k * k_k)
    kk = kk / jnp.maximum(jnp.sqrt(jnp.sum(kk * kk, axis=-1, keepdims=True)), 1e-12)
    k = k.astype(f32) * (1.0 + (a - 1.0) * k_a.astype(f32))
    rh, kh, vh, ah, dh = heads(r), heads(k), heads(v), heads(a), heads(decay)
    tfirst = lambda t: jnp.swapaxes(t, 0, 1)
    S, o = lax.scan(rwkv7_step, S0.astype(f32), tuple(tfirst(t) for t in (rh, dh, kh, vh, kk, ah)))
    o = tfirst(o)
    mean = jnp.mean(o, axis=-1, keepdims=True)
    var = jnp.mean(jnp.square(o - mean), axis=-1, keepdims=True)
    o = (o - mean) * lax.rsqrt(var + RWKV_GN_EPS)
    o = o.reshape(B, L, W) * ln_w.astype(f32) + ln_b.astype(f32)
    bonus = jnp.sum(rh * kh * r_k.astype(f32), axis=-1, keepdims=True) * vh
    out = (o + bonus.reshape(B, L, W)) * g.astype(f32)
    return out.astype(z.dtype), z[:, -1], S.astype(S0.dtype)


def gla_chunked(q, k, v, gk, S0):
    B, L, H, DK = q.shape
    DV = v.shape[-1]
    C = min(GLA_CHUNK, L)
    n_c = -(-L // C)
    pad = n_c * C - L

    def blocks(t):
        t = jnp.pad(t, ((0, 0), (0, pad), (0, 0), (0, 0)))
        return t.reshape(B, n_c, C, H, t.shape[-1]).transpose(1, 0, 3, 2, 4)

    causal = jnp.tril(jnp.ones((C, C), dtype=bool))[None, None, :, :, None]

    def step(S, inp):
        qc, kc, vc, gc = inp
        G = jnp.cumsum(gc, axis=2)
        diff = G[:, :, :, None, :] - G[:, :, None, :, :]
        dec = jnp.exp(jnp.where(causal, diff, -jnp.inf))
        A = jnp.sum(qc[:, :, :, None, :] * kc[:, :, None, :, :] * dec, axis=-1)
        o = A @ vc + jnp.einsum('bhid,bhdv->bhiv', qc * jnp.exp(G), S)
        G_last = G[:, :, -1:, :]
        S = S * jnp.exp(G_last[:, :, 0, :])[..., None] + jnp.einsum('bhjd,bhjv->bhdv', kc * jnp.exp(G_last - G), vc)
        return S, o

    S, o = lax.scan(step, S0, (blocks(q), blocks(k), blocks(v), blocks(gk)))
    o = o.transpose(1, 0, 3, 2, 4).reshape(B, n_c * C, H, DV)[:, :L]
    return o, S


def gla_mix(z, S0, gk_w2, gk_b, norm_w):
    B, L, _ = z.shape
    f32 = jnp.float32
    QK = GLA_HEADS * GLA_DK
    o_v = 2 * QK
    o_g = o_v + GLA_WIDTH
    o_gk = o_g + GLA_WIDTH
    q, k, v, g, zgk = z[..., :QK], z[..., QK:o_v], z[..., o_v:o_g], z[..., o_g:o_gk], z[..., o_gk:]
    gk = jax.nn.log_sigmoid((zgk @ gk_w2 + gk_b).astype(f32)) / GLA_GATE_NORMALIZER
    hk = lambda t: t.astype(f32).reshape(B, L, GLA_HEADS, GLA_DK)
    o, S = gla_chunked(hk(q) * GLA_DK ** -0.5, hk(k), v.astype(f32).reshape(B, L, GLA_HEADS, GLA_DV), hk(gk), S0.astype(f32))
    o = o * lax.rsqrt(jnp.mean(o * o, axis=-1, keepdims=True) + NORM_EPS) * norm_w.astype(f32)
    out = o.reshape(B, L, GLA_WIDTH) * jax.nn.silu(g.astype(f32))
    return out.astype(z.dtype), S.astype(S0.dtype)


def moe_ffn(xn, w_router, b_router, w_gate, b_gate, w_up, b_up, w_down, b_down):
    n, d = xn.shape
    logits = (xn @ w_router + b_router).astype(jnp.float32)
    top_val, top_idx = lax.top_k(logits, TOP_K)
    gates = jax.nn.softmax(top_val, axis=-1)
    nk = n * TOP_K
    flat_e = top_idx.reshape(nk)
    flat_tok = jnp.repeat(jnp.arange(n, dtype=jnp.int32), TOP_K)
    flat_gate = gates.reshape(nk)
    order = jnp.argsort(flat_e)
    se = flat_e[order]
    counts = jnp.zeros((N_EXPERTS,), jnp.int32).at[flat_e].add(1)
    pcounts = (counts + MOE_BLOCK - 1) // MOE_BLOCK * MOE_BLOCK
    starts = jnp.cumsum(counts) - counts
    pends = jnp.cumsum(pcounts)
    pstarts = pends - pcounts
    dest = pstarts[se] + jnp.arange(nk, dtype=jnp.int32) - starts[se]
    n_blocks = (nk + N_EXPERTS * (MOE_BLOCK - 1) + MOE_BLOCK - 1) // MOE_BLOCK
    n_slots = n_blocks * MOE_BLOCK
    slot_tok = jnp.full((n_slots,), n, jnp.int32).at[dest].set(flat_tok[order])
    slot_gate = jnp.zeros((n_slots,), xn.dtype).at[dest].set(flat_gate[order].astype(xn.dtype))
    block_e = jnp.minimum(jnp.searchsorted(pends, jnp.arange(n_blocks, dtype=jnp.int32) * MOE_BLOCK, side='right'), N_EXPERTS - 1)
    x_pad = jnp.concatenate([xn, jnp.zeros((1, d), xn.dtype)], axis=0)
    xb = x_pad[slot_tok].reshape(n_blocks, MOE_BLOCK, d)

    def expert_block(args):
        xblk, e = args
        gt = xblk @ w_gate[e] + b_gate[e]
        up = xblk @ w_up[e] + b_up[e]
        gt = jnp.minimum(gt, SWIGLU_LIMIT)
        up = jnp.clip(up, -SWIGLU_LIMIT, SWIGLU_LIMIT)
        h = (up + 1.0) * gt * jax.nn.sigmoid(SWIGLU_ALPHA * gt)
        return h @ w_down[e] + b_down[e]

    yb = lax.map(expert_block, (xb, block_e))
    y = yb.reshape(n_slots, d) * slot_gate[:, None]
    return jax.ops.segment_sum(y, slot_tok, num_segments=n + 1)[:n]


def decoder_layer(x, shift0, wkv0, gla0, lp):
    (norm_mix, w_in, rw_mu, rw_w0, rw_w2, rw_a0, rw_a2, rw_g2, rw_k_k, rw_k_a, rw_r_k, rw_ln_w, rw_ln_b,
     gla_gk_w2, gla_gk_b, gla_norm_w, w_out, norm_ffn, w_router, b_router,
     w_gate, b_gate, w_up, b_up, w_down, b_down) = lp
    B, L, D = x.shape
    z = rms_norm(x, norm_mix) @ w_in
    o_r, new_shift, new_wkv = rwkv7_mix(z[..., :RWKV_PROJ], shift0, wkv0, rw_mu, rw_w0, rw_w2, rw_a0, rw_a2,
                                        rw_g2, rw_k_k, rw_k_a, rw_r_k, rw_ln_w, rw_ln_b)
    o_g, new_gla = gla_mix(z[..., RWKV_PROJ:], gla0, gla_gk_w2, gla_gk_b, gla_norm_w)
    h = x + jnp.concatenate([o_r, o_g], axis=-1) @ w_out
    f = moe_ffn(rms_norm(h, norm_ffn).reshape(B * L, D), w_router, b_router, w_gate, b_gate, w_up, b_up, w_down, b_down)
    return h + f.reshape(B, L, D), new_shift, new_wkv, new_gla


def setup_inputs(seed: int = 0) -> dict:
    key = jax.random.key(seed)
    ks = iter(jax.random.split(key, 40))
    f32 = jnp.float32

    def nrm(shape, scale):
        return scale * jax.random.normal(next(ks), shape, f32)

    return {
        'x_prompt': nrm((BATCH, SEQ, D_MODEL), 1.0),
        'x_sample': nrm((DEC_BATCH, DEC_SEQ, D_MODEL), 1.0),
        'state_rwkv_shift': nrm((DEPTH, DEC_BATCH, RWKV_PROJ), 1.0),
        'state_rwkv_wkv': nrm((DEPTH, DEC_BATCH, RWKV_HEADS, RWKV_HEAD_DIM, RWKV_HEAD_DIM), 0.3),
        'state_gla': nrm((DEPTH, DEC_BATCH, GLA_HEADS, GLA_DK, GLA_DV), 0.3),
        'norm_mix': 1.0 + nrm((DEPTH, D_MODEL), 0.01),
        'w_in': nrm((DEPTH, D_MODEL, PROJ_TOTAL), D_MODEL ** -0.5),
        'rw_mu': jax.random.uniform(next(ks), (DEPTH, RWKV_PROJ), f32, 0.1, 0.9),
        'rw_w0': nrm((DEPTH, RWKV_WIDTH), 0.5) - 1.0,
        'rw_w2': nrm((DEPTH, RWKV_DECAY_LORA, RWKV_WIDTH), 0.5 * RWKV_DECAY_LORA ** -0.5),
        'rw_a0': nrm((DEPTH, RWKV_WIDTH), 0.1),
        'rw_a2': nrm((DEPTH, RWKV_A_LORA, RWKV_WIDTH), 0.5 * RWKV_A_LORA ** -0.5),
        'rw_g2': nrm((DEPTH, RWKV_GATE_LORA, RWKV_WIDTH), RWKV_GATE_LORA ** -0.5),
        'rw_k_k': 0.85 + nrm((DEPTH, RWKV_WIDTH), 0.05),
        'rw_k_a': 1.0 + nrm((DEPTH, RWKV_WIDTH), 0.05),
        'rw_r_k': nrm((DEPTH, RWKV_HEADS, RWKV_HEAD_DIM), 0.1),
        'rw_ln_w': 1.0 + nrm((DEPTH, RWKV_WIDTH), 0.01),
        'rw_ln_b': nrm((DEPTH, RWKV_WIDTH), 0.01),
        'gla_gk_w2': nrm((DEPTH, GLA_GATE_LORA, GLA_HEADS * GLA_DK), GLA_GATE_LORA ** -0.5),
        'gla_gk_b': nrm((DEPTH, GLA_HEADS * GLA_DK), 0.1),
        'gla_norm_w': 1.0 + nrm((DEPTH, GLA_DV), 0.01),
        'w_out': nrm((DEPTH, MIX_WIDTH, D_MODEL), MIX_WIDTH ** -0.5),
        'norm_ffn': 1.0 + nrm((DEPTH, D_MODEL), 0.01),
        'w_router': nrm((DEPTH, D_MODEL, N_EXPERTS), D_MODEL ** -0.5),
        'b_router': nrm((DEPTH, N_EXPERTS), 0.01),
        'w_gate': nrm((DEPTH, N_EXPERTS, D_MODEL, MOE_FF), D_MODEL ** -0.5),
        'b_gate': nrm((DEPTH, N_EXPERTS, MOE_FF), 0.01),
        'w_up': nrm((DEPTH, N_EXPERTS, D_MODEL, MOE_FF), D_MODEL ** -0.5),
        'b_up': nrm((DEPTH, N_EXPERTS, MOE_FF), 0.01),
        'w_down': nrm((DEPTH, N_EXPERTS, MOE_FF, D_MODEL), MOE_FF ** -0.5),
        'b_down': nrm((DEPTH, N_EXPERTS, D_MODEL), 0.01),
        'norm_final': 1.0 + nrm((D_MODEL,), 0.01),
    }


def reference(x_prompt, x_sample, state_rwkv_shift, state_rwkv_wkv, state_gla, norm_mix, w_in, rw_mu, rw_w0,
              rw_w2, rw_a0, rw_a2, rw_g2, rw_k_k, rw_k_a, rw_r_k, rw_ln_w, rw_ln_b, gla_gk_w2, gla_gk_b,
              gla_norm_w, w_out, norm_ffn, w_router, b_router, w_gate, b_gate, w_up, b_up, w_down, b_down,
              norm_final):
    dt = x_prompt.dtype
    bp = x_prompt.shape[0]
    shift_p0 = jnp.zeros((bp, RWKV_PROJ), dt)
    wkv_p0 = jnp.zeros((bp, RWKV_HEADS, RWKV_HEAD_DIM, RWKV_HEAD_DIM), dt)
    gla_p0 = jnp.zeros((bp, GLA_HEADS, GLA_DK, GLA_DV), dt)
    y_p, y_s = x_prompt, x_sample
    sp, wp, gp, ss, ws, gs = [], [], [], [], [], []
    for l in range(DEPTH):
        lp = (norm_mix[l], w_in[l], rw_mu[l], rw_w0[l], rw_w2[l], rw_a0[l], rw_a2[l], rw_g2[l], rw_k_k[l],
              rw_k_a[l], rw_r_k[l], rw_ln_w[l], rw_ln_b[l], gla_gk_w2[l], gla_gk_b[l], gla_norm_w[l], w_out[l],
              norm_ffn[l], w_router[l], b_router[l], w_gate[l], b_gate[l], w_up[l], b_up[l], w_down[l], b_down[l])
        y_p, s_a, s_b, s_c = decoder_layer(y_p, shift_p0, wkv_p0, gla_p0, lp)
        sp.append(s_a)
        wp.append(s_b)
        gp.append(s_c)
        y_s, s_a, s_b, s_c = decoder_layer(y_s, state_rwkv_shift[l], state_rwkv_wkv[l], state_gla[l], lp)
        ss.append(s_a)
        ws.append(s_b)
        gs.append(s_c)
    y_prompt = rms_norm(y_p, norm_final)
    y_sample = rms_norm(y_s, norm_final)
    return (y_prompt, y_sample, jnp.stack(sp), jnp.stack(wp), jnp.stack(gp), jnp.stack(ss), jnp.stack(ws), jnp.stack(gs))
```

```python
import functools

import jax
import jax.numpy as jnp
from jax import lax
from jax.experimental import pallas as pl
from jax.experimental.pallas import tpu as pltpu

F32 = jnp.float32
BF16 = jnp.bfloat16
HIGHEST = lax.Precision.HIGHEST

D_MODEL = 1024
RW_WIDTH = 512
RW_HEADS = 8
RW_N = 64
RW_PROJ = 1792
RW_GN_EPS = 64e-5
GLA_HEADS = 4
GLA_DK = 64
GLA_DV = 128
GLA_WIDTH = 512
GLA_QK = GLA_HEADS * GLA_DK
GLA_PROJ = 1552
GLA_PROJ_PAD = 1664
GLA_LORA_PAD = 128
GLA_GATE_NORMALIZER = 16.0
N_EXPERTS = 32
TOP_K = 4
SWIGLU_LIMIT = 7.0
SWIGLU_ALPHA = 1.702
NORM_EPS = 1e-5

RW_CHUNK = 64
GLA_CHUNK = 16
SEQ_BLOCK = 512
TOK_BLOCK = 512
MOE_BLOCK = 512
VMEM_LIMIT = 56 * 1024 * 1024


def _dot(a, b, precision=None):
    return jnp.dot(a, b, preferred_element_type=F32, precision=precision)


def _dot_nt(a, b, precision=None):
    return lax.dot_general(a, b, (((1,), (1,)), ((), ())), preferred_element_type=F32, precision=precision)


def _dot_tn(a, b, precision=None):
    return lax.dot_general(a, b, (((0,), (0,)), ((), ())), preferred_element_type=F32, precision=precision)


def _sigmoid(x):
    return 1.0 / (1.0 + jnp.exp(-x))


def _softplus(x):
    return jnp.maximum(x, 0.0) + jnp.log(1.0 + jnp.exp(-jnp.abs(x)))


def _params(sem):
    return pltpu.CompilerParams(dimension_semantics=sem, vmem_limit_bytes=VMEM_LIMIT)


def _inproj_kernel(x_ref, gain_ref, wr_ref, wg_ref, zr_ref, zg_ref):
    x = x_ref[...]
    xn = x * lax.rsqrt(jnp.mean(x * x, axis=-1, keepdims=True) + NORM_EPS) * gain_ref[...]
    xb = xn.astype(BF16)
    zr_ref[...] = _dot(xb, wr_ref[...])
    zg_ref[...] = _dot(xb, wg_ref[...])


def _inproj(x, gain, w_r, w_g, tm):
    n = x.shape[0]
    return pl.pallas_call(
        _inproj_kernel,
        grid=(n // tm,),
        in_specs=[
            pl.BlockSpec((tm, D_MODEL), lambda i: (i, 0)),
            pl.BlockSpec((1, D_MODEL), lambda i: (0, 0)),
            pl.BlockSpec((D_MODEL, RW_PROJ), lambda i: (0, 0)),
            pl.BlockSpec((D_MODEL, GLA_PROJ_PAD), lambda i: (0, 0)),
        ],
        out_specs=[
            pl.BlockSpec((tm, RW_PROJ), lambda i: (i, 0)),
            pl.BlockSpec((tm, GLA_PROJ_PAD), lambda i: (i, 0)),
        ],
        out_shape=[
            jax.ShapeDtypeStruct((n, RW_PROJ), F32),
            jax.ShapeDtypeStruct((n, GLA_PROJ_PAD), F32),
        ],
        compiler_params=_params(("parallel",)),
    )(x, gain, w_r, w_g)


def _rwkv_features(zs, w0, w2p, a0, a2p, g2, k_k, k_a):
    W = RW_WIDTH
    r = zs[:, 0:W]
    k_raw = zs[:, W : 2 * W]
    v = zs[:, 2 * W : 3 * W]
    zwa = zs[:, 3 * W : 3 * W + 128]
    zg = zs[:, 3 * W + 128 :]
    w = -_softplus(-(w0 + _dot(jnp.tanh(zwa).astype(BF16), w2p))) - 0.5
    log_decay = -jnp.exp(w)
    a = _sigmoid(a0 + _dot(zwa.astype(BF16), a2p))
    g = _dot(_sigmoid(zg).astype(BF16), g2)
    kk_raw = k_raw * k_k
    k = k_raw * (1.0 + (a - 1.0) * k_a)
    return r, k, v, kk_raw, a, log_decay, g


def _level_mask(ri, ci, lvl):
    same = (ri >> (lvl + 1)) == (ci >> (lvl + 1))
    return same & (((ri >> lvl) & 1) == 1) & (((ci >> lvl) & 1) == 0)


def _unit_lower_inverse(low, ri, ci):
    eye = (ri == ci).astype(F32)
    t = eye - jnp.where(_level_mask(ri, ci, 0), low, 0.0)
    lvl = 1
    while (1 << lvl) < RW_CHUNK:
        n = jnp.where(_level_mask(ri, ci, lvl), low, 0.0)
        t = t - _dot(_dot(t, n), t)
        lvl += 1
    return t


def _rwkv_seq_kernel(z_ref, shift0_ref, s0_ref, mu_ref, w0_ref, w2_ref, a0_ref, a2_ref, g2_ref, kk_ref, ka_ref,
                     rk_ref, lnw_ref, lnb_ref, o_ref, sout_ref,
                     m_scr, prev_scr, r_scr, k_scr, v_scr, kkr_scr, a_scr, lw_scr, on_scr, bon_scr):
    C = RW_CHUNK
    t_idx = pl.program_id(1)
    tb = z_ref.shape[1]

    @pl.when(t_idx == 0)
    def _():
        prev_scr[...] = shift0_ref[0]
        for h in range(RW_HEADS):
            m_scr[h] = s0_ref[0, h].T

    z = z_ref[0]
    row = lax.broadcasted_iota(jnp.int32, z.shape, 0)
    z_prev = jnp.where(row == 0, prev_scr[...], pltpu.roll(z, 1, axis=0))
    prev_scr[...] = z[tb - 1 : tb, :]
    zs = z + mu_ref[...] * (z_prev - z)
    r, k, v, kk_raw, a, log_decay, g = _rwkv_features(
        zs, w0_ref[...], w2_ref[...], a0_ref[...], a2_ref[...], g2_ref[...], kk_ref[...], ka_ref[...])
    r_scr[...] = r
    k_scr[...] = k
    v_scr[...] = v
    kkr_scr[...] = kk_raw
    a_scr[...] = a
    lw_scr[...] = log_decay

    ri = lax.broadcasted_iota(jnp.int32, (C, C), 0)
    ci = lax.broadcasted_iota(jnp.int32, (C, C), 1)
    tril = ri >= ci
    stril = ri > ci
    eye = ri == ci
    tril_f = tril.astype(F32)
    rk_all = rk_ref[...]

    def chunk_body(c, carry):
        sl = pl.ds(pl.multiple_of(c * C, C), C)
        lw = lw_scr[sl, :]
        cum = _dot(tril_f, lw, precision=HIGHEST)
        cum_last = cum[C - 1 : C, :]
        e_incl = jnp.exp(cum)
        e_excl = jnp.exp(cum - lw)
        e_neg = jnp.exp(-cum)
        e_tail = jnp.exp(cum_last - cum)
        p_last = jnp.exp(cum_last)
        r_c, k_c, v_c, kkr_c, a_c = r_scr[sl, :], k_scr[sl, :], v_scr[sl, :], kkr_scr[sl, :], a_scr[sl, :]
        for h in range(RW_HEADS):
            hs = slice(h * RW_N, (h + 1) * RW_N)
            kk_h = kkr_c[:, hs]
            nrm = jnp.sqrt(jnp.sum(kk_h * kk_h, axis=-1, keepdims=True))
            al = kk_h / jnp.maximum(nrm, 1e-12)
            be = al * a_c[:, hs]
            r_h, k_h, v_h = r_c[:, hs], k_c[:, hs], v_c[:, hs]
            al_t = al * e_excl[:, hs]
            r_t = r_h * e_incl[:, hs]
            be_n = be * e_neg[:, hs]
            k_n = k_h * e_neg[:, hs]
            k_e = k_h * e_tail[:, hs]
            be_e = be * e_tail[:, hs]
            l_ab = jnp.where(stril, _dot_nt(al_t, be_n), 0.0)
            l_ak = jnp.where(stril, _dot_nt(al_t, k_n), 0.0)
            a_rb = jnp.where(tril, _dot_nt(r_t, be_n), 0.0)
            a_rk = jnp.where(tril, _dot_nt(r_t, k_n), 0.0)
            t_inv = _unit_lower_inverse(l_ab, ri, ci)
            a_til = _dot(t_inv, al_t)
            b_til = _dot(t_inv, _dot(l_ak, v_h))
            r_hat = r_t - _dot(a_rb, a_til)
            o_hat = _dot(a_rk, v_h) - _dot(a_rb, b_til)
            g_mat = jnp.where(eye, p_last[:, hs], 0.0) - _dot_tn(be_e, a_til)
            h_mat = _dot_tn(k_e, v_h) - _dot_tn(be_e, b_til)
            m0 = m_scr[h]
            o_h = _dot(r_hat, m0) + o_hat
            m_scr[h] = _dot(g_mat, m0) + h_mat
            mean = jnp.mean(o_h, axis=-1, keepdims=True)
            cen = o_h - mean
            var = jnp.mean(cen * cen, axis=-1, keepdims=True)
            on_scr[sl, hs] = cen * lax.rsqrt(var + RW_GN_EPS)
            bon_scr[sl, hs] = jnp.sum(r_h * k_h * rk_all[:, hs], axis=-1, keepdims=True) * v_h
        return carry

    lax.fori_loop(0, tb // C, chunk_body, 0)
    out = (on_scr[...] * lnw_ref[...] + lnb_ref[...] + bon_scr[...]) * g
    o_ref[0] = out.astype(o_ref.dtype)

    @pl.when(t_idx == pl.num_programs(1) - 1)
    def _():
        for h in range(RW_HEADS):
            sout_ref[0, h] = m_scr[h].T


def _rwkv_seq(z3, shift0, s0, rw, tb):
    b, l, _ = z3.shape
    const = lambda shape: pl.BlockSpec(shape, lambda i, j: (0,) * len(shape))
    wide = lambda: pltpu.VMEM((tb, RW_WIDTH), F32)
    return pl.pallas_call(
        _rwkv_seq_kernel,
        grid=(b, l // tb),
        in_specs=[
            pl.BlockSpec((1, tb, RW_PROJ), lambda i, j: (i, j, 0)),
            pl.BlockSpec((1, 1, RW_PROJ), lambda i, j: (i, 0, 0)),
            pl.BlockSpec((1, RW_HEADS, RW_N, RW_N), lambda i, j: (i, 0, 0, 0)),
            const((1, RW_PROJ)),
            const((1, RW_WIDTH)), const((128, RW_WIDTH)),
            const((1, RW_WIDTH)), const((128, RW_WIDTH)),
            const((128, RW_WIDTH)),
            const((1, RW_WIDTH)), const((1, RW_WIDTH)), const((1, RW_WIDTH)),
            const((1, RW_WIDTH)), const((1, RW_WIDTH)),
        ],
        out_specs=[
            pl.BlockSpec((1, tb, RW_WIDTH), lambda i, j: (i, j, 0)),
            pl.BlockSpec((1, RW_HEADS, RW_N, RW_N), lambda i, j: (i, 0, 0, 0)),
        ],
        out_shape=[
            jax.ShapeDtypeStruct((b, l, RW_WIDTH), BF16),
            jax.ShapeDtypeStruct((b, RW_HEADS, RW_N, RW_N), F32),
        ],
        scratch_shapes=[
            pltpu.VMEM((RW_HEADS, RW_N, RW_N), F32),
            pltpu.VMEM((1, RW_PROJ), F32),
            wide(), wide(), wide(), wide(), wide(), wide(), wide(), wide(),
        ],
        compiler_params=_params(("parallel", "arbitrary")),
    )(z3, shift0, s0, rw["mu"], rw["w0"], rw["w2p"], rw["a0"], rw["a2p"], rw["g2"], rw["k_k"], rw["k_a"],
      rw["r_k"], rw["ln_w"], rw["ln_b"])


def _rwkv_step_prep_kernel(z_ref, shift0_ref, mu_ref, w0_ref, w2_ref, a0_ref, a2_ref, g2_ref, kk_ref, ka_ref,
                           rk_ref, r_ref, k_ref, v_ref, al_ref, be_ref, dec_ref, g_ref, bon_ref):
    z = z_ref[...]
    zs = z + mu_ref[...] * (shift0_ref[...] - z)
    r, k, v, kk_raw, a, log_decay, g = _rwkv_features(
        zs, w0_ref[...], w2_ref[...], a0_ref[...], a2_ref[...], g2_ref[...], kk_ref[...], ka_ref[...])
    rk_all = rk_ref[...]
    for h in range(RW_HEADS):
        hs = slice(h * RW_N, (h + 1) * RW_N)
        kk_h = kk_raw[:, hs]
        nrm = jnp.sqrt(jnp.sum(kk_h * kk_h, axis=-1, keepdims=True))
        al = kk_h / jnp.maximum(nrm, 1e-12)
        al_ref[:, hs] = al
        be_ref[:, hs] = al * a[:, hs]
        bon_ref[:, hs] = jnp.sum(r[:, hs] * k[:, hs] * rk_all[:, hs], axis=-1, keepdims=True) * v[:, hs]
    r_ref[...] = r
    k_ref[...] = k
    v_ref[...] = v
    dec_ref[...] = jnp.exp(log_decay)
    g_ref[...] = g


def _rwkv_step_prep(z, shift0, rw):
    n = z.shape[0]
    out = jax.ShapeDtypeStruct((n, RW_WIDTH), F32)
    return pl.pallas_call(
        _rwkv_step_prep_kernel,
        out_shape=[out] * 8,
        compiler_params=pltpu.CompilerParams(vmem_limit_bytes=VMEM_LIMIT),
    )(z, shift0, rw["mu"], rw["w0"], rw["w2p"], rw["a0"], rw["a2p"], rw["g2"], rw["k_k"], rw["k_a"], rw["r_k"])


def _rwkv_step_kernel(s_ref, r_ref, k_ref, al_ref, be_ref, dec_ref, vcol_ref, snew_ref, ocol_ref):
    s = s_ref[...]
    sa = -jnp.sum(s * al_ref[...], axis=-1, keepdims=True)
    s_new = s * dec_ref[...] + sa * be_ref[...] + vcol_ref[...] * k_ref[...]
    snew_ref[...] = s_new
    ocol_ref[...] = jnp.sum(s_new * r_ref[...], axis=-1, keepdims=True)


def _rwkv_step(s0, r, k, al, be, dec, v, bb):
    n = s0.shape[0]
    rowv = lambda t: t.reshape(n, RW_HEADS, 1, RW_N)
    row_spec = pl.BlockSpec((bb, RW_HEADS, 1, RW_N), lambda i: (i, 0, 0, 0))
    col_spec = pl.BlockSpec((bb, RW_HEADS, RW_N, 1), lambda i: (i, 0, 0, 0))
    s_spec = pl.BlockSpec((bb, RW_HEADS, RW_N, RW_N), lambda i: (i, 0, 0, 0))
    s_new, o_col = pl.pallas_call(
        _rwkv_step_kernel,
        grid=(n // bb,),
        in_specs=[s_spec, row_spec, row_spec, row_spec, row_spec, row_spec, col_spec],
        out_specs=[s_spec, col_spec],
        out_shape=[
            jax.ShapeDtypeStruct(s0.shape, F32),
            jax.ShapeDtypeStruct((n, RW_HEADS, RW_N, 1), F32),
        ],
        compiler_params=_params(("parallel",)),
    )(s0, rowv(r), rowv(k), rowv(al), rowv(be), rowv(dec), v.reshape(n, RW_HEADS, RW_N, 1))
    return s_new, o_col.reshape(n, RW_WIDTH)


def _gla_features(z, gkw, gkb):
    q = z[:, 0:GLA_QK] * (GLA_DK ** -0.5)
    k = z[:, GLA_QK : 2 * GLA_QK]
    v = z[:, 2 * GLA_QK : 2 * GLA_QK + GLA_WIDTH]
    g = z[:, 2 * GLA_QK + GLA_WIDTH : 2 * GLA_QK + 2 * GLA_WIDTH]
    zgk = z[:, 2 * GLA_QK + 2 * GLA_WIDTH :]
    gk = -_softplus(-(_dot(zgk.astype(BF16), gkw) + gkb)) / GLA_GATE_NORMALIZER
    return q, k, v, g, gk


def _gla_finish(o, g, norm_w):
    outs = []
    for h in range(GLA_HEADS):
        hs = slice(h * GLA_DV, (h + 1) * GLA_DV)
        o_h = o[:, hs]
        o_h = o_h * lax.rsqrt(jnp.mean(o_h * o_h, axis=-1, keepdims=True) + NORM_EPS) * norm_w
        g_h = g[:, hs]
        outs.append(o_h * (g_h * _sigmoid(g_h)))
    return jnp.concatenate(outs, axis=-1)


def _gla_seq_kernel(z_ref, s0_ref, gkw_ref, gkb_ref, nw_ref, wsel_ref, o_ref, sout_ref,
                    st_scr, x_scr, q_scr, k_scr, v_scr, gc_scr, oi_scr):
    C = GLA_CHUNK
    G = 128
    t_idx = pl.program_id(1)
    tb = z_ref.shape[1]
    nc = tb // C

    @pl.when(t_idx == 0)
    def _():
        for h in range(GLA_HEADS):
            st_scr[h] = s0_ref[0, h].T

    q, k, v, g, gk = _gla_features(z_ref[0], gkw_ref[...], gkb_ref[...])
    ri = lax.broadcasted_iota(jnp.int32, (G, G), 0)
    ci = lax.broadcasted_iota(jnp.int32, (G, G), 1)
    cum_mat = ((ri // C == ci // C) & (ri >= ci)).astype(F32)
    for m in range(tb // G):
        rows = slice(m * G, (m + 1) * G)
        gc_scr[rows, :] = _dot(cum_mat, gk[rows, :], precision=HIGHEST)
    q_scr[...] = q
    k_scr[...] = k
    v_scr[...] = v
    gcum = gc_scr[...]

    rg = lax.broadcasted_iota(jnp.int32, (tb, 2 * G), 0)
    cg = lax.broadcasted_iota(jnp.int32, (tb, 2 * G), 1)
    blk_mask = ((cg % G) // C == (rg % G) // C) & (cg % C <= rg % C)
    for p in range(GLA_HEADS // 2):
        ls = slice(p * 128, (p + 1) * 128)
        q3 = q[:, ls].reshape(nc, C, 128)
        k3 = k[:, ls].reshape(nc, C, 128)
        g3 = gcum[:, ls].reshape(nc, C, 128)
        for j in range(C):
            e = q3 * jnp.exp(jnp.minimum(g3 - g3[:, j : j + 1, :], 0.0)) * k3[:, j : j + 1, :]
            x_scr[:, j * 128 : (j + 1) * 128] = e.reshape(tb, 128).astype(BF16)
        a_t = jnp.where(blk_mask, _dot(x_scr[...], wsel_ref[...]), 0.0).astype(BF16)
        for hl in range(2):
            h = 2 * p + hl
            for m in range(tb // G):
                rows = slice(m * G, (m + 1) * G)
                a_blk = a_t[rows, hl * G : (hl + 1) * G]
                oi_scr[rows, h * GLA_DV : (h + 1) * GLA_DV] = _dot(
                    a_blk, v[rows, h * GLA_DV : (h + 1) * GLA_DV].astype(BF16))

    def chunk_body(c, carry):
        sl = pl.ds(pl.multiple_of(c * C, C), C)
        g_c = gc_scr[sl, :]
        q_c = q_scr[sl, :]
        k_c = k_scr[sl, :]
        v_c = v_scr[sl, :]
        for h in range(GLA_HEADS):
            ks = slice(h * GLA_DK, (h + 1) * GLA_DK)
            vs = slice(h * GLA_DV, (h + 1) * GLA_DV)
            g_h = g_c[:, ks]
            g_last = g_h[C - 1 : C, :]
            st = st_scr[h]
            oi_scr[sl, vs] += _dot_nt(q_c[:, ks] * jnp.exp(g_h), st)
            k_dec = k_c[:, ks] * jnp.exp(g_last - g_h)
            st_scr[h] = st * jnp.exp(g_last) + _dot_tn(v_c[:, vs], k_dec)
        return carry

    lax.fori_loop(0, nc, chunk_body, 0)
    o_ref[0] = _gla_finish(oi_scr[...], g, nw_ref[...]).astype(o_ref.dtype)

    @pl.when(t_idx == pl.num_programs(1) - 1)
    def _():
        for h in range(GLA_HEADS):
            sout_ref[0, h] = st_scr[h].T


def _gla_select_matrix():
    j = jnp.arange(GLA_CHUNK)[:, None, None]
    hl = jnp.arange(2)[None, :, None]
    rows_j = jnp.broadcast_to(j, (GLA_CHUNK, 2, GLA_DK)).reshape(-1)
    rows_h = jnp.broadcast_to(hl, (GLA_CHUNK, 2, GLA_DK)).reshape(-1)
    cols = jnp.arange(256)
    sel = (rows_j[:, None] == cols[None, :] % GLA_CHUNK) & (rows_h[:, None] == cols[None, :] // 128)
    return sel.astype(BF16)


def _gla_seq(z3, s0, gl, tb):
    b, l, _ = z3.shape
    const = lambda shape: pl.BlockSpec(shape, lambda i, j: (0,) * len(shape))
    return pl.pallas_call(
        _gla_seq_kernel,
        grid=(b, l // tb),
        in_specs=[
            pl.BlockSpec((1, tb, GLA_PROJ_PAD), lambda i, j: (i, j, 0)),
            pl.BlockSpec((1, GLA_HEADS, GLA_DK, GLA_DV), lambda i, j: (i, 0, 0, 0)),
            const((GLA_LORA_PAD, GLA_QK)), const((1, GLA_QK)), const((1, GLA_DV)),
            const((GLA_CHUNK * 128, 256)),
        ],
        out_specs=[
            pl.BlockSpec((1, tb, GLA_WIDTH), lambda i, j: (i, j, 0)),
            pl.BlockSpec((1, GLA_HEADS, GLA_DK, GLA_DV), lambda i, j: (i, 0, 0, 0)),
        ],
        out_shape=[
            jax.ShapeDtypeStruct((b, l, GLA_WIDTH), BF16),
            jax.ShapeDtypeStruct((b, GLA_HEADS, GLA_DK, GLA_DV), F32),
        ],
        scratch_shapes=[
            pltpu.VMEM((GLA_HEADS, GLA_DV, GLA_DK), F32),
            pltpu.VMEM((tb, GLA_CHUNK * 128), BF16),
            pltpu.VMEM((tb, GLA_QK), F32), pltpu.VMEM((tb, GLA_QK), F32), pltpu.VMEM((tb, GLA_WIDTH), F32),
            pltpu.VMEM((tb, GLA_QK), F32), pltpu.VMEM((tb, GLA_WIDTH), F32),
        ],
        compiler_params=_params(("parallel", "arbitrary")),
    )(z3, s0, gl["gkw"], gl["gkb"], gl["norm_w"], _gla_select_matrix())


def _gla_step_prep_kernel(z_ref, gkw_ref, gkb_ref, q_ref, k_ref, v_ref, g_ref, dec_ref):
    q, k, v, g, gk = _gla_features(z_ref[...], gkw_ref[...], gkb_ref[...])
    q_ref[...] = q
    k_ref[...] = k
    v_ref[...] = v
    g_ref[...] = g
    dec_ref[...] = jnp.exp(gk)


def _gla_step_prep(z, gl):
    n = z.shape[0]
    qk = jax.ShapeDtypeStruct((n, GLA_QK), F32)
    wide = jax.ShapeDtypeStruct((n, GLA_WIDTH), F32)
    return pl.pallas_call(
        _gla_step_prep_kernel,
        out_shape=[qk, qk, wide, wide, qk],
        compiler_params=pltpu.CompilerParams(vmem_limit_bytes=VMEM_LIMIT),
    )(z, gl["gkw"], gl["gkb"])


def _gla_step_kernel(s_ref, qcol_ref, kcol_ref, dcol_ref, vrow_ref, snew_ref, orow_ref):
    s_new = s_ref[...] * dcol_ref[...] + kcol_ref[...] * vrow_ref[...]
    snew_ref[...] = s_new
    orow_ref[...] = jnp.sum(s_new * qcol_ref[...], axis=2, keepdims=True)


def _gla_step(s0, q, k, dec, v, bb):
    n = s0.shape[0]
    colv = lambda t: t.reshape(n, GLA_HEADS, GLA_DK, 1)
    col_spec = pl.BlockSpec((bb, GLA_HEADS, GLA_DK, 1), lambda i: (i, 0, 0, 0))
    row_spec = pl.BlockSpec((bb, GLA_HEADS, 1, GLA_DV), lambda i: (i, 0, 0, 0))
    s_spec = pl.BlockSpec((bb, GLA_HEADS, GLA_DK, GLA_DV), lambda i: (i, 0, 0, 0))
    s_new, o_row = pl.pallas_call(
        _gla_step_kernel,
        grid=(n // bb,),
        in_specs=[s_spec, col_spec, col_spec, col_spec, row_spec],
        out_specs=[s_spec, row_spec],
        out_shape=[
            jax.ShapeDtypeStruct(s0.shape, F32),
            jax.ShapeDtypeStruct((n, GLA_HEADS, 1, GLA_DV), F32),
        ],
        compiler_params=_params(("parallel",)),
    )(s0, colv(q), colv(k), colv(dec), v.reshape(n, GLA_HEADS, 1, GLA_DV))
    return s_new, o_row.reshape(n, GLA_WIDTH)


def _step_post_kernel(orw_ref, bon_ref, grw_ref, lnw_ref, lnb_ref, ogl_ref, ggl_ref, nw_ref, o_ref):
    o = orw_ref[...]
    for h in range(RW_HEADS):
        hs = slice(h * RW_N, (h + 1) * RW_N)
        o_h = o[:, hs]
        mean = jnp.mean(o_h, axis=-1, keepdims=True)
        cen = o_h - mean
        var = jnp.mean(cen * cen, axis=-1, keepdims=True)
        on = cen * lax.rsqrt(var + RW_GN_EPS)
        res = (on * lnw_ref[:, hs] + lnb_ref[:, hs] + bon_ref[:, hs]) * grw_ref[:, hs]
        o_ref[:, hs] = res.astype(o_ref.dtype)
    o_ref[:, RW_WIDTH:] = _gla_finish(ogl_ref[...], ggl_ref[...], nw_ref[...]).astype(o_ref.dtype)


def _step_post(o_rw, bonus, g_rw, rw, o_gl, g_gl, gl):
    n = o_rw.shape[0]
    return pl.pallas_call(
        _step_post_kernel,
        out_shape=jax.ShapeDtypeStruct((n, RW_WIDTH + GLA_WIDTH), BF16),
        compiler_params=pltpu.CompilerParams(vmem_limit_bytes=VMEM_LIMIT),
    )(o_rw, bonus, g_rw, rw["ln_w"], rw["ln_b"], o_gl, g_gl, gl["norm_w"])


def _outproj_router_kernel(x_ref, orw_ref, ogl_ref, wo_ref, gain_ref, wrt_ref, br_ref,
                           h_ref, xn_ref, idx_ref, gate_ref, rank_ref, cnt_ref):
    tm = x_ref.shape[0]
    mix = jnp.concatenate([orw_ref[...], ogl_ref[...]], axis=-1)
    h = x_ref[...] + _dot(mix, wo_ref[...])
    h_ref[...] = h
    xn = h * lax.rsqrt(jnp.mean(h * h, axis=-1, keepdims=True) + NORM_EPS) * gain_ref[...]
    xn_ref[...] = xn.astype(xn_ref.dtype)
    logits = _dot_nt(wrt_ref[...], xn, precision=HIGHEST) + br_ref[...]
    eidx = lax.broadcasted_iota(jnp.int32, logits.shape, 0)
    ti = lax.broadcasted_iota(jnp.int32, (tm, tm), 0)
    tj = lax.broadcasted_iota(jnp.int32, (tm, tm), 1)
    before = (ti < tj).astype(BF16)
    vals, idxs = [], []
    work = logits
    chosen = jnp.zeros(logits.shape, F32)
    for _ in range(TOP_K):
        m = jnp.max(work, axis=0, keepdims=True)
        sel = jnp.min(jnp.where(work == m, eidx, N_EXPERTS), axis=0, keepdims=True)
        hit = eidx == sel
        work = jnp.where(hit, -jnp.inf, work)
        chosen = chosen + hit.astype(F32)
        vals.append(m)
        idxs.append(sel)
    prefix = _dot(chosen.astype(BF16), before)
    exps = [jnp.exp(v - vals[0]) for v in vals]
    denom = exps[0] + exps[1] + exps[2] + exps[3]
    for j in range(TOP_K):
        idx_ref[j : j + 1, :] = idxs[j]
        gate_ref[j : j + 1, :] = exps[j] / denom
        rank = jnp.sum(jnp.where(eidx == idxs[j], prefix, 0.0), axis=0, keepdims=True)
        rank_ref[j : j + 1, :] = rank.astype(jnp.int32)
    cnt = jnp.sum(chosen, axis=1, keepdims=True)
    cnt_ref[0] = jnp.broadcast_to(cnt, (N_EXPERTS, 128)).astype(jnp.int32)


def _outproj_router(x, o_rw, o_gl, w_out, gain, w_router_t, b_router, tm):
    n = x.shape[0]
    nt = n // tm
    const = lambda shape: pl.BlockSpec(shape, lambda i: (0,) * len(shape))
    tok = lambda width: pl.BlockSpec((tm, width), lambda i: (i, 0))
    lane = pl.BlockSpec((TOP_K, tm), lambda i: (0, i))
    return pl.pallas_call(
        _outproj_router_kernel,
        grid=(nt,),
        in_specs=[
            tok(D_MODEL), tok(RW_WIDTH), tok(GLA_WIDTH),
            const((D_MODEL, D_MODEL)), const((1, D_MODEL)), const((N_EXPERTS, D_MODEL)), const((N_EXPERTS, 1)),
        ],
        out_specs=[tok(D_MODEL), tok(D_MODEL), lane, lane, lane,
                   pl.BlockSpec((1, N_EXPERTS, 128), lambda i: (i, 0, 0))],
        out_shape=[
            jax.ShapeDtypeStruct((n, D_MODEL), F32),
            jax.ShapeDtypeStruct((n, D_MODEL), BF16),
            jax.ShapeDtypeStruct((TOP_K, n), jnp.int32),
            jax.ShapeDtypeStruct((TOP_K, n), F32),
            jax.ShapeDtypeStruct((TOP_K, n), jnp.int32),
            jax.ShapeDtypeStruct((nt, N_EXPERTS, 128), jnp.int32),
        ],
        compiler_params=_params(("parallel",)),
    )(x, o_rw, o_gl, w_out, gain, w_router_t, b_router)


def _moe_kernel(be_ref, nu_ref, xs_ref, wg_ref, wu_ref, wd_ref, bg_ref, bu_ref, bd_ref, y_ref, wg_b, wu_b, wd_b):
    b = pl.program_id(0)
    prev = be_ref[jnp.maximum(b - 1, 0)]
    new_expert = (b == 0) | (be_ref[b] != prev)

    @pl.when(new_expert)
    def _():
        wg_b[...] = wg_ref[0].astype(BF16)
        wu_b[...] = wu_ref[0].astype(BF16)
        wd_b[...] = wd_ref[0].astype(BF16)

    @pl.when(b < nu_ref[0])
    def _():
        x = xs_ref[...]
        half = D_MODEL // 2
        acc = None
        for f in range(2):
            fs = slice(f * half, (f + 1) * half)
            gt = _dot(x, wg_b[:, fs]) + bg_ref[0, :, fs]
            up = _dot(x, wu_b[:, fs]) + bu_ref[0, :, fs]
            gt = jnp.minimum(gt, SWIGLU_LIMIT)
            up = jnp.clip(up, -SWIGLU_LIMIT, SWIGLU_LIMIT)
            hid = (up + 1.0) * gt * _sigmoid(SWIGLU_ALPHA * gt)
            part = _dot(hid.astype(BF16), wd_b[fs, :])
            acc = part if acc is None else acc + part
        y_ref[...] = (acc + bd_ref[0]).astype(y_ref.dtype)


def _moe_ffn(block_expert, n_used, xs, w_gate, w_up, w_down, b_gate, b_up, b_down):
    n_blocks = block_expert.shape[0]
    row = lambda b, be, nu: (jnp.minimum(b, nu[0] - 1), 0)
    wsel = lambda b, be, nu: (be[b], 0, 0)
    wspec = pl.BlockSpec((1, D_MODEL, D_MODEL), wsel)
    bspec = pl.BlockSpec((1, 1, D_MODEL), wsel)
    grid_spec = pltpu.PrefetchScalarGridSpec(
        num_scalar_prefetch=2,
        grid=(n_blocks,),
        in_specs=[pl.BlockSpec((MOE_BLOCK, D_MODEL), row), wspec, wspec, wspec, bspec, bspec, bspec],
        out_specs=pl.BlockSpec((MOE_BLOCK, D_MODEL), row),
        scratch_shapes=[pltpu.VMEM((D_MODEL, D_MODEL), BF16)] * 3,
    )
    return pl.pallas_call(
        _moe_kernel,
        grid_spec=grid_spec,
        out_shape=jax.ShapeDtypeStruct((n_blocks * MOE_BLOCK, D_MODEL), F32),
        compiler_params=_params(("arbitrary",)),
    )(block_expert, n_used, xs, w_gate, w_up, w_down,
      b_gate.reshape(N_EXPERTS, 1, D_MODEL), b_up.reshape(N_EXPERTS, 1, D_MODEL),
      b_down.reshape(N_EXPERTS, 1, D_MODEL))


def _combine_kernel(h_ref, y_ref, gate_ref, gain_ref, o_ref):
    f = h_ref[...]
    gates = gate_ref[...]
    for j in range(TOP_K):
        f = f + gates[:, j : j + 1] * y_ref[j]
    o_ref[...] = f * lax.rsqrt(jnp.mean(f * f, axis=-1, keepdims=True) + NORM_EPS) * gain_ref[...]


def _combine(h, y4, gates, gain, tm):
    n = h.shape[0]
    return pl.pallas_call(
        _combine_kernel,
        grid=(n // tm,),
        in_specs=[
            pl.BlockSpec((tm, D_MODEL), lambda i: (i, 0)),
            pl.BlockSpec((TOP_K, tm, D_MODEL), lambda i: (0, i, 0)),
            pl.BlockSpec((tm, TOP_K), lambda i: (i, 0)),
            pl.BlockSpec((1, D_MODEL), lambda i: (0, 0)),
        ],
        out_specs=pl.BlockSpec((tm, D_MODEL), lambda i: (i, 0)),
        out_shape=jax.ShapeDtypeStruct((n, D_MODEL), F32),
        compiler_params=_params(("parallel",)),
    )(h, y4, gates, gain)


def _pad_rows(w, rows, offset):
    out = jnp.zeros((rows, w.shape[1]), w.dtype)
    return out.at[offset : offset + w.shape[0]].set(w)


def _routing_tables(idx_t, rank_t, counts, tm):
    n = idx_t.shape[1]
    n_blocks = (n * TOP_K + N_EXPERTS * (MOE_BLOCK - 1) + MOE_BLOCK - 1) // MOE_BLOCK
    total = jnp.sum(counts, axis=0)
    padded = (total + MOE_BLOCK - 1) // MOE_BLOCK * MOE_BLOCK
    pends = jnp.cumsum(padded)
    pstarts = pends - padded
    tile_base = pstarts[None, :] + jnp.cumsum(counts, axis=0) - counts
    tile = jnp.arange(n, dtype=jnp.int32) // tm
    dest_t = tile_base[tile[None, :], idx_t] + rank_t
    blocks = jnp.arange(n_blocks, dtype=jnp.int32) * MOE_BLOCK
    n_used = (pends[-1] // MOE_BLOCK).astype(jnp.int32)
    owner = jnp.sum((pends[None, :] <= blocks[:, None]).astype(jnp.int32), axis=1)
    block_expert = jnp.minimum(owner, N_EXPERTS - 1)
    last = block_expert[jnp.maximum(n_used - 1, 0)]
    block_expert = jnp.where(jnp.arange(n_blocks) < n_used, block_expert, last)
    return dest_t, block_expert, n_used.reshape(1), n_blocks


def kernel(x_prompt, x_sample, state_rwkv_shift, state_rwkv_wkv, state_gla, norm_mix, w_in, rw_mu, rw_w0, rw_w2, rw_a0, rw_a2, rw_g2, rw_k_k, rw_k_a, rw_r_k, rw_ln_w, rw_ln_b, gla_gk_w2, gla_gk_b, gla_norm_w, w_out, norm_ffn, w_router, b_router, w_gate, b_gate, w_up, b_up, w_down, b_down, norm_final):
    depth = norm_mix.shape[0]
    assert depth == 1
    bp, lp, d = x_prompt.shape
    bs, ls, _ = x_sample.shape
    assert ls == 1 and lp % SEQ_BLOCK == 0
    l = 0
    row = lambda t: t.reshape(1, -1)

    w_in_b = w_in[l].astype(BF16)
    w_in_r = w_in_b[:, :RW_PROJ]
    w_in_g = jnp.pad(w_in_b[:, RW_PROJ:], ((0, 0), (0, GLA_PROJ_PAD - GLA_PROJ)))
    rw = dict(
        mu=row(rw_mu[l]), w0=row(rw_w0[l]), a0=row(rw_a0[l]),
        w2p=_pad_rows(rw_w2[l].astype(BF16), 128, 0), a2p=_pad_rows(rw_a2[l].astype(BF16), 128, 64),
        g2=rw_g2[l].astype(BF16), k_k=row(rw_k_k[l]), k_a=row(rw_k_a[l]), r_k=row(rw_r_k[l]),
        ln_w=row(rw_ln_w[l]), ln_b=row(rw_ln_b[l]))
    gl = dict(gkw=_pad_rows(gla_gk_w2[l].astype(BF16), GLA_LORA_PAD, 0), gkb=row(gla_gk_b[l]),
              norm_w=row(gla_norm_w[l]))
    gain_mix = row(norm_mix[l])

    n_p = bp * lp
    xp = x_prompt.reshape(n_p, d)
    zr_p, zg_p = _inproj(xp, gain_mix, w_in_r, w_in_g, TOK_BLOCK)
    zr_p3 = zr_p.reshape(bp, lp, RW_PROJ)
    o_rw_p, wkv_p = _rwkv_seq(zr_p3, jnp.zeros((bp, 1, RW_PROJ), F32),
                              jnp.zeros((bp, RW_HEADS, RW_N, RW_N), F32), rw, SEQ_BLOCK)
    o_gl_p, gla_p = _gla_seq(zg_p.reshape(bp, lp, GLA_PROJ_PAD),
                             jnp.zeros((bp, GLA_HEADS, GLA_DK, GLA_DV), F32), gl, SEQ_BLOCK)
    shift_p = zr_p3[:, -1, :]

    xs_ = x_sample.reshape(bs, d)
    zr_s, zg_s = _inproj(xs_, gain_mix, w_in_r, w_in_g, bs)
    r, k, v, al, be, dec, g_rw, bonus = _rwkv_step_prep(zr_s, state_rwkv_shift[l], rw)
    wkv_s, o_rw_s = _rwkv_step(state_rwkv_wkv[l], r, k, al, be, dec, v, 16)
    q, kg, vg, g_gl, dec_g = _gla_step_prep(zg_s, gl)
    gla_s, o_gl_s = _gla_step(state_gla[l], q, kg, dec_g, vg, 16)
    o_mix_s = _step_post(o_rw_s, bonus, g_rw, rw, o_gl_s, g_gl, gl)
    shift_s = zr_s

    n = n_p + bs
    x_all = jnp.concatenate([xp, xs_], axis=0)
    o_rw_all = jnp.concatenate([o_rw_p.reshape(n_p, RW_WIDTH), o_mix_s[:, :RW_WIDTH]], axis=0)
    o_gl_all = jnp.concatenate([o_gl_p.reshape(n_p, GLA_WIDTH), o_mix_s[:, RW_WIDTH:]], axis=0)
    tm = bs
    h, xn, idx_t, gate_t, rank_t, cnt = _outproj_router(
        x_all, o_rw_all, o_gl_all, w_out[l].astype(BF16), row(norm_ffn[l]), w_router[l].T,
        b_router[l].reshape(N_EXPERTS, 1), tm)
    dest_t, block_expert, n_used, n_blocks = _routing_tables(idx_t, rank_t, cnt[:, :, 0], tm)
    slot_tok = jnp.zeros((n_blocks * MOE_BLOCK,), jnp.int32).at[dest_t.reshape(-1)].set(
        jnp.tile(jnp.arange(n, dtype=jnp.int32), TOP_K))
    xs_rows = xn[slot_tok]
    y_rows = _moe_ffn(block_expert, n_used, xs_rows, w_gate[l], w_up[l], w_down[l], b_gate[l], b_up[l], b_down[l])
    y4 = y_rows[dest_t]
    y_all = _combine(h, y4, gate_t.T, row(norm_final), tm)

    y_prompt = y_all[:n_p].reshape(bp, lp, d)
    y_sample = y_all[n_p:].reshape(bs, ls, d)
    return (y_prompt, y_sample, shift_p[None], wkv_p[None], gla_p[None], shift_s[None], wkv_s[None], gla_s[None])
```

```python
import functools

import jax
import jax.numpy as jnp
from jax import lax
from jax.experimental import pallas as pl
from jax.experimental.pallas import tpu as pltpu

F32 = jnp.float32
BF16 = jnp.bfloat16
HIGHEST = lax.Precision.HIGHEST

D_MODEL = 1024
RW_WIDTH = 512
RW_HEADS = 8
RW_N = 64
RW_PROJ = 1792
RW_GN_EPS = 64e-5
GLA_HEADS = 4
GLA_DK = 64
GLA_DV = 128
GLA_WIDTH = 512
GLA_QK = GLA_HEADS * GLA_DK
GLA_PROJ = 1552
GLA_PROJ_PAD = 1664
GLA_LORA_PAD = 128
GLA_GATE_NORMALIZER = 16.0
N_EXPERTS = 32
TOP_K = 4
SWIGLU_LIMIT = 7.0
SWIGLU_ALPHA = 1.702
NORM_EPS = 1e-5

RW_CHUNK = 64
GLA_CHUNK = 16
SEQ_BLOCK = 512
TOK_BLOCK = 512
MOE_BLOCK = 512
VMEM_LIMIT = 56 * 1024 * 1024


def _dot(a, b, precision=None):
    return jnp.dot(a, b, preferred_element_type=F32, precision=precision)


def _dot_nt(a, b, precision=None):
    return lax.dot_general(a, b, (((1,), (1,)), ((), ())), preferred_element_type=F32, precision=precision)


def _dot_tn(a, b, precision=None):
    return lax.dot_general(a, b, (((0,), (0,)), ((), ())), preferred_element_type=F32, precision=precision)


def _sigmoid(x):
    return 1.0 / (1.0 + jnp.exp(-x))


def _softplus(x):
    return jnp.maximum(x, 0.0) + jnp.log(1.0 + jnp.exp(-jnp.abs(x)))


def _params(sem):
    return pltpu.CompilerParams(dimension_semantics=sem, vmem_limit_bytes=VMEM_LIMIT)


def _inproj_kernel(x_ref, gain_ref, wr_ref, wg_ref, zr_ref, zg_ref):
    x = x_ref[...]
    xn = x * lax.rsqrt(jnp.mean(x * x, axis=-1, keepdims=True) + NORM_EPS) * gain_ref[...]
    xb = xn.astype(BF16)
    zr_ref[...] = _dot(xb, wr_ref[...])
    zg_ref[...] = _dot(xb, wg_ref[...])


def _inproj(x, gain, w_r, w_g, tm):
    n = x.shape[0]
    return pl.pallas_call(
        _inproj_kernel,
        grid=(n // tm,),
        in_specs=[
            pl.BlockSpec((tm, D_MODEL), lambda i: (i, 0)),
            pl.BlockSpec((1, D_MODEL), lambda i: (0, 0)),
            pl.BlockSpec((D_MODEL, RW_PROJ), lambda i: (0, 0)),
            pl.BlockSpec((D_MODEL, GLA_PROJ_PAD), lambda i: (0, 0)),
        ],
        out_specs=[
            pl.BlockSpec((tm, RW_PROJ), lambda i: (i, 0)),
            pl.BlockSpec((tm, GLA_PROJ_PAD), lambda i: (i, 0)),
        ],
        out_shape=[
            jax.ShapeDtypeStruct((n, RW_PROJ), F32),
            jax.ShapeDtypeStruct((n, GLA_PROJ_PAD), F32),
        ],
        compiler_params=_params(("parallel",)),
    )(x, gain, w_r, w_g)


def _rwkv_features(zs, w0, w2p, a0, a2p, g2, k_k, k_a):
    W = RW_WIDTH
    r = zs[:, 0:W]
    k_raw = zs[:, W : 2 * W]
    v = zs[:, 2 * W : 3 * W]
    zwa = zs[:, 3 * W : 3 * W + 128]
    zg = zs[:, 3 * W + 128 :]
    w = -_softplus(-(w0 + _dot(jnp.tanh(zwa).astype(BF16), w2p))) - 0.5
    log_decay = -jnp.exp(w)
    a = _sigmoid(a0 + _dot(zwa.astype(BF16), a2p))
    g = _dot(_sigmoid(zg).astype(BF16), g2)
    kk_raw = k_raw * k_k
    k = k_raw * (1.0 + (a - 1.0) * k_a)
    return r, k, v, kk_raw, a, log_decay, g


def _level_mask(ri, ci, lvl):
    same = (ri >> (lvl + 1)) == (ci >> (lvl + 1))
    return same & (((ri >> lvl) & 1) == 1) & (((ci >> lvl) & 1) == 0)


def _rwkv_seq_kernel(z_ref, shift0_ref, s0_ref, mu_ref, w0_ref, w2_ref, a0_ref, a2_ref, g2_ref, kk_ref, ka_ref,
                     rk_ref, lnw_ref, lnb_ref, o_ref, sout_ref,
                     m_scr, prev_scr, r_scr, k_scr, v_scr, kkr_scr, a_scr, lw_scr, on_scr, bon_scr):
    C = RW_CHUNK
    t_idx = pl.program_id(1)
    tb = z_ref.shape[1]

    @pl.when(t_idx == 0)
    def _():
        prev_scr[...] = shift0_ref[0]
        for h in range(RW_HEADS):
            m_scr[h] = s0_ref[0, h].T

    z = z_ref[0]
    row = lax.broadcasted_iota(jnp.int32, z.shape, 0)
    z_prev = jnp.where(row == 0, prev_scr[...], pltpu.roll(z, 1, axis=0))
    prev_scr[...] = z[tb - 1 : tb, :]
    zs = z + mu_ref[...] * (z_prev - z)
    r, k, v, kk_raw, a, log_decay, g = _rwkv_features(
        zs, w0_ref[...], w2_ref[...], a0_ref[...], a2_ref[...], g2_ref[...], kk_ref[...], ka_ref[...])
    r_scr[...] = r
    k_scr[...] = k
    v_scr[...] = v
    kkr_scr[...] = kk_raw
    a_scr[...] = a
    lw_scr[...] = log_decay

    ri = lax.broadcasted_iota(jnp.int32, (C, C), 0)
    ci = lax.broadcasted_iota(jnp.int32, (C, C), 1)
    tril = ri >= ci
    stril = ri > ci
    eye = ri == ci
    tril_f = tril.astype(F32)
    eye_f = eye.astype(F32)
    rk_all = rk_ref[...]

    def chunk_body(c, carry):
        sl = pl.ds(pl.multiple_of(c * C, C), C)
        lw = lw_scr[sl, :]
        cum = _dot(tril_f, lw, precision=HIGHEST)
        cum_last = cum[C - 1 : C, :]
        e_incl = jnp.exp(cum)
        e_excl = jnp.exp(cum - lw)
        e_neg = jnp.exp(-cum)
        e_tail = jnp.exp(cum_last - cum)
        p_last = jnp.exp(cum_last)
        r_c, k_c, v_c, kkr_c, a_c = r_scr[sl, :], k_scr[sl, :], v_scr[sl, :], kkr_scr[sl, :], a_scr[sl, :]
        heads = range(RW_HEADS)
        hsl = [slice(h * RW_N, (h + 1) * RW_N) for h in heads]
        r_h = [r_c[:, s] for s in hsl]
        k_h = [k_c[:, s] for s in hsl]
        v_h = [v_c[:, s] for s in hsl]
        al, be = [], []
        for h in heads:
            kk_h = kkr_c[:, hsl[h]]
            nrm = jnp.sqrt(jnp.sum(kk_h * kk_h, axis=-1, keepdims=True))
            al.append(kk_h / jnp.maximum(nrm, 1e-12))
            be.append(al[h] * a_c[:, hsl[h]])
        al_t = [al[h] * e_excl[:, hsl[h]] for h in heads]
        r_t = [r_h[h] * e_incl[:, hsl[h]] for h in heads]
        be_n = [be[h] * e_neg[:, hsl[h]] for h in heads]
        k_n = [k_h[h] * e_neg[:, hsl[h]] for h in heads]
        k_et = [(k_h[h] * e_tail[:, hsl[h]]).T for h in heads]
        be_et = [(be[h] * e_tail[:, hsl[h]]).T for h in heads]
        lhs = [jnp.concatenate([al_t[h], r_t[h]], axis=0) for h in heads]
        s_b = [_dot_nt(lhs[h], be_n[h]) for h in heads]
        s_k = [_dot_nt(lhs[h], k_n[h]) for h in heads]
        l_ab = [jnp.where(stril, s_b[h][:C], 0.0) for h in heads]
        a_rb = [jnp.where(tril, s_b[h][C:], 0.0) for h in heads]
        l_ak = [jnp.where(stril, s_k[h][:C], 0.0) for h in heads]
        a_rk = [jnp.where(tril, s_k[h][C:], 0.0) for h in heads]
        lakv = [_dot(l_ak[h], v_h[h]) for h in heads]
        arkv = [_dot(a_rk[h], v_h[h]) for h in heads]
        kev = [_dot(k_et[h], v_h[h]) for h in heads]
        t_inv = [eye_f - jnp.where(_level_mask(ri, ci, 0), l_ab[h], 0.0) for h in heads]
        lvl = 1
        while (1 << lvl) < C:
            lm = _level_mask(ri, ci, lvl)
            tn = [_dot(t_inv[h], jnp.where(lm, l_ab[h], 0.0)) for h in heads]
            t_inv = [t_inv[h] - _dot(tn[h], t_inv[h]) for h in heads]
            lvl += 1
        a_til = [_dot(t_inv[h], al_t[h]) for h in heads]
        b_til = [_dot(t_inv[h], lakv[h]) for h in heads]
        r_hat = [r_t[h] - _dot(a_rb[h], a_til[h]) for h in heads]
        o_hat = [arkv[h] - _dot(a_rb[h], b_til[h]) for h in heads]
        g_mat = [jnp.where(eye, p_last[:, hsl[h]], 0.0) - _dot(be_et[h], a_til[h]) for h in heads]
        h_mat = [kev[h] - _dot(be_et[h], b_til[h]) for h in heads]
        res = [_dot(jnp.concatenate([r_hat[h], g_mat[h]], axis=0), m_scr[h]) for h in heads]
        for h in heads:
            m_scr[h] = res[h][C:] + h_mat[h]
        for h in heads:
            o_h = res[h][:C] + o_hat[h]
            mean = jnp.mean(o_h, axis=-1, keepdims=True)
            cen = o_h - mean
            var = jnp.mean(cen * cen, axis=-1, keepdims=True)
            on_scr[sl, hsl[h]] = cen * lax.rsqrt(var + RW_GN_EPS)
            bon_scr[sl, hsl[h]] = jnp.sum(r_h[h] * k_h[h] * rk_all[:, hsl[h]], axis=-1, keepdims=True) * v_h[h]
        return carry

    lax.fori_loop(0, tb // C, chunk_body, 0)
    out = (on_scr[...] * lnw_ref[...] + lnb_ref[...] + bon_scr[...]) * g
    o_ref[0] = out.astype(o_ref.dtype)

    @pl.when(t_idx == pl.num_programs(1) - 1)
    def _():
        for h in range(RW_HEADS):
            sout_ref[0, h] = m_scr[h].T


def _rwkv_seq(z3, shift0, s0, rw, tb):
    b, l, _ = z3.shape
    const = lambda shape: pl.BlockSpec(shape, lambda i, j: (0,) * len(shape))
    wide = lambda: pltpu.VMEM((tb, RW_WIDTH), F32)
    return pl.pallas_call(
        _rwkv_seq_kernel,
        grid=(b, l // tb),
        in_specs=[
            pl.BlockSpec((1, tb, RW_PROJ), lambda i, j: (i, j, 0)),
            pl.BlockSpec((1, 1, RW_PROJ), lambda i, j: (i, 0, 0)),
            pl.BlockSpec((1, RW_HEADS, RW_N, RW_N), lambda i, j: (i, 0, 0, 0)),
            const((1, RW_PROJ)),
            const((1, RW_WIDTH)), const((128, RW_WIDTH)),
            const((1, RW_WIDTH)), const((128, RW_WIDTH)),
            const((128, RW_WIDTH)),
            const((1, RW_WIDTH)), const((1, RW_WIDTH)), const((1, RW_WIDTH)),
            const((1, RW_WIDTH)), const((1, RW_WIDTH)),
        ],
        out_specs=[
            pl.BlockSpec((1, tb, RW_WIDTH), lambda i, j: (i, j, 0)),
            pl.BlockSpec((1, RW_HEADS, RW_N, RW_N), lambda i, j: (i, 0, 0, 0)),
        ],
        out_shape=[
            jax.ShapeDtypeStruct((b, l, RW_WIDTH), BF16),
            jax.ShapeDtypeStruct((b, RW_HEADS, RW_N, RW_N), F32),
        ],
        scratch_shapes=[
            pltpu.VMEM((RW_HEADS, RW_N, RW_N), F32),
            pltpu.VMEM((1, RW_PROJ), F32),
            wide(), wide(), wide(), wide(), wide(), wide(), wide(), wide(),
        ],
        compiler_params=_params(("parallel", "arbitrary")),
    )(z3, shift0, s0, rw["mu"], rw["w0"], rw["w2p"], rw["a0"], rw["a2p"], rw["g2"], rw["k_k"], rw["k_a"],
      rw["r_k"], rw["ln_w"], rw["ln_b"])


def _rwkv_step_prep_kernel(z_ref, shift0_ref, mu_ref, w0_ref, w2_ref, a0_ref, a2_ref, g2_ref, kk_ref, ka_ref,
                           rk_ref, r_ref, k_ref, v_ref, al_ref, be_ref, dec_ref, g_ref, bon_ref):
    z = z_ref[...]
    zs = z + mu_ref[...] * (shift0_ref[...] - z)
    r, k, v, kk_raw, a, log_decay, g = _rwkv_features(
        zs, w0_ref[...], w2_ref[...], a0_ref[...], a2_ref[...], g2_ref[...], kk_ref[...], ka_ref[...])
    rk_all = rk_ref[...]
    for h in range(RW_HEADS):
        hs = slice(h * RW_N, (h + 1) * RW_N)
        kk_h = kk_raw[:, hs]
        nrm = jnp.sqrt(jnp.sum(kk_h * kk_h, axis=-1, keepdims=True))
        al = kk_h / jnp.maximum(nrm, 1e-12)
        al_ref[:, hs] = al
        be_ref[:, hs] = al * a[:, hs]
        bon_ref[:, hs] = jnp.sum(r[:, hs] * k[:, hs] * rk_all[:, hs], axis=-1, keepdims=True) * v[:, hs]
    r_ref[...] = r
    k_ref[...] = k
    v_ref[...] = v
    dec_ref[...] = jnp.exp(log_decay)
    g_ref[...] = g


def _rwkv_step_prep(z, shift0, rw):
    n = z.shape[0]
    out = jax.ShapeDtypeStruct((n, RW_WIDTH), F32)
    return pl.pallas_call(
        _rwkv_step_prep_kernel,
        out_shape=[out] * 8,
        compiler_params=pltpu.CompilerParams(vmem_limit_bytes=VMEM_LIMIT),
    )(z, shift0, rw["mu"], rw["w0"], rw["w2p"], rw["a0"], rw["a2p"], rw["g2"], rw["k_k"], rw["k_a"], rw["r_k"])


def _rwkv_step_kernel(s_ref, r_ref, k_ref, al_ref, be_ref, dec_ref, vcol_ref, snew_ref, ocol_ref):
    s = s_ref[...]
    sa = -jnp.sum(s * al_ref[...], axis=-1, keepdims=True)
    s_new = s * dec_ref[...] + sa * be_ref[...] + vcol_ref[...] * k_ref[...]
    snew_ref[...] = s_new
    ocol_ref[...] = jnp.sum(s_new * r_ref[...], axis=-1, keepdims=True)


def _rwkv_step(s0, r, k, al, be, dec, v, bb):
    n = s0.shape[0]
    rowv = lambda t: t.reshape(n, RW_HEADS, 1, RW_N)
    row_spec = pl.BlockSpec((bb, RW_HEADS, 1, RW_N), lambda i: (i, 0, 0, 0))
    col_spec = pl.BlockSpec((bb, RW_HEADS, RW_N, 1), lambda i: (i, 0, 0, 0))
    s_spec = pl.BlockSpec((bb, RW_HEADS, RW_N, RW_N), lambda i: (i, 0, 0, 0))
    s_new, o_col = pl.pallas_call(
        _rwkv_step_kernel,
        grid=(n // bb,),
        in_specs=[s_spec, row_spec, row_spec, row_spec, row_spec, row_spec, col_spec],
        out_specs=[s_spec, col_spec],
        out_shape=[
            jax.ShapeDtypeStruct(s0.shape, F32),
            jax.ShapeDtypeStruct((n, RW_HEADS, RW_N, 1), F32),
        ],
        compiler_params=_params(("parallel",)),
    )(s0, rowv(r), rowv(k), rowv(al), rowv(be), rowv(dec), v.reshape(n, RW_HEADS, RW_N, 1))
    return s_new, o_col.reshape(n, RW_WIDTH)


def _gla_features(z, gkw, gkb):
    q = z[:, 0:GLA_QK] * (GLA_DK ** -0.5)
    k = z[:, GLA_QK : 2 * GLA_QK]
    v = z[:, 2 * GLA_QK : 2 * GLA_QK + GLA_WIDTH]
    g = z[:, 2 * GLA_QK + GLA_WIDTH : 2 * GLA_QK + 2 * GLA_WIDTH]
    zgk = z[:, 2 * GLA_QK + 2 * GLA_WIDTH :]
    gk = -_softplus(-(_dot(zgk.astype(BF16), gkw) + gkb)) / GLA_GATE_NORMALIZER
    return q, k, v, g, gk


def _gla_finish(o, g, norm_w):
    outs = []
    for h in range(GLA_HEADS):
        hs = slice(h * GLA_DV, (h + 1) * GLA_DV)
        o_h = o[:, hs]
        o_h = o_h * lax.rsqrt(jnp.mean(o_h * o_h, axis=-1, keepdims=True) + NORM_EPS) * norm_w
        g_h = g[:, hs]
        outs.append(o_h * (g_h * _sigmoid(g_h)))
    return jnp.concatenate(outs, axis=-1)


def _gla_seq_kernel(z_ref, s0_ref, gkw_ref, gkb_ref, nw_ref, wsel_ref, o_ref, sout_ref,
                    st_scr, x_scr, q_scr, k_scr, v_scr, gc_scr, oi_scr):
    C = GLA_CHUNK
    G = 128
    t_idx = pl.program_id(1)
    tb = z_ref.shape[1]
    nc = tb // C

    @pl.when(t_idx == 0)
    def _():
        for h in range(GLA_HEADS):
            st_scr[h] = s0_ref[0, h].T

    q, k, v, g, gk = _gla_features(z_ref[0], gkw_ref[...], gkb_ref[...])
    ri = lax.broadcasted_iota(jnp.int32, (G, G), 0)
    ci = lax.broadcasted_iota(jnp.int32, (G, G), 1)
    cum_mat = ((ri // C == ci // C) & (ri >= ci)).astype(F32)
    for m in range(tb // G):
        rows = slice(m * G, (m + 1) * G)
        gc_scr[rows, :] = _dot(cum_mat, gk[rows, :], precision=HIGHEST)
    q_scr[...] = q
    k_scr[...] = k
    v_scr[...] = v
    gcum = gc_scr[...]

    rg = lax.broadcasted_iota(jnp.int32, (tb, 2 * G), 0)
    cg = lax.broadcasted_iota(jnp.int32, (tb, 2 * G), 1)
    blk_mask = ((cg % G) // C == (rg % G) // C) & (cg % C <= rg % C)
    for p in range(GLA_HEADS // 2):
        ls = slice(p * 128, (p + 1) * 128)
        q3 = q[:, ls].reshape(nc, C, 128)
        k3 = k[:, ls].reshape(nc, C, 128)
        g3 = gcum[:, ls].reshape(nc, C, 128)
        for j in range(C):
            e = q3 * jnp.exp(jnp.minimum(g3 - g3[:, j : j + 1, :], 0.0)) * k3[:, j : j + 1, :]
            x_scr[:, j * 128 : (j + 1) * 128] = e.reshape(tb, 128).astype(BF16)
        a_t = jnp.where(blk_mask, _dot(x_scr[...], wsel_ref[...]), 0.0).astype(BF16)
        for hl in range(2):
            h = 2 * p + hl
            for m in range(tb // G):
                rows = slice(m * G, (m + 1) * G)
                a_blk = a_t[rows, hl * G : (hl + 1) * G]
                oi_scr[rows, h * GLA_DV : (h + 1) * GLA_DV] = _dot(
                    a_blk, v[rows, h * GLA_DV : (h + 1) * GLA_DV].astype(BF16))

    def chunk_body(c, carry):
        sl = pl.ds(pl.multiple_of(c * C, C), C)
        g_c = gc_scr[sl, :]
        q_c = q_scr[sl, :]
        k_c = k_scr[sl, :]
        v_c = v_scr[sl, :]
        for h in range(GLA_HEADS):
            ks = slice(h * GLA_DK, (h + 1) * GLA_DK)
            vs = slice(h * GLA_DV, (h + 1) * GLA_DV)
            g_h = g_c[:, ks]
            g_last = g_h[C - 1 : C, :]
            st = st_scr[h]
            oi_scr[sl, vs] += _dot_nt(q_c[:, ks] * jnp.exp(g_h), st)
            k_dec = k_c[:, ks] * jnp.exp(g_last - g_h)
            st_scr[h] = st * jnp.exp(g_last) + _dot_tn(v_c[:, vs], k_dec)
        return carry

    lax.fori_loop(0, nc, chunk_body, 0)
    o_ref[0] = _gla_finish(oi_scr[...], g, nw_ref[...]).astype(o_ref.dtype)

    @pl.when(t_idx == pl.num_programs(1) - 1)
    def _():
        for h in range(GLA_HEADS):
            sout_ref[0, h] = st_scr[h].T


def _gla_select_matrix():
    j = jnp.arange(GLA_CHUNK)[:, None, None]
    hl = jnp.arange(2)[None, :, None]
    rows_j = jnp.broadcast_to(j, (GLA_CHUNK, 2, GLA_DK)).reshape(-1)
    rows_h = jnp.broadcast_to(hl, (GLA_CHUNK, 2, GLA_DK)).reshape(-1)
    cols = jnp.arange(256)
    sel = (rows_j[:, None] == cols[None, :] % GLA_CHUNK) & (rows_h[:, None] == cols[None, :] // 128)
    return sel.astype(BF16)


def _gla_seq(z3, s0, gl, tb):
    b, l, _ = z3.shape
    const = lambda shape: pl.BlockSpec(shape, lambda i, j: (0,) * len(shape))
    return pl.pallas_call(
        _gla_seq_kernel,
        grid=(b, l // tb),
        in_specs=[
            pl.BlockSpec((1, tb, GLA_PROJ_PAD), lambda i, j: (i, j, 0)),
            pl.BlockSpec((1, GLA_HEADS, GLA_DK, GLA_DV), lambda i, j: (i, 0, 0, 0)),
            const((GLA_LORA_PAD, GLA_QK)), const((1, GLA_QK)), const((1, GLA_DV)),
            const((GLA_CHUNK * 128, 256)),
        ],
        out_specs=[
            pl.BlockSpec((1, tb, GLA_WIDTH), lambda i, j: (i, j, 0)),
            pl.BlockSpec((1, GLA_HEADS, GLA_DK, GLA_DV), lambda i, j: (i, 0, 0, 0)),
        ],
        out_shape=[
            jax.ShapeDtypeStruct((b, l, GLA_WIDTH), BF16),
            jax.ShapeDtypeStruct((b, GLA_HEADS, GLA_DK, GLA_DV), F32),
        ],
        scratch_shapes=[
            pltpu.VMEM((GLA_HEADS, GLA_DV, GLA_DK), F32),
            pltpu.VMEM((tb, GLA_CHUNK * 128), BF16),
            pltpu.VMEM((tb, GLA_QK), F32), pltpu.VMEM((tb, GLA_QK), F32), pltpu.VMEM((tb, GLA_WIDTH), F32),
            pltpu.VMEM((tb, GLA_QK), F32), pltpu.VMEM((tb, GLA_WIDTH), F32),
        ],
        compiler_params=_params(("parallel", "arbitrary")),
    )(z3, s0, gl["gkw"], gl["gkb"], gl["norm_w"], _gla_select_matrix())


def _gla_step_prep_kernel(z_ref, gkw_ref, gkb_ref, q_ref, k_ref, v_ref, g_ref, dec_ref):
    q, k, v, g, gk = _gla_features(z_ref[...], gkw_ref[...], gkb_ref[...])
    q_ref[...] = q
    k_ref[...] = k
    v_ref[...] = v
    g_ref[...] = g
    dec_ref[...] = jnp.exp(gk)


def _gla_step_prep(z, gl):
    n = z.shape[0]
    qk = jax.ShapeDtypeStruct((n, GLA_QK), F32)
    wide = jax.ShapeDtypeStruct((n, GLA_WIDTH), F32)
    return pl.pallas_call(
        _gla_step_prep_kernel,
        out_shape=[qk, qk, wide, wide, qk],
        compiler_params=pltpu.CompilerParams(vmem_limit_bytes=VMEM_LIMIT),
    )(z, gl["gkw"], gl["gkb"])


def _gla_step_kernel(s_ref, qcol_ref, kcol_ref, dcol_ref, vrow_ref, snew_ref, orow_ref):
    s_new = s_ref[...] * dcol_ref[...] + kcol_ref[...] * vrow_ref[...]
    snew_ref[...] = s_new
    orow_ref[...] = jnp.sum(s_new * qcol_ref[...], axis=2, keepdims=True)


def _gla_step(s0, q, k, dec, v, bb):
    n = s0.shape[0]
    colv = lambda t: t.reshape(n, GLA_HEADS, GLA_DK, 1)
    col_spec = pl.BlockSpec((bb, GLA_HEADS, GLA_DK, 1), lambda i: (i, 0, 0, 0))
    row_spec = pl.BlockSpec((bb, GLA_HEADS, 1, GLA_DV), lambda i: (i, 0, 0, 0))
    s_spec = pl.BlockSpec((bb, GLA_HEADS, GLA_DK, GLA_DV), lambda i: (i, 0, 0, 0))
    s_new, o_row = pl.pallas_call(
        _gla_step_kernel,
        grid=(n // bb,),
        in_specs=[s_spec, col_spec, col_spec, col_spec, row_spec],
        out_specs=[s_spec, row_spec],
        out_shape=[
            jax.ShapeDtypeStruct(s0.shape, F32),
            jax.ShapeDtypeStruct((n, GLA_HEADS, 1, GLA_DV), F32),
        ],
        compiler_params=_params(("parallel",)),
    )(s0, colv(q), colv(k), colv(dec), v.reshape(n, GLA_HEADS, 1, GLA_DV))
    return s_new, o_row.reshape(n, GLA_WIDTH)


def _step_post_kernel(orw_ref, bon_ref, grw_ref, lnw_ref, lnb_ref, ogl_ref, ggl_ref, nw_ref, o_rw_ref, o_gl_ref):
    o = orw_ref[...]
    for h in range(RW_HEADS):
        hs = slice(h * RW_N, (h + 1) * RW_N)
        o_h = o[:, hs]
        mean = jnp.mean(o_h, axis=-1, keepdims=True)
        cen = o_h - mean
        var = jnp.mean(cen * cen, axis=-1, keepdims=True)
        on = cen * lax.rsqrt(var + RW_GN_EPS)
        res = (on * lnw_ref[:, hs] + lnb_ref[:, hs] + bon_ref[:, hs]) * grw_ref[:, hs]
        o_rw_ref[:, hs] = res.astype(o_rw_ref.dtype)
    o_gl_ref[...] = _gla_finish(ogl_ref[...], ggl_ref[...], nw_ref[...]).astype(o_gl_ref.dtype)


def _step_post(o_rw, bonus, g_rw, rw, o_gl, g_gl, gl):
    n = o_rw.shape[0]
    return pl.pallas_call(
        _step_post_kernel,
        out_shape=[jax.ShapeDtypeStruct((n, RW_WIDTH), BF16), jax.ShapeDtypeStruct((n, GLA_WIDTH), BF16)],
        compiler_params=pltpu.CompilerParams(vmem_limit_bytes=VMEM_LIMIT),
    )(o_rw, bonus, g_rw, rw["ln_w"], rw["ln_b"], o_gl, g_gl, gl["norm_w"])


def _outproj_router_kernel(x_ref, orw_ref, ogl_ref, wo_ref, gain_ref, wrt_ref, br_ref,
                           h_ref, xn_ref, idx_ref, gate_ref, rank_ref, cnt_ref):
    tm = x_ref.shape[0]
    mix = jnp.concatenate([orw_ref[...], ogl_ref[...]], axis=-1)
    h = x_ref[...] + _dot(mix, wo_ref[...])
    h_ref[...] = h
    xn = h * lax.rsqrt(jnp.mean(h * h, axis=-1, keepdims=True) + NORM_EPS) * gain_ref[...]
    xn_ref[...] = xn.astype(xn_ref.dtype)
    logits = _dot_nt(wrt_ref[...], xn, precision=HIGHEST) + br_ref[...]
    eidx = lax.broadcasted_iota(jnp.int32, logits.shape, 0)
    ti = lax.broadcasted_iota(jnp.int32, (tm, tm), 0)
    tj = lax.broadcasted_iota(jnp.int32, (tm, tm), 1)
    before = (ti < tj).astype(BF16)
    vals, idxs = [], []
    work = logits
    chosen = jnp.zeros(logits.shape, F32)
    for _ in range(TOP_K):
        m = jnp.max(work, axis=0, keepdims=True)
        sel = jnp.min(jnp.where(work == m, eidx, N_EXPERTS), axis=0, keepdims=True)
        hit = eidx == sel
        work = jnp.where(hit, -jnp.inf, work)
        chosen = chosen + hit.astype(F32)
        vals.append(m)
        idxs.append(sel)
    prefix = _dot(chosen.astype(BF16), before)
    exps = [jnp.exp(v - vals[0]) for v in vals]
    denom = exps[0] + exps[1] + exps[2] + exps[3]
    for j in range(TOP_K):
        idx_ref[0, j : j + 1, :] = idxs[j]
        gate_ref[0, j : j + 1, :] = exps[j] / denom
        rank = jnp.sum(jnp.where(eidx == idxs[j], prefix, 0.0), axis=0, keepdims=True)
        rank_ref[0, j : j + 1, :] = rank.astype(jnp.int32)
    cnt = jnp.sum(chosen, axis=1, keepdims=True)
    cnt_ref[0] = jnp.broadcast_to(cnt, (N_EXPERTS, 128)).astype(jnp.int32)


def _outproj_router(x, o_rw, o_gl, w_out, gain, w_router_t, b_router, tm):
    n = x.shape[0]
    nt = n // tm
    const = lambda shape: pl.BlockSpec(shape, lambda i: (0,) * len(shape))
    tok = lambda width: pl.BlockSpec((tm, width), lambda i: (i, 0))
    lane = pl.BlockSpec((1, TOP_K, tm), lambda i: (i, 0, 0))
    return pl.pallas_call(
        _outproj_router_kernel,
        grid=(nt,),
        in_specs=[
            tok(D_MODEL), tok(RW_WIDTH), tok(GLA_WIDTH),
            const((D_MODEL, D_MODEL)), const((1, D_MODEL)), const((N_EXPERTS, D_MODEL)), const((N_EXPERTS, 1)),
        ],
        out_specs=[tok(D_MODEL), tok(D_MODEL), lane, lane, lane,
                   pl.BlockSpec((1, N_EXPERTS, 128), lambda i: (i, 0, 0))],
        out_shape=[
            jax.ShapeDtypeStruct((n, D_MODEL), F32),
            jax.ShapeDtypeStruct((n, D_MODEL), F32),
            jax.ShapeDtypeStruct((nt, TOP_K, tm), jnp.int32),
            jax.ShapeDtypeStruct((nt, TOP_K, tm), F32),
            jax.ShapeDtypeStruct((nt, TOP_K, tm), jnp.int32),
            jax.ShapeDtypeStruct((nt, N_EXPERTS, 128), jnp.int32),
        ],
        compiler_params=_params(("parallel",)),
    )(x, o_rw, o_gl, w_out, gain, w_router_t, b_router)


def _moe_kernel(be_ref, nu_ref, xs_ref, wg_ref, wu_ref, wd_ref, bg_ref, bu_ref, bd_ref, y_ref, wg_b, wu_b, wd_b):
    b = pl.program_id(0)
    prev = be_ref[jnp.maximum(b - 1, 0)]
    new_expert = (b == 0) | (be_ref[b] != prev)

    @pl.when(new_expert)
    def _():
        wg_b[...] = wg_ref[0].astype(BF16)
        wu_b[...] = wu_ref[0].astype(BF16)
        wd_b[...] = wd_ref[0].astype(BF16)

    @pl.when(b < nu_ref[0])
    def _():
        x = xs_ref[...].astype(BF16)
        half = D_MODEL // 2
        acc = None
        for f in range(2):
            fs = slice(f * half, (f + 1) * half)
            gt = _dot(x, wg_b[:, fs]) + bg_ref[0, :, fs]
            up = _dot(x, wu_b[:, fs]) + bu_ref[0, :, fs]
            gt = jnp.minimum(gt, SWIGLU_LIMIT)
            up = jnp.clip(up, -SWIGLU_LIMIT, SWIGLU_LIMIT)
            hid = (up + 1.0) * gt * _sigmoid(SWIGLU_ALPHA * gt)
            part = _dot(hid.astype(BF16), wd_b[fs, :])
            acc = part if acc is None else acc + part
        y_ref[...] = (acc + bd_ref[0]).astype(y_ref.dtype)


def _moe_ffn(block_expert, n_used, xs, w_gate, w_up, w_down, b_gate, b_up, b_down):
    n_blocks = block_expert.shape[0]
    row = lambda b, be, nu: (jnp.minimum(b, nu[0] - 1), 0)
    wsel = lambda b, be, nu: (be[b], 0, 0)
    wspec = pl.BlockSpec((1, D_MODEL, D_MODEL), wsel)
    bspec = pl.BlockSpec((1, 1, D_MODEL), wsel)
    grid_spec = pltpu.PrefetchScalarGridSpec(
        num_scalar_prefetch=2,
        grid=(n_blocks,),
        in_specs=[pl.BlockSpec((MOE_BLOCK, D_MODEL), row), wspec, wspec, wspec, bspec, bspec, bspec],
        out_specs=pl.BlockSpec((MOE_BLOCK, D_MODEL), row),
        scratch_shapes=[pltpu.VMEM((D_MODEL, D_MODEL), BF16)] * 3,
    )
    return pl.pallas_call(
        _moe_kernel,
        grid_spec=grid_spec,
        out_shape=jax.ShapeDtypeStruct((n_blocks * MOE_BLOCK, D_MODEL), F32),
        compiler_params=_params(("arbitrary",)),
    )(block_expert, n_used, xs, w_gate, w_up, w_down,
      b_gate.reshape(N_EXPERTS, 1, D_MODEL), b_up.reshape(N_EXPERTS, 1, D_MODEL),
      b_down.reshape(N_EXPERTS, 1, D_MODEL))


def _slot(idx_ref, rank_ref, base_ref, j, t):
    return base_ref[0, 0, idx_ref[0, j, t]] + rank_ref[0, j, t]


def _combine_kernel(idx_c, rank_c, base_c, idx_n, rank_n, base_n, h_ref, gate_ref, gain_ref, y_hbm, o_ref,
                    ybuf, sems):
    i = pl.program_id(0)
    nt = pl.num_programs(0)
    tm = h_ref.shape[0]

    def row_copy(src_row, slot, j, t):
        return pltpu.make_async_copy(y_hbm.at[pl.ds(src_row, 1)], ybuf.at[slot, j, pl.ds(t, 1)], sems.at[slot])

    def issue(idx_ref, rank_ref, base_ref, slot):
        def body(t, carry):
            for j in range(TOP_K):
                row_copy(_slot(idx_ref, rank_ref, base_ref, j, t), slot, j, t).start()
            return carry
        lax.fori_loop(0, tm, body, 0, unroll=8)

    @pl.when(i == 0)
    def _():
        issue(idx_c, rank_c, base_c, 0)

    @pl.when(i + 1 < nt)
    def _():
        issue(idx_n, rank_n, base_n, (i + 1) % 2)

    slot = i % 2
    for j in range(TOP_K):
        pltpu.make_async_copy(y_hbm.at[pl.ds(0, tm)], ybuf.at[slot, j], sems.at[slot]).wait()
    f = h_ref[...]
    gates = gate_ref[...]
    for j in range(TOP_K):
        f = f + gates[:, j : j + 1] * ybuf[slot, j]
    o_ref[...] = f * lax.rsqrt(jnp.mean(f * f, axis=-1, keepdims=True) + NORM_EPS) * gain_ref[...]


def _combine(h, y_rows, idx3, rank3, base3, gates, gain, tm):
    n = h.shape[0]
    nt = n // tm
    smem = lambda shape, imap: pl.BlockSpec(shape, imap, memory_space=pltpu.SMEM)
    cur = lambda i: (i, 0, 0)
    nxt = lambda i: (jnp.minimum(i + 1, nt - 1), 0, 0)
    return pl.pallas_call(
        _combine_kernel,
        grid=(nt,),
        in_specs=[
            smem((1, TOP_K, tm), cur), smem((1, TOP_K, tm), cur), smem((1, 1, N_EXPERTS), cur),
            smem((1, TOP_K, tm), nxt), smem((1, TOP_K, tm), nxt), smem((1, 1, N_EXPERTS), nxt),
            pl.BlockSpec((tm, D_MODEL), lambda i: (i, 0)),
            pl.BlockSpec((tm, TOP_K), lambda i: (i, 0)),
            pl.BlockSpec((1, D_MODEL), lambda i: (0, 0)),
            pl.BlockSpec(memory_space=pl.ANY),
        ],
        out_specs=pl.BlockSpec((tm, D_MODEL), lambda i: (i, 0)),
        out_shape=jax.ShapeDtypeStruct((n, D_MODEL), F32),
        scratch_shapes=[pltpu.VMEM((2, TOP_K, tm, D_MODEL), F32), pltpu.SemaphoreType.DMA((2,))],
        compiler_params=pltpu.CompilerParams(dimension_semantics=("arbitrary",), vmem_limit_bytes=VMEM_LIMIT,
                                             disable_bounds_checks=True),
    )(idx3, rank3, base3, idx3, rank3, base3, h, gates, gain, y_rows)


def _dispatch_kernel(*refs, fill):
    if fill:
        idx_ref, rank_ref, base_ref, ends_ref, x_ref, xs_hbm, zero_scr, sem, zsem = refs
    else:
        idx_ref, rank_ref, base_ref, x_ref, _, xs_hbm, sem = refs
    tm = x_ref.shape[0]

    if fill:
        def fill_copy(e):
            start = pl.multiple_of(ends_ref[0, e] - MOE_BLOCK, MOE_BLOCK)
            return pltpu.make_async_copy(zero_scr, xs_hbm.at[pl.ds(start, MOE_BLOCK)], zsem)

        @pl.when(pl.program_id(0) == 0)
        def _():
            zero_scr[...] = jnp.zeros(zero_scr.shape, zero_scr.dtype)
            for e in range(N_EXPERTS):
                @pl.when(ends_ref[1, e] > 0)
                def _():
                    fill_copy(e).start()
            for e in range(N_EXPERTS):
                @pl.when(ends_ref[1, e] > 0)
                def _():
                    fill_copy(e).wait()

    def body(t, carry):
        for j in range(TOP_K):
            dst = _slot(idx_ref, rank_ref, base_ref, j, t)
            pltpu.make_async_copy(x_ref.at[pl.ds(t, 1)], xs_hbm.at[pl.ds(dst, 1)], sem).start()
        return carry

    lax.fori_loop(0, tm, body, 0, unroll=8)
    for j in range(TOP_K):
        pltpu.make_async_copy(x_ref, xs_hbm.at[pl.ds(0, tm)], sem).wait()


def _dispatch(xn, idx3, rank3, base3, tm, n_slots, ends=None, xs=None):
    n = xn.shape[0]
    fill = xs is None
    smem = lambda shape, imap: pl.BlockSpec(shape, imap, memory_space=pltpu.SMEM)
    cur = lambda i: (i, 0, 0)
    in_specs = [smem((1, TOP_K, tm), cur), smem((1, TOP_K, tm), cur), smem((1, 1, N_EXPERTS), cur)]
    args = [idx3, rank3, base3]
    scratch = []
    if fill:
        in_specs.append(smem((2, N_EXPERTS), lambda i: (0, 0)))
        args.append(ends)
        scratch = [pltpu.VMEM((MOE_BLOCK, D_MODEL), F32)]
    in_specs.append(pl.BlockSpec((tm, D_MODEL), lambda i: (i, 0)))
    args.append(xn)
    aliases = {}
    if not fill:
        in_specs.append(pl.BlockSpec(memory_space=pl.ANY))
        args.append(xs)
        aliases = {len(args) - 1: 0}
    scratch.append(pltpu.SemaphoreType.DMA(()))
    if fill:
        scratch.append(pltpu.SemaphoreType.DMA(()))
    return pl.pallas_call(
        functools.partial(_dispatch_kernel, fill=fill),
        grid=(n // tm,),
        in_specs=in_specs,
        out_specs=pl.BlockSpec(memory_space=pl.ANY),
        out_shape=jax.ShapeDtypeStruct((n_slots, D_MODEL), F32),
        scratch_shapes=scratch,
        input_output_aliases=aliases,
        compiler_params=pltpu.CompilerParams(dimension_semantics=("arbitrary",), vmem_limit_bytes=VMEM_LIMIT,
                                             disable_bounds_checks=True, has_side_effects=True),
    )(*args)


def _pad_rows(w, rows, offset):
    out = jnp.zeros((rows, w.shape[1]), w.dtype)
    return out.at[offset : offset + w.shape[0]].set(w)


def _routing_tables(counts, n_pairs):
    n_blocks = (n_pairs + N_EXPERTS * (MOE_BLOCK - 1) + MOE_BLOCK - 1) // MOE_BLOCK
    total = jnp.sum(counts, axis=0)
    padded = (total + MOE_BLOCK - 1) // MOE_BLOCK * MOE_BLOCK
    pends = jnp.cumsum(padded)
    pstarts = pends - padded
    tile_base = pstarts[None, :] + jnp.cumsum(counts, axis=0) - counts
    blocks = jnp.arange(n_blocks, dtype=jnp.int32) * MOE_BLOCK
    n_used = (pends[-1] // MOE_BLOCK).astype(jnp.int32)
    owner = jnp.sum((pends[None, :] <= blocks[:, None]).astype(jnp.int32), axis=1)
    block_expert = jnp.minimum(owner, N_EXPERTS - 1)
    last = jnp.sum(jnp.where(jnp.arange(n_blocks) == n_used - 1, block_expert, 0))
    block_expert = jnp.where(jnp.arange(n_blocks) < n_used, block_expert, last)
    ends = jnp.stack([pends, padded]).astype(jnp.int32)
    return tile_base.astype(jnp.int32)[:, None, :], ends, block_expert.astype(jnp.int32), n_used.reshape(1), n_blocks


def kernel(x_prompt, x_sample, state_rwkv_shift, state_rwkv_wkv, state_gla, norm_mix, w_in, rw_mu, rw_w0, rw_w2, rw_a0, rw_a2, rw_g2, rw_k_k, rw_k_a, rw_r_k, rw_ln_w, rw_ln_b, gla_gk_w2, gla_gk_b, gla_norm_w, w_out, norm_ffn, w_router, b_router, w_gate, b_gate, w_up, b_up, w_down, b_down, norm_final):
    depth = norm_mix.shape[0]
    assert depth == 1
    bp, lp, d = x_prompt.shape
    bs, ls, _ = x_sample.shape
    assert ls == 1 and lp % SEQ_BLOCK == 0
    l = 0
    row = lambda t: t.reshape(1, -1)

    w_in_b = w_in[l].astype(BF16)
    w_in_r = w_in_b[:, :RW_PROJ]
    w_in_g = jnp.pad(w_in_b[:, RW_PROJ:], ((0, 0), (0, GLA_PROJ_PAD - GLA_PROJ)))
    rw = dict(
        mu=row(rw_mu[l]), w0=row(rw_w0[l]), a0=row(rw_a0[l]),
        w2p=_pad_rows(rw_w2[l].astype(BF16), 128, 0), a2p=_pad_rows(rw_a2[l].astype(BF16), 128, 64),
        g2=rw_g2[l].astype(BF16), k_k=row(rw_k_k[l]), k_a=row(rw_k_a[l]), r_k=row(rw_r_k[l]),
        ln_w=row(rw_ln_w[l]), ln_b=row(rw_ln_b[l]))
    gl = dict(gkw=_pad_rows(gla_gk_w2[l].astype(BF16), GLA_LORA_PAD, 0), gkb=row(gla_gk_b[l]),
              norm_w=row(gla_norm_w[l]))
    gain_mix = row(norm_mix[l])

    n_p = bp * lp
    xp = x_prompt.reshape(n_p, d)
    zr_p, zg_p = _inproj(xp, gain_mix, w_in_r, w_in_g, TOK_BLOCK)
    zr_p3 = zr_p.reshape(bp, lp, RW_PROJ)
    o_rw_p, wkv_p = _rwkv_seq(zr_p3, jnp.zeros((bp, 1, RW_PROJ), F32),
                              jnp.zeros((bp, RW_HEADS, RW_N, RW_N), F32), rw, SEQ_BLOCK)
    o_gl_p, gla_p = _gla_seq(zg_p.reshape(bp, lp, GLA_PROJ_PAD),
                             jnp.zeros((bp, GLA_HEADS, GLA_DK, GLA_DV), F32), gl, SEQ_BLOCK)
    shift_p = zr_p3[:, -1, :]

    xs_ = x_sample.reshape(bs, d)
    zr_s, zg_s = _inproj(xs_, gain_mix, w_in_r, w_in_g, bs)
    r, k, v, al, be, dec, g_rw, bonus = _rwkv_step_prep(zr_s, state_rwkv_shift[l], rw)
    wkv_s, o_rw_s = _rwkv_step(state_rwkv_wkv[l], r, k, al, be, dec, v, 16)
    q, kg, vg, g_gl, dec_g = _gla_step_prep(zg_s, gl)
    gla_s, o_gl_s = _gla_step(state_gla[l], q, kg, dec_g, vg, 16)
    o_rw_s2, o_gl_s2 = _step_post(o_rw_s, bonus, g_rw, rw, o_gl_s, g_gl, gl)
    shift_s = zr_s

    w_out_b = w_out[l].astype(BF16)
    router = (w_out_b, row(norm_ffn[l]), w_router[l].T, b_router[l].reshape(N_EXPERTS, 1))
    h_p, xn_p, idx_p, gate_p, rank_p, cnt_p = _outproj_router(
        xp, o_rw_p.reshape(n_p, RW_WIDTH), o_gl_p.reshape(n_p, GLA_WIDTH), *router, TOK_BLOCK)
    h_s, xn_s, idx_s, gate_s, rank_s, cnt_s = _outproj_router(xs_, o_rw_s2, o_gl_s2, *router, bs)
    nt_p = n_p // TOK_BLOCK
    counts = jnp.concatenate([cnt_p[:, :, 0], cnt_s[:, :, 0]], axis=0)
    base3, ends, block_expert, n_used, n_blocks = _routing_tables(counts, (n_p + bs) * TOP_K)
    n_slots = n_blocks * MOE_BLOCK
    xs_rows = _dispatch(xn_p, idx_p, rank_p, base3[:nt_p], TOK_BLOCK, n_slots, ends=ends)
    xs_rows = _dispatch(xn_s, idx_s, rank_s, base3[nt_p:], bs, n_slots, xs=xs_rows)
    y_rows = _moe_ffn(block_expert, n_used, xs_rows, w_gate[l], w_up[l], w_down[l], b_gate[l], b_up[l], b_down[l])
    gates = lambda g3: jnp.swapaxes(g3, 1, 2).reshape(-1, TOP_K)
    gain_f = row(norm_final)
    y_p = _combine(h_p, y_rows, idx_p, rank_p, base3[:nt_p], gates(gate_p), gain_f, TOK_BLOCK)
    y_s = _combine(h_s, y_rows, idx_s, rank_s, base3[nt_p:], gates(gate_s), gain_f, bs)

    y_prompt = y_p.reshape(bp, lp, d)
    y_sample = y_s.reshape(bs, ls, d)
    return (y_prompt, y_sample, shift_p[None], wkv_p[None], gla_p[None], shift_s[None], wkv_s[None], gla_s[None])
```

```python
import functools

import jax
import jax.numpy as jnp
from jax import lax
from jax.experimental import pallas as pl
from jax.experimental.pallas import tpu as pltpu

F32 = jnp.float32
BF16 = jnp.bfloat16
HIGHEST = lax.Precision.HIGHEST

D_MODEL = 1024
RW_WIDTH = 512
RW_HEADS = 8
RW_N = 64
RW_PROJ = 1792
RW_GN_EPS = 64e-5
GLA_HEADS = 4
GLA_DK = 64
GLA_DV = 128
GLA_WIDTH = 512
GLA_QK = GLA_HEADS * GLA_DK
GLA_PROJ = 1552
GLA_PROJ_PAD = 1664
GLA_LORA_PAD = 128
GLA_GATE_NORMALIZER = 16.0
N_EXPERTS = 32
TOP_K = 4
SWIGLU_LIMIT = 7.0
SWIGLU_ALPHA = 1.702
NORM_EPS = 1e-5

RW_CHUNK = 64
GLA_CHUNK = 16
SEQ_BLOCK = 512
TOK_BLOCK = 512
MOE_BLOCK = 512
VMEM_LIMIT = 56 * 1024 * 1024


def _dot(a, b, precision=None):
    return jnp.dot(a, b, preferred_element_type=F32, precision=precision)


def _dot_nt(a, b, precision=None):
    return lax.dot_general(a, b, (((1,), (1,)), ((), ())), preferred_element_type=F32, precision=precision)


def _dot_tn(a, b, precision=None):
    return lax.dot_general(a, b, (((0,), (0,)), ((), ())), preferred_element_type=F32, precision=precision)


def _sigmoid(x):
    return 1.0 / (1.0 + jnp.exp(-x))


def _softplus(x):
    return jnp.maximum(x, 0.0) + jnp.log(1.0 + jnp.exp(-jnp.abs(x)))


def _params(sem):
    return pltpu.CompilerParams(dimension_semantics=sem, vmem_limit_bytes=VMEM_LIMIT)


def _inproj_kernel(x_ref, gain_ref, wr_ref, wg_ref, zr_ref, zg_ref):
    x = x_ref[...]
    xn = x * lax.rsqrt(jnp.mean(x * x, axis=-1, keepdims=True) + NORM_EPS) * gain_ref[...]
    xb = xn.astype(BF16)
    zr_ref[...] = _dot(xb, wr_ref[...])
    zg_ref[...] = _dot(xb, wg_ref[...])


def _inproj(x, gain, w_r, w_g, tm):
    n = x.shape[0]
    return pl.pallas_call(
        _inproj_kernel,
        grid=(n // tm,),
        in_specs=[
            pl.BlockSpec((tm, D_MODEL), lambda i: (i, 0)),
            pl.BlockSpec((1, D_MODEL), lambda i: (0, 0)),
            pl.BlockSpec((D_MODEL, RW_PROJ), lambda i: (0, 0)),
            pl.BlockSpec((D_MODEL, GLA_PROJ_PAD), lambda i: (0, 0)),
        ],
        out_specs=[
            pl.BlockSpec((tm, RW_PROJ), lambda i: (i, 0)),
            pl.BlockSpec((tm, GLA_PROJ_PAD), lambda i: (i, 0)),
        ],
        out_shape=[
            jax.ShapeDtypeStruct((n, RW_PROJ), F32),
            jax.ShapeDtypeStruct((n, GLA_PROJ_PAD), F32),
        ],
        compiler_params=_params(("parallel",)),
    )(x, gain, w_r, w_g)


def _rwkv_features(zs, w0, w2p, a0, a2p, g2, k_k, k_a):
    W = RW_WIDTH
    r = zs[:, 0:W]
    k_raw = zs[:, W : 2 * W]
    v = zs[:, 2 * W : 3 * W]
    zwa = zs[:, 3 * W : 3 * W + 128]
    zg = zs[:, 3 * W + 128 :]
    w = -_softplus(-(w0 + _dot(jnp.tanh(zwa).astype(BF16), w2p))) - 0.5
    log_decay = -jnp.exp(w)
    a = _sigmoid(a0 + _dot(zwa.astype(BF16), a2p))
    g = _dot(_sigmoid(zg).astype(BF16), g2)
    kk_raw = k_raw * k_k
    k = k_raw * (1.0 + (a - 1.0) * k_a)
    return r, k, v, kk_raw, a, log_decay, g


def _level_mask(ri, ci, lvl):
    same = (ri >> (lvl + 1)) == (ci >> (lvl + 1))
    return same & (((ri >> lvl) & 1) == 1) & (((ci >> lvl) & 1) == 0)


def _rwkv_seq_kernel(z_ref, shift0_ref, s0_ref, mu_ref, w0_ref, w2_ref, a0_ref, a2_ref, g2_ref, kk_ref, ka_ref,
                     rk_ref, lnw_ref, lnb_ref, o_ref, sout_ref,
                     m_scr, prev_scr, r_scr, k_scr, v_scr, kkr_scr, a_scr, lw_scr, on_scr, bon_scr):
    C = RW_CHUNK
    N = RW_N
    t_idx = pl.program_id(1)
    tb = z_ref.shape[1]
    zero_nn = jnp.zeros((N, N), F32)

    @pl.when(t_idx == 0)
    def _():
        prev_scr[...] = shift0_ref[0]
        for p in range(RW_HEADS // 2):
            top = jnp.concatenate([s0_ref[0, 2 * p].T, zero_nn], axis=1)
            bot = jnp.concatenate([zero_nn, s0_ref[0, 2 * p + 1].T], axis=1)
            m_scr[p] = jnp.concatenate([top, bot], axis=0)

    z = z_ref[0]
    row = lax.broadcasted_iota(jnp.int32, z.shape, 0)
    z_prev = jnp.where(row == 0, prev_scr[...], pltpu.roll(z, 1, axis=0))
    prev_scr[...] = z[tb - 1 : tb, :]
    zs = z + mu_ref[...] * (z_prev - z)
    r, k, v, kk_raw, a, log_decay, g = _rwkv_features(
        zs, w0_ref[...], w2_ref[...], a0_ref[...], a2_ref[...], g2_ref[...], kk_ref[...], ka_ref[...])
    r_scr[...] = r
    k_scr[...] = k
    v_scr[...] = v
    kkr_scr[...] = kk_raw
    a_scr[...] = a
    lw_scr[...] = log_decay

    P2 = 2 * N
    ri = lax.broadcasted_iota(jnp.int32, (C, P2), 0)
    ci = lax.broadcasted_iota(jnp.int32, (C, P2), 1) % N
    left = lax.broadcasted_iota(jnp.int32, (C, P2), 1) < N
    tril = ri >= ci
    stril = ri > ci
    eye_f = (ri == ci).astype(F32)
    rb = lax.broadcasted_iota(jnp.int32, (P2, P2), 0)
    cb = lax.broadcasted_iota(jnp.int32, (P2, P2), 1)
    same_head = (rb < N) == (cb < N)
    eye_b = rb == cb
    rc = lax.broadcasted_iota(jnp.int32, (C, C), 0)
    cc = lax.broadcasted_iota(jnp.int32, (C, C), 1)
    tril_f = (rc >= cc).astype(F32)
    rk_all = rk_ref[...]

    def bdiag(x):
        return jnp.concatenate([jnp.where(left, x, 0.0), jnp.where(left, 0.0, x)], axis=0)

    def head_sum(x):
        s0 = jnp.sum(jnp.where(left, x, 0.0), axis=-1, keepdims=True)
        s1 = jnp.sum(jnp.where(left, 0.0, x), axis=-1, keepdims=True)
        return jnp.where(left, s0, s1)

    n_sub = 2
    pairs = range(RW_HEADS // 2)

    def chunk_body(it, carry):
        units = [(s, p) for s in range(n_sub) for p in pairs]
        sls = [pl.ds(pl.multiple_of((it * n_sub + s) * C, C), C) for s in range(n_sub)]
        prep = []
        for s in range(n_sub):
            lw = lw_scr[sls[s], :]
            cum = _dot(tril_f, lw, precision=HIGHEST)
            cum_last = cum[C - 1 : C, :]
            prep.append(dict(
                e_incl=jnp.exp(cum), e_excl=jnp.exp(cum - lw), e_neg=jnp.exp(-cum),
                e_tail=jnp.exp(cum_last - cum), p_last=jnp.exp(cum_last),
                r=r_scr[sls[s], :], k=k_scr[sls[s], :], v=v_scr[sls[s], :], kk=kkr_scr[sls[s], :],
                a=a_scr[sls[s], :]))
        lanes = [slice(p * P2, (p + 1) * P2) for p in pairs]
        get = lambda name: [prep[s][name][:, lanes[p]] for s, p in units]
        r2, k2, v2, kk2, a2 = get("r"), get("k"), get("v"), get("kk"), get("a")
        e_incl, e_excl, e_neg, e_tail, p_last = get("e_incl"), get("e_excl"), get("e_neg"), get("e_tail"), get("p_last")
        un = range(len(units))
        al = [kk2[u] / jnp.maximum(jnp.sqrt(head_sum(kk2[u] * kk2[u])), 1e-12) for u in un]
        be = [al[u] * a2[u] for u in un]
        al_t = [al[u] * e_excl[u] for u in un]
        r_t = [r2[u] * e_incl[u] for u in un]
        be_n = [be[u] * e_neg[u] for u in un]
        k_n = [k2[u] * e_neg[u] for u in un]
        k_et = [(k2[u] * e_tail[u]).T for u in un]
        be_et = [(be[u] * e_tail[u]).T for u in un]
        v_bd = [bdiag(v2[u]) for u in un]
        lhs = [jnp.concatenate([al_t[u], r_t[u]], axis=0) for u in un]
        s_b = [_dot_nt(lhs[u], bdiag(be_n[u])) for u in un]
        s_k = [_dot_nt(lhs[u], bdiag(k_n[u])) for u in un]
        l_ab = [jnp.where(stril, s_b[u][:C], 0.0) for u in un]
        a_rb = [jnp.where(tril, s_b[u][C:], 0.0) for u in un]
        l_ak = [jnp.where(stril, s_k[u][:C], 0.0) for u in un]
        a_rk = [jnp.where(tril, s_k[u][C:], 0.0) for u in un]
        lakv = [_dot(l_ak[u], v_bd[u]) for u in un]
        arkv = [_dot(a_rk[u], v_bd[u]) for u in un]
        kev = [_dot(k_et[u], v2[u]) for u in un]
        t_inv = [eye_f - jnp.where(_level_mask(ri, ci, 0), l_ab[u], 0.0) for u in un]
        lvl = 1
        while (1 << lvl) < C:
            lm = _level_mask(ri, ci, lvl)
            tn = [_dot(t_inv[u], bdiag(jnp.where(lm, l_ab[u], 0.0))) for u in un]
            t_inv = [t_inv[u] - _dot(tn[u], bdiag(t_inv[u])) for u in un]
            lvl += 1
        a_til = [_dot(t_inv[u], bdiag(al_t[u])) for u in un]
        b_til = [_dot(t_inv[u], bdiag(lakv[u])) for u in un]
        r_hat = [r_t[u] - _dot(a_rb[u], bdiag(a_til[u])) for u in un]
        o_hat = [arkv[u] - _dot(a_rb[u], bdiag(b_til[u])) for u in un]
        g_bd = [jnp.where(same_head, jnp.where(eye_b, p_last[u], 0.0) - _dot(be_et[u], a_til[u]), 0.0) for u in un]
        h_bd = [jnp.where(same_head, kev[u] - _dot(be_et[u], b_til[u]), 0.0) for u in un]
        lhs_m = [jnp.concatenate([r_hat[u], g_bd[u]], axis=0) for u in un]
        for u, (s, p) in enumerate(units):
            res = _dot(lhs_m[u], m_scr[p])
            m_scr[p] = res[C:] + h_bd[u]
            o_p = res[:C] + o_hat[u]
            cen = o_p - head_sum(o_p) * (1.0 / N)
            var = head_sum(cen * cen) * (1.0 / N)
            on_scr[sls[s], lanes[p]] = cen * lax.rsqrt(var + RW_GN_EPS)
            bon_scr[sls[s], lanes[p]] = head_sum(r2[u] * k2[u] * rk_all[:, lanes[p]]) * v2[u]
        return carry

    lax.fori_loop(0, tb // (C * n_sub), chunk_body, 0)
    out = (on_scr[...] * lnw_ref[...] + lnb_ref[...] + bon_scr[...]) * g
    o_ref[0] = out.astype(o_ref.dtype)

    @pl.when(t_idx == pl.num_programs(1) - 1)
    def _():
        for p in range(RW_HEADS // 2):
            m = m_scr[p]
            sout_ref[0, 2 * p] = m[:N, :N].T
            sout_ref[0, 2 * p + 1] = m[N:, N:].T


def _rwkv_seq(z3, shift0, s0, rw, tb):
    b, l, _ = z3.shape
    const = lambda shape: pl.BlockSpec(shape, lambda i, j: (0,) * len(shape))
    wide = lambda: pltpu.VMEM((tb, RW_WIDTH), F32)
    return pl.pallas_call(
        _rwkv_seq_kernel,
        grid=(b, l // tb),
        in_specs=[
            pl.BlockSpec((1, tb, RW_PROJ), lambda i, j: (i, j, 0)),
            pl.BlockSpec((1, 1, RW_PROJ), lambda i, j: (i, 0, 0)),
            pl.BlockSpec((1, RW_HEADS, RW_N, RW_N), lambda i, j: (i, 0, 0, 0)),
            const((1, RW_PROJ)),
            const((1, RW_WIDTH)), const((128, RW_WIDTH)),
            const((1, RW_WIDTH)), const((128, RW_WIDTH)),
            const((128, RW_WIDTH)),
            const((1, RW_WIDTH)), const((1, RW_WIDTH)), const((1, RW_WIDTH)),
            const((1, RW_WIDTH)), const((1, RW_WIDTH)),
        ],
        out_specs=[
            pl.BlockSpec((1, tb, RW_WIDTH), lambda i, j: (i, j, 0)),
            pl.BlockSpec((1, RW_HEADS, RW_N, RW_N), lambda i, j: (i, 0, 0, 0)),
        ],
        out_shape=[
            jax.ShapeDtypeStruct((b, l, RW_WIDTH), BF16),
            jax.ShapeDtypeStruct((b, RW_HEADS, RW_N, RW_N), F32),
        ],
        scratch_shapes=[
            pltpu.VMEM((RW_HEADS // 2, 2 * RW_N, 2 * RW_N), F32),
            pltpu.VMEM((1, RW_PROJ), F32),
            wide(), wide(), wide(), wide(), wide(), wide(), wide(), wide(),
        ],
        compiler_params=_params(("parallel", "arbitrary")),
    )(z3, shift0, s0, rw["mu"], rw["w0"], rw["w2p"], rw["a0"], rw["a2p"], rw["g2"], rw["k_k"], rw["k_a"],
      rw["r_k"], rw["ln_w"], rw["ln_b"])


def _rwkv_step_prep_kernel(z_ref, shift0_ref, mu_ref, w0_ref, w2_ref, a0_ref, a2_ref, g2_ref, kk_ref, ka_ref,
                           rk_ref, r_ref, k_ref, v_ref, al_ref, be_ref, dec_ref, g_ref, bon_ref):
    z = z_ref[...]
    zs = z + mu_ref[...] * (shift0_ref[...] - z)
    r, k, v, kk_raw, a, log_decay, g = _rwkv_features(
        zs, w0_ref[...], w2_ref[...], a0_ref[...], a2_ref[...], g2_ref[...], kk_ref[...], ka_ref[...])
    rk_all = rk_ref[...]
    for h in range(RW_HEADS):
        hs = slice(h * RW_N, (h + 1) * RW_N)
        kk_h = kk_raw[:, hs]
        nrm = jnp.sqrt(jnp.sum(kk_h * kk_h, axis=-1, keepdims=True))
        al = kk_h / jnp.maximum(nrm, 1e-12)
        al_ref[:, hs] = al
        be_ref[:, hs] = al * a[:, hs]
        bon_ref[:, hs] = jnp.sum(r[:, hs] * k[:, hs] * rk_all[:, hs], axis=-1, keepdims=True) * v[:, hs]
    r_ref[...] = r
    k_ref[...] = k
    v_ref[...] = v
    dec_ref[...] = jnp.exp(log_decay)
    g_ref[...] = g


def _rwkv_step_prep(z, shift0, rw):
    n = z.shape[0]
    out = jax.ShapeDtypeStruct((n, RW_WIDTH), F32)
    return pl.pallas_call(
        _rwkv_step_prep_kernel,
        out_shape=[out] * 8,
        compiler_params=pltpu.CompilerParams(vmem_limit_bytes=VMEM_LIMIT),
    )(z, shift0, rw["mu"], rw["w0"], rw["w2p"], rw["a0"], rw["a2p"], rw["g2"], rw["k_k"], rw["k_a"], rw["r_k"])


def _rwkv_step_kernel(s_ref, r_ref, k_ref, al_ref, be_ref, dec_ref, vcol_ref, snew_ref, ocol_ref):
    s = s_ref[...]
    sa = -jnp.sum(s * al_ref[...], axis=-1, keepdims=True)
    s_new = s * dec_ref[...] + sa * be_ref[...] + vcol_ref[...] * k_ref[...]
    snew_ref[...] = s_new
    ocol_ref[...] = jnp.sum(s_new * r_ref[...], axis=-1, keepdims=True)


def _rwkv_step(s0, r, k, al, be, dec, v, bb):
    n = s0.shape[0]
    rowv = lambda t: t.reshape(n, RW_HEADS, 1, RW_N)
    row_spec = pl.BlockSpec((bb, RW_HEADS, 1, RW_N), lambda i: (i, 0, 0, 0))
    col_spec = pl.BlockSpec((bb, RW_HEADS, RW_N, 1), lambda i: (i, 0, 0, 0))
    s_spec = pl.BlockSpec((bb, RW_HEADS, RW_N, RW_N), lambda i: (i, 0, 0, 0))
    s_new, o_col = pl.pallas_call(
        _rwkv_step_kernel,
        grid=(n // bb,),
        in_specs=[s_spec, row_spec, row_spec, row_spec, row_spec, row_spec, col_spec],
        out_specs=[s_spec, col_spec],
        out_shape=[
            jax.ShapeDtypeStruct(s0.shape, F32),
            jax.ShapeDtypeStruct((n, RW_HEADS, RW_N, 1), F32),
        ],
        compiler_params=_params(("parallel",)),
    )(s0, rowv(r), rowv(k), rowv(al), rowv(be), rowv(dec), v.reshape(n, RW_HEADS, RW_N, 1))
    return s_new, o_col.reshape(n, RW_WIDTH)


def _gla_features(z, gkw, gkb):
    q = z[:, 0:GLA_QK] * (GLA_DK ** -0.5)
    k = z[:, GLA_QK : 2 * GLA_QK]
    v = z[:, 2 * GLA_QK : 2 * GLA_QK + GLA_WIDTH]
    g = z[:, 2 * GLA_QK + GLA_WIDTH : 2 * GLA_QK + 2 * GLA_WIDTH]
    zgk = z[:, 2 * GLA_QK + 2 * GLA_WIDTH :]
    gk = -_softplus(-(_dot(zgk.astype(BF16), gkw) + gkb)) / GLA_GATE_NORMALIZER
    return q, k, v, g, gk


def _gla_finish(o, g, norm_w):
    outs = []
    for h in range(GLA_HEADS):
        hs = slice(h * GLA_DV, (h + 1) * GLA_DV)
        o_h = o[:, hs]
        o_h = o_h * lax.rsqrt(jnp.mean(o_h * o_h, axis=-1, keepdims=True) + NORM_EPS) * norm_w
        g_h = g[:, hs]
        outs.append(o_h * (g_h * _sigmoid(g_h)))
    return jnp.concatenate(outs, axis=-1)


def _gla_seq_kernel(z_ref, s0_ref, gkw_ref, gkb_ref, nw_ref, wsel_ref, o_ref, sout_ref,
                    st_scr, x_scr, q_scr, k_scr, v_scr, gc_scr, oi_scr):
    C = GLA_CHUNK
    G = 128
    t_idx = pl.program_id(1)
    tb = z_ref.shape[1]
    nc = tb // C

    @pl.when(t_idx == 0)
    def _():
        for h in range(GLA_HEADS):
            st_scr[h] = s0_ref[0, h].T

    q, k, v, g, gk = _gla_features(z_ref[0], gkw_ref[...], gkb_ref[...])
    ri = lax.broadcasted_iota(jnp.int32, (G, G), 0)
    ci = lax.broadcasted_iota(jnp.int32, (G, G), 1)
    cum_mat = ((ri // C == ci // C) & (ri >= ci)).astype(F32)
    for m in range(tb // G):
        rows = slice(m * G, (m + 1) * G)
        gc_scr[rows, :] = _dot(cum_mat, gk[rows, :], precision=HIGHEST)
    q_scr[...] = q
    k_scr[...] = k
    v_scr[...] = v
    gcum = gc_scr[...]

    rg = lax.broadcasted_iota(jnp.int32, (tb, 2 * G), 0)
    cg = lax.broadcasted_iota(jnp.int32, (tb, 2 * G), 1)
    blk_mask = ((cg % G) // C == (rg % G) // C) & (cg % C <= rg % C)
    for p in range(GLA_HEADS // 2):
        ls = slice(p * 128, (p + 1) * 128)
        q3 = q[:, ls].reshape(nc, C, 128)
        k3 = k[:, ls].reshape(nc, C, 128)
        g3 = gcum[:, ls].reshape(nc, C, 128)
        for j in range(C):
            e = q3 * jnp.exp(jnp.minimum(g3 - g3[:, j : j + 1, :], 0.0)) * k3[:, j : j + 1, :]
            x_scr[:, j * 128 : (j + 1) * 128] = e.reshape(tb, 128).astype(BF16)
        a_t = jnp.where(blk_mask, _dot(x_scr[...], wsel_ref[...]), 0.0).astype(BF16)
        for hl in range(2):
            h = 2 * p + hl
            for m in range(tb // G):
                rows = slice(m * G, (m + 1) * G)
                a_blk = a_t[rows, hl * G : (hl + 1) * G]
                oi_scr[rows, h * GLA_DV : (h + 1) * GLA_DV] = _dot(
                    a_blk, v[rows, h * GLA_DV : (h + 1) * GLA_DV].astype(BF16))

    def chunk_body(c, carry):
        sl = pl.ds(pl.multiple_of(c * C, C), C)
        g_c = gc_scr[sl, :]
        q_c = q_scr[sl, :]
        k_c = k_scr[sl, :]
        v_c = v_scr[sl, :]
        for h in range(GLA_HEADS):
            ks = slice(h * GLA_DK, (h + 1) * GLA_DK)
            vs = slice(h * GLA_DV, (h + 1) * GLA_DV)
            g_h = g_c[:, ks]
            g_last = g_h[C - 1 : C, :]
            st = st_scr[h]
            oi_scr[sl, vs] += _dot_nt(q_c[:, ks] * jnp.exp(g_h), st)
            k_dec = k_c[:, ks] * jnp.exp(g_last - g_h)
            st_scr[h] = st * jnp.exp(g_last) + _dot_tn(v_c[:, vs], k_dec)
        return carry

    lax.fori_loop(0, nc, chunk_body, 0, unroll=8)
    o_ref[0] = _gla_finish(oi_scr[...], g, nw_ref[...]).astype(o_ref.dtype)

    @pl.when(t_idx == pl.num_programs(1) - 1)
    def _():
        for h in range(GLA_HEADS):
            sout_ref[0, h] = st_scr[h].T


def _gla_select_matrix():
    j = jnp.arange(GLA_CHUNK)[:, None, None]
    hl = jnp.arange(2)[None, :, None]
    rows_j = jnp.broadcast_to(j, (GLA_CHUNK, 2, GLA_DK)).reshape(-1)
    rows_h = jnp.broadcast_to(hl, (GLA_CHUNK, 2, GLA_DK)).reshape(-1)
    cols = jnp.arange(256)
    sel = (rows_j[:, None] == cols[None, :] % GLA_CHUNK) & (rows_h[:, None] == cols[None, :] // 128)
    return sel.astype(BF16)


def _gla_seq(z3, s0, gl, tb):
    b, l, _ = z3.shape
    const = lambda shape: pl.BlockSpec(shape, lambda i, j: (0,) * len(shape))
    return pl.pallas_call(
        _gla_seq_kernel,
        grid=(b, l // tb),
        in_specs=[
            pl.BlockSpec((1, tb, GLA_PROJ_PAD), lambda i, j: (i, j, 0)),
            pl.BlockSpec((1, GLA_HEADS, GLA_DK, GLA_DV), lambda i, j: (i, 0, 0, 0)),
            const((GLA_LORA_PAD, GLA_QK)), const((1, GLA_QK)), const((1, GLA_DV)),
            const((GLA_CHUNK * 128, 256)),
        ],
        out_specs=[
            pl.BlockSpec((1, tb, GLA_WIDTH), lambda i, j: (i, j, 0)),
            pl.BlockSpec((1, GLA_HEADS, GLA_DK, GLA_DV), lambda i, j: (i, 0, 0, 0)),
        ],
        out_shape=[
            jax.ShapeDtypeStruct((b, l, GLA_WIDTH), BF16),
            jax.ShapeDtypeStruct((b, GLA_HEADS, GLA_DK, GLA_DV), F32),
        ],
        scratch_shapes=[
            pltpu.VMEM((GLA_HEADS, GLA_DV, GLA_DK), F32),
            pltpu.VMEM((tb, GLA_CHUNK * 128), BF16),
            pltpu.VMEM((tb, GLA_QK), F32), pltpu.VMEM((tb, GLA_QK), F32), pltpu.VMEM((tb, GLA_WIDTH), F32),
            pltpu.VMEM((tb, GLA_QK), F32), pltpu.VMEM((tb, GLA_WIDTH), F32),
        ],
        compiler_params=_params(("parallel", "arbitrary")),
    )(z3, s0, gl["gkw"], gl["gkb"], gl["norm_w"], _gla_select_matrix())


def _gla_step_prep_kernel(z_ref, gkw_ref, gkb_ref, q_ref, k_ref, v_ref, g_ref, dec_ref):
    q, k, v, g, gk = _gla_features(z_ref[...], gkw_ref[...], gkb_ref[...])
    q_ref[...] = q
    k_ref[...] = k
    v_ref[...] = v
    g_ref[...] = g
    dec_ref[...] = jnp.exp(gk)


def _gla_step_prep(z, gl):
    n = z.shape[0]
    qk = jax.ShapeDtypeStruct((n, GLA_QK), F32)
    wide = jax.ShapeDtypeStruct((n, GLA_WIDTH), F32)
    return pl.pallas_call(
        _gla_step_prep_kernel,
        out_shape=[qk, qk, wide, wide, qk],
        compiler_params=pltpu.CompilerParams(vmem_limit_bytes=VMEM_LIMIT),
    )(z, gl["gkw"], gl["gkb"])


def _gla_step_kernel(s_ref, qcol_ref, kcol_ref, dcol_ref, vrow_ref, snew_ref, orow_ref):
    s_new = s_ref[...] * dcol_ref[...] + kcol_ref[...] * vrow_ref[...]
    snew_ref[...] = s_new
    orow_ref[...] = jnp.sum(s_new * qcol_ref[...], axis=2, keepdims=True)


def _gla_step(s0, q, k, dec, v, bb):
    n = s0.shape[0]
    colv = lambda t: t.reshape(n, GLA_HEADS, GLA_DK, 1)
    col_spec = pl.BlockSpec((bb, GLA_HEADS, GLA_DK, 1), lambda i: (i, 0, 0, 0))
    row_spec = pl.BlockSpec((bb, GLA_HEADS, 1, GLA_DV), lambda i: (i, 0, 0, 0))
    s_spec = pl.BlockSpec((bb, GLA_HEADS, GLA_DK, GLA_DV), lambda i: (i, 0, 0, 0))
    s_new, o_row = pl.pallas_call(
        _gla_step_kernel,
        grid=(n // bb,),
        in_specs=[s_spec, col_spec, col_spec, col_spec, row_spec],
        out_specs=[s_spec, row_spec],
        out_shape=[
            jax.ShapeDtypeStruct(s0.shape, F32),
            jax.ShapeDtypeStruct((n, GLA_HEADS, 1, GLA_DV), F32),
        ],
        compiler_params=_params(("parallel",)),
    )(s0, colv(q), colv(k), colv(dec), v.reshape(n, GLA_HEADS, 1, GLA_DV))
    return s_new, o_row.reshape(n, GLA_WIDTH)


def _step_post_kernel(orw_ref, bon_ref, grw_ref, lnw_ref, lnb_ref, ogl_ref, ggl_ref, nw_ref, o_rw_ref, o_gl_ref):
    o = orw_ref[...]
    for h in range(RW_HEADS):
        hs = slice(h * RW_N, (h + 1) * RW_N)
        o_h = o[:, hs]
        mean = jnp.mean(o_h, axis=-1, keepdims=True)
        cen = o_h - mean
        var = jnp.mean(cen * cen, axis=-1, keepdims=True)
        on = cen * lax.rsqrt(var + RW_GN_EPS)
        res = (on * lnw_ref[:, hs] + lnb_ref[:, hs] + bon_ref[:, hs]) * grw_ref[:, hs]
        o_rw_ref[:, hs] = res.astype(o_rw_ref.dtype)
    o_gl_ref[...] = _gla_finish(ogl_ref[...], ggl_ref[...], nw_ref[...]).astype(o_gl_ref.dtype)


def _step_post(o_rw, bonus, g_rw, rw, o_gl, g_gl, gl):
    n = o_rw.shape[0]
    return pl.pallas_call(
        _step_post_kernel,
        out_shape=[jax.ShapeDtypeStruct((n, RW_WIDTH), BF16), jax.ShapeDtypeStruct((n, GLA_WIDTH), BF16)],
        compiler_params=pltpu.CompilerParams(vmem_limit_bytes=VMEM_LIMIT),
    )(o_rw, bonus, g_rw, rw["ln_w"], rw["ln_b"], o_gl, g_gl, gl["norm_w"])


def _outproj_router_kernel(x_ref, orw_ref, ogl_ref, wo_ref, gain_ref, wrt_ref, br_ref,
                           h_ref, xn_ref, idx_ref, gate_ref, rank_ref, cnt_ref):
    tm = x_ref.shape[0]
    mix = jnp.concatenate([orw_ref[...], ogl_ref[...]], axis=-1)
    h = x_ref[...] + _dot(mix, wo_ref[...])
    h_ref[...] = h
    xn = h * lax.rsqrt(jnp.mean(h * h, axis=-1, keepdims=True) + NORM_EPS) * gain_ref[...]
    xn_ref[...] = xn.astype(xn_ref.dtype)
    logits = _dot_nt(wrt_ref[...], xn, precision=HIGHEST) + br_ref[...]
    eidx = lax.broadcasted_iota(jnp.int32, logits.shape, 0)
    ti = lax.broadcasted_iota(jnp.int32, (tm, tm), 0)
    tj = lax.broadcasted_iota(jnp.int32, (tm, tm), 1)
    before = (ti < tj).astype(BF16)
    vals, idxs = [], []
    work = logits
    chosen = jnp.zeros(logits.shape, F32)
    for _ in range(TOP_K):
        m = jnp.max(work, axis=0, keepdims=True)
        sel = jnp.min(jnp.where(work == m, eidx, N_EXPERTS), axis=0, keepdims=True)
        hit = eidx == sel
        work = jnp.where(hit, -jnp.inf, work)
        chosen = chosen + hit.astype(F32)
        vals.append(m)
        idxs.append(sel)
    prefix = _dot(chosen.astype(BF16), before)
    exps = [jnp.exp(v - vals[0]) for v in vals]
    denom = exps[0] + exps[1] + exps[2] + exps[3]
    for j in range(TOP_K):
        idx_ref[0, j : j + 1, :] = idxs[j]
        gate_ref[0, j : j + 1, :] = exps[j] / denom
        rank = jnp.sum(jnp.where(eidx == idxs[j], prefix, 0.0), axis=0, keepdims=True)
        rank_ref[0, j : j + 1, :] = rank.astype(jnp.int32)
    cnt = jnp.sum(chosen, axis=1, keepdims=True)
    cnt_ref[0] = jnp.broadcast_to(cnt, (N_EXPERTS, 128)).astype(jnp.int32)


def _outproj_router(x, o_rw, o_gl, w_out, gain, w_router_t, b_router, tm):
    n = x.shape[0]
    nt = n // tm
    const = lambda shape: pl.BlockSpec(shape, lambda i: (0,) * len(shape))
    tok = lambda width: pl.BlockSpec((tm, width), lambda i: (i, 0))
    lane = pl.BlockSpec((1, TOP_K, tm), lambda i: (i, 0, 0))
    return pl.pallas_call(
        _outproj_router_kernel,
        grid=(nt,),
        in_specs=[
            tok(D_MODEL), tok(RW_WIDTH), tok(GLA_WIDTH),
            const((D_MODEL, D_MODEL)), const((1, D_MODEL)), const((N_EXPERTS, D_MODEL)), const((N_EXPERTS, 1)),
        ],
        out_specs=[tok(D_MODEL), tok(D_MODEL), lane, lane, lane,
                   pl.BlockSpec((1, N_EXPERTS, 128), lambda i: (i, 0, 0))],
        out_shape=[
            jax.ShapeDtypeStruct((n, D_MODEL), F32),
            jax.ShapeDtypeStruct((n, D_MODEL), F32),
            jax.ShapeDtypeStruct((nt, TOP_K, tm), jnp.int32),
            jax.ShapeDtypeStruct((nt, TOP_K, tm), F32),
            jax.ShapeDtypeStruct((nt, TOP_K, tm), jnp.int32),
            jax.ShapeDtypeStruct((nt, N_EXPERTS, 128), jnp.int32),
        ],
        compiler_params=_params(("parallel",)),
    )(x, o_rw, o_gl, w_out, gain, w_router_t, b_router)


def _moe_kernel(be_ref, nu_ref, xs_ref, wg_ref, wu_ref, wd_ref, bg_ref, bu_ref, bd_ref, y_ref, wg_b, wu_b, wd_b):
    b = pl.program_id(0)
    prev = be_ref[jnp.maximum(b - 1, 0)]
    new_expert = (b == 0) | (be_ref[b] != prev)

    @pl.when(new_expert)
    def _():
        wg_b[...] = wg_ref[0].astype(BF16)
        wu_b[...] = wu_ref[0].astype(BF16)
        wd_b[...] = wd_ref[0].astype(BF16)

    @pl.when(b < nu_ref[0])
    def _():
        x = xs_ref[...].astype(BF16)
        half = D_MODEL // 2
        acc = None
        for f in range(2):
            fs = slice(f * half, (f + 1) * half)
            gt = _dot(x, wg_b[:, fs]) + bg_ref[0, :, fs]
            up = _dot(x, wu_b[:, fs]) + bu_ref[0, :, fs]
            gt = jnp.minimum(gt, SWIGLU_LIMIT)
            up = jnp.clip(up, -SWIGLU_LIMIT, SWIGLU_LIMIT)
            hid = (up + 1.0) * gt * _sigmoid(SWIGLU_ALPHA * gt)
            part = _dot(hid.astype(BF16), wd_b[fs, :])
            acc = part if acc is None else acc + part
        y_ref[...] = (acc + bd_ref[0]).astype(y_ref.dtype)


def _moe_ffn(block_expert, n_used, xs, w_gate, w_up, w_down, b_gate, b_up, b_down):
    n_blocks = block_expert.shape[0]
    row = lambda b, be, nu: (jnp.minimum(b, nu[0] - 1), 0)
    wsel = lambda b, be, nu: (be[b], 0, 0)
    wspec = pl.BlockSpec((1, D_MODEL, D_MODEL), wsel)
    bspec = pl.BlockSpec((1, 1, D_MODEL), wsel)
    grid_spec = pltpu.PrefetchScalarGridSpec(
        num_scalar_prefetch=2,
        grid=(n_blocks,),
        in_specs=[pl.BlockSpec((MOE_BLOCK, D_MODEL), row), wspec, wspec, wspec, bspec, bspec, bspec],
        out_specs=pl.BlockSpec((MOE_BLOCK, D_MODEL), row),
        scratch_shapes=[pltpu.VMEM((D_MODEL, D_MODEL), BF16)] * 3,
    )
    return pl.pallas_call(
        _moe_kernel,
        grid_spec=grid_spec,
        out_shape=jax.ShapeDtypeStruct((n_blocks * MOE_BLOCK, D_MODEL), F32),
        compiler_params=_params(("arbitrary",)),
    )(block_expert, n_used, xs, w_gate, w_up, w_down,
      b_gate.reshape(N_EXPERTS, 1, D_MODEL), b_up.reshape(N_EXPERTS, 1, D_MODEL),
      b_down.reshape(N_EXPERTS, 1, D_MODEL))


def _combine_kernel(slot_c, slot_n, h_ref, gate_ref, gain_ref, y_hbm, o_ref, ybuf, sems):
    i = pl.program_id(0)
    nt = pl.num_programs(0)
    tm = h_ref.shape[0]

    def row_copy(src_row, slot, j, t):
        return pltpu.make_async_copy(y_hbm.at[pl.ds(src_row, 1)], ybuf.at[slot, j, pl.ds(t, 1)], sems.at[slot])

    def issue(slot_ref, slot):
        def body(t, carry):
            for j in range(TOP_K):
                row_copy(slot_ref[0, j, t], slot, j, t).start()
            return carry
        lax.fori_loop(0, tm, body, 0, unroll=8)

    @pl.when(i == 0)
    def _():
        issue(slot_c, 0)

    @pl.when(i + 1 < nt)
    def _():
        issue(slot_n, (i + 1) % 2)

    slot = i % 2
    for j in range(TOP_K):
        pltpu.make_async_copy(y_hbm.at[pl.ds(0, tm)], ybuf.at[slot, j], sems.at[slot]).wait()
    f = h_ref[...]
    gates = gate_ref[...]
    for j in range(TOP_K):
        f = f + gates[:, j : j + 1] * ybuf[slot, j]
    o_ref[...] = f * lax.rsqrt(jnp.mean(f * f, axis=-1, keepdims=True) + NORM_EPS) * gain_ref[...]


def _combine(h, y_rows, slot3, gates, gain, tm):
    n = h.shape[0]
    nt = n // tm
    smem = lambda shape, imap: pl.BlockSpec(shape, imap, memory_space=pltpu.SMEM)
    cur = lambda i: (i, 0, 0)
    nxt = lambda i: (jnp.minimum(i + 1, nt - 1), 0, 0)
    return pl.pallas_call(
        _combine_kernel,
        grid=(nt,),
        in_specs=[
            smem((1, TOP_K, tm), cur), smem((1, TOP_K, tm), nxt),
            pl.BlockSpec((tm, D_MODEL), lambda i: (i, 0)),
            pl.BlockSpec((tm, TOP_K), lambda i: (i, 0)),
            pl.BlockSpec((1, D_MODEL), lambda i: (0, 0)),
            pl.BlockSpec(memory_space=pl.ANY),
        ],
        out_specs=pl.BlockSpec((tm, D_MODEL), lambda i: (i, 0)),
        out_shape=jax.ShapeDtypeStruct((n, D_MODEL), F32),
        scratch_shapes=[pltpu.VMEM((2, TOP_K, tm, D_MODEL), F32), pltpu.SemaphoreType.DMA((2,))],
        compiler_params=pltpu.CompilerParams(dimension_semantics=("arbitrary",), vmem_limit_bytes=VMEM_LIMIT,
                                             disable_bounds_checks=True),
    )(slot3, slot3, h, gates, gain, y_rows)


def _dispatch_kernel(*refs, fill):
    if fill:
        slot_ref, ends_ref, x_ref, xs_hbm, zero_scr, sem, zsem = refs
    else:
        slot_ref, x_ref, _, xs_hbm, sem = refs
    tm = x_ref.shape[0]

    if fill:
        def fill_copy(e):
            start = pl.multiple_of(ends_ref[0, e] - MOE_BLOCK, MOE_BLOCK)
            return pltpu.make_async_copy(zero_scr, xs_hbm.at[pl.ds(start, MOE_BLOCK)], zsem)

        @pl.when(pl.program_id(0) == 0)
        def _():
            zero_scr[...] = jnp.zeros(zero_scr.shape, zero_scr.dtype)
            for e in range(N_EXPERTS):
                @pl.when(ends_ref[1, e] > 0)
                def _():
                    fill_copy(e).start()
            for e in range(N_EXPERTS):
                @pl.when(ends_ref[1, e] > 0)
                def _():
                    fill_copy(e).wait()

    def body(t, carry):
        for j in range(TOP_K):
            pltpu.make_async_copy(x_ref.at[pl.ds(t, 1)], xs_hbm.at[pl.ds(slot_ref[0, j, t], 1)], sem).start()
        return carry

    lax.fori_loop(0, tm, body, 0, unroll=8)
    for j in range(TOP_K):
        pltpu.make_async_copy(x_ref, xs_hbm.at[pl.ds(0, tm)], sem).wait()


def _dispatch(xn, slot3, tm, n_slots, ends=None, xs=None):
    n = xn.shape[0]
    fill = xs is None
    smem = lambda shape, imap: pl.BlockSpec(shape, imap, memory_space=pltpu.SMEM)
    in_specs = [smem((1, TOP_K, tm), lambda i: (i, 0, 0))]
    args = [slot3]
    scratch = []
    if fill:
        in_specs.append(smem((2, N_EXPERTS), lambda i: (0, 0)))
        args.append(ends)
        scratch = [pltpu.VMEM((MOE_BLOCK, D_MODEL), F32)]
    in_specs.append(pl.BlockSpec((tm, D_MODEL), lambda i: (i, 0)))
    args.append(xn)
    aliases = {}
    if not fill:
        in_specs.append(pl.BlockSpec(memory_space=pl.ANY))
        args.append(xs)
        aliases = {len(args) - 1: 0}
    scratch.append(pltpu.SemaphoreType.DMA(()))
    if fill:
        scratch.append(pltpu.SemaphoreType.DMA(()))
    return pl.pallas_call(
        functools.partial(_dispatch_kernel, fill=fill),
        grid=(n // tm,),
        in_specs=in_specs,
        out_specs=pl.BlockSpec(memory_space=pl.ANY),
        out_shape=jax.ShapeDtypeStruct((n_slots, D_MODEL), F32),
        scratch_shapes=scratch,
        input_output_aliases=aliases,
        compiler_params=pltpu.CompilerParams(dimension_semantics=("arbitrary",), vmem_limit_bytes=VMEM_LIMIT,
                                             disable_bounds_checks=True, has_side_effects=True),
    )(*args)


def _pad_rows(w, rows, offset):
    out = jnp.zeros((rows, w.shape[1]), w.dtype)
    return out.at[offset : offset + w.shape[0]].set(w)


def _routing_tables(counts, n_pairs):
    n_blocks = (n_pairs + N_EXPERTS * (MOE_BLOCK - 1) + MOE_BLOCK - 1) // MOE_BLOCK
    total = jnp.sum(counts, axis=0)
    padded = (total + MOE_BLOCK - 1) // MOE_BLOCK * MOE_BLOCK
    pends = jnp.cumsum(padded)
    pstarts = pends - padded
    tile_base = pstarts[None, :] + jnp.cumsum(counts, axis=0) - counts
    blocks = jnp.arange(n_blocks, dtype=jnp.int32) * MOE_BLOCK
    n_used = (pends[-1] // MOE_BLOCK).astype(jnp.int32)
    owner = jnp.sum((pends[None, :] <= blocks[:, None]).astype(jnp.int32), axis=1)
    block_expert = jnp.minimum(owner, N_EXPERTS - 1)
    last = jnp.sum(jnp.where(jnp.arange(n_blocks) == n_used - 1, block_expert, 0))
    block_expert = jnp.where(jnp.arange(n_blocks) < n_used, block_expert, last)
    ends = jnp.stack([pends, padded]).astype(jnp.int32)
    return tile_base.astype(jnp.int32)[:, None, :], ends, block_expert.astype(jnp.int32), n_used.reshape(1), n_blocks


def kernel(x_prompt, x_sample, state_rwkv_shift, state_rwkv_wkv, state_gla, norm_mix, w_in, rw_mu, rw_w0, rw_w2, rw_a0, rw_a2, rw_g2, rw_k_k, rw_k_a, rw_r_k, rw_ln_w, rw_ln_b, gla_gk_w2, gla_gk_b, gla_norm_w, w_out, norm_ffn, w_router, b_router, w_gate, b_gate, w_up, b_up, w_down, b_down, norm_final):
    depth = norm_mix.shape[0]
    assert depth == 1
    bp, lp, d = x_prompt.shape
    bs, ls, _ = x_sample.shape
    assert ls == 1 and lp % SEQ_BLOCK == 0
    l = 0
    row = lambda t: t.reshape(1, -1)

    w_in_b = w_in[l].astype(BF16)
    w_in_r = w_in_b[:, :RW_PROJ]
    w_in_g = jnp.pad(w_in_b[:, RW_PROJ:], ((0, 0), (0, GLA_PROJ_PAD - GLA_PROJ)))
    rw = dict(
        mu=row(rw_mu[l]), w0=row(rw_w0[l]), a0=row(rw_a0[l]),
        w2p=_pad_rows(rw_w2[l].astype(BF16), 128, 0), a2p=_pad_rows(rw_a2[l].astype(BF16), 128, 64),
        g2=rw_g2[l].astype(BF16), k_k=row(rw_k_k[l]), k_a=row(rw_k_a[l]), r_k=row(rw_r_k[l]),
        ln_w=row(rw_ln_w[l]), ln_b=row(rw_ln_b[l]))
    gl = dict(gkw=_pad_rows(gla_gk_w2[l].astype(BF16), GLA_LORA_PAD, 0), gkb=row(gla_gk_b[l]),
              norm_w=row(gla_norm_w[l]))
    gain_mix = row(norm_mix[l])

    n_p = bp * lp
    xp = x_prompt.reshape(n_p, d)
    zr_p, zg_p = _inproj(xp, gain_mix, w_in_r, w_in_g, TOK_BLOCK)
    zr_p3 = zr_p.reshape(bp, lp, RW_PROJ)
    o_rw_p, wkv_p = _rwkv_seq(zr_p3, jnp.zeros((bp, 1, RW_PROJ), F32),
                              jnp.zeros((bp, RW_HEADS, RW_N, RW_N), F32), rw, SEQ_BLOCK)
    o_gl_p, gla_p = _gla_seq(zg_p.reshape(bp, lp, GLA_PROJ_PAD),
                             jnp.zeros((bp, GLA_HEADS, GLA_DK, GLA_DV), F32), gl, SEQ_BLOCK)
    shift_p = zr_p3[:, -1, :]

    xs_ = x_sample.reshape(bs, d)
    zr_s, zg_s = _inproj(xs_, gain_mix, w_in_r, w_in_g, bs)
    r, k, v, al, be, dec, g_rw, bonus = _rwkv_step_prep(zr_s, state_rwkv_shift[l], rw)
    wkv_s, o_rw_s = _rwkv_step(state_rwkv_wkv[l], r, k, al, be, dec, v, 16)
    q, kg, vg, g_gl, dec_g = _gla_step_prep(zg_s, gl)
    gla_s, o_gl_s = _gla_step(state_gla[l], q, kg, dec_g, vg, 16)
    o_rw_s2, o_gl_s2 = _step_post(o_rw_s, bonus, g_rw, rw, o_gl_s, g_gl, gl)
    shift_s = zr_s

    w_out_b = w_out[l].astype(BF16)
    router = (w_out_b, row(norm_ffn[l]), w_router[l].T, b_router[l].reshape(N_EXPERTS, 1))
    h_p, xn_p, idx_p, gate_p, rank_p, cnt_p = _outproj_router(
        xp, o_rw_p.reshape(n_p, RW_WIDTH), o_gl_p.reshape(n_p, GLA_WIDTH), *router, TOK_BLOCK)
    h_s, xn_s, idx_s, gate_s, rank_s, cnt_s = _outproj_router(xs_, o_rw_s2, o_gl_s2, *router, bs)
    nt_p = n_p // TOK_BLOCK
    counts = jnp.concatenate([cnt_p[:, :, 0], cnt_s[:, :, 0]], axis=0)
    base3, ends, block_expert, n_used, n_blocks = _routing_tables(counts, (n_p + bs) * TOP_K)
    n_slots = n_blocks * MOE_BLOCK

    def slots(idx3, rank3, base):
        hit = idx3[..., None] == jnp.arange(N_EXPERTS, dtype=jnp.int32)
        return rank3 + jnp.sum(jnp.where(hit, base[:, :, None, :], 0), axis=-1)

    slot_p = slots(idx_p, rank_p, base3[:nt_p])
    slot_s = slots(idx_s, rank_s, base3[nt_p:])
    xs_rows = _dispatch(xn_p, slot_p, TOK_BLOCK, n_slots, ends=ends)
    xs_rows = _dispatch(xn_s, slot_s, bs, n_slots, xs=xs_rows)
    y_rows = _moe_ffn(block_expert, n_used, xs_rows, w_gate[l], w_up[l], w_down[l], b_gate[l], b_up[l], b_down[l])
    gates = lambda g3: jnp.swapaxes(g3, 1, 2).reshape(-1, TOP_K)
    gain_f = row(norm_final)
    y_p = _combine(h_p, y_rows, slot_p, gates(gate_p), gain_f, TOK_BLOCK)
    y_s = _combine(h_s, y_rows, slot_s, gates(gate_s), gain_f, bs)

    y_prompt = y_p.reshape(bp, lp, d)
    y_sample = y_s.reshape(bs, ls, d)
    return (y_prompt, y_sample, shift_p[None], wkv_p[None], gla_p[None], shift_s[None], wkv_s[None], gla_s[None])
```

```python
import functools

import jax
import jax.numpy as jnp
from jax import lax
from jax.experimental import pallas as pl
from jax.experimental.pallas import tpu as pltpu

F32 = jnp.float32
BF16 = jnp.bfloat16
HIGHEST = lax.Precision.HIGHEST

D_MODEL = 1024
RW_WIDTH = 512
RW_HEADS = 8
RW_N = 64
RW_PROJ = 1792
RW_GN_EPS = 64e-5
GLA_HEADS = 4
GLA_DK = 64
GLA_DV = 128
GLA_WIDTH = 512
GLA_QK = GLA_HEADS * GLA_DK
GLA_PROJ = 1552
GLA_PROJ_PAD = 1664
GLA_LORA_PAD = 128
GLA_GATE_NORMALIZER = 16.0
N_EXPERTS = 32
TOP_K = 4
SWIGLU_LIMIT = 7.0
SWIGLU_ALPHA = 1.702
NORM_EPS = 1e-5

RW_CHUNK = 64
GLA_CHUNK = 16
SEQ_BLOCK = 512
TOK_BLOCK = 512
MOE_BLOCK = 512
VMEM_LIMIT = 56 * 1024 * 1024


def _dot(a, b, precision=None):
    return jnp.dot(a, b, preferred_element_type=F32, precision=precision)


def _dot_nt(a, b, precision=None):
    return lax.dot_general(a, b, (((1,), (1,)), ((), ())), preferred_element_type=F32, precision=precision)


def _dot_tn(a, b, precision=None):
    return lax.dot_general(a, b, (((0,), (0,)), ((), ())), preferred_element_type=F32, precision=precision)


def _sigmoid(x):
    return 1.0 / (1.0 + jnp.exp(-x))


def _softplus(x):
    return jnp.maximum(x, 0.0) + jnp.log(1.0 + jnp.exp(-jnp.abs(x)))


def _params(sem):
    return pltpu.CompilerParams(dimension_semantics=sem, vmem_limit_bytes=VMEM_LIMIT)


ROW_TILE = D_MODEL // 128


def _store_row_tiles(ref, x):
    m = x.shape[0]
    for c in range(ROW_TILE):
        ref[pl.ds(c, m, stride=ROW_TILE), :] = x[:, c * 128 : (c + 1) * 128]


def _load_row_tiles(ref, m):
    return jnp.concatenate([ref[pl.ds(c, m, stride=ROW_TILE), :] for c in range(ROW_TILE)], axis=-1)


def _inproj_kernel(x_ref, gain_ref, wr_ref, wg_ref, zr_ref, zg_ref):
    x = x_ref[...]
    xn = x * lax.rsqrt(jnp.mean(x * x, axis=-1, keepdims=True) + NORM_EPS) * gain_ref[...]
    xb = xn.astype(BF16)
    zr_ref[...] = _dot(xb, wr_ref[...])
    zg_ref[...] = _dot(xb, wg_ref[...])


def _inproj(x, gain, w_r, w_g, tm):
    n = x.shape[0]
    return pl.pallas_call(
        _inproj_kernel,
        grid=(n // tm,),
        in_specs=[
            pl.BlockSpec((tm, D_MODEL), lambda i: (i, 0)),
            pl.BlockSpec((1, D_MODEL), lambda i: (0, 0)),
            pl.BlockSpec((D_MODEL, RW_PROJ), lambda i: (0, 0)),
            pl.BlockSpec((D_MODEL, GLA_PROJ_PAD), lambda i: (0, 0)),
        ],
        out_specs=[
            pl.BlockSpec((tm, RW_PROJ), lambda i: (i, 0)),
            pl.BlockSpec((tm, GLA_PROJ_PAD), lambda i: (i, 0)),
        ],
        out_shape=[
            jax.ShapeDtypeStruct((n, RW_PROJ), F32),
            jax.ShapeDtypeStruct((n, GLA_PROJ_PAD), F32),
        ],
        compiler_params=_params(("parallel",)),
    )(x, gain, w_r, w_g)


def _rwkv_features(zs, w0, w2p, a0, a2p, g2, k_k, k_a):
    W = RW_WIDTH
    r = zs[:, 0:W]
    k_raw = zs[:, W : 2 * W]
    v = zs[:, 2 * W : 3 * W]
    zwa = zs[:, 3 * W : 3 * W + 128]
    zg = zs[:, 3 * W + 128 :]
    w = -_softplus(-(w0 + _dot(jnp.tanh(zwa).astype(BF16), w2p))) - 0.5
    log_decay = -jnp.exp(w)
    a = _sigmoid(a0 + _dot(zwa.astype(BF16), a2p))
    g = _dot(_sigmoid(zg).astype(BF16), g2)
    kk_raw = k_raw * k_k
    k = k_raw * (1.0 + (a - 1.0) * k_a)
    return r, k, v, kk_raw, a, log_decay, g


def _level_mask(ri, ci, lvl):
    same = (ri >> (lvl + 1)) == (ci >> (lvl + 1))
    return same & (((ri >> lvl) & 1) == 1) & (((ci >> lvl) & 1) == 0)


def _rwkv_seq_kernel(z_ref, shift0_ref, s0_ref, mu_ref, w0_ref, w2_ref, a0_ref, a2_ref, g2_ref, kk_ref, ka_ref,
                     rk_ref, lnw_ref, lnb_ref, o_ref, sout_ref,
                     m_scr, prev_scr, r_scr, k_scr, v_scr, kkr_scr, a_scr, lw_scr, on_scr, bon_scr):
    C = RW_CHUNK
    N = RW_N
    t_idx = pl.program_id(1)
    tb = z_ref.shape[1]
    zero_nn = jnp.zeros((N, N), F32)

    @pl.when(t_idx == 0)
    def _():
        prev_scr[...] = shift0_ref[0]
        for p in range(RW_HEADS // 2):
            top = jnp.concatenate([s0_ref[0, 2 * p].T, zero_nn], axis=1)
            bot = jnp.concatenate([zero_nn, s0_ref[0, 2 * p + 1].T], axis=1)
            m_scr[p] = jnp.concatenate([top, bot], axis=0)

    z = z_ref[0]
    row = lax.broadcasted_iota(jnp.int32, z.shape, 0)
    z_prev = jnp.where(row == 0, prev_scr[...], pltpu.roll(z, 1, axis=0))
    prev_scr[...] = z[tb - 1 : tb, :]
    zs = z + mu_ref[...] * (z_prev - z)
    r, k, v, kk_raw, a, log_decay, g = _rwkv_features(
        zs, w0_ref[...], w2_ref[...], a0_ref[...], a2_ref[...], g2_ref[...], kk_ref[...], ka_ref[...])
    r_scr[...] = r
    k_scr[...] = k
    v_scr[...] = v
    kkr_scr[...] = kk_raw
    a_scr[...] = a
    lw_scr[...] = log_decay

    P2 = 2 * N
    ri = lax.broadcasted_iota(jnp.int32, (C, P2), 0)
    ci = lax.broadcasted_iota(jnp.int32, (C, P2), 1) % N
    left = lax.broadcasted_iota(jnp.int32, (C, P2), 1) < N
    tril = ri >= ci
    stril = ri > ci
    eye_f = (ri == ci).astype(F32)
    rb = lax.broadcasted_iota(jnp.int32, (P2, P2), 0)
    cb = lax.broadcasted_iota(jnp.int32, (P2, P2), 1)
    same_head = (rb < N) == (cb < N)
    eye_b = rb == cb
    rc = lax.broadcasted_iota(jnp.int32, (C, C), 0)
    cc = lax.broadcasted_iota(jnp.int32, (C, C), 1)
    tril_f = (rc >= cc).astype(F32)
    rk_all = rk_ref[...]

    def bdiag(x):
        return jnp.concatenate([jnp.where(left, x, 0.0), jnp.where(left, 0.0, x)], axis=0)

    def head_sum(x):
        s0 = jnp.sum(jnp.where(left, x, 0.0), axis=-1, keepdims=True)
        s1 = jnp.sum(jnp.where(left, 0.0, x), axis=-1, keepdims=True)
        return jnp.where(left, s0, s1)

    n_sub = 2
    pairs = range(RW_HEADS // 2)

    def chunk_body(it, carry):
        units = [(s, p) for s in range(n_sub) for p in pairs]
        sls = [pl.ds(pl.multiple_of((it * n_sub + s) * C, C), C) for s in range(n_sub)]
        prep = []
        for s in range(n_sub):
            lw = lw_scr[sls[s], :]
            cum = _dot(tril_f, lw, precision=HIGHEST)
            cum_last = cum[C - 1 : C, :]
            prep.append(dict(
                e_incl=jnp.exp(cum), e_excl=jnp.exp(cum - lw), e_neg=jnp.exp(-cum),
                e_tail=jnp.exp(cum_last - cum), p_last=jnp.exp(cum_last),
                r=r_scr[sls[s], :], k=k_scr[sls[s], :], v=v_scr[sls[s], :], kk=kkr_scr[sls[s], :],
                a=a_scr[sls[s], :]))
        lanes = [slice(p * P2, (p + 1) * P2) for p in pairs]
        get = lambda name: [prep[s][name][:, lanes[p]] for s, p in units]
        r2, k2, v2, kk2, a2 = get("r"), get("k"), get("v"), get("kk"), get("a")
        e_incl, e_excl, e_neg, e_tail, p_last = get("e_incl"), get("e_excl"), get("e_neg"), get("e_tail"), get("p_last")
        un = range(len(units))
        al = [kk2[u] / jnp.maximum(jnp.sqrt(head_sum(kk2[u] * kk2[u])), 1e-12) for u in un]
        be = [al[u] * a2[u] for u in un]
        al_t = [al[u] * e_excl[u] for u in un]
        r_t = [r2[u] * e_incl[u] for u in un]
        be_n = [be[u] * e_neg[u] for u in un]
        k_n = [k2[u] * e_neg[u] for u in un]
        k_et = [(k2[u] * e_tail[u]).T for u in un]
        be_et = [(be[u] * e_tail[u]).T for u in un]
        v_bd = [bdiag(v2[u]) for u in un]
        lhs = [jnp.concatenate([al_t[u], r_t[u]], axis=0) for u in un]
        s_b = [_dot_nt(lhs[u], bdiag(be_n[u])) for u in un]
        s_k = [_dot_nt(lhs[u], bdiag(k_n[u])) for u in un]
        l_ab = [jnp.where(stril, s_b[u][:C], 0.0) for u in un]
        a_rb = [jnp.where(tril, s_b[u][C:], 0.0) for u in un]
        l_ak = [jnp.where(stril, s_k[u][:C], 0.0) for u in un]
        a_rk = [jnp.where(tril, s_k[u][C:], 0.0) for u in un]
        lakv = [_dot(l_ak[u], v_bd[u]) for u in un]
        arkv = [_dot(a_rk[u], v_bd[u]) for u in un]
        kev = [_dot(k_et[u], v2[u]) for u in un]
        t_inv = [eye_f - jnp.where(_level_mask(ri, ci, 0), l_ab[u], 0.0) for u in un]
        lvl = 1
        while (1 << lvl) < C:
            lm = _level_mask(ri, ci, lvl)
            tn = [_dot(t_inv[u], bdiag(jnp.where(lm, l_ab[u], 0.0))) for u in un]
            t_inv = [t_inv[u] - _dot(tn[u], bdiag(t_inv[u])) for u in un]
            lvl += 1
        a_til = [_dot(t_inv[u], bdiag(al_t[u])) for u in un]
        b_til = [_dot(t_inv[u], bdiag(lakv[u])) for u in un]
        r_hat = [r_t[u] - _dot(a_rb[u], bdiag(a_til[u])) for u in un]
        o_hat = [arkv[u] - _dot(a_rb[u], bdiag(b_til[u])) for u in un]
        g_bd = [jnp.where(same_head, jnp.where(eye_b, p_last[u], 0.0) - _dot(be_et[u], a_til[u]), 0.0) for u in un]
        h_bd = [jnp.where(same_head, kev[u] - _dot(be_et[u], b_til[u]), 0.0) for u in un]
        lhs_m = [jnp.concatenate([r_hat[u], g_bd[u]], axis=0) for u in un]
        for u, (s, p) in enumerate(units):
            res = _dot(lhs_m[u], m_scr[p])
            m_scr[p] = res[C:] + h_bd[u]
            o_p = res[:C] + o_hat[u]
            cen = o_p - head_sum(o_p) * (1.0 / N)
            var = head_sum(cen * cen) * (1.0 / N)
            on_scr[sls[s], lanes[p]] = cen * lax.rsqrt(var + RW_GN_EPS)
            bon_scr[sls[s], lanes[p]] = head_sum(r2[u] * k2[u] * rk_all[:, lanes[p]]) * v2[u]
        return carry

    lax.fori_loop(0, tb // (C * n_sub), chunk_body, 0)
    out = (on_scr[...] * lnw_ref[...] + lnb_ref[...] + bon_scr[...]) * g
    o_ref[0] = out.astype(o_ref.dtype)

    @pl.when(t_idx == pl.num_programs(1) - 1)
    def _():
        for p in range(RW_HEADS // 2):
            m = m_scr[p]
            sout_ref[0, 2 * p] = m[:N, :N].T
            sout_ref[0, 2 * p + 1] = m[N:, N:].T


def _rwkv_seq(z3, shift0, s0, rw, tb):
    b, l, _ = z3.shape
    const = lambda shape: pl.BlockSpec(shape, lambda i, j: (0,) * len(shape))
    wide = lambda: pltpu.VMEM((tb, RW_WIDTH), F32)
    return pl.pallas_call(
        _rwkv_seq_kernel,
        grid=(b, l // tb),
        in_specs=[
            pl.BlockSpec((1, tb, RW_PROJ), lambda i, j: (i, j, 0)),
            pl.BlockSpec((1, 1, RW_PROJ), lambda i, j: (i, 0, 0)),
            pl.BlockSpec((1, RW_HEADS, RW_N, RW_N), lambda i, j: (i, 0, 0, 0)),
            const((1, RW_PROJ)),
            const((1, RW_WIDTH)), const((128, RW_WIDTH)),
            const((1, RW_WIDTH)), const((128, RW_WIDTH)),
            const((128, RW_WIDTH)),
            const((1, RW_WIDTH)), const((1, RW_WIDTH)), const((1, RW_WIDTH)),
            const((1, RW_WIDTH)), const((1, RW_WIDTH)),
        ],
        out_specs=[
            pl.BlockSpec((1, tb, RW_WIDTH), lambda i, j: (i, j, 0)),
            pl.BlockSpec((1, RW_HEADS, RW_N, RW_N), lambda i, j: (i, 0, 0, 0)),
        ],
        out_shape=[
            jax.ShapeDtypeStruct((b, l, RW_WIDTH), BF16),
            jax.ShapeDtypeStruct((b, RW_HEADS, RW_N, RW_N), F32),
        ],
        scratch_shapes=[
            pltpu.VMEM((RW_HEADS // 2, 2 * RW_N, 2 * RW_N), F32),
            pltpu.VMEM((1, RW_PROJ), F32),
            wide(), wide(), wide(), wide(), wide(), wide(), wide(), wide(),
        ],
        compiler_params=_params(("parallel", "arbitrary")),
    )(z3, shift0, s0, rw["mu"], rw["w0"], rw["w2p"], rw["a0"], rw["a2p"], rw["g2"], rw["k_k"], rw["k_a"],
      rw["r_k"], rw["ln_w"], rw["ln_b"])


def _rwkv_step_prep_kernel(z_ref, shift0_ref, mu_ref, w0_ref, w2_ref, a0_ref, a2_ref, g2_ref, kk_ref, ka_ref,
                           rk_ref, r_ref, k_ref, v_ref, al_ref, be_ref, dec_ref, g_ref, bon_ref):
    z = z_ref[...]
    zs = z + mu_ref[...] * (shift0_ref[...] - z)
    r, k, v, kk_raw, a, log_decay, g = _rwkv_features(
        zs, w0_ref[...], w2_ref[...], a0_ref[...], a2_ref[...], g2_ref[...], kk_ref[...], ka_ref[...])
    rk_all = rk_ref[...]
    for h in range(RW_HEADS):
        hs = slice(h * RW_N, (h + 1) * RW_N)
        kk_h = kk_raw[:, hs]
        nrm = jnp.sqrt(jnp.sum(kk_h * kk_h, axis=-1, keepdims=True))
        al = kk_h / jnp.maximum(nrm, 1e-12)
        al_ref[:, hs] = al
        be_ref[:, hs] = al * a[:, hs]
        bon_ref[:, hs] = jnp.sum(r[:, hs] * k[:, hs] * rk_all[:, hs], axis=-1, keepdims=True) * v[:, hs]
    r_ref[...] = r
    k_ref[...] = k
    v_ref[...] = v
    dec_ref[...] = jnp.exp(log_decay)
    g_ref[...] = g


def _rwkv_step_prep(z, shift0, rw):
    n = z.shape[0]
    out = jax.ShapeDtypeStruct((n, RW_WIDTH), F32)
    return pl.pallas_call(
        _rwkv_step_prep_kernel,
        out_shape=[out] * 8,
        compiler_params=pltpu.CompilerParams(vmem_limit_bytes=VMEM_LIMIT),
    )(z, shift0, rw["mu"], rw["w0"], rw["w2p"], rw["a0"], rw["a2p"], rw["g2"], rw["k_k"], rw["k_a"], rw["r_k"])


def _rwkv_step_kernel(s_ref, r_ref, k_ref, al_ref, be_ref, dec_ref, vcol_ref, snew_ref, ocol_ref):
    s = s_ref[...]
    sa = -jnp.sum(s * al_ref[...], axis=-1, keepdims=True)
    s_new = s * dec_ref[...] + sa * be_ref[...] + vcol_ref[...] * k_ref[...]
    snew_ref[...] = s_new
    ocol_ref[...] = jnp.sum(s_new * r_ref[...], axis=-1, keepdims=True)


def _rwkv_step(s0, r, k, al, be, dec, v, bb):
    n = s0.shape[0]
    rowv = lambda t: t.reshape(n, RW_HEADS, 1, RW_N)
    row_spec = pl.BlockSpec((bb, RW_HEADS, 1, RW_N), lambda i: (i, 0, 0, 0))
    col_spec = pl.BlockSpec((bb, RW_HEADS, RW_N, 1), lambda i: (i, 0, 0, 0))
    s_spec = pl.BlockSpec((bb, RW_HEADS, RW_N, RW_N), lambda i: (i, 0, 0, 0))
    s_new, o_col = pl.pallas_call(
        _rwkv_step_kernel,
        grid=(n // bb,),
        in_specs=[s_spec, row_spec, row_spec, row_spec, row_spec, row_spec, col_spec],
        out_specs=[s_spec, col_spec],
        out_shape=[
            jax.ShapeDtypeStruct(s0.shape, F32),
            jax.ShapeDtypeStruct((n, RW_HEADS, RW_N, 1), F32),
        ],
        compiler_params=_params(("parallel",)),
    )(s0, rowv(r), rowv(k), rowv(al), rowv(be), rowv(dec), v.reshape(n, RW_HEADS, RW_N, 1))
    return s_new, o_col.reshape(n, RW_WIDTH)


def _gla_features(z, gkw, gkb):
    q = z[:, 0:GLA_QK] * (GLA_DK ** -0.5)
    k = z[:, GLA_QK : 2 * GLA_QK]
    v = z[:, 2 * GLA_QK : 2 * GLA_QK + GLA_WIDTH]
    g = z[:, 2 * GLA_QK + GLA_WIDTH : 2 * GLA_QK + 2 * GLA_WIDTH]
    zgk = z[:, 2 * GLA_QK + 2 * GLA_WIDTH :]
    gk = -_softplus(-(_dot(zgk.astype(BF16), gkw) + gkb)) / GLA_GATE_NORMALIZER
    return q, k, v, g, gk


def _gla_finish(o, g, norm_w):
    outs = []
    for h in range(GLA_HEADS):
        hs = slice(h * GLA_DV, (h + 1) * GLA_DV)
        o_h = o[:, hs]
        o_h = o_h * lax.rsqrt(jnp.mean(o_h * o_h, axis=-1, keepdims=True) + NORM_EPS) * norm_w
        g_h = g[:, hs]
        outs.append(o_h * (g_h * _sigmoid(g_h)))
    return jnp.concatenate(outs, axis=-1)


def _gla_seq_kernel(z_ref, s0_ref, gkw_ref, gkb_ref, nw_ref, wsel_ref, o_ref, sout_ref,
                    st_scr, x_scr, q_scr, k_scr, v_scr, gc_scr, oi_scr):
    C = GLA_CHUNK
    G = 128
    t_idx = pl.program_id(1)
    tb = z_ref.shape[1]
    nc = tb // C

    @pl.when(t_idx == 0)
    def _():
        for h in range(GLA_HEADS):
            st_scr[h] = s0_ref[0, h].T

    q, k, v, g, gk = _gla_features(z_ref[0], gkw_ref[...], gkb_ref[...])
    ri = lax.broadcasted_iota(jnp.int32, (G, G), 0)
    ci = lax.broadcasted_iota(jnp.int32, (G, G), 1)
    cum_mat = ((ri // C == ci // C) & (ri >= ci)).astype(F32)
    for m in range(tb // G):
        rows = slice(m * G, (m + 1) * G)
        gc_scr[rows, :] = _dot(cum_mat, gk[rows, :], precision=HIGHEST)
    q_scr[...] = q
    k_scr[...] = k
    v_scr[...] = v
    gcum = gc_scr[...]

    rg = lax.broadcasted_iota(jnp.int32, (tb, 2 * G), 0)
    cg = lax.broadcasted_iota(jnp.int32, (tb, 2 * G), 1)
    blk_mask = ((cg % G) // C == (rg % G) // C) & (cg % C <= rg % C)
    for p in range(GLA_HEADS // 2):
        ls = slice(p * 128, (p + 1) * 128)
        q3 = q[:, ls].reshape(nc, C, 128)
        k3 = k[:, ls].reshape(nc, C, 128)
        g3 = gcum[:, ls].reshape(nc, C, 128)
        for j in range(C):
            e = q3 * jnp.exp(jnp.minimum(g3 - g3[:, j : j + 1, :], 0.0)) * k3[:, j : j + 1, :]
            x_scr[:, j * 128 : (j + 1) * 128] = e.reshape(tb, 128).astype(BF16)
        a_t = jnp.where(blk_mask, _dot(x_scr[...], wsel_ref[...]), 0.0).astype(BF16)
        for hl in range(2):
            h = 2 * p + hl
            for m in range(tb // G):
                rows = slice(m * G, (m + 1) * G)
                a_blk = a_t[rows, hl * G : (hl + 1) * G]
                oi_scr[rows, h * GLA_DV : (h + 1) * GLA_DV] = _dot(
                    a_blk, v[rows, h * GLA_DV : (h + 1) * GLA_DV].astype(BF16))

    def chunk_body(c, carry):
        sl = pl.ds(pl.multiple_of(c * C, C), C)
        g_c = gc_scr[sl, :]
        q_c = q_scr[sl, :]
        k_c = k_scr[sl, :]
        v_c = v_scr[sl, :]
        for h in range(GLA_HEADS):
            ks = slice(h * GLA_DK, (h + 1) * GLA_DK)
            vs = slice(h * GLA_DV, (h + 1) * GLA_DV)
            g_h = g_c[:, ks]
            g_last = g_h[C - 1 : C, :]
            st = st_scr[h]
            oi_scr[sl, vs] += _dot_nt(q_c[:, ks] * jnp.exp(g_h), st)
            k_dec = k_c[:, ks] * jnp.exp(g_last - g_h)
            st_scr[h] = st * jnp.exp(g_last) + _dot_tn(v_c[:, vs], k_dec)
        return carry

    lax.fori_loop(0, nc, chunk_body, 0, unroll=8)
    o_ref[0] = _gla_finish(oi_scr[...], g, nw_ref[...]).astype(o_ref.dtype)

    @pl.when(t_idx == pl.num_programs(1) - 1)
    def _():
        for h in range(GLA_HEADS):
            sout_ref[0, h] = st_scr[h].T


def _gla_select_matrix():
    j = jnp.arange(GLA_CHUNK)[:, None, None]
    hl = jnp.arange(2)[None, :, None]
    rows_j = jnp.broadcast_to(j, (GLA_CHUNK, 2, GLA_DK)).reshape(-1)
    rows_h = jnp.broadcast_to(hl, (GLA_CHUNK, 2, GLA_DK)).reshape(-1)
    cols = jnp.arange(256)
    sel = (rows_j[:, None] == cols[None, :] % GLA_CHUNK) & (rows_h[:, None] == cols[None, :] // 128)
    return sel.astype(BF16)


def _gla_seq(z3, s0, gl, tb):
    b, l, _ = z3.shape
    const = lambda shape: pl.BlockSpec(shape, lambda i, j: (0,) * len(shape))
    return pl.pallas_call(
        _gla_seq_kernel,
        grid=(b, l // tb),
        in_specs=[
            pl.BlockSpec((1, tb, GLA_PROJ_PAD), lambda i, j: (i, j, 0)),
            pl.BlockSpec((1, GLA_HEADS, GLA_DK, GLA_DV), lambda i, j: (i, 0, 0, 0)),
            const((GLA_LORA_PAD, GLA_QK)), const((1, GLA_QK)), const((1, GLA_DV)),
            const((GLA_CHUNK * 128, 256)),
        ],
        out_specs=[
            pl.BlockSpec((1, tb, GLA_WIDTH), lambda i, j: (i, j, 0)),
            pl.BlockSpec((1, GLA_HEADS, GLA_DK, GLA_DV), lambda i, j: (i, 0, 0, 0)),
        ],
        out_shape=[
            jax.ShapeDtypeStruct((b, l, GLA_WIDTH), BF16),
            jax.ShapeDtypeStruct((b, GLA_HEADS, GLA_DK, GLA_DV), F32),
        ],
        scratch_shapes=[
            pltpu.VMEM((GLA_HEADS, GLA_DV, GLA_DK), F32),
            pltpu.VMEM((tb, GLA_CHUNK * 128), BF16),
            pltpu.VMEM((tb, GLA_QK), F32), pltpu.VMEM((tb, GLA_QK), F32), pltpu.VMEM((tb, GLA_WIDTH), F32),
            pltpu.VMEM((tb, GLA_QK), F32), pltpu.VMEM((tb, GLA_WIDTH), F32),
        ],
        compiler_params=_params(("parallel", "arbitrary")),
    )(z3, s0, gl["gkw"], gl["gkb"], gl["norm_w"], _gla_select_matrix())


def _gla_step_prep_kernel(z_ref, gkw_ref, gkb_ref, q_ref, k_ref, v_ref, g_ref, dec_ref):
    q, k, v, g, gk = _gla_features(z_ref[...], gkw_ref[...], gkb_ref[...])
    q_ref[...] = q
    k_ref[...] = k
    v_ref[...] = v
    g_ref[...] = g
    dec_ref[...] = jnp.exp(gk)


def _gla_step_prep(z, gl):
    n = z.shape[0]
    qk = jax.ShapeDtypeStruct((n, GLA_QK), F32)
    wide = jax.ShapeDtypeStruct((n, GLA_WIDTH), F32)
    return pl.pallas_call(
        _gla_step_prep_kernel,
        out_shape=[qk, qk, wide, wide, qk],
        compiler_params=pltpu.CompilerParams(vmem_limit_bytes=VMEM_LIMIT),
    )(z, gl["gkw"], gl["gkb"])


def _gla_step_kernel(s_ref, qcol_ref, kcol_ref, dcol_ref, vrow_ref, snew_ref, orow_ref):
    s_new = s_ref[...] * dcol_ref[...] + kcol_ref[...] * vrow_ref[...]
    snew_ref[...] = s_new
    orow_ref[...] = jnp.sum(s_new * qcol_ref[...], axis=2, keepdims=True)


def _gla_step(s0, q, k, dec, v, bb):
    n = s0.shape[0]
    colv = lambda t: t.reshape(n, GLA_HEADS, GLA_DK, 1)
    col_spec = pl.BlockSpec((bb, GLA_HEADS, GLA_DK, 1), lambda i: (i, 0, 0, 0))
    row_spec = pl.BlockSpec((bb, GLA_HEADS, 1, GLA_DV), lambda i: (i, 0, 0, 0))
    s_spec = pl.BlockSpec((bb, GLA_HEADS, GLA_DK, GLA_DV), lambda i: (i, 0, 0, 0))
    s_new, o_row = pl.pallas_call(
        _gla_step_kernel,
        grid=(n // bb,),
        in_specs=[s_spec, col_spec, col_spec, col_spec, row_spec],
        out_specs=[s_spec, row_spec],
        out_shape=[
            jax.ShapeDtypeStruct(s0.shape, F32),
            jax.ShapeDtypeStruct((n, GLA_HEADS, 1, GLA_DV), F32),
        ],
        compiler_params=_params(("parallel",)),
    )(s0, colv(q), colv(k), colv(dec), v.reshape(n, GLA_HEADS, 1, GLA_DV))
    return s_new, o_row.reshape(n, GLA_WIDTH)


def _step_post_kernel(orw_ref, bon_ref, grw_ref, lnw_ref, lnb_ref, ogl_ref, ggl_ref, nw_ref, o_rw_ref, o_gl_ref):
    o = orw_ref[...]
    for h in range(RW_HEADS):
        hs = slice(h * RW_N, (h + 1) * RW_N)
        o_h = o[:, hs]
        mean = jnp.mean(o_h, axis=-1, keepdims=True)
        cen = o_h - mean
        var = jnp.mean(cen * cen, axis=-1, keepdims=True)
        on = cen * lax.rsqrt(var + RW_GN_EPS)
        res = (on * lnw_ref[:, hs] + lnb_ref[:, hs] + bon_ref[:, hs]) * grw_ref[:, hs]
        o_rw_ref[:, hs] = res.astype(o_rw_ref.dtype)
    o_gl_ref[...] = _gla_finish(ogl_ref[...], ggl_ref[...], nw_ref[...]).astype(o_gl_ref.dtype)


def _step_post(o_rw, bonus, g_rw, rw, o_gl, g_gl, gl):
    n = o_rw.shape[0]
    return pl.pallas_call(
        _step_post_kernel,
        out_shape=[jax.ShapeDtypeStruct((n, RW_WIDTH), BF16), jax.ShapeDtypeStruct((n, GLA_WIDTH), BF16)],
        compiler_params=pltpu.CompilerParams(vmem_limit_bytes=VMEM_LIMIT),
    )(o_rw, bonus, g_rw, rw["ln_w"], rw["ln_b"], o_gl, g_gl, gl["norm_w"])


def _outproj_router_kernel(x_ref, orw_ref, ogl_ref, wo_ref, gain_ref, wrt_ref, br_ref,
                           h_ref, xn_ref, idx_ref, gate_ref, rank_ref, cnt_ref):
    tm = x_ref.shape[0]
    mix = jnp.concatenate([orw_ref[...], ogl_ref[...]], axis=-1)
    h = x_ref[...] + _dot(mix, wo_ref[...])
    h_ref[...] = h
    xn = h * lax.rsqrt(jnp.mean(h * h, axis=-1, keepdims=True) + NORM_EPS) * gain_ref[...]
    _store_row_tiles(xn_ref, xn)
    logits = _dot_nt(wrt_ref[...], xn, precision=HIGHEST) + br_ref[...]
    eidx = lax.broadcasted_iota(jnp.int32, logits.shape, 0)
    ti = lax.broadcasted_iota(jnp.int32, (tm, tm), 0)
    tj = lax.broadcasted_iota(jnp.int32, (tm, tm), 1)
    before = (ti < tj).astype(BF16)
    vals, idxs = [], []
    work = logits
    chosen = jnp.zeros(logits.shape, F32)
    for _ in range(TOP_K):
        m = jnp.max(work, axis=0, keepdims=True)
        sel = jnp.min(jnp.where(work == m, eidx, N_EXPERTS), axis=0, keepdims=True)
        hit = eidx == sel
        work = jnp.where(hit, -jnp.inf, work)
        chosen = chosen + hit.astype(F32)
        vals.append(m)
        idxs.append(sel)
    prefix = _dot(chosen.astype(BF16), before)
    exps = [jnp.exp(v - vals[0]) for v in vals]
    denom = exps[0] + exps[1] + exps[2] + exps[3]
    for j in range(TOP_K):
        idx_ref[0, j : j + 1, :] = idxs[j]
        gate_ref[0, j : j + 1, :] = exps[j] / denom
        rank = jnp.sum(jnp.where(eidx == idxs[j], prefix, 0.0), axis=0, keepdims=True)
        rank_ref[0, j : j + 1, :] = rank.astype(jnp.int32)
    cnt = jnp.sum(chosen, axis=1, keepdims=True)
    cnt_ref[0] = jnp.broadcast_to(cnt, (N_EXPERTS, 128)).astype(jnp.int32)


def _outproj_router(x, o_rw, o_gl, w_out, gain, w_router_t, b_router, tm):
    n = x.shape[0]
    nt = n // tm
    const = lambda shape: pl.BlockSpec(shape, lambda i: (0,) * len(shape))
    tok = lambda width: pl.BlockSpec((tm, width), lambda i: (i, 0))
    lane = pl.BlockSpec((1, TOP_K, tm), lambda i: (i, 0, 0))
    return pl.pallas_call(
        _outproj_router_kernel,
        grid=(nt,),
        in_specs=[
            tok(D_MODEL), tok(RW_WIDTH), tok(GLA_WIDTH),
            const((D_MODEL, D_MODEL)), const((1, D_MODEL)), const((N_EXPERTS, D_MODEL)), const((N_EXPERTS, 1)),
        ],
        out_specs=[tok(D_MODEL), pl.BlockSpec((tm * ROW_TILE, 128), lambda i: (i, 0)), lane, lane, lane,
                   pl.BlockSpec((1, N_EXPERTS, 128), lambda i: (i, 0, 0))],
        out_shape=[
            jax.ShapeDtypeStruct((n, D_MODEL), F32),
            jax.ShapeDtypeStruct((n * ROW_TILE, 128), F32),
            jax.ShapeDtypeStruct((nt, TOP_K, tm), jnp.int32),
            jax.ShapeDtypeStruct((nt, TOP_K, tm), F32),
            jax.ShapeDtypeStruct((nt, TOP_K, tm), jnp.int32),
            jax.ShapeDtypeStruct((nt, N_EXPERTS, 128), jnp.int32),
        ],
        compiler_params=_params(("parallel",)),
    )(x, o_rw, o_gl, w_out, gain, w_router_t, b_router)


def _moe_kernel(be_ref, nu_ref, xs_ref, wg_ref, wu_ref, wd_ref, bg_ref, bu_ref, bd_ref, y_ref, wg_b, wu_b, wd_b):
    b = pl.program_id(0)
    prev = be_ref[jnp.maximum(b - 1, 0)]
    new_expert = (b == 0) | (be_ref[b] != prev)

    @pl.when(new_expert)
    def _():
        wg_b[...] = wg_ref[0].astype(BF16)
        wu_b[...] = wu_ref[0].astype(BF16)
        wd_b[...] = wd_ref[0].astype(BF16)

    @pl.when(b < nu_ref[0])
    def _():
        x = _load_row_tiles(xs_ref, MOE_BLOCK).astype(BF16)
        half = D_MODEL // 2
        acc = None
        for f in range(2):
            fs = slice(f * half, (f + 1) * half)
            gt = _dot(x, wg_b[:, fs]) + bg_ref[0, :, fs]
            up = _dot(x, wu_b[:, fs]) + bu_ref[0, :, fs]
            gt = jnp.minimum(gt, SWIGLU_LIMIT)
            up = jnp.clip(up, -SWIGLU_LIMIT, SWIGLU_LIMIT)
            hid = (up + 1.0) * gt * _sigmoid(SWIGLU_ALPHA * gt)
            part = _dot(hid.astype(BF16), wd_b[fs, :])
            acc = part if acc is None else acc + part
        _store_row_tiles(y_ref, acc + bd_ref[0])


def _moe_ffn(block_expert, n_used, xs, w_gate, w_up, w_down, b_gate, b_up, b_down):
    n_blocks = block_expert.shape[0]
    row = lambda b, be, nu: (jnp.minimum(b, nu[0] - 1), 0)
    wsel = lambda b, be, nu: (be[b], 0, 0)
    wspec = pl.BlockSpec((1, D_MODEL, D_MODEL), wsel)
    bspec = pl.BlockSpec((1, 1, D_MODEL), wsel)
    grid_spec = pltpu.PrefetchScalarGridSpec(
        num_scalar_prefetch=2,
        grid=(n_blocks,),
        in_specs=[pl.BlockSpec((MOE_BLOCK * ROW_TILE, 128), row), wspec, wspec, wspec, bspec, bspec, bspec],
        out_specs=pl.BlockSpec((MOE_BLOCK * ROW_TILE, 128), row),
        scratch_shapes=[pltpu.VMEM((D_MODEL, D_MODEL), BF16)] * 3,
    )
    return pl.pallas_call(
        _moe_kernel,
        grid_spec=grid_spec,
        out_shape=jax.ShapeDtypeStruct((n_blocks * MOE_BLOCK * ROW_TILE, 128), F32),
        compiler_params=_params(("arbitrary",)),
    )(block_expert, n_used, xs, w_gate, w_up, w_down,
      b_gate.reshape(N_EXPERTS, 1, D_MODEL), b_up.reshape(N_EXPERTS, 1, D_MODEL),
      b_down.reshape(N_EXPERTS, 1, D_MODEL))


def _combine_kernel(slot_c, slot_n, h_ref, gate_ref, gain_ref, y_hbm, o_ref, ybuf, sems):
    i = pl.program_id(0)
    nt = pl.num_programs(0)
    tm = h_ref.shape[0]

    def row_copy(src, slot, j, t):
        dst = pl.multiple_of(t * ROW_TILE, ROW_TILE)
        return pltpu.make_async_copy(y_hbm.at[pl.ds(src, ROW_TILE)], ybuf.at[slot, j, pl.ds(dst, ROW_TILE)],
                                     sems.at[slot])

    def issue(slot_ref, slot):
        def body(t, carry):
            for j in range(TOP_K):
                row_copy(pl.multiple_of(slot_ref[0, j, t], ROW_TILE), slot, j, t).start()
            return carry
        lax.fori_loop(0, tm, body, 0, unroll=8)

    @pl.when(i == 0)
    def _():
        issue(slot_c, 0)

    @pl.when(i + 1 < nt)
    def _():
        issue(slot_n, (i + 1) % 2)

    slot = i % 2
    for j in range(TOP_K):
        pltpu.make_async_copy(y_hbm.at[pl.ds(0, tm * ROW_TILE)], ybuf.at[slot, j], sems.at[slot]).wait()
    f = h_ref[...]
    gates = gate_ref[...]
    for j in range(TOP_K):
        f = f + gates[:, j : j + 1] * _load_row_tiles(ybuf.at[slot, j], tm)
    o_ref[...] = f * lax.rsqrt(jnp.mean(f * f, axis=-1, keepdims=True) + NORM_EPS) * gain_ref[...]


def _combine(h, y_rows, slot3, gates, gain, tm):
    n = h.shape[0]
    nt = n // tm
    smem = lambda shape, imap: pl.BlockSpec(shape, imap, memory_space=pltpu.SMEM)
    cur = lambda i: (i, 0, 0)
    nxt = lambda i: (jnp.minimum(i + 1, nt - 1), 0, 0)
    return pl.pallas_call(
        _combine_kernel,
        grid=(nt,),
        in_specs=[
            smem((1, TOP_K, tm), cur), smem((1, TOP_K, tm), nxt),
            pl.BlockSpec((tm, D_MODEL), lambda i: (i, 0)),
            pl.BlockSpec((tm, TOP_K), lambda i: (i, 0)),
            pl.BlockSpec((1, D_MODEL), lambda i: (0, 0)),
            pl.BlockSpec(memory_space=pl.ANY),
        ],
        out_specs=pl.BlockSpec((tm, D_MODEL), lambda i: (i, 0)),
        out_shape=jax.ShapeDtypeStruct((n, D_MODEL), F32),
        scratch_shapes=[pltpu.VMEM((2, TOP_K, tm * ROW_TILE, 128), F32), pltpu.SemaphoreType.DMA((2,))],
        compiler_params=pltpu.CompilerParams(dimension_semantics=("arbitrary",), vmem_limit_bytes=VMEM_LIMIT,
                                             disable_bounds_checks=True),
    )(slot3, slot3, h, gates, gain, y_rows)


def _dispatch_kernel(*refs, fill):
    if fill:
        slot_ref, ends_ref, x_ref, xs_hbm, zero_scr, sem, zsem = refs
    else:
        slot_ref, x_ref, _, xs_hbm, sem = refs
    tm = x_ref.shape[0] // ROW_TILE
    blk = MOE_BLOCK * ROW_TILE

    if fill:
        def fill_copy(e):
            start = pl.multiple_of((ends_ref[0, e] - MOE_BLOCK) * ROW_TILE, blk)
            return pltpu.make_async_copy(zero_scr, xs_hbm.at[pl.ds(start, blk)], zsem)

        @pl.when(pl.program_id(0) == 0)
        def _():
            zero_scr[...] = jnp.zeros(zero_scr.shape, zero_scr.dtype)
            for e in range(N_EXPERTS):
                @pl.when(ends_ref[1, e] > 0)
                def _():
                    fill_copy(e).start()
            for e in range(N_EXPERTS):
                @pl.when(ends_ref[1, e] > 0)
                def _():
                    fill_copy(e).wait()

    def body(t, carry):
        src = x_ref.at[pl.ds(pl.multiple_of(t * ROW_TILE, ROW_TILE), ROW_TILE)]
        for j in range(TOP_K):
            dst = pl.multiple_of(slot_ref[0, j, t], ROW_TILE)
            pltpu.make_async_copy(src, xs_hbm.at[pl.ds(dst, ROW_TILE)], sem).start()
        return carry

    lax.fori_loop(0, tm, body, 0, unroll=8)
    for j in range(TOP_K):
        pltpu.make_async_copy(x_ref, xs_hbm.at[pl.ds(0, tm * ROW_TILE)], sem).wait()


def _dispatch(xn, slot3, tm, n_slots, ends=None, xs=None):
    n = xn.shape[0] // ROW_TILE
    fill = xs is None
    smem = lambda shape, imap: pl.BlockSpec(shape, imap, memory_space=pltpu.SMEM)
    in_specs = [smem((1, TOP_K, tm), lambda i: (i, 0, 0))]
    args = [slot3]
    scratch = []
    if fill:
        in_specs.append(smem((2, N_EXPERTS), lambda i: (0, 0)))
        args.append(ends)
        scratch = [pltpu.VMEM((MOE_BLOCK * ROW_TILE, 128), F32)]
    in_specs.append(pl.BlockSpec((tm * ROW_TILE, 128), lambda i: (i, 0)))
    args.append(xn)
    aliases = {}
    if not fill:
        in_specs.append(pl.BlockSpec(memory_space=pl.ANY))
        args.append(xs)
        aliases = {len(args) - 1: 0}
    scratch.append(pltpu.SemaphoreType.DMA(()))
    if fill:
        scratch.append(pltpu.SemaphoreType.DMA(()))
    return pl.pallas_call(
        functools.partial(_dispatch_kernel, fill=fill),
        grid=(n // tm,),
        in_specs=in_specs,
        out_specs=pl.BlockSpec(memory_space=pl.ANY),
        out_shape=jax.ShapeDtypeStruct((n_slots * ROW_TILE, 128), F32),
        scratch_shapes=scratch,
        input_output_aliases=aliases,
        compiler_params=pltpu.CompilerParams(dimension_semantics=("arbitrary",), vmem_limit_bytes=VMEM_LIMIT,
                                             disable_bounds_checks=True, has_side_effects=True),
    )(*args)


def _pad_rows(w, rows, offset):
    out = jnp.zeros((rows, w.shape[1]), w.dtype)
    return out.at[offset : offset + w.shape[0]].set(w)


def _routing_tables(counts, n_pairs):
    n_blocks = (n_pairs + N_EXPERTS * (MOE_BLOCK - 1) + MOE_BLOCK - 1) // MOE_BLOCK
    total = jnp.sum(counts, axis=0)
    padded = (total + MOE_BLOCK - 1) // MOE_BLOCK * MOE_BLOCK
    pends = jnp.cumsum(padded)
    pstarts = pends - padded
    tile_base = pstarts[None, :] + jnp.cumsum(counts, axis=0) - counts
    blocks = jnp.arange(n_blocks, dtype=jnp.int32) * MOE_BLOCK
    n_used = (pends[-1] // MOE_BLOCK).astype(jnp.int32)
    owner = jnp.sum((pends[None, :] <= blocks[:, None]).astype(jnp.int32), axis=1)
    block_expert = jnp.minimum(owner, N_EXPERTS - 1)
    last = jnp.sum(jnp.where(jnp.arange(n_blocks) == n_used - 1, block_expert, 0))
    block_expert = jnp.where(jnp.arange(n_blocks) < n_used, block_expert, last)
    ends = jnp.stack([pends, padded]).astype(jnp.int32)
    return tile_base.astype(jnp.int32)[:, None, :], ends, block_expert.astype(jnp.int32), n_used.reshape(1), n_blocks


def kernel(x_prompt, x_sample, state_rwkv_shift, state_rwkv_wkv, state_gla, norm_mix, w_in, rw_mu, rw_w0, rw_w2, rw_a0, rw_a2, rw_g2, rw_k_k, rw_k_a, rw_r_k, rw_ln_w, rw_ln_b, gla_gk_w2, gla_gk_b, gla_norm_w, w_out, norm_ffn, w_router, b_router, w_gate, b_gate, w_up, b_up, w_down, b_down, norm_final):
    depth = norm_mix.shape[0]
    assert depth == 1
    bp, lp, d = x_prompt.shape
    bs, ls, _ = x_sample.shape
    assert ls == 1 and lp % SEQ_BLOCK == 0
    l = 0
    row = lambda t: t.reshape(1, -1)

    w_in_b = w_in[l].astype(BF16)
    w_in_r = w_in_b[:, :RW_PROJ]
    w_in_g = jnp.pad(w_in_b[:, RW_PROJ:], ((0, 0), (0, GLA_PROJ_PAD - GLA_PROJ)))
    rw = dict(
        mu=row(rw_mu[l]), w0=row(rw_w0[l]), a0=row(rw_a0[l]),
        w2p=_pad_rows(rw_w2[l].astype(BF16), 128, 0), a2p=_pad_rows(rw_a2[l].astype(BF16), 128, 64),
        g2=rw_g2[l].astype(BF16), k_k=row(rw_k_k[l]), k_a=row(rw_k_a[l]), r_k=row(rw_r_k[l]),
        ln_w=row(rw_ln_w[l]), ln_b=row(rw_ln_b[l]))
    gl = dict(gkw=_pad_rows(gla_gk_w2[l].astype(BF16), GLA_LORA_PAD, 0), gkb=row(gla_gk_b[l]),
              norm_w=row(gla_norm_w[l]))
    gain_mix = row(norm_mix[l])

    n_p = bp * lp
    xp = x_prompt.reshape(n_p, d)
    zr_p, zg_p = _inproj(xp, gain_mix, w_in_r, w_in_g, TOK_BLOCK)
    zr_p3 = zr_p.reshape(bp, lp, RW_PROJ)
    o_rw_p, wkv_p = _rwkv_seq(zr_p3, jnp.zeros((bp, 1, RW_PROJ), F32),
                              jnp.zeros((bp, RW_HEADS, RW_N, RW_N), F32), rw, SEQ_BLOCK)
    o_gl_p, gla_p = _gla_seq(zg_p.reshape(bp, lp, GLA_PROJ_PAD),
                             jnp.zeros((bp, GLA_HEADS, GLA_DK, GLA_DV), F32), gl, SEQ_BLOCK)
    shift_p = zr_p3[:, -1, :]

    xs_ = x_sample.reshape(bs, d)
    zr_s, zg_s = _inproj(xs_, gain_mix, w_in_r, w_in_g, bs)
    r, k, v, al, be, dec, g_rw, bonus = _rwkv_step_prep(zr_s, state_rwkv_shift[l], rw)
    wkv_s, o_rw_s = _rwkv_step(state_rwkv_wkv[l], r, k, al, be, dec, v, 16)
    q, kg, vg, g_gl, dec_g = _gla_step_prep(zg_s, gl)
    gla_s, o_gl_s = _gla_step(state_gla[l], q, kg, dec_g, vg, 16)
    o_rw_s2, o_gl_s2 = _step_post(o_rw_s, bonus, g_rw, rw, o_gl_s, g_gl, gl)
    shift_s = zr_s

    w_out_b = w_out[l].astype(BF16)
    router = (w_out_b, row(norm_ffn[l]), w_router[l].T, b_router[l].reshape(N_EXPERTS, 1))
    h_p, xn_p, idx_p, gate_p, rank_p, cnt_p = _outproj_router(
        xp, o_rw_p.reshape(n_p, RW_WIDTH), o_gl_p.reshape(n_p, GLA_WIDTH), *router, TOK_BLOCK)
    h_s, xn_s, idx_s, gate_s, rank_s, cnt_s = _outproj_router(xs_, o_rw_s2, o_gl_s2, *router, bs)
    nt_p = n_p // TOK_BLOCK
    counts = jnp.concatenate([cnt_p[:, :, 0], cnt_s[:, :, 0]], axis=0)
    base3, ends, block_expert, n_used, n_blocks = _routing_tables(counts, (n_p + bs) * TOP_K)
    n_slots = n_blocks * MOE_BLOCK

    def slots(idx3, rank3, base):
        hit = idx3[..., None] == jnp.arange(N_EXPERTS, dtype=jnp.int32)
        return (rank3 + jnp.sum(jnp.where(hit, base[:, :, None, :], 0), axis=-1)) * ROW_TILE

    slot_p = slots(idx_p, rank_p, base3[:nt_p])
    slot_s = slots(idx_s, rank_s, base3[nt_p:])
    xs_rows = _dispatch(xn_p, slot_p, TOK_BLOCK, n_slots, ends=ends)
    xs_rows = _dispatch(xn_s, slot_s, bs, n_slots, xs=xs_rows)
    y_rows = _moe_ffn(block_expert, n_used, xs_rows, w_gate[l], w_up[l], w_down[l], b_gate[l], b_up[l], b_down[l])
    gates = lambda g3: jnp.swapaxes(g3, 1, 2).reshape(-1, TOP_K)
    gain_f = row(norm_final)
    y_p = _combine(h_p, y_rows, slot_p, gates(gate_p), gain_f, TOK_BLOCK)
    y_s = _combine(h_s, y_rows, slot_s, gates(gate_s), gain_f, bs)

    y_prompt = y_p.reshape(bp, lp, d)
    y_sample = y_s.reshape(bs, ls, d)
    return (y_prompt, y_sample, shift_p[None], wkv_p[None], gla_p[None], shift_s[None], wkv_s[None], gla_s[None])
```

```python
import functools

import jax
import jax.numpy as jnp
from jax import lax
from jax.experimental import pallas as pl
from jax.experimental.pallas import tpu as pltpu

F32 = jnp.float32
BF16 = jnp.bfloat16
HIGHEST = lax.Precision.HIGHEST

D_MODEL = 1024
RW_WIDTH = 512
RW_HEADS = 8
RW_N = 64
RW_PROJ = 1792
RW_GN_EPS = 64e-5
GLA_HEADS = 4
GLA_DK = 64
GLA_DV = 128
GLA_WIDTH = 512
GLA_QK = GLA_HEADS * GLA_DK
GLA_PROJ = 1552
GLA_PROJ_PAD = 1664
GLA_LORA_PAD = 128
GLA_GATE_NORMALIZER = 16.0
N_EXPERTS = 32
TOP_K = 4
SWIGLU_LIMIT = 7.0
SWIGLU_ALPHA = 1.702
NORM_EPS = 1e-5

RW_CHUNK = 64
GLA_CHUNK = 16
SEQ_BLOCK = 512
TOK_BLOCK = 512
MOE_BLOCK = 512
VMEM_LIMIT = 56 * 1024 * 1024


def _dot(a, b, precision=None):
    return jnp.dot(a, b, preferred_element_type=F32, precision=precision)


def _dot_nt(a, b, precision=None):
    return lax.dot_general(a, b, (((1,), (1,)), ((), ())), preferred_element_type=F32, precision=precision)


def _dot_tn(a, b, precision=None):
    return lax.dot_general(a, b, (((0,), (0,)), ((), ())), preferred_element_type=F32, precision=precision)


def _sigmoid(x):
    return 1.0 / (1.0 + jnp.exp(-x))


def _softplus(x):
    return jnp.maximum(x, 0.0) + jnp.log(1.0 + jnp.exp(-jnp.abs(x)))


def _params(sem):
    return pltpu.CompilerParams(dimension_semantics=sem, vmem_limit_bytes=VMEM_LIMIT)


ROW_TILE = D_MODEL // 128


def _store_row_tiles(ref, x):
    m = x.shape[0]
    for c in range(ROW_TILE):
        ref[pl.ds(c, m, stride=ROW_TILE), :] = x[:, c * 128 : (c + 1) * 128]


def _load_row_tiles(ref, m):
    return jnp.concatenate([ref[pl.ds(c, m, stride=ROW_TILE), :] for c in range(ROW_TILE)], axis=-1)


def _inproj_kernel(x_ref, gain_ref, wr_ref, wg_ref, zr_ref, zg_ref):
    x = x_ref[...]
    xn = x * lax.rsqrt(jnp.mean(x * x, axis=-1, keepdims=True) + NORM_EPS) * gain_ref[...]
    xb = xn.astype(BF16)
    zr_ref[...] = _dot(xb, wr_ref[...])
    zg_ref[...] = _dot(xb, wg_ref[...])


def _inproj(x, gain, w_r, w_g, tm):
    n = x.shape[0]
    return pl.pallas_call(
        _inproj_kernel,
        grid=(n // tm,),
        in_specs=[
            pl.BlockSpec((tm, D_MODEL), lambda i: (i, 0)),
            pl.BlockSpec((1, D_MODEL), lambda i: (0, 0)),
            pl.BlockSpec((D_MODEL, RW_PROJ), lambda i: (0, 0)),
            pl.BlockSpec((D_MODEL, GLA_PROJ_PAD), lambda i: (0, 0)),
        ],
        out_specs=[
            pl.BlockSpec((tm, RW_PROJ), lambda i: (i, 0)),
            pl.BlockSpec((tm, GLA_PROJ_PAD), lambda i: (i, 0)),
        ],
        out_shape=[
            jax.ShapeDtypeStruct((n, RW_PROJ), F32),
            jax.ShapeDtypeStruct((n, GLA_PROJ_PAD), F32),
        ],
        compiler_params=_params(("parallel",)),
    )(x, gain, w_r, w_g)


def _rwkv_features(zs, w0, w2p, a0, a2p, g2, k_k, k_a):
    W = RW_WIDTH
    r = zs[:, 0:W]
    k_raw = zs[:, W : 2 * W]
    v = zs[:, 2 * W : 3 * W]
    zwa = zs[:, 3 * W : 3 * W + 128]
    zg = zs[:, 3 * W + 128 :]
    w = -_softplus(-(w0 + _dot(jnp.tanh(zwa).astype(BF16), w2p))) - 0.5
    log_decay = -jnp.exp(w)
    a = _sigmoid(a0 + _dot(zwa.astype(BF16), a2p))
    g = _dot(_sigmoid(zg).astype(BF16), g2)
    kk_raw = k_raw * k_k
    k = k_raw * (1.0 + (a - 1.0) * k_a)
    return r, k, v, kk_raw, a, log_decay, g


def _level_mask(ri, ci, lvl):
    same = (ri >> (lvl + 1)) == (ci >> (lvl + 1))
    return same & (((ri >> lvl) & 1) == 1) & (((ci >> lvl) & 1) == 0)


def _rwkv_seq_kernel(z_ref, shift0_ref, s0_ref, mu_ref, w0_ref, w2_ref, a0_ref, a2_ref, g2_ref, kk_ref, ka_ref,
                     rk_ref, lnw_ref, lnb_ref, o_ref, sout_ref,
                     m_scr, prev_scr, r_scr, k_scr, v_scr, kkr_scr, a_scr, lw_scr, on_scr, bon_scr):
    C = RW_CHUNK
    N = RW_N
    t_idx = pl.program_id(1)
    tb = z_ref.shape[1]
    zero_nn = jnp.zeros((N, N), F32)

    @pl.when(t_idx == 0)
    def _():
        prev_scr[...] = shift0_ref[0]
        for p in range(RW_HEADS // 2):
            top = jnp.concatenate([s0_ref[0, 2 * p].T, zero_nn], axis=1)
            bot = jnp.concatenate([zero_nn, s0_ref[0, 2 * p + 1].T], axis=1)
            m_scr[p] = jnp.concatenate([top, bot], axis=0)

    z = z_ref[0]
    row = lax.broadcasted_iota(jnp.int32, z.shape, 0)
    z_prev = jnp.where(row == 0, prev_scr[...], pltpu.roll(z, 1, axis=0))
    prev_scr[...] = z[tb - 1 : tb, :]
    zs = z + mu_ref[...] * (z_prev - z)
    r, k, v, kk_raw, a, log_decay, g = _rwkv_features(
        zs, w0_ref[...], w2_ref[...], a0_ref[...], a2_ref[...], g2_ref[...], kk_ref[...], ka_ref[...])
    r_scr[...] = r
    k_scr[...] = k
    v_scr[...] = v
    kkr_scr[...] = kk_raw
    a_scr[...] = a
    lw_scr[...] = log_decay

    P2 = 2 * N
    ri = lax.broadcasted_iota(jnp.int32, (C, P2), 0)
    ci = lax.broadcasted_iota(jnp.int32, (C, P2), 1) % N
    left = lax.broadcasted_iota(jnp.int32, (C, P2), 1) < N
    tril = ri >= ci
    stril = ri > ci
    eye_f = (ri == ci).astype(F32)
    rb = lax.broadcasted_iota(jnp.int32, (P2, P2), 0)
    cb = lax.broadcasted_iota(jnp.int32, (P2, P2), 1)
    same_head = (rb < N) == (cb < N)
    eye_b = rb == cb
    rc = lax.broadcasted_iota(jnp.int32, (C, C), 0)
    cc = lax.broadcasted_iota(jnp.int32, (C, C), 1)
    tril_f = (rc >= cc).astype(F32)
    rk_all = rk_ref[...]

    def bdiag(x):
        return jnp.concatenate([jnp.where(left, x, 0.0), jnp.where(left, 0.0, x)], axis=0)

    def head_sum(x):
        s0 = jnp.sum(jnp.where(left, x, 0.0), axis=-1, keepdims=True)
        s1 = jnp.sum(jnp.where(left, 0.0, x), axis=-1, keepdims=True)
        return jnp.where(left, s0, s1)

    n_sub = 8
    pairs = range(RW_HEADS // 2)

    def chunk_body(it, carry):
        units = [(s, p) for s in range(n_sub) for p in pairs]
        sls = [pl.ds(pl.multiple_of((it * n_sub + s) * C, C), C) for s in range(n_sub)]
        prep = []
        for s in range(n_sub):
            lw = lw_scr[sls[s], :]
            cum = _dot(tril_f, lw, precision=HIGHEST)
            cum_last = cum[C - 1 : C, :]
            prep.append(dict(
                e_incl=jnp.exp(cum), e_excl=jnp.exp(cum - lw), e_neg=jnp.exp(-cum),
                e_tail=jnp.exp(cum_last - cum), p_last=jnp.exp(cum_last),
                r=r_scr[sls[s], :], k=k_scr[sls[s], :], v=v_scr[sls[s], :], kk=kkr_scr[sls[s], :],
                a=a_scr[sls[s], :]))
        lanes = [slice(p * P2, (p + 1) * P2) for p in pairs]
        get = lambda name: [prep[s][name][:, lanes[p]] for s, p in units]
        r2, k2, v2, kk2, a2 = get("r"), get("k"), get("v"), get("kk"), get("a")
        e_incl, e_excl, e_neg, e_tail, p_last = get("e_incl"), get("e_excl"), get("e_neg"), get("e_tail"), get("p_last")
        un = range(len(units))
        al = [kk2[u] / jnp.maximum(jnp.sqrt(head_sum(kk2[u] * kk2[u])), 1e-12) for u in un]
        be = [al[u] * a2[u] for u in un]
        al_t = [al[u] * e_excl[u] for u in un]
        r_t = [r2[u] * e_incl[u] for u in un]
        be_n = [be[u] * e_neg[u] for u in un]
        k_n = [k2[u] * e_neg[u] for u in un]
        k_et = [(k2[u] * e_tail[u]).T for u in un]
        be_et = [(be[u] * e_tail[u]).T for u in un]
        v_bd = [bdiag(v2[u]) for u in un]
        lhs = [jnp.concatenate([al_t[u], r_t[u]], axis=0) for u in un]
        s_b = [_dot_nt(lhs[u], bdiag(be_n[u])) for u in un]
        s_k = [_dot_nt(lhs[u], bdiag(k_n[u])) for u in un]
        l_ab = [jnp.where(stril, s_b[u][:C], 0.0) for u in un]
        a_rb = [jnp.where(tril, s_b[u][C:], 0.0) for u in un]
        l_ak = [jnp.where(stril, s_k[u][:C], 0.0) for u in un]
        a_rk = [jnp.where(tril, s_k[u][C:], 0.0) for u in un]
        lakv = [_dot(l_ak[u], v_bd[u]) for u in un]
        arkv = [_dot(a_rk[u], v_bd[u]) for u in un]
        kev = [_dot(k_et[u], v2[u]) for u in un]
        t_inv = [eye_f - jnp.where(_level_mask(ri, ci, 0), l_ab[u], 0.0) for u in un]
        lvl = 1
        while (1 << lvl) < C:
            lm = _level_mask(ri, ci, lvl)
            tn = [_dot(t_inv[u], bdiag(jnp.where(lm, l_ab[u], 0.0))) for u in un]
            t_inv = [t_inv[u] - _dot(tn[u], bdiag(t_inv[u])) for u in un]
            lvl += 1
        a_til = [_dot(t_inv[u], bdiag(al_t[u])) for u in un]
        b_til = [_dot(t_inv[u], bdiag(lakv[u])) for u in un]
        r_hat = [r_t[u] - _dot(a_rb[u], bdiag(a_til[u])) for u in un]
        o_hat = [arkv[u] - _dot(a_rb[u], bdiag(b_til[u])) for u in un]
        g_bd = [jnp.where(same_head, jnp.where(eye_b, p_last[u], 0.0) - _dot(be_et[u], a_til[u]), 0.0) for u in un]
        h_bd = [jnp.where(same_head, kev[u] - _dot(be_et[u], b_til[u]), 0.0) for u in un]
        lhs_m = [jnp.concatenate([r_hat[u], g_bd[u]], axis=0) for u in un]
        for u, (s, p) in enumerate(units):
            res = _dot(lhs_m[u], m_scr[p])
            m_scr[p] = res[C:] + h_bd[u]
            o_p = res[:C] + o_hat[u]
            cen = o_p - head_sum(o_p) * (1.0 / N)
            var = head_sum(cen * cen) * (1.0 / N)
            on_scr[sls[s], lanes[p]] = cen * lax.rsqrt(var + RW_GN_EPS)
            bon_scr[sls[s], lanes[p]] = head_sum(r2[u] * k2[u] * rk_all[:, lanes[p]]) * v2[u]
        return carry

    lax.fori_loop(0, tb // (C * n_sub), chunk_body, 0)
    out = (on_scr[...] * lnw_ref[...] + lnb_ref[...] + bon_scr[...]) * g
    o_ref[0] = out.astype(o_ref.dtype)

    @pl.when(t_idx == pl.num_programs(1) - 1)
    def _():
        for p in range(RW_HEADS // 2):
            m = m_scr[p]
            sout_ref[0, 2 * p] = m[:N, :N].T
            sout_ref[0, 2 * p + 1] = m[N:, N:].T


def _rwkv_seq(z3, shift0, s0, rw, tb):
    b, l, _ = z3.shape
    const = lambda shape: pl.BlockSpec(shape, lambda i, j: (0,) * len(shape))
    wide = lambda: pltpu.VMEM((tb, RW_WIDTH), F32)
    return pl.pallas_call(
        _rwkv_seq_kernel,
        grid=(b, l // tb),
        in_specs=[
            pl.BlockSpec((1, tb, RW_PROJ), lambda i, j: (i, j, 0)),
            pl.BlockSpec((1, 1, RW_PROJ), lambda i, j: (i, 0, 0)),
            pl.BlockSpec((1, RW_HEADS, RW_N, RW_N), lambda i, j: (i, 0, 0, 0)),
            const((1, RW_PROJ)),
            const((1, RW_WIDTH)), const((128, RW_WIDTH)),
            const((1, RW_WIDTH)), const((128, RW_WIDTH)),
            const((128, RW_WIDTH)),
            const((1, RW_WIDTH)), const((1, RW_WIDTH)), const((1, RW_WIDTH)),
            const((1, RW_WIDTH)), const((1, RW_WIDTH)),
        ],
        out_specs=[
            pl.BlockSpec((1, tb, RW_WIDTH), lambda i, j: (i, j, 0)),
            pl.BlockSpec((1, RW_HEADS, RW_N, RW_N), lambda i, j: (i, 0, 0, 0)),
        ],
        out_shape=[
            jax.ShapeDtypeStruct((b, l, RW_WIDTH), BF16),
            jax.ShapeDtypeStruct((b, RW_HEADS, RW_N, RW_N), F32),
        ],
        scratch_shapes=[
            pltpu.VMEM((RW_HEADS // 2, 2 * RW_N, 2 * RW_N), F32),
            pltpu.VMEM((1, RW_PROJ), F32),
            wide(), wide(), wide(), wide(), wide(), wide(), wide(), wide(),
        ],
        compiler_params=_params(("parallel", "arbitrary")),
    )(z3, shift0, s0, rw["mu"], rw["w0"], rw["w2p"], rw["a0"], rw["a2p"], rw["g2"], rw["k_k"], rw["k_a"],
      rw["r_k"], rw["ln_w"], rw["ln_b"])


def _rwkv_step_prep_kernel(z_ref, shift0_ref, mu_ref, w0_ref, w2_ref, a0_ref, a2_ref, g2_ref, kk_ref, ka_ref,
                           rk_ref, r_ref, k_ref, v_ref, al_ref, be_ref, dec_ref, g_ref, bon_ref):
    z = z_ref[...]
    zs = z + mu_ref[...] * (shift0_ref[...] - z)
    r, k, v, kk_raw, a, log_decay, g = _rwkv_features(
        zs, w0_ref[...], w2_ref[...], a0_ref[...], a2_ref[...], g2_ref[...], kk_ref[...], ka_ref[...])
    rk_all = rk_ref[...]
    for h in range(RW_HEADS):
        hs = slice(h * RW_N, (h + 1) * RW_N)
        kk_h = kk_raw[:, hs]
        nrm = jnp.sqrt(jnp.sum(kk_h * kk_h, axis=-1, keepdims=True))
        al = kk_h / jnp.maximum(nrm, 1e-12)
        al_ref[:, hs] = al
        be_ref[:, hs] = al * a[:, hs]
        bon_ref[:, hs] = jnp.sum(r[:, hs] * k[:, hs] * rk_all[:, hs], axis=-1, keepdims=True) * v[:, hs]
    r_ref[...] = r
    k_ref[...] = k
    v_ref[...] = v
    dec_ref[...] = jnp.exp(log_decay)
    g_ref[...] = g


def _rwkv_step_prep(z, shift0, rw):
    n = z.shape[0]
    out = jax.ShapeDtypeStruct((n, RW_WIDTH), F32)
    return pl.pallas_call(
        _rwkv_step_prep_kernel,
        out_shape=[out] * 8,
        compiler_params=pltpu.CompilerParams(vmem_limit_bytes=VMEM_LIMIT),
    )(z, shift0, rw["mu"], rw["w0"], rw["w2p"], rw["a0"], rw["a2p"], rw["g2"], rw["k_k"], rw["k_a"], rw["r_k"])


def _rwkv_step_kernel(s_ref, r_ref, k_ref, al_ref, be_ref, dec_ref, v_ref, snew_ref, o_ref):
    r, k, al, be, dec = r_ref[...], k_ref[...], al_ref[...], be_ref[...], dec_ref[...]

    def body(g, carry):
        rows = pl.ds(pl.multiple_of(g * 8, 8), 8)
        v8 = v_ref[rows, :]
        outs = []
        for j in range(8):
            s = s_ref[0, g * 8 + j]
            sa = -jnp.sum(s * al, axis=0, keepdims=True)
            s_new = s * dec + sa * be + v8[j : j + 1, :] * k
            snew_ref[0, g * 8 + j] = s_new
            outs.append(jnp.sum(s_new * r, axis=0, keepdims=True))
        o_ref[rows, :] = jnp.concatenate(outs, axis=0)
        return carry

    lax.fori_loop(0, RW_N // 8, body, 0)


def _rwkv_step(s0, r, k, al, be, dec, v):
    n = s0.shape[0]
    s_t = jnp.transpose(s0, (1, 2, 3, 0))
    s_spec = pl.BlockSpec((1, RW_N, RW_N, n), lambda h: (h, 0, 0, 0))
    op_spec = pl.BlockSpec((RW_N, n), lambda h: (h, 0))
    s_new_t, o_t = pl.pallas_call(
        _rwkv_step_kernel,
        grid=(RW_HEADS,),
        in_specs=[s_spec] + [op_spec] * 6,
        out_specs=[s_spec, op_spec],
        out_shape=[
            jax.ShapeDtypeStruct(s_t.shape, F32),
            jax.ShapeDtypeStruct((RW_WIDTH, n), F32),
        ],
        compiler_params=_params(("parallel",)),
    )(s_t, r.T, k.T, al.T, be.T, dec.T, v.T)
    return jnp.transpose(s_new_t, (3, 0, 1, 2)), o_t.T


def _gla_features(z, gkw, gkb):
    q = z[:, 0:GLA_QK] * (GLA_DK ** -0.5)
    k = z[:, GLA_QK : 2 * GLA_QK]
    v = z[:, 2 * GLA_QK : 2 * GLA_QK + GLA_WIDTH]
    g = z[:, 2 * GLA_QK + GLA_WIDTH : 2 * GLA_QK + 2 * GLA_WIDTH]
    zgk = z[:, 2 * GLA_QK + 2 * GLA_WIDTH :]
    gk = -_softplus(-(_dot(zgk.astype(BF16), gkw) + gkb)) / GLA_GATE_NORMALIZER
    return q, k, v, g, gk


def _gla_finish(o, g, norm_w):
    outs = []
    for h in range(GLA_HEADS):
        hs = slice(h * GLA_DV, (h + 1) * GLA_DV)
        o_h = o[:, hs]
        o_h = o_h * lax.rsqrt(jnp.mean(o_h * o_h, axis=-1, keepdims=True) + NORM_EPS) * norm_w
        g_h = g[:, hs]
        outs.append(o_h * (g_h * _sigmoid(g_h)))
    return jnp.concatenate(outs, axis=-1)


def _gla_seq_kernel(z_ref, s0_ref, gkw_ref, gkb_ref, nw_ref, wsel_ref, o_ref, sout_ref,
                    st_scr, x_scr, q_scr, k_scr, v_scr, gc_scr, oi_scr):
    C = GLA_CHUNK
    G = 128
    t_idx = pl.program_id(1)
    tb = z_ref.shape[1]
    nc = tb // C

    @pl.when(t_idx == 0)
    def _():
        for h in range(GLA_HEADS):
            st_scr[h] = s0_ref[0, h].T

    q, k, v, g, gk = _gla_features(z_ref[0], gkw_ref[...], gkb_ref[...])
    ri = lax.broadcasted_iota(jnp.int32, (G, G), 0)
    ci = lax.broadcasted_iota(jnp.int32, (G, G), 1)
    cum_mat = ((ri // C == ci // C) & (ri >= ci)).astype(F32)
    for m in range(tb // G):
        rows = slice(m * G, (m + 1) * G)
        gc_scr[rows, :] = _dot(cum_mat, gk[rows, :], precision=HIGHEST)
    q_scr[...] = q
    k_scr[...] = k
    v_scr[...] = v
    gcum = gc_scr[...]

    rg = lax.broadcasted_iota(jnp.int32, (tb, 2 * G), 0)
    cg = lax.broadcasted_iota(jnp.int32, (tb, 2 * G), 1)
    blk_mask = ((cg % G) // C == (rg % G) // C) & (cg % C <= rg % C)
    for p in range(GLA_HEADS // 2):
        ls = slice(p * 128, (p + 1) * 128)
        q3 = q[:, ls].reshape(nc, C, 128)
        k3 = k[:, ls].reshape(nc, C, 128)
        g3 = gcum[:, ls].reshape(nc, C, 128)
        for j in range(C):
            e = q3 * jnp.exp(jnp.minimum(g3 - g3[:, j : j + 1, :], 0.0)) * k3[:, j : j + 1, :]
            x_scr[:, j * 128 : (j + 1) * 128] = e.reshape(tb, 128).astype(BF16)
        a_t = jnp.where(blk_mask, _dot(x_scr[...], wsel_ref[...]), 0.0).astype(BF16)
        for hl in range(2):
            h = 2 * p + hl
            for m in range(tb // G):
                rows = slice(m * G, (m + 1) * G)
                a_blk = a_t[rows, hl * G : (hl + 1) * G]
                oi_scr[rows, h * GLA_DV : (h + 1) * GLA_DV] = _dot(
                    a_blk, v[rows, h * GLA_DV : (h + 1) * GLA_DV].astype(BF16))

    def chunk_body(c, carry):
        sl = pl.ds(pl.multiple_of(c * C, C), C)
        g_c = gc_scr[sl, :]
        q_c = q_scr[sl, :]
        k_c = k_scr[sl, :]
        v_c = v_scr[sl, :]
        for h in range(GLA_HEADS):
            ks = slice(h * GLA_DK, (h + 1) * GLA_DK)
            vs = slice(h * GLA_DV, (h + 1) * GLA_DV)
            g_h = g_c[:, ks]
            g_last = g_h[C - 1 : C, :]
            st = st_scr[h]
            oi_scr[sl, vs] += _dot_nt(q_c[:, ks] * jnp.exp(g_h), st)
            k_dec = k_c[:, ks] * jnp.exp(g_last - g_h)
            st_scr[h] = st * jnp.exp(g_last) + _dot_tn(v_c[:, vs], k_dec)
        return carry

    lax.fori_loop(0, nc, chunk_body, 0, unroll=8)
    o_ref[0] = _gla_finish(oi_scr[...], g, nw_ref[...]).astype(o_ref.dtype)

    @pl.when(t_idx == pl.num_programs(1) - 1)
    def _():
        for h in range(GLA_HEADS):
            sout_ref[0, h] = st_scr[h].T


def _gla_select_matrix():
    j = jnp.arange(GLA_CHUNK)[:, None, None]
    hl = jnp.arange(2)[None, :, None]
    rows_j = jnp.broadcast_to(j, (GLA_CHUNK, 2, GLA_DK)).reshape(-1)
    rows_h = jnp.broadcast_to(hl, (GLA_CHUNK, 2, GLA_DK)).reshape(-1)
    cols = jnp.arange(256)
    sel = (rows_j[:, None] == cols[None, :] % GLA_CHUNK) & (rows_h[:, None] == cols[None, :] // 128)
    return sel.astype(BF16)


def _gla_seq(z3, s0, gl, tb):
    b, l, _ = z3.shape
    const = lambda shape: pl.BlockSpec(shape, lambda i, j: (0,) * len(shape))
    return pl.pallas_call(
        _gla_seq_kernel,
        grid=(b, l // tb),
        in_specs=[
            pl.BlockSpec((1, tb, GLA_PROJ_PAD), lambda i, j: (i, j, 0)),
            pl.BlockSpec((1, GLA_HEADS, GLA_DK, GLA_DV), lambda i, j: (i, 0, 0, 0)),
            const((GLA_LORA_PAD, GLA_QK)), const((1, GLA_QK)), const((1, GLA_DV)),
            const((GLA_CHUNK * 128, 256)),
        ],
        out_specs=[
            pl.BlockSpec((1, tb, GLA_WIDTH), lambda i, j: (i, j, 0)),
            pl.BlockSpec((1, GLA_HEADS, GLA_DK, GLA_DV), lambda i, j: (i, 0, 0, 0)),
        ],
        out_shape=[
            jax.ShapeDtypeStruct((b, l, GLA_WIDTH), BF16),
            jax.ShapeDtypeStruct((b, GLA_HEADS, GLA_DK, GLA_DV), F32),
        ],
        scratch_shapes=[
            pltpu.VMEM((GLA_HEADS, GLA_DV, GLA_DK), F32),
            pltpu.VMEM((tb, GLA_CHUNK * 128), BF16),
            pltpu.VMEM((tb, GLA_QK), F32), pltpu.VMEM((tb, GLA_QK), F32), pltpu.VMEM((tb, GLA_WIDTH), F32),
            pltpu.VMEM((tb, GLA_QK), F32), pltpu.VMEM((tb, GLA_WIDTH), F32),
        ],
        compiler_params=_params(("parallel", "arbitrary")),
    )(z3, s0, gl["gkw"], gl["gkb"], gl["norm_w"], _gla_select_matrix())


def _gla_step_prep_kernel(z_ref, gkw_ref, gkb_ref, q_ref, k_ref, v_ref, g_ref, dec_ref):
    q, k, v, g, gk = _gla_features(z_ref[...], gkw_ref[...], gkb_ref[...])
    q_ref[...] = q
    k_ref[...] = k
    v_ref[...] = v
    g_ref[...] = g
    dec_ref[...] = jnp.exp(gk)


def _gla_step_prep(z, gl):
    n = z.shape[0]
    qk = jax.ShapeDtypeStruct((n, GLA_QK), F32)
    wide = jax.ShapeDtypeStruct((n, GLA_WIDTH), F32)
    return pl.pallas_call(
        _gla_step_prep_kernel,
        out_shape=[qk, qk, wide, wide, qk],
        compiler_params=pltpu.CompilerParams(vmem_limit_bytes=VMEM_LIMIT),
    )(z, gl["gkw"], gl["gkb"])


def _gla_step_kernel(s_ref, qcol_ref, kcol_ref, dcol_ref, vrow_ref, snew_ref, orow_ref):
    s_new = s_ref[...] * dcol_ref[...] + kcol_ref[...] * vrow_ref[...]
    snew_ref[...] = s_new
    orow_ref[...] = jnp.sum(s_new * qcol_ref[...], axis=2, keepdims=True)


def _gla_step(s0, q, k, dec, v, bb):
    n = s0.shape[0]
    colv = lambda t: t.reshape(n, GLA_HEADS, GLA_DK, 1)
    col_spec = pl.BlockSpec((bb, GLA_HEADS, GLA_DK, 1), lambda i: (i, 0, 0, 0))
    row_spec = pl.BlockSpec((bb, GLA_HEADS, 1, GLA_DV), lambda i: (i, 0, 0, 0))
    s_spec = pl.BlockSpec((bb, GLA_HEADS, GLA_DK, GLA_DV), lambda i: (i, 0, 0, 0))
    s_new, o_row = pl.pallas_call(
        _gla_step_kernel,
        grid=(n // bb,),
        in_specs=[s_spec, col_spec, col_spec, col_spec, row_spec],
        out_specs=[s_spec, row_spec],
        out_shape=[
            jax.ShapeDtypeStruct(s0.shape, F32),
            jax.ShapeDtypeStruct((n, GLA_HEADS, 1, GLA_DV), F32),
        ],
        compiler_params=_params(("parallel",)),
    )(s0, colv(q), colv(k), colv(dec), v.reshape(n, GLA_HEADS, 1, GLA_DV))
    return s_new, o_row.reshape(n, GLA_WIDTH)


def _step_post_kernel(orw_ref, bon_ref, grw_ref, lnw_ref, lnb_ref, ogl_ref, ggl_ref, nw_ref, o_rw_ref, o_gl_ref):
    o = orw_ref[...]
    for h in range(RW_HEADS):
        hs = slice(h * RW_N, (h + 1) * RW_N)
        o_h = o[:, hs]
        mean = jnp.mean(o_h, axis=-1, keepdims=True)
        cen = o_h - mean
        var = jnp.mean(cen * cen, axis=-1, keepdims=True)
        on = cen * lax.rsqrt(var + RW_GN_EPS)
        res = (on * lnw_ref[:, hs] + lnb_ref[:, hs] + bon_ref[:, hs]) * grw_ref[:, hs]
        o_rw_ref[:, hs] = res.astype(o_rw_ref.dtype)
    o_gl_ref[...] = _gla_finish(ogl_ref[...], ggl_ref[...], nw_ref[...]).astype(o_gl_ref.dtype)


def _step_post(o_rw, bonus, g_rw, rw, o_gl, g_gl, gl):
    n = o_rw.shape[0]
    return pl.pallas_call(
        _step_post_kernel,
        out_shape=[jax.ShapeDtypeStruct((n, RW_WIDTH), BF16), jax.ShapeDtypeStruct((n, GLA_WIDTH), BF16)],
        compiler_params=pltpu.CompilerParams(vmem_limit_bytes=VMEM_LIMIT),
    )(o_rw, bonus, g_rw, rw["ln_w"], rw["ln_b"], o_gl, g_gl, gl["norm_w"])


def _outproj_router_kernel(x_ref, orw_ref, ogl_ref, wo_ref, gain_ref, wrt_ref, br_ref,
                           h_ref, xn_ref, idx_ref, gate_ref, rank_ref, cnt_ref):
    tm = x_ref.shape[0]
    mix = jnp.concatenate([orw_ref[...], ogl_ref[...]], axis=-1)
    h = x_ref[...] + _dot(mix, wo_ref[...])
    h_ref[...] = h
    xn = h * lax.rsqrt(jnp.mean(h * h, axis=-1, keepdims=True) + NORM_EPS) * gain_ref[...]
    _store_row_tiles(xn_ref, xn)
    logits = _dot_nt(wrt_ref[...], xn, precision=HIGHEST) + br_ref[...]
    eidx = lax.broadcasted_iota(jnp.int32, logits.shape, 0)
    ti = lax.broadcasted_iota(jnp.int32, (tm, tm), 0)
    tj = lax.broadcasted_iota(jnp.int32, (tm, tm), 1)
    before = (ti < tj).astype(BF16)
    vals, idxs = [], []
    work = logits
    chosen = jnp.zeros(logits.shape, F32)
    for _ in range(TOP_K):
        m = jnp.max(work, axis=0, keepdims=True)
        sel = jnp.min(jnp.where(work == m, eidx, N_EXPERTS), axis=0, keepdims=True)
        hit = eidx == sel
        work = jnp.where(hit, -jnp.inf, work)
        chosen = chosen + hit.astype(F32)
        vals.append(m)
        idxs.append(sel)
    prefix = _dot(chosen.astype(BF16), before)
    exps = [jnp.exp(v - vals[0]) for v in vals]
    denom = exps[0] + exps[1] + exps[2] + exps[3]
    for j in range(TOP_K):
        idx_ref[0, j : j + 1, :] = idxs[j]
        gate_ref[0, j : j + 1, :] = exps[j] / denom
        rank = jnp.sum(jnp.where(eidx == idxs[j], prefix, 0.0), axis=0, keepdims=True)
        rank_ref[0, j : j + 1, :] = rank.astype(jnp.int32)
    cnt = jnp.sum(chosen, axis=1, keepdims=True)
    cnt_ref[0] = jnp.broadcast_to(cnt, (N_EXPERTS, 128)).astype(jnp.int32)


def _outproj_router(x, o_rw, o_gl, w_out, gain, w_router_t, b_router, tm):
    n = x.shape[0]
    nt = n // tm
    const = lambda shape: pl.BlockSpec(shape, lambda i: (0,) * len(shape))
    tok = lambda width: pl.BlockSpec((tm, width), lambda i: (i, 0))
    lane = pl.BlockSpec((1, TOP_K, tm), lambda i: (i, 0, 0))
    return pl.pallas_call(
        _outproj_router_kernel,
        grid=(nt,),
        in_specs=[
            tok(D_MODEL), tok(RW_WIDTH), tok(GLA_WIDTH),
            const((D_MODEL, D_MODEL)), const((1, D_MODEL)), const((N_EXPERTS, D_MODEL)), const((N_EXPERTS, 1)),
        ],
        out_specs=[tok(D_MODEL), pl.BlockSpec((tm * ROW_TILE, 128), lambda i: (i, 0)), lane, lane, lane,
                   pl.BlockSpec((1, N_EXPERTS, 128), lambda i: (i, 0, 0))],
        out_shape=[
            jax.ShapeDtypeStruct((n, D_MODEL), F32),
            jax.ShapeDtypeStruct((n * ROW_TILE, 128), F32),
            jax.ShapeDtypeStruct((nt, TOP_K, tm), jnp.int32),
            jax.ShapeDtypeStruct((nt, TOP_K, tm), F32),
            jax.ShapeDtypeStruct((nt, TOP_K, tm), jnp.int32),
            jax.ShapeDtypeStruct((nt, N_EXPERTS, 128), jnp.int32),
        ],
        compiler_params=_params(("parallel",)),
    )(x, o_rw, o_gl, w_out, gain, w_router_t, b_router)


def _moe_kernel(be_ref, nu_ref, epoch_ref, next_ref, xs_ref, wg_hbm, wu_hbm, wd_hbm, bg_ref, bu_ref, bd_ref, y_ref,
                w_f32, wg_b, wu_b, wd_b, sems):
    b = pl.program_id(0)
    prev = be_ref[jnp.maximum(b - 1, 0)]
    new_expert = (b == 0) | (be_ref[b] != prev)

    def fetch(e, slot):
        return [pltpu.make_async_copy(w.at[e], w_f32.at[slot, i], sems.at[slot])
                for i, w in enumerate((wg_hbm, wu_hbm, wd_hbm))]

    @pl.when(b == 0)
    def _():
        for c in fetch(be_ref[0], 0):
            c.start()

    @pl.when(new_expert)
    def _():
        slot = epoch_ref[b] % 2
        for c in fetch(be_ref[b], slot):
            c.wait()

        @pl.when(next_ref[b] >= 0)
        def _():
            for c in fetch(next_ref[b], 1 - slot):
                c.start()

        wg_b[...] = w_f32[slot, 0].astype(BF16)
        wu_b[...] = w_f32[slot, 1].astype(BF16)
        wd_b[...] = w_f32[slot, 2].astype(BF16)

    @pl.when(b < nu_ref[0])
    def _():
        x = _load_row_tiles(xs_ref, MOE_BLOCK).astype(BF16)
        half = D_MODEL // 2
        acc = None
        for f in range(2):
            fs = slice(f * half, (f + 1) * half)
            gt = _dot(x, wg_b[:, fs]) + bg_ref[0, :, fs]
            up = _dot(x, wu_b[:, fs]) + bu_ref[0, :, fs]
            gt = jnp.minimum(gt, SWIGLU_LIMIT)
            up = jnp.clip(up, -SWIGLU_LIMIT, SWIGLU_LIMIT)
            hid = (up + 1.0) * gt * _sigmoid(SWIGLU_ALPHA * gt)
            part = _dot(hid.astype(BF16), wd_b[fs, :])
            acc = part if acc is None else acc + part
        _store_row_tiles(y_ref, acc + bd_ref[0])


def _moe_ffn(block_expert, n_used, xs, w_gate, w_up, w_down, b_gate, b_up, b_down):
    n_blocks = block_expert.shape[0]
    pos = jnp.arange(n_blocks, dtype=jnp.int32)
    change = (pos > 0) & (block_expert != jnp.roll(block_expert, 1))
    epoch = jnp.cumsum(change.astype(jnp.int32))
    later = change[None, :] & (pos[None, :] > pos[:, None])
    first = jnp.min(jnp.where(later, pos[None, :], n_blocks), axis=1)
    next_e = jnp.sum(jnp.where(pos[None, :] == first[:, None], block_expert[None, :], 0), axis=1)
    next_e = jnp.where(first < n_blocks, next_e, -1).astype(jnp.int32)

    row = lambda b, be, nu, ep, nx: (jnp.minimum(b, nu[0] - 1), 0)
    bspec = pl.BlockSpec((1, 1, D_MODEL), lambda b, be, nu, ep, nx: (be[b], 0, 0))
    wspec = pl.BlockSpec(memory_space=pl.ANY)
    grid_spec = pltpu.PrefetchScalarGridSpec(
        num_scalar_prefetch=4,
        grid=(n_blocks,),
        in_specs=[pl.BlockSpec((MOE_BLOCK * ROW_TILE, 128), row), wspec, wspec, wspec, bspec, bspec, bspec],
        out_specs=pl.BlockSpec((MOE_BLOCK * ROW_TILE, 128), row),
        scratch_shapes=[pltpu.VMEM((2, 3, D_MODEL, D_MODEL), F32)] + [pltpu.VMEM((D_MODEL, D_MODEL), BF16)] * 3
        + [pltpu.SemaphoreType.DMA((2,))],
    )
    return pl.pallas_call(
        _moe_kernel,
        grid_spec=grid_spec,
        out_shape=jax.ShapeDtypeStruct((n_blocks * MOE_BLOCK * ROW_TILE, 128), F32),
        compiler_params=_params(("arbitrary",)),
    )(block_expert, n_used, epoch, next_e, xs, w_gate, w_up, w_down,
      b_gate.reshape(N_EXPERTS, 1, D_MODEL), b_up.reshape(N_EXPERTS, 1, D_MODEL),
      b_down.reshape(N_EXPERTS, 1, D_MODEL))


def _combine_kernel(slot_c, slot_n, h_ref, gate_ref, gain_ref, y_hbm, o_ref, ybuf, sems):
    i = pl.program_id(0)
    nt = pl.num_programs(0)
    tm = h_ref.shape[0]

    def row_copy(src, slot, j, t):
        dst = pl.multiple_of(t * ROW_TILE, ROW_TILE)
        return pltpu.make_async_copy(y_hbm.at[pl.ds(src, ROW_TILE)], ybuf.at[slot, j, pl.ds(dst, ROW_TILE)],
                                     sems.at[slot])

    def issue(slot_ref, slot):
        def body(t, carry):
            for j in range(TOP_K):
                row_copy(pl.multiple_of(slot_ref[0, j, t], ROW_TILE), slot, j, t).start()
            return carry
        lax.fori_loop(0, tm, body, 0, unroll=8)

    @pl.when(i == 0)
    def _():
        issue(slot_c, 0)

    @pl.when(i + 1 < nt)
    def _():
        issue(slot_n, (i + 1) % 2)

    slot = i % 2
    for j in range(TOP_K):
        pltpu.make_async_copy(y_hbm.at[pl.ds(0, tm * ROW_TILE)], ybuf.at[slot, j], sems.at[slot]).wait()
    f = h_ref[...]
    gates = gate_ref[...]
    for j in range(TOP_K):
        f = f + gates[:, j : j + 1] * _load_row_tiles(ybuf.at[slot, j], tm)
    o_ref[...] = f * lax.rsqrt(jnp.mean(f * f, axis=-1, keepdims=True) + NORM_EPS) * gain_ref[...]


def _combine(h, y_rows, slot3, gates, gain, tm):
    n = h.shape[0]
    nt = n // tm
    smem = lambda shape, imap: pl.BlockSpec(shape, imap, memory_space=pltpu.SMEM)
    cur = lambda i: (i, 0, 0)
    nxt = lambda i: (jnp.minimum(i + 1, nt - 1), 0, 0)
    return pl.pallas_call(
        _combine_kernel,
        grid=(nt,),
        in_specs=[
            smem((1, TOP_K, tm), cur), smem((1, TOP_K, tm), nxt),
            pl.BlockSpec((tm, D_MODEL), lambda i: (i, 0)),
            pl.BlockSpec((tm, TOP_K), lambda i: (i, 0)),
            pl.BlockSpec((1, D_MODEL), lambda i: (0, 0)),
            pl.BlockSpec(memory_space=pl.ANY),
        ],
        out_specs=pl.BlockSpec((tm, D_MODEL), lambda i: (i, 0)),
        out_shape=jax.ShapeDtypeStruct((n, D_MODEL), F32),
        scratch_shapes=[pltpu.VMEM((2, TOP_K, tm * ROW_TILE, 128), F32), pltpu.SemaphoreType.DMA((2,))],
        compiler_params=pltpu.CompilerParams(dimension_semantics=("arbitrary",), vmem_limit_bytes=VMEM_LIMIT,
                                             disable_bounds_checks=True),
    )(slot3, slot3, h, gates, gain, y_rows)


def _dispatch_kernel(*refs, fill):
    if fill:
        slot_ref, ends_ref, x_ref, xs_hbm, zero_scr, sem, zsem = refs
    else:
        slot_ref, x_ref, _, xs_hbm, sem = refs
    tm = x_ref.shape[0] // ROW_TILE
    blk = MOE_BLOCK * ROW_TILE

    if fill:
        def fill_copy(e):
            start = pl.multiple_of((ends_ref[0, e] - MOE_BLOCK) * ROW_TILE, blk)
            return pltpu.make_async_copy(zero_scr, xs_hbm.at[pl.ds(start, blk)], zsem)

        @pl.when(pl.program_id(0) == 0)
        def _():
            zero_scr[...] = jnp.zeros(zero_scr.shape, zero_scr.dtype)
            for e in range(N_EXPERTS):
                @pl.when(ends_ref[1, e] > 0)
                def _():
                    fill_copy(e).start()
            for e in range(N_EXPERTS):
                @pl.when(ends_ref[1, e] > 0)
                def _():
                    fill_copy(e).wait()

    def body(t, carry):
        src = x_ref.at[pl.ds(pl.multiple_of(t * ROW_TILE, ROW_TILE), ROW_TILE)]
        for j in range(TOP_K):
            dst = pl.multiple_of(slot_ref[0, j, t], ROW_TILE)
            pltpu.make_async_copy(src, xs_hbm.at[pl.ds(dst, ROW_TILE)], sem).start()
        return carry

    lax.fori_loop(0, tm, body, 0, unroll=8)
    for j in range(TOP_K):
        pltpu.make_async_copy(x_ref, xs_hbm.at[pl.ds(0, tm * ROW_TILE)], sem).wait()


def _dispatch(xn, slot3, tm, n_slots, ends=None, xs=None):
    n = xn.shape[0] // ROW_TILE
    fill = xs is None
    smem = lambda shape, imap: pl.BlockSpec(shape, imap, memory_space=pltpu.SMEM)
    in_specs = [smem((1, TOP_K, tm), lambda i: (i, 0, 0))]
    args = [slot3]
    scratch = []
    if fill:
        in_specs.append(smem((2, N_EXPERTS), lambda i: (0, 0)))
        args.append(ends)
        scratch = [pltpu.VMEM((MOE_BLOCK * ROW_TILE, 128), F32)]
    in_specs.append(pl.BlockSpec((tm * ROW_TILE, 128), lambda i: (i, 0)))
    args.append(xn)
    aliases = {}
    if not fill:
        in_specs.append(pl.BlockSpec(memory_space=pl.ANY))
        args.append(xs)
        aliases = {len(args) - 1: 0}
    scratch.append(pltpu.SemaphoreType.DMA(()))
    if fill:
        scratch.append(pltpu.SemaphoreType.DMA(()))
    return pl.pallas_call(
        functools.partial(_dispatch_kernel, fill=fill),
        grid=(n // tm,),
        in_specs=in_specs,
        out_specs=pl.BlockSpec(memory_space=pl.ANY),
        out_shape=jax.ShapeDtypeStruct((n_slots * ROW_TILE, 128), F32),
        scratch_shapes=scratch,
        input_output_aliases=aliases,
        compiler_params=pltpu.CompilerParams(dimension_semantics=("arbitrary",), vmem_limit_bytes=VMEM_LIMIT,
                                             disable_bounds_checks=True, has_side_effects=True),
    )(*args)


def _pad_rows(w, rows, offset):
    out = jnp.zeros((rows, w.shape[1]), w.dtype)
    return out.at[offset : offset + w.shape[0]].set(w)


def _routing_tables(counts, n_pairs):
    n_blocks = (n_pairs + N_EXPERTS * (MOE_BLOCK - 1) + MOE_BLOCK - 1) // MOE_BLOCK
    total = jnp.sum(counts, axis=0)
    padded = (total + MOE_BLOCK - 1) // MOE_BLOCK * MOE_BLOCK
    pends = jnp.cumsum(padded)
    pstarts = pends - padded
    tile_base = pstarts[None, :] + jnp.cumsum(counts, axis=0) - counts
    blocks = jnp.arange(n_blocks, dtype=jnp.int32) * MOE_BLOCK
    n_used = (pends[-1] // MOE_BLOCK).astype(jnp.int32)
    owner = jnp.sum((pends[None, :] <= blocks[:, None]).astype(jnp.int32), axis=1)
    block_expert = jnp.minimum(owner, N_EXPERTS - 1)
    last = jnp.sum(jnp.where(jnp.arange(n_blocks) == n_used - 1, block_expert, 0))
    block_expert = jnp.where(jnp.arange(n_blocks) < n_used, block_expert, last)
    ends = jnp.stack([pends, padded]).astype(jnp.int32)
    return tile_base.astype(jnp.int32)[:, None, :], ends, block_expert.astype(jnp.int32), n_used.reshape(1), n_blocks


def kernel(x_prompt, x_sample, state_rwkv_shift, state_rwkv_wkv, state_gla, norm_mix, w_in, rw_mu, rw_w0, rw_w2, rw_a0, rw_a2, rw_g2, rw_k_k, rw_k_a, rw_r_k, rw_ln_w, rw_ln_b, gla_gk_w2, gla_gk_b, gla_norm_w, w_out, norm_ffn, w_router, b_router, w_gate, b_gate, w_up, b_up, w_down, b_down, norm_final):
    depth = norm_mix.shape[0]
    assert depth == 1
    bp, lp, d = x_prompt.shape
    bs, ls, _ = x_sample.shape
    assert ls == 1 and lp % SEQ_BLOCK == 0
    l = 0
    row = lambda t: t.reshape(1, -1)

    w_in_b = w_in[l].astype(BF16)
    w_in_r = w_in_b[:, :RW_PROJ]
    w_in_g = jnp.pad(w_in_b[:, RW_PROJ:], ((0, 0), (0, GLA_PROJ_PAD - GLA_PROJ)))
    rw = dict(
        mu=row(rw_mu[l]), w0=row(rw_w0[l]), a0=row(rw_a0[l]),
        w2p=_pad_rows(rw_w2[l].astype(BF16), 128, 0), a2p=_pad_rows(rw_a2[l].astype(BF16), 128, 64),
        g2=rw_g2[l].astype(BF16), k_k=row(rw_k_k[l]), k_a=row(rw_k_a[l]), r_k=row(rw_r_k[l]),
        ln_w=row(rw_ln_w[l]), ln_b=row(rw_ln_b[l]))
    gl = dict(gkw=_pad_rows(gla_gk_w2[l].astype(BF16), GLA_LORA_PAD, 0), gkb=row(gla_gk_b[l]),
              norm_w=row(gla_norm_w[l]))
    gain_mix = row(norm_mix[l])

    n_p = bp * lp
    xp = x_prompt.reshape(n_p, d)
    zr_p, zg_p = _inproj(xp, gain_mix, w_in_r, w_in_g, TOK_BLOCK)
    zr_p3 = zr_p.reshape(bp, lp, RW_PROJ)
    o_rw_p, wkv_p = _rwkv_seq(zr_p3, jnp.zeros((bp, 1, RW_PROJ), F32),
                              jnp.zeros((bp, RW_HEADS, RW_N, RW_N), F32), rw, SEQ_BLOCK)
    o_gl_p, gla_p = _gla_seq(zg_p.reshape(bp, lp, GLA_PROJ_PAD),
                             jnp.zeros((bp, GLA_HEADS, GLA_DK, GLA_DV), F32), gl, SEQ_BLOCK)
    shift_p = zr_p3[:, -1, :]

    xs_ = x_sample.reshape(bs, d)
    zr_s, zg_s = _inproj(xs_, gain_mix, w_in_r, w_in_g, bs)
    r, k, v, al, be, dec, g_rw, bonus = _rwkv_step_prep(zr_s, state_rwkv_shift[l], rw)
    wkv_s, o_rw_s = _rwkv_step(state_rwkv_wkv[l], r, k, al, be, dec, v)
    q, kg, vg, g_gl, dec_g = _gla_step_prep(zg_s, gl)
    gla_s, o_gl_s = _gla_step(state_gla[l], q, kg, dec_g, vg, 16)
    o_rw_s2, o_gl_s2 = _step_post(o_rw_s, bonus, g_rw, rw, o_gl_s, g_gl, gl)
    shift_s = zr_s

    w_out_b = w_out[l].astype(BF16)
    router = (w_out_b, row(norm_ffn[l]), w_router[l].T, b_router[l].reshape(N_EXPERTS, 1))
    h_p, xn_p, idx_p, gate_p, rank_p, cnt_p = _outproj_router(
        xp, o_rw_p.reshape(n_p, RW_WIDTH), o_gl_p.reshape(n_p, GLA_WIDTH), *router, TOK_BLOCK)
    h_s, xn_s, idx_s, gate_s, rank_s, cnt_s = _outproj_router(xs_, o_rw_s2, o_gl_s2, *router, bs)
    nt_p = n_p // TOK_BLOCK
    counts = jnp.concatenate([cnt_p[:, :, 0], cnt_s[:, :, 0]], axis=0)
    base3, ends, block_expert, n_used, n_blocks = _routing_tables(counts, (n_p + bs) * TOP_K)
    n_slots = n_blocks * MOE_BLOCK

    def slots(idx3, rank3, base):
        hit = idx3[..., None] == jnp.arange(N_EXPERTS, dtype=jnp.int32)
        return (rank3 + jnp.sum(jnp.where(hit, base[:, :, None, :], 0), axis=-1)) * ROW_TILE

    slot_p = slots(idx_p, rank_p, base3[:nt_p])
    slot_s = slots(idx_s, rank_s, base3[nt_p:])
    xs_rows = _dispatch(xn_p, slot_p, TOK_BLOCK, n_slots, ends=ends)
    xs_rows = _dispatch(xn_s, slot_s, bs, n_slots, xs=xs_rows)
    y_rows = _moe_ffn(block_expert, n_used, xs_rows, w_gate[l], w_up[l], w_down[l], b_gate[l], b_up[l], b_down[l])
    gates = lambda g3: jnp.swapaxes(g3, 1, 2).reshape(-1, TOP_K)
    gain_f = row(norm_final)
    y_p = _combine(h_p, y_rows, slot_p, gates(gate_p), gain_f, TOK_BLOCK)
    y_s = _combine(h_s, y_rows, slot_s, gates(gate_s), gain_f, bs)

    y_prompt = y_p.reshape(bp, lp, d)
    y_sample = y_s.reshape(bs, ls, d)
    return (y_prompt, y_sample, shift_p[None], wkv_p[None], gla_p[None], shift_s[None], wkv_s[None], gla_s[None])
```

```python
import functools

import jax
import jax.numpy as jnp
from jax import lax
from jax.experimental import pallas as pl
from jax.experimental.pallas import tpu as pltpu

F32 = jnp.float32
BF16 = jnp.bfloat16
HIGHEST = lax.Precision.HIGHEST

D_MODEL = 1024
RW_WIDTH = 512
RW_HEADS = 8
RW_N = 64
RW_PROJ = 1792
RW_GN_EPS = 64e-5
GLA_HEADS = 4
GLA_DK = 64
GLA_DV = 128
GLA_WIDTH = 512
GLA_QK = GLA_HEADS * GLA_DK
GLA_PROJ = 1552
GLA_PROJ_PAD = 1664
GLA_LORA_PAD = 128
GLA_GATE_NORMALIZER = 16.0
N_EXPERTS = 32
TOP_K = 4
SWIGLU_LIMIT = 7.0
SWIGLU_ALPHA = 1.702
NORM_EPS = 1e-5

RW_CHUNK = 64
GLA_CHUNK = 16
SEQ_BLOCK = 512
TOK_BLOCK = 512
MOE_BLOCK = 512
VMEM_LIMIT = 56 * 1024 * 1024


def _dot(a, b, precision=None):
    return jnp.dot(a, b, preferred_element_type=F32, precision=precision)


def _dot_nt(a, b, precision=None):
    return lax.dot_general(a, b, (((1,), (1,)), ((), ())), preferred_element_type=F32, precision=precision)


def _dot_tn(a, b, precision=None):
    return lax.dot_general(a, b, (((0,), (0,)), ((), ())), preferred_element_type=F32, precision=precision)


def _sigmoid(x):
    return 1.0 / (1.0 + jnp.exp(-x))


def _softplus(x):
    return jnp.maximum(x, 0.0) + jnp.log(1.0 + jnp.exp(-jnp.abs(x)))


def _params(sem):
    return pltpu.CompilerParams(dimension_semantics=sem, vmem_limit_bytes=VMEM_LIMIT)


ROW_TILE = D_MODEL // 128


def _store_row_tiles(ref, x):
    m = x.shape[0]
    for c in range(ROW_TILE):
        ref[pl.ds(c, m, stride=ROW_TILE), :] = x[:, c * 128 : (c + 1) * 128]


def _load_row_tiles(ref, m):
    return jnp.concatenate([ref[pl.ds(c, m, stride=ROW_TILE), :] for c in range(ROW_TILE)], axis=-1)


def _inproj_kernel(x_ref, gain_ref, wr_ref, wg_ref, zr_ref, zg_ref):
    x = x_ref[...]
    xn = x * lax.rsqrt(jnp.mean(x * x, axis=-1, keepdims=True) + NORM_EPS) * gain_ref[...]
    xb = xn.astype(BF16)
    zr_ref[...] = _dot(xb, wr_ref[...])
    zg_ref[...] = _dot(xb, wg_ref[...])


def _inproj(x, gain, w_r, w_g, tm):
    n = x.shape[0]
    return pl.pallas_call(
        _inproj_kernel,
        grid=(n // tm,),
        in_specs=[
            pl.BlockSpec((tm, D_MODEL), lambda i: (i, 0)),
            pl.BlockSpec((1, D_MODEL), lambda i: (0, 0)),
            pl.BlockSpec((D_MODEL, RW_PROJ), lambda i: (0, 0)),
            pl.BlockSpec((D_MODEL, GLA_PROJ_PAD), lambda i: (0, 0)),
        ],
        out_specs=[
            pl.BlockSpec((tm, RW_PROJ), lambda i: (i, 0)),
            pl.BlockSpec((tm, GLA_PROJ_PAD), lambda i: (i, 0)),
        ],
        out_shape=[
            jax.ShapeDtypeStruct((n, RW_PROJ), F32),
            jax.ShapeDtypeStruct((n, GLA_PROJ_PAD), F32),
        ],
        compiler_params=_params(("parallel",)),
    )(x, gain, w_r, w_g)


def _rwkv_features(zs, w0, w2p, a0, a2p, g2, k_k, k_a):
    W = RW_WIDTH
    r = zs[:, 0:W]
    k_raw = zs[:, W : 2 * W]
    v = zs[:, 2 * W : 3 * W]
    zwa = zs[:, 3 * W : 3 * W + 128]
    zg = zs[:, 3 * W + 128 :]
    w = -_softplus(-(w0 + _dot(jnp.tanh(zwa).astype(BF16), w2p))) - 0.5
    log_decay = -jnp.exp(w)
    a = _sigmoid(a0 + _dot(zwa.astype(BF16), a2p))
    g = _dot(_sigmoid(zg).astype(BF16), g2)
    kk_raw = k_raw * k_k
    k = k_raw * (1.0 + (a - 1.0) * k_a)
    return r, k, v, kk_raw, a, log_decay, g


def _level_mask(ri, ci, lvl):
    same = (ri >> (lvl + 1)) == (ci >> (lvl + 1))
    return same & (((ri >> lvl) & 1) == 1) & (((ci >> lvl) & 1) == 0)


def _rwkv_seq_kernel(z_ref, shift0_ref, s0_ref, mu_ref, w0_ref, w2_ref, a0_ref, a2_ref, g2_ref, kk_ref, ka_ref,
                     rk_ref, lnw_ref, lnb_ref, o_ref, sout_ref,
                     m_scr, prev_scr, r_scr, k_scr, v_scr, kkr_scr, a_scr, lw_scr, on_scr, bon_scr):
    C = RW_CHUNK
    N = RW_N
    t_idx = pl.program_id(1)
    tb = z_ref.shape[1]
    zero_nn = jnp.zeros((N, N), F32)

    @pl.when(t_idx == 0)
    def _():
        prev_scr[...] = shift0_ref[0]
        for p in range(RW_HEADS // 2):
            top = jnp.concatenate([s0_ref[0, 2 * p].T, zero_nn], axis=1)
            bot = jnp.concatenate([zero_nn, s0_ref[0, 2 * p + 1].T], axis=1)
            m_scr[p] = jnp.concatenate([top, bot], axis=0)

    z = z_ref[0]
    row = lax.broadcasted_iota(jnp.int32, z.shape, 0)
    z_prev = jnp.where(row == 0, prev_scr[...], pltpu.roll(z, 1, axis=0))
    prev_scr[...] = z[tb - 1 : tb, :]
    zs = z + mu_ref[...] * (z_prev - z)
    r, k, v, kk_raw, a, log_decay, g = _rwkv_features(
        zs, w0_ref[...], w2_ref[...], a0_ref[...], a2_ref[...], g2_ref[...], kk_ref[...], ka_ref[...])
    r_scr[...] = r
    k_scr[...] = k
    v_scr[...] = v
    kkr_scr[...] = kk_raw
    a_scr[...] = a
    lw_scr[...] = log_decay

    P2 = 2 * N
    ri = lax.broadcasted_iota(jnp.int32, (C, P2), 0)
    ci = lax.broadcasted_iota(jnp.int32, (C, P2), 1) % N
    left = lax.broadcasted_iota(jnp.int32, (C, P2), 1) < N
    tril = ri >= ci
    stril = ri > ci
    eye_f = (ri == ci).astype(F32)
    rb = lax.broadcasted_iota(jnp.int32, (P2, P2), 0)
    cb = lax.broadcasted_iota(jnp.int32, (P2, P2), 1)
    same_head = (rb < N) == (cb < N)
    eye_b = rb == cb
    rc = lax.broadcasted_iota(jnp.int32, (C, C), 0)
    cc = lax.broadcasted_iota(jnp.int32, (C, C), 1)
    tril_f = (rc >= cc).astype(F32)
    rk_all = rk_ref[...]

    def bdiag(x):
        return jnp.concatenate([jnp.where(left, x, 0.0), jnp.where(left, 0.0, x)], axis=0)

    def head_sum(x):
        s0 = jnp.sum(jnp.where(left, x, 0.0), axis=-1, keepdims=True)
        s1 = jnp.sum(jnp.where(left, 0.0, x), axis=-1, keepdims=True)
        return jnp.where(left, s0, s1)

    n_sub = tb // C
    pairs = range(RW_HEADS // 2)

    def chunk_body(it, carry):
        units = [(s, p) for s in range(n_sub) for p in pairs]
        sls = [pl.ds(pl.multiple_of((it * n_sub + s) * C, C), C) for s in range(n_sub)]
        prep = []
        for s in range(n_sub):
            lw = lw_scr[sls[s], :]
            cum = _dot(tril_f, lw, precision=HIGHEST)
            cum_last = cum[C - 1 : C, :]
            prep.append(dict(
                e_incl=jnp.exp(cum), e_excl=jnp.exp(cum - lw), e_neg=jnp.exp(-cum),
                e_tail=jnp.exp(cum_last - cum), p_last=jnp.exp(cum_last),
                r=r_scr[sls[s], :], k=k_scr[sls[s], :], v=v_scr[sls[s], :], kk=kkr_scr[sls[s], :],
                a=a_scr[sls[s], :]))
        lanes = [slice(p * P2, (p + 1) * P2) for p in pairs]
        get = lambda name: [prep[s][name][:, lanes[p]] for s, p in units]
        r2, k2, v2, kk2, a2 = get("r"), get("k"), get("v"), get("kk"), get("a")
        e_incl, e_excl, e_neg, e_tail, p_last = get("e_incl"), get("e_excl"), get("e_neg"), get("e_tail"), get("p_last")
        un = range(len(units))
        al = [kk2[u] / jnp.maximum(jnp.sqrt(head_sum(kk2[u] * kk2[u])), 1e-12) for u in un]
        be = [al[u] * a2[u] for u in un]
        al_t = [al[u] * e_excl[u] for u in un]
        r_t = [r2[u] * e_incl[u] for u in un]
        be_n = [be[u] * e_neg[u] for u in un]
        k_n = [k2[u] * e_neg[u] for u in un]
        k_et = [(k2[u] * e_tail[u]).T for u in un]
        be_et = [(be[u] * e_tail[u]).T for u in un]
        v_bd = [bdiag(v2[u]) for u in un]
        lhs = [jnp.concatenate([al_t[u], r_t[u]], axis=0) for u in un]
        s_b = [_dot_nt(lhs[u], bdiag(be_n[u])) for u in un]
        s_k = [_dot_nt(lhs[u], bdiag(k_n[u])) for u in un]
        l_ab = [jnp.where(stril, s_b[u][:C], 0.0) for u in un]
        a_rb = [jnp.where(tril, s_b[u][C:], 0.0) for u in un]
        l_ak = [jnp.where(stril, s_k[u][:C], 0.0) for u in un]
        a_rk = [jnp.where(tril, s_k[u][C:], 0.0) for u in un]
        lakv = [_dot(l_ak[u], v_bd[u]) for u in un]
        arkv = [_dot(a_rk[u], v_bd[u]) for u in un]
        kev = [_dot(k_et[u], v2[u]) for u in un]
        t_inv = [eye_f - jnp.where(_level_mask(ri, ci, 0), l_ab[u], 0.0) for u in un]
        lvl = 1
        while (1 << lvl) < C:
            lm = _level_mask(ri, ci, lvl)
            tn = [_dot(t_inv[u], bdiag(jnp.where(lm, l_ab[u], 0.0))) for u in un]
            t_inv = [t_inv[u] - _dot(tn[u], bdiag(t_inv[u])) for u in un]
            lvl += 1
        a_til = [_dot(t_inv[u], bdiag(al_t[u])) for u in un]
        b_til = [_dot(t_inv[u], bdiag(lakv[u])) for u in un]
        r_hat = [r_t[u] - _dot(a_rb[u], bdiag(a_til[u])) for u in un]
        o_hat = [arkv[u] - _dot(a_rb[u], bdiag(b_til[u])) for u in un]
        g_bd = [jnp.where(same_head, jnp.where(eye_b, p_last[u], 0.0) - _dot(be_et[u], a_til[u]), 0.0) for u in un]
        h_bd = [jnp.where(same_head, kev[u] - _dot(be_et[u], b_til[u]), 0.0) for u in un]
        lhs_m = [jnp.concatenate([r_hat[u], g_bd[u]], axis=0) for u in un]
        for u, (s, p) in enumerate(units):
            res = _dot(lhs_m[u], m_scr[p])
            m_scr[p] = res[C:] + h_bd[u]
            o_p = res[:C] + o_hat[u]
            cen = o_p - head_sum(o_p) * (1.0 / N)
            var = head_sum(cen * cen) * (1.0 / N)
            on_scr[sls[s], lanes[p]] = cen * lax.rsqrt(var + RW_GN_EPS)
            bon_scr[sls[s], lanes[p]] = head_sum(r2[u] * k2[u] * rk_all[:, lanes[p]]) * v2[u]
        return carry

    lax.fori_loop(0, tb // (C * n_sub), chunk_body, 0)
    out = (on_scr[...] * lnw_ref[...] + lnb_ref[...] + bon_scr[...]) * g
    o_ref[0] = out.astype(o_ref.dtype)

    @pl.when(t_idx == pl.num_programs(1) - 1)
    def _():
        for p in range(RW_HEADS // 2):
            m = m_scr[p]
            sout_ref[0, 2 * p] = m[:N, :N].T
            sout_ref[0, 2 * p + 1] = m[N:, N:].T


def _rwkv_seq(z3, shift0, s0, rw, tb):
    b, l, _ = z3.shape
    const = lambda shape: pl.BlockSpec(shape, lambda i, j: (0,) * len(shape))
    wide = lambda: pltpu.VMEM((tb, RW_WIDTH), F32)
    return pl.pallas_call(
        _rwkv_seq_kernel,
        grid=(b, l // tb),
        in_specs=[
            pl.BlockSpec((1, tb, RW_PROJ), lambda i, j: (i, j, 0)),
            pl.BlockSpec((1, 1, RW_PROJ), lambda i, j: (i, 0, 0)),
            pl.BlockSpec((1, RW_HEADS, RW_N, RW_N), lambda i, j: (i, 0, 0, 0)),
            const((1, RW_PROJ)),
            const((1, RW_WIDTH)), const((128, RW_WIDTH)),
            const((1, RW_WIDTH)), const((128, RW_WIDTH)),
            const((128, RW_WIDTH)),
            const((1, RW_WIDTH)), const((1, RW_WIDTH)), const((1, RW_WIDTH)),
            const((1, RW_WIDTH)), const((1, RW_WIDTH)),
        ],
        out_specs=[
            pl.BlockSpec((1, tb, RW_WIDTH), lambda i, j: (i, j, 0)),
            pl.BlockSpec((1, RW_HEADS, RW_N, RW_N), lambda i, j: (i, 0, 0, 0)),
        ],
        out_shape=[
            jax.ShapeDtypeStruct((b, l, RW_WIDTH), BF16),
            jax.ShapeDtypeStruct((b, RW_HEADS, RW_N, RW_N), F32),
        ],
        scratch_shapes=[
            pltpu.VMEM((RW_HEADS // 2, 2 * RW_N, 2 * RW_N), F32),
            pltpu.VMEM((1, RW_PROJ), F32),
            wide(), wide(), wide(), wide(), wide(), wide(), wide(), wide(),
        ],
        compiler_params=_params(("parallel", "arbitrary")),
    )(z3, shift0, s0, rw["mu"], rw["w0"], rw["w2p"], rw["a0"], rw["a2p"], rw["g2"], rw["k_k"], rw["k_a"],
      rw["r_k"], rw["ln_w"], rw["ln_b"])


def _rwkv_step_prep_kernel(z_ref, shift0_ref, mu_ref, w0_ref, w2_ref, a0_ref, a2_ref, g2_ref, kk_ref, ka_ref,
                           rk_ref, r_ref, k_ref, v_ref, al_ref, be_ref, dec_ref, g_ref, bon_ref):
    z = z_ref[...]
    zs = z + mu_ref[...] * (shift0_ref[...] - z)
    r, k, v, kk_raw, a, log_decay, g = _rwkv_features(
        zs, w0_ref[...], w2_ref[...], a0_ref[...], a2_ref[...], g2_ref[...], kk_ref[...], ka_ref[...])
    rk_all = rk_ref[...]
    for h in range(RW_HEADS):
        hs = slice(h * RW_N, (h + 1) * RW_N)
        kk_h = kk_raw[:, hs]
        nrm = jnp.sqrt(jnp.sum(kk_h * kk_h, axis=-1, keepdims=True))
        al = kk_h / jnp.maximum(nrm, 1e-12)
        al_ref[:, hs] = al
        be_ref[:, hs] = al * a[:, hs]
        bon_ref[:, hs] = jnp.sum(r[:, hs] * k[:, hs] * rk_all[:, hs], axis=-1, keepdims=True) * v[:, hs]
    r_ref[...] = r
    k_ref[...] = k
    v_ref[...] = v
    dec_ref[...] = jnp.exp(log_decay)
    g_ref[...] = g


def _rwkv_step_prep(z, shift0, rw):
    n = z.shape[0]
    out = jax.ShapeDtypeStruct((n, RW_WIDTH), F32)
    return pl.pallas_call(
        _rwkv_step_prep_kernel,
        out_shape=[out] * 8,
        compiler_params=pltpu.CompilerParams(vmem_limit_bytes=VMEM_LIMIT),
    )(z, shift0, rw["mu"], rw["w0"], rw["w2p"], rw["a0"], rw["a2p"], rw["g2"], rw["k_k"], rw["k_a"], rw["r_k"])


def _rwkv_step_kernel(s_ref, r_ref, k_ref, al_ref, be_ref, dec_ref, v_ref, snew_ref, o_ref):
    r, k, al, be, dec = r_ref[...], k_ref[...], al_ref[...], be_ref[...], dec_ref[...]

    def body(g, carry):
        rows = pl.ds(pl.multiple_of(g * 8, 8), 8)
        v8 = v_ref[rows, :]
        outs = []
        for j in range(8):
            s = s_ref[0, g * 8 + j]
            sa = -jnp.sum(s * al, axis=0, keepdims=True)
            s_new = s * dec + sa * be + v8[j : j + 1, :] * k
            snew_ref[0, g * 8 + j] = s_new
            outs.append(jnp.sum(s_new * r, axis=0, keepdims=True))
        o_ref[rows, :] = jnp.concatenate(outs, axis=0)
        return carry

    lax.fori_loop(0, RW_N // 8, body, 0)


def _rwkv_step(s0, r, k, al, be, dec, v):
    n = s0.shape[0]
    s_t = jnp.transpose(s0, (1, 2, 3, 0))
    s_spec = pl.BlockSpec((1, RW_N, RW_N, n), lambda h: (h, 0, 0, 0))
    op_spec = pl.BlockSpec((RW_N, n), lambda h: (h, 0))
    s_new_t, o_t = pl.pallas_call(
        _rwkv_step_kernel,
        grid=(RW_HEADS,),
        in_specs=[s_spec] + [op_spec] * 6,
        out_specs=[s_spec, op_spec],
        out_shape=[
            jax.ShapeDtypeStruct(s_t.shape, F32),
            jax.ShapeDtypeStruct((RW_WIDTH, n), F32),
        ],
        compiler_params=_params(("parallel",)),
    )(s_t, r.T, k.T, al.T, be.T, dec.T, v.T)
    return jnp.transpose(s_new_t, (3, 0, 1, 2)), o_t.T


def _gla_features(z, gkw, gkb):
    q = z[:, 0:GLA_QK] * (GLA_DK ** -0.5)
    k = z[:, GLA_QK : 2 * GLA_QK]
    v = z[:, 2 * GLA_QK : 2 * GLA_QK + GLA_WIDTH]
    g = z[:, 2 * GLA_QK + GLA_WIDTH : 2 * GLA_QK + 2 * GLA_WIDTH]
    zgk = z[:, 2 * GLA_QK + 2 * GLA_WIDTH :]
    gk = -_softplus(-(_dot(zgk.astype(BF16), gkw) + gkb)) / GLA_GATE_NORMALIZER
    return q, k, v, g, gk


def _gla_finish(o, g, norm_w):
    outs = []
    for h in range(GLA_HEADS):
        hs = slice(h * GLA_DV, (h + 1) * GLA_DV)
        o_h = o[:, hs]
        o_h = o_h * lax.rsqrt(jnp.mean(o_h * o_h, axis=-1, keepdims=True) + NORM_EPS) * norm_w
        g_h = g[:, hs]
        outs.append(o_h * (g_h * _sigmoid(g_h)))
    return jnp.concatenate(outs, axis=-1)


def _gla_seq_kernel(z_ref, s0_ref, gkw_ref, gkb_ref, nw_ref, wsel_ref, o_ref, sout_ref,
                    st_scr, x_scr, q_scr, k_scr, v_scr, gc_scr, oi_scr):
    C = GLA_CHUNK
    G = 128
    t_idx = pl.program_id(1)
    tb = z_ref.shape[1]
    nc = tb // C

    @pl.when(t_idx == 0)
    def _():
        for h in range(GLA_HEADS):
            st_scr[h] = s0_ref[0, h].T

    q, k, v, g, gk = _gla_features(z_ref[0], gkw_ref[...], gkb_ref[...])
    ri = lax.broadcasted_iota(jnp.int32, (G, G), 0)
    ci = lax.broadcasted_iota(jnp.int32, (G, G), 1)
    cum_mat = ((ri // C == ci // C) & (ri >= ci)).astype(F32)
    for m in range(tb // G):
        rows = slice(m * G, (m + 1) * G)
        gc_scr[rows, :] = _dot(cum_mat, gk[rows, :], precision=HIGHEST)
    q_scr[...] = q
    k_scr[...] = k
    v_scr[...] = v
    gcum = gc_scr[...]

    rg = lax.broadcasted_iota(jnp.int32, (tb, 2 * G), 0)
    cg = lax.broadcasted_iota(jnp.int32, (tb, 2 * G), 1)
    blk_mask = ((cg % G) // C == (rg % G) // C) & (cg % C <= rg % C)
    for p in range(GLA_HEADS // 2):
        ls = slice(p * 128, (p + 1) * 128)
        q3 = q[:, ls].reshape(nc, C, 128)
        k3 = k[:, ls].reshape(nc, C, 128)
        g3 = gcum[:, ls].reshape(nc, C, 128)
        for j in range(C):
            e = q3 * jnp.exp(jnp.minimum(g3 - g3[:, j : j + 1, :], 0.0)) * k3[:, j : j + 1, :]
            x_scr[:, j * 128 : (j + 1) * 128] = e.reshape(tb, 128).astype(BF16)
        a_t = jnp.where(blk_mask, _dot(x_scr[...], wsel_ref[...]), 0.0).astype(BF16)
        for hl in range(2):
            h = 2 * p + hl
            for m in range(tb // G):
                rows = slice(m * G, (m + 1) * G)
                a_blk = a_t[rows, hl * G : (hl + 1) * G]
                oi_scr[rows, h * GLA_DV : (h + 1) * GLA_DV] = _dot(
                    a_blk, v[rows, h * GLA_DV : (h + 1) * GLA_DV].astype(BF16))

    def chunk_body(c, carry):
        sl = pl.ds(pl.multiple_of(c * C, C), C)
        g_c = gc_scr[sl, :]
        q_c = q_scr[sl, :]
        k_c = k_scr[sl, :]
        v_c = v_scr[sl, :]
        for h in range(GLA_HEADS):
            ks = slice(h * GLA_DK, (h + 1) * GLA_DK)
            vs = slice(h * GLA_DV, (h + 1) * GLA_DV)
            g_h = g_c[:, ks]
            g_last = g_h[C - 1 : C, :]
            st = st_scr[h]
            oi_scr[sl, vs] += _dot_nt(q_c[:, ks] * jnp.exp(g_h), st)
            k_dec = k_c[:, ks] * jnp.exp(g_last - g_h)
            st_scr[h] = st * jnp.exp(g_last) + _dot_tn(v_c[:, vs], k_dec)
        return carry

    lax.fori_loop(0, nc, chunk_body, 0, unroll=8)
    o_ref[0] = _gla_finish(oi_scr[...], g, nw_ref[...]).astype(o_ref.dtype)

    @pl.when(t_idx == pl.num_programs(1) - 1)
    def _():
        for h in range(GLA_HEADS):
            sout_ref[0, h] = st_scr[h].T


def _gla_select_matrix():
    j = jnp.arange(GLA_CHUNK)[:, None, None]
    hl = jnp.arange(2)[None, :, None]
    rows_j = jnp.broadcast_to(j, (GLA_CHUNK, 2, GLA_DK)).reshape(-1)
    rows_h = jnp.broadcast_to(hl, (GLA_CHUNK, 2, GLA_DK)).reshape(-1)
    cols = jnp.arange(256)
    sel = (rows_j[:, None] == cols[None, :] % GLA_CHUNK) & (rows_h[:, None] == cols[None, :] // 128)
    return sel.astype(BF16)


def _gla_seq(z3, s0, gl, tb):
    b, l, _ = z3.shape
    const = lambda shape: pl.BlockSpec(shape, lambda i, j: (0,) * len(shape))
    return pl.pallas_call(
        _gla_seq_kernel,
        grid=(b, l // tb),
        in_specs=[
            pl.BlockSpec((1, tb, GLA_PROJ_PAD), lambda i, j: (i, j, 0)),
            pl.BlockSpec((1, GLA_HEADS, GLA_DK, GLA_DV), lambda i, j: (i, 0, 0, 0)),
            const((GLA_LORA_PAD, GLA_QK)), const((1, GLA_QK)), const((1, GLA_DV)),
            const((GLA_CHUNK * 128, 256)),
        ],
        out_specs=[
            pl.BlockSpec((1, tb, GLA_WIDTH), lambda i, j: (i, j, 0)),
            pl.BlockSpec((1, GLA_HEADS, GLA_DK, GLA_DV), lambda i, j: (i, 0, 0, 0)),
        ],
        out_shape=[
            jax.ShapeDtypeStruct((b, l, GLA_WIDTH), BF16),
            jax.ShapeDtypeStruct((b, GLA_HEADS, GLA_DK, GLA_DV), F32),
        ],
        scratch_shapes=[
            pltpu.VMEM((GLA_HEADS, GLA_DV, GLA_DK), F32),
            pltpu.VMEM((tb, GLA_CHUNK * 128), BF16),
            pltpu.VMEM((tb, GLA_QK), F32), pltpu.VMEM((tb, GLA_QK), F32), pltpu.VMEM((tb, GLA_WIDTH), F32),
            pltpu.VMEM((tb, GLA_QK), F32), pltpu.VMEM((tb, GLA_WIDTH), F32),
        ],
        compiler_params=_params(("parallel", "arbitrary")),
    )(z3, s0, gl["gkw"], gl["gkb"], gl["norm_w"], _gla_select_matrix())


def _gla_step_prep_kernel(z_ref, gkw_ref, gkb_ref, q_ref, k_ref, v_ref, g_ref, dec_ref):
    q, k, v, g, gk = _gla_features(z_ref[...], gkw_ref[...], gkb_ref[...])
    q_ref[...] = q
    k_ref[...] = k
    v_ref[...] = v
    g_ref[...] = g
    dec_ref[...] = jnp.exp(gk)


def _gla_step_prep(z, gl):
    n = z.shape[0]
    qk = jax.ShapeDtypeStruct((n, GLA_QK), F32)
    wide = jax.ShapeDtypeStruct((n, GLA_WIDTH), F32)
    return pl.pallas_call(
        _gla_step_prep_kernel,
        out_shape=[qk, qk, wide, wide, qk],
        compiler_params=pltpu.CompilerParams(vmem_limit_bytes=VMEM_LIMIT),
    )(z, gl["gkw"], gl["gkb"])


def _gla_step_kernel(s_ref, qcol_ref, kcol_ref, dcol_ref, vrow_ref, snew_ref, orow_ref):
    s_new = s_ref[...] * dcol_ref[...] + kcol_ref[...] * vrow_ref[...]
    snew_ref[...] = s_new
    orow_ref[...] = jnp.sum(s_new * qcol_ref[...], axis=2, keepdims=True)


def _gla_step(s0, q, k, dec, v, bb):
    n = s0.shape[0]
    colv = lambda t: t.reshape(n, GLA_HEADS, GLA_DK, 1)
    col_spec = pl.BlockSpec((bb, GLA_HEADS, GLA_DK, 1), lambda i: (i, 0, 0, 0))
    row_spec = pl.BlockSpec((bb, GLA_HEADS, 1, GLA_DV), lambda i: (i, 0, 0, 0))
    s_spec = pl.BlockSpec((bb, GLA_HEADS, GLA_DK, GLA_DV), lambda i: (i, 0, 0, 0))
    s_new, o_row = pl.pallas_call(
        _gla_step_kernel,
        grid=(n // bb,),
        in_specs=[s_spec, col_spec, col_spec, col_spec, row_spec],
        out_specs=[s_spec, row_spec],
        out_shape=[
            jax.ShapeDtypeStruct(s0.shape, F32),
            jax.ShapeDtypeStruct((n, GLA_HEADS, 1, GLA_DV), F32),
        ],
        compiler_params=_params(("parallel",)),
    )(s0, colv(q), colv(k), colv(dec), v.reshape(n, GLA_HEADS, 1, GLA_DV))
    return s_new, o_row.reshape(n, GLA_WIDTH)


def _step_post_kernel(orw_ref, bon_ref, grw_ref, lnw_ref, lnb_ref, ogl_ref, ggl_ref, nw_ref, o_rw_ref, o_gl_ref):
    o = orw_ref[...]
    for h in range(RW_HEADS):
        hs = slice(h * RW_N, (h + 1) * RW_N)
        o_h = o[:, hs]
        mean = jnp.mean(o_h, axis=-1, keepdims=True)
        cen = o_h - mean
        var = jnp.mean(cen * cen, axis=-1, keepdims=True)
        on = cen * lax.rsqrt(var + RW_GN_EPS)
        res = (on * lnw_ref[:, hs] + lnb_ref[:, hs] + bon_ref[:, hs]) * grw_ref[:, hs]
        o_rw_ref[:, hs] = res.astype(o_rw_ref.dtype)
    o_gl_ref[...] = _gla_finish(ogl_ref[...], ggl_ref[...], nw_ref[...]).astype(o_gl_ref.dtype)


def _step_post(o_rw, bonus, g_rw, rw, o_gl, g_gl, gl):
    n = o_rw.shape[0]
    return pl.pallas_call(
        _step_post_kernel,
        out_shape=[jax.ShapeDtypeStruct((n, RW_WIDTH), BF16), jax.ShapeDtypeStruct((n, GLA_WIDTH), BF16)],
        compiler_params=pltpu.CompilerParams(vmem_limit_bytes=VMEM_LIMIT),
    )(o_rw, bonus, g_rw, rw["ln_w"], rw["ln_b"], o_gl, g_gl, gl["norm_w"])


def _outproj_router_kernel(x_ref, orw_ref, ogl_ref, wo_ref, gain_ref, wrt_ref, br_ref,
                           h_ref, xn_ref, idx_ref, gate_ref, rank_ref, cnt_ref):
    tm = x_ref.shape[0]
    mix = jnp.concatenate([orw_ref[...], ogl_ref[...]], axis=-1)
    h = x_ref[...] + _dot(mix, wo_ref[...])
    h_ref[...] = h
    xn = h * lax.rsqrt(jnp.mean(h * h, axis=-1, keepdims=True) + NORM_EPS) * gain_ref[...]
    _store_row_tiles(xn_ref, xn)
    logits = _dot_nt(wrt_ref[...], xn, precision=HIGHEST) + br_ref[...]
    eidx = lax.broadcasted_iota(jnp.int32, logits.shape, 0)
    ti = lax.broadcasted_iota(jnp.int32, (tm, tm), 0)
    tj = lax.broadcasted_iota(jnp.int32, (tm, tm), 1)
    before = (ti < tj).astype(BF16)
    vals, idxs = [], []
    work = logits
    chosen = jnp.zeros(logits.shape, F32)
    for _ in range(TOP_K):
        m = jnp.max(work, axis=0, keepdims=True)
        sel = jnp.min(jnp.where(work == m, eidx, N_EXPERTS), axis=0, keepdims=True)
        hit = eidx == sel
        work = jnp.where(hit, -jnp.inf, work)
        chosen = chosen + hit.astype(F32)
        vals.append(m)
        idxs.append(sel)
    prefix = _dot(chosen.astype(BF16), before)
    exps = [jnp.exp(v - vals[0]) for v in vals]
    denom = exps[0] + exps[1] + exps[2] + exps[3]
    for j in range(TOP_K):
        idx_ref[0, j : j + 1, :] = idxs[j]
        gate_ref[0, j : j + 1, :] = exps[j] / denom
        rank = jnp.sum(jnp.where(eidx == idxs[j], prefix, 0.0), axis=0, keepdims=True)
        rank_ref[0, j : j + 1, :] = rank.astype(jnp.int32)
    cnt = jnp.sum(chosen, axis=1, keepdims=True)
    cnt_ref[0] = jnp.broadcast_to(cnt, (N_EXPERTS, 128)).astype(jnp.int32)


def _outproj_router(x, o_rw, o_gl, w_out, gain, w_router_t, b_router, tm):
    n = x.shape[0]
    nt = n // tm
    const = lambda shape: pl.BlockSpec(shape, lambda i: (0,) * len(shape))
    tok = lambda width: pl.BlockSpec((tm, width), lambda i: (i, 0))
    lane = pl.BlockSpec((1, TOP_K, tm), lambda i: (i, 0, 0))
    return pl.pallas_call(
        _outproj_router_kernel,
        grid=(nt,),
        in_specs=[
            tok(D_MODEL), tok(RW_WIDTH), tok(GLA_WIDTH),
            const((D_MODEL, D_MODEL)), const((1, D_MODEL)), const((N_EXPERTS, D_MODEL)), const((N_EXPERTS, 1)),
        ],
        out_specs=[tok(D_MODEL), pl.BlockSpec((tm * ROW_TILE, 128), lambda i: (i, 0)), lane, lane, lane,
                   pl.BlockSpec((1, N_EXPERTS, 128), lambda i: (i, 0, 0))],
        out_shape=[
            jax.ShapeDtypeStruct((n, D_MODEL), F32),
            jax.ShapeDtypeStruct((n * ROW_TILE, 128), F32),
            jax.ShapeDtypeStruct((nt, TOP_K, tm), jnp.int32),
            jax.ShapeDtypeStruct((nt, TOP_K, tm), F32),
            jax.ShapeDtypeStruct((nt, TOP_K, tm), jnp.int32),
            jax.ShapeDtypeStruct((nt, N_EXPERTS, 128), jnp.int32),
        ],
        compiler_params=_params(("parallel",)),
    )(x, o_rw, o_gl, w_out, gain, w_router_t, b_router)


def _moe_kernel(be_ref, nu_ref, epoch_ref, next_ref, xs_ref, wg_hbm, wu_hbm, wd_hbm, bg_ref, bu_ref, bd_ref, y_ref,
                w_f32, wg_b, wu_b, wd_b, sems):
    b = pl.program_id(0)
    prev = be_ref[jnp.maximum(b - 1, 0)]
    new_expert = (b == 0) | (be_ref[b] != prev)

    def fetch(e, slot):
        return [pltpu.make_async_copy(w.at[e], w_f32.at[slot, i], sems.at[slot])
                for i, w in enumerate((wg_hbm, wu_hbm, wd_hbm))]

    @pl.when(b == 0)
    def _():
        for c in fetch(be_ref[0], 0):
            c.start()

    @pl.when(new_expert)
    def _():
        slot = epoch_ref[b] % 2
        for c in fetch(be_ref[b], slot):
            c.wait()

        @pl.when(next_ref[b] >= 0)
        def _():
            for c in fetch(next_ref[b], 1 - slot):
                c.start()

        wg_b[...] = w_f32[slot, 0].astype(BF16)
        wu_b[...] = w_f32[slot, 1].astype(BF16)
        wd_b[...] = w_f32[slot, 2].astype(BF16)

    @pl.when(b < nu_ref[0])
    def _():
        x = _load_row_tiles(xs_ref, MOE_BLOCK).astype(BF16)
        half = D_MODEL // 2
        acc = None
        for f in range(2):
            fs = slice(f * half, (f + 1) * half)
            gt = _dot(x, wg_b[:, fs]) + bg_ref[0, :, fs]
            up = _dot(x, wu_b[:, fs]) + bu_ref[0, :, fs]
            gt = jnp.minimum(gt, SWIGLU_LIMIT)
            up = jnp.clip(up, -SWIGLU_LIMIT, SWIGLU_LIMIT)
            hid = (up + 1.0) * gt * _sigmoid(SWIGLU_ALPHA * gt)
            part = _dot(hid.astype(BF16), wd_b[fs, :])
            acc = part if acc is None else acc + part
        _store_row_tiles(y_ref, acc + bd_ref[0])


def _moe_ffn(block_expert, n_used, xs, w_gate, w_up, w_down, b_gate, b_up, b_down):
    n_blocks = block_expert.shape[0]
    pos = jnp.arange(n_blocks, dtype=jnp.int32)
    change = (pos > 0) & (block_expert != jnp.roll(block_expert, 1))
    epoch = jnp.cumsum(change.astype(jnp.int32))
    later = change[None, :] & (pos[None, :] > pos[:, None])
    first = jnp.min(jnp.where(later, pos[None, :], n_blocks), axis=1)
    next_e = jnp.sum(jnp.where(pos[None, :] == first[:, None], block_expert[None, :], 0), axis=1)
    next_e = jnp.where(first < n_blocks, next_e, -1).astype(jnp.int32)

    row = lambda b, be, nu, ep, nx: (jnp.minimum(b, nu[0] - 1), 0)
    bspec = pl.BlockSpec((1, 1, D_MODEL), lambda b, be, nu, ep, nx: (be[b], 0, 0))
    wspec = pl.BlockSpec(memory_space=pl.ANY)
    grid_spec = pltpu.PrefetchScalarGridSpec(
        num_scalar_prefetch=4,
        grid=(n_blocks,),
        in_specs=[pl.BlockSpec((MOE_BLOCK * ROW_TILE, 128), row), wspec, wspec, wspec, bspec, bspec, bspec],
        out_specs=pl.BlockSpec((MOE_BLOCK * ROW_TILE, 128), row),
        scratch_shapes=[pltpu.VMEM((2, 3, D_MODEL, D_MODEL), F32)] + [pltpu.VMEM((D_MODEL, D_MODEL), BF16)] * 3
        + [pltpu.SemaphoreType.DMA((2,))],
    )
    return pl.pallas_call(
        _moe_kernel,
        grid_spec=grid_spec,
        out_shape=jax.ShapeDtypeStruct((n_blocks * MOE_BLOCK * ROW_TILE, 128), F32),
        compiler_params=_params(("arbitrary",)),
    )(block_expert, n_used, epoch, next_e, xs, w_gate, w_up, w_down,
      b_gate.reshape(N_EXPERTS, 1, D_MODEL), b_up.reshape(N_EXPERTS, 1, D_MODEL),
      b_down.reshape(N_EXPERTS, 1, D_MODEL))


SEG_ALIGN = 8
GROUP_ROWS = SEG_ALIGN * ROW_TILE


def _local_rows(tm):
    return tm * TOP_K + N_EXPERTS * SEG_ALIGN


def _group_rows(ref, group):
    start = group * GROUP_ROWS
    if not isinstance(group, int):
        start = pl.multiple_of(start, GROUP_ROWS)
    return ref.at[pl.ds(start, GROUP_ROWS)]


def _group_copy(hbm, hbm_group, buf, buf_group, sem, to_hbm):
    h, b = _group_rows(hbm, hbm_group), _group_rows(buf, buf_group)
    return pltpu.make_async_copy(b, h, sem) if to_hbm else pltpu.make_async_copy(h, b, sem)


def _combine_kernel(grp_c, grp_n, lpos_ref, gate_ref, h_ref, gain_ref, y_hbm, o_ref, ybuf, fbuf, sems):
    i = pl.program_id(0)
    nt = pl.num_programs(0)
    tm = h_ref.shape[0]
    n_groups = _local_rows(tm) // SEG_ALIGN

    def issue(grp_ref, slot):
        def body(g, carry):
            _group_copy(y_hbm, grp_ref[0, 0, g], ybuf.at[slot], g, sems.at[slot], False).start()
            return carry
        lax.fori_loop(0, grp_ref[0, 0, n_groups], body, 0)

    @pl.when(i == 0)
    def _():
        issue(grp_c, 0)

    @pl.when(i + 1 < nt)
    def _():
        issue(grp_n, (i + 1) % 2)

    slot = i % 2

    yb = ybuf.at[slot]
    rows_in = pl.ds(0, pl.multiple_of(grp_c[0, 0, n_groups] * GROUP_ROWS, GROUP_ROWS))
    pltpu.make_async_copy(y_hbm.at[rows_in], yb.at[rows_in], sems.at[slot]).wait()

    def token_body(t, carry):
        acc = None
        for j in range(TOP_K):
            row = pl.multiple_of(lpos_ref[0, 0, t * TOP_K + j], ROW_TILE)
            term = gate_ref[0, 0, t * TOP_K + j] * yb[pl.ds(row, ROW_TILE), :]
            acc = term if acc is None else acc + term
        fbuf[pl.ds(pl.multiple_of(t * ROW_TILE, ROW_TILE), ROW_TILE), :] = acc
        return carry

    lax.fori_loop(0, tm, token_body, 0, unroll=8)
    f = h_ref[...] + _load_row_tiles(fbuf, tm)
    o_ref[...] = f * lax.rsqrt(jnp.mean(f * f, axis=-1, keepdims=True) + NORM_EPS) * gain_ref[...]


def _combine(h, y_rows, grp3, lpos3, gate3, gain, tm):
    n = h.shape[0]
    nt = n // tm
    n_local = _local_rows(tm)
    gw = grp3.shape[-1]
    smem = lambda shape, imap: pl.BlockSpec(shape, imap, memory_space=pltpu.SMEM)
    cur = lambda i: (i, 0, 0)
    nxt = lambda i: (jnp.minimum(i + 1, nt - 1), 0, 0)
    return pl.pallas_call(
        _combine_kernel,
        grid=(nt,),
        in_specs=[
            smem((1, 1, gw), cur), smem((1, 1, gw), nxt),
            smem((1, 1, TOP_K * tm), cur), smem((1, 1, TOP_K * tm), cur),
            pl.BlockSpec((tm, D_MODEL), lambda i: (i, 0)),
            pl.BlockSpec((1, D_MODEL), lambda i: (0, 0)),
            pl.BlockSpec(memory_space=pl.ANY),
        ],
        out_specs=pl.BlockSpec((tm, D_MODEL), lambda i: (i, 0)),
        out_shape=jax.ShapeDtypeStruct((n, D_MODEL), F32),
        scratch_shapes=[pltpu.VMEM((2, n_local * ROW_TILE, 128), F32), pltpu.VMEM((tm * ROW_TILE, 128), F32),
                        pltpu.SemaphoreType.DMA((2,))],
        compiler_params=pltpu.CompilerParams(dimension_semantics=("arbitrary",), vmem_limit_bytes=VMEM_LIMIT,
                                             disable_bounds_checks=True),
    )(grp3, grp3, lpos3, gate3, h, gain, y_rows)


def _dispatch_kernel(*refs, fill):
    if fill:
        grp_ref, lpos_ref, ends_ref, x_ref, xs_hbm, sorted_scr, zero_scr, sem, zsem = refs
    else:
        grp_ref, lpos_ref, x_ref, _, xs_hbm, sorted_scr, sem = refs
    tm = x_ref.shape[0] // ROW_TILE
    n_groups = _local_rows(tm) // SEG_ALIGN
    blk = MOE_BLOCK * ROW_TILE

    if fill:
        def fill_copy(e):
            start = pl.multiple_of((ends_ref[0, e] - MOE_BLOCK) * ROW_TILE, blk)
            return pltpu.make_async_copy(zero_scr, xs_hbm.at[pl.ds(start, blk)], zsem)

        @pl.when(pl.program_id(0) == 0)
        def _():
            zero_scr[...] = jnp.zeros(zero_scr.shape, zero_scr.dtype)
            for e in range(N_EXPERTS):
                @pl.when(ends_ref[1, e] > 0)
                def _():
                    fill_copy(e).start()
            for e in range(N_EXPERTS):
                @pl.when(ends_ref[1, e] > 0)
                def _():
                    fill_copy(e).wait()

    sorted_scr[...] = jnp.zeros(sorted_scr.shape, sorted_scr.dtype)

    def move(t, carry):
        row = x_ref[pl.ds(pl.multiple_of(t * ROW_TILE, ROW_TILE), ROW_TILE), :]
        for j in range(TOP_K):
            dst = pl.multiple_of(lpos_ref[0, 0, t * TOP_K + j], ROW_TILE)
            sorted_scr[pl.ds(dst, ROW_TILE), :] = row
        return carry

    lax.fori_loop(0, tm, move, 0, unroll=8)
    n_used = grp_ref[0, 0, n_groups]

    def send(g, carry):
        _group_copy(xs_hbm, grp_ref[0, 0, g], sorted_scr, g, sem, True).start()
        return carry

    lax.fori_loop(0, n_used, send, 0)

    rows_out = pl.ds(0, pl.multiple_of(n_used * GROUP_ROWS, GROUP_ROWS))
    pltpu.make_async_copy(sorted_scr.at[rows_out], xs_hbm.at[rows_out], sem).wait()


def _dispatch(xn, grp3, lpos3, tm, n_slots, ends=None, xs=None):
    n = xn.shape[0] // ROW_TILE
    fill = xs is None
    smem = lambda shape, imap: pl.BlockSpec(shape, imap, memory_space=pltpu.SMEM)
    in_specs = [smem((1, 1, grp3.shape[-1]), lambda i: (i, 0, 0)), smem((1, 1, TOP_K * tm), lambda i: (i, 0, 0))]
    args = [grp3, lpos3]
    scratch = [pltpu.VMEM((_local_rows(tm) * ROW_TILE, 128), F32)]
    if fill:
        in_specs.append(smem((2, N_EXPERTS), lambda i: (0, 0)))
        args.append(ends)
        scratch.append(pltpu.VMEM((MOE_BLOCK * ROW_TILE, 128), F32))
    in_specs.append(pl.BlockSpec((tm * ROW_TILE, 128), lambda i: (i, 0)))
    args.append(xn)
    aliases = {}
    if not fill:
        in_specs.append(pl.BlockSpec(memory_space=pl.ANY))
        args.append(xs)
        aliases = {len(args) - 1: 0}
    scratch.append(pltpu.SemaphoreType.DMA(()))
    if fill:
        scratch.append(pltpu.SemaphoreType.DMA(()))
    return pl.pallas_call(
        functools.partial(_dispatch_kernel, fill=fill),
        grid=(n // tm,),
        in_specs=in_specs,
        out_specs=pl.BlockSpec(memory_space=pl.ANY),
        out_shape=jax.ShapeDtypeStruct((n_slots * ROW_TILE, 128), F32),
        scratch_shapes=scratch,
        input_output_aliases=aliases,
        compiler_params=pltpu.CompilerParams(dimension_semantics=("arbitrary",), vmem_limit_bytes=VMEM_LIMIT,
                                             disable_bounds_checks=True, has_side_effects=True),
    )(*args)


def _pad_rows(w, rows, offset):
    out = jnp.zeros((rows, w.shape[1]), w.dtype)
    return out.at[offset : offset + w.shape[0]].set(w)


def _routing_tables(counts, n_pairs):
    n_tiles = counts.shape[0]
    n_blocks = (n_pairs + n_tiles * N_EXPERTS * (SEG_ALIGN - 1) + N_EXPERTS * (MOE_BLOCK - 1)
                + MOE_BLOCK - 1) // MOE_BLOCK
    runs = (counts + SEG_ALIGN - 1) // SEG_ALIGN * SEG_ALIGN
    local_start = jnp.cumsum(runs, axis=1) - runs
    total = jnp.sum(runs, axis=0)
    padded = (total + MOE_BLOCK - 1) // MOE_BLOCK * MOE_BLOCK
    pends = jnp.cumsum(padded)
    pstarts = pends - padded
    global_start = pstarts[None, :] + jnp.cumsum(runs, axis=0) - runs
    blocks = jnp.arange(n_blocks, dtype=jnp.int32) * MOE_BLOCK
    n_used = (pends[-1] // MOE_BLOCK).astype(jnp.int32)
    owner = jnp.sum((pends[None, :] <= blocks[:, None]).astype(jnp.int32), axis=1)
    block_expert = jnp.minimum(owner, N_EXPERTS - 1)
    last = jnp.sum(jnp.where(jnp.arange(n_blocks) == n_used - 1, block_expert, 0))
    block_expert = jnp.where(jnp.arange(n_blocks) < n_used, block_expert, last)
    ends = jnp.stack([pends, padded]).astype(jnp.int32)
    return runs, local_start, global_start, ends, block_expert.astype(jnp.int32), n_used.reshape(1), n_blocks


def _tile_tables(idx3, rank3, runs, local_start, global_start, tm):
    experts = jnp.arange(N_EXPERTS, dtype=jnp.int32)
    hit = idx3[..., None] == experts
    lpos = rank3 + jnp.sum(jnp.where(hit, local_start[:, None, None, :], 0), axis=-1)
    n_groups = _local_rows(tm) // SEG_ALIGN
    g = jnp.arange(n_groups, dtype=jnp.int32)
    run_end = (local_start + runs) // SEG_ALIGN
    owner = jnp.minimum(jnp.sum((run_end[:, None, :] <= g[None, :, None]).astype(jnp.int32), axis=-1), N_EXPERTS - 1)
    sel = owner[..., None] == experts
    offset = jnp.sum(jnp.where(sel, (global_start - local_start)[:, None, :], 0), axis=-1) // SEG_ALIGN
    used = jnp.sum(runs, axis=1) // SEG_ALIGN
    table = jnp.where(g[None, :] < used[:, None], g[None, :] + offset, 0)
    tail = jnp.broadcast_to(used[:, None], (used.shape[0], 8))
    lpos_rows = _token_major(lpos.astype(jnp.int32) * ROW_TILE)
    return lpos_rows, jnp.concatenate([table, tail], axis=1).astype(jnp.int32)[:, None, :]


def _token_major(t3):
    return jnp.swapaxes(t3, 1, 2).reshape(t3.shape[0], 1, -1)


def kernel(x_prompt, x_sample, state_rwkv_shift, state_rwkv_wkv, state_gla, norm_mix, w_in, rw_mu, rw_w0, rw_w2, rw_a0, rw_a2, rw_g2, rw_k_k, rw_k_a, rw_r_k, rw_ln_w, rw_ln_b, gla_gk_w2, gla_gk_b, gla_norm_w, w_out, norm_ffn, w_router, b_router, w_gate, b_gate, w_up, b_up, w_down, b_down, norm_final):
    depth = norm_mix.shape[0]
    assert depth == 1
    bp, lp, d = x_prompt.shape
    bs, ls, _ = x_sample.shape
    assert ls == 1 and lp % SEQ_BLOCK == 0
    l = 0
    row = lambda t: t.reshape(1, -1)

    w_in_b = w_in[l].astype(BF16)
    w_in_r = w_in_b[:, :RW_PROJ]
    w_in_g = jnp.pad(w_in_b[:, RW_PROJ:], ((0, 0), (0, GLA_PROJ_PAD - GLA_PROJ)))
    rw = dict(
        mu=row(rw_mu[l]), w0=row(rw_w0[l]), a0=row(rw_a0[l]),
        w2p=_pad_rows(rw_w2[l].astype(BF16), 128, 0), a2p=_pad_rows(rw_a2[l].astype(BF16), 128, 64),
        g2=rw_g2[l].astype(BF16), k_k=row(rw_k_k[l]), k_a=row(rw_k_a[l]), r_k=row(rw_r_k[l]),
        ln_w=row(rw_ln_w[l]), ln_b=row(rw_ln_b[l]))
    gl = dict(gkw=_pad_rows(gla_gk_w2[l].astype(BF16), GLA_LORA_PAD, 0), gkb=row(gla_gk_b[l]),
              norm_w=row(gla_norm_w[l]))
    gain_mix = row(norm_mix[l])

    n_p = bp * lp
    xp = x_prompt.reshape(n_p, d)
    zr_p, zg_p = _inproj(xp, gain_mix, w_in_r, w_in_g, TOK_BLOCK)
    zr_p3 = zr_p.reshape(bp, lp, RW_PROJ)
    o_rw_p, wkv_p = _rwkv_seq(zr_p3, jnp.zeros((bp, 1, RW_PROJ), F32),
                              jnp.zeros((bp, RW_HEADS, RW_N, RW_N), F32), rw, SEQ_BLOCK)
    o_gl_p, gla_p = _gla_seq(zg_p.reshape(bp, lp, GLA_PROJ_PAD),
                             jnp.zeros((bp, GLA_HEADS, GLA_DK, GLA_DV), F32), gl, SEQ_BLOCK)
    shift_p = zr_p3[:, -1, :]

    xs_ = x_sample.reshape(bs, d)
    zr_s, zg_s = _inproj(xs_, gain_mix, w_in_r, w_in_g, bs)
    r, k, v, al, be, dec, g_rw, bonus = _rwkv_step_prep(zr_s, state_rwkv_shift[l], rw)
    wkv_s, o_rw_s = _rwkv_step(state_rwkv_wkv[l], r, k, al, be, dec, v)
    q, kg, vg, g_gl, dec_g = _gla_step_prep(zg_s, gl)
    gla_s, o_gl_s = _gla_step(state_gla[l], q, kg, dec_g, vg, 16)
    o_rw_s2, o_gl_s2 = _step_post(o_rw_s, bonus, g_rw, rw, o_gl_s, g_gl, gl)
    shift_s = zr_s

    w_out_b = w_out[l].astype(BF16)
    router = (w_out_b, row(norm_ffn[l]), w_router[l].T, b_router[l].reshape(N_EXPERTS, 1))
    h_p, xn_p, idx_p, gate_p, rank_p, cnt_p = _outproj_router(
        xp, o_rw_p.reshape(n_p, RW_WIDTH), o_gl_p.reshape(n_p, GLA_WIDTH), *router, TOK_BLOCK)
    h_s, xn_s, idx_s, gate_s, rank_s, cnt_s = _outproj_router(xs_, o_rw_s2, o_gl_s2, *router, bs)
    nt_p = n_p // TOK_BLOCK
    counts = jnp.concatenate([cnt_p[:, :, 0], cnt_s[:, :, 0]], axis=0)
    runs, lstart, gstart, ends, block_expert, n_used, n_blocks = _routing_tables(counts, (n_p + bs) * TOP_K)
    n_slots = n_blocks * MOE_BLOCK
    lpos_p, grp_p = _tile_tables(idx_p, rank_p, runs[:nt_p], lstart[:nt_p], gstart[:nt_p], TOK_BLOCK)
    lpos_s, grp_s = _tile_tables(idx_s, rank_s, runs[nt_p:], lstart[nt_p:], gstart[nt_p:], bs)
    xs_rows = _dispatch(xn_p, grp_p, lpos_p, TOK_BLOCK, n_slots, ends=ends)
    xs_rows = _dispatch(xn_s, grp_s, lpos_s, bs, n_slots, xs=xs_rows)
    y_rows = _moe_ffn(block_expert, n_used, xs_rows, w_gate[l], w_up[l], w_down[l], b_gate[l], b_up[l], b_down[l])
    gain_f = row(norm_final)
    y_p = _combine(h_p, y_rows, grp_p, lpos_p, _token_major(gate_p), gain_f, TOK_BLOCK)
    y_s = _combine(h_s, y_rows, grp_s, lpos_s, _token_major(gate_s), gain_f, bs)

    y_prompt = y_p.reshape(bp, lp, d)
    y_sample = y_s.reshape(bs, ls, d)
    return (y_prompt, y_sample, shift_p[None], wkv_p[None], gla_p[None], shift_s[None], wkv_s[None], gla_s[None])
```

```python
import functools

import jax
import jax.numpy as jnp
from jax import lax
from jax.experimental import pallas as pl
from jax.experimental.pallas import tpu as pltpu

F32 = jnp.float32
BF16 = jnp.bfloat16
HIGHEST = lax.Precision.HIGHEST

D_MODEL = 1024
RW_WIDTH = 512
RW_HEADS = 8
RW_N = 64
RW_PROJ = 1792
RW_GN_EPS = 64e-5
GLA_HEADS = 4
GLA_DK = 64
GLA_DV = 128
GLA_WIDTH = 512
GLA_QK = GLA_HEADS * GLA_DK
GLA_PROJ = 1552
GLA_PROJ_PAD = 1664
GLA_LORA_PAD = 128
GLA_GATE_NORMALIZER = 16.0
N_EXPERTS = 32
TOP_K = 4
SWIGLU_LIMIT = 7.0
SWIGLU_ALPHA = 1.702
NORM_EPS = 1e-5

RW_CHUNK = 64
GLA_CHUNK = 16
SEQ_BLOCK = 512
TOK_BLOCK = 512
MOE_BLOCK = 512
VMEM_LIMIT = 56 * 1024 * 1024


def _dot(a, b, precision=None):
    return jnp.dot(a, b, preferred_element_type=F32, precision=precision)


def _dot_nt(a, b, precision=None):
    return lax.dot_general(a, b, (((1,), (1,)), ((), ())), preferred_element_type=F32, precision=precision)


def _dot_tn(a, b, precision=None):
    return lax.dot_general(a, b, (((0,), (0,)), ((), ())), preferred_element_type=F32, precision=precision)


def _sigmoid(x):
    return 1.0 / (1.0 + jnp.exp(-x))


def _softplus(x):
    return jnp.maximum(x, 0.0) + jnp.log(1.0 + jnp.exp(-jnp.abs(x)))


def _params(sem):
    return pltpu.CompilerParams(dimension_semantics=sem, vmem_limit_bytes=VMEM_LIMIT)


ROW_TILE = D_MODEL // 128


def _store_row_tiles(ref, x):
    m = x.shape[0]
    for c in range(ROW_TILE):
        ref[pl.ds(c, m, stride=ROW_TILE), :] = x[:, c * 128 : (c + 1) * 128]


def _load_row_tiles(ref, m):
    return jnp.concatenate([ref[pl.ds(c, m, stride=ROW_TILE), :] for c in range(ROW_TILE)], axis=-1)


def _inproj_kernel(x_ref, gain_ref, wr_ref, wg_ref, zr_ref, zg_ref):
    x = x_ref[...]
    xn = x * lax.rsqrt(jnp.mean(x * x, axis=-1, keepdims=True) + NORM_EPS) * gain_ref[...]
    xb = xn.astype(BF16)
    zr_ref[...] = _dot(xb, wr_ref[...])
    zg_ref[...] = _dot(xb, wg_ref[...])


def _inproj(x, gain, w_r, w_g, tm):
    n = x.shape[0]
    return pl.pallas_call(
        _inproj_kernel,
        grid=(n // tm,),
        in_specs=[
            pl.BlockSpec((tm, D_MODEL), lambda i: (i, 0)),
            pl.BlockSpec((1, D_MODEL), lambda i: (0, 0)),
            pl.BlockSpec((D_MODEL, RW_PROJ), lambda i: (0, 0)),
            pl.BlockSpec((D_MODEL, GLA_PROJ_PAD), lambda i: (0, 0)),
        ],
        out_specs=[
            pl.BlockSpec((tm, RW_PROJ), lambda i: (i, 0)),
            pl.BlockSpec((tm, GLA_PROJ_PAD), lambda i: (i, 0)),
        ],
        out_shape=[
            jax.ShapeDtypeStruct((n, RW_PROJ), F32),
            jax.ShapeDtypeStruct((n, GLA_PROJ_PAD), F32),
        ],
        compiler_params=_params(("parallel",)),
    )(x, gain, w_r, w_g)


def _rwkv_features(zs, w0, w2p, a0, a2p, g2, k_k, k_a):
    W = RW_WIDTH
    r = zs[:, 0:W]
    k_raw = zs[:, W : 2 * W]
    v = zs[:, 2 * W : 3 * W]
    zwa = zs[:, 3 * W : 3 * W + 128]
    zg = zs[:, 3 * W + 128 :]
    w = -_softplus(-(w0 + _dot(jnp.tanh(zwa).astype(BF16), w2p))) - 0.5
    log_decay = -jnp.exp(w)
    a = _sigmoid(a0 + _dot(zwa.astype(BF16), a2p))
    g = _dot(_sigmoid(zg).astype(BF16), g2)
    kk_raw = k_raw * k_k
    k = k_raw * (1.0 + (a - 1.0) * k_a)
    return r, k, v, kk_raw, a, log_decay, g


def _level_mask(ri, ci, lvl):
    same = (ri >> (lvl + 1)) == (ci >> (lvl + 1))
    return same & (((ri >> lvl) & 1) == 1) & (((ci >> lvl) & 1) == 0)


def _rwkv_seq_kernel(z_ref, shift0_ref, s0_ref, mu_ref, w0_ref, w2_ref, a0_ref, a2_ref, g2_ref, kk_ref, ka_ref,
                     rk_ref, lnw_ref, lnb_ref, o_ref, sout_ref,
                     m_scr, prev_scr, r_scr, k_scr, v_scr, kkr_scr, a_scr, lw_scr, on_scr, bon_scr):
    C = RW_CHUNK
    N = RW_N
    t_idx = pl.program_id(1)
    tb = z_ref.shape[1]
    zero_nn = jnp.zeros((N, N), F32)

    @pl.when(t_idx == 0)
    def _():
        prev_scr[...] = shift0_ref[0]
        for p in range(RW_HEADS // 2):
            top = jnp.concatenate([s0_ref[0, 2 * p].T, zero_nn], axis=1)
            bot = jnp.concatenate([zero_nn, s0_ref[0, 2 * p + 1].T], axis=1)
            m_scr[p] = jnp.concatenate([top, bot], axis=0)

    z = z_ref[0]
    row = lax.broadcasted_iota(jnp.int32, z.shape, 0)
    z_prev = jnp.where(row == 0, prev_scr[...], pltpu.roll(z, 1, axis=0))
    prev_scr[...] = z[tb - 1 : tb, :]
    zs = z + mu_ref[...] * (z_prev - z)
    r, k, v, kk_raw, a, log_decay, g = _rwkv_features(
        zs, w0_ref[...], w2_ref[...], a0_ref[...], a2_ref[...], g2_ref[...], kk_ref[...], ka_ref[...])
    r_scr[...] = r
    k_scr[...] = k
    v_scr[...] = v
    kkr_scr[...] = kk_raw
    a_scr[...] = a
    lw_scr[...] = log_decay

    P2 = 2 * N
    ri = lax.broadcasted_iota(jnp.int32, (C, P2), 0)
    ci = lax.broadcasted_iota(jnp.int32, (C, P2), 1) % N
    left = lax.broadcasted_iota(jnp.int32, (C, P2), 1) < N
    tril = ri >= ci
    stril = ri > ci
    eye_f = (ri == ci).astype(F32)
    rb = lax.broadcasted_iota(jnp.int32, (P2, P2), 0)
    cb = lax.broadcasted_iota(jnp.int32, (P2, P2), 1)
    same_head = (rb < N) == (cb < N)
    eye_b = rb == cb
    rc = lax.broadcasted_iota(jnp.int32, (C, C), 0)
    cc = lax.broadcasted_iota(jnp.int32, (C, C), 1)
    tril_f = (rc >= cc).astype(F32)
    rk_all = rk_ref[...]

    def bdiag(x):
        return jnp.concatenate([jnp.where(left, x, 0.0), jnp.where(left, 0.0, x)], axis=0)

    def head_sum(x):
        s0 = jnp.sum(jnp.where(left, x, 0.0), axis=-1, keepdims=True)
        s1 = jnp.sum(jnp.where(left, 0.0, x), axis=-1, keepdims=True)
        return jnp.where(left, s0, s1)

    n_sub = tb // C
    pairs = range(RW_HEADS // 2)

    def chunk_body(it, carry):
        units = [(s, p) for s in range(n_sub) for p in pairs]
        sls = [pl.ds(pl.multiple_of((it * n_sub + s) * C, C), C) for s in range(n_sub)]
        prep = []
        for s in range(n_sub):
            lw = lw_scr[sls[s], :]
            cum = _dot(tril_f, lw, precision=HIGHEST)
            cum_last = cum[C - 1 : C, :]
            prep.append(dict(
                e_incl=jnp.exp(cum), e_excl=jnp.exp(cum - lw), e_neg=jnp.exp(-cum),
                e_tail=jnp.exp(cum_last - cum), p_last=jnp.exp(cum_last),
                r=r_scr[sls[s], :], k=k_scr[sls[s], :], v=v_scr[sls[s], :], kk=kkr_scr[sls[s], :],
                a=a_scr[sls[s], :]))
        lanes = [slice(p * P2, (p + 1) * P2) for p in pairs]
        get = lambda name: [prep[s][name][:, lanes[p]] for s, p in units]
        r2, k2, v2, kk2, a2 = get("r"), get("k"), get("v"), get("kk"), get("a")
        e_incl, e_excl, e_neg, e_tail, p_last = get("e_incl"), get("e_excl"), get("e_neg"), get("e_tail"), get("p_last")
        un = range(len(units))
        al = [kk2[u] / jnp.maximum(jnp.sqrt(head_sum(kk2[u] * kk2[u])), 1e-12) for u in un]
        be = [al[u] * a2[u] for u in un]
        al_t = [al[u] * e_excl[u] for u in un]
        r_t = [r2[u] * e_incl[u] for u in un]
        be_n = [be[u] * e_neg[u] for u in un]
        k_n = [k2[u] * e_neg[u] for u in un]
        k_et = [(k2[u] * e_tail[u]).T for u in un]
        be_et = [(be[u] * e_tail[u]).T for u in un]
        v_bd = [bdiag(v2[u]) for u in un]
        lhs = [jnp.concatenate([al_t[u], r_t[u]], axis=0) for u in un]
        s_b = [_dot_nt(lhs[u], bdiag(be_n[u])) for u in un]
        s_k = [_dot_nt(lhs[u], bdiag(k_n[u])) for u in un]
        l_ab = [jnp.where(stril, s_b[u][:C], 0.0) for u in un]
        a_rb = [jnp.where(tril, s_b[u][C:], 0.0) for u in un]
        l_ak = [jnp.where(stril, s_k[u][:C], 0.0) for u in un]
        a_rk = [jnp.where(tril, s_k[u][C:], 0.0) for u in un]
        lakv = [_dot(l_ak[u], v_bd[u]) for u in un]
        arkv = [_dot(a_rk[u], v_bd[u]) for u in un]
        kev = [_dot(k_et[u], v2[u]) for u in un]
        t_inv = [eye_f - jnp.where(_level_mask(ri, ci, 0), l_ab[u], 0.0) for u in un]
        lvl = 1
        while (1 << lvl) < C:
            lm = _level_mask(ri, ci, lvl)
            tn = [_dot(t_inv[u], bdiag(jnp.where(lm, l_ab[u], 0.0))) for u in un]
            t_inv = [t_inv[u] - _dot(tn[u], bdiag(t_inv[u])) for u in un]
            lvl += 1
        a_til = [_dot(t_inv[u], bdiag(al_t[u])) for u in un]
        b_til = [_dot(t_inv[u], bdiag(lakv[u])) for u in un]
        r_hat = [r_t[u] - _dot(a_rb[u], bdiag(a_til[u])) for u in un]
        o_hat = [arkv[u] - _dot(a_rb[u], bdiag(b_til[u])) for u in un]
        g_bd = [jnp.where(same_head, jnp.where(eye_b, p_last[u], 0.0) - _dot(be_et[u], a_til[u]), 0.0) for u in un]
        h_bd = [jnp.where(same_head, kev[u] - _dot(be_et[u], b_til[u]), 0.0) for u in un]
        lhs_m = [jnp.concatenate([r_hat[u], g_bd[u]], axis=0) for u in un]
        for u, (s, p) in enumerate(units):
            res = _dot(lhs_m[u], m_scr[p])
            m_scr[p] = res[C:] + h_bd[u]
            o_p = res[:C] + o_hat[u]
            cen = o_p - head_sum(o_p) * (1.0 / N)
            var = head_sum(cen * cen) * (1.0 / N)
            on_scr[sls[s], lanes[p]] = cen * lax.rsqrt(var + RW_GN_EPS)
            bon_scr[sls[s], lanes[p]] = head_sum(r2[u] * k2[u] * rk_all[:, lanes[p]]) * v2[u]
        return carry

    lax.fori_loop(0, tb // (C * n_sub), chunk_body, 0)
    out = (on_scr[...] * lnw_ref[...] + lnb_ref[...] + bon_scr[...]) * g
    o_ref[0] = out.astype(o_ref.dtype)

    @pl.when(t_idx == pl.num_programs(1) - 1)
    def _():
        for p in range(RW_HEADS // 2):
            m = m_scr[p]
            sout_ref[0, 2 * p] = m[:N, :N].T
            sout_ref[0, 2 * p + 1] = m[N:, N:].T


def _rwkv_seq(z3, shift0, s0, rw, tb):
    b, l, _ = z3.shape
    const = lambda shape: pl.BlockSpec(shape, lambda i, j: (0,) * len(shape))
    wide = lambda: pltpu.VMEM((tb, RW_WIDTH), F32)
    return pl.pallas_call(
        _rwkv_seq_kernel,
        grid=(b, l // tb),
        in_specs=[
            pl.BlockSpec((1, tb, RW_PROJ), lambda i, j: (i, j, 0)),
            pl.BlockSpec((1, 1, RW_PROJ), lambda i, j: (i, 0, 0)),
            pl.BlockSpec((1, RW_HEADS, RW_N, RW_N), lambda i, j: (i, 0, 0, 0)),
            const((1, RW_PROJ)),
            const((1, RW_WIDTH)), const((128, RW_WIDTH)),
            const((1, RW_WIDTH)), const((128, RW_WIDTH)),
            const((128, RW_WIDTH)),
            const((1, RW_WIDTH)), const((1, RW_WIDTH)), const((1, RW_WIDTH)),
            const((1, RW_WIDTH)), const((1, RW_WIDTH)),
        ],
        out_specs=[
            pl.BlockSpec((1, tb, RW_WIDTH), lambda i, j: (i, j, 0)),
            pl.BlockSpec((1, RW_HEADS, RW_N, RW_N), lambda i, j: (i, 0, 0, 0)),
        ],
        out_shape=[
            jax.ShapeDtypeStruct((b, l, RW_WIDTH), BF16),
            jax.ShapeDtypeStruct((b, RW_HEADS, RW_N, RW_N), F32),
        ],
        scratch_shapes=[
            pltpu.VMEM((RW_HEADS // 2, 2 * RW_N, 2 * RW_N), F32),
            pltpu.VMEM((1, RW_PROJ), F32),
            wide(), wide(), wide(), wide(), wide(), wide(), wide(), wide(),
        ],
        compiler_params=_params(("parallel", "arbitrary")),
    )(z3, shift0, s0, rw["mu"], rw["w0"], rw["w2p"], rw["a0"], rw["a2p"], rw["g2"], rw["k_k"], rw["k_a"],
      rw["r_k"], rw["ln_w"], rw["ln_b"])


def _rwkv_step_prep_kernel(z_ref, shift0_ref, mu_ref, w0_ref, w2_ref, a0_ref, a2_ref, g2_ref, kk_ref, ka_ref,
                           rk_ref, r_ref, k_ref, v_ref, al_ref, be_ref, dec_ref, g_ref, bon_ref):
    z = z_ref[...]
    zs = z + mu_ref[...] * (shift0_ref[...] - z)
    r, k, v, kk_raw, a, log_decay, g = _rwkv_features(
        zs, w0_ref[...], w2_ref[...], a0_ref[...], a2_ref[...], g2_ref[...], kk_ref[...], ka_ref[...])
    rk_all = rk_ref[...]
    for h in range(RW_HEADS):
        hs = slice(h * RW_N, (h + 1) * RW_N)
        kk_h = kk_raw[:, hs]
        nrm = jnp.sqrt(jnp.sum(kk_h * kk_h, axis=-1, keepdims=True))
        al = kk_h / jnp.maximum(nrm, 1e-12)
        al_ref[:, hs] = al
        be_ref[:, hs] = al * a[:, hs]
        bon_ref[:, hs] = jnp.sum(r[:, hs] * k[:, hs] * rk_all[:, hs], axis=-1, keepdims=True) * v[:, hs]
    r_ref[...] = r
    k_ref[...] = k
    v_ref[...] = v
    dec_ref[...] = jnp.exp(log_decay)
    g_ref[...] = g


def _rwkv_step_prep(z, shift0, rw):
    n = z.shape[0]
    out = jax.ShapeDtypeStruct((n, RW_WIDTH), F32)
    return pl.pallas_call(
        _rwkv_step_prep_kernel,
        out_shape=[out] * 8,
        compiler_params=pltpu.CompilerParams(vmem_limit_bytes=VMEM_LIMIT),
    )(z, shift0, rw["mu"], rw["w0"], rw["w2p"], rw["a0"], rw["a2p"], rw["g2"], rw["k_k"], rw["k_a"], rw["r_k"])


def _rwkv_step_kernel(s_ref, r_ref, k_ref, al_ref, be_ref, dec_ref, v_ref, snew_ref, o_ref):
    r, k, al, be, dec = r_ref[...], k_ref[...], al_ref[...], be_ref[...], dec_ref[...]

    def body(g, carry):
        rows = pl.ds(pl.multiple_of(g * 8, 8), 8)
        v8 = v_ref[rows, :]
        outs = []
        for j in range(8):
            s = s_ref[0, g * 8 + j]
            sa = -jnp.sum(s * al, axis=0, keepdims=True)
            s_new = s * dec + sa * be + v8[j : j + 1, :] * k
            snew_ref[0, g * 8 + j] = s_new
            outs.append(jnp.sum(s_new * r, axis=0, keepdims=True))
        o_ref[rows, :] = jnp.concatenate(outs, axis=0)
        return carry

    lax.fori_loop(0, RW_N // 8, body, 0)


def _rwkv_step(s0, r, k, al, be, dec, v):
    n = s0.shape[0]
    s_t = jnp.transpose(s0, (1, 2, 3, 0))
    s_spec = pl.BlockSpec((1, RW_N, RW_N, n), lambda h: (h, 0, 0, 0))
    op_spec = pl.BlockSpec((RW_N, n), lambda h: (h, 0))
    s_new_t, o_t = pl.pallas_call(
        _rwkv_step_kernel,
        grid=(RW_HEADS,),
        in_specs=[s_spec] + [op_spec] * 6,
        out_specs=[s_spec, op_spec],
        out_shape=[
            jax.ShapeDtypeStruct(s_t.shape, F32),
            jax.ShapeDtypeStruct((RW_WIDTH, n), F32),
        ],
        compiler_params=_params(("parallel",)),
    )(s_t, r.T, k.T, al.T, be.T, dec.T, v.T)
    return jnp.transpose(s_new_t, (3, 0, 1, 2)), o_t.T


def _gla_features(z, gkw, gkb):
    q = z[:, 0:GLA_QK] * (GLA_DK ** -0.5)
    k = z[:, GLA_QK : 2 * GLA_QK]
    v = z[:, 2 * GLA_QK : 2 * GLA_QK + GLA_WIDTH]
    g = z[:, 2 * GLA_QK + GLA_WIDTH : 2 * GLA_QK + 2 * GLA_WIDTH]
    zgk = z[:, 2 * GLA_QK + 2 * GLA_WIDTH :]
    gk = -_softplus(-(_dot(zgk.astype(BF16), gkw) + gkb)) / GLA_GATE_NORMALIZER
    return q, k, v, g, gk


def _gla_finish(o, g, norm_w):
    outs = []
    for h in range(GLA_HEADS):
        hs = slice(h * GLA_DV, (h + 1) * GLA_DV)
        o_h = o[:, hs]
        o_h = o_h * lax.rsqrt(jnp.mean(o_h * o_h, axis=-1, keepdims=True) + NORM_EPS) * norm_w
        g_h = g[:, hs]
        outs.append(o_h * (g_h * _sigmoid(g_h)))
    return jnp.concatenate(outs, axis=-1)


def _gla_seq_kernel(z_ref, s0_ref, gkw_ref, gkb_ref, nw_ref, wsel_ref, o_ref, sout_ref,
                    st_scr, x_scr, q_scr, k_scr, v_scr, gc_scr, oi_scr):
    C = GLA_CHUNK
    G = 128
    t_idx = pl.program_id(1)
    tb = z_ref.shape[1]
    nc = tb // C

    @pl.when(t_idx == 0)
    def _():
        for h in range(GLA_HEADS):
            st_scr[h] = s0_ref[0, h].T

    q, k, v, g, gk = _gla_features(z_ref[0], gkw_ref[...], gkb_ref[...])
    ri = lax.broadcasted_iota(jnp.int32, (G, G), 0)
    ci = lax.broadcasted_iota(jnp.int32, (G, G), 1)
    cum_mat = ((ri // C == ci // C) & (ri >= ci)).astype(F32)
    for m in range(tb // G):
        rows = slice(m * G, (m + 1) * G)
        gc_scr[rows, :] = _dot(cum_mat, gk[rows, :], precision=HIGHEST)
    q_scr[...] = q
    k_scr[...] = k
    v_scr[...] = v
    gcum = gc_scr[...]

    rg = lax.broadcasted_iota(jnp.int32, (tb, 2 * G), 0)
    cg = lax.broadcasted_iota(jnp.int32, (tb, 2 * G), 1)
    blk_mask = ((cg % G) // C == (rg % G) // C) & (cg % C <= rg % C)
    for p in range(GLA_HEADS // 2):
        ls = slice(p * 128, (p + 1) * 128)
        q3 = q[:, ls].reshape(nc, C, 128)
        k3 = k[:, ls].reshape(nc, C, 128)
        g3 = gcum[:, ls].reshape(nc, C, 128)
        for j in range(C):
            e = q3 * jnp.exp(jnp.minimum(g3 - g3[:, j : j + 1, :], 0.0)) * k3[:, j : j + 1, :]
            x_scr[:, j * 128 : (j + 1) * 128] = e.reshape(tb, 128).astype(BF16)
        a_t = jnp.where(blk_mask, _dot(x_scr[...], wsel_ref[...]), 0.0).astype(BF16)
        for hl in range(2):
            h = 2 * p + hl
            for m in range(tb // G):
                rows = slice(m * G, (m + 1) * G)
                a_blk = a_t[rows, hl * G : (hl + 1) * G]
                oi_scr[rows, h * GLA_DV : (h + 1) * GLA_DV] = _dot(
                    a_blk, v[rows, h * GLA_DV : (h + 1) * GLA_DV].astype(BF16))

    def chunk_body(c, carry):
        sl = pl.ds(pl.multiple_of(c * C, C), C)
        g_c = gc_scr[sl, :]
        q_c = q_scr[sl, :]
        k_c = k_scr[sl, :]
        v_c = v_scr[sl, :]
        for h in range(GLA_HEADS):
            ks = slice(h * GLA_DK, (h + 1) * GLA_DK)
            vs = slice(h * GLA_DV, (h + 1) * GLA_DV)
            g_h = g_c[:, ks]
            g_last = g_h[C - 1 : C, :]
            st = st_scr[h]
            oi_scr[sl, vs] += _dot_nt(q_c[:, ks] * jnp.exp(g_h), st)
            k_dec = k_c[:, ks] * jnp.exp(g_last - g_h)
            st_scr[h] = st * jnp.exp(g_last) + _dot_tn(v_c[:, vs], k_dec)
        return carry

    lax.fori_loop(0, nc, chunk_body, 0, unroll=8)
    o_ref[0] = _gla_finish(oi_scr[...], g, nw_ref[...]).astype(o_ref.dtype)

    @pl.when(t_idx == pl.num_programs(1) - 1)
    def _():
        for h in range(GLA_HEADS):
            sout_ref[0, h] = st_scr[h].T


def _gla_select_matrix():
    j = jnp.arange(GLA_CHUNK)[:, None, None]
    hl = jnp.arange(2)[None, :, None]
    rows_j = jnp.broadcast_to(j, (GLA_CHUNK, 2, GLA_DK)).reshape(-1)
    rows_h = jnp.broadcast_to(hl, (GLA_CHUNK, 2, GLA_DK)).reshape(-1)
    cols = jnp.arange(256)
    sel = (rows_j[:, None] == cols[None, :] % GLA_CHUNK) & (rows_h[:, None] == cols[None, :] // 128)
    return sel.astype(BF16)


def _gla_seq(z3, s0, gl, tb):
    b, l, _ = z3.shape
    const = lambda shape: pl.BlockSpec(shape, lambda i, j: (0,) * len(shape))
    return pl.pallas_call(
        _gla_seq_kernel,
        grid=(b, l // tb),
        in_specs=[
            pl.BlockSpec((1, tb, GLA_PROJ_PAD), lambda i, j: (i, j, 0)),
            pl.BlockSpec((1, GLA_HEADS, GLA_DK, GLA_DV), lambda i, j: (i, 0, 0, 0)),
            const((GLA_LORA_PAD, GLA_QK)), const((1, GLA_QK)), const((1, GLA_DV)),
            const((GLA_CHUNK * 128, 256)),
        ],
        out_specs=[
            pl.BlockSpec((1, tb, GLA_WIDTH), lambda i, j: (i, j, 0)),
            pl.BlockSpec((1, GLA_HEADS, GLA_DK, GLA_DV), lambda i, j: (i, 0, 0, 0)),
        ],
        out_shape=[
            jax.ShapeDtypeStruct((b, l, GLA_WIDTH), BF16),
            jax.ShapeDtypeStruct((b, GLA_HEADS, GLA_DK, GLA_DV), F32),
        ],
        scratch_shapes=[
            pltpu.VMEM((GLA_HEADS, GLA_DV, GLA_DK), F32),
            pltpu.VMEM((tb, GLA_CHUNK * 128), BF16),
            pltpu.VMEM((tb, GLA_QK), F32), pltpu.VMEM((tb, GLA_QK), F32), pltpu.VMEM((tb, GLA_WIDTH), F32),
            pltpu.VMEM((tb, GLA_QK), F32), pltpu.VMEM((tb, GLA_WIDTH), F32),
        ],
        compiler_params=_params(("parallel", "arbitrary")),
    )(z3, s0, gl["gkw"], gl["gkb"], gl["norm_w"], _gla_select_matrix())


def _gla_step_prep_kernel(z_ref, gkw_ref, gkb_ref, q_ref, k_ref, v_ref, g_ref, dec_ref):
    q, k, v, g, gk = _gla_features(z_ref[...], gkw_ref[...], gkb_ref[...])
    q_ref[...] = q
    k_ref[...] = k
    v_ref[...] = v
    g_ref[...] = g
    dec_ref[...] = jnp.exp(gk)


def _gla_step_prep(z, gl):
    n = z.shape[0]
    qk = jax.ShapeDtypeStruct((n, GLA_QK), F32)
    wide = jax.ShapeDtypeStruct((n, GLA_WIDTH), F32)
    return pl.pallas_call(
        _gla_step_prep_kernel,
        out_shape=[qk, qk, wide, wide, qk],
        compiler_params=pltpu.CompilerParams(vmem_limit_bytes=VMEM_LIMIT),
    )(z, gl["gkw"], gl["gkb"])


def _gla_step_kernel(s_ref, qcol_ref, kcol_ref, dcol_ref, vrow_ref, snew_ref, orow_ref):
    s_new = s_ref[...] * dcol_ref[...] + kcol_ref[...] * vrow_ref[...]
    snew_ref[...] = s_new
    orow_ref[...] = jnp.sum(s_new * qcol_ref[...], axis=2, keepdims=True)


def _gla_step(s0, q, k, dec, v, bb):
    n = s0.shape[0]
    colv = lambda t: t.reshape(n, GLA_HEADS, GLA_DK, 1)
    col_spec = pl.BlockSpec((bb, GLA_HEADS, GLA_DK, 1), lambda i: (i, 0, 0, 0))
    row_spec = pl.BlockSpec((bb, GLA_HEADS, 1, GLA_DV), lambda i: (i, 0, 0, 0))
    s_spec = pl.BlockSpec((bb, GLA_HEADS, GLA_DK, GLA_DV), lambda i: (i, 0, 0, 0))
    s_new, o_row = pl.pallas_call(
        _gla_step_kernel,
        grid=(n // bb,),
        in_specs=[s_spec, col_spec, col_spec, col_spec, row_spec],
        out_specs=[s_spec, row_spec],
        out_shape=[
            jax.ShapeDtypeStruct(s0.shape, F32),
            jax.ShapeDtypeStruct((n, GLA_HEADS, 1, GLA_DV), F32),
        ],
        compiler_params=_params(("parallel",)),
    )(s0, colv(q), colv(k), colv(dec), v.reshape(n, GLA_HEADS, 1, GLA_DV))
    return s_new, o_row.reshape(n, GLA_WIDTH)


def _step_post_kernel(orw_ref, bon_ref, grw_ref, lnw_ref, lnb_ref, ogl_ref, ggl_ref, nw_ref, o_rw_ref, o_gl_ref):
    o = orw_ref[...]
    for h in range(RW_HEADS):
        hs = slice(h * RW_N, (h + 1) * RW_N)
        o_h = o[:, hs]
        mean = jnp.mean(o_h, axis=-1, keepdims=True)
        cen = o_h - mean
        var = jnp.mean(cen * cen, axis=-1, keepdims=True)
        on = cen * lax.rsqrt(var + RW_GN_EPS)
        res = (on * lnw_ref[:, hs] + lnb_ref[:, hs] + bon_ref[:, hs]) * grw_ref[:, hs]
        o_rw_ref[:, hs] = res.astype(o_rw_ref.dtype)
    o_gl_ref[...] = _gla_finish(ogl_ref[...], ggl_ref[...], nw_ref[...]).astype(o_gl_ref.dtype)


def _step_post(o_rw, bonus, g_rw, rw, o_gl, g_gl, gl):
    n = o_rw.shape[0]
    return pl.pallas_call(
        _step_post_kernel,
        out_shape=[jax.ShapeDtypeStruct((n, RW_WIDTH), BF16), jax.ShapeDtypeStruct((n, GLA_WIDTH), BF16)],
        compiler_params=pltpu.CompilerParams(vmem_limit_bytes=VMEM_LIMIT),
    )(o_rw, bonus, g_rw, rw["ln_w"], rw["ln_b"], o_gl, g_gl, gl["norm_w"])


def _outproj_router_kernel(x_ref, orw_ref, ogl_ref, wo_ref, gain_ref, wrt_ref, br_ref,
                           h_ref, xn_ref, idx_ref, gate_ref, rank_ref, cnt_ref):
    tm = x_ref.shape[0]
    mix = jnp.concatenate([orw_ref[...], ogl_ref[...]], axis=-1)
    h = x_ref[...] + _dot(mix, wo_ref[...])
    h_ref[...] = h
    xn = h * lax.rsqrt(jnp.mean(h * h, axis=-1, keepdims=True) + NORM_EPS) * gain_ref[...]
    _store_row_tiles(xn_ref, xn)
    logits = _dot_nt(wrt_ref[...], xn, precision=HIGHEST) + br_ref[...]
    eidx = lax.broadcasted_iota(jnp.int32, logits.shape, 0)
    ti = lax.broadcasted_iota(jnp.int32, (tm, tm), 0)
    tj = lax.broadcasted_iota(jnp.int32, (tm, tm), 1)
    before = (ti < tj).astype(BF16)
    vals, idxs = [], []
    work = logits
    chosen = jnp.zeros(logits.shape, F32)
    for _ in range(TOP_K):
        m = jnp.max(work, axis=0, keepdims=True)
        sel = jnp.min(jnp.where(work == m, eidx, N_EXPERTS), axis=0, keepdims=True)
        hit = eidx == sel
        work = jnp.where(hit, -jnp.inf, work)
        chosen = chosen + hit.astype(F32)
        vals.append(m)
        idxs.append(sel)
    prefix = _dot(chosen.astype(BF16), before)
    exps = [jnp.exp(v - vals[0]) for v in vals]
    denom = exps[0] + exps[1] + exps[2] + exps[3]
    for j in range(TOP_K):
        idx_ref[0, j : j + 1, :] = idxs[j]
        gate_ref[0, j : j + 1, :] = exps[j] / denom
        rank = jnp.sum(jnp.where(eidx == idxs[j], prefix, 0.0), axis=0, keepdims=True)
        rank_ref[0, j : j + 1, :] = rank.astype(jnp.int32)
    cnt = jnp.sum(chosen, axis=1, keepdims=True)
    cnt_ref[0] = jnp.broadcast_to(cnt, (N_EXPERTS, 128)).astype(jnp.int32)


def _outproj_router(x, o_rw, o_gl, w_out, gain, w_router_t, b_router, tm):
    n = x.shape[0]
    nt = n // tm
    const = lambda shape: pl.BlockSpec(shape, lambda i: (0,) * len(shape))
    tok = lambda width: pl.BlockSpec((tm, width), lambda i: (i, 0))
    lane = pl.BlockSpec((1, TOP_K, tm), lambda i: (i, 0, 0))
    return pl.pallas_call(
        _outproj_router_kernel,
        grid=(nt,),
        in_specs=[
            tok(D_MODEL), tok(RW_WIDTH), tok(GLA_WIDTH),
            const((D_MODEL, D_MODEL)), const((1, D_MODEL)), const((N_EXPERTS, D_MODEL)), const((N_EXPERTS, 1)),
        ],
        out_specs=[tok(D_MODEL), pl.BlockSpec((tm * ROW_TILE, 128), lambda i: (i, 0)), lane, lane, lane,
                   pl.BlockSpec((1, N_EXPERTS, 128), lambda i: (i, 0, 0))],
        out_shape=[
            jax.ShapeDtypeStruct((n, D_MODEL), F32),
            jax.ShapeDtypeStruct((n * ROW_TILE, 128), F32),
            jax.ShapeDtypeStruct((nt, TOP_K, tm), jnp.int32),
            jax.ShapeDtypeStruct((nt, TOP_K, tm), F32),
            jax.ShapeDtypeStruct((nt, TOP_K, tm), jnp.int32),
            jax.ShapeDtypeStruct((nt, N_EXPERTS, 128), jnp.int32),
        ],
        compiler_params=_params(("parallel",)),
    )(x, o_rw, o_gl, w_out, gain, w_router_t, b_router)


def _moe_kernel(be_ref, nu_ref, epoch_ref, next_ref, rows_ref, xs_ref, wg_hbm, wu_hbm, wd_hbm, bg_ref, bu_ref, bd_ref,
                y_ref, w_f32, wg_b, wu_b, wd_b, sems):
    b = pl.program_id(0)
    prev = be_ref[jnp.maximum(b - 1, 0)]
    new_expert = (b == 0) | (be_ref[b] != prev)

    def fetch(e, slot):
        return [pltpu.make_async_copy(w.at[e], w_f32.at[slot, i], sems.at[slot])
                for i, w in enumerate((wg_hbm, wu_hbm, wd_hbm))]

    @pl.when(b == 0)
    def _():
        for c in fetch(be_ref[0], 0):
            c.start()

    @pl.when(new_expert)
    def _():
        slot = epoch_ref[b] % 2
        for c in fetch(be_ref[b], slot):
            c.wait()

        @pl.when(next_ref[b] >= 0)
        def _():
            for c in fetch(next_ref[b], 1 - slot):
                c.start()

        wg_b[...] = w_f32[slot, 0].astype(BF16)
        wu_b[...] = w_f32[slot, 1].astype(BF16)
        wd_b[...] = w_f32[slot, 2].astype(BF16)

    def ffn(m):
        x = _load_row_tiles(xs_ref, m).astype(BF16)
        half = D_MODEL // 2
        acc = None
        for f in range(2):
            fs = slice(f * half, (f + 1) * half)
            gt = _dot(x, wg_b[:, fs]) + bg_ref[0, :, fs]
            up = _dot(x, wu_b[:, fs]) + bu_ref[0, :, fs]
            gt = jnp.minimum(gt, SWIGLU_LIMIT)
            up = jnp.clip(up, -SWIGLU_LIMIT, SWIGLU_LIMIT)
            hid = (up + 1.0) * gt * _sigmoid(SWIGLU_ALPHA * gt)
            part = _dot(hid.astype(BF16), wd_b[fs, :])
            acc = part if acc is None else acc + part
        _store_row_tiles(y_ref, acc + bd_ref[0])

    quarter = MOE_BLOCK // 4
    quarters = (rows_ref[b] + quarter - 1) // quarter
    for q in range(1, 5):
        @pl.when((b < nu_ref[0]) & (quarters == q))
        def _():
            ffn(q * quarter)


def _moe_ffn(block_expert, n_used, block_rows, xs, w_gate, w_up, w_down, b_gate, b_up, b_down):
    n_blocks = block_expert.shape[0]
    pos = jnp.arange(n_blocks, dtype=jnp.int32)
    change = (pos > 0) & (block_expert != jnp.roll(block_expert, 1))
    epoch = jnp.cumsum(change.astype(jnp.int32))
    later = change[None, :] & (pos[None, :] > pos[:, None])
    first = jnp.min(jnp.where(later, pos[None, :], n_blocks), axis=1)
    next_e = jnp.sum(jnp.where(pos[None, :] == first[:, None], block_expert[None, :], 0), axis=1)
    next_e = jnp.where(first < n_blocks, next_e, -1).astype(jnp.int32)

    row = lambda b, be, nu, ep, nx, br: (jnp.minimum(b, nu[0] - 1), 0)
    bspec = pl.BlockSpec((1, 1, D_MODEL), lambda b, be, nu, ep, nx, br: (be[b], 0, 0))
    wspec = pl.BlockSpec(memory_space=pl.ANY)
    grid_spec = pltpu.PrefetchScalarGridSpec(
        num_scalar_prefetch=5,
        grid=(n_blocks,),
        in_specs=[pl.BlockSpec((MOE_BLOCK * ROW_TILE, 128), row), wspec, wspec, wspec, bspec, bspec, bspec],
        out_specs=pl.BlockSpec((MOE_BLOCK * ROW_TILE, 128), row),
        scratch_shapes=[pltpu.VMEM((2, 3, D_MODEL, D_MODEL), F32)] + [pltpu.VMEM((D_MODEL, D_MODEL), BF16)] * 3
        + [pltpu.SemaphoreType.DMA((2,))],
    )
    return pl.pallas_call(
        _moe_kernel,
        grid_spec=grid_spec,
        out_shape=jax.ShapeDtypeStruct((n_blocks * MOE_BLOCK * ROW_TILE, 128), F32),
        compiler_params=_params(("arbitrary",)),
    )(block_expert, n_used, epoch, next_e, block_rows, xs, w_gate, w_up, w_down,
      b_gate.reshape(N_EXPERTS, 1, D_MODEL), b_up.reshape(N_EXPERTS, 1, D_MODEL),
      b_down.reshape(N_EXPERTS, 1, D_MODEL))


SEG_ALIGN = 8
GROUP_ROWS = SEG_ALIGN * ROW_TILE


def _local_rows(tm):
    return tm * TOP_K + N_EXPERTS * SEG_ALIGN


def _group_rows(ref, group):
    start = group * GROUP_ROWS
    if not isinstance(group, int):
        start = pl.multiple_of(start, GROUP_ROWS)
    return ref.at[pl.ds(start, GROUP_ROWS)]


def _group_copy(hbm, hbm_group, buf, buf_group, sem, to_hbm):
    h, b = _group_rows(hbm, hbm_group), _group_rows(buf, buf_group)
    return pltpu.make_async_copy(b, h, sem) if to_hbm else pltpu.make_async_copy(h, b, sem)


def _combine_kernel(grp_c, grp_n, lpos_ref, gate_ref, h_ref, gain_ref, y_hbm, o_ref, ybuf, fbuf, sems):
    i = pl.program_id(0)
    nt = pl.num_programs(0)
    tm = h_ref.shape[0]
    n_groups = _local_rows(tm) // SEG_ALIGN

    def issue(grp_ref, slot):
        def body(g, carry):
            _group_copy(y_hbm, grp_ref[0, 0, g], ybuf.at[slot], g, sems.at[slot], False).start()
            return carry
        lax.fori_loop(0, grp_ref[0, 0, n_groups], body, 0)

    @pl.when(i == 0)
    def _():
        issue(grp_c, 0)

    @pl.when(i + 1 < nt)
    def _():
        issue(grp_n, (i + 1) % 2)

    slot = i % 2

    yb = ybuf.at[slot]
    rows_in = pl.ds(0, pl.multiple_of(grp_c[0, 0, n_groups] * GROUP_ROWS, GROUP_ROWS))
    pltpu.make_async_copy(y_hbm.at[rows_in], yb.at[rows_in], sems.at[slot]).wait()

    def token_body(t, carry):
        acc = None
        for j in range(TOP_K):
            row = pl.multiple_of(lpos_ref[0, 0, t * TOP_K + j], ROW_TILE)
            term = gate_ref[0, 0, t * TOP_K + j] * yb[pl.ds(row, ROW_TILE), :]
            acc = term if acc is None else acc + term
        fbuf[pl.ds(pl.multiple_of(t * ROW_TILE, ROW_TILE), ROW_TILE), :] = acc
        return carry

    lax.fori_loop(0, tm, token_body, 0, unroll=8)
    f = h_ref[...] + _load_row_tiles(fbuf, tm)
    o_ref[...] = f * lax.rsqrt(jnp.mean(f * f, axis=-1, keepdims=True) + NORM_EPS) * gain_ref[...]


def _combine(h, y_rows, grp3, lpos3, gate3, gain, tm):
    n = h.shape[0]
    nt = n // tm
    n_local = _local_rows(tm)
    gw = grp3.shape[-1]
    smem = lambda shape, imap: pl.BlockSpec(shape, imap, memory_space=pltpu.SMEM)
    cur = lambda i: (i, 0, 0)
    nxt = lambda i: (jnp.minimum(i + 1, nt - 1), 0, 0)
    return pl.pallas_call(
        _combine_kernel,
        grid=(nt,),
        in_specs=[
            smem((1, 1, gw), cur), smem((1, 1, gw), nxt),
            smem((1, 1, TOP_K * tm), cur), smem((1, 1, TOP_K * tm), cur),
            pl.BlockSpec((tm, D_MODEL), lambda i: (i, 0)),
            pl.BlockSpec((1, D_MODEL), lambda i: (0, 0)),
            pl.BlockSpec(memory_space=pl.ANY),
        ],
        out_specs=pl.BlockSpec((tm, D_MODEL), lambda i: (i, 0)),
        out_shape=jax.ShapeDtypeStruct((n, D_MODEL), F32),
        scratch_shapes=[pltpu.VMEM((2, n_local * ROW_TILE, 128), F32), pltpu.VMEM((tm * ROW_TILE, 128), F32),
                        pltpu.SemaphoreType.DMA((2,))],
        compiler_params=pltpu.CompilerParams(dimension_semantics=("arbitrary",), vmem_limit_bytes=VMEM_LIMIT,
                                             disable_bounds_checks=True),
    )(grp3, grp3, lpos3, gate3, h, gain, y_rows)


def _dispatch_kernel(*refs, fill):
    if fill:
        grp_ref, grp_prev, lpos_ref, ends_ref, x_ref, xs_hbm, sorted_buf, zero_scr, sems, zsem = refs
    else:
        grp_ref, grp_prev, lpos_ref, x_ref, _, xs_hbm, sorted_buf, sems = refs
    i = pl.program_id(0)
    tm = x_ref.shape[0] // ROW_TILE
    n_groups = _local_rows(tm) // SEG_ALIGN
    blk = MOE_BLOCK * ROW_TILE
    sorted_scr = sorted_buf.at[i % 2]
    sem = sems.at[i % 2]

    if fill:
        def fill_copy(e):
            start = pl.multiple_of((ends_ref[0, e] - MOE_BLOCK) * ROW_TILE, blk)
            return pltpu.make_async_copy(zero_scr, xs_hbm.at[pl.ds(start, blk)], zsem)

        @pl.when(i == 0)
        def _():
            zero_scr[...] = jnp.zeros(zero_scr.shape, zero_scr.dtype)
            for e in range(N_EXPERTS):
                @pl.when(ends_ref[1, e] > 0)
                def _():
                    fill_copy(e).start()

    sorted_scr[...] = jnp.zeros(sorted_scr.shape, sorted_scr.dtype)

    def move(t, carry):
        row = x_ref[pl.ds(pl.multiple_of(t * ROW_TILE, ROW_TILE), ROW_TILE), :]
        for j in range(TOP_K):
            dst = pl.multiple_of(lpos_ref[0, 0, t * TOP_K + j], ROW_TILE)
            sorted_scr[pl.ds(dst, ROW_TILE), :] = row
        return carry

    lax.fori_loop(0, tm, move, 0, unroll=8)

    def wait_groups(n, buf, s):
        rows_out = pl.ds(0, pl.multiple_of(n * GROUP_ROWS, GROUP_ROWS))
        pltpu.make_async_copy(buf.at[rows_out], xs_hbm.at[rows_out], s).wait()

    if fill:
        @pl.when(i == 0)
        def _():
            for e in range(N_EXPERTS):
                @pl.when(ends_ref[1, e] > 0)
                def _():
                    fill_copy(e).wait()

    @pl.when(i > 0)
    def _():
        wait_groups(grp_prev[0, 0, n_groups], sorted_buf.at[(i + 1) % 2], sems.at[(i + 1) % 2])

    n_used = grp_ref[0, 0, n_groups]

    def send(g, carry):
        _group_copy(xs_hbm, grp_ref[0, 0, g], sorted_scr, g, sem, True).start()
        return carry

    lax.fori_loop(0, n_used, send, 0)

    @pl.when(i == pl.num_programs(0) - 1)
    def _():
        wait_groups(n_used, sorted_scr, sem)


def _dispatch(xn, grp3, lpos3, tm, n_slots, ends=None, xs=None):
    n = xn.shape[0] // ROW_TILE
    fill = xs is None
    smem = lambda shape, imap: pl.BlockSpec(shape, imap, memory_space=pltpu.SMEM)
    in_specs = [smem((1, 1, grp3.shape[-1]), lambda i: (i, 0, 0)),
                smem((1, 1, grp3.shape[-1]), lambda i: (jnp.maximum(i - 1, 0), 0, 0)),
                smem((1, 1, TOP_K * tm), lambda i: (i, 0, 0))]
    args = [grp3, grp3, lpos3]
    scratch = [pltpu.VMEM((2, _local_rows(tm) * ROW_TILE, 128), F32)]
    if fill:
        in_specs.append(smem((2, N_EXPERTS), lambda i: (0, 0)))
        args.append(ends)
        scratch.append(pltpu.VMEM((MOE_BLOCK * ROW_TILE, 128), F32))
    in_specs.append(pl.BlockSpec((tm * ROW_TILE, 128), lambda i: (i, 0)))
    args.append(xn)
    aliases = {}
    if not fill:
        in_specs.append(pl.BlockSpec(memory_space=pl.ANY))
        args.append(xs)
        aliases = {len(args) - 1: 0}
    scratch.append(pltpu.SemaphoreType.DMA((2,)))
    if fill:
        scratch.append(pltpu.SemaphoreType.DMA(()))
    return pl.pallas_call(
        functools.partial(_dispatch_kernel, fill=fill),
        grid=(n // tm,),
        in_specs=in_specs,
        out_specs=pl.BlockSpec(memory_space=pl.ANY),
        out_shape=jax.ShapeDtypeStruct((n_slots * ROW_TILE, 128), F32),
        scratch_shapes=scratch,
        input_output_aliases=aliases,
        compiler_params=pltpu.CompilerParams(dimension_semantics=("arbitrary",), vmem_limit_bytes=VMEM_LIMIT,
                                             disable_bounds_checks=True, has_side_effects=True),
    )(*args)


def _pad_rows(w, rows, offset):
    out = jnp.zeros((rows, w.shape[1]), w.dtype)
    return out.at[offset : offset + w.shape[0]].set(w)


def _routing_tables(counts, n_pairs):
    n_tiles = counts.shape[0]
    n_blocks = (n_pairs + n_tiles * N_EXPERTS * (SEG_ALIGN - 1) + N_EXPERTS * (MOE_BLOCK - 1)
                + MOE_BLOCK - 1) // MOE_BLOCK
    runs = (counts + SEG_ALIGN - 1) // SEG_ALIGN * SEG_ALIGN
    local_start = jnp.cumsum(runs, axis=1) - runs
    total = jnp.sum(runs, axis=0)
    padded = (total + MOE_BLOCK - 1) // MOE_BLOCK * MOE_BLOCK
    pends = jnp.cumsum(padded)
    pstarts = pends - padded
    global_start = pstarts[None, :] + jnp.cumsum(runs, axis=0) - runs
    blocks = jnp.arange(n_blocks, dtype=jnp.int32) * MOE_BLOCK
    n_used = (pends[-1] // MOE_BLOCK).astype(jnp.int32)
    owner = jnp.sum((pends[None, :] <= blocks[:, None]).astype(jnp.int32), axis=1)
    block_expert = jnp.minimum(owner, N_EXPERTS - 1)
    last = jnp.sum(jnp.where(jnp.arange(n_blocks) == n_used - 1, block_expert, 0))
    block_expert = jnp.where(jnp.arange(n_blocks) < n_used, block_expert, last)
    ends = jnp.stack([pends, padded]).astype(jnp.int32)
    row_end = jnp.sum(jnp.where(block_expert[:, None] == jnp.arange(N_EXPERTS), (pstarts + total)[None, :], 0), axis=1)
    block_rows = jnp.clip(row_end - blocks, 0, MOE_BLOCK).astype(jnp.int32)
    return (runs, local_start, global_start, ends, block_expert.astype(jnp.int32), n_used.reshape(1), block_rows,
            n_blocks)


def _tile_tables(idx3, rank3, runs, local_start, global_start, tm):
    experts = jnp.arange(N_EXPERTS, dtype=jnp.int32)
    hit = idx3[..., None] == experts
    lpos = rank3 + jnp.sum(jnp.where(hit, local_start[:, None, None, :], 0), axis=-1)
    n_groups = _local_rows(tm) // SEG_ALIGN
    g = jnp.arange(n_groups, dtype=jnp.int32)
    run_end = (local_start + runs) // SEG_ALIGN
    owner = jnp.minimum(jnp.sum((run_end[:, None, :] <= g[None, :, None]).astype(jnp.int32), axis=-1), N_EXPERTS - 1)
    sel = owner[..., None] == experts
    offset = jnp.sum(jnp.where(sel, (global_start - local_start)[:, None, :], 0), axis=-1) // SEG_ALIGN
    used = jnp.sum(runs, axis=1) // SEG_ALIGN
    table = jnp.where(g[None, :] < used[:, None], g[None, :] + offset, 0)
    tail = jnp.broadcast_to(used[:, None], (used.shape[0], 8))
    lpos_rows = _token_major(lpos.astype(jnp.int32) * ROW_TILE)
    return lpos_rows, jnp.concatenate([table, tail], axis=1).astype(jnp.int32)[:, None, :]


def _token_major(t3):
    return jnp.swapaxes(t3, 1, 2).reshape(t3.shape[0], 1, -1)


def kernel(x_prompt, x_sample, state_rwkv_shift, state_rwkv_wkv, state_gla, norm_mix, w_in, rw_mu, rw_w0, rw_w2, rw_a0, rw_a2, rw_g2, rw_k_k, rw_k_a, rw_r_k, rw_ln_w, rw_ln_b, gla_gk_w2, gla_gk_b, gla_norm_w, w_out, norm_ffn, w_router, b_router, w_gate, b_gate, w_up, b_up, w_down, b_down, norm_final):
    depth = norm_mix.shape[0]
    assert depth == 1
    bp, lp, d = x_prompt.shape
    bs, ls, _ = x_sample.shape
    assert ls == 1 and lp % SEQ_BLOCK == 0
    l = 0
    row = lambda t: t.reshape(1, -1)

    w_in_b = w_in[l].astype(BF16)
    w_in_r = w_in_b[:, :RW_PROJ]
    w_in_g = jnp.pad(w_in_b[:, RW_PROJ:], ((0, 0), (0, GLA_PROJ_PAD - GLA_PROJ)))
    rw = dict(
        mu=row(rw_mu[l]), w0=row(rw_w0[l]), a0=row(rw_a0[l]),
        w2p=_pad_rows(rw_w2[l].astype(BF16), 128, 0), a2p=_pad_rows(rw_a2[l].astype(BF16), 128, 64),
        g2=rw_g2[l].astype(BF16), k_k=row(rw_k_k[l]), k_a=row(rw_k_a[l]), r_k=row(rw_r_k[l]),
        ln_w=row(rw_ln_w[l]), ln_b=row(rw_ln_b[l]))
    gl = dict(gkw=_pad_rows(gla_gk_w2[l].astype(BF16), GLA_LORA_PAD, 0), gkb=row(gla_gk_b[l]),
              norm_w=row(gla_norm_w[l]))
    gain_mix = row(norm_mix[l])

    n_p = bp * lp
    xp = x_prompt.reshape(n_p, d)
    zr_p, zg_p = _inproj(xp, gain_mix, w_in_r, w_in_g, TOK_BLOCK)
    zr_p3 = zr_p.reshape(bp, lp, RW_PROJ)
    o_rw_p, wkv_p = _rwkv_seq(zr_p3, jnp.zeros((bp, 1, RW_PROJ), F32),
                              jnp.zeros((bp, RW_HEADS, RW_N, RW_N), F32), rw, SEQ_BLOCK)
    o_gl_p, gla_p = _gla_seq(zg_p.reshape(bp, lp, GLA_PROJ_PAD),
                             jnp.zeros((bp, GLA_HEADS, GLA_DK, GLA_DV), F32), gl, SEQ_BLOCK)
    shift_p = zr_p3[:, -1, :]

    xs_ = x_sample.reshape(bs, d)
    zr_s, zg_s = _inproj(xs_, gain_mix, w_in_r, w_in_g, bs)
    r, k, v, al, be, dec, g_rw, bonus = _rwkv_step_prep(zr_s, state_rwkv_shift[l], rw)
    wkv_s, o_rw_s = _rwkv_step(state_rwkv_wkv[l], r, k, al, be, dec, v)
    q, kg, vg, g_gl, dec_g = _gla_step_prep(zg_s, gl)
    gla_s, o_gl_s = _gla_step(state_gla[l], q, kg, dec_g, vg, 16)
    o_rw_s2, o_gl_s2 = _step_post(o_rw_s, bonus, g_rw, rw, o_gl_s, g_gl, gl)
    shift_s = zr_s

    w_out_b = w_out[l].astype(BF16)
    router = (w_out_b, row(norm_ffn[l]), w_router[l].T, b_router[l].reshape(N_EXPERTS, 1))
    h_p, xn_p, idx_p, gate_p, rank_p, cnt_p = _outproj_router(
        xp, o_rw_p.reshape(n_p, RW_WIDTH), o_gl_p.reshape(n_p, GLA_WIDTH), *router, TOK_BLOCK)
    h_s, xn_s, idx_s, gate_s, rank_s, cnt_s = _outproj_router(xs_, o_rw_s2, o_gl_s2, *router, bs)
    nt_p = n_p // TOK_BLOCK
    counts = jnp.concatenate([cnt_p[:, :, 0], cnt_s[:, :, 0]], axis=0)
    runs, lstart, gstart, ends, block_expert, n_used, block_rows, n_blocks = _routing_tables(
        counts, (n_p + bs) * TOP_K)
    n_slots = n_blocks * MOE_BLOCK
    lpos_p, grp_p = _tile_tables(idx_p, rank_p, runs[:nt_p], lstart[:nt_p], gstart[:nt_p], TOK_BLOCK)
    lpos_s, grp_s = _tile_tables(idx_s, rank_s, runs[nt_p:], lstart[nt_p:], gstart[nt_p:], bs)
    xs_rows = _dispatch(xn_p, grp_p, lpos_p, TOK_BLOCK, n_slots, ends=ends)
    xs_rows = _dispatch(xn_s, grp_s, lpos_s, bs, n_slots, xs=xs_rows)
    y_rows = _moe_ffn(block_expert, n_used, block_rows, xs_rows, w_gate[l], w_up[l], w_down[l], b_gate[l], b_up[l], b_down[l])
    gain_f = row(norm_final)
    y_p = _combine(h_p, y_rows, grp_p, lpos_p, _token_major(gate_p), gain_f, TOK_BLOCK)
    y_s = _combine(h_s, y_rows, grp_s, lpos_s, _token_major(gate_s), gain_f, bs)

    y_prompt = y_p.reshape(bp, lp, d)
    y_sample = y_s.reshape(bs, ls, d)
    return (y_prompt, y_sample, shift_p[None], wkv_p[None], gla_p[None], shift_s[None], wkv_s[None], gla_s[None])
```

```python
import functools

import jax
import jax.numpy as jnp
from jax import lax
from jax.experimental import pallas as pl
from jax.experimental.pallas import tpu as pltpu

F32 = jnp.float32
BF16 = jnp.bfloat16
HIGHEST = lax.Precision.HIGHEST

D_MODEL = 1024
RW_WIDTH = 512
RW_HEADS = 8
RW_N = 64
RW_PROJ = 1792
RW_GN_EPS = 64e-5
GLA_HEADS = 4
GLA_DK = 64
GLA_DV = 128
GLA_WIDTH = 512
GLA_QK = GLA_HEADS * GLA_DK
GLA_PROJ = 1552
GLA_PROJ_PAD = 1664
GLA_LORA_PAD = 128
GLA_GATE_NORMALIZER = 16.0
N_EXPERTS = 32
TOP_K = 4
SWIGLU_LIMIT = 7.0
SWIGLU_ALPHA = 1.702
NORM_EPS = 1e-5
LOG2_E = 1.4426950408889634

RW_CHUNK = 64
GLA_CHUNK = 16
SEQ_BLOCK = 512
TOK_BLOCK = 512
MOE_BLOCK = 512
VMEM_LIMIT = 56 * 1024 * 1024


def _dot(a, b, precision=None):
    return jnp.dot(a, b, preferred_element_type=F32, precision=precision)


def _dot_nt(a, b, precision=None):
    return lax.dot_general(a, b, (((1,), (1,)), ((), ())), preferred_element_type=F32, precision=precision)


def _dot_tn(a, b, precision=None):
    return lax.dot_general(a, b, (((0,), (0,)), ((), ())), preferred_element_type=F32, precision=precision)


def _sigmoid(x):
    return 1.0 / (1.0 + jnp.exp(-x))


def _softplus(x):
    return jnp.maximum(x, 0.0) + jnp.log(1.0 + jnp.exp(-jnp.abs(x)))


def _params(sem):
    return pltpu.CompilerParams(dimension_semantics=sem, vmem_limit_bytes=VMEM_LIMIT)


ROW_TILE = D_MODEL // 128


def _store_row_tiles(ref, x):
    m = x.shape[0]
    for c in range(ROW_TILE):
        ref[pl.ds(c, m, stride=ROW_TILE), :] = x[:, c * 128 : (c + 1) * 128]


def _load_row_tiles(ref, m):
    return jnp.concatenate([ref[pl.ds(c, m, stride=ROW_TILE), :] for c in range(ROW_TILE)], axis=-1)


def _inproj_kernel(x_ref, gain_ref, wr_ref, wg_ref, zr_ref, zg_ref):
    x = x_ref[...]
    xn = x * lax.rsqrt(jnp.mean(x * x, axis=-1, keepdims=True) + NORM_EPS) * gain_ref[...]
    xb = xn.astype(BF16)
    zr_ref[...] = _dot(xb, wr_ref[...])
    zg_ref[...] = _dot(xb, wg_ref[...])


def _inproj(x, gain, w_r, w_g, tm):
    n = x.shape[0]
    return pl.pallas_call(
        _inproj_kernel,
        grid=(n // tm,),
        in_specs=[
            pl.BlockSpec((tm, D_MODEL), lambda i: (i, 0)),
            pl.BlockSpec((1, D_MODEL), lambda i: (0, 0)),
            pl.BlockSpec((D_MODEL, RW_PROJ), lambda i: (0, 0)),
            pl.BlockSpec((D_MODEL, GLA_PROJ_PAD), lambda i: (0, 0)),
        ],
        out_specs=[
            pl.BlockSpec((tm, RW_PROJ), lambda i: (i, 0)),
            pl.BlockSpec((tm, GLA_PROJ_PAD), lambda i: (i, 0)),
        ],
        out_shape=[
            jax.ShapeDtypeStruct((n, RW_PROJ), F32),
            jax.ShapeDtypeStruct((n, GLA_PROJ_PAD), F32),
        ],
        compiler_params=_params(("parallel",)),
    )(x, gain, w_r, w_g)


def _rwkv_features(zs, w0, w2p, a0, a2p, g2, k_k, k_a):
    W = RW_WIDTH
    r = zs[:, 0:W]
    k_raw = zs[:, W : 2 * W]
    v = zs[:, 2 * W : 3 * W]
    zwa = zs[:, 3 * W : 3 * W + 128]
    zg = zs[:, 3 * W + 128 :]
    w = -_softplus(-(w0 + _dot(jnp.tanh(zwa).astype(BF16), w2p))) - 0.5
    log_decay = -jnp.exp(w)
    a = _sigmoid(a0 + _dot(zwa.astype(BF16), a2p))
    g = _dot(_sigmoid(zg).astype(BF16), g2)
    kk_raw = k_raw * k_k
    k = k_raw * (1.0 + (a - 1.0) * k_a)
    return r, k, v, kk_raw, a, log_decay, g


def _level_mask(ri, ci, lvl):
    same = (ri >> (lvl + 1)) == (ci >> (lvl + 1))
    return same & (((ri >> lvl) & 1) == 1) & (((ci >> lvl) & 1) == 0)


def _rwkv_seq_kernel(z_ref, shift0_ref, s0_ref, mu_ref, w0_ref, w2_ref, a0_ref, a2_ref, g2_ref, kk_ref, ka_ref,
                     rk_ref, lnw_ref, lnb_ref, o_ref, sout_ref,
                     m_scr, prev_scr, r_scr, k_scr, v_scr, kkr_scr, a_scr, lw_scr, on_scr, bon_scr):
    C = RW_CHUNK
    N = RW_N
    t_idx = pl.program_id(1)
    tb = z_ref.shape[1]
    zero_nn = jnp.zeros((N, N), F32)

    @pl.when(t_idx == 0)
    def _():
        prev_scr[...] = shift0_ref[0]
        for p in range(RW_HEADS // 2):
            top = jnp.concatenate([s0_ref[0, 2 * p].T, zero_nn], axis=1)
            bot = jnp.concatenate([zero_nn, s0_ref[0, 2 * p + 1].T], axis=1)
            m_scr[p] = jnp.concatenate([top, bot], axis=0)

    z = z_ref[0]
    row = lax.broadcasted_iota(jnp.int32, z.shape, 0)
    z_prev = jnp.where(row == 0, prev_scr[...], pltpu.roll(z, 1, axis=0))
    prev_scr[...] = z[tb - 1 : tb, :]
    zs = z + mu_ref[...] * (z_prev - z)
    r, k, v, kk_raw, a, log_decay, g = _rwkv_features(
        zs, w0_ref[...], w2_ref[...], a0_ref[...], a2_ref[...], g2_ref[...], kk_ref[...], ka_ref[...])
    P2 = 2 * N
    left1 = lax.broadcasted_iota(jnp.int32, (1, P2), 1) < N

    def head_sum(x):
        s0 = jnp.sum(jnp.where(left1, x, 0.0), axis=-1, keepdims=True)
        s1 = jnp.sum(jnp.where(left1, 0.0, x), axis=-1, keepdims=True)
        return jnp.where(left1, s0, s1)

    def head_sum_wide(x):
        return jnp.concatenate([head_sum(x[:, p * P2 : (p + 1) * P2]) for p in range(RW_HEADS // 2)], axis=1)

    alpha = kk_raw * lax.rsqrt(jnp.maximum(head_sum_wide(kk_raw * kk_raw), 1e-24))
    r_scr[...] = r
    k_scr[...] = k
    v_scr[...] = v
    kkr_scr[...] = alpha
    a_scr[...] = alpha * a
    lw_scr[...] = log_decay
    bon_scr[...] = head_sum_wide(r * k * rk_ref[...]) * v

    ri = lax.broadcasted_iota(jnp.int32, (C, P2), 0)
    ci = lax.broadcasted_iota(jnp.int32, (C, P2), 1) % N
    left = lax.broadcasted_iota(jnp.int32, (C, P2), 1) < N
    tril = ri >= ci
    stril = ri > ci
    eye_f = (ri == ci).astype(F32)
    rb = lax.broadcasted_iota(jnp.int32, (P2, P2), 0)
    cb = lax.broadcasted_iota(jnp.int32, (P2, P2), 1)
    same_head = (rb < N) == (cb < N)
    eye_b = rb == cb
    rc = lax.broadcasted_iota(jnp.int32, (C, C), 0)
    cc = lax.broadcasted_iota(jnp.int32, (C, C), 1)
    tril_f = (rc >= cc).astype(F32)

    def bdiag(x):
        return jnp.concatenate([jnp.where(left, x, 0.0), jnp.where(left, 0.0, x)], axis=0)

    n_sub = tb // C
    pairs = range(RW_HEADS // 2)

    def chunk_body(it, carry):
        units = [(s, p) for s in range(n_sub) for p in pairs]
        sls = [pl.ds(pl.multiple_of((it * n_sub + s) * C, C), C) for s in range(n_sub)]
        prep = []
        for s in range(n_sub):
            lw = lw_scr[sls[s], :]
            cum = _dot(tril_f, lw, precision=HIGHEST)
            cum_last = cum[C - 1 : C, :]
            prep.append(dict(
                e_incl=jnp.exp(cum), e_excl=jnp.exp(cum - lw), e_neg=jnp.exp(-cum),
                e_tail=jnp.exp(cum_last - cum), p_last=jnp.exp(cum_last),
                r=r_scr[sls[s], :], k=k_scr[sls[s], :], v=v_scr[sls[s], :], kk=kkr_scr[sls[s], :],
                a=a_scr[sls[s], :]))
        lanes = [slice(p * P2, (p + 1) * P2) for p in pairs]
        get = lambda name: [prep[s][name][:, lanes[p]] for s, p in units]
        r2, k2, v2, al, be = get("r"), get("k"), get("v"), get("kk"), get("a")
        e_incl, e_excl, e_neg, e_tail, p_last = get("e_incl"), get("e_excl"), get("e_neg"), get("e_tail"), get("p_last")
        un = range(len(units))
        al_t = [al[u] * e_excl[u] for u in un]
        r_t = [r2[u] * e_incl[u] for u in un]
        be_n = [be[u] * e_neg[u] for u in un]
        k_n = [k2[u] * e_neg[u] for u in un]
        k_et = [(k2[u] * e_tail[u]).T for u in un]
        be_et = [(be[u] * e_tail[u]).T for u in un]
        v_bd = [bdiag(v2[u]) for u in un]
        lhs = [jnp.concatenate([al_t[u], r_t[u]], axis=0) for u in un]
        s_b = [_dot_nt(lhs[u], bdiag(be_n[u])) for u in un]
        s_k = [_dot_nt(lhs[u], bdiag(k_n[u])) for u in un]
        l_ab = [jnp.where(stril, s_b[u][:C], 0.0) for u in un]
        a_rb = [jnp.where(tril, s_b[u][C:], 0.0) for u in un]
        l_ak = [jnp.where(stril, s_k[u][:C], 0.0) for u in un]
        a_rk = [jnp.where(tril, s_k[u][C:], 0.0) for u in un]
        lakv = [_dot(l_ak[u], v_bd[u]) for u in un]
        arkv = [_dot(a_rk[u], v_bd[u]) for u in un]
        kev = [_dot(k_et[u], v2[u]) for u in un]
        t_inv = [eye_f - jnp.where(_level_mask(ri, ci, 0), l_ab[u], 0.0) for u in un]
        lvl = 1
        while (1 << lvl) < C:
            lm = _level_mask(ri, ci, lvl)
            tn = [_dot(t_inv[u], bdiag(jnp.where(lm, l_ab[u], 0.0))) for u in un]
            t_inv = [t_inv[u] - _dot(tn[u], bdiag(t_inv[u])) for u in un]
            lvl += 1
        a_til = [_dot(t_inv[u], bdiag(al_t[u])) for u in un]
        b_til = [_dot(t_inv[u], bdiag(lakv[u])) for u in un]
        r_hat = [r_t[u] - _dot(a_rb[u], bdiag(a_til[u])) for u in un]
        o_hat = [arkv[u] - _dot(a_rb[u], bdiag(b_til[u])) for u in un]
        g_bd = [jnp.where(same_head, jnp.where(eye_b, p_last[u], 0.0) - _dot(be_et[u], a_til[u]), 0.0) for u in un]
        h_bd = [jnp.where(same_head, kev[u] - _dot(be_et[u], b_til[u]), 0.0) for u in un]
        lhs_m = [jnp.concatenate([r_hat[u], g_bd[u]], axis=0) for u in un]
        for u, (s, p) in enumerate(units):
            res = _dot(lhs_m[u], m_scr[p])
            m_scr[p] = res[C:] + h_bd[u]
            o_p = res[:C] + o_hat[u]
            cen = o_p - head_sum(o_p) * (1.0 / N)
            var = head_sum(cen * cen) * (1.0 / N)
            on_scr[sls[s], lanes[p]] = cen * lax.rsqrt(var + RW_GN_EPS)
        return carry

    lax.fori_loop(0, tb // (C * n_sub), chunk_body, 0)
    out = (on_scr[...] * lnw_ref[...] + lnb_ref[...] + bon_scr[...]) * g
    o_ref[0] = out.astype(o_ref.dtype)

    @pl.when(t_idx == pl.num_programs(1) - 1)
    def _():
        for p in range(RW_HEADS // 2):
            m = m_scr[p]
            sout_ref[0, 2 * p] = m[:N, :N].T
            sout_ref[0, 2 * p + 1] = m[N:, N:].T


def _rwkv_seq(z3, shift0, s0, rw, tb):
    b, l, _ = z3.shape
    const = lambda shape: pl.BlockSpec(shape, lambda i, j: (0,) * len(shape))
    wide = lambda: pltpu.VMEM((tb, RW_WIDTH), F32)
    return pl.pallas_call(
        _rwkv_seq_kernel,
        grid=(b, l // tb),
        in_specs=[
            pl.BlockSpec((1, tb, RW_PROJ), lambda i, j: (i, j, 0)),
            pl.BlockSpec((1, 1, RW_PROJ), lambda i, j: (i, 0, 0)),
            pl.BlockSpec((1, RW_HEADS, RW_N, RW_N), lambda i, j: (i, 0, 0, 0)),
            const((1, RW_PROJ)),
            const((1, RW_WIDTH)), const((128, RW_WIDTH)),
            const((1, RW_WIDTH)), const((128, RW_WIDTH)),
            const((128, RW_WIDTH)),
            const((1, RW_WIDTH)), const((1, RW_WIDTH)), const((1, RW_WIDTH)),
            const((1, RW_WIDTH)), const((1, RW_WIDTH)),
        ],
        out_specs=[
            pl.BlockSpec((1, tb, RW_WIDTH), lambda i, j: (i, j, 0)),
            pl.BlockSpec((1, RW_HEADS, RW_N, RW_N), lambda i, j: (i, 0, 0, 0)),
        ],
        out_shape=[
            jax.ShapeDtypeStruct((b, l, RW_WIDTH), BF16),
            jax.ShapeDtypeStruct((b, RW_HEADS, RW_N, RW_N), F32),
        ],
        scratch_shapes=[
            pltpu.VMEM((RW_HEADS // 2, 2 * RW_N, 2 * RW_N), F32),
            pltpu.VMEM((1, RW_PROJ), F32),
            wide(), wide(), wide(), wide(), wide(), wide(), wide(), wide(),
        ],
        compiler_params=_params(("parallel", "arbitrary")),
    )(z3, shift0, s0, rw["mu"], rw["w0"], rw["w2p"], rw["a0"], rw["a2p"], rw["g2"], rw["k_k"], rw["k_a"],
      rw["r_k"], rw["ln_w"], rw["ln_b"])


def _rwkv_step_prep_kernel(z_ref, shift0_ref, mu_ref, w0_ref, w2_ref, a0_ref, a2_ref, g2_ref, kk_ref, ka_ref,
                           rk_ref, r_ref, k_ref, v_ref, al_ref, be_ref, dec_ref, g_ref, bon_ref):
    z = z_ref[...]
    zs = z + mu_ref[...] * (shift0_ref[...] - z)
    r, k, v, kk_raw, a, log_decay, g = _rwkv_features(
        zs, w0_ref[...], w2_ref[...], a0_ref[...], a2_ref[...], g2_ref[...], kk_ref[...], ka_ref[...])
    rk_all = rk_ref[...]
    for h in range(RW_HEADS):
        hs = slice(h * RW_N, (h + 1) * RW_N)
        kk_h = kk_raw[:, hs]
        nrm = jnp.sqrt(jnp.sum(kk_h * kk_h, axis=-1, keepdims=True))
        al = kk_h / jnp.maximum(nrm, 1e-12)
        al_ref[:, hs] = al
        be_ref[:, hs] = al * a[:, hs]
        bon_ref[:, hs] = jnp.sum(r[:, hs] * k[:, hs] * rk_all[:, hs], axis=-1, keepdims=True) * v[:, hs]
    r_ref[...] = r
    k_ref[...] = k
    v_ref[...] = v
    dec_ref[...] = jnp.exp(log_decay)
    g_ref[...] = g


def _rwkv_step_prep(z, shift0, rw):
    n = z.shape[0]
    out = jax.ShapeDtypeStruct((n, RW_WIDTH), F32)
    return pl.pallas_call(
        _rwkv_step_prep_kernel,
        out_shape=[out] * 8,
        compiler_params=pltpu.CompilerParams(vmem_limit_bytes=VMEM_LIMIT),
    )(z, shift0, rw["mu"], rw["w0"], rw["w2p"], rw["a0"], rw["a2p"], rw["g2"], rw["k_k"], rw["k_a"], rw["r_k"])


def _rwkv_step_kernel(s_ref, r_ref, k_ref, al_ref, be_ref, dec_ref, v_ref, snew_ref, o_ref):
    r, k, al, be, dec = r_ref[...], k_ref[...], al_ref[...], be_ref[...], dec_ref[...]

    def body(g, carry):
        rows = pl.ds(pl.multiple_of(g * 8, 8), 8)
        v8 = v_ref[rows, :]
        outs = []
        for j in range(8):
            s = s_ref[0, g * 8 + j]
            sa = -jnp.sum(s * al, axis=0, keepdims=True)
            s_new = s * dec + sa * be + v8[j : j + 1, :] * k
            snew_ref[0, g * 8 + j] = s_new
            outs.append(jnp.sum(s_new * r, axis=0, keepdims=True))
        o_ref[rows, :] = jnp.concatenate(outs, axis=0)
        return carry

    lax.fori_loop(0, RW_N // 8, body, 0)


def _rwkv_step(s0, r, k, al, be, dec, v):
    n = s0.shape[0]
    s_t = jnp.transpose(s0, (1, 2, 3, 0))
    s_spec = pl.BlockSpec((1, RW_N, RW_N, n), lambda h: (h, 0, 0, 0))
    op_spec = pl.BlockSpec((RW_N, n), lambda h: (h, 0))
    s_new_t, o_t = pl.pallas_call(
        _rwkv_step_kernel,
        grid=(RW_HEADS,),
        in_specs=[s_spec] + [op_spec] * 6,
        out_specs=[s_spec, op_spec],
        out_shape=[
            jax.ShapeDtypeStruct(s_t.shape, F32),
            jax.ShapeDtypeStruct((RW_WIDTH, n), F32),
        ],
        compiler_params=_params(("parallel",)),
    )(s_t, r.T, k.T, al.T, be.T, dec.T, v.T)
    return jnp.transpose(s_new_t, (3, 0, 1, 2)), o_t.T


def _gla_features(z, gkw, gkb):
    q = z[:, 0:GLA_QK] * (GLA_DK ** -0.5)
    k = z[:, GLA_QK : 2 * GLA_QK]
    v = z[:, 2 * GLA_QK : 2 * GLA_QK + GLA_WIDTH]
    g = z[:, 2 * GLA_QK + GLA_WIDTH : 2 * GLA_QK + 2 * GLA_WIDTH]
    zgk = z[:, 2 * GLA_QK + 2 * GLA_WIDTH :]
    gk = -_softplus(-(_dot(zgk.astype(BF16), gkw) + gkb)) / GLA_GATE_NORMALIZER
    return q, k, v, g, gk


def _gla_finish(o, g, norm_w):
    outs = []
    for h in range(GLA_HEADS):
        hs = slice(h * GLA_DV, (h + 1) * GLA_DV)
        o_h = o[:, hs]
        o_h = o_h * lax.rsqrt(jnp.mean(o_h * o_h, axis=-1, keepdims=True) + NORM_EPS) * norm_w
        g_h = g[:, hs]
        outs.append(o_h * (g_h * _sigmoid(g_h)))
    return jnp.concatenate(outs, axis=-1)


def _gla_seq_kernel(z_ref, s0_ref, gkw_ref, gkb_ref, nw_ref, wsel_ref, o_ref, sout_ref,
                    st_scr, x_scr, gc_scr, oi_scr):
    C = GLA_CHUNK
    G = 128
    t_idx = pl.program_id(1)
    tb = z_ref.shape[1]
    nc = tb // C
    zero_vk = jnp.zeros((GLA_DV, GLA_DK), F32)

    @pl.when(t_idx == 0)
    def _():
        for p in range(GLA_HEADS // 2):
            top = jnp.concatenate([s0_ref[0, 2 * p].T, zero_vk], axis=1)
            bot = jnp.concatenate([zero_vk, s0_ref[0, 2 * p + 1].T], axis=1)
            st_scr[p] = jnp.concatenate([top, bot], axis=0)

    q, k, v, g, gk = _gla_features(z_ref[0], gkw_ref[...], gkb_ref[...])
    ri = lax.broadcasted_iota(jnp.int32, (G, G), 0)
    ci = lax.broadcasted_iota(jnp.int32, (G, G), 1)
    cum_mat = ((ri // C == ci // C) & (ri >= ci)).astype(F32)
    for m in range(tb // G):
        rows = slice(m * G, (m + 1) * G)
        gc_scr[rows, :] = _dot(cum_mat, gk[rows, :], precision=HIGHEST)
    gcum = gc_scr[...]

    rg = lax.broadcasted_iota(jnp.int32, (tb, 2 * G), 0)
    cg = lax.broadcasted_iota(jnp.int32, (tb, 2 * G), 1)
    blk_mask = ((cg % G) // C == (rg % G) // C) & (cg % C <= rg % C)
    for p in range(GLA_HEADS // 2):
        ls = slice(p * 128, (p + 1) * 128)
        q3 = q[:, ls].reshape(nc, C, 128)
        k3 = k[:, ls].reshape(nc, C, 128)
        g3 = gcum[:, ls].reshape(nc, C, 128) * LOG2_E
        half = C // 2
        for j in range(C):
            lo = 0 if j < half else half
            e = (q3[:, lo:] * jnp.exp2(jnp.minimum(g3[:, lo:] - g3[:, j : j + 1, :], 0.0))) * k3[:, j : j + 1, :]
            if lo:
                e = jnp.concatenate([jnp.zeros((nc, lo, 128), F32), e], axis=1)
            x_scr[:, j * 128 : (j + 1) * 128] = e.reshape(tb, 128).astype(BF16)
        a_t = jnp.where(blk_mask, _dot(x_scr[...], wsel_ref[...]), 0.0).astype(BF16)
        for hl in range(2):
            h = 2 * p + hl
            for m in range(tb // G):
                rows = slice(m * G, (m + 1) * G)
                a_blk = a_t[rows, hl * G : (hl + 1) * G]
                oi_scr[rows, h * GLA_DV : (h + 1) * GLA_DV] = _dot(
                    a_blk, v[rows, h * GLA_DV : (h + 1) * GLA_DV].astype(BF16))

    CG = G // C
    rt = lax.broadcasted_iota(jnp.int32, (G, CG * 128), 0)
    ct = lax.broadcasted_iota(jnp.int32, (G, CG * 128), 1)
    own_chunk = rt // C == ct // 128
    rs = lax.broadcasted_iota(jnp.int32, (2 * GLA_DV, CG * 128), 0)
    cs = lax.broadcasted_iota(jnp.int32, (2 * GLA_DV, CG * 128), 1)
    same_head = rs // GLA_DV == (cs % 128) // GLA_DK

    def chunk_diag(x):
        return jnp.where(own_chunk, jnp.concatenate([x] * CG, axis=1), 0.0)

    for m in range(tb // G):
        rows = slice(m * G, (m + 1) * G)
        for p in range(GLA_HEADS // 2):
            ls = slice(p * 128, (p + 1) * 128)
            vs = slice(p * 2 * GLA_DV, (p + 1) * 2 * GLA_DV)
            g_g = gcum[rows, ls]
            g3 = g_g.reshape(CG, C, 128)
            g_last = jnp.broadcast_to(g3[:, C - 1 : C, :], (CG, C, 128)).reshape(G, 128)
            q_t = q[rows, ls] * jnp.exp(g_g)
            k_t = k[rows, ls] * jnp.exp(g_last - g_g)
            d_s = jnp.where(same_head, _dot_tn(v[rows, vs], chunk_diag(k_t)), 0.0)
            st = st_scr[p]
            starts = []
            for c in range(CG):
                starts.append(st)
                decay = jnp.exp(g_g[c * C + C - 1 : c * C + C, :])
                st = st * decay + d_s[:, c * 128 : (c + 1) * 128]
            st_scr[p] = st
            oi_scr[rows, vs] += _dot_nt(chunk_diag(q_t), jnp.concatenate(starts, axis=1))

    o_ref[0] = _gla_finish(oi_scr[...], g, nw_ref[...]).astype(o_ref.dtype)

    @pl.when(t_idx == pl.num_programs(1) - 1)
    def _():
        for p in range(GLA_HEADS // 2):
            st = st_scr[p]
            sout_ref[0, 2 * p] = st[:GLA_DV, :GLA_DK].T
            sout_ref[0, 2 * p + 1] = st[GLA_DV:, GLA_DK:].T


def _gla_select_matrix():
    j = jnp.arange(GLA_CHUNK)[:, None, None]
    hl = jnp.arange(2)[None, :, None]
    rows_j = jnp.broadcast_to(j, (GLA_CHUNK, 2, GLA_DK)).reshape(-1)
    rows_h = jnp.broadcast_to(hl, (GLA_CHUNK, 2, GLA_DK)).reshape(-1)
    cols = jnp.arange(256)
    sel = (rows_j[:, None] == cols[None, :] % GLA_CHUNK) & (rows_h[:, None] == cols[None, :] // 128)
    return sel.astype(BF16)


def _gla_seq(z3, s0, gl, tb):
    b, l, _ = z3.shape
    const = lambda shape: pl.BlockSpec(shape, lambda i, j: (0,) * len(shape))
    return pl.pallas_call(
        _gla_seq_kernel,
        grid=(b, l // tb),
        in_specs=[
            pl.BlockSpec((1, tb, GLA_PROJ_PAD), lambda i, j: (i, j, 0)),
            pl.BlockSpec((1, GLA_HEADS, GLA_DK, GLA_DV), lambda i, j: (i, 0, 0, 0)),
            const((GLA_LORA_PAD, GLA_QK)), const((1, GLA_QK)), const((1, GLA_DV)),
            const((GLA_CHUNK * 128, 256)),
        ],
        out_specs=[
            pl.BlockSpec((1, tb, GLA_WIDTH), lambda i, j: (i, j, 0)),
            pl.BlockSpec((1, GLA_HEADS, GLA_DK, GLA_DV), lambda i, j: (i, 0, 0, 0)),
        ],
        out_shape=[
            jax.ShapeDtypeStruct((b, l, GLA_WIDTH), BF16),
            jax.ShapeDtypeStruct((b, GLA_HEADS, GLA_DK, GLA_DV), F32),
        ],
        scratch_shapes=[
            pltpu.VMEM((GLA_HEADS // 2, 2 * GLA_DV, 2 * GLA_DK), F32),
            pltpu.VMEM((tb, GLA_CHUNK * 128), BF16),
            pltpu.VMEM((tb, GLA_QK), F32), pltpu.VMEM((tb, GLA_WIDTH), F32),
        ],
        compiler_params=_params(("parallel", "arbitrary")),
    )(z3, s0, gl["gkw"], gl["gkb"], gl["norm_w"], _gla_select_matrix())


def _gla_step_prep_kernel(z_ref, gkw_ref, gkb_ref, q_ref, k_ref, v_ref, g_ref, dec_ref):
    q, k, v, g, gk = _gla_features(z_ref[...], gkw_ref[...], gkb_ref[...])
    q_ref[...] = q
    k_ref[...] = k
    v_ref[...] = v
    g_ref[...] = g
    dec_ref[...] = jnp.exp(gk)


def _gla_step_prep(z, gl):
    n = z.shape[0]
    qk = jax.ShapeDtypeStruct((n, GLA_QK), F32)
    wide = jax.ShapeDtypeStruct((n, GLA_WIDTH), F32)
    return pl.pallas_call(
        _gla_step_prep_kernel,
        out_shape=[qk, qk, wide, wide, qk],
        compiler_params=pltpu.CompilerParams(vmem_limit_bytes=VMEM_LIMIT),
    )(z, gl["gkw"], gl["gkb"])


def _gla_step_kernel(s_ref, qcol_ref, kcol_ref, dcol_ref, vrow_ref, snew_ref, orow_ref):
    s_new = s_ref[...] * dcol_ref[...] + kcol_ref[...] * vrow_ref[...]
    snew_ref[...] = s_new
    orow_ref[...] = jnp.sum(s_new * qcol_ref[...], axis=2, keepdims=True)


def _gla_step(s0, q, k, dec, v, bb):
    n = s0.shape[0]
    colv = lambda t: t.reshape(n, GLA_HEADS, GLA_DK, 1)
    col_spec = pl.BlockSpec((bb, GLA_HEADS, GLA_DK, 1), lambda i: (i, 0, 0, 0))
    row_spec = pl.BlockSpec((bb, GLA_HEADS, 1, GLA_DV), lambda i: (i, 0, 0, 0))
    s_spec = pl.BlockSpec((bb, GLA_HEADS, GLA_DK, GLA_DV), lambda i: (i, 0, 0, 0))
    s_new, o_row = pl.pallas_call(
        _gla_step_kernel,
        grid=(n // bb,),
        in_specs=[s_spec, col_spec, col_spec, col_spec, row_spec],
        out_specs=[s_spec, row_spec],
        out_shape=[
            jax.ShapeDtypeStruct(s0.shape, F32),
            jax.ShapeDtypeStruct((n, GLA_HEADS, 1, GLA_DV), F32),
        ],
        compiler_params=_params(("parallel",)),
    )(s0, colv(q), colv(k), colv(dec), v.reshape(n, GLA_HEADS, 1, GLA_DV))
    return s_new, o_row.reshape(n, GLA_WIDTH)


def _step_post_kernel(orw_ref, bon_ref, grw_ref, lnw_ref, lnb_ref, ogl_ref, ggl_ref, nw_ref, o_rw_ref, o_gl_ref):
    o = orw_ref[...]
    for h in range(RW_HEADS):
        hs = slice(h * RW_N, (h + 1) * RW_N)
        o_h = o[:, hs]
        mean = jnp.mean(o_h, axis=-1, keepdims=True)
        cen = o_h - mean
        var = jnp.mean(cen * cen, axis=-1, keepdims=True)
        on = cen * lax.rsqrt(var + RW_GN_EPS)
        res = (on * lnw_ref[:, hs] + lnb_ref[:, hs] + bon_ref[:, hs]) * grw_ref[:, hs]
        o_rw_ref[:, hs] = res.astype(o_rw_ref.dtype)
    o_gl_ref[...] = _gla_finish(ogl_ref[...], ggl_ref[...], nw_ref[...]).astype(o_gl_ref.dtype)


def _step_post(o_rw, bonus, g_rw, rw, o_gl, g_gl, gl):
    n = o_rw.shape[0]
    return pl.pallas_call(
        _step_post_kernel,
        out_shape=[jax.ShapeDtypeStruct((n, RW_WIDTH), BF16), jax.ShapeDtypeStruct((n, GLA_WIDTH), BF16)],
        compiler_params=pltpu.CompilerParams(vmem_limit_bytes=VMEM_LIMIT),
    )(o_rw, bonus, g_rw, rw["ln_w"], rw["ln_b"], o_gl, g_gl, gl["norm_w"])


def _outproj_router_kernel(x_ref, orw_ref, ogl_ref, wo_ref, gain_ref, wrt_ref, br_ref,
                           h_ref, xn_ref, idx_ref, gate_ref, rank_ref, cnt_ref):
    tm = x_ref.shape[0]
    mix = jnp.concatenate([orw_ref[...], ogl_ref[...]], axis=-1)
    h = x_ref[...] + _dot(mix, wo_ref[...])
    h_ref[...] = h
    xn = h * lax.rsqrt(jnp.mean(h * h, axis=-1, keepdims=True) + NORM_EPS) * gain_ref[...]
    _store_row_tiles(xn_ref, xn)
    logits = _dot_nt(wrt_ref[...], xn, precision=HIGHEST) + br_ref[...]
    eidx = lax.broadcasted_iota(jnp.int32, logits.shape, 0)
    ti = lax.broadcasted_iota(jnp.int32, (tm, tm), 0)
    tj = lax.broadcasted_iota(jnp.int32, (tm, tm), 1)
    before = (ti < tj).astype(BF16)
    vals, idxs = [], []
    work = logits
    chosen = jnp.zeros(logits.shape, F32)
    for _ in range(TOP_K):
        m = jnp.max(work, axis=0, keepdims=True)
        sel = jnp.min(jnp.where(work == m, eidx, N_EXPERTS), axis=0, keepdims=True)
        hit = eidx == sel
        work = jnp.where(hit, -jnp.inf, work)
        chosen = chosen + hit.astype(F32)
        vals.append(m)
        idxs.append(sel)
    prefix = _dot(chosen.astype(BF16), before)
    exps = [jnp.exp(v - vals[0]) for v in vals]
    denom = exps[0] + exps[1] + exps[2] + exps[3]
    for j in range(TOP_K):
        idx_ref[0, j : j + 1, :] = idxs[j]
        gate_ref[0, j : j + 1, :] = exps[j] / denom
        rank = jnp.sum(jnp.where(eidx == idxs[j], prefix, 0.0), axis=0, keepdims=True)
        rank_ref[0, j : j + 1, :] = rank.astype(jnp.int32)
    cnt = jnp.sum(chosen, axis=1, keepdims=True)
    cnt_ref[0] = jnp.broadcast_to(cnt, (N_EXPERTS, 128)).astype(jnp.int32)


def _outproj_router(x, o_rw, o_gl, w_out, gain, w_router_t, b_router, tm):
    n = x.shape[0]
    nt = n // tm
    const = lambda shape: pl.BlockSpec(shape, lambda i: (0,) * len(shape))
    tok = lambda width: pl.BlockSpec((tm, width), lambda i: (i, 0))
    lane = pl.BlockSpec((1, TOP_K, tm), lambda i: (i, 0, 0))
    return pl.pallas_call(
        _outproj_router_kernel,
        grid=(nt,),
        in_specs=[
            tok(D_MODEL), tok(RW_WIDTH), tok(GLA_WIDTH),
            const((D_MODEL, D_MODEL)), const((1, D_MODEL)), const((N_EXPERTS, D_MODEL)), const((N_EXPERTS, 1)),
        ],
        out_specs=[tok(D_MODEL), pl.BlockSpec((tm * ROW_TILE, 128), lambda i: (i, 0)), lane, lane, lane,
                   pl.BlockSpec((1, N_EXPERTS, 128), lambda i: (i, 0, 0))],
        out_shape=[
            jax.ShapeDtypeStruct((n, D_MODEL), F32),
            jax.ShapeDtypeStruct((n * ROW_TILE, 128), F32),
            jax.ShapeDtypeStruct((nt, TOP_K, tm), jnp.int32),
            jax.ShapeDtypeStruct((nt, TOP_K, tm), F32),
            jax.ShapeDtypeStruct((nt, TOP_K, tm), jnp.int32),
            jax.ShapeDtypeStruct((nt, N_EXPERTS, 128), jnp.int32),
        ],
        compiler_params=_params(("parallel",)),
    )(x, o_rw, o_gl, w_out, gain, w_router_t, b_router)


def _moe_kernel(be_ref, nu_ref, epoch_ref, next_ref, rows_ref, xs_ref, wg_hbm, wu_hbm, wd_hbm, bg_ref, bu_ref, bd_ref,
                y_ref, w_f32, wg_b, wu_b, wd_b, sems):
    b = pl.program_id(0)
    prev = be_ref[jnp.maximum(b - 1, 0)]
    new_expert = (b == 0) | (be_ref[b] != prev)

    def fetch(e, slot):
        return [pltpu.make_async_copy(w.at[e], w_f32.at[slot, i], sems.at[slot])
                for i, w in enumerate((wg_hbm, wu_hbm, wd_hbm))]

    @pl.when(b == 0)
    def _():
        for c in fetch(be_ref[0], 0):
            c.start()

    @pl.when(new_expert)
    def _():
        slot = epoch_ref[b] % 2
        for c in fetch(be_ref[b], slot):
            c.wait()

        @pl.when(next_ref[b] >= 0)
        def _():
            for c in fetch(next_ref[b], 1 - slot):
                c.start()

        wg_b[...] = w_f32[slot, 0].astype(BF16)
        wu_b[...] = w_f32[slot, 1].astype(BF16)
        wd_b[...] = w_f32[slot, 2].astype(BF16)

    def ffn(m):
        x = _load_row_tiles(xs_ref, m).astype(BF16)
        half = D_MODEL // 2
        acc = None
        for f in range(2):
            fs = slice(f * half, (f + 1) * half)
            gt = _dot(x, wg_b[:, fs]) + bg_ref[0, :, fs]
            up = _dot(x, wu_b[:, fs]) + bu_ref[0, :, fs]
            gt = jnp.minimum(gt, SWIGLU_LIMIT)
            up = jnp.clip(up, -SWIGLU_LIMIT, SWIGLU_LIMIT)
            hid = (up + 1.0) * gt * _sigmoid(SWIGLU_ALPHA * gt)
            part = _dot(hid.astype(BF16), wd_b[fs, :])
            acc = part if acc is None else acc + part
        _store_row_tiles(y_ref, acc + bd_ref[0])

    quarter = MOE_BLOCK // 4
    quarters = (rows_ref[b] + quarter - 1) // quarter
    for q in range(1, 5):
        @pl.when((b < nu_ref[0]) & (quarters == q))
        def _():
            ffn(q * quarter)


def _moe_ffn(block_expert, n_used, block_rows, xs, w_gate, w_up, w_down, b_gate, b_up, b_down):
    n_blocks = block_expert.shape[0]
    pos = jnp.arange(n_blocks, dtype=jnp.int32)
    change = (pos > 0) & (block_expert != jnp.roll(block_expert, 1))
    epoch = jnp.cumsum(change.astype(jnp.int32))
    later = change[None, :] & (pos[None, :] > pos[:, None])
    first = jnp.min(jnp.where(later, pos[None, :], n_blocks), axis=1)
    next_e = jnp.sum(jnp.where(pos[None, :] == first[:, None], block_expert[None, :], 0), axis=1)
    next_e = jnp.where(first < n_blocks, next_e, -1).astype(jnp.int32)

    row = lambda b, be, nu, ep, nx, br: (jnp.minimum(b, nu[0] - 1), 0)
    bspec = pl.BlockSpec((1, 1, D_MODEL), lambda b, be, nu, ep, nx, br: (be[b], 0, 0))
    wspec = pl.BlockSpec(memory_space=pl.ANY)
    grid_spec = pltpu.PrefetchScalarGridSpec(
        num_scalar_prefetch=5,
        grid=(n_blocks,),
        in_specs=[pl.BlockSpec((MOE_BLOCK * ROW_TILE, 128), row), wspec, wspec, wspec, bspec, bspec, bspec],
        out_specs=pl.BlockSpec((MOE_BLOCK * ROW_TILE, 128), row),
        scratch_shapes=[pltpu.VMEM((2, 3, D_MODEL, D_MODEL), F32)] + [pltpu.VMEM((D_MODEL, D_MODEL), BF16)] * 3
        + [pltpu.SemaphoreType.DMA((2,))],
    )
    return pl.pallas_call(
        _moe_kernel,
        grid_spec=grid_spec,
        out_shape=jax.ShapeDtypeStruct((n_blocks * MOE_BLOCK * ROW_TILE, 128), F32),
        compiler_params=_params(("arbitrary",)),
    )(block_expert, n_used, epoch, next_e, block_rows, xs, w_gate, w_up, w_down,
      b_gate.reshape(N_EXPERTS, 1, D_MODEL), b_up.reshape(N_EXPERTS, 1, D_MODEL),
      b_down.reshape(N_EXPERTS, 1, D_MODEL))


SEG_ALIGN = 8
GROUP_ROWS = SEG_ALIGN * ROW_TILE


def _local_rows(tm):
    return tm * TOP_K + N_EXPERTS * SEG_ALIGN


def _group_rows(ref, group):
    start = group * GROUP_ROWS
    if not isinstance(group, int):
        start = pl.multiple_of(start, GROUP_ROWS)
    return ref.at[pl.ds(start, GROUP_ROWS)]


def _group_copy(hbm, hbm_group, buf, buf_group, sem, to_hbm):
    h, b = _group_rows(hbm, hbm_group), _group_rows(buf, buf_group)
    return pltpu.make_async_copy(b, h, sem) if to_hbm else pltpu.make_async_copy(h, b, sem)


def _combine_kernel(grp_c, grp_n, lpos_ref, gate_ref, h_ref, gain_ref, y_hbm, o_ref, ybuf, fbuf, sems):
    i = pl.program_id(0)
    nt = pl.num_programs(0)
    tm = h_ref.shape[0]
    n_groups = _local_rows(tm) // SEG_ALIGN

    def issue(grp_ref, slot):
        def body(g, carry):
            _group_copy(y_hbm, grp_ref[0, 0, g], ybuf.at[slot], g, sems.at[slot], False).start()
            return carry
        lax.fori_loop(0, grp_ref[0, 0, n_groups], body, 0)

    @pl.when(i == 0)
    def _():
        issue(grp_c, 0)

    @pl.when(i + 1 < nt)
    def _():
        issue(grp_n, (i + 1) % 2)

    slot = i % 2

    yb = ybuf.at[slot]
    rows_in = pl.ds(0, pl.multiple_of(grp_c[0, 0, n_groups] * GROUP_ROWS, GROUP_ROWS))
    pltpu.make_async_copy(y_hbm.at[rows_in], yb.at[rows_in], sems.at[slot]).wait()

    def token_body(t, carry):
        acc = None
        for j in range(TOP_K):
            row = pl.multiple_of(lpos_ref[0, 0, t * TOP_K + j], ROW_TILE)
            term = gate_ref[0, 0, t * TOP_K + j] * yb[pl.ds(row, ROW_TILE), :]
            acc = term if acc is None else acc + term
        fbuf[pl.ds(pl.multiple_of(t * ROW_TILE, ROW_TILE), ROW_TILE), :] = acc
        return carry

    lax.fori_loop(0, tm, token_body, 0, unroll=8)
    f = h_ref[...] + _load_row_tiles(fbuf, tm)
    o_ref[...] = f * lax.rsqrt(jnp.mean(f * f, axis=-1, keepdims=True) + NORM_EPS) * gain_ref[...]


def _combine(h, y_rows, grp3, lpos3, gate3, gain, tm):
    n = h.shape[0]
    nt = n // tm
    n_local = _local_rows(tm)
    gw = grp3.shape[-1]
    smem = lambda shape, imap: pl.BlockSpec(shape, imap, memory_space=pltpu.SMEM)
    cur = lambda i: (i, 0, 0)
    nxt = lambda i: (jnp.minimum(i + 1, nt - 1), 0, 0)
    return pl.pallas_call(
        _combine_kernel,
        grid=(nt,),
        in_specs=[
            smem((1, 1, gw), cur), smem((1, 1, gw), nxt),
            smem((1, 1, TOP_K * tm), cur), smem((1, 1, TOP_K * tm), cur),
            pl.BlockSpec((tm, D_MODEL), lambda i: (i, 0)),
            pl.BlockSpec((1, D_MODEL), lambda i: (0, 0)),
            pl.BlockSpec(memory_space=pl.ANY),
        ],
        out_specs=pl.BlockSpec((tm, D_MODEL), lambda i: (i, 0)),
        out_shape=jax.ShapeDtypeStruct((n, D_MODEL), F32),
        scratch_shapes=[pltpu.VMEM((2, n_local * ROW_TILE, 128), F32), pltpu.VMEM((tm * ROW_TILE, 128), F32),
                        pltpu.SemaphoreType.DMA((2,))],
        compiler_params=pltpu.CompilerParams(dimension_semantics=("arbitrary",), vmem_limit_bytes=VMEM_LIMIT,
                                             disable_bounds_checks=True),
    )(grp3, grp3, lpos3, gate3, h, gain, y_rows)


def _dispatch_kernel(*refs, fill):
    if fill:
        grp_ref, grp_prev, lpos_ref, ends_ref, x_ref, xs_hbm, sorted_buf, zero_scr, sems, zsem = refs
    else:
        grp_ref, grp_prev, lpos_ref, x_ref, _, xs_hbm, sorted_buf, sems = refs
    i = pl.program_id(0)
    tm = x_ref.shape[0] // ROW_TILE
    n_groups = _local_rows(tm) // SEG_ALIGN
    blk = MOE_BLOCK * ROW_TILE
    sorted_scr = sorted_buf.at[i % 2]
    sem = sems.at[i % 2]

    if fill:
        def fill_copy(e):
            start = pl.multiple_of((ends_ref[0, e] - MOE_BLOCK) * ROW_TILE, blk)
            return pltpu.make_async_copy(zero_scr, xs_hbm.at[pl.ds(start, blk)], zsem)

        @pl.when(i == 0)
        def _():
            zero_scr[...] = jnp.zeros(zero_scr.shape, zero_scr.dtype)
            for e in range(N_EXPERTS):
                @pl.when(ends_ref[1, e] > 0)
                def _():
                    fill_copy(e).start()

    sorted_scr[...] = jnp.zeros(sorted_scr.shape, sorted_scr.dtype)

    def move(t, carry):
        row = x_ref[pl.ds(pl.multiple_of(t * ROW_TILE, ROW_TILE), ROW_TILE), :]
        for j in range(TOP_K):
            dst = pl.multiple_of(lpos_ref[0, 0, t * TOP_K + j], ROW_TILE)
            sorted_scr[pl.ds(dst, ROW_TILE), :] = row
        return carry

    lax.fori_loop(0, tm, move, 0, unroll=8)

    def wait_groups(n, buf, s):
        rows_out = pl.ds(0, pl.multiple_of(n * GROUP_ROWS, GROUP_ROWS))
        pltpu.make_async_copy(buf.at[rows_out], xs_hbm.at[rows_out], s).wait()

    if fill:
        @pl.when(i == 0)
        def _():
            for e in range(N_EXPERTS):
                @pl.when(ends_ref[1, e] > 0)
                def _():
                    fill_copy(e).wait()

    @pl.when(i > 0)
    def _():
        wait_groups(grp_prev[0, 0, n_groups], sorted_buf.at[(i + 1) % 2], sems.at[(i + 1) % 2])

    n_used = grp_ref[0, 0, n_groups]

    def send(g, carry):
        _group_copy(xs_hbm, grp_ref[0, 0, g], sorted_scr, g, sem, True).start()
        return carry

    lax.fori_loop(0, n_used, send, 0)

    @pl.when(i == pl.num_programs(0) - 1)
    def _():
        wait_groups(n_used, sorted_scr, sem)


def _dispatch(xn, grp3, lpos3, tm, n_slots, ends=None, xs=None):
    n = xn.shape[0] // ROW_TILE
    fill = xs is None
    smem = lambda shape, imap: pl.BlockSpec(shape, imap, memory_space=pltpu.SMEM)
    in_specs = [smem((1, 1, grp3.shape[-1]), lambda i: (i, 0, 0)),
                smem((1, 1, grp3.shape[-1]), lambda i: (jnp.maximum(i - 1, 0), 0, 0)),
                smem((1, 1, TOP_K * tm), lambda i: (i, 0, 0))]
    args = [grp3, grp3, lpos3]
    scratch = [pltpu.VMEM((2, _local_rows(tm) * ROW_TILE, 128), F32)]
    if fill:
        in_specs.append(smem((2, N_EXPERTS), lambda i: (0, 0)))
        args.append(ends)
        scratch.append(pltpu.VMEM((MOE_BLOCK * ROW_TILE, 128), F32))
    in_specs.append(pl.BlockSpec((tm * ROW_TILE, 128), lambda i: (i, 0)))
    args.append(xn)
    aliases = {}
    if not fill:
        in_specs.append(pl.BlockSpec(memory_space=pl.ANY))
        args.append(xs)
        aliases = {len(args) - 1: 0}
    scratch.append(pltpu.SemaphoreType.DMA((2,)))
    if fill:
        scratch.append(pltpu.SemaphoreType.DMA(()))
    return pl.pallas_call(
        functools.partial(_dispatch_kernel, fill=fill),
        grid=(n // tm,),
        in_specs=in_specs,
        out_specs=pl.BlockSpec(memory_space=pl.ANY),
        out_shape=jax.ShapeDtypeStruct((n_slots * ROW_TILE, 128), F32),
        scratch_shapes=scratch,
        input_output_aliases=aliases,
        compiler_params=pltpu.CompilerParams(dimension_semantics=("arbitrary",), vmem_limit_bytes=VMEM_LIMIT,
                                             disable_bounds_checks=True, has_side_effects=True),
    )(*args)


def _pad_rows(w, rows, offset):
    out = jnp.zeros((rows, w.shape[1]), w.dtype)
    return out.at[offset : offset + w.shape[0]].set(w)


def _routing_tables(counts, n_pairs):
    n_tiles = counts.shape[0]
    n_blocks = (n_pairs + n_tiles * N_EXPERTS * (SEG_ALIGN - 1) + N_EXPERTS * (MOE_BLOCK - 1)
                + MOE_BLOCK - 1) // MOE_BLOCK
    runs = (counts + SEG_ALIGN - 1) // SEG_ALIGN * SEG_ALIGN
    local_start = jnp.cumsum(runs, axis=1) - runs
    total = jnp.sum(runs, axis=0)
    padded = (total + MOE_BLOCK - 1) // MOE_BLOCK * MOE_BLOCK
    pends = jnp.cumsum(padded)
    pstarts = pends - padded
    global_start = pstarts[None, :] + jnp.cumsum(runs, axis=0) - runs
    blocks = jnp.arange(n_blocks, dtype=jnp.int32) * MOE_BLOCK
    n_used = (pends[-1] // MOE_BLOCK).astype(jnp.int32)
    owner = jnp.sum((pends[None, :] <= blocks[:, None]).astype(jnp.int32), axis=1)
    block_expert = jnp.minimum(owner, N_EXPERTS - 1)
    last = jnp.sum(jnp.where(jnp.arange(n_blocks) == n_used - 1, block_expert, 0))
    block_expert = jnp.where(jnp.arange(n_blocks) < n_used, block_expert, last)
    ends = jnp.stack([pends, padded]).astype(jnp.int32)
    row_end = jnp.sum(jnp.where(block_expert[:, None] == jnp.arange(N_EXPERTS), (pstarts + total)[None, :], 0), axis=1)
    block_rows = jnp.clip(row_end - blocks, 0, MOE_BLOCK).astype(jnp.int32)
    return (runs, local_start, global_start, ends, block_expert.astype(jnp.int32), n_used.reshape(1), block_rows,
            n_blocks)


def _tile_tables(idx3, rank3, runs, local_start, global_start, tm):
    experts = jnp.arange(N_EXPERTS, dtype=jnp.int32)
    hit = idx3[..., None] == experts
    lpos = rank3 + jnp.sum(jnp.where(hit, local_start[:, None, None, :], 0), axis=-1)
    n_groups = _local_rows(tm) // SEG_ALIGN
    g = jnp.arange(n_groups, dtype=jnp.int32)
    run_end = (local_start + runs) // SEG_ALIGN
    owner = jnp.minimum(jnp.sum((run_end[:, None, :] <= g[None, :, None]).astype(jnp.int32), axis=-1), N_EXPERTS - 1)
    sel = owner[..., None] == experts
    offset = jnp.sum(jnp.where(sel, (global_start - local_start)[:, None, :], 0), axis=-1) // SEG_ALIGN
    used = jnp.sum(runs, axis=1) // SEG_ALIGN
    table = jnp.where(g[None, :] < used[:, None], g[None, :] + offset, 0)
    tail = jnp.broadcast_to(used[:, None], (used.shape[0], 8))
    lpos_rows = _token_major(lpos.astype(jnp.int32) * ROW_TILE)
    return lpos_rows, jnp.concatenate([table, tail], axis=1).astype(jnp.int32)[:, None, :]


def _token_major(t3):
    return jnp.swapaxes(t3, 1, 2).reshape(t3.shape[0], 1, -1)


def kernel(x_prompt, x_sample, state_rwkv_shift, state_rwkv_wkv, state_gla, norm_mix, w_in, rw_mu, rw_w0, rw_w2, rw_a0, rw_a2, rw_g2, rw_k_k, rw_k_a, rw_r_k, rw_ln_w, rw_ln_b, gla_gk_w2, gla_gk_b, gla_norm_w, w_out, norm_ffn, w_router, b_router, w_gate, b_gate, w_up, b_up, w_down, b_down, norm_final):
    depth = norm_mix.shape[0]
    assert depth == 1
    bp, lp, d = x_prompt.shape
    bs, ls, _ = x_sample.shape
    assert ls == 1 and lp % SEQ_BLOCK == 0
    l = 0
    row = lambda t: t.reshape(1, -1)

    w_in_b = w_in[l].astype(BF16)
    w_in_r = w_in_b[:, :RW_PROJ]
    w_in_g = jnp.pad(w_in_b[:, RW_PROJ:], ((0, 0), (0, GLA_PROJ_PAD - GLA_PROJ)))
    rw = dict(
        mu=row(rw_mu[l]), w0=row(rw_w0[l]), a0=row(rw_a0[l]),
        w2p=_pad_rows(rw_w2[l].astype(BF16), 128, 0), a2p=_pad_rows(rw_a2[l].astype(BF16), 128, 64),
        g2=rw_g2[l].astype(BF16), k_k=row(rw_k_k[l]), k_a=row(rw_k_a[l]), r_k=row(rw_r_k[l]),
        ln_w=row(rw_ln_w[l]), ln_b=row(rw_ln_b[l]))
    gl = dict(gkw=_pad_rows(gla_gk_w2[l].astype(BF16), GLA_LORA_PAD, 0), gkb=row(gla_gk_b[l]),
              norm_w=row(gla_norm_w[l]))
    gain_mix = row(norm_mix[l])

    n_p = bp * lp
    xp = x_prompt.reshape(n_p, d)
    zr_p, zg_p = _inproj(xp, gain_mix, w_in_r, w_in_g, TOK_BLOCK)
    zr_p3 = zr_p.reshape(bp, lp, RW_PROJ)
    o_rw_p, wkv_p = _rwkv_seq(zr_p3, jnp.zeros((bp, 1, RW_PROJ), F32),
                              jnp.zeros((bp, RW_HEADS, RW_N, RW_N), F32), rw, SEQ_BLOCK)
    o_gl_p, gla_p = _gla_seq(zg_p.reshape(bp, lp, GLA_PROJ_PAD),
                             jnp.zeros((bp, GLA_HEADS, GLA_DK, GLA_DV), F32), gl, SEQ_BLOCK)
    shift_p = zr_p3[:, -1, :]

    xs_ = x_sample.reshape(bs, d)
    zr_s, zg_s = _inproj(xs_, gain_mix, w_in_r, w_in_g, bs)
    r, k, v, al, be, dec, g_rw, bonus = _rwkv_step_prep(zr_s, state_rwkv_shift[l], rw)
    wkv_s, o_rw_s = _rwkv_step(state_rwkv_wkv[l], r, k, al, be, dec, v)
    q, kg, vg, g_gl, dec_g = _gla_step_prep(zg_s, gl)
    gla_s, o_gl_s = _gla_step(state_gla[l], q, kg, dec_g, vg, 16)
    o_rw_s2, o_gl_s2 = _step_post(o_rw_s, bonus, g_rw, rw, o_gl_s, g_gl, gl)
    shift_s = zr_s

    w_out_b = w_out[l].astype(BF16)
    router = (w_out_b, row(norm_ffn[l]), w_router[l].T, b_router[l].reshape(N_EXPERTS, 1))
    h_p, xn_p, idx_p, gate_p, rank_p, cnt_p = _outproj_router(
        xp, o_rw_p.reshape(n_p, RW_WIDTH), o_gl_p.reshape(n_p, GLA_WIDTH), *router, TOK_BLOCK)
    h_s, xn_s, idx_s, gate_s, rank_s, cnt_s = _outproj_router(xs_, o_rw_s2, o_gl_s2, *router, bs)
    nt_p = n_p // TOK_BLOCK
    counts = jnp.concatenate([cnt_p[:, :, 0], cnt_s[:, :, 0]], axis=0)
    runs, lstart, gstart, ends, block_expert, n_used, block_rows, n_blocks = _routing_tables(
        counts, (n_p + bs) * TOP_K)
    n_slots = n_blocks * MOE_BLOCK
    lpos_p, grp_p = _tile_tables(idx_p, rank_p, runs[:nt_p], lstart[:nt_p], gstart[:nt_p], TOK_BLOCK)
    lpos_s, grp_s = _tile_tables(idx_s, rank_s, runs[nt_p:], lstart[nt_p:], gstart[nt_p:], bs)
    xs_rows = _dispatch(xn_p, grp_p, lpos_p, TOK_BLOCK, n_slots, ends=ends)
    xs_rows = _dispatch(xn_s, grp_s, lpos_s, bs, n_slots, xs=xs_rows)
    y_rows = _moe_ffn(block_expert, n_used, block_rows, xs_rows, w_gate[l], w_up[l], w_down[l], b_gate[l], b_up[l], b_down[l])
    gain_f = row(norm_final)
    y_p = _combine(h_p, y_rows, grp_p, lpos_p, _token_major(gate_p), gain_f, TOK_BLOCK)
    y_s = _combine(h_s, y_rows, grp_s, lpos_s, _token_major(gate_s), gain_f, bs)

    y_prompt = y_p.reshape(bp, lp, d)
    y_sample = y_s.reshape(bs, ls, d)
    return (y_prompt, y_sample, shift_p[None], wkv_p[None], gla_p[None], shift_s[None], wkv_s[None], gla_s[None])
```

```python
import functools

import jax
import jax.numpy as jnp
from jax import lax
from jax.experimental import pallas as pl
from jax.experimental.pallas import tpu as pltpu

F32 = jnp.float32
BF16 = jnp.bfloat16
HIGHEST = lax.Precision.HIGHEST

D_MODEL = 1024
RW_WIDTH = 512
RW_HEADS = 8
RW_N = 64
RW_PROJ = 1792
RW_GN_EPS = 64e-5
GLA_HEADS = 4
GLA_DK = 64
GLA_DV = 128
GLA_WIDTH = 512
GLA_QK = GLA_HEADS * GLA_DK
GLA_PROJ = 1552
GLA_PROJ_PAD = 1664
GLA_LORA_PAD = 128
GLA_GATE_NORMALIZER = 16.0
N_EXPERTS = 32
TOP_K = 4
SWIGLU_LIMIT = 7.0
SWIGLU_ALPHA = 1.702
NORM_EPS = 1e-5
LOG2_E = 1.4426950408889634

RW_CHUNK = 64
GLA_CHUNK = 16
SEQ_BLOCK = 512
TOK_BLOCK = 512
MOE_BLOCK = 512
VMEM_LIMIT = 56 * 1024 * 1024


def _dot(a, b, precision=None):
    return jnp.dot(a, b, preferred_element_type=F32, precision=precision)


def _dot_nt(a, b, precision=None):
    return lax.dot_general(a, b, (((1,), (1,)), ((), ())), preferred_element_type=F32, precision=precision)


def _dot_tn(a, b, precision=None):
    return lax.dot_general(a, b, (((0,), (0,)), ((), ())), preferred_element_type=F32, precision=precision)


def _sigmoid(x):
    return 1.0 / (1.0 + jnp.exp(-x))


def _softplus(x):
    return jnp.maximum(x, 0.0) + jnp.log(1.0 + jnp.exp(-jnp.abs(x)))


def _params(sem):
    return pltpu.CompilerParams(dimension_semantics=sem, vmem_limit_bytes=VMEM_LIMIT)


ROW_TILE = D_MODEL // 128


def _store_row_tiles(ref, x):
    m = x.shape[0]
    for c in range(ROW_TILE):
        ref[pl.ds(c, m, stride=ROW_TILE), :] = x[:, c * 128 : (c + 1) * 128]


def _load_row_tiles(ref, m):
    return jnp.concatenate([ref[pl.ds(c, m, stride=ROW_TILE), :] for c in range(ROW_TILE)], axis=-1)


def _inproj_kernel(x_ref, gain_ref, wr_ref, wg_ref, zr_ref, zg_ref):
    x = x_ref[...]
    xn = x * lax.rsqrt(jnp.mean(x * x, axis=-1, keepdims=True) + NORM_EPS) * gain_ref[...]
    xb = xn.astype(BF16)
    zr_ref[...] = _dot(xb, wr_ref[...])
    zg_ref[...] = _dot(xb, wg_ref[...])


def _inproj(x, gain, w_r, w_g, tm):
    n = x.shape[0]
    return pl.pallas_call(
        _inproj_kernel,
        grid=(n // tm,),
        in_specs=[
            pl.BlockSpec((tm, D_MODEL), lambda i: (i, 0)),
            pl.BlockSpec((1, D_MODEL), lambda i: (0, 0)),
            pl.BlockSpec((D_MODEL, RW_PROJ), lambda i: (0, 0)),
            pl.BlockSpec((D_MODEL, GLA_PROJ_PAD), lambda i: (0, 0)),
        ],
        out_specs=[
            pl.BlockSpec((tm, RW_PROJ), lambda i: (i, 0)),
            pl.BlockSpec((tm, GLA_PROJ_PAD), lambda i: (i, 0)),
        ],
        out_shape=[
            jax.ShapeDtypeStruct((n, RW_PROJ), F32),
            jax.ShapeDtypeStruct((n, GLA_PROJ_PAD), F32),
        ],
        compiler_params=_params(("parallel",)),
    )(x, gain, w_r, w_g)


def _rwkv_features(zs, w0, w2p, a0, a2p, g2, k_k, k_a):
    W = RW_WIDTH
    r = zs[:, 0:W]
    k_raw = zs[:, W : 2 * W]
    v = zs[:, 2 * W : 3 * W]
    zwa = zs[:, 3 * W : 3 * W + 128]
    zg = zs[:, 3 * W + 128 :]
    w = -_softplus(-(w0 + _dot(jnp.tanh(zwa).astype(BF16), w2p))) - 0.5
    log_decay = -jnp.exp(w)
    a = _sigmoid(a0 + _dot(zwa.astype(BF16), a2p))
    g = _dot(_sigmoid(zg).astype(BF16), g2)
    kk_raw = k_raw * k_k
    k = k_raw * (1.0 + (a - 1.0) * k_a)
    return r, k, v, kk_raw, a, log_decay, g


def _level_mask(ri, ci, lvl):
    same = (ri >> (lvl + 1)) == (ci >> (lvl + 1))
    return same & (((ri >> lvl) & 1) == 1) & (((ci >> lvl) & 1) == 0)


def _rwkv_seq_kernel(z_ref, shift0_ref, s0_ref, mu_ref, w0_ref, w2_ref, a0_ref, a2_ref, g2_ref, kk_ref, ka_ref,
                     rk_ref, lnw_ref, lnb_ref, o_ref, sout_ref,
                     m_scr, prev_scr, r_scr, k_scr, v_scr, kkr_scr, a_scr, lw_scr, on_scr, bon_scr):
    C = RW_CHUNK
    N = RW_N
    t_idx = pl.program_id(1)
    tb = z_ref.shape[1]
    zero_nn = jnp.zeros((N, N), F32)

    @pl.when(t_idx == 0)
    def _():
        prev_scr[...] = shift0_ref[0]
        for p in range(RW_HEADS // 2):
            top = jnp.concatenate([s0_ref[0, 2 * p].T, zero_nn], axis=1)
            bot = jnp.concatenate([zero_nn, s0_ref[0, 2 * p + 1].T], axis=1)
            m_scr[p] = jnp.concatenate([top, bot], axis=0)

    z = z_ref[0]
    row = lax.broadcasted_iota(jnp.int32, z.shape, 0)
    z_prev = jnp.where(row == 0, prev_scr[...], pltpu.roll(z, 1, axis=0))
    prev_scr[...] = z[tb - 1 : tb, :]
    zs = z + mu_ref[...] * (z_prev - z)
    r, k, v, kk_raw, a, log_decay, g = _rwkv_features(
        zs, w0_ref[...], w2_ref[...], a0_ref[...], a2_ref[...], g2_ref[...], kk_ref[...], ka_ref[...])
    P2 = 2 * N
    left1 = lax.broadcasted_iota(jnp.int32, (1, P2), 1) < N

    def head_sum(x):
        s0 = jnp.sum(jnp.where(left1, x, 0.0), axis=-1, keepdims=True)
        s1 = jnp.sum(jnp.where(left1, 0.0, x), axis=-1, keepdims=True)
        return jnp.where(left1, s0, s1)

    def head_sum_wide(x):
        return jnp.concatenate([head_sum(x[:, p * P2 : (p + 1) * P2]) for p in range(RW_HEADS // 2)], axis=1)

    alpha = kk_raw * lax.rsqrt(jnp.maximum(head_sum_wide(kk_raw * kk_raw), 1e-24))
    r_scr[...] = r
    k_scr[...] = k
    v_scr[...] = v
    kkr_scr[...] = alpha
    a_scr[...] = alpha * a
    lw_scr[...] = log_decay
    bon_scr[...] = head_sum_wide(r * k * rk_ref[...]) * v

    ri = lax.broadcasted_iota(jnp.int32, (C, P2), 0)
    ci = lax.broadcasted_iota(jnp.int32, (C, P2), 1) % N
    left = lax.broadcasted_iota(jnp.int32, (C, P2), 1) < N
    tril = ri >= ci
    stril = ri > ci
    eye_f = (ri == ci).astype(F32)
    rb = lax.broadcasted_iota(jnp.int32, (P2, P2), 0)
    cb = lax.broadcasted_iota(jnp.int32, (P2, P2), 1)
    same_head = (rb < N) == (cb < N)
    eye_b = rb == cb
    rc = lax.broadcasted_iota(jnp.int32, (C, C), 0)
    cc = lax.broadcasted_iota(jnp.int32, (C, C), 1)
    tril_f = (rc >= cc).astype(F32)

    def bdiag(x):
        return jnp.concatenate([jnp.where(left, x, 0.0), jnp.where(left, 0.0, x)], axis=0)

    n_sub = tb // C
    pairs = range(RW_HEADS // 2)

    def chunk_body(it, carry):
        units = [(s, p) for s in range(n_sub) for p in pairs]
        sls = [pl.ds(pl.multiple_of((it * n_sub + s) * C, C), C) for s in range(n_sub)]
        prep = []
        for s in range(n_sub):
            lw = lw_scr[sls[s], :]
            cum = _dot(tril_f, lw, precision=HIGHEST)
            cum_last = cum[C - 1 : C, :]
            prep.append(dict(
                e_incl=jnp.exp(cum), e_excl=jnp.exp(cum - lw), e_neg=jnp.exp(-cum),
                e_tail=jnp.exp(cum_last - cum), p_last=jnp.exp(cum_last),
                r=r_scr[sls[s], :], k=k_scr[sls[s], :], v=v_scr[sls[s], :], kk=kkr_scr[sls[s], :],
                a=a_scr[sls[s], :]))
        lanes = [slice(p * P2, (p + 1) * P2) for p in pairs]
        get = lambda name: [prep[s][name][:, lanes[p]] for s, p in units]
        r2, k2, v2, al, be = get("r"), get("k"), get("v"), get("kk"), get("a")
        e_incl, e_excl, e_neg, e_tail, p_last = get("e_incl"), get("e_excl"), get("e_neg"), get("e_tail"), get("p_last")
        un = range(len(units))
        al_t = [al[u] * e_excl[u] for u in un]
        r_t = [r2[u] * e_incl[u] for u in un]
        be_n = [be[u] * e_neg[u] for u in un]
        k_n = [k2[u] * e_neg[u] for u in un]
        k_et = [(k2[u] * e_tail[u]).T for u in un]
        be_et = [(be[u] * e_tail[u]).T for u in un]
        v_bd = [bdiag(v2[u]) for u in un]
        lhs = [jnp.concatenate([al_t[u], r_t[u]], axis=0) for u in un]
        s_b = [_dot_nt(lhs[u], bdiag(be_n[u])) for u in un]
        s_k = [_dot_nt(lhs[u], bdiag(k_n[u])) for u in un]
        l_ab = [jnp.where(stril, s_b[u][:C], 0.0) for u in un]
        a_rb = [jnp.where(tril, s_b[u][C:], 0.0) for u in un]
        l_ak = [jnp.where(stril, s_k[u][:C], 0.0) for u in un]
        a_rk = [jnp.where(tril, s_k[u][C:], 0.0) for u in un]
        lakv = [_dot(l_ak[u], v_bd[u]) for u in un]
        arkv = [_dot(a_rk[u], v_bd[u]) for u in un]
        kev = [_dot(k_et[u], v2[u]) for u in un]
        t_inv = [eye_f - jnp.where(_level_mask(ri, ci, 0), l_ab[u], 0.0) for u in un]
        lvl = 1
        while (1 << lvl) < C:
            lm = _level_mask(ri, ci, lvl)
            tn = [_dot(t_inv[u], bdiag(jnp.where(lm, l_ab[u], 0.0))) for u in un]
            t_inv = [t_inv[u] - _dot(tn[u], bdiag(t_inv[u])) for u in un]
            lvl += 1
        a_til = [_dot(t_inv[u], bdiag(al_t[u])) for u in un]
        b_til = [_dot(t_inv[u], bdiag(lakv[u])) for u in un]
        r_hat = [r_t[u] - _dot(a_rb[u], bdiag(a_til[u])) for u in un]
        o_hat = [arkv[u] - _dot(a_rb[u], bdiag(b_til[u])) for u in un]
        g_bd = [jnp.where(same_head, jnp.where(eye_b, p_last[u], 0.0) - _dot(be_et[u], a_til[u]), 0.0) for u in un]
        h_bd = [jnp.where(same_head, kev[u] - _dot(be_et[u], b_til[u]), 0.0) for u in un]
        lhs_m = [jnp.concatenate([r_hat[u], g_bd[u]], axis=0) for u in un]
        for u, (s, p) in enumerate(units):
            res = _dot(lhs_m[u], m_scr[p])
            m_scr[p] = res[C:] + h_bd[u]
            o_p = res[:C] + o_hat[u]
            cen = o_p - head_sum(o_p) * (1.0 / N)
            var = head_sum(cen * cen) * (1.0 / N)
            on_scr[sls[s], lanes[p]] = cen * lax.rsqrt(var + RW_GN_EPS)
        return carry

    lax.fori_loop(0, tb // (C * n_sub), chunk_body, 0)
    out = (on_scr[...] * lnw_ref[...] + lnb_ref[...] + bon_scr[...]) * g
    o_ref[0] = out.astype(o_ref.dtype)

    @pl.when(t_idx == pl.num_programs(1) - 1)
    def _():
        for p in range(RW_HEADS // 2):
            m = m_scr[p]
            sout_ref[0, 2 * p] = m[:N, :N].T
            sout_ref[0, 2 * p + 1] = m[N:, N:].T


def _rwkv_seq(z3, shift0, s0, rw, tb):
    b, l, _ = z3.shape
    const = lambda shape: pl.BlockSpec(shape, lambda i, j: (0,) * len(shape))
    wide = lambda: pltpu.VMEM((tb, RW_WIDTH), F32)
    return pl.pallas_call(
        _rwkv_seq_kernel,
        grid=(b, l // tb),
        in_specs=[
            pl.BlockSpec((1, tb, RW_PROJ), lambda i, j: (i, j, 0)),
            pl.BlockSpec((1, 1, RW_PROJ), lambda i, j: (i, 0, 0)),
            pl.BlockSpec((1, RW_HEADS, RW_N, RW_N), lambda i, j: (i, 0, 0, 0)),
            const((1, RW_PROJ)),
            const((1, RW_WIDTH)), const((128, RW_WIDTH)),
            const((1, RW_WIDTH)), const((128, RW_WIDTH)),
            const((128, RW_WIDTH)),
            const((1, RW_WIDTH)), const((1, RW_WIDTH)), const((1, RW_WIDTH)),
            const((1, RW_WIDTH)), const((1, RW_WIDTH)),
        ],
        out_specs=[
            pl.BlockSpec((1, tb, RW_WIDTH), lambda i, j: (i, j, 0)),
            pl.BlockSpec((1, RW_HEADS, RW_N, RW_N), lambda i, j: (i, 0, 0, 0)),
        ],
        out_shape=[
            jax.ShapeDtypeStruct((b, l, RW_WIDTH), BF16),
            jax.ShapeDtypeStruct((b, RW_HEADS, RW_N, RW_N), F32),
        ],
        scratch_shapes=[
            pltpu.VMEM((RW_HEADS // 2, 2 * RW_N, 2 * RW_N), F32),
            pltpu.VMEM((1, RW_PROJ), F32),
            wide(), wide(), wide(), wide(), wide(), wide(), wide(), wide(),
        ],
        compiler_params=_params(("parallel", "arbitrary")),
    )(z3, shift0, s0, rw["mu"], rw["w0"], rw["w2p"], rw["a0"], rw["a2p"], rw["g2"], rw["k_k"], rw["k_a"],
      rw["r_k"], rw["ln_w"], rw["ln_b"])


def _rwkv_step_prep_kernel(z_ref, shift0_ref, mu_ref, w0_ref, w2_ref, a0_ref, a2_ref, g2_ref, kk_ref, ka_ref,
                           rk_ref, r_ref, k_ref, v_ref, al_ref, be_ref, dec_ref, g_ref, bon_ref):
    z = z_ref[...]
    zs = z + mu_ref[...] * (shift0_ref[...] - z)
    r, k, v, kk_raw, a, log_decay, g = _rwkv_features(
        zs, w0_ref[...], w2_ref[...], a0_ref[...], a2_ref[...], g2_ref[...], kk_ref[...], ka_ref[...])
    rk_all = rk_ref[...]
    for h in range(RW_HEADS):
        hs = slice(h * RW_N, (h + 1) * RW_N)
        kk_h = kk_raw[:, hs]
        nrm = jnp.sqrt(jnp.sum(kk_h * kk_h, axis=-1, keepdims=True))
        al = kk_h / jnp.maximum(nrm, 1e-12)
        al_ref[:, hs] = al
        be_ref[:, hs] = al * a[:, hs]
        bon_ref[:, hs] = jnp.sum(r[:, hs] * k[:, hs] * rk_all[:, hs], axis=-1, keepdims=True) * v[:, hs]
    r_ref[...] = r
    k_ref[...] = k
    v_ref[...] = v
    dec_ref[...] = jnp.exp(log_decay)
    g_ref[...] = g


def _rwkv_step_prep(z, shift0, rw):
    n = z.shape[0]
    out = jax.ShapeDtypeStruct((n, RW_WIDTH), F32)
    return pl.pallas_call(
        _rwkv_step_prep_kernel,
        out_shape=[out] * 8,
        compiler_params=pltpu.CompilerParams(vmem_limit_bytes=VMEM_LIMIT),
    )(z, shift0, rw["mu"], rw["w0"], rw["w2p"], rw["a0"], rw["a2p"], rw["g2"], rw["k_k"], rw["k_a"], rw["r_k"])


def _rwkv_step_kernel(s_ref, r_ref, k_ref, al_ref, be_ref, dec_ref, v_ref, snew_ref, o_ref):
    r, k, al, be, dec = r_ref[...], k_ref[...], al_ref[...], be_ref[...], dec_ref[...]

    def body(g, carry):
        rows = pl.ds(pl.multiple_of(g * 8, 8), 8)
        v8 = v_ref[rows, :]
        outs = []
        for j in range(8):
            s = s_ref[0, g * 8 + j]
            sa = -jnp.sum(s * al, axis=0, keepdims=True)
            s_new = s * dec + sa * be + v8[j : j + 1, :] * k
            snew_ref[0, g * 8 + j] = s_new
            outs.append(jnp.sum(s_new * r, axis=0, keepdims=True))
        o_ref[rows, :] = jnp.concatenate(outs, axis=0)
        return carry

    lax.fori_loop(0, RW_N // 8, body, 0)


def _rwkv_step(s0, r, k, al, be, dec, v):
    n = s0.shape[0]
    s_t = jnp.transpose(s0, (1, 2, 3, 0))
    s_spec = pl.BlockSpec((1, RW_N, RW_N, n), lambda h: (h, 0, 0, 0))
    op_spec = pl.BlockSpec((RW_N, n), lambda h: (h, 0))
    s_new_t, o_t = pl.pallas_call(
        _rwkv_step_kernel,
        grid=(RW_HEADS,),
        in_specs=[s_spec] + [op_spec] * 6,
        out_specs=[s_spec, op_spec],
        out_shape=[
            jax.ShapeDtypeStruct(s_t.shape, F32),
            jax.ShapeDtypeStruct((RW_WIDTH, n), F32),
        ],
        compiler_params=_params(("parallel",)),
    )(s_t, r.T, k.T, al.T, be.T, dec.T, v.T)
    return jnp.transpose(s_new_t, (3, 0, 1, 2)), o_t.T


def _gla_features(z, gkw, gkb):
    q = z[:, 0:GLA_QK] * (GLA_DK ** -0.5)
    k = z[:, GLA_QK : 2 * GLA_QK]
    v = z[:, 2 * GLA_QK : 2 * GLA_QK + GLA_WIDTH]
    g = z[:, 2 * GLA_QK + GLA_WIDTH : 2 * GLA_QK + 2 * GLA_WIDTH]
    zgk = z[:, 2 * GLA_QK + 2 * GLA_WIDTH :]
    gk = -_softplus(-(_dot(zgk.astype(BF16), gkw) + gkb)) / GLA_GATE_NORMALIZER
    return q, k, v, g, gk


def _gla_finish(o, g, norm_w):
    outs = []
    for h in range(GLA_HEADS):
        hs = slice(h * GLA_DV, (h + 1) * GLA_DV)
        o_h = o[:, hs]
        o_h = o_h * lax.rsqrt(jnp.mean(o_h * o_h, axis=-1, keepdims=True) + NORM_EPS) * norm_w
        g_h = g[:, hs]
        outs.append(o_h * (g_h * _sigmoid(g_h)))
    return jnp.concatenate(outs, axis=-1)


def _gla_seq_kernel(z_ref, s0_ref, gkw_ref, gkb_ref, nw_ref, wsel_ref, o_ref, sout_ref,
                    st_scr, x_scr, gc_scr, oi_scr):
    C = GLA_CHUNK
    G = 128
    t_idx = pl.program_id(1)
    tb = z_ref.shape[1]
    nc = tb // C
    zero_vk = jnp.zeros((GLA_DV, GLA_DK), F32)

    @pl.when(t_idx == 0)
    def _():
        for p in range(GLA_HEADS // 2):
            top = jnp.concatenate([s0_ref[0, 2 * p].T, zero_vk], axis=1)
            bot = jnp.concatenate([zero_vk, s0_ref[0, 2 * p + 1].T], axis=1)
            st_scr[p] = jnp.concatenate([top, bot], axis=0)

    q, k, v, g, gk = _gla_features(z_ref[0], gkw_ref[...], gkb_ref[...])
    ri = lax.broadcasted_iota(jnp.int32, (G, G), 0)
    ci = lax.broadcasted_iota(jnp.int32, (G, G), 1)
    cum_mat = ((ri // C == ci // C) & (ri >= ci)).astype(F32)
    for m in range(tb // G):
        rows = slice(m * G, (m + 1) * G)
        gc_scr[rows, :] = _dot(cum_mat, gk[rows, :], precision=HIGHEST)
    gcum = gc_scr[...]

    rg = lax.broadcasted_iota(jnp.int32, (tb, 2 * G), 0)
    cg = lax.broadcasted_iota(jnp.int32, (tb, 2 * G), 1)
    blk_mask = ((cg % G) // C == (rg % G) // C) & (cg % C <= rg % C)
    for p in range(GLA_HEADS // 2):
        ls = slice(p * 128, (p + 1) * 128)
        q3 = q[:, ls].reshape(nc, C, 128)
        k3 = k[:, ls].reshape(nc, C, 128)
        g3 = gcum[:, ls].reshape(nc, C, 128) * LOG2_E
        half = C // 2
        for j in range(C):
            lo = 0 if j < half else half
            e = (q3[:, lo:] * jnp.exp2(jnp.minimum(g3[:, lo:] - g3[:, j : j + 1, :], 0.0))) * k3[:, j : j + 1, :]
            if lo:
                e = jnp.concatenate([jnp.zeros((nc, lo, 128), F32), e], axis=1)
            x_scr[:, j * 128 : (j + 1) * 128] = e.reshape(tb, 128).astype(BF16)
        a_t = jnp.where(blk_mask, _dot(x_scr[...], wsel_ref[...]), 0.0).astype(BF16)
        for hl in range(2):
            h = 2 * p + hl
            for m in range(tb // G):
                rows = slice(m * G, (m + 1) * G)
                a_blk = a_t[rows, hl * G : (hl + 1) * G]
                oi_scr[rows, h * GLA_DV : (h + 1) * GLA_DV] = _dot(
                    a_blk, v[rows, h * GLA_DV : (h + 1) * GLA_DV].astype(BF16))

    CG = G // C
    rt = lax.broadcasted_iota(jnp.int32, (G, CG * 128), 0)
    ct = lax.broadcasted_iota(jnp.int32, (G, CG * 128), 1)
    own_chunk = rt // C == ct // 128
    rs = lax.broadcasted_iota(jnp.int32, (2 * GLA_DV, CG * 128), 0)
    cs = lax.broadcasted_iota(jnp.int32, (2 * GLA_DV, CG * 128), 1)
    same_head = rs // GLA_DV == (cs % 128) // GLA_DK

    def chunk_diag(x):
        return jnp.where(own_chunk, jnp.concatenate([x] * CG, axis=1), 0.0)

    for m in range(tb // G):
        rows = slice(m * G, (m + 1) * G)
        for p in range(GLA_HEADS // 2):
            ls = slice(p * 128, (p + 1) * 128)
            vs = slice(p * 2 * GLA_DV, (p + 1) * 2 * GLA_DV)
            g_g = gcum[rows, ls]
            g3 = g_g.reshape(CG, C, 128)
            g_last = jnp.broadcast_to(g3[:, C - 1 : C, :], (CG, C, 128)).reshape(G, 128)
            q_t = q[rows, ls] * jnp.exp(g_g)
            k_t = k[rows, ls] * jnp.exp(g_last - g_g)
            d_s = jnp.where(same_head, _dot_tn(v[rows, vs], chunk_diag(k_t)), 0.0)
            st = st_scr[p]
            starts = []
            for c in range(CG):
                starts.append(st)
                decay = jnp.exp(g_g[c * C + C - 1 : c * C + C, :])
                st = st * decay + d_s[:, c * 128 : (c + 1) * 128]
            st_scr[p] = st
            oi_scr[rows, vs] += _dot_nt(chunk_diag(q_t), jnp.concatenate(starts, axis=1))

    o_ref[0] = _gla_finish(oi_scr[...], g, nw_ref[...]).astype(o_ref.dtype)

    @pl.when(t_idx == pl.num_programs(1) - 1)
    def _():
        for p in range(GLA_HEADS // 2):
            st = st_scr[p]
            sout_ref[0, 2 * p] = st[:GLA_DV, :GLA_DK].T
            sout_ref[0, 2 * p + 1] = st[GLA_DV:, GLA_DK:].T


def _gla_select_matrix():
    j = jnp.arange(GLA_CHUNK)[:, None, None]
    hl = jnp.arange(2)[None, :, None]
    rows_j = jnp.broadcast_to(j, (GLA_CHUNK, 2, GLA_DK)).reshape(-1)
    rows_h = jnp.broadcast_to(hl, (GLA_CHUNK, 2, GLA_DK)).reshape(-1)
    cols = jnp.arange(256)
    sel = (rows_j[:, None] == cols[None, :] % GLA_CHUNK) & (rows_h[:, None] == cols[None, :] // 128)
    return sel.astype(BF16)


def _gla_seq(z3, s0, gl, tb):
    b, l, _ = z3.shape
    const = lambda shape: pl.BlockSpec(shape, lambda i, j: (0,) * len(shape))
    return pl.pallas_call(
        _gla_seq_kernel,
        grid=(b, l // tb),
        in_specs=[
            pl.BlockSpec((1, tb, GLA_PROJ_PAD), lambda i, j: (i, j, 0)),
            pl.BlockSpec((1, GLA_HEADS, GLA_DK, GLA_DV), lambda i, j: (i, 0, 0, 0)),
            const((GLA_LORA_PAD, GLA_QK)), const((1, GLA_QK)), const((1, GLA_DV)),
            const((GLA_CHUNK * 128, 256)),
        ],
        out_specs=[
            pl.BlockSpec((1, tb, GLA_WIDTH), lambda i, j: (i, j, 0)),
            pl.BlockSpec((1, GLA_HEADS, GLA_DK, GLA_DV), lambda i, j: (i, 0, 0, 0)),
        ],
        out_shape=[
            jax.ShapeDtypeStruct((b, l, GLA_WIDTH), BF16),
            jax.ShapeDtypeStruct((b, GLA_HEADS, GLA_DK, GLA_DV), F32),
        ],
        scratch_shapes=[
            pltpu.VMEM((GLA_HEADS // 2, 2 * GLA_DV, 2 * GLA_DK), F32),
            pltpu.VMEM((tb, GLA_CHUNK * 128), BF16),
            pltpu.VMEM((tb, GLA_QK), F32), pltpu.VMEM((tb, GLA_WIDTH), F32),
        ],
        compiler_params=_params(("parallel", "arbitrary")),
    )(z3, s0, gl["gkw"], gl["gkb"], gl["norm_w"], _gla_select_matrix())


def _gla_step_prep_kernel(z_ref, gkw_ref, gkb_ref, q_ref, k_ref, v_ref, g_ref, dec_ref):
    q, k, v, g, gk = _gla_features(z_ref[...], gkw_ref[...], gkb_ref[...])
    q_ref[...] = q
    k_ref[...] = k
    v_ref[...] = v
    g_ref[...] = g
    dec_ref[...] = jnp.exp(gk)


def _gla_step_prep(z, gl):
    n = z.shape[0]
    qk = jax.ShapeDtypeStruct((n, GLA_QK), F32)
    wide = jax.ShapeDtypeStruct((n, GLA_WIDTH), F32)
    return pl.pallas_call(
        _gla_step_prep_kernel,
        out_shape=[qk, qk, wide, wide, qk],
        compiler_params=pltpu.CompilerParams(vmem_limit_bytes=VMEM_LIMIT),
    )(z, gl["gkw"], gl["gkb"])


def _gla_step_kernel(s_ref, qcol_ref, kcol_ref, dcol_ref, vrow_ref, snew_ref, orow_ref):
    s_new = s_ref[...] * dcol_ref[...] + kcol_ref[...] * vrow_ref[...]
    snew_ref[...] = s_new
    orow_ref[...] = jnp.sum(s_new * qcol_ref[...], axis=2, keepdims=True)


def _gla_step(s0, q, k, dec, v, bb):
    n = s0.shape[0]
    colv = lambda t: t.reshape(n, GLA_HEADS, GLA_DK, 1)
    col_spec = pl.BlockSpec((bb, GLA_HEADS, GLA_DK, 1), lambda i: (i, 0, 0, 0))
    row_spec = pl.BlockSpec((bb, GLA_HEADS, 1, GLA_DV), lambda i: (i, 0, 0, 0))
    s_spec = pl.BlockSpec((bb, GLA_HEADS, GLA_DK, GLA_DV), lambda i: (i, 0, 0, 0))
    s_new, o_row = pl.pallas_call(
        _gla_step_kernel,
        grid=(n // bb,),
        in_specs=[s_spec, col_spec, col_spec, col_spec, row_spec],
        out_specs=[s_spec, row_spec],
        out_shape=[
            jax.ShapeDtypeStruct(s0.shape, F32),
            jax.ShapeDtypeStruct((n, GLA_HEADS, 1, GLA_DV), F32),
        ],
        compiler_params=_params(("parallel",)),
    )(s0, colv(q), colv(k), colv(dec), v.reshape(n, GLA_HEADS, 1, GLA_DV))
    return s_new, o_row.reshape(n, GLA_WIDTH)


def _step_post_kernel(orw_ref, bon_ref, grw_ref, lnw_ref, lnb_ref, ogl_ref, ggl_ref, nw_ref, o_rw_ref, o_gl_ref):
    o = orw_ref[...]
    for h in range(RW_HEADS):
        hs = slice(h * RW_N, (h + 1) * RW_N)
        o_h = o[:, hs]
        mean = jnp.mean(o_h, axis=-1, keepdims=True)
        cen = o_h - mean
        var = jnp.mean(cen * cen, axis=-1, keepdims=True)
        on = cen * lax.rsqrt(var + RW_GN_EPS)
        res = (on * lnw_ref[:, hs] + lnb_ref[:, hs] + bon_ref[:, hs]) * grw_ref[:, hs]
        o_rw_ref[:, hs] = res.astype(o_rw_ref.dtype)
    o_gl_ref[...] = _gla_finish(ogl_ref[...], ggl_ref[...], nw_ref[...]).astype(o_gl_ref.dtype)


def _step_post(o_rw, bonus, g_rw, rw, o_gl, g_gl, gl):
    n = o_rw.shape[0]
    return pl.pallas_call(
        _step_post_kernel,
        out_shape=[jax.ShapeDtypeStruct((n, RW_WIDTH), BF16), jax.ShapeDtypeStruct((n, GLA_WIDTH), BF16)],
        compiler_params=pltpu.CompilerParams(vmem_limit_bytes=VMEM_LIMIT),
    )(o_rw, bonus, g_rw, rw["ln_w"], rw["ln_b"], o_gl, g_gl, gl["norm_w"])


def _outproj_router_kernel(x_ref, orw_ref, ogl_ref, wo_ref, gain_ref, wrt_ref, br_ref,
                           h_ref, xn_ref, idx_ref, gate_ref, rank_ref, cnt_ref):
    tm = x_ref.shape[0]
    mix = jnp.concatenate([orw_ref[...], ogl_ref[...]], axis=-1)
    h = x_ref[...] + _dot(mix, wo_ref[...])
    h_ref[...] = h
    xn = h * lax.rsqrt(jnp.mean(h * h, axis=-1, keepdims=True) + NORM_EPS) * gain_ref[...]
    _store_row_tiles(xn_ref, xn)
    logits = _dot_nt(wrt_ref[...], xn) + br_ref[...]
    eidx = lax.broadcasted_iota(jnp.int32, logits.shape, 0)
    ti = lax.broadcasted_iota(jnp.int32, (tm, tm), 0)
    tj = lax.broadcasted_iota(jnp.int32, (tm, tm), 1)
    before = (ti < tj).astype(BF16)
    vals, idxs = [], []
    work = logits
    chosen = jnp.zeros(logits.shape, F32)
    for _ in range(TOP_K):
        m = jnp.max(work, axis=0, keepdims=True)
        sel = jnp.min(jnp.where(work == m, eidx, N_EXPERTS), axis=0, keepdims=True)
        hit = eidx == sel
        work = jnp.where(hit, -jnp.inf, work)
        chosen = chosen + hit.astype(F32)
        vals.append(m)
        idxs.append(sel)
    prefix = _dot(chosen.astype(BF16), before)
    exps = [jnp.exp(v - vals[0]) for v in vals]
    denom = exps[0] + exps[1] + exps[2] + exps[3]
    for j in range(TOP_K):
        idx_ref[0, j : j + 1, :] = idxs[j]
        gate_ref[0, j : j + 1, :] = exps[j] / denom
        rank = jnp.sum(jnp.where(eidx == idxs[j], prefix, 0.0), axis=0, keepdims=True)
        rank_ref[0, j : j + 1, :] = rank.astype(jnp.int32)
    cnt = jnp.sum(chosen, axis=1, keepdims=True)
    cnt_ref[0] = jnp.broadcast_to(cnt, (N_EXPERTS, 128)).astype(jnp.int32)


def _outproj_router(x, o_rw, o_gl, w_out, gain, w_router_t, b_router, tm):
    n = x.shape[0]
    nt = n // tm
    const = lambda shape: pl.BlockSpec(shape, lambda i: (0,) * len(shape))
    tok = lambda width: pl.BlockSpec((tm, width), lambda i: (i, 0))
    lane = pl.BlockSpec((1, TOP_K, tm), lambda i: (i, 0, 0))
    return pl.pallas_call(
        _outproj_router_kernel,
        grid=(nt,),
        in_specs=[
            tok(D_MODEL), tok(RW_WIDTH), tok(GLA_WIDTH),
            const((D_MODEL, D_MODEL)), const((1, D_MODEL)), const((N_EXPERTS, D_MODEL)), const((N_EXPERTS, 1)),
        ],
        out_specs=[tok(D_MODEL), pl.BlockSpec((tm * ROW_TILE, 128), lambda i: (i, 0)), lane, lane, lane,
                   pl.BlockSpec((1, N_EXPERTS, 128), lambda i: (i, 0, 0))],
        out_shape=[
            jax.ShapeDtypeStruct((n, D_MODEL), F32),
            jax.ShapeDtypeStruct((n * ROW_TILE, 128), F32),
            jax.ShapeDtypeStruct((nt, TOP_K, tm), jnp.int32),
            jax.ShapeDtypeStruct((nt, TOP_K, tm), F32),
            jax.ShapeDtypeStruct((nt, TOP_K, tm), jnp.int32),
            jax.ShapeDtypeStruct((nt, N_EXPERTS, 128), jnp.int32),
        ],
        compiler_params=_params(("parallel",)),
    )(x, o_rw, o_gl, w_out, gain, w_router_t, b_router)


def _moe_kernel(be_ref, nu_ref, epoch_ref, next_ref, rows_ref, parts_ref, xs_ref, wg_hbm, wu_hbm, wd_hbm, bg_ref, bu_ref, bd_ref,
                y_ref, w_f32, wg_b, wu_b, wd_b, sems):
    b = pl.program_id(0)
    prev = be_ref[jnp.maximum(b - 1, 0)]
    new_expert = (b == 0) | (be_ref[b] != prev)

    def fetch(e, slot):
        return [pltpu.make_async_copy(w.at[e], w_f32.at[slot, i], sems.at[slot])
                for i, w in enumerate((wg_hbm, wu_hbm, wd_hbm))]

    @pl.when(b == 0)
    def _():
        for c in fetch(be_ref[0], 0):
            c.start()

    @pl.when(new_expert)
    def _():
        slot = epoch_ref[b] % 2
        for c in fetch(be_ref[b], slot):
            c.wait()
        wg_b[...] = w_f32[slot, 0].astype(BF16)
        wu_b[...] = w_f32[slot, 1].astype(BF16)
        wd_b[...] = w_f32[slot, 2].astype(BF16)

    first_part, end_part = parts_ref[b] // 4, parts_ref[b] % 4
    for i in range(3):
        @pl.when((next_ref[b] >= 0) & (first_part <= i) & (i < end_part))
        def _():
            fetch(next_ref[b], 1 - epoch_ref[b] % 2)[i].start()

    def ffn(m):
        x = _load_row_tiles(xs_ref, m).astype(BF16)
        half = D_MODEL // 2
        acc = None
        for f in range(2):
            fs = slice(f * half, (f + 1) * half)
            gt = _dot(x, wg_b[:, fs]) + bg_ref[0, :, fs]
            up = _dot(x, wu_b[:, fs]) + bu_ref[0, :, fs]
            gt = jnp.minimum(gt, SWIGLU_LIMIT)
            up = jnp.clip(up, -SWIGLU_LIMIT, SWIGLU_LIMIT)
            hid = (up + 1.0) * gt * _sigmoid(SWIGLU_ALPHA * gt)
            part = _dot(hid.astype(BF16), wd_b[fs, :])
            acc = part if acc is None else acc + part
        _store_row_tiles(y_ref, acc + bd_ref[0])

    quarter = MOE_BLOCK // 4
    quarters = (rows_ref[b] + quarter - 1) // quarter
    for q in range(1, 5):
        @pl.when((b < nu_ref[0]) & (quarters == q))
        def _():
            ffn(q * quarter)


def _moe_ffn(block_expert, n_used, block_rows, xs, w_gate, w_up, w_down, b_gate, b_up, b_down):
    n_blocks = block_expert.shape[0]
    pos = jnp.arange(n_blocks, dtype=jnp.int32)
    change = (pos > 0) & (block_expert != jnp.roll(block_expert, 1))
    epoch = jnp.cumsum(change.astype(jnp.int32))
    later = change[None, :] & (pos[None, :] > pos[:, None])
    first = jnp.min(jnp.where(later, pos[None, :], n_blocks), axis=1)
    next_e = jnp.sum(jnp.where(pos[None, :] == first[:, None], block_expert[None, :], 0), axis=1)
    next_e = jnp.where(first < n_blocks, next_e, -1).astype(jnp.int32)
    run_start = jnp.max(jnp.where((pos[None, :] <= pos[:, None]) & (change | (pos == 0))[None, :], pos[None, :], 0),
                        axis=1)
    q = jnp.minimum(pos - run_start, 3)
    used = pos < n_used[0]
    is_last = jnp.roll(change, -1) | (pos == n_used[0] - 1)
    parts = jnp.where(used, q * 4 + jnp.where(is_last, 3, jnp.minimum(q + 1, 3)), 15).astype(jnp.int32)

    row = lambda b, be, nu, ep, nx, br, pt: (jnp.minimum(b, nu[0] - 1), 0)
    bspec = pl.BlockSpec((1, 1, D_MODEL), lambda b, be, nu, ep, nx, br, pt: (be[b], 0, 0))
    wspec = pl.BlockSpec(memory_space=pl.ANY)
    grid_spec = pltpu.PrefetchScalarGridSpec(
        num_scalar_prefetch=6,
        grid=(n_blocks,),
        in_specs=[pl.BlockSpec((MOE_BLOCK * ROW_TILE, 128), row), wspec, wspec, wspec, bspec, bspec, bspec],
        out_specs=pl.BlockSpec((MOE_BLOCK * ROW_TILE, 128), row),
        scratch_shapes=[pltpu.VMEM((2, 3, D_MODEL, D_MODEL), F32)] + [pltpu.VMEM((D_MODEL, D_MODEL), BF16)] * 3
        + [pltpu.SemaphoreType.DMA((2,))],
    )
    return pl.pallas_call(
        _moe_kernel,
        grid_spec=grid_spec,
        out_shape=jax.ShapeDtypeStruct((n_blocks * MOE_BLOCK * ROW_TILE, 128), F32),
        compiler_params=_params(("arbitrary",)),
    )(block_expert, n_used, epoch, next_e, block_rows, parts, xs, w_gate, w_up, w_down,
      b_gate.reshape(N_EXPERTS, 1, D_MODEL), b_up.reshape(N_EXPERTS, 1, D_MODEL),
      b_down.reshape(N_EXPERTS, 1, D_MODEL))


SEG_ALIGN = 8
GROUP_ROWS = SEG_ALIGN * ROW_TILE


def _local_rows(tm):
    return tm * TOP_K + N_EXPERTS * SEG_ALIGN


def _group_rows(ref, group):
    start = group * GROUP_ROWS
    if not isinstance(group, int):
        start = pl.multiple_of(start, GROUP_ROWS)
    return ref.at[pl.ds(start, GROUP_ROWS)]


def _group_copy(hbm, hbm_group, buf, buf_group, sem, to_hbm):
    h, b = _group_rows(hbm, hbm_group), _group_rows(buf, buf_group)
    return pltpu.make_async_copy(b, h, sem) if to_hbm else pltpu.make_async_copy(h, b, sem)


def _combine_kernel(grp_c, grp_n, lpos_ref, gate_ref, h_ref, gain_ref, y_hbm, o_ref, ybuf, fbuf, sems):
    i = pl.program_id(0)
    nt = pl.num_programs(0)
    tm = h_ref.shape[0]
    n_groups = _local_rows(tm) // SEG_ALIGN

    def issue(grp_ref, slot):
        def body(g, carry):
            _group_copy(y_hbm, grp_ref[0, 0, g], ybuf.at[slot], g, sems.at[slot], False).start()
            return carry
        lax.fori_loop(0, grp_ref[0, 0, n_groups], body, 0)

    @pl.when(i == 0)
    def _():
        issue(grp_c, 0)

    @pl.when(i + 1 < nt)
    def _():
        issue(grp_n, (i + 1) % 2)

    slot = i % 2

    yb = ybuf.at[slot]
    rows_in = pl.ds(0, pl.multiple_of(grp_c[0, 0, n_groups] * GROUP_ROWS, GROUP_ROWS))
    pltpu.make_async_copy(y_hbm.at[rows_in], yb.at[rows_in], sems.at[slot]).wait()

    def token_body(t, carry):
        acc = None
        for j in range(TOP_K):
            row = pl.multiple_of(lpos_ref[0, 0, t * TOP_K + j], ROW_TILE)
            term = gate_ref[0, 0, t * TOP_K + j] * yb[pl.ds(row, ROW_TILE), :]
            acc = term if acc is None else acc + term
        fbuf[pl.ds(pl.multiple_of(t * ROW_TILE, ROW_TILE), ROW_TILE), :] = acc
        return carry

    lax.fori_loop(0, tm, token_body, 0, unroll=8)
    f = h_ref[...] + _load_row_tiles(fbuf, tm)
    o_ref[...] = f * lax.rsqrt(jnp.mean(f * f, axis=-1, keepdims=True) + NORM_EPS) * gain_ref[...]


def _combine(h, y_rows, grp3, lpos3, gate3, gain, tm):
    n = h.shape[0]
    nt = n // tm
    n_local = _local_rows(tm)
    gw = grp3.shape[-1]
    smem = lambda shape, imap: pl.BlockSpec(shape, imap, memory_space=pltpu.SMEM)
    cur = lambda i: (i, 0, 0)
    nxt = lambda i: (jnp.minimum(i + 1, nt - 1), 0, 0)
    return pl.pallas_call(
        _combine_kernel,
        grid=(nt,),
        in_specs=[
            smem((1, 1, gw), cur), smem((1, 1, gw), nxt),
            smem((1, 1, TOP_K * tm), cur), smem((1, 1, TOP_K * tm), cur),
            pl.BlockSpec((tm, D_MODEL), lambda i: (i, 0)),
            pl.BlockSpec((1, D_MODEL), lambda i: (0, 0)),
            pl.BlockSpec(memory_space=pl.ANY),
        ],
        out_specs=pl.BlockSpec((tm, D_MODEL), lambda i: (i, 0)),
        out_shape=jax.ShapeDtypeStruct((n, D_MODEL), F32),
        scratch_shapes=[pltpu.VMEM((2, n_local * ROW_TILE, 128), F32), pltpu.VMEM((tm * ROW_TILE, 128), F32),
                        pltpu.SemaphoreType.DMA((2,))],
        compiler_params=pltpu.CompilerParams(dimension_semantics=("arbitrary",), vmem_limit_bytes=VMEM_LIMIT,
                                             disable_bounds_checks=True),
    )(grp3, grp3, lpos3, gate3, h, gain, y_rows)


def _dispatch_kernel(*refs, fill):
    if fill:
        grp_ref, grp_prev, lpos_ref, ends_ref, x_ref, xs_hbm, sorted_buf, zero_scr, sems, zsem = refs
    else:
        grp_ref, grp_prev, lpos_ref, x_ref, _, xs_hbm, sorted_buf, sems = refs
    i = pl.program_id(0)
    tm = x_ref.shape[0] // ROW_TILE
    n_groups = _local_rows(tm) // SEG_ALIGN
    blk = MOE_BLOCK * ROW_TILE
    sorted_scr = sorted_buf.at[i % 2]
    sem = sems.at[i % 2]

    if fill:
        def fill_copy(e):
            start = pl.multiple_of((ends_ref[0, e] - MOE_BLOCK) * ROW_TILE, blk)
            return pltpu.make_async_copy(zero_scr, xs_hbm.at[pl.ds(start, blk)], zsem)

        @pl.when(i == 0)
        def _():
            zero_scr[...] = jnp.zeros(zero_scr.shape, zero_scr.dtype)
            for e in range(N_EXPERTS):
                @pl.when(ends_ref[1, e] > 0)
                def _():
                    fill_copy(e).start()

    sorted_scr[...] = jnp.zeros(sorted_scr.shape, sorted_scr.dtype)

    def move(t, carry):
        row = x_ref[pl.ds(pl.multiple_of(t * ROW_TILE, ROW_TILE), ROW_TILE), :]
        for j in range(TOP_K):
            dst = pl.multiple_of(lpos_ref[0, 0, t * TOP_K + j], ROW_TILE)
            sorted_scr[pl.ds(dst, ROW_TILE), :] = row
        return carry

    lax.fori_loop(0, tm, move, 0, unroll=8)

    def wait_groups(n, buf, s):
        rows_out = pl.ds(0, pl.multiple_of(n * GROUP_ROWS, GROUP_ROWS))
        pltpu.make_async_copy(buf.at[rows_out], xs_hbm.at[rows_out], s).wait()

    if fill:
        @pl.when(i == 0)
        def _():
            for e in range(N_EXPERTS):
                @pl.when(ends_ref[1, e] > 0)
                def _():
                    fill_copy(e).wait()

    @pl.when(i > 0)
    def _():
        wait_groups(grp_prev[0, 0, n_groups], sorted_buf.at[(i + 1) % 2], sems.at[(i + 1) % 2])

    n_used = grp_ref[0, 0, n_groups]

    def send(g, carry):
        _group_copy(xs_hbm, grp_ref[0, 0, g], sorted_scr, g, sem, True).start()
        return carry

    lax.fori_loop(0, n_used, send, 0)

    @pl.when(i == pl.num_programs(0) - 1)
    def _():
        wait_groups(n_used, sorted_scr, sem)


def _dispatch(xn, grp3, lpos3, tm, n_slots, ends=None, xs=None):
    n = xn.shape[0] // ROW_TILE
    fill = xs is None
    smem = lambda shape, imap: pl.BlockSpec(shape, imap, memory_space=pltpu.SMEM)
    in_specs = [smem((1, 1, grp3.shape[-1]), lambda i: (i, 0, 0)),
                smem((1, 1, grp3.shape[-1]), lambda i: (jnp.maximum(i - 1, 0), 0, 0)),
                smem((1, 1, TOP_K * tm), lambda i: (i, 0, 0))]
    args = [grp3, grp3, lpos3]
    scratch = [pltpu.VMEM((2, _local_rows(tm) * ROW_TILE, 128), F32)]
    if fill:
        in_specs.append(smem((2, N_EXPERTS), lambda i: (0, 0)))
        args.append(ends)
        scratch.append(pltpu.VMEM((MOE_BLOCK * ROW_TILE, 128), F32))
    in_specs.append(pl.BlockSpec((tm * ROW_TILE, 128), lambda i: (i, 0)))
    args.append(xn)
    aliases = {}
    if not fill:
        in_specs.append(pl.BlockSpec(memory_space=pl.ANY))
        args.append(xs)
        aliases = {len(args) - 1: 0}
    scratch.append(pltpu.SemaphoreType.DMA((2,)))
    if fill:
        scratch.append(pltpu.SemaphoreType.DMA(()))
    return pl.pallas_call(
        functools.partial(_dispatch_kernel, fill=fill),
        grid=(n // tm,),
        in_specs=in_specs,
        out_specs=pl.BlockSpec(memory_space=pl.ANY),
        out_shape=jax.ShapeDtypeStruct((n_slots * ROW_TILE, 128), F32),
        scratch_shapes=scratch,
        input_output_aliases=aliases,
        compiler_params=pltpu.CompilerParams(dimension_semantics=("arbitrary",), vmem_limit_bytes=VMEM_LIMIT,
                                             disable_bounds_checks=True, has_side_effects=True),
    )(*args)


def _pad_rows(w, rows, offset):
    out = jnp.zeros((rows, w.shape[1]), w.dtype)
    return out.at[offset : offset + w.shape[0]].set(w)


def _routing_tables(counts, n_pairs):
    n_tiles = counts.shape[0]
    n_blocks = (n_pairs + n_tiles * N_EXPERTS * (SEG_ALIGN - 1) + N_EXPERTS * (MOE_BLOCK - 1)
                + MOE_BLOCK - 1) // MOE_BLOCK
    runs = (counts + SEG_ALIGN - 1) // SEG_ALIGN * SEG_ALIGN
    local_start = jnp.cumsum(runs, axis=1) - runs
    total = jnp.sum(runs, axis=0)
    padded = (total + MOE_BLOCK - 1) // MOE_BLOCK * MOE_BLOCK
    pends = jnp.cumsum(padded)
    pstarts = pends - padded
    global_start = pstarts[None, :] + jnp.cumsum(runs, axis=0) - runs
    blocks = jnp.arange(n_blocks, dtype=jnp.int32) * MOE_BLOCK
    n_used = (pends[-1] // MOE_BLOCK).astype(jnp.int32)
    owner = jnp.sum((pends[None, :] <= blocks[:, None]).astype(jnp.int32), axis=1)
    block_expert = jnp.minimum(owner, N_EXPERTS - 1)
    last = jnp.sum(jnp.where(jnp.arange(n_blocks) == n_used - 1, block_expert, 0))
    block_expert = jnp.where(jnp.arange(n_blocks) < n_used, block_expert, last)
    ends = jnp.stack([pends, padded]).astype(jnp.int32)
    row_end = jnp.sum(jnp.where(block_expert[:, None] == jnp.arange(N_EXPERTS), (pstarts + total)[None, :], 0), axis=1)
    block_rows = jnp.clip(row_end - blocks, 0, MOE_BLOCK).astype(jnp.int32)
    return (runs, local_start, global_start, ends, block_expert.astype(jnp.int32), n_used.reshape(1), block_rows,
            n_blocks)


def _tile_tables(idx3, rank3, runs, local_start, global_start, tm):
    experts = jnp.arange(N_EXPERTS, dtype=jnp.int32)
    hit = idx3[..., None] == experts
    lpos = rank3 + jnp.sum(jnp.where(hit, local_start[:, None, None, :], 0), axis=-1)
    n_groups = _local_rows(tm) // SEG_ALIGN
    g = jnp.arange(n_groups, dtype=jnp.int32)
    run_end = (local_start + runs) // SEG_ALIGN
    owner = jnp.minimum(jnp.sum((run_end[:, None, :] <= g[None, :, None]).astype(jnp.int32), axis=-1), N_EXPERTS - 1)
    sel = owner[..., None] == experts
    offset = jnp.sum(jnp.where(sel, (global_start - local_start)[:, None, :], 0), axis=-1) // SEG_ALIGN
    used = jnp.sum(runs, axis=1) // SEG_ALIGN
    table = jnp.where(g[None, :] < used[:, None], g[None, :] + offset, 0)
    tail = jnp.broadcast_to(used[:, None], (used.shape[0], 8))
    lpos_rows = _token_major(lpos.astype(jnp.int32) * ROW_TILE)
    return lpos_rows, jnp.concatenate([table, tail], axis=1).astype(jnp.int32)[:, None, :]


def _token_major(t3):
    return jnp.swapaxes(t3, 1, 2).reshape(t3.shape[0], 1, -1)


def kernel(x_prompt, x_sample, state_rwkv_shift, state_rwkv_wkv, state_gla, norm_mix, w_in, rw_mu, rw_w0, rw_w2, rw_a0, rw_a2, rw_g2, rw_k_k, rw_k_a, rw_r_k, rw_ln_w, rw_ln_b, gla_gk_w2, gla_gk_b, gla_norm_w, w_out, norm_ffn, w_router, b_router, w_gate, b_gate, w_up, b_up, w_down, b_down, norm_final):
    depth = norm_mix.shape[0]
    assert depth == 1
    bp, lp, d = x_prompt.shape
    bs, ls, _ = x_sample.shape
    assert ls == 1 and lp % SEQ_BLOCK == 0
    l = 0
    row = lambda t: t.reshape(1, -1)

    w_in_b = w_in[l].astype(BF16)
    w_in_r = w_in_b[:, :RW_PROJ]
    w_in_g = jnp.pad(w_in_b[:, RW_PROJ:], ((0, 0), (0, GLA_PROJ_PAD - GLA_PROJ)))
    rw = dict(
        mu=row(rw_mu[l]), w0=row(rw_w0[l]), a0=row(rw_a0[l]),
        w2p=_pad_rows(rw_w2[l].astype(BF16), 128, 0), a2p=_pad_rows(rw_a2[l].astype(BF16), 128, 64),
        g2=rw_g2[l].astype(BF16), k_k=row(rw_k_k[l]), k_a=row(rw_k_a[l]), r_k=row(rw_r_k[l]),
        ln_w=row(rw_ln_w[l]), ln_b=row(rw_ln_b[l]))
    gl = dict(gkw=_pad_rows(gla_gk_w2[l].astype(BF16), GLA_LORA_PAD, 0), gkb=row(gla_gk_b[l]),
              norm_w=row(gla_norm_w[l]))
    gain_mix = row(norm_mix[l])

    n_p = bp * lp
    xp = x_prompt.reshape(n_p, d)
    zr_p, zg_p = _inproj(xp, gain_mix, w_in_r, w_in_g, TOK_BLOCK)
    zr_p3 = zr_p.reshape(bp, lp, RW_PROJ)
    o_rw_p, wkv_p = _rwkv_seq(zr_p3, jnp.zeros((bp, 1, RW_PROJ), F32),
                              jnp.zeros((bp, RW_HEADS, RW_N, RW_N), F32), rw, SEQ_BLOCK)
    o_gl_p, gla_p = _gla_seq(zg_p.reshape(bp, lp, GLA_PROJ_PAD),
                             jnp.zeros((bp, GLA_HEADS, GLA_DK, GLA_DV), F32), gl, SEQ_BLOCK)
    shift_p = zr_p3[:, -1, :]

    xs_ = x_sample.reshape(bs, d)
    zr_s, zg_s = _inproj(xs_, gain_mix, w_in_r, w_in_g, bs)
    r, k, v, al, be, dec, g_rw, bonus = _rwkv_step_prep(zr_s, state_rwkv_shift[l], rw)
    wkv_s, o_rw_s = _rwkv_step(state_rwkv_wkv[l], r, k, al, be, dec, v)
    q, kg, vg, g_gl, dec_g = _gla_step_prep(zg_s, gl)
    gla_s, o_gl_s = _gla_step(state_gla[l], q, kg, dec_g, vg, 16)
    o_rw_s2, o_gl_s2 = _step_post(o_rw_s, bonus, g_rw, rw, o_gl_s, g_gl, gl)
    shift_s = zr_s

    w_out_b = w_out[l].astype(BF16)
    router = (w_out_b, row(norm_ffn[l]), w_router[l].T, b_router[l].reshape(N_EXPERTS, 1))
    h_p, xn_p, idx_p, gate_p, rank_p, cnt_p = _outproj_router(
        xp, o_rw_p.reshape(n_p, RW_WIDTH), o_gl_p.reshape(n_p, GLA_WIDTH), *router, TOK_BLOCK)
    h_s, xn_s, idx_s, gate_s, rank_s, cnt_s = _outproj_router(xs_, o_rw_s2, o_gl_s2, *router, bs)
    nt_p = n_p // TOK_BLOCK
    counts = jnp.concatenate([cnt_p[:, :, 0], cnt_s[:, :, 0]], axis=0)
    runs, lstart, gstart, ends, block_expert, n_used, block_rows, n_blocks = _routing_tables(
        counts, (n_p + bs) * TOP_K)
    n_slots = n_blocks * MOE_BLOCK
    lpos_p, grp_p = _tile_tables(idx_p, rank_p, runs[:nt_p], lstart[:nt_p], gstart[:nt_p], TOK_BLOCK)
    lpos_s, grp_s = _tile_tables(idx_s, rank_s, runs[nt_p:], lstart[nt_p:], gstart[nt_p:], bs)
    xs_rows = _dispatch(xn_p, grp_p, lpos_p, TOK_BLOCK, n_slots, ends=ends)
    xs_rows = _dispatch(xn_s, grp_s, lpos_s, bs, n_slots, xs=xs_rows)
    y_rows = _moe_ffn(block_expert, n_used, block_rows, xs_rows, w_gate[l], w_up[l], w_down[l], b_gate[l], b_up[l], b_down[l])
    gain_f = row(norm_final)
    y_p = _combine(h_p, y_rows, grp_p, lpos_p, _token_major(gate_p), gain_f, TOK_BLOCK)
    y_s = _combine(h_s, y_rows, grp_s, lpos_s, _token_major(gate_s), gain_f, bs)

    y_prompt = y_p.reshape(bp, lp, d)
    y_sample = y_s.reshape(bs, ls, d)
    return (y_prompt, y_sample, shift_p[None], wkv_p[None], gla_p[None], shift_s[None], wkv_s[None], gla_s[None])
```

```python
import functools

import jax
import jax.numpy as jnp
from jax import lax
from jax.experimental import pallas as pl
from jax.experimental.pallas import tpu as pltpu

F32 = jnp.float32
BF16 = jnp.bfloat16
HIGHEST = lax.Precision.HIGHEST

D_MODEL = 1024
RW_WIDTH = 512
RW_HEADS = 8
RW_N = 64
RW_PROJ = 1792
RW_GN_EPS = 64e-5
GLA_HEADS = 4
GLA_DK = 64
GLA_DV = 128
GLA_WIDTH = 512
GLA_QK = GLA_HEADS * GLA_DK
GLA_PROJ = 1552
GLA_PROJ_PAD = 1664
GLA_LORA_PAD = 128
GLA_GATE_NORMALIZER = 16.0
N_EXPERTS = 32
TOP_K = 4
SWIGLU_LIMIT = 7.0
SWIGLU_ALPHA = 1.702
NORM_EPS = 1e-5
LOG2_E = 1.4426950408889634

RW_CHUNK = 64
GLA_CHUNK = 16
SEQ_BLOCK = 512
TOK_BLOCK = 512
MOE_BLOCK = 512
VMEM_LIMIT = 56 * 1024 * 1024


def _dot(a, b, precision=None):
    return jnp.dot(a, b, preferred_element_type=F32, precision=precision)


def _dot_nt(a, b, precision=None):
    return lax.dot_general(a, b, (((1,), (1,)), ((), ())), preferred_element_type=F32, precision=precision)


def _dot_tn(a, b, precision=None):
    return lax.dot_general(a, b, (((0,), (0,)), ((), ())), preferred_element_type=F32, precision=precision)


def _sigmoid(x):
    return 1.0 / (1.0 + jnp.exp(-x))


def _softplus(x):
    return jnp.maximum(x, 0.0) + jnp.log(1.0 + jnp.exp(-jnp.abs(x)))


def _params(sem):
    return pltpu.CompilerParams(dimension_semantics=sem, vmem_limit_bytes=VMEM_LIMIT)


ROW_TILE = D_MODEL // 128


def _store_row_tiles(ref, x):
    m = x.shape[0]
    for c in range(ROW_TILE):
        ref[pl.ds(c, m, stride=ROW_TILE), :] = x[:, c * 128 : (c + 1) * 128]


def _load_row_tiles(ref, m):
    return jnp.concatenate([ref[pl.ds(c, m, stride=ROW_TILE), :] for c in range(ROW_TILE)], axis=-1)


def _inproj_kernel(x_ref, gain_ref, wr_ref, wg_ref, zr_ref, zg_ref):
    x = x_ref[...]
    xn = x * lax.rsqrt(jnp.mean(x * x, axis=-1, keepdims=True) + NORM_EPS) * gain_ref[...]
    xb = xn.astype(BF16)
    zr_ref[...] = _dot(xb, wr_ref[...])
    zg_ref[...] = _dot(xb, wg_ref[...])


def _inproj(x, gain, w_r, w_g, tm):
    n = x.shape[0]
    return pl.pallas_call(
        _inproj_kernel,
        grid=(n // tm,),
        in_specs=[
            pl.BlockSpec((tm, D_MODEL), lambda i: (i, 0)),
            pl.BlockSpec((1, D_MODEL), lambda i: (0, 0)),
            pl.BlockSpec((D_MODEL, RW_PROJ), lambda i: (0, 0)),
            pl.BlockSpec((D_MODEL, GLA_PROJ_PAD), lambda i: (0, 0)),
        ],
        out_specs=[
            pl.BlockSpec((tm, RW_PROJ), lambda i: (i, 0)),
            pl.BlockSpec((tm, GLA_PROJ_PAD), lambda i: (i, 0)),
        ],
        out_shape=[
            jax.ShapeDtypeStruct((n, RW_PROJ), F32),
            jax.ShapeDtypeStruct((n, GLA_PROJ_PAD), F32),
        ],
        compiler_params=_params(("parallel",)),
    )(x, gain, w_r, w_g)


def _rwkv_features(zs, w0, w2p, a0, a2p, g2, k_k, k_a):
    W = RW_WIDTH
    r = zs[:, 0:W]
    k_raw = zs[:, W : 2 * W]
    v = zs[:, 2 * W : 3 * W]
    zwa = zs[:, 3 * W : 3 * W + 128]
    zg = zs[:, 3 * W + 128 :]
    w = -_softplus(-(w0 + _dot(jnp.tanh(zwa).astype(BF16), w2p))) - 0.5
    log_decay = -jnp.exp(w)
    a = _sigmoid(a0 + _dot(zwa.astype(BF16), a2p))
    g = _dot(_sigmoid(zg).astype(BF16), g2)
    kk_raw = k_raw * k_k
    k = k_raw * (1.0 + (a - 1.0) * k_a)
    return r, k, v, kk_raw, a, log_decay, g


def _level_mask(ri, ci, lvl):
    same = (ri >> (lvl + 1)) == (ci >> (lvl + 1))
    return same & (((ri >> lvl) & 1) == 1) & (((ci >> lvl) & 1) == 0)


def _rwkv_seq_kernel(z_ref, shift0_ref, s0_ref, mu_ref, w0_ref, w2_ref, a0_ref, a2_ref, g2_ref, kk_ref, ka_ref,
                     rk_ref, lnw_ref, lnb_ref, o_ref, sout_ref,
                     m_scr, prev_scr, r_scr, k_scr, v_scr, kkr_scr, a_scr, lw_scr, on_scr, bon_scr):
    C = RW_CHUNK
    N = RW_N
    t_idx = pl.program_id(1)
    tb = z_ref.shape[1]
    zero_nn = jnp.zeros((N, N), F32)

    @pl.when(t_idx == 0)
    def _():
        prev_scr[...] = shift0_ref[0]
        for p in range(RW_HEADS // 2):
            top = jnp.concatenate([s0_ref[0, 2 * p].T, zero_nn], axis=1)
            bot = jnp.concatenate([zero_nn, s0_ref[0, 2 * p + 1].T], axis=1)
            m_scr[p] = jnp.concatenate([top, bot], axis=0)

    z = z_ref[0]
    row = lax.broadcasted_iota(jnp.int32, z.shape, 0)
    z_prev = jnp.where(row == 0, prev_scr[...], pltpu.roll(z, 1, axis=0))
    prev_scr[...] = z[tb - 1 : tb, :]
    zs = z + mu_ref[...] * (z_prev - z)
    r, k, v, kk_raw, a, log_decay, g = _rwkv_features(
        zs, w0_ref[...], w2_ref[...], a0_ref[...], a2_ref[...], g2_ref[...], kk_ref[...], ka_ref[...])
    P2 = 2 * N
    left1 = lax.broadcasted_iota(jnp.int32, (1, P2), 1) < N

    def head_sum(x):
        s0 = jnp.sum(jnp.where(left1, x, 0.0), axis=-1, keepdims=True)
        s1 = jnp.sum(jnp.where(left1, 0.0, x), axis=-1, keepdims=True)
        return jnp.where(left1, s0, s1)

    def head_sum_wide(x):
        return jnp.concatenate([head_sum(x[:, p * P2 : (p + 1) * P2]) for p in range(RW_HEADS // 2)], axis=1)

    alpha = kk_raw * lax.rsqrt(jnp.maximum(head_sum_wide(kk_raw * kk_raw), 1e-24))
    r_scr[...] = r
    k_scr[...] = k
    v_scr[...] = v
    kkr_scr[...] = alpha
    a_scr[...] = alpha * a
    lw_scr[...] = log_decay
    bon_scr[...] = head_sum_wide(r * k * rk_ref[...]) * v

    ri = lax.broadcasted_iota(jnp.int32, (C, P2), 0)
    ci = lax.broadcasted_iota(jnp.int32, (C, P2), 1) % N
    left = lax.broadcasted_iota(jnp.int32, (C, P2), 1) < N
    tril = ri >= ci
    stril = ri > ci
    eye_f = (ri == ci).astype(F32)
    rb = lax.broadcasted_iota(jnp.int32, (P2, P2), 0)
    cb = lax.broadcasted_iota(jnp.int32, (P2, P2), 1)
    same_head = (rb < N) == (cb < N)
    eye_b = rb == cb
    rc = lax.broadcasted_iota(jnp.int32, (C, C), 0)
    cc = lax.broadcasted_iota(jnp.int32, (C, C), 1)
    tril_f = (rc >= cc).astype(F32)

    def bdiag(x):
        return jnp.concatenate([jnp.where(left, x, 0.0), jnp.where(left, 0.0, x)], axis=0)

    n_sub = tb // C
    pairs = range(RW_HEADS // 2)

    def chunk_body(it, carry):
        units = [(s, p) for s in range(n_sub) for p in pairs]
        sls = [pl.ds(pl.multiple_of((it * n_sub + s) * C, C), C) for s in range(n_sub)]
        prep = []
        for s in range(n_sub):
            lw = lw_scr[sls[s], :]
            cum = _dot(tril_f, lw, precision=HIGHEST)
            cum_last = cum[C - 1 : C, :]
            prep.append(dict(
                e_incl=jnp.exp(cum), e_excl=jnp.exp(cum - lw), e_neg=jnp.exp(-cum),
                e_tail=jnp.exp(cum_last - cum), p_last=jnp.exp(cum_last),
                r=r_scr[sls[s], :], k=k_scr[sls[s], :], v=v_scr[sls[s], :], kk=kkr_scr[sls[s], :],
                a=a_scr[sls[s], :]))
        lanes = [slice(p * P2, (p + 1) * P2) for p in pairs]
        get = lambda name: [prep[s][name][:, lanes[p]] for s, p in units]
        r2, k2, v2, al, be = get("r"), get("k"), get("v"), get("kk"), get("a")
        e_incl, e_excl, e_neg, e_tail, p_last = get("e_incl"), get("e_excl"), get("e_neg"), get("e_tail"), get("p_last")
        un = range(len(units))
        al_t = [al[u] * e_excl[u] for u in un]
        r_t = [r2[u] * e_incl[u] for u in un]
        be_n = [be[u] * e_neg[u] for u in un]
        k_n = [k2[u] * e_neg[u] for u in un]
        k_et = [(k2[u] * e_tail[u]).T for u in un]
        be_et = [(be[u] * e_tail[u]).T for u in un]
        v_bd = [bdiag(v2[u]) for u in un]
        lhs = [jnp.concatenate([al_t[u], r_t[u]], axis=0) for u in un]
        s_b = [_dot_nt(lhs[u], bdiag(be_n[u])) for u in un]
        s_k = [_dot_nt(lhs[u], bdiag(k_n[u])) for u in un]
        l_ab = [jnp.where(stril, s_b[u][:C], 0.0) for u in un]
        a_rb = [jnp.where(tril, s_b[u][C:], 0.0) for u in un]
        l_ak = [jnp.where(stril, s_k[u][:C], 0.0) for u in un]
        a_rk = [jnp.where(tril, s_k[u][C:], 0.0) for u in un]
        lakv = [_dot(l_ak[u], v_bd[u]) for u in un]
        arkv = [_dot(a_rk[u], v_bd[u]) for u in un]
        kev = [_dot(k_et[u], v2[u]) for u in un]
        t_inv = [eye_f - jnp.where(_level_mask(ri, ci, 0), l_ab[u], 0.0) for u in un]
        lvl = 1
        while (1 << lvl) < C:
            lm = _level_mask(ri, ci, lvl)
            tn = [_dot(t_inv[u], bdiag(jnp.where(lm, l_ab[u], 0.0))) for u in un]
            t_inv = [t_inv[u] - _dot(tn[u], bdiag(t_inv[u])) for u in un]
            lvl += 1
        a_til = [_dot(t_inv[u], bdiag(al_t[u])) for u in un]
        b_til = [_dot(t_inv[u], bdiag(lakv[u])) for u in un]
        r_hat = [r_t[u] - _dot(a_rb[u], bdiag(a_til[u])) for u in un]
        o_hat = [arkv[u] - _dot(a_rb[u], bdiag(b_til[u])) for u in un]
        g_bd = [jnp.where(same_head, jnp.where(eye_b, p_last[u], 0.0) - _dot(be_et[u], a_til[u]), 0.0) for u in un]
        h_bd = [jnp.where(same_head, kev[u] - _dot(be_et[u], b_til[u]), 0.0) for u in un]
        lhs_m = [jnp.concatenate([r_hat[u], g_bd[u]], axis=0) for u in un]
        for u, (s, p) in enumerate(units):
            res = _dot(lhs_m[u], m_scr[p])
            m_scr[p] = res[C:] + h_bd[u]
            o_p = res[:C] + o_hat[u]
            cen = o_p - head_sum(o_p) * (1.0 / N)
            var = head_sum(cen * cen) * (1.0 / N)
            on_scr[sls[s], lanes[p]] = cen * lax.rsqrt(var + RW_GN_EPS)
        return carry

    lax.fori_loop(0, tb // (C * n_sub), chunk_body, 0)
    out = (on_scr[...] * lnw_ref[...] + lnb_ref[...] + bon_scr[...]) * g
    o_ref[0] = out.astype(o_ref.dtype)

    @pl.when(t_idx == pl.num_programs(1) - 1)
    def _():
        for p in range(RW_HEADS // 2):
            m = m_scr[p]
            sout_ref[0, 2 * p] = m[:N, :N].T
            sout_ref[0, 2 * p + 1] = m[N:, N:].T


def _rwkv_seq(z3, shift0, s0, rw, tb):
    b, l, _ = z3.shape
    const = lambda shape: pl.BlockSpec(shape, lambda i, j: (0,) * len(shape))
    wide = lambda: pltpu.VMEM((tb, RW_WIDTH), F32)
    return pl.pallas_call(
        _rwkv_seq_kernel,
        grid=(b, l // tb),
        in_specs=[
            pl.BlockSpec((1, tb, RW_PROJ), lambda i, j: (i, j, 0)),
            pl.BlockSpec((1, 1, RW_PROJ), lambda i, j: (i, 0, 0)),
            pl.BlockSpec((1, RW_HEADS, RW_N, RW_N), lambda i, j: (i, 0, 0, 0)),
            const((1, RW_PROJ)),
            const((1, RW_WIDTH)), const((128, RW_WIDTH)),
            const((1, RW_WIDTH)), const((128, RW_WIDTH)),
            const((128, RW_WIDTH)),
            const((1, RW_WIDTH)), const((1, RW_WIDTH)), const((1, RW_WIDTH)),
            const((1, RW_WIDTH)), const((1, RW_WIDTH)),
        ],
        out_specs=[
            pl.BlockSpec((1, tb, RW_WIDTH), lambda i, j: (i, j, 0)),
            pl.BlockSpec((1, RW_HEADS, RW_N, RW_N), lambda i, j: (i, 0, 0, 0)),
        ],
        out_shape=[
            jax.ShapeDtypeStruct((b, l, RW_WIDTH), BF16),
            jax.ShapeDtypeStruct((b, RW_HEADS, RW_N, RW_N), F32),
        ],
        scratch_shapes=[
            pltpu.VMEM((RW_HEADS // 2, 2 * RW_N, 2 * RW_N), F32),
            pltpu.VMEM((1, RW_PROJ), F32),
            wide(), wide(), wide(), wide(), wide(), wide(), wide(), wide(),
        ],
        compiler_params=_params(("parallel", "arbitrary")),
    )(z3, shift0, s0, rw["mu"], rw["w0"], rw["w2p"], rw["a0"], rw["a2p"], rw["g2"], rw["k_k"], rw["k_a"],
      rw["r_k"], rw["ln_w"], rw["ln_b"])


def _rwkv_step_prep_kernel(z_ref, shift0_ref, mu_ref, w0_ref, w2_ref, a0_ref, a2_ref, g2_ref, kk_ref, ka_ref,
                           rk_ref, r_ref, k_ref, v_ref, al_ref, be_ref, dec_ref, g_ref, bon_ref):
    z = z_ref[...]
    zs = z + mu_ref[...] * (shift0_ref[...] - z)
    r, k, v, kk_raw, a, log_decay, g = _rwkv_features(
        zs, w0_ref[...], w2_ref[...], a0_ref[...], a2_ref[...], g2_ref[...], kk_ref[...], ka_ref[...])
    rk_all = rk_ref[...]
    for h in range(RW_HEADS):
        hs = slice(h * RW_N, (h + 1) * RW_N)
        kk_h = kk_raw[:, hs]
        nrm = jnp.sqrt(jnp.sum(kk_h * kk_h, axis=-1, keepdims=True))
        al = kk_h / jnp.maximum(nrm, 1e-12)
        al_ref[:, hs] = al
        be_ref[:, hs] = al * a[:, hs]
        bon_ref[:, hs] = jnp.sum(r[:, hs] * k[:, hs] * rk_all[:, hs], axis=-1, keepdims=True) * v[:, hs]
    r_ref[...] = r
    k_ref[...] = k
    v_ref[...] = v
    dec_ref[...] = jnp.exp(log_decay)
    g_ref[...] = g


def _rwkv_step_prep(z, shift0, rw):
    n = z.shape[0]
    out = jax.ShapeDtypeStruct((n, RW_WIDTH), F32)
    return pl.pallas_call(
        _rwkv_step_prep_kernel,
        out_shape=[out] * 8,
        compiler_params=pltpu.CompilerParams(vmem_limit_bytes=VMEM_LIMIT),
    )(z, shift0, rw["mu"], rw["w0"], rw["w2p"], rw["a0"], rw["a2p"], rw["g2"], rw["k_k"], rw["k_a"], rw["r_k"])


def _rwkv_step_kernel(s_ref, r_ref, k_ref, al_ref, be_ref, dec_ref, v_ref, snew_ref, o_ref):
    r, k, al, be, dec = r_ref[...], k_ref[...], al_ref[...], be_ref[...], dec_ref[...]

    def body(g, carry):
        rows = pl.ds(pl.multiple_of(g * 8, 8), 8)
        v8 = v_ref[rows, :]
        outs = []
        for j in range(8):
            s = s_ref[0, g * 8 + j]
            sa = -jnp.sum(s * al, axis=0, keepdims=True)
            s_new = s * dec + sa * be + v8[j : j + 1, :] * k
            snew_ref[0, g * 8 + j] = s_new
            outs.append(jnp.sum(s_new * r, axis=0, keepdims=True))
        o_ref[rows, :] = jnp.concatenate(outs, axis=0)
        return carry

    lax.fori_loop(0, RW_N // 8, body, 0)


def _rwkv_step(s0, r, k, al, be, dec, v):
    n = s0.shape[0]
    s_t = jnp.transpose(s0, (1, 2, 3, 0))
    s_spec = pl.BlockSpec((1, RW_N, RW_N, n), lambda h: (h, 0, 0, 0))
    op_spec = pl.BlockSpec((RW_N, n), lambda h: (h, 0))
    s_new_t, o_t = pl.pallas_call(
        _rwkv_step_kernel,
        grid=(RW_HEADS,),
        in_specs=[s_spec] + [op_spec] * 6,
        out_specs=[s_spec, op_spec],
        out_shape=[
            jax.ShapeDtypeStruct(s_t.shape, F32),
            jax.ShapeDtypeStruct((RW_WIDTH, n), F32),
        ],
        compiler_params=_params(("parallel",)),
    )(s_t, r.T, k.T, al.T, be.T, dec.T, v.T)
    return jnp.transpose(s_new_t, (3, 0, 1, 2)), o_t.T


def _gla_features(z, gkw, gkb):
    q = z[:, 0:GLA_QK] * (GLA_DK ** -0.5)
    k = z[:, GLA_QK : 2 * GLA_QK]
    v = z[:, 2 * GLA_QK : 2 * GLA_QK + GLA_WIDTH]
    g = z[:, 2 * GLA_QK + GLA_WIDTH : 2 * GLA_QK + 2 * GLA_WIDTH]
    zgk = z[:, 2 * GLA_QK + 2 * GLA_WIDTH :]
    gk = -_softplus(-(_dot(zgk.astype(BF16), gkw) + gkb)) / GLA_GATE_NORMALIZER
    return q, k, v, g, gk


def _gla_finish(o, g, norm_w):
    outs = []
    for h in range(GLA_HEADS):
        hs = slice(h * GLA_DV, (h + 1) * GLA_DV)
        o_h = o[:, hs]
        o_h = o_h * lax.rsqrt(jnp.mean(o_h * o_h, axis=-1, keepdims=True) + NORM_EPS) * norm_w
        g_h = g[:, hs]
        outs.append(o_h * (g_h * _sigmoid(g_h)))
    return jnp.concatenate(outs, axis=-1)


def _gla_seq_kernel(z_ref, s0_ref, gkw_ref, gkb_ref, nw_ref, wsel_ref, o_ref, sout_ref,
                    st_scr, x_scr, gc_scr, oi_scr):
    C = GLA_CHUNK
    G = 128
    t_idx = pl.program_id(1)
    tb = z_ref.shape[1]
    nc = tb // C
    zero_vk = jnp.zeros((GLA_DV, GLA_DK), F32)

    @pl.when(t_idx == 0)
    def _():
        for p in range(GLA_HEADS // 2):
            top = jnp.concatenate([s0_ref[0, 2 * p].T, zero_vk], axis=1)
            bot = jnp.concatenate([zero_vk, s0_ref[0, 2 * p + 1].T], axis=1)
            st_scr[p] = jnp.concatenate([top, bot], axis=0)

    q, k, v, g, gk = _gla_features(z_ref[0], gkw_ref[...], gkb_ref[...])
    ri = lax.broadcasted_iota(jnp.int32, (G, G), 0)
    ci = lax.broadcasted_iota(jnp.int32, (G, G), 1)
    cum_mat = ((ri // C == ci // C) & (ri >= ci)).astype(F32)
    for m in range(tb // G):
        rows = slice(m * G, (m + 1) * G)
        gc_scr[rows, :] = _dot(cum_mat, gk[rows, :], precision=HIGHEST)
    gcum = gc_scr[...]

    rg = lax.broadcasted_iota(jnp.int32, (tb, 2 * G), 0)
    cg = lax.broadcasted_iota(jnp.int32, (tb, 2 * G), 1)
    blk_mask = ((cg % G) // C == (rg % G) // C) & (cg % C <= rg % C)
    for p in range(GLA_HEADS // 2):
        ls = slice(p * 128, (p + 1) * 128)
        q3 = q[:, ls].reshape(nc, C, 128)
        k3 = k[:, ls].reshape(nc, C, 128)
        g3 = gcum[:, ls].reshape(nc, C, 128) * LOG2_E
        half = C // 2
        for j in range(C):
            lo = 0 if j < half else half
            e = (q3[:, lo:] * jnp.exp2(jnp.minimum(g3[:, lo:] - g3[:, j : j + 1, :], 0.0))) * k3[:, j : j + 1, :]
            if lo:
                e = jnp.concatenate([jnp.zeros((nc, lo, 128), F32), e], axis=1)
            x_scr[:, j * 128 : (j + 1) * 128] = e.reshape(tb, 128).astype(BF16)
        a_t = jnp.where(blk_mask, _dot(x_scr[...], wsel_ref[...]), 0.0).astype(BF16)
        for hl in range(2):
            h = 2 * p + hl
            for m in range(tb // G):
                rows = slice(m * G, (m + 1) * G)
                a_blk = a_t[rows, hl * G : (hl + 1) * G]
                oi_scr[rows, h * GLA_DV : (h + 1) * GLA_DV] = _dot(
                    a_blk, v[rows, h * GLA_DV : (h + 1) * GLA_DV].astype(BF16))

    CG = G // C
    rt = lax.broadcasted_iota(jnp.int32, (G, CG * 128), 0)
    ct = lax.broadcasted_iota(jnp.int32, (G, CG * 128), 1)
    own_chunk = rt // C == ct // 128
    rs = lax.broadcasted_iota(jnp.int32, (2 * GLA_DV, CG * 128), 0)
    cs = lax.broadcasted_iota(jnp.int32, (2 * GLA_DV, CG * 128), 1)
    same_head = rs // GLA_DV == (cs % 128) // GLA_DK

    def chunk_diag(x):
        return jnp.where(own_chunk, jnp.concatenate([x] * CG, axis=1), 0.0)

    for m in range(tb // G):
        rows = slice(m * G, (m + 1) * G)
        for p in range(GLA_HEADS // 2):
            ls = slice(p * 128, (p + 1) * 128)
            vs = slice(p * 2 * GLA_DV, (p + 1) * 2 * GLA_DV)
            g_g = gcum[rows, ls]
            g3 = g_g.reshape(CG, C, 128)
            g_last = jnp.broadcast_to(g3[:, C - 1 : C, :], (CG, C, 128)).reshape(G, 128)
            q_t = q[rows, ls] * jnp.exp(g_g)
            k_t = k[rows, ls] * jnp.exp(g_last - g_g)
            d_s = jnp.where(same_head, _dot_tn(v[rows, vs], chunk_diag(k_t)), 0.0)
            st = st_scr[p]
            starts = []
            for c in range(CG):
                starts.append(st)
                decay = jnp.exp(g_g[c * C + C - 1 : c * C + C, :])
                st = st * decay + d_s[:, c * 128 : (c + 1) * 128]
            st_scr[p] = st
            oi_scr[rows, vs] += _dot_nt(chunk_diag(q_t), jnp.concatenate(starts, axis=1))

    o_ref[0] = _gla_finish(oi_scr[...], g, nw_ref[...]).astype(o_ref.dtype)

    @pl.when(t_idx == pl.num_programs(1) - 1)
    def _():
        for p in range(GLA_HEADS // 2):
            st = st_scr[p]
            sout_ref[0, 2 * p] = st[:GLA_DV, :GLA_DK].T
            sout_ref[0, 2 * p + 1] = st[GLA_DV:, GLA_DK:].T


def _gla_select_matrix():
    j = jnp.arange(GLA_CHUNK)[:, None, None]
    hl = jnp.arange(2)[None, :, None]
    rows_j = jnp.broadcast_to(j, (GLA_CHUNK, 2, GLA_DK)).reshape(-1)
    rows_h = jnp.broadcast_to(hl, (GLA_CHUNK, 2, GLA_DK)).reshape(-1)
    cols = jnp.arange(256)
    sel = (rows_j[:, None] == cols[None, :] % GLA_CHUNK) & (rows_h[:, None] == cols[None, :] // 128)
    return sel.astype(BF16)


def _gla_seq(z3, s0, gl, tb):
    b, l, _ = z3.shape
    const = lambda shape: pl.BlockSpec(shape, lambda i, j: (0,) * len(shape))
    return pl.pallas_call(
        _gla_seq_kernel,
        grid=(b, l // tb),
        in_specs=[
            pl.BlockSpec((1, tb, GLA_PROJ_PAD), lambda i, j: (i, j, 0)),
            pl.BlockSpec((1, GLA_HEADS, GLA_DK, GLA_DV), lambda i, j: (i, 0, 0, 0)),
            const((GLA_LORA_PAD, GLA_QK)), const((1, GLA_QK)), const((1, GLA_DV)),
            const((GLA_CHUNK * 128, 256)),
        ],
        out_specs=[
            pl.BlockSpec((1, tb, GLA_WIDTH), lambda i, j: (i, j, 0)),
            pl.BlockSpec((1, GLA_HEADS, GLA_DK, GLA_DV), lambda i, j: (i, 0, 0, 0)),
        ],
        out_shape=[
            jax.ShapeDtypeStruct((b, l, GLA_WIDTH), BF16),
            jax.ShapeDtypeStruct((b, GLA_HEADS, GLA_DK, GLA_DV), F32),
        ],
        scratch_shapes=[
            pltpu.VMEM((GLA_HEADS // 2, 2 * GLA_DV, 2 * GLA_DK), F32),
            pltpu.VMEM((tb, GLA_CHUNK * 128), BF16),
            pltpu.VMEM((tb, GLA_QK), F32), pltpu.VMEM((tb, GLA_WIDTH), F32),
        ],
        compiler_params=_params(("parallel", "arbitrary")),
    )(z3, s0, gl["gkw"], gl["gkb"], gl["norm_w"], _gla_select_matrix())


def _gla_step_prep_kernel(z_ref, gkw_ref, gkb_ref, q_ref, k_ref, v_ref, g_ref, dec_ref):
    q, k, v, g, gk = _gla_features(z_ref[...], gkw_ref[...], gkb_ref[...])
    q_ref[...] = q
    k_ref[...] = k
    v_ref[...] = v
    g_ref[...] = g
    dec_ref[...] = jnp.exp(gk)


def _gla_step_prep(z, gl):
    n = z.shape[0]
    qk = jax.ShapeDtypeStruct((n, GLA_QK), F32)
    wide = jax.ShapeDtypeStruct((n, GLA_WIDTH), F32)
    return pl.pallas_call(
        _gla_step_prep_kernel,
        out_shape=[qk, qk, wide, wide, qk],
        compiler_params=pltpu.CompilerParams(vmem_limit_bytes=VMEM_LIMIT),
    )(z, gl["gkw"], gl["gkb"])


def _gla_step_kernel(s_ref, qcol_ref, kcol_ref, dcol_ref, vrow_ref, snew_ref, orow_ref):
    s_new = s_ref[...] * dcol_ref[...] + kcol_ref[...] * vrow_ref[...]
    snew_ref[...] = s_new
    orow_ref[...] = jnp.sum(s_new * qcol_ref[...], axis=2, keepdims=True)


def _gla_step(s0, q, k, dec, v, bb):
    n = s0.shape[0]
    colv = lambda t: t.reshape(n, GLA_HEADS, GLA_DK, 1)
    col_spec = pl.BlockSpec((bb, GLA_HEADS, GLA_DK, 1), lambda i: (i, 0, 0, 0))
    row_spec = pl.BlockSpec((bb, GLA_HEADS, 1, GLA_DV), lambda i: (i, 0, 0, 0))
    s_spec = pl.BlockSpec((bb, GLA_HEADS, GLA_DK, GLA_DV), lambda i: (i, 0, 0, 0))
    s_new, o_row = pl.pallas_call(
        _gla_step_kernel,
        grid=(n // bb,),
        in_specs=[s_spec, col_spec, col_spec, col_spec, row_spec],
        out_specs=[s_spec, row_spec],
        out_shape=[
            jax.ShapeDtypeStruct(s0.shape, F32),
            jax.ShapeDtypeStruct((n, GLA_HEADS, 1, GLA_DV), F32),
        ],
        compiler_params=_params(("parallel",)),
    )(s0, colv(q), colv(k), colv(dec), v.reshape(n, GLA_HEADS, 1, GLA_DV))
    return s_new, o_row.reshape(n, GLA_WIDTH)


def _step_post_kernel(orw_ref, bon_ref, grw_ref, lnw_ref, lnb_ref, ogl_ref, ggl_ref, nw_ref, o_rw_ref, o_gl_ref):
    o = orw_ref[...]
    for h in range(RW_HEADS):
        hs = slice(h * RW_N, (h + 1) * RW_N)
        o_h = o[:, hs]
        mean = jnp.mean(o_h, axis=-1, keepdims=True)
        cen = o_h - mean
        var = jnp.mean(cen * cen, axis=-1, keepdims=True)
        on = cen * lax.rsqrt(var + RW_GN_EPS)
        res = (on * lnw_ref[:, hs] + lnb_ref[:, hs] + bon_ref[:, hs]) * grw_ref[:, hs]
        o_rw_ref[:, hs] = res.astype(o_rw_ref.dtype)
    o_gl_ref[...] = _gla_finish(ogl_ref[...], ggl_ref[...], nw_ref[...]).astype(o_gl_ref.dtype)


def _step_post(o_rw, bonus, g_rw, rw, o_gl, g_gl, gl):
    n = o_rw.shape[0]
    return pl.pallas_call(
        _step_post_kernel,
        out_shape=[jax.ShapeDtypeStruct((n, RW_WIDTH), BF16), jax.ShapeDtypeStruct((n, GLA_WIDTH), BF16)],
        compiler_params=pltpu.CompilerParams(vmem_limit_bytes=VMEM_LIMIT),
    )(o_rw, bonus, g_rw, rw["ln_w"], rw["ln_b"], o_gl, g_gl, gl["norm_w"])


def _outproj_router_kernel(x_ref, orw_ref, ogl_ref, wo_ref, gain_ref, wrt_ref, br_ref,
                           h_ref, xn_ref, idx_ref, gate_ref, rank_ref, cnt_ref):
    tm = x_ref.shape[0]
    mix = jnp.concatenate([orw_ref[...], ogl_ref[...]], axis=-1)
    h = x_ref[...] + _dot(mix, wo_ref[...])
    h_ref[...] = h
    xn = h * lax.rsqrt(jnp.mean(h * h, axis=-1, keepdims=True) + NORM_EPS) * gain_ref[...]
    _store_row_tiles(xn_ref, xn)
    logits = _dot_nt(wrt_ref[...], xn) + br_ref[...]
    eidx = lax.broadcasted_iota(jnp.int32, logits.shape, 0)
    ti = lax.broadcasted_iota(jnp.int32, (tm, tm), 0)
    tj = lax.broadcasted_iota(jnp.int32, (tm, tm), 1)
    before = (ti < tj).astype(BF16)
    vals, idxs = [], []
    work = logits
    chosen = jnp.zeros(logits.shape, F32)
    for _ in range(TOP_K):
        m = jnp.max(work, axis=0, keepdims=True)
        sel = jnp.min(jnp.where(work == m, eidx, N_EXPERTS), axis=0, keepdims=True)
        hit = eidx == sel
        work = jnp.where(hit, -jnp.inf, work)
        chosen = chosen + hit.astype(F32)
        vals.append(m)
        idxs.append(sel)
    prefix = _dot(chosen.astype(BF16), before)
    exps = [jnp.exp(v - vals[0]) for v in vals]
    denom = exps[0] + exps[1] + exps[2] + exps[3]
    for j in range(TOP_K):
        idx_ref[0, j : j + 1, :] = idxs[j]
        gate_ref[0, j : j + 1, :] = exps[j] / denom
        rank = jnp.sum(jnp.where(eidx == idxs[j], prefix, 0.0), axis=0, keepdims=True)
        rank_ref[0, j : j + 1, :] = rank.astype(jnp.int32)
    cnt = jnp.sum(chosen, axis=1, keepdims=True)
    cnt_ref[0] = jnp.broadcast_to(cnt, (N_EXPERTS, 128)).astype(jnp.int32)


def _outproj_router(x, o_rw, o_gl, w_out, gain, w_router_t, b_router, tm):
    n = x.shape[0]
    nt = n // tm
    const = lambda shape: pl.BlockSpec(shape, lambda i: (0,) * len(shape))
    tok = lambda width: pl.BlockSpec((tm, width), lambda i: (i, 0))
    lane = pl.BlockSpec((1, TOP_K, tm), lambda i: (i, 0, 0))
    return pl.pallas_call(
        _outproj_router_kernel,
        grid=(nt,),
        in_specs=[
            tok(D_MODEL), tok(RW_WIDTH), tok(GLA_WIDTH),
            const((D_MODEL, D_MODEL)), const((1, D_MODEL)), const((N_EXPERTS, D_MODEL)), const((N_EXPERTS, 1)),
        ],
        out_specs=[tok(D_MODEL), pl.BlockSpec((tm * ROW_TILE, 128), lambda i: (i, 0)), lane, lane, lane,
                   pl.BlockSpec((1, N_EXPERTS, 128), lambda i: (i, 0, 0))],
        out_shape=[
            jax.ShapeDtypeStruct((n, D_MODEL), F32),
            jax.ShapeDtypeStruct((n * ROW_TILE, 128), F32),
            jax.ShapeDtypeStruct((nt, TOP_K, tm), jnp.int32),
            jax.ShapeDtypeStruct((nt, TOP_K, tm), F32),
            jax.ShapeDtypeStruct((nt, TOP_K, tm), jnp.int32),
            jax.ShapeDtypeStruct((nt, N_EXPERTS, 128), jnp.int32),
        ],
        compiler_params=_params(("parallel",)),
    )(x, o_rw, o_gl, w_out, gain, w_router_t, b_router)


def _moe_kernel(be_ref, nu_ref, epoch_ref, next_ref, rows_ref, parts_ref, xs_ref, wg_hbm, wu_hbm, wd_hbm, bg_ref, bu_ref, bd_ref,
                y_ref, w_f32, wg_b, wu_b, wd_b, sems):
    b = pl.program_id(0)
    prev = be_ref[jnp.maximum(b - 1, 0)]
    new_expert = (b == 0) | (be_ref[b] != prev)

    def fetch(e, slot, i):
        w = (wg_hbm, wu_hbm, wd_hbm)[i]
        return pltpu.make_async_copy(w.at[e], w_f32.at[slot, i], sems.at[slot])

    @pl.when(b == 0)
    def _():
        for i in range(3):
            fetch(be_ref[0], 0, i).start()

    @pl.when(new_expert)
    def _():
        slot = epoch_ref[b] % 2
        for i in range(3):
            fetch(be_ref[b], slot, i).wait()
        wg_b[...] = w_f32[slot, 0].astype(BF16)
        wu_b[...] = w_f32[slot, 1].astype(BF16)
        wd_b[...] = w_f32[slot, 2].astype(BF16)

    first_part, end_part = parts_ref[b] // 4, parts_ref[b] % 4
    for i in range(3):
        @pl.when((next_ref[b] >= 0) & (first_part <= i) & (i < end_part))
        def _():
            fetch(next_ref[b], 1 - epoch_ref[b] % 2, i).start()

    def ffn(m):
        x = _load_row_tiles(xs_ref, m).astype(BF16)
        half = D_MODEL // 2
        acc = None
        for f in range(2):
            fs = slice(f * half, (f + 1) * half)
            gt = _dot(x, wg_b[:, fs]) + bg_ref[0, :, fs]
            up = _dot(x, wu_b[:, fs]) + bu_ref[0, :, fs]
            gt = jnp.minimum(gt, SWIGLU_LIMIT)
            up = jnp.clip(up, -SWIGLU_LIMIT, SWIGLU_LIMIT)
            hid = (up + 1.0) * gt * _sigmoid(SWIGLU_ALPHA * gt)
            part = _dot(hid.astype(BF16), wd_b[fs, :])
            acc = part if acc is None else acc + part
        _store_row_tiles(y_ref, acc + bd_ref[0])

    quarter = MOE_BLOCK // 4
    quarters = (rows_ref[b] + quarter - 1) // quarter
    for q in range(1, 5):
        @pl.when((b < nu_ref[0]) & (quarters == q))
        def _():
            ffn(q * quarter)


def _moe_ffn(block_expert, n_used, block_rows, xs, w_gate, w_up, w_down, b_gate, b_up, b_down):
    n_blocks = block_expert.shape[0]
    pos = jnp.arange(n_blocks, dtype=jnp.int32)
    change = (pos > 0) & (block_expert != jnp.roll(block_expert, 1))
    epoch = jnp.cumsum(change.astype(jnp.int32))
    later = change[None, :] & (pos[None, :] > pos[:, None])
    first = jnp.min(jnp.where(later, pos[None, :], n_blocks), axis=1)
    next_e = jnp.sum(jnp.where(pos[None, :] == first[:, None], block_expert[None, :], 0), axis=1)
    next_e = jnp.where(first < n_blocks, next_e, -1).astype(jnp.int32)
    run_start = jnp.max(jnp.where((pos[None, :] <= pos[:, None]) & (change | (pos == 0))[None, :], pos[None, :], 0),
                        axis=1)
    q = jnp.minimum(pos - run_start, 3)
    used = pos < n_used[0]
    is_last = jnp.roll(change, -1) | (pos == n_used[0] - 1)
    parts = jnp.where(used, q * 4 + jnp.where(is_last, 3, jnp.minimum(q + 1, 3)), 15).astype(jnp.int32)

    row = lambda b, be, nu, ep, nx, br, pt: (jnp.minimum(b, nu[0] - 1), 0)
    bspec = pl.BlockSpec((1, 1, D_MODEL), lambda b, be, nu, ep, nx, br, pt: (be[b], 0, 0))
    wspec = pl.BlockSpec(memory_space=pl.ANY)
    grid_spec = pltpu.PrefetchScalarGridSpec(
        num_scalar_prefetch=6,
        grid=(n_blocks,),
        in_specs=[pl.BlockSpec((MOE_BLOCK * ROW_TILE, 128), row), wspec, wspec, wspec, bspec, bspec, bspec],
        out_specs=pl.BlockSpec((MOE_BLOCK * ROW_TILE, 128), row),
        scratch_shapes=[pltpu.VMEM((2, 3, D_MODEL, D_MODEL), F32)] + [pltpu.VMEM((D_MODEL, D_MODEL), BF16)] * 3
        + [pltpu.SemaphoreType.DMA((2,))],
    )
    return pl.pallas_call(
        _moe_kernel,
        grid_spec=grid_spec,
        out_shape=jax.ShapeDtypeStruct((n_blocks * MOE_BLOCK * ROW_TILE, 128), F32),
        compiler_params=_params(("arbitrary",)),
    )(block_expert, n_used, epoch, next_e, block_rows, parts, xs, w_gate, w_up, w_down,
      b_gate.reshape(N_EXPERTS, 1, D_MODEL), b_up.reshape(N_EXPERTS, 1, D_MODEL),
      b_down.reshape(N_EXPERTS, 1, D_MODEL))


SEG_ALIGN = 8
GROUP_ROWS = SEG_ALIGN * ROW_TILE


def _local_rows(tm):
    return tm * TOP_K + N_EXPERTS * SEG_ALIGN


BIG_COPY = 4


def _copy_tables(tm):
    n_big = _local_rows(tm) // SEG_ALIGN // BIG_COPY
    n_small = N_EXPERTS * (BIG_COPY - 1)
    return 2 * n_big, 2 * (n_big + n_small)


def _issue_group_copies(cp_ref, hbm, buf, sem, tm, to_hbm):
    small_at, counts_at = _copy_tables(tm)

    def copy(at, n_groups):
        rows = n_groups * GROUP_ROWS
        b = buf.at[pl.ds(pl.multiple_of(cp_ref[0, 0, at] * GROUP_ROWS, GROUP_ROWS), rows)]
        h = hbm.at[pl.ds(pl.multiple_of(cp_ref[0, 0, at + 1] * GROUP_ROWS, GROUP_ROWS), rows)]
        return pltpu.make_async_copy(b, h, sem) if to_hbm else pltpu.make_async_copy(h, b, sem)

    def big(i, carry):
        copy(2 * i, BIG_COPY).start()
        return carry

    def small(i, carry):
        copy(small_at + 2 * i, 1).start()
        return carry

    lax.fori_loop(0, cp_ref[0, 0, counts_at], big, 0)
    lax.fori_loop(0, cp_ref[0, 0, counts_at + 1], small, 0)


def _wait_group_copies(cp_ref, hbm, buf, sem, tm, to_hbm):
    rows = pl.ds(0, pl.multiple_of(cp_ref[0, 0, _copy_tables(tm)[1] + 2] * GROUP_ROWS, GROUP_ROWS))
    b, h = buf.at[rows], hbm.at[rows]
    (pltpu.make_async_copy(b, h, sem) if to_hbm else pltpu.make_async_copy(h, b, sem)).wait()


def _combine_kernel(grp_c, grp_n, lpos_ref, gate_ref, h_ref, gain_ref, y_hbm, o_ref, ybuf, fbuf, sems):
    i = pl.program_id(0)
    nt = pl.num_programs(0)
    tm = h_ref.shape[0]

    @pl.when(i == 0)
    def _():
        _issue_group_copies(grp_c, y_hbm, ybuf.at[0], sems.at[0], tm, False)

    @pl.when(i + 1 < nt)
    def _():
        _issue_group_copies(grp_n, y_hbm, ybuf.at[(i + 1) % 2], sems.at[(i + 1) % 2], tm, False)

    slot = i % 2
    yb = ybuf.at[slot]
    _wait_group_copies(grp_c, y_hbm, yb, sems.at[slot], tm, False)

    def token_body(t, carry):
        acc = None
        for j in range(TOP_K):
            row = pl.multiple_of(lpos_ref[0, 0, t * TOP_K + j], ROW_TILE)
            term = gate_ref[0, 0, t * TOP_K + j] * yb[pl.ds(row, ROW_TILE), :]
            acc = term if acc is None else acc + term
        fbuf[pl.ds(pl.multiple_of(t * ROW_TILE, ROW_TILE), ROW_TILE), :] = acc
        return carry

    lax.fori_loop(0, tm, token_body, 0, unroll=8)
    f = h_ref[...] + _load_row_tiles(fbuf, tm)
    o_ref[...] = f * lax.rsqrt(jnp.mean(f * f, axis=-1, keepdims=True) + NORM_EPS) * gain_ref[...]


def _combine(h, y_rows, grp3, lpos3, gate3, gain, tm):
    n = h.shape[0]
    nt = n // tm
    n_local = _local_rows(tm)
    gw = grp3.shape[-1]
    smem = lambda shape, imap: pl.BlockSpec(shape, imap, memory_space=pltpu.SMEM)
    cur = lambda i: (i, 0, 0)
    nxt = lambda i: (jnp.minimum(i + 1, nt - 1), 0, 0)
    return pl.pallas_call(
        _combine_kernel,
        grid=(nt,),
        in_specs=[
            smem((1, 1, gw), cur), smem((1, 1, gw), nxt),
            smem((1, 1, TOP_K * tm), cur), smem((1, 1, TOP_K * tm), cur),
            pl.BlockSpec((tm, D_MODEL), lambda i: (i, 0)),
            pl.BlockSpec((1, D_MODEL), lambda i: (0, 0)),
            pl.BlockSpec(memory_space=pl.ANY),
        ],
        out_specs=pl.BlockSpec((tm, D_MODEL), lambda i: (i, 0)),
        out_shape=jax.ShapeDtypeStruct((n, D_MODEL), F32),
        scratch_shapes=[pltpu.VMEM((2, n_local * ROW_TILE, 128), F32), pltpu.VMEM((tm * ROW_TILE, 128), F32),
                        pltpu.SemaphoreType.DMA((2,))],
        compiler_params=pltpu.CompilerParams(dimension_semantics=("arbitrary",), vmem_limit_bytes=VMEM_LIMIT,
                                             disable_bounds_checks=True),
    )(grp3, grp3, lpos3, gate3, h, gain, y_rows)


def _dispatch_kernel(*refs, fill):
    if fill:
        grp_ref, grp_prev, lpos_ref, ends_ref, x_ref, xs_hbm, sorted_buf, zero_scr, sems, zsem = refs
    else:
        grp_ref, grp_prev, lpos_ref, x_ref, _, xs_hbm, sorted_buf, sems = refs
    i = pl.program_id(0)
    tm = x_ref.shape[0] // ROW_TILE
    blk = MOE_BLOCK * ROW_TILE
    sorted_scr = sorted_buf.at[i % 2]
    sem = sems.at[i % 2]

    if fill:
        def fill_copy(e):
            start = pl.multiple_of((ends_ref[0, e] - MOE_BLOCK) * ROW_TILE, blk)
            return pltpu.make_async_copy(zero_scr, xs_hbm.at[pl.ds(start, blk)], zsem)

        @pl.when(i == 0)
        def _():
            zero_scr[...] = jnp.zeros(zero_scr.shape, zero_scr.dtype)
            for e in range(N_EXPERTS):
                @pl.when(ends_ref[1, e] > 0)
                def _():
                    fill_copy(e).start()

    sorted_scr[...] = jnp.zeros(sorted_scr.shape, sorted_scr.dtype)

    def move(t, carry):
        row = x_ref[pl.ds(pl.multiple_of(t * ROW_TILE, ROW_TILE), ROW_TILE), :]
        for j in range(TOP_K):
            dst = pl.multiple_of(lpos_ref[0, 0, t * TOP_K + j], ROW_TILE)
            sorted_scr[pl.ds(dst, ROW_TILE), :] = row
        return carry

    lax.fori_loop(0, tm, move, 0, unroll=8)

    if fill:
        @pl.when(i == 0)
        def _():
            for e in range(N_EXPERTS):
                @pl.when(ends_ref[1, e] > 0)
                def _():
                    fill_copy(e).wait()

    @pl.when(i > 0)
    def _():
        _wait_group_copies(grp_prev, xs_hbm, sorted_buf.at[(i + 1) % 2], sems.at[(i + 1) % 2], tm, True)

    _issue_group_copies(grp_ref, xs_hbm, sorted_scr, sem, tm, True)

    @pl.when(i == pl.num_programs(0) - 1)
    def _():
        _wait_group_copies(grp_ref, xs_hbm, sorted_scr, sem, tm, True)


def _dispatch(xn, grp3, lpos3, tm, n_slots, ends=None, xs=None):
    n = xn.shape[0] // ROW_TILE
    fill = xs is None
    smem = lambda shape, imap: pl.BlockSpec(shape, imap, memory_space=pltpu.SMEM)
    in_specs = [smem((1, 1, grp3.shape[-1]), lambda i: (i, 0, 0)),
                smem((1, 1, grp3.shape[-1]), lambda i: (jnp.maximum(i - 1, 0), 0, 0)),
                smem((1, 1, TOP_K * tm), lambda i: (i, 0, 0))]
    args = [grp3, grp3, lpos3]
    scratch = [pltpu.VMEM((2, _local_rows(tm) * ROW_TILE, 128), F32)]
    if fill:
        in_specs.append(smem((2, N_EXPERTS), lambda i: (0, 0)))
        args.append(ends)
        scratch.append(pltpu.VMEM((MOE_BLOCK * ROW_TILE, 128), F32))
    in_specs.append(pl.BlockSpec((tm * ROW_TILE, 128), lambda i: (i, 0)))
    args.append(xn)
    aliases = {}
    if not fill:
        in_specs.append(pl.BlockSpec(memory_space=pl.ANY))
        args.append(xs)
        aliases = {len(args) - 1: 0}
    scratch.append(pltpu.SemaphoreType.DMA((2,)))
    if fill:
        scratch.append(pltpu.SemaphoreType.DMA(()))
    return pl.pallas_call(
        functools.partial(_dispatch_kernel, fill=fill),
        grid=(n // tm,),
        in_specs=in_specs,
        out_specs=pl.BlockSpec(memory_space=pl.ANY),
        out_shape=jax.ShapeDtypeStruct((n_slots * ROW_TILE, 128), F32),
        scratch_shapes=scratch,
        input_output_aliases=aliases,
        compiler_params=pltpu.CompilerParams(dimension_semantics=("arbitrary",), vmem_limit_bytes=VMEM_LIMIT,
                                             disable_bounds_checks=True, has_side_effects=True),
    )(*args)


def _pad_rows(w, rows, offset):
    out = jnp.zeros((rows, w.shape[1]), w.dtype)
    return out.at[offset : offset + w.shape[0]].set(w)


def _routing_tables(counts, n_pairs):
    n_tiles = counts.shape[0]
    n_blocks = (n_pairs + n_tiles * N_EXPERTS * (SEG_ALIGN - 1) + N_EXPERTS * (MOE_BLOCK - 1)
                + MOE_BLOCK - 1) // MOE_BLOCK
    runs = (counts + SEG_ALIGN - 1) // SEG_ALIGN * SEG_ALIGN
    local_start = jnp.cumsum(runs, axis=1) - runs
    total = jnp.sum(runs, axis=0)
    padded = (total + MOE_BLOCK - 1) // MOE_BLOCK * MOE_BLOCK
    pends = jnp.cumsum(padded)
    pstarts = pends - padded
    global_start = pstarts[None, :] + jnp.cumsum(runs, axis=0) - runs
    blocks = jnp.arange(n_blocks, dtype=jnp.int32) * MOE_BLOCK
    n_used = (pends[-1] // MOE_BLOCK).astype(jnp.int32)
    owner = jnp.sum((pends[None, :] <= blocks[:, None]).astype(jnp.int32), axis=1)
    block_expert = jnp.minimum(owner, N_EXPERTS - 1)
    last = jnp.sum(jnp.where(jnp.arange(n_blocks) == n_used - 1, block_expert, 0))
    block_expert = jnp.where(jnp.arange(n_blocks) < n_used, block_expert, last)
    ends = jnp.stack([pends, padded]).astype(jnp.int32)
    row_end = jnp.sum(jnp.where(block_expert[:, None] == jnp.arange(N_EXPERTS), (pstarts + total)[None, :], 0), axis=1)
    block_rows = jnp.clip(row_end - blocks, 0, MOE_BLOCK).astype(jnp.int32)
    return (runs, local_start, global_start, ends, block_expert.astype(jnp.int32), n_used.reshape(1), block_rows,
            n_blocks)


def _tile_tables(idx3, rank3, runs, local_start, global_start, tm):
    experts = jnp.arange(N_EXPERTS, dtype=jnp.int32)
    hit = idx3[..., None] == experts
    lpos = rank3 + jnp.sum(jnp.where(hit, local_start[:, None, None, :], 0), axis=-1)
    lpos_rows = _token_major(lpos.astype(jnp.int32) * ROW_TILE)

    groups = runs // SEG_ALIGN
    l8, g8 = local_start // SEG_ALIGN, global_start // SEG_ALIGN
    n_big, n_small = groups // BIG_COPY, groups % BIG_COPY

    def copy_list(count, first_local, first_global, step, length):
        ends = jnp.cumsum(count, axis=1)
        o = jnp.arange(length, dtype=jnp.int32)
        owner = jnp.minimum(jnp.sum((ends[:, None, :] <= o[None, :, None]).astype(jnp.int32), axis=-1), N_EXPERTS - 1)
        sel = owner[..., None] == experts
        pick = lambda t: jnp.sum(jnp.where(sel, t[:, None, :], 0), axis=-1)
        k = (o[None, :] - pick(ends - count)) * step
        pairs = jnp.stack([pick(first_local) + k, pick(first_global) + k], axis=-1)
        live = (o[None, :] < ends[:, -1:])[..., None]
        return jnp.where(live, pairs, 0).reshape(count.shape[0], 2 * length)

    n_groups = _local_rows(tm) // SEG_ALIGN
    big = copy_list(n_big, l8, g8, BIG_COPY, n_groups // BIG_COPY)
    small = copy_list(n_small, l8 + n_big * BIG_COPY, g8 + n_big * BIG_COPY, 1, N_EXPERTS * (BIG_COPY - 1))
    counts = jnp.stack([jnp.sum(n_big, axis=1), jnp.sum(n_small, axis=1), jnp.sum(groups, axis=1)], axis=1)
    counts = jnp.pad(counts, ((0, 0), (0, 5)))
    return lpos_rows, jnp.concatenate([big, small, counts], axis=1).astype(jnp.int32)[:, None, :]


def _token_major(t3):
    return jnp.swapaxes(t3, 1, 2).reshape(t3.shape[0], 1, -1)


def kernel(x_prompt, x_sample, state_rwkv_shift, state_rwkv_wkv, state_gla, norm_mix, w_in, rw_mu, rw_w0, rw_w2, rw_a0, rw_a2, rw_g2, rw_k_k, rw_k_a, rw_r_k, rw_ln_w, rw_ln_b, gla_gk_w2, gla_gk_b, gla_norm_w, w_out, norm_ffn, w_router, b_router, w_gate, b_gate, w_up, b_up, w_down, b_down, norm_final):
    depth = norm_mix.shape[0]
    assert depth == 1
    bp, lp, d = x_prompt.shape
    bs, ls, _ = x_sample.shape
    assert ls == 1 and lp % SEQ_BLOCK == 0
    l = 0
    row = lambda t: t.reshape(1, -1)

    w_in_b = w_in[l].astype(BF16)
    w_in_r = w_in_b[:, :RW_PROJ]
    w_in_g = jnp.pad(w_in_b[:, RW_PROJ:], ((0, 0), (0, GLA_PROJ_PAD - GLA_PROJ)))
    rw = dict(
        mu=row(rw_mu[l]), w0=row(rw_w0[l]), a0=row(rw_a0[l]),
        w2p=_pad_rows(rw_w2[l].astype(BF16), 128, 0), a2p=_pad_rows(rw_a2[l].astype(BF16), 128, 64),
        g2=rw_g2[l].astype(BF16), k_k=row(rw_k_k[l]), k_a=row(rw_k_a[l]), r_k=row(rw_r_k[l]),
        ln_w=row(rw_ln_w[l]), ln_b=row(rw_ln_b[l]))
    gl = dict(gkw=_pad_rows(gla_gk_w2[l].astype(BF16), GLA_LORA_PAD, 0), gkb=row(gla_gk_b[l]),
              norm_w=row(gla_norm_w[l]))
    gain_mix = row(norm_mix[l])

    n_p = bp * lp
    xp = x_prompt.reshape(n_p, d)
    zr_p, zg_p = _inproj(xp, gain_mix, w_in_r, w_in_g, TOK_BLOCK)
    zr_p3 = zr_p.reshape(bp, lp, RW_PROJ)
    o_rw_p, wkv_p = _rwkv_seq(zr_p3, jnp.zeros((bp, 1, RW_PROJ), F32),
                              jnp.zeros((bp, RW_HEADS, RW_N, RW_N), F32), rw, SEQ_BLOCK)
    o_gl_p, gla_p = _gla_seq(zg_p.reshape(bp, lp, GLA_PROJ_PAD),
                             jnp.zeros((bp, GLA_HEADS, GLA_DK, GLA_DV), F32), gl, SEQ_BLOCK)
    shift_p = zr_p3[:, -1, :]

    xs_ = x_sample.reshape(bs, d)
    zr_s, zg_s = _inproj(xs_, gain_mix, w_in_r, w_in_g, bs)
    r, k, v, al, be, dec, g_rw, bonus = _rwkv_step_prep(zr_s, state_rwkv_shift[l], rw)
    wkv_s, o_rw_s = _rwkv_step(state_rwkv_wkv[l], r, k, al, be, dec, v)
    q, kg, vg, g_gl, dec_g = _gla_step_prep(zg_s, gl)
    gla_s, o_gl_s = _gla_step(state_gla[l], q, kg, dec_g, vg, 16)
    o_rw_s2, o_gl_s2 = _step_post(o_rw_s, bonus, g_rw, rw, o_gl_s, g_gl, gl)
    shift_s = zr_s

    w_out_b = w_out[l].astype(BF16)
    router = (w_out_b, row(norm_ffn[l]), w_router[l].T, b_router[l].reshape(N_EXPERTS, 1))
    h_p, xn_p, idx_p, gate_p, rank_p, cnt_p = _outproj_router(
        xp, o_rw_p.reshape(n_p, RW_WIDTH), o_gl_p.reshape(n_p, GLA_WIDTH), *router, TOK_BLOCK)
    h_s, xn_s, idx_s, gate_s, rank_s, cnt_s = _outproj_router(xs_, o_rw_s2, o_gl_s2, *router, bs)
    nt_p = n_p // TOK_BLOCK
    counts = jnp.concatenate([cnt_p[:, :, 0], cnt_s[:, :, 0]], axis=0)
    runs, lstart, gstart, ends, block_expert, n_used, block_rows, n_blocks = _routing_tables(
        counts, (n_p + bs) * TOP_K)
    n_slots = n_blocks * MOE_BLOCK
    lpos_p, grp_p = _tile_tables(idx_p, rank_p, runs[:nt_p], lstart[:nt_p], gstart[:nt_p], TOK_BLOCK)
    lpos_s, grp_s = _tile_tables(idx_s, rank_s, runs[nt_p:], lstart[nt_p:], gstart[nt_p:], bs)
    xs_rows = _dispatch(xn_p, grp_p, lpos_p, TOK_BLOCK, n_slots, ends=ends)
    xs_rows = _dispatch(xn_s, grp_s, lpos_s, bs, n_slots, xs=xs_rows)
    y_rows = _moe_ffn(block_expert, n_used, block_rows, xs_rows, w_gate[l], w_up[l], w_down[l], b_gate[l], b_up[l], b_down[l])
    gain_f = row(norm_final)
    y_p = _combine(h_p, y_rows, grp_p, lpos_p, _token_major(gate_p), gain_f, TOK_BLOCK)
    y_s = _combine(h_s, y_rows, grp_s, lpos_s, _token_major(gate_s), gain_f, bs)

    y_prompt = y_p.reshape(bp, lp, d)
    y_sample = y_s.reshape(bs, ls, d)
    return (y_prompt, y_sample, shift_p[None], wkv_p[None], gla_p[None], shift_s[None], wkv_s[None], gla_s[None])
```

```python
import functools

import jax
import jax.numpy as jnp
from jax import lax
from jax.experimental import pallas as pl
from jax.experimental.pallas import tpu as pltpu

F32 = jnp.float32
BF16 = jnp.bfloat16
HIGHEST = lax.Precision.HIGHEST

D_MODEL = 1024
RW_WIDTH = 512
RW_HEADS = 8
RW_N = 64
RW_PROJ = 1792
RW_GN_EPS = 64e-5
GLA_HEADS = 4
GLA_DK = 64
GLA_DV = 128
GLA_WIDTH = 512
GLA_QK = GLA_HEADS * GLA_DK
GLA_PROJ = 1552
GLA_PROJ_PAD = 1664
GLA_LORA_PAD = 128
GLA_GATE_NORMALIZER = 16.0
N_EXPERTS = 32
TOP_K = 4
SWIGLU_LIMIT = 7.0
SWIGLU_ALPHA = 1.702
NORM_EPS = 1e-5
LOG2_E = 1.4426950408889634

RW_CHUNK = 64
GLA_CHUNK = 16
SEQ_BLOCK = 512
TOK_BLOCK = 512
MOE_BLOCK = 512
VMEM_LIMIT = 56 * 1024 * 1024


def _dot(a, b, precision=None):
    return jnp.dot(a, b, preferred_element_type=F32, precision=precision)


def _dot_nt(a, b, precision=None):
    return lax.dot_general(a, b, (((1,), (1,)), ((), ())), preferred_element_type=F32, precision=precision)


def _dot_tn(a, b, precision=None):
    return lax.dot_general(a, b, (((0,), (0,)), ((), ())), preferred_element_type=F32, precision=precision)


def _sigmoid(x):
    return 1.0 / (1.0 + jnp.exp(-x))


def _softplus(x):
    return jnp.maximum(x, 0.0) + jnp.log(1.0 + jnp.exp(-jnp.abs(x)))


def _params(sem):
    return pltpu.CompilerParams(dimension_semantics=sem, vmem_limit_bytes=VMEM_LIMIT)


ROW_TILE = D_MODEL // 128


def _store_row_tiles(ref, x):
    m = x.shape[0]
    for c in range(ROW_TILE):
        ref[pl.ds(c, m, stride=ROW_TILE), :] = x[:, c * 128 : (c + 1) * 128]


def _load_row_tiles(ref, m):
    return jnp.concatenate([ref[pl.ds(c, m, stride=ROW_TILE), :] for c in range(ROW_TILE)], axis=-1)


def _inproj_kernel(x_ref, gain_ref, wr_ref, wg_ref, zr_ref, zg_ref):
    x = x_ref[...]
    xn = x * lax.rsqrt(jnp.mean(x * x, axis=-1, keepdims=True) + NORM_EPS) * gain_ref[...]
    xb = xn.astype(BF16)
    zr_ref[...] = _dot(xb, wr_ref[...])
    zg_ref[...] = _dot(xb, wg_ref[...])


def _inproj(x, gain, w_r, w_g, tm):
    n = x.shape[0]
    return pl.pallas_call(
        _inproj_kernel,
        grid=(n // tm,),
        in_specs=[
            pl.BlockSpec((tm, D_MODEL), lambda i: (i, 0)),
            pl.BlockSpec((1, D_MODEL), lambda i: (0, 0)),
            pl.BlockSpec((D_MODEL, RW_PROJ), lambda i: (0, 0)),
            pl.BlockSpec((D_MODEL, GLA_PROJ_PAD), lambda i: (0, 0)),
        ],
        out_specs=[
            pl.BlockSpec((tm, RW_PROJ), lambda i: (i, 0)),
            pl.BlockSpec((tm, GLA_PROJ_PAD), lambda i: (i, 0)),
        ],
        out_shape=[
            jax.ShapeDtypeStruct((n, RW_PROJ), F32),
            jax.ShapeDtypeStruct((n, GLA_PROJ_PAD), F32),
        ],
        compiler_params=_params(("parallel",)),
    )(x, gain, w_r, w_g)


def _rwkv_features(zs, w0, w2p, a0, a2p, g2, k_k, k_a):
    W = RW_WIDTH
    r = zs[:, 0:W]
    k_raw = zs[:, W : 2 * W]
    v = zs[:, 2 * W : 3 * W]
    zwa = zs[:, 3 * W : 3 * W + 128]
    zg = zs[:, 3 * W + 128 :]
    w = -_softplus(-(w0 + _dot(jnp.tanh(zwa).astype(BF16), w2p))) - 0.5
    log_decay = -jnp.exp(w)
    a = _sigmoid(a0 + _dot(zwa.astype(BF16), a2p))
    g = _dot(_sigmoid(zg).astype(BF16), g2)
    kk_raw = k_raw * k_k
    k = k_raw * (1.0 + (a - 1.0) * k_a)
    return r, k, v, kk_raw, a, log_decay, g


def _level_mask(ri, ci, lvl):
    same = (ri >> (lvl + 1)) == (ci >> (lvl + 1))
    return same & (((ri >> lvl) & 1) == 1) & (((ci >> lvl) & 1) == 0)


def _rwkv_seq_kernel(z_ref, shift0_ref, s0_ref, mu_ref, w0_ref, w2_ref, a0_ref, a2_ref, g2_ref, kk_ref, ka_ref,
                     rk_ref, lnw_ref, lnb_ref, o_ref, sout_ref,
                     m_scr, prev_scr, r_scr, k_scr, v_scr, kkr_scr, a_scr, lw_scr, on_scr, bon_scr):
    C = RW_CHUNK
    N = RW_N
    t_idx = pl.program_id(1)
    tb = z_ref.shape[1]
    zero_nn = jnp.zeros((N, N), F32)

    @pl.when(t_idx == 0)
    def _():
        prev_scr[...] = shift0_ref[0]
        for p in range(RW_HEADS // 2):
            top = jnp.concatenate([s0_ref[0, 2 * p].T, zero_nn], axis=1)
            bot = jnp.concatenate([zero_nn, s0_ref[0, 2 * p + 1].T], axis=1)
            m_scr[p] = jnp.concatenate([top, bot], axis=0)

    z = z_ref[0]
    row = lax.broadcasted_iota(jnp.int32, z.shape, 0)
    z_prev = jnp.where(row == 0, prev_scr[...], pltpu.roll(z, 1, axis=0))
    prev_scr[...] = z[tb - 1 : tb, :]
    zs = z + mu_ref[...] * (z_prev - z)
    r, k, v, kk_raw, a, log_decay, g = _rwkv_features(
        zs, w0_ref[...], w2_ref[...], a0_ref[...], a2_ref[...], g2_ref[...], kk_ref[...], ka_ref[...])
    P2 = 2 * N
    left1 = lax.broadcasted_iota(jnp.int32, (1, P2), 1) < N

    def head_sum(x):
        s0 = jnp.sum(jnp.where(left1, x, 0.0), axis=-1, keepdims=True)
        s1 = jnp.sum(jnp.where(left1, 0.0, x), axis=-1, keepdims=True)
        return jnp.where(left1, s0, s1)

    def head_sum_wide(x):
        return jnp.concatenate([head_sum(x[:, p * P2 : (p + 1) * P2]) for p in range(RW_HEADS // 2)], axis=1)

    alpha = kk_raw * lax.rsqrt(jnp.maximum(head_sum_wide(kk_raw * kk_raw), 1e-24))
    r_scr[...] = r
    k_scr[...] = k
    v_scr[...] = v
    kkr_scr[...] = alpha
    a_scr[...] = alpha * a
    lw_scr[...] = log_decay
    bon_scr[...] = head_sum_wide(r * k * rk_ref[...]) * v

    ri = lax.broadcasted_iota(jnp.int32, (C, P2), 0)
    ci = lax.broadcasted_iota(jnp.int32, (C, P2), 1) % N
    left = lax.broadcasted_iota(jnp.int32, (C, P2), 1) < N
    tril = ri >= ci
    stril = ri > ci
    eye_f = (ri == ci).astype(F32)
    rb = lax.broadcasted_iota(jnp.int32, (P2, P2), 0)
    cb = lax.broadcasted_iota(jnp.int32, (P2, P2), 1)
    same_head = (rb < N) == (cb < N)
    eye_b = rb == cb
    rc = lax.broadcasted_iota(jnp.int32, (C, C), 0)
    cc = lax.broadcasted_iota(jnp.int32, (C, C), 1)
    tril_f = (rc >= cc).astype(F32)

    def bdiag(x):
        return jnp.concatenate([jnp.where(left, x, 0.0), jnp.where(left, 0.0, x)], axis=0)

    n_sub = tb // C
    pairs = range(RW_HEADS // 2)

    def chunk_body(it, carry):
        units = [(s, p) for s in range(n_sub) for p in pairs]
        sls = [pl.ds(pl.multiple_of((it * n_sub + s) * C, C), C) for s in range(n_sub)]
        prep = []
        for s in range(n_sub):
            lw = lw_scr[sls[s], :]
            cum = _dot(tril_f, lw, precision=HIGHEST)
            cum_last = cum[C - 1 : C, :]
            prep.append(dict(
                e_incl=jnp.exp(cum), e_excl=jnp.exp(cum - lw), e_neg=jnp.exp(-cum),
                e_tail=jnp.exp(cum_last - cum), p_last=jnp.exp(cum_last),
                r=r_scr[sls[s], :], k=k_scr[sls[s], :], v=v_scr[sls[s], :], kk=kkr_scr[sls[s], :],
                a=a_scr[sls[s], :]))
        lanes = [slice(p * P2, (p + 1) * P2) for p in pairs]
        get = lambda name: [prep[s][name][:, lanes[p]] for s, p in units]
        r2, k2, v2, al, be = get("r"), get("k"), get("v"), get("kk"), get("a")
        e_incl, e_excl, e_neg, e_tail, p_last = get("e_incl"), get("e_excl"), get("e_neg"), get("e_tail"), get("p_last")
        un = range(len(units))
        al_t = [al[u] * e_excl[u] for u in un]
        r_t = [r2[u] * e_incl[u] for u in un]
        be_n = [be[u] * e_neg[u] for u in un]
        k_n = [k2[u] * e_neg[u] for u in un]
        k_et = [(k2[u] * e_tail[u]).T for u in un]
        be_et = [(be[u] * e_tail[u]).T for u in un]
        v_bd = [bdiag(v2[u]) for u in un]
        lhs = [jnp.concatenate([al_t[u], r_t[u]], axis=0) for u in un]
        s_b = [_dot_nt(lhs[u], bdiag(be_n[u])) for u in un]
        s_k = [_dot_nt(lhs[u], bdiag(k_n[u])) for u in un]
        l_ab = [jnp.where(stril, s_b[u][:C], 0.0) for u in un]
        a_rb = [jnp.where(tril, s_b[u][C:], 0.0) for u in un]
        l_ak = [jnp.where(stril, s_k[u][:C], 0.0) for u in un]
        a_rk = [jnp.where(tril, s_k[u][C:], 0.0) for u in un]
        lakv = [_dot(l_ak[u], v_bd[u]) for u in un]
        arkv = [_dot(a_rk[u], v_bd[u]) for u in un]
        kev = [_dot(k_et[u], v2[u]) for u in un]
        t_inv = [eye_f - jnp.where(_level_mask(ri, ci, 0), l_ab[u], 0.0) for u in un]
        lvl = 1
        while (1 << lvl) < C:
            lm = _level_mask(ri, ci, lvl)
            tn = [_dot(t_inv[u], bdiag(jnp.where(lm, l_ab[u], 0.0))) for u in un]
            t_inv = [t_inv[u] - _dot(tn[u], bdiag(t_inv[u])) for u in un]
            lvl += 1
        a_til = [_dot(t_inv[u], bdiag(al_t[u])) for u in un]
        b_til = [_dot(t_inv[u], bdiag(lakv[u])) for u in un]
        r_hat = [r_t[u] - _dot(a_rb[u], bdiag(a_til[u])) for u in un]
        o_hat = [arkv[u] - _dot(a_rb[u], bdiag(b_til[u])) for u in un]
        g_bd = [jnp.where(same_head, jnp.where(eye_b, p_last[u], 0.0) - _dot(be_et[u], a_til[u]), 0.0) for u in un]
        h_bd = [jnp.where(same_head, kev[u] - _dot(be_et[u], b_til[u]), 0.0) for u in un]
        lhs_m = [jnp.concatenate([r_hat[u], g_bd[u]], axis=0) for u in un]
        for u, (s, p) in enumerate(units):
            res = _dot(lhs_m[u], m_scr[p])
            m_scr[p] = res[C:] + h_bd[u]
            o_p = res[:C] + o_hat[u]
            cen = o_p - head_sum(o_p) * (1.0 / N)
            var = head_sum(cen * cen) * (1.0 / N)
            on_scr[sls[s], lanes[p]] = cen * lax.rsqrt(var + RW_GN_EPS)
        return carry

    lax.fori_loop(0, tb // (C * n_sub), chunk_body, 0)
    out = (on_scr[...] * lnw_ref[...] + lnb_ref[...] + bon_scr[...]) * g
    o_ref[0] = out.astype(o_ref.dtype)

    @pl.when(t_idx == pl.num_programs(1) - 1)
    def _():
        for p in range(RW_HEADS // 2):
            m = m_scr[p]
            sout_ref[0, 2 * p] = m[:N, :N].T
            sout_ref[0, 2 * p + 1] = m[N:, N:].T


def _rwkv_seq(z3, shift0, s0, rw, tb):
    b, l, _ = z3.shape
    const = lambda shape: pl.BlockSpec(shape, lambda i, j: (0,) * len(shape))
    wide = lambda: pltpu.VMEM((tb, RW_WIDTH), F32)
    return pl.pallas_call(
        _rwkv_seq_kernel,
        grid=(b, l // tb),
        in_specs=[
            pl.BlockSpec((1, tb, RW_PROJ), lambda i, j: (i, j, 0)),
            pl.BlockSpec((1, 1, RW_PROJ), lambda i, j: (i, 0, 0)),
            pl.BlockSpec((1, RW_HEADS, RW_N, RW_N), lambda i, j: (i, 0, 0, 0)),
            const((1, RW_PROJ)),
            const((1, RW_WIDTH)), const((128, RW_WIDTH)),
            const((1, RW_WIDTH)), const((128, RW_WIDTH)),
            const((128, RW_WIDTH)),
            const((1, RW_WIDTH)), const((1, RW_WIDTH)), const((1, RW_WIDTH)),
            const((1, RW_WIDTH)), const((1, RW_WIDTH)),
        ],
        out_specs=[
            pl.BlockSpec((1, tb, RW_WIDTH), lambda i, j: (i, j, 0)),
            pl.BlockSpec((1, RW_HEADS, RW_N, RW_N), lambda i, j: (i, 0, 0, 0)),
        ],
        out_shape=[
            jax.ShapeDtypeStruct((b, l, RW_WIDTH), BF16),
            jax.ShapeDtypeStruct((b, RW_HEADS, RW_N, RW_N), F32),
        ],
        scratch_shapes=[
            pltpu.VMEM((RW_HEADS // 2, 2 * RW_N, 2 * RW_N), F32),
            pltpu.VMEM((1, RW_PROJ), F32),
            wide(), wide(), wide(), wide(), wide(), wide(), wide(), wide(),
        ],
        compiler_params=_params(("parallel", "arbitrary")),
    )(z3, shift0, s0, rw["mu"], rw["w0"], rw["w2p"], rw["a0"], rw["a2p"], rw["g2"], rw["k_k"], rw["k_a"],
      rw["r_k"], rw["ln_w"], rw["ln_b"])


def _rwkv_step_prep_kernel(z_ref, shift0_ref, mu_ref, w0_ref, w2_ref, a0_ref, a2_ref, g2_ref, kk_ref, ka_ref,
                           rk_ref, r_ref, k_ref, v_ref, al_ref, be_ref, dec_ref, g_ref, bon_ref):
    z = z_ref[...]
    zs = z + mu_ref[...] * (shift0_ref[...] - z)
    r, k, v, kk_raw, a, log_decay, g = _rwkv_features(
        zs, w0_ref[...], w2_ref[...], a0_ref[...], a2_ref[...], g2_ref[...], kk_ref[...], ka_ref[...])
    rk_all = rk_ref[...]
    for h in range(RW_HEADS):
        hs = slice(h * RW_N, (h + 1) * RW_N)
        kk_h = kk_raw[:, hs]
        nrm = jnp.sqrt(jnp.sum(kk_h * kk_h, axis=-1, keepdims=True))
        al = kk_h / jnp.maximum(nrm, 1e-12)
        al_ref[:, hs] = al
        be_ref[:, hs] = al * a[:, hs]
        bon_ref[:, hs] = jnp.sum(r[:, hs] * k[:, hs] * rk_all[:, hs], axis=-1, keepdims=True) * v[:, hs]
    r_ref[...] = r
    k_ref[...] = k
    v_ref[...] = v
    dec_ref[...] = jnp.exp(log_decay)
    g_ref[...] = g


def _rwkv_step_prep(z, shift0, rw):
    n = z.shape[0]
    out = jax.ShapeDtypeStruct((n, RW_WIDTH), F32)
    return pl.pallas_call(
        _rwkv_step_prep_kernel,
        out_shape=[out] * 8,
        compiler_params=pltpu.CompilerParams(vmem_limit_bytes=VMEM_LIMIT),
    )(z, shift0, rw["mu"], rw["w0"], rw["w2p"], rw["a0"], rw["a2p"], rw["g2"], rw["k_k"], rw["k_a"], rw["r_k"])


def _rwkv_step_kernel(s_ref, r_ref, k_ref, al_ref, be_ref, dec_ref, v_ref, snew_ref, o_ref):
    r, k, al, be, dec = r_ref[...], k_ref[...], al_ref[...], be_ref[...], dec_ref[...]

    def body(g, carry):
        rows = pl.ds(pl.multiple_of(g * 8, 8), 8)
        v8 = v_ref[rows, :]
        outs = []
        for j in range(8):
            s = s_ref[0, g * 8 + j]
            sa = -jnp.sum(s * al, axis=0, keepdims=True)
            s_new = s * dec + sa * be + v8[j : j + 1, :] * k
            snew_ref[0, g * 8 + j] = s_new
            outs.append(jnp.sum(s_new * r, axis=0, keepdims=True))
        o_ref[rows, :] = jnp.concatenate(outs, axis=0)
        return carry

    lax.fori_loop(0, RW_N // 8, body, 0)


def _rwkv_step(s0, r, k, al, be, dec, v):
    n = s0.shape[0]
    s_t = jnp.transpose(s0, (1, 2, 3, 0))
    s_spec = pl.BlockSpec((1, RW_N, RW_N, n), lambda h: (h, 0, 0, 0))
    op_spec = pl.BlockSpec((RW_N, n), lambda h: (h, 0))
    s_new_t, o_t = pl.pallas_call(
        _rwkv_step_kernel,
        grid=(RW_HEADS,),
        in_specs=[s_spec] + [op_spec] * 6,
        out_specs=[s_spec, op_spec],
        out_shape=[
            jax.ShapeDtypeStruct(s_t.shape, F32),
            jax.ShapeDtypeStruct((RW_WIDTH, n), F32),
        ],
        compiler_params=_params(("parallel",)),
    )(s_t, r.T, k.T, al.T, be.T, dec.T, v.T)
    return jnp.transpose(s_new_t, (3, 0, 1, 2)), o_t.T


def _gla_features(z, gkw, gkb):
    q = z[:, 0:GLA_QK] * (GLA_DK ** -0.5)
    k = z[:, GLA_QK : 2 * GLA_QK]
    v = z[:, 2 * GLA_QK : 2 * GLA_QK + GLA_WIDTH]
    g = z[:, 2 * GLA_QK + GLA_WIDTH : 2 * GLA_QK + 2 * GLA_WIDTH]
    zgk = z[:, 2 * GLA_QK + 2 * GLA_WIDTH :]
    gk = -_softplus(-(_dot(zgk.astype(BF16), gkw) + gkb)) / GLA_GATE_NORMALIZER
    return q, k, v, g, gk


def _gla_finish(o, g, norm_w):
    outs = []
    for h in range(GLA_HEADS):
        hs = slice(h * GLA_DV, (h + 1) * GLA_DV)
        o_h = o[:, hs]
        o_h = o_h * lax.rsqrt(jnp.mean(o_h * o_h, axis=-1, keepdims=True) + NORM_EPS) * norm_w
        g_h = g[:, hs]
        outs.append(o_h * (g_h * _sigmoid(g_h)))
    return jnp.concatenate(outs, axis=-1)


def _gla_seq_kernel(z_ref, s0_ref, gkw_ref, gkb_ref, nw_ref, wsel_ref, o_ref, sout_ref,
                    st_scr, x_scr, gc_scr, oi_scr):
    C = GLA_CHUNK
    G = 128
    t_idx = pl.program_id(1)
    tb = z_ref.shape[1]
    nc = tb // C
    zero_vk = jnp.zeros((GLA_DV, GLA_DK), F32)

    @pl.when(t_idx == 0)
    def _():
        for p in range(GLA_HEADS // 2):
            top = jnp.concatenate([s0_ref[0, 2 * p].T, zero_vk], axis=1)
            bot = jnp.concatenate([zero_vk, s0_ref[0, 2 * p + 1].T], axis=1)
            st_scr[p] = jnp.concatenate([top, bot], axis=0)

    q, k, v, g, gk = _gla_features(z_ref[0], gkw_ref[...], gkb_ref[...])
    ri = lax.broadcasted_iota(jnp.int32, (G, G), 0)
    ci = lax.broadcasted_iota(jnp.int32, (G, G), 1)
    cum_mat = ((ri // C == ci // C) & (ri >= ci)).astype(F32)
    for m in range(tb // G):
        rows = slice(m * G, (m + 1) * G)
        gc_scr[rows, :] = _dot(cum_mat, gk[rows, :], precision=HIGHEST)
    gcum = gc_scr[...]

    rg = lax.broadcasted_iota(jnp.int32, (tb, 2 * G), 0)
    cg = lax.broadcasted_iota(jnp.int32, (tb, 2 * G), 1)
    blk_mask = ((cg % G) // C == (rg % G) // C) & (cg % C <= rg % C)
    for p in range(GLA_HEADS // 2):
        ls = slice(p * 128, (p + 1) * 128)
        q3 = q[:, ls].reshape(nc, C, 128)
        k3 = k[:, ls].reshape(nc, C, 128)
        g3 = gcum[:, ls].reshape(nc, C, 128) * LOG2_E
        half = C // 2
        for j in range(C):
            lo = 0 if j < half else half
            e = (q3[:, lo:] * jnp.exp2(jnp.minimum(g3[:, lo:] - g3[:, j : j + 1, :], 0.0))) * k3[:, j : j + 1, :]
            if lo:
                e = jnp.concatenate([jnp.zeros((nc, lo, 128), F32), e], axis=1)
            x_scr[:, j * 128 : (j + 1) * 128] = e.reshape(tb, 128).astype(BF16)
        a_t = jnp.where(blk_mask, _dot(x_scr[...], wsel_ref[...]), 0.0).astype(BF16)
        for hl in range(2):
            h = 2 * p + hl
            for m in range(tb // G):
                rows = slice(m * G, (m + 1) * G)
                a_blk = a_t[rows, hl * G : (hl + 1) * G]
                oi_scr[rows, h * GLA_DV : (h + 1) * GLA_DV] = _dot(
                    a_blk, v[rows, h * GLA_DV : (h + 1) * GLA_DV].astype(BF16))

    CG = G // C
    rt = lax.broadcasted_iota(jnp.int32, (G, CG * 128), 0)
    ct = lax.broadcasted_iota(jnp.int32, (G, CG * 128), 1)
    own_chunk = rt // C == ct // 128
    rs = lax.broadcasted_iota(jnp.int32, (2 * GLA_DV, CG * 128), 0)
    cs = lax.broadcasted_iota(jnp.int32, (2 * GLA_DV, CG * 128), 1)
    same_head = rs // GLA_DV == (cs % 128) // GLA_DK

    def chunk_diag(x):
        return jnp.where(own_chunk, jnp.concatenate([x] * CG, axis=1), 0.0)

    for m in range(tb // G):
        rows = slice(m * G, (m + 1) * G)
        for p in range(GLA_HEADS // 2):
            ls = slice(p * 128, (p + 1) * 128)
            vs = slice(p * 2 * GLA_DV, (p + 1) * 2 * GLA_DV)
            g_g = gcum[rows, ls]
            g3 = g_g.reshape(CG, C, 128)
            g_last = jnp.broadcast_to(g3[:, C - 1 : C, :], (CG, C, 128)).reshape(G, 128)
            q_t = q[rows, ls] * jnp.exp(g_g)
            k_t = k[rows, ls] * jnp.exp(g_last - g_g)
            d_s = jnp.where(same_head, _dot_tn(v[rows, vs], chunk_diag(k_t)), 0.0)
            st = st_scr[p]
            starts = []
            for c in range(CG):
                starts.append(st)
                decay = jnp.exp(g_g[c * C + C - 1 : c * C + C, :])
                st = st * decay + d_s[:, c * 128 : (c + 1) * 128]
            st_scr[p] = st
            oi_scr[rows, vs] += _dot_nt(chunk_diag(q_t), jnp.concatenate(starts, axis=1))

    o_ref[0] = _gla_finish(oi_scr[...], g, nw_ref[...]).astype(o_ref.dtype)

    @pl.when(t_idx == pl.num_programs(1) - 1)
    def _():
        for p in range(GLA_HEADS // 2):
            st = st_scr[p]
            sout_ref[0, 2 * p] = st[:GLA_DV, :GLA_DK].T
            sout_ref[0, 2 * p + 1] = st[GLA_DV:, GLA_DK:].T


def _gla_select_matrix():
    j = jnp.arange(GLA_CHUNK)[:, None, None]
    hl = jnp.arange(2)[None, :, None]
    rows_j = jnp.broadcast_to(j, (GLA_CHUNK, 2, GLA_DK)).reshape(-1)
    rows_h = jnp.broadcast_to(hl, (GLA_CHUNK, 2, GLA_DK)).reshape(-1)
    cols = jnp.arange(256)
    sel = (rows_j[:, None] == cols[None, :] % GLA_CHUNK) & (rows_h[:, None] == cols[None, :] // 128)
    return sel.astype(BF16)


def _gla_seq(z3, s0, gl, tb):
    b, l, _ = z3.shape
    const = lambda shape: pl.BlockSpec(shape, lambda i, j: (0,) * len(shape))
    return pl.pallas_call(
        _gla_seq_kernel,
        grid=(b, l // tb),
        in_specs=[
            pl.BlockSpec((1, tb, GLA_PROJ_PAD), lambda i, j: (i, j, 0)),
            pl.BlockSpec((1, GLA_HEADS, GLA_DK, GLA_DV), lambda i, j: (i, 0, 0, 0)),
            const((GLA_LORA_PAD, GLA_QK)), const((1, GLA_QK)), const((1, GLA_DV)),
            const((GLA_CHUNK * 128, 256)),
        ],
        out_specs=[
            pl.BlockSpec((1, tb, GLA_WIDTH), lambda i, j: (i, j, 0)),
            pl.BlockSpec((1, GLA_HEADS, GLA_DK, GLA_DV), lambda i, j: (i, 0, 0, 0)),
        ],
        out_shape=[
            jax.ShapeDtypeStruct((b, l, GLA_WIDTH), BF16),
            jax.ShapeDtypeStruct((b, GLA_HEADS, GLA_DK, GLA_DV), F32),
        ],
        scratch_shapes=[
            pltpu.VMEM((GLA_HEADS // 2, 2 * GLA_DV, 2 * GLA_DK), F32),
            pltpu.VMEM((tb, GLA_CHUNK * 128), BF16),
            pltpu.VMEM((tb, GLA_QK), F32), pltpu.VMEM((tb, GLA_WIDTH), F32),
        ],
        compiler_params=_params(("parallel", "arbitrary")),
    )(z3, s0, gl["gkw"], gl["gkb"], gl["norm_w"], _gla_select_matrix())


def _gla_step_prep_kernel(z_ref, gkw_ref, gkb_ref, q_ref, k_ref, v_ref, g_ref, dec_ref):
    q, k, v, g, gk = _gla_features(z_ref[...], gkw_ref[...], gkb_ref[...])
    q_ref[...] = q
    k_ref[...] = k
    v_ref[...] = v
    g_ref[...] = g
    dec_ref[...] = jnp.exp(gk)


def _gla_step_prep(z, gl):
    n = z.shape[0]
    qk = jax.ShapeDtypeStruct((n, GLA_QK), F32)
    wide = jax.ShapeDtypeStruct((n, GLA_WIDTH), F32)
    return pl.pallas_call(
        _gla_step_prep_kernel,
        out_shape=[qk, qk, wide, wide, qk],
        compiler_params=pltpu.CompilerParams(vmem_limit_bytes=VMEM_LIMIT),
    )(z, gl["gkw"], gl["gkb"])


def _gla_step_kernel(s_ref, q_ref, k_ref, dec_ref, v_ref, snew_ref, o_ref):
    bb = s_ref.shape[0]
    rows = lax.broadcasted_iota(jnp.int32, (bb, bb * GLA_DK), 0)
    cols = lax.broadcasted_iota(jnp.int32, (bb, bb * GLA_DK), 1)
    own = rows == cols // GLA_DK
    ones = jnp.ones((bb, GLA_DV), F32)
    zeros = jnp.zeros((bb, GLA_DV), F32)

    def seq_diag(x):
        return jnp.where(own, jnp.concatenate([x] * bb, axis=1), 0.0)

    for h in range(GLA_HEADS):
        ks = slice(h * GLA_DK, (h + 1) * GLA_DK)
        vs = slice(h * GLA_DV, (h + 1) * GLA_DV)
        s = s_ref[:, h].reshape(bb * GLA_DK, GLA_DV)
        v = v_ref[:, vs]
        lhs_t = jnp.concatenate([seq_diag(k_ref[:, ks]), seq_diag(dec_ref[:, ks])], axis=0)
        rhs = jnp.concatenate([jnp.concatenate([v, zeros], axis=1), jnp.concatenate([zeros, ones], axis=1)], axis=0)
        both = _dot_tn(lhs_t, rhs, precision=HIGHEST)
        s_new = s * both[:, GLA_DV:] + both[:, :GLA_DV]
        snew_ref[:, h] = s_new.reshape(bb, GLA_DK, GLA_DV)
        o_ref[:, vs] = _dot(seq_diag(q_ref[:, ks]), s_new)


def _gla_step(s0, q, k, dec, v, bb):
    n = s0.shape[0]
    s_spec = pl.BlockSpec((bb, GLA_HEADS, GLA_DK, GLA_DV), lambda i: (i, 0, 0, 0))
    qk_spec = pl.BlockSpec((bb, GLA_QK), lambda i: (i, 0))
    v_spec = pl.BlockSpec((bb, GLA_WIDTH), lambda i: (i, 0))
    return pl.pallas_call(
        _gla_step_kernel,
        grid=(n // bb,),
        in_specs=[s_spec, qk_spec, qk_spec, qk_spec, v_spec],
        out_specs=[s_spec, v_spec],
        out_shape=[jax.ShapeDtypeStruct(s0.shape, F32), jax.ShapeDtypeStruct((n, GLA_WIDTH), F32)],
        compiler_params=_params(("parallel",)),
    )(s0, q, k, dec, v)


def _step_post_kernel(orw_ref, bon_ref, grw_ref, lnw_ref, lnb_ref, ogl_ref, ggl_ref, nw_ref, o_rw_ref, o_gl_ref):
    o = orw_ref[...]
    for h in range(RW_HEADS):
        hs = slice(h * RW_N, (h + 1) * RW_N)
        o_h = o[:, hs]
        mean = jnp.mean(o_h, axis=-1, keepdims=True)
        cen = o_h - mean
        var = jnp.mean(cen * cen, axis=-1, keepdims=True)
        on = cen * lax.rsqrt(var + RW_GN_EPS)
        res = (on * lnw_ref[:, hs] + lnb_ref[:, hs] + bon_ref[:, hs]) * grw_ref[:, hs]
        o_rw_ref[:, hs] = res.astype(o_rw_ref.dtype)
    o_gl_ref[...] = _gla_finish(ogl_ref[...], ggl_ref[...], nw_ref[...]).astype(o_gl_ref.dtype)


def _step_post(o_rw, bonus, g_rw, rw, o_gl, g_gl, gl):
    n = o_rw.shape[0]
    return pl.pallas_call(
        _step_post_kernel,
        out_shape=[jax.ShapeDtypeStruct((n, RW_WIDTH), BF16), jax.ShapeDtypeStruct((n, GLA_WIDTH), BF16)],
        compiler_params=pltpu.CompilerParams(vmem_limit_bytes=VMEM_LIMIT),
    )(o_rw, bonus, g_rw, rw["ln_w"], rw["ln_b"], o_gl, g_gl, gl["norm_w"])


def _outproj_router_kernel(x_ref, orw_ref, ogl_ref, wo_ref, gain_ref, wrt_ref, br_ref,
                           h_ref, xn_ref, idx_ref, gate_ref, rank_ref, cnt_ref):
    tm = x_ref.shape[0]
    mix = jnp.concatenate([orw_ref[...], ogl_ref[...]], axis=-1)
    h = x_ref[...] + _dot(mix, wo_ref[...])
    h_ref[...] = h
    xn = h * lax.rsqrt(jnp.mean(h * h, axis=-1, keepdims=True) + NORM_EPS) * gain_ref[...]
    _store_row_tiles(xn_ref, xn)
    logits = _dot_nt(wrt_ref[...], xn) + br_ref[...]
    eidx = lax.broadcasted_iota(jnp.int32, logits.shape, 0)
    ti = lax.broadcasted_iota(jnp.int32, (tm, tm), 0)
    tj = lax.broadcasted_iota(jnp.int32, (tm, tm), 1)
    before = (ti < tj).astype(BF16)
    vals, idxs = [], []
    work = logits
    chosen = jnp.zeros(logits.shape, F32)
    for _ in range(TOP_K):
        m = jnp.max(work, axis=0, keepdims=True)
        sel = jnp.min(jnp.where(work == m, eidx, N_EXPERTS), axis=0, keepdims=True)
        hit = eidx == sel
        work = jnp.where(hit, -jnp.inf, work)
        chosen = chosen + hit.astype(F32)
        vals.append(m)
        idxs.append(sel)
    prefix = _dot(chosen.astype(BF16), before)
    exps = [jnp.exp(v - vals[0]) for v in vals]
    denom = exps[0] + exps[1] + exps[2] + exps[3]
    for j in range(TOP_K):
        idx_ref[0, j : j + 1, :] = idxs[j]
        gate_ref[0, j : j + 1, :] = exps[j] / denom
        rank = jnp.sum(jnp.where(eidx == idxs[j], prefix, 0.0), axis=0, keepdims=True)
        rank_ref[0, j : j + 1, :] = rank.astype(jnp.int32)
    cnt = jnp.sum(chosen, axis=1, keepdims=True)
    cnt_ref[0] = jnp.broadcast_to(cnt, (N_EXPERTS, 128)).astype(jnp.int32)


def _outproj_router(x, o_rw, o_gl, w_out, gain, w_router_t, b_router, tm):
    n = x.shape[0]
    nt = n // tm
    const = lambda shape: pl.BlockSpec(shape, lambda i: (0,) * len(shape))
    tok = lambda width: pl.BlockSpec((tm, width), lambda i: (i, 0))
    lane = pl.BlockSpec((1, TOP_K, tm), lambda i: (i, 0, 0))
    return pl.pallas_call(
        _outproj_router_kernel,
        grid=(nt,),
        in_specs=[
            tok(D_MODEL), tok(RW_WIDTH), tok(GLA_WIDTH),
            const((D_MODEL, D_MODEL)), const((1, D_MODEL)), const((N_EXPERTS, D_MODEL)), const((N_EXPERTS, 1)),
        ],
        out_specs=[tok(D_MODEL), pl.BlockSpec((tm * ROW_TILE, 128), lambda i: (i, 0)), lane, lane, lane,
                   pl.BlockSpec((1, N_EXPERTS, 128), lambda i: (i, 0, 0))],
        out_shape=[
            jax.ShapeDtypeStruct((n, D_MODEL), F32),
            jax.ShapeDtypeStruct((n * ROW_TILE, 128), F32),
            jax.ShapeDtypeStruct((nt, TOP_K, tm), jnp.int32),
            jax.ShapeDtypeStruct((nt, TOP_K, tm), F32),
            jax.ShapeDtypeStruct((nt, TOP_K, tm), jnp.int32),
            jax.ShapeDtypeStruct((nt, N_EXPERTS, 128), jnp.int32),
        ],
        compiler_params=_params(("parallel",)),
    )(x, o_rw, o_gl, w_out, gain, w_router_t, b_router)


def _moe_kernel(be_ref, nu_ref, epoch_ref, next_ref, rows_ref, parts_ref, xs_ref, wg_hbm, wu_hbm, wd_hbm, bg_ref, bu_ref, bd_ref,
                y_ref, w_f32, wg_b, wu_b, wd_b, sems):
    b = pl.program_id(0)
    prev = be_ref[jnp.maximum(b - 1, 0)]
    new_expert = (b == 0) | (be_ref[b] != prev)

    def fetch(e, slot, i):
        w = (wg_hbm, wu_hbm, wd_hbm)[i]
        return pltpu.make_async_copy(w.at[e], w_f32.at[slot, i], sems.at[slot])

    @pl.when(b == 0)
    def _():
        for i in range(3):
            fetch(be_ref[0], 0, i).start()

    @pl.when(new_expert)
    def _():
        slot = epoch_ref[b] % 2
        for i in range(3):
            fetch(be_ref[b], slot, i).wait()
        wg_b[...] = w_f32[slot, 0].astype(BF16)
        wu_b[...] = w_f32[slot, 1].astype(BF16)
        wd_b[...] = w_f32[slot, 2].astype(BF16)

    first_part, end_part = parts_ref[b] // 4, parts_ref[b] % 4
    for i in range(3):
        @pl.when((next_ref[b] >= 0) & (first_part <= i) & (i < end_part))
        def _():
            fetch(next_ref[b], 1 - epoch_ref[b] % 2, i).start()

    def ffn(m):
        x = _load_row_tiles(xs_ref, m).astype(BF16)
        half = D_MODEL // 2
        acc = None
        for f in range(2):
            fs = slice(f * half, (f + 1) * half)
            gt = _dot(x, wg_b[:, fs]) + bg_ref[0, :, fs]
            up = _dot(x, wu_b[:, fs]) + bu_ref[0, :, fs]
            gt = jnp.minimum(gt, SWIGLU_LIMIT)
            up = jnp.clip(up, -SWIGLU_LIMIT, SWIGLU_LIMIT)
            hid = (up + 1.0) * gt * _sigmoid(SWIGLU_ALPHA * gt)
            part = _dot(hid.astype(BF16), wd_b[fs, :])
            acc = part if acc is None else acc + part
        _store_row_tiles(y_ref, acc + bd_ref[0])

    quarter = MOE_BLOCK // 4
    quarters = (rows_ref[b] + quarter - 1) // quarter
    for q in range(1, 5):
        @pl.when((b < nu_ref[0]) & (quarters == q))
        def _():
            ffn(q * quarter)


def _moe_ffn(block_expert, n_used, block_rows, xs, w_gate, w_up, w_down, b_gate, b_up, b_down):
    n_blocks = block_expert.shape[0]
    pos = jnp.arange(n_blocks, dtype=jnp.int32)
    change = (pos > 0) & (block_expert != jnp.roll(block_expert, 1))
    epoch = jnp.cumsum(change.astype(jnp.int32))
    later = change[None, :] & (pos[None, :] > pos[:, None])
    first = jnp.min(jnp.where(later, pos[None, :], n_blocks), axis=1)
    next_e = jnp.sum(jnp.where(pos[None, :] == first[:, None], block_expert[None, :], 0), axis=1)
    next_e = jnp.where(first < n_blocks, next_e, -1).astype(jnp.int32)
    run_start = jnp.max(jnp.where((pos[None, :] <= pos[:, None]) & (change | (pos == 0))[None, :], pos[None, :], 0),
                        axis=1)
    q = jnp.minimum(pos - run_start, 3)
    used = pos < n_used[0]
    is_last = jnp.roll(change, -1) | (pos == n_used[0] - 1)
    parts = jnp.where(used, q * 4 + jnp.where(is_last, 3, jnp.minimum(q + 1, 3)), 15).astype(jnp.int32)

    row = lambda b, be, nu, ep, nx, br, pt: (jnp.minimum(b, nu[0] - 1), 0)
    bspec = pl.BlockSpec((1, 1, D_MODEL), lambda b, be, nu, ep, nx, br, pt: (be[b], 0, 0))
    wspec = pl.BlockSpec(memory_space=pl.ANY)
    grid_spec = pltpu.PrefetchScalarGridSpec(
        num_scalar_prefetch=6,
        grid=(n_blocks,),
        in_specs=[pl.BlockSpec((MOE_BLOCK * ROW_TILE, 128), row), wspec, wspec, wspec, bspec, bspec, bspec],
        out_specs=pl.BlockSpec((MOE_BLOCK * ROW_TILE, 128), row),
        scratch_shapes=[pltpu.VMEM((2, 3, D_MODEL, D_MODEL), F32)] + [pltpu.VMEM((D_MODEL, D_MODEL), BF16)] * 3
        + [pltpu.SemaphoreType.DMA((2,))],
    )
    return pl.pallas_call(
        _moe_kernel,
        grid_spec=grid_spec,
        out_shape=jax.ShapeDtypeStruct((n_blocks * MOE_BLOCK * ROW_TILE, 128), F32),
        compiler_params=_params(("arbitrary",)),
    )(block_expert, n_used, epoch, next_e, block_rows, parts, xs, w_gate, w_up, w_down,
      b_gate.reshape(N_EXPERTS, 1, D_MODEL), b_up.reshape(N_EXPERTS, 1, D_MODEL),
      b_down.reshape(N_EXPERTS, 1, D_MODEL))


SEG_ALIGN = 8
GROUP_ROWS = SEG_ALIGN * ROW_TILE


def _local_rows(tm):
    return tm * TOP_K + N_EXPERTS * SEG_ALIGN


BIG_COPY = 4


def _copy_tables(tm):
    n_big = _local_rows(tm) // SEG_ALIGN // BIG_COPY
    n_small = N_EXPERTS * (BIG_COPY - 1)
    return 2 * n_big, 2 * (n_big + n_small)


def _issue_group_copies(cp_ref, hbm, buf, sem, tm, to_hbm):
    small_at, counts_at = _copy_tables(tm)

    def copy(at, n_groups):
        rows = n_groups * GROUP_ROWS
        b = buf.at[pl.ds(pl.multiple_of(cp_ref[0, 0, at] * GROUP_ROWS, GROUP_ROWS), rows)]
        h = hbm.at[pl.ds(pl.multiple_of(cp_ref[0, 0, at + 1] * GROUP_ROWS, GROUP_ROWS), rows)]
        return pltpu.make_async_copy(b, h, sem) if to_hbm else pltpu.make_async_copy(h, b, sem)

    def big(i, carry):
        copy(2 * i, BIG_COPY).start()
        return carry

    def small(i, carry):
        copy(small_at + 2 * i, 1).start()
        return carry

    lax.fori_loop(0, cp_ref[0, 0, counts_at], big, 0)
    lax.fori_loop(0, cp_ref[0, 0, counts_at + 1], small, 0)


def _wait_group_copies(cp_ref, hbm, buf, sem, tm, to_hbm):
    rows = pl.ds(0, pl.multiple_of(cp_ref[0, 0, _copy_tables(tm)[1] + 2] * GROUP_ROWS, GROUP_ROWS))
    b, h = buf.at[rows], hbm.at[rows]
    (pltpu.make_async_copy(b, h, sem) if to_hbm else pltpu.make_async_copy(h, b, sem)).wait()


def _combine_kernel(grp_c, grp_n, lpos_ref, gate_ref, h_ref, gain_ref, y_hbm, o_ref, ybuf, fbuf, sems):
    i = pl.program_id(0)
    nt = pl.num_programs(0)
    tm = h_ref.shape[0]

    @pl.when(i == 0)
    def _():
        _issue_group_copies(grp_c, y_hbm, ybuf.at[0], sems.at[0], tm, False)

    @pl.when(i + 1 < nt)
    def _():
        _issue_group_copies(grp_n, y_hbm, ybuf.at[(i + 1) % 2], sems.at[(i + 1) % 2], tm, False)

    slot = i % 2
    yb = ybuf.at[slot]
    _wait_group_copies(grp_c, y_hbm, yb, sems.at[slot], tm, False)

    def token_body(t, carry):
        acc = None
        for j in range(TOP_K):
            row = pl.multiple_of(lpos_ref[0, 0, t * TOP_K + j], ROW_TILE)
            term = gate_ref[0, 0, t * TOP_K + j] * yb[pl.ds(row, ROW_TILE), :]
            acc = term if acc is None else acc + term
        fbuf[pl.ds(pl.multiple_of(t * ROW_TILE, ROW_TILE), ROW_TILE), :] = acc
        return carry

    lax.fori_loop(0, tm, token_body, 0, unroll=8)
    f = h_ref[...] + _load_row_tiles(fbuf, tm)
    o_ref[...] = f * lax.rsqrt(jnp.mean(f * f, axis=-1, keepdims=True) + NORM_EPS) * gain_ref[...]


def _combine(h, y_rows, grp3, lpos3, gate3, gain, tm):
    n = h.shape[0]
    nt = n // tm
    n_local = _local_rows(tm)
    gw = grp3.shape[-1]
    smem = lambda shape, imap: pl.BlockSpec(shape, imap, memory_space=pltpu.SMEM)
    cur = lambda i: (i, 0, 0)
    nxt = lambda i: (jnp.minimum(i + 1, nt - 1), 0, 0)
    return pl.pallas_call(
        _combine_kernel,
        grid=(nt,),
        in_specs=[
            smem((1, 1, gw), cur), smem((1, 1, gw), nxt),
            smem((1, 1, TOP_K * tm), cur), smem((1, 1, TOP_K * tm), cur),
            pl.BlockSpec((tm, D_MODEL), lambda i: (i, 0)),
            pl.BlockSpec((1, D_MODEL), lambda i: (0, 0)),
            pl.BlockSpec(memory_space=pl.ANY),
        ],
        out_specs=pl.BlockSpec((tm, D_MODEL), lambda i: (i, 0)),
        out_shape=jax.ShapeDtypeStruct((n, D_MODEL), F32),
        scratch_shapes=[pltpu.VMEM((2, n_local * ROW_TILE, 128), F32), pltpu.VMEM((tm * ROW_TILE, 128), F32),
                        pltpu.SemaphoreType.DMA((2,))],
        compiler_params=pltpu.CompilerParams(dimension_semantics=("arbitrary",), vmem_limit_bytes=VMEM_LIMIT,
                                             disable_bounds_checks=True),
    )(grp3, grp3, lpos3, gate3, h, gain, y_rows)


def _dispatch_kernel(*refs, fill):
    if fill:
        grp_ref, grp_prev, lpos_ref, ends_ref, x_ref, xs_hbm, sorted_buf, zero_scr, sems, zsem = refs
    else:
        grp_ref, grp_prev, lpos_ref, x_ref, _, xs_hbm, sorted_buf, sems = refs
    i = pl.program_id(0)
    tm = x_ref.shape[0] // ROW_TILE
    blk = MOE_BLOCK * ROW_TILE
    sorted_scr = sorted_buf.at[i % 2]
    sem = sems.at[i % 2]

    if fill:
        def fill_copy(e):
            start = pl.multiple_of((ends_ref[0, e] - MOE_BLOCK) * ROW_TILE, blk)
            return pltpu.make_async_copy(zero_scr, xs_hbm.at[pl.ds(start, blk)], zsem)

        @pl.when(i == 0)
        def _():
            zero_scr[...] = jnp.zeros(zero_scr.shape, zero_scr.dtype)
            for e in range(N_EXPERTS):
                @pl.when(ends_ref[1, e] > 0)
                def _():
                    fill_copy(e).start()

    sorted_scr[...] = jnp.zeros(sorted_scr.shape, sorted_scr.dtype)

    def move(t, carry):
        row = x_ref[pl.ds(pl.multiple_of(t * ROW_TILE, ROW_TILE), ROW_TILE), :]
        for j in range(TOP_K):
            dst = pl.multiple_of(lpos_ref[0, 0, t * TOP_K + j], ROW_TILE)
            sorted_scr[pl.ds(dst, ROW_TILE), :] = row
        return carry

    lax.fori_loop(0, tm, move, 0, unroll=8)

    if fill:
        @pl.when(i == 0)
        def _():
            for e in range(N_EXPERTS):
                @pl.when(ends_ref[1, e] > 0)
                def _():
                    fill_copy(e).wait()

    @pl.when(i > 0)
    def _():
        _wait_group_copies(grp_prev, xs_hbm, sorted_buf.at[(i + 1) % 2], sems.at[(i + 1) % 2], tm, True)

    _issue_group_copies(grp_ref, xs_hbm, sorted_scr, sem, tm, True)

    @pl.when(i == pl.num_programs(0) - 1)
    def _():
        _wait_group_copies(grp_ref, xs_hbm, sorted_scr, sem, tm, True)


def _dispatch(xn, grp3, lpos3, tm, n_slots, ends=None, xs=None):
    n = xn.shape[0] // ROW_TILE
    fill = xs is None
    smem = lambda shape, imap: pl.BlockSpec(shape, imap, memory_space=pltpu.SMEM)
    in_specs = [smem((1, 1, grp3.shape[-1]), lambda i: (i, 0, 0)),
                smem((1, 1, grp3.shape[-1]), lambda i: (jnp.maximum(i - 1, 0), 0, 0)),
                smem((1, 1, TOP_K * tm), lambda i: (i, 0, 0))]
    args = [grp3, grp3, lpos3]
    scratch = [pltpu.VMEM((2, _local_rows(tm) * ROW_TILE, 128), F32)]
    if fill:
        in_specs.append(smem((2, N_EXPERTS), lambda i: (0, 0)))
        args.append(ends)
        scratch.append(pltpu.VMEM((MOE_BLOCK * ROW_TILE, 128), F32))
    in_specs.append(pl.BlockSpec((tm * ROW_TILE, 128), lambda i: (i, 0)))
    args.append(xn)
    aliases = {}
    if not fill:
        in_specs.append(pl.BlockSpec(memory_space=pl.ANY))
        args.append(xs)
        aliases = {len(args) - 1: 0}
    scratch.append(pltpu.SemaphoreType.DMA((2,)))
    if fill:
        scratch.append(pltpu.SemaphoreType.DMA(()))
    return pl.pallas_call(
        functools.partial(_dispatch_kernel, fill=fill),
        grid=(n // tm,),
        in_specs=in_specs,
        out_specs=pl.BlockSpec(memory_space=pl.ANY),
        out_shape=jax.ShapeDtypeStruct((n_slots * ROW_TILE, 128), F32),
        scratch_shapes=scratch,
        input_output_aliases=aliases,
        compiler_params=pltpu.CompilerParams(dimension_semantics=("arbitrary",), vmem_limit_bytes=VMEM_LIMIT,
                                             disable_bounds_checks=True, has_side_effects=True),
    )(*args)


def _pad_rows(w, rows, offset):
    out = jnp.zeros((rows, w.shape[1]), w.dtype)
    return out.at[offset : offset + w.shape[0]].set(w)


def _routing_tables(counts, n_pairs):
    n_tiles = counts.shape[0]
    n_blocks = (n_pairs + n_tiles * N_EXPERTS * (SEG_ALIGN - 1) + N_EXPERTS * (MOE_BLOCK - 1)
                + MOE_BLOCK - 1) // MOE_BLOCK
    runs = (counts + SEG_ALIGN - 1) // SEG_ALIGN * SEG_ALIGN
    local_start = jnp.cumsum(runs, axis=1) - runs
    total = jnp.sum(runs, axis=0)
    padded = (total + MOE_BLOCK - 1) // MOE_BLOCK * MOE_BLOCK
    pends = jnp.cumsum(padded)
    pstarts = pends - padded
    global_start = pstarts[None, :] + jnp.cumsum(runs, axis=0) - runs
    blocks = jnp.arange(n_blocks, dtype=jnp.int32) * MOE_BLOCK
    n_used = (pends[-1] // MOE_BLOCK).astype(jnp.int32)
    owner = jnp.sum((pends[None, :] <= blocks[:, None]).astype(jnp.int32), axis=1)
    block_expert = jnp.minimum(owner, N_EXPERTS - 1)
    last = jnp.sum(jnp.where(jnp.arange(n_blocks) == n_used - 1, block_expert, 0))
    block_expert = jnp.where(jnp.arange(n_blocks) < n_used, block_expert, last)
    ends = jnp.stack([pends, padded]).astype(jnp.int32)
    row_end = jnp.sum(jnp.where(block_expert[:, None] == jnp.arange(N_EXPERTS), (pstarts + total)[None, :], 0), axis=1)
    block_rows = jnp.clip(row_end - blocks, 0, MOE_BLOCK).astype(jnp.int32)
    return (runs, local_start, global_start, ends, block_expert.astype(jnp.int32), n_used.reshape(1), block_rows,
            n_blocks)


def _tile_tables(idx3, rank3, runs, local_start, global_start, tm):
    experts = jnp.arange(N_EXPERTS, dtype=jnp.int32)
    hit = idx3[..., None] == experts
    lpos = rank3 + jnp.sum(jnp.where(hit, local_start[:, None, None, :], 0), axis=-1)
    lpos_rows = _token_major(lpos.astype(jnp.int32) * ROW_TILE)

    groups = runs // SEG_ALIGN
    l8, g8 = local_start // SEG_ALIGN, global_start // SEG_ALIGN
    n_big, n_small = groups // BIG_COPY, groups % BIG_COPY

    def copy_list(count, first_local, first_global, step, length):
        ends = jnp.cumsum(count, axis=1)
        o = jnp.arange(length, dtype=jnp.int32)
        owner = jnp.minimum(jnp.sum((ends[:, None, :] <= o[None, :, None]).astype(jnp.int32), axis=-1), N_EXPERTS - 1)
        sel = owner[..., None] == experts
        pick = lambda t: jnp.sum(jnp.where(sel, t[:, None, :], 0), axis=-1)
        k = (o[None, :] - pick(ends - count)) * step
        pairs = jnp.stack([pick(first_local) + k, pick(first_global) + k], axis=-1)
        live = (o[None, :] < ends[:, -1:])[..., None]
        return jnp.where(live, pairs, 0).reshape(count.shape[0], 2 * length)

    n_groups = _local_rows(tm) // SEG_ALIGN
    big = copy_list(n_big, l8, g8, BIG_COPY, n_groups // BIG_COPY)
    small = copy_list(n_small, l8 + n_big * BIG_COPY, g8 + n_big * BIG_COPY, 1, N_EXPERTS * (BIG_COPY - 1))
    counts = jnp.stack([jnp.sum(n_big, axis=1), jnp.sum(n_small, axis=1), jnp.sum(groups, axis=1)], axis=1)
    counts = jnp.pad(counts, ((0, 0), (0, 5)))
    return lpos_rows, jnp.concatenate([big, small, counts], axis=1).astype(jnp.int32)[:, None, :]


def _token_major(t3):
    return jnp.swapaxes(t3, 1, 2).reshape(t3.shape[0], 1, -1)


def kernel(x_prompt, x_sample, state_rwkv_shift, state_rwkv_wkv, state_gla, norm_mix, w_in, rw_mu, rw_w0, rw_w2, rw_a0, rw_a2, rw_g2, rw_k_k, rw_k_a, rw_r_k, rw_ln_w, rw_ln_b, gla_gk_w2, gla_gk_b, gla_norm_w, w_out, norm_ffn, w_router, b_router, w_gate, b_gate, w_up, b_up, w_down, b_down, norm_final):
    depth = norm_mix.shape[0]
    assert depth == 1
    bp, lp, d = x_prompt.shape
    bs, ls, _ = x_sample.shape
    assert ls == 1 and lp % SEQ_BLOCK == 0
    l = 0
    row = lambda t: t.reshape(1, -1)

    w_in_b = w_in[l].astype(BF16)
    w_in_r = w_in_b[:, :RW_PROJ]
    w_in_g = jnp.pad(w_in_b[:, RW_PROJ:], ((0, 0), (0, GLA_PROJ_PAD - GLA_PROJ)))
    rw = dict(
        mu=row(rw_mu[l]), w0=row(rw_w0[l]), a0=row(rw_a0[l]),
        w2p=_pad_rows(rw_w2[l].astype(BF16), 128, 0), a2p=_pad_rows(rw_a2[l].astype(BF16), 128, 64),
        g2=rw_g2[l].astype(BF16), k_k=row(rw_k_k[l]), k_a=row(rw_k_a[l]), r_k=row(rw_r_k[l]),
        ln_w=row(rw_ln_w[l]), ln_b=row(rw_ln_b[l]))
    gl = dict(gkw=_pad_rows(gla_gk_w2[l].astype(BF16), GLA_LORA_PAD, 0), gkb=row(gla_gk_b[l]),
              norm_w=row(gla_norm_w[l]))
    gain_mix = row(norm_mix[l])

    n_p = bp * lp
    xp = x_prompt.reshape(n_p, d)
    zr_p, zg_p = _inproj(xp, gain_mix, w_in_r, w_in_g, TOK_BLOCK)
    zr_p3 = zr_p.reshape(bp, lp, RW_PROJ)
    o_rw_p, wkv_p = _rwkv_seq(zr_p3, jnp.zeros((bp, 1, RW_PROJ), F32),
                              jnp.zeros((bp, RW_HEADS, RW_N, RW_N), F32), rw, SEQ_BLOCK)
    o_gl_p, gla_p = _gla_seq(zg_p.reshape(bp, lp, GLA_PROJ_PAD),
                             jnp.zeros((bp, GLA_HEADS, GLA_DK, GLA_DV), F32), gl, SEQ_BLOCK)
    shift_p = zr_p3[:, -1, :]

    xs_ = x_sample.reshape(bs, d)
    zr_s, zg_s = _inproj(xs_, gain_mix, w_in_r, w_in_g, bs)
    r, k, v, al, be, dec, g_rw, bonus = _rwkv_step_prep(zr_s, state_rwkv_shift[l], rw)
    wkv_s, o_rw_s = _rwkv_step(state_rwkv_wkv[l], r, k, al, be, dec, v)
    q, kg, vg, g_gl, dec_g = _gla_step_prep(zg_s, gl)
    gla_s, o_gl_s = _gla_step(state_gla[l], q, kg, dec_g, vg, 16)
    o_rw_s2, o_gl_s2 = _step_post(o_rw_s, bonus, g_rw, rw, o_gl_s, g_gl, gl)
    shift_s = zr_s

    w_out_b = w_out[l].astype(BF16)
    router = (w_out_b, row(norm_ffn[l]), w_router[l].T, b_router[l].reshape(N_EXPERTS, 1))
    h_p, xn_p, idx_p, gate_p, rank_p, cnt_p = _outproj_router(
        xp, o_rw_p.reshape(n_p, RW_WIDTH), o_gl_p.reshape(n_p, GLA_WIDTH), *router, TOK_BLOCK)
    h_s, xn_s, idx_s, gate_s, rank_s, cnt_s = _outproj_router(xs_, o_rw_s2, o_gl_s2, *router, bs)
    nt_p = n_p // TOK_BLOCK
    counts = jnp.concatenate([cnt_p[:, :, 0], cnt_s[:, :, 0]], axis=0)
    runs, lstart, gstart, ends, block_expert, n_used, block_rows, n_blocks = _routing_tables(
        counts, (n_p + bs) * TOP_K)
    n_slots = n_blocks * MOE_BLOCK
    lpos_p, grp_p = _tile_tables(idx_p, rank_p, runs[:nt_p], lstart[:nt_p], gstart[:nt_p], TOK_BLOCK)
    lpos_s, grp_s = _tile_tables(idx_s, rank_s, runs[nt_p:], lstart[nt_p:], gstart[nt_p:], bs)
    xs_rows = _dispatch(xn_p, grp_p, lpos_p, TOK_BLOCK, n_slots, ends=ends)
    xs_rows = _dispatch(xn_s, grp_s, lpos_s, bs, n_slots, xs=xs_rows)
    y_rows = _moe_ffn(block_expert, n_used, block_rows, xs_rows, w_gate[l], w_up[l], w_down[l], b_gate[l], b_up[l], b_down[l])
    gain_f = row(norm_final)
    y_p = _combine(h_p, y_rows, grp_p, lpos_p, _token_major(gate_p), gain_f, TOK_BLOCK)
    y_s = _combine(h_s, y_rows, grp_s, lpos_s, _token_major(gate_s), gain_f, bs)

    y_prompt = y_p.reshape(bp, lp, d)
    y_sample = y_s.reshape(bs, ls, d)
    return (y_prompt, y_sample, shift_p[None], wkv_p[None], gla_p[None], shift_s[None], wkv_s[None], gla_s[None])
```

```python
import functools

import jax
import jax.numpy as jnp
from jax import lax
from jax.experimental import pallas as pl
from jax.experimental.pallas import tpu as pltpu

F32 = jnp.float32
BF16 = jnp.bfloat16
HIGHEST = lax.Precision.HIGHEST

D_MODEL = 1024
RW_WIDTH = 512
RW_HEADS = 8
RW_N = 64
RW_PROJ = 1792
RW_GN_EPS = 64e-5
GLA_HEADS = 4
GLA_DK = 64
GLA_DV = 128
GLA_WIDTH = 512
GLA_QK = GLA_HEADS * GLA_DK
GLA_PROJ = 1552
GLA_PROJ_PAD = 1664
GLA_LORA_PAD = 128
GLA_GATE_NORMALIZER = 16.0
N_EXPERTS = 32
TOP_K = 4
SWIGLU_LIMIT = 7.0
SWIGLU_ALPHA = 1.702
NORM_EPS = 1e-5
LOG2_E = 1.4426950408889634

RW_CHUNK = 64
GLA_CHUNK = 16
SEQ_BLOCK = 512
TOK_BLOCK = 512
MOE_BLOCK = 512
VMEM_LIMIT = 56 * 1024 * 1024


def _dot(a, b, precision=None):
    return jnp.dot(a, b, preferred_element_type=F32, precision=precision)


def _dot_nt(a, b, precision=None):
    return lax.dot_general(a, b, (((1,), (1,)), ((), ())), preferred_element_type=F32, precision=precision)


def _dot_tn(a, b, precision=None):
    return lax.dot_general(a, b, (((0,), (0,)), ((), ())), preferred_element_type=F32, precision=precision)


def _sigmoid(x):
    return 1.0 / (1.0 + jnp.exp(-x))


def _softplus(x):
    return jnp.maximum(x, 0.0) + jnp.log(1.0 + jnp.exp(-jnp.abs(x)))


def _params(sem):
    return pltpu.CompilerParams(dimension_semantics=sem, vmem_limit_bytes=VMEM_LIMIT)


ROW_TILE = D_MODEL // 128


def _store_row_tiles(ref, x):
    m = x.shape[0]
    for c in range(ROW_TILE):
        ref[pl.ds(c, m, stride=ROW_TILE), :] = x[:, c * 128 : (c + 1) * 128]


def _load_row_tiles(ref, m):
    return jnp.concatenate([ref[pl.ds(c, m, stride=ROW_TILE), :] for c in range(ROW_TILE)], axis=-1)


def _inproj_kernel(x_ref, gain_ref, wr_ref, wg_ref, zr_ref, zg_ref):
    x = x_ref[...]
    xn = x * lax.rsqrt(jnp.mean(x * x, axis=-1, keepdims=True) + NORM_EPS) * gain_ref[...]
    xb = xn.astype(BF16)
    zr_ref[...] = _dot(xb, wr_ref[...])
    zg_ref[...] = _dot(xb, wg_ref[...])


def _inproj(x, gain, w_r, w_g, tm):
    n = x.shape[0]
    return pl.pallas_call(
        _inproj_kernel,
        grid=(n // tm,),
        in_specs=[
            pl.BlockSpec((tm, D_MODEL), lambda i: (i, 0)),
            pl.BlockSpec((1, D_MODEL), lambda i: (0, 0)),
            pl.BlockSpec((D_MODEL, RW_PROJ), lambda i: (0, 0)),
            pl.BlockSpec((D_MODEL, GLA_PROJ_PAD), lambda i: (0, 0)),
        ],
        out_specs=[
            pl.BlockSpec((tm, RW_PROJ), lambda i: (i, 0)),
            pl.BlockSpec((tm, GLA_PROJ_PAD), lambda i: (i, 0)),
        ],
        out_shape=[
            jax.ShapeDtypeStruct((n, RW_PROJ), F32),
            jax.ShapeDtypeStruct((n, GLA_PROJ_PAD), F32),
        ],
        compiler_params=_params(("parallel",)),
    )(x, gain, w_r, w_g)


def _rwkv_features(zs, w0, w2p, a0, a2p, g2, k_k, k_a):
    W = RW_WIDTH
    r = zs[:, 0:W]
    k_raw = zs[:, W : 2 * W]
    v = zs[:, 2 * W : 3 * W]
    zwa = zs[:, 3 * W : 3 * W + 128]
    zg = zs[:, 3 * W + 128 :]
    w = -_softplus(-(w0 + _dot(jnp.tanh(zwa).astype(BF16), w2p))) - 0.5
    log_decay = -jnp.exp(w)
    a = _sigmoid(a0 + _dot(zwa.astype(BF16), a2p))
    g = _dot(_sigmoid(zg).astype(BF16), g2)
    kk_raw = k_raw * k_k
    k = k_raw * (1.0 + (a - 1.0) * k_a)
    return r, k, v, kk_raw, a, log_decay, g


def _level_mask(ri, ci, lvl):
    same = (ri >> (lvl + 1)) == (ci >> (lvl + 1))
    return same & (((ri >> lvl) & 1) == 1) & (((ci >> lvl) & 1) == 0)


def _rwkv_seq_kernel(z_ref, shift0_ref, s0_ref, mu_ref, w0_ref, w2_ref, a0_ref, a2_ref, g2_ref, kk_ref, ka_ref,
                     rk_ref, lnw_ref, lnb_ref, o_ref, sout_ref,
                     m_scr, prev_scr, r_scr, k_scr, v_scr, kkr_scr, a_scr, lw_scr, on_scr, bon_scr):
    C = RW_CHUNK
    N = RW_N
    t_idx = pl.program_id(1)
    tb = z_ref.shape[1]
    zero_nn = jnp.zeros((N, N), F32)

    @pl.when(t_idx == 0)
    def _():
        prev_scr[...] = shift0_ref[0]
        for p in range(RW_HEADS // 2):
            top = jnp.concatenate([s0_ref[0, 2 * p].T, zero_nn], axis=1)
            bot = jnp.concatenate([zero_nn, s0_ref[0, 2 * p + 1].T], axis=1)
            m_scr[p] = jnp.concatenate([top, bot], axis=0)

    z = z_ref[0]
    row = lax.broadcasted_iota(jnp.int32, z.shape, 0)
    z_prev = jnp.where(row == 0, prev_scr[...], pltpu.roll(z, 1, axis=0))
    prev_scr[...] = z[tb - 1 : tb, :]
    zs = z + mu_ref[...] * (z_prev - z)
    r, k, v, kk_raw, a, log_decay, g = _rwkv_features(
        zs, w0_ref[...], w2_ref[...], a0_ref[...], a2_ref[...], g2_ref[...], kk_ref[...], ka_ref[...])
    P2 = 2 * N
    left1 = lax.broadcasted_iota(jnp.int32, (1, P2), 1) < N

    def head_sum(x):
        s0 = jnp.sum(jnp.where(left1, x, 0.0), axis=-1, keepdims=True)
        s1 = jnp.sum(jnp.where(left1, 0.0, x), axis=-1, keepdims=True)
        return jnp.where(left1, s0, s1)

    def head_sum_wide(x):
        return jnp.concatenate([head_sum(x[:, p * P2 : (p + 1) * P2]) for p in range(RW_HEADS // 2)], axis=1)

    alpha = kk_raw * lax.rsqrt(jnp.maximum(head_sum_wide(kk_raw * kk_raw), 1e-24))
    r_scr[...] = r
    k_scr[...] = k
    v_scr[...] = v
    kkr_scr[...] = alpha
    a_scr[...] = alpha * a
    lw_scr[...] = log_decay
    bon_scr[...] = head_sum_wide(r * k * rk_ref[...]) * v

    ri = lax.broadcasted_iota(jnp.int32, (C, P2), 0)
    ci = lax.broadcasted_iota(jnp.int32, (C, P2), 1) % N
    left = lax.broadcasted_iota(jnp.int32, (C, P2), 1) < N
    tril = ri >= ci
    stril = ri > ci
    eye_f = (ri == ci).astype(F32)
    rb = lax.broadcasted_iota(jnp.int32, (P2, P2), 0)
    cb = lax.broadcasted_iota(jnp.int32, (P2, P2), 1)
    same_head = (rb < N) == (cb < N)
    eye_b = rb == cb
    rc = lax.broadcasted_iota(jnp.int32, (C, C), 0)
    cc = lax.broadcasted_iota(jnp.int32, (C, C), 1)
    tril_f = (rc >= cc).astype(F32)

    def bdiag(x):
        return jnp.concatenate([jnp.where(left, x, 0.0), jnp.where(left, 0.0, x)], axis=0)

    n_sub = tb // C
    pairs = range(RW_HEADS // 2)

    def chunk_body(it, carry):
        units = [(s, p) for s in range(n_sub) for p in pairs]
        sls = [pl.ds(pl.multiple_of((it * n_sub + s) * C, C), C) for s in range(n_sub)]
        prep = []
        for s in range(n_sub):
            lw = lw_scr[sls[s], :]
            cum = _dot(tril_f, lw, precision=HIGHEST)
            cum_last = cum[C - 1 : C, :]
            prep.append(dict(
                e_incl=jnp.exp(cum), e_excl=jnp.exp(cum - lw), e_neg=jnp.exp(-cum),
                e_tail=jnp.exp(cum_last - cum), p_last=jnp.exp(cum_last),
                r=r_scr[sls[s], :], k=k_scr[sls[s], :], v=v_scr[sls[s], :], kk=kkr_scr[sls[s], :],
                a=a_scr[sls[s], :]))
        lanes = [slice(p * P2, (p + 1) * P2) for p in pairs]
        get = lambda name: [prep[s][name][:, lanes[p]] for s, p in units]
        r2, k2, v2, al, be = get("r"), get("k"), get("v"), get("kk"), get("a")
        e_incl, e_excl, e_neg, e_tail, p_last = get("e_incl"), get("e_excl"), get("e_neg"), get("e_tail"), get("p_last")
        un = range(len(units))
        al_t = [al[u] * e_excl[u] for u in un]
        r_t = [r2[u] * e_incl[u] for u in un]
        be_n = [be[u] * e_neg[u] for u in un]
        k_n = [k2[u] * e_neg[u] for u in un]
        k_et = [(k2[u] * e_tail[u]).T for u in un]
        be_et = [(be[u] * e_tail[u]).T for u in un]
        v_bd = [bdiag(v2[u]) for u in un]
        lhs = [jnp.concatenate([al_t[u], r_t[u]], axis=0) for u in un]
        s_b = [_dot_nt(lhs[u], bdiag(be_n[u])) for u in un]
        s_k = [_dot_nt(lhs[u], bdiag(k_n[u])) for u in un]
        l_ab = [jnp.where(stril, s_b[u][:C], 0.0) for u in un]
        a_rb = [jnp.where(tril, s_b[u][C:], 0.0) for u in un]
        l_ak = [jnp.where(stril, s_k[u][:C], 0.0) for u in un]
        a_rk = [jnp.where(tril, s_k[u][C:], 0.0) for u in un]
        lakv = [_dot(l_ak[u], v_bd[u]) for u in un]
        arkv = [_dot(a_rk[u], v_bd[u]) for u in un]
        kev = [_dot(k_et[u], v2[u]) for u in un]
        t_inv = [eye_f - jnp.where(_level_mask(ri, ci, 0), l_ab[u], 0.0) for u in un]
        lvl = 1
        while (1 << lvl) < C:
            lm = _level_mask(ri, ci, lvl)
            tn = [_dot(t_inv[u], bdiag(jnp.where(lm, l_ab[u], 0.0))) for u in un]
            t_inv = [t_inv[u] - _dot(tn[u], bdiag(t_inv[u])) for u in un]
            lvl += 1
        a_til = [_dot(t_inv[u], bdiag(al_t[u])) for u in un]
        b_til = [_dot(t_inv[u], bdiag(lakv[u])) for u in un]
        r_hat = [r_t[u] - _dot(a_rb[u], bdiag(a_til[u])) for u in un]
        o_hat = [arkv[u] - _dot(a_rb[u], bdiag(b_til[u])) for u in un]
        g_bd = [jnp.where(same_head, jnp.where(eye_b, p_last[u], 0.0) - _dot(be_et[u], a_til[u]), 0.0) for u in un]
        h_bd = [jnp.where(same_head, kev[u] - _dot(be_et[u], b_til[u]), 0.0) for u in un]
        lhs_m = [jnp.concatenate([r_hat[u], g_bd[u]], axis=0) for u in un]
        for u, (s, p) in enumerate(units):
            res = _dot(lhs_m[u], m_scr[p])
            m_scr[p] = res[C:] + h_bd[u]
            o_p = res[:C] + o_hat[u]
            cen = o_p - head_sum(o_p) * (1.0 / N)
            var = head_sum(cen * cen) * (1.0 / N)
            on_scr[sls[s], lanes[p]] = cen * lax.rsqrt(var + RW_GN_EPS)
        return carry

    lax.fori_loop(0, tb // (C * n_sub), chunk_body, 0)
    out = (on_scr[...] * lnw_ref[...] + lnb_ref[...] + bon_scr[...]) * g
    o_ref[0] = out.astype(o_ref.dtype)

    @pl.when(t_idx == pl.num_programs(1) - 1)
    def _():
        for p in range(RW_HEADS // 2):
            m = m_scr[p]
            sout_ref[0, 2 * p] = m[:N, :N].T
            sout_ref[0, 2 * p + 1] = m[N:, N:].T


def _rwkv_seq(z3, shift0, s0, rw, tb):
    b, l, _ = z3.shape
    const = lambda shape: pl.BlockSpec(shape, lambda i, j: (0,) * len(shape))
    wide = lambda: pltpu.VMEM((tb, RW_WIDTH), F32)
    return pl.pallas_call(
        _rwkv_seq_kernel,
        grid=(b, l // tb),
        in_specs=[
            pl.BlockSpec((1, tb, RW_PROJ), lambda i, j: (i, j, 0)),
            pl.BlockSpec((1, 1, RW_PROJ), lambda i, j: (i, 0, 0)),
            pl.BlockSpec((1, RW_HEADS, RW_N, RW_N), lambda i, j: (i, 0, 0, 0)),
            const((1, RW_PROJ)),
            const((1, RW_WIDTH)), const((128, RW_WIDTH)),
            const((1, RW_WIDTH)), const((128, RW_WIDTH)),
            const((128, RW_WIDTH)),
            const((1, RW_WIDTH)), const((1, RW_WIDTH)), const((1, RW_WIDTH)),
            const((1, RW_WIDTH)), const((1, RW_WIDTH)),
        ],
        out_specs=[
            pl.BlockSpec((1, tb, RW_WIDTH), lambda i, j: (i, j, 0)),
            pl.BlockSpec((1, RW_HEADS, RW_N, RW_N), lambda i, j: (i, 0, 0, 0)),
        ],
        out_shape=[
            jax.ShapeDtypeStruct((b, l, RW_WIDTH), BF16),
            jax.ShapeDtypeStruct((b, RW_HEADS, RW_N, RW_N), F32),
        ],
        scratch_shapes=[
            pltpu.VMEM((RW_HEADS // 2, 2 * RW_N, 2 * RW_N), F32),
            pltpu.VMEM((1, RW_PROJ), F32),
            wide(), wide(), wide(), wide(), wide(), wide(), wide(), wide(),
        ],
        compiler_params=_params(("parallel", "arbitrary")),
    )(z3, shift0, s0, rw["mu"], rw["w0"], rw["w2p"], rw["a0"], rw["a2p"], rw["g2"], rw["k_k"], rw["k_a"],
      rw["r_k"], rw["ln_w"], rw["ln_b"])


def _rwkv_step_prep_kernel(z_ref, shift0_ref, mu_ref, w0_ref, w2_ref, a0_ref, a2_ref, g2_ref, kk_ref, ka_ref,
                           rk_ref, r_ref, k_ref, v_ref, al_ref, be_ref, dec_ref, g_ref, bon_ref):
    z = z_ref[...]
    zs = z + mu_ref[...] * (shift0_ref[...] - z)
    r, k, v, kk_raw, a, log_decay, g = _rwkv_features(
        zs, w0_ref[...], w2_ref[...], a0_ref[...], a2_ref[...], g2_ref[...], kk_ref[...], ka_ref[...])
    rk_all = rk_ref[...]
    for h in range(RW_HEADS):
        hs = slice(h * RW_N, (h + 1) * RW_N)
        kk_h = kk_raw[:, hs]
        nrm = jnp.sqrt(jnp.sum(kk_h * kk_h, axis=-1, keepdims=True))
        al = kk_h / jnp.maximum(nrm, 1e-12)
        al_ref[:, hs] = al
        be_ref[:, hs] = al * a[:, hs]
        bon_ref[:, hs] = jnp.sum(r[:, hs] * k[:, hs] * rk_all[:, hs], axis=-1, keepdims=True) * v[:, hs]
    r_ref[...] = r
    k_ref[...] = k
    v_ref[...] = v
    dec_ref[...] = jnp.exp(log_decay)
    g_ref[...] = g


def _rwkv_step_prep(z, shift0, rw):
    n = z.shape[0]
    out = jax.ShapeDtypeStruct((n, RW_WIDTH), F32)
    return pl.pallas_call(
        _rwkv_step_prep_kernel,
        out_shape=[out] * 8,
        compiler_params=pltpu.CompilerParams(vmem_limit_bytes=VMEM_LIMIT),
    )(z, shift0, rw["mu"], rw["w0"], rw["w2p"], rw["a0"], rw["a2p"], rw["g2"], rw["k_k"], rw["k_a"], rw["r_k"])


def _rwkv_step_kernel(s_ref, r_ref, k_ref, al_ref, be_ref, dec_ref, v_ref, snew_ref, o_ref):
    r, k, al, be, dec = r_ref[...], k_ref[...], al_ref[...], be_ref[...], dec_ref[...]

    def body(g, carry):
        rows = pl.ds(pl.multiple_of(g * 8, 8), 8)
        v8 = v_ref[rows, :]
        outs = []
        for j in range(8):
            s = s_ref[0, g * 8 + j]
            sa = -jnp.sum(s * al, axis=0, keepdims=True)
            s_new = s * dec + sa * be + v8[j : j + 1, :] * k
            snew_ref[0, g * 8 + j] = s_new
            outs.append(jnp.sum(s_new * r, axis=0, keepdims=True))
        o_ref[rows, :] = jnp.concatenate(outs, axis=0)
        return carry

    lax.fori_loop(0, RW_N // 8, body, 0)


def _rwkv_step(s0, r, k, al, be, dec, v):
    n = s0.shape[0]
    s_t = jnp.transpose(s0, (1, 2, 3, 0))
    s_spec = pl.BlockSpec((1, RW_N, RW_N, n), lambda h: (h, 0, 0, 0))
    op_spec = pl.BlockSpec((RW_N, n), lambda h: (h, 0))
    s_new_t, o_t = pl.pallas_call(
        _rwkv_step_kernel,
        grid=(RW_HEADS,),
        in_specs=[s_spec] + [op_spec] * 6,
        out_specs=[s_spec, op_spec],
        out_shape=[
            jax.ShapeDtypeStruct(s_t.shape, F32),
            jax.ShapeDtypeStruct((RW_WIDTH, n), F32),
        ],
        compiler_params=_params(("parallel",)),
    )(s_t, r.T, k.T, al.T, be.T, dec.T, v.T)
    return jnp.transpose(s_new_t, (3, 0, 1, 2)), o_t.T


def _gla_features(z, gkw, gkb):
    q = z[:, 0:GLA_QK] * (GLA_DK ** -0.5)
    k = z[:, GLA_QK : 2 * GLA_QK]
    v = z[:, 2 * GLA_QK : 2 * GLA_QK + GLA_WIDTH]
    g = z[:, 2 * GLA_QK + GLA_WIDTH : 2 * GLA_QK + 2 * GLA_WIDTH]
    zgk = z[:, 2 * GLA_QK + 2 * GLA_WIDTH :]
    gk = -_softplus(-(_dot(zgk.astype(BF16), gkw) + gkb)) / GLA_GATE_NORMALIZER
    return q, k, v, g, gk


def _gla_finish(o, g, norm_w):
    outs = []
    for h in range(GLA_HEADS):
        hs = slice(h * GLA_DV, (h + 1) * GLA_DV)
        o_h = o[:, hs]
        o_h = o_h * lax.rsqrt(jnp.mean(o_h * o_h, axis=-1, keepdims=True) + NORM_EPS) * norm_w
        g_h = g[:, hs]
        outs.append(o_h * (g_h * _sigmoid(g_h)))
    return jnp.concatenate(outs, axis=-1)


def _gla_seq_kernel(z_ref, s0_ref, gkw_ref, gkb_ref, nw_ref, wsel_ref, o_ref, sout_ref,
                    st_scr, x_scr, gc_scr, oi_scr):
    C = GLA_CHUNK
    G = 128
    t_idx = pl.program_id(1)
    tb = z_ref.shape[1]
    nc = tb // C
    zero_vk = jnp.zeros((GLA_DV, GLA_DK), F32)

    @pl.when(t_idx == 0)
    def _():
        for p in range(GLA_HEADS // 2):
            top = jnp.concatenate([s0_ref[0, 2 * p].T, zero_vk], axis=1)
            bot = jnp.concatenate([zero_vk, s0_ref[0, 2 * p + 1].T], axis=1)
            st_scr[p] = jnp.concatenate([top, bot], axis=0)

    q, k, v, g, gk = _gla_features(z_ref[0], gkw_ref[...], gkb_ref[...])
    ri = lax.broadcasted_iota(jnp.int32, (G, G), 0)
    ci = lax.broadcasted_iota(jnp.int32, (G, G), 1)
    cum_mat = ((ri // C == ci // C) & (ri >= ci)).astype(F32)
    for m in range(tb // G):
        rows = slice(m * G, (m + 1) * G)
        gc_scr[rows, :] = _dot(cum_mat, gk[rows, :], precision=HIGHEST)
    gcum = gc_scr[...]

    rg = lax.broadcasted_iota(jnp.int32, (tb, 2 * G), 0)
    cg = lax.broadcasted_iota(jnp.int32, (tb, 2 * G), 1)
    blk_mask = ((cg % G) // C == (rg % G) // C) & (cg % C <= rg % C)
    for p in range(GLA_HEADS // 2):
        ls = slice(p * 128, (p + 1) * 128)
        q3 = q[:, ls].reshape(nc, C, 128)
        k3 = k[:, ls].reshape(nc, C, 128)
        g3 = gcum[:, ls].reshape(nc, C, 128) * LOG2_E
        half = C // 2
        for j in range(C):
            lo = 0 if j < half else half
            e = (q3[:, lo:] * jnp.exp2(jnp.minimum(g3[:, lo:] - g3[:, j : j + 1, :], 0.0))) * k3[:, j : j + 1, :]
            if lo:
                e = jnp.concatenate([jnp.zeros((nc, lo, 128), F32), e], axis=1)
            x_scr[:, j * 128 : (j + 1) * 128] = e.reshape(tb, 128).astype(BF16)
        a_t = jnp.where(blk_mask, _dot(x_scr[...], wsel_ref[...]), 0.0).astype(BF16)
        for hl in range(2):
            h = 2 * p + hl
            for m in range(tb // G):
                rows = slice(m * G, (m + 1) * G)
                a_blk = a_t[rows, hl * G : (hl + 1) * G]
                oi_scr[rows, h * GLA_DV : (h + 1) * GLA_DV] = _dot(
                    a_blk, v[rows, h * GLA_DV : (h + 1) * GLA_DV].astype(BF16))

    CG = G // C
    rt = lax.broadcasted_iota(jnp.int32, (G, CG * 128), 0)
    ct = lax.broadcasted_iota(jnp.int32, (G, CG * 128), 1)
    own_chunk = rt // C == ct // 128
    rs = lax.broadcasted_iota(jnp.int32, (2 * GLA_DV, CG * 128), 0)
    cs = lax.broadcasted_iota(jnp.int32, (2 * GLA_DV, CG * 128), 1)
    same_head = rs // GLA_DV == (cs % 128) // GLA_DK

    def chunk_diag(x):
        return jnp.where(own_chunk, jnp.concatenate([x] * CG, axis=1), 0.0)

    for m in range(tb // G):
        rows = slice(m * G, (m + 1) * G)
        for p in range(GLA_HEADS // 2):
            ls = slice(p * 128, (p + 1) * 128)
            vs = slice(p * 2 * GLA_DV, (p + 1) * 2 * GLA_DV)
            g_g = gcum[rows, ls]
            g3 = g_g.reshape(CG, C, 128)
            g_last = jnp.broadcast_to(g3[:, C - 1 : C, :], (CG, C, 128)).reshape(G, 128)
            q_t = q[rows, ls] * jnp.exp(g_g)
            k_t = k[rows, ls] * jnp.exp(g_last - g_g)
            d_s = jnp.where(same_head, _dot_tn(v[rows, vs], chunk_diag(k_t)), 0.0)
            st = st_scr[p]
            starts = []
            for c in range(CG):
                starts.append(st)
                decay = jnp.exp(g_g[c * C + C - 1 : c * C + C, :])
                st = st * decay + d_s[:, c * 128 : (c + 1) * 128]
            st_scr[p] = st
            oi_scr[rows, vs] += _dot_nt(chunk_diag(q_t), jnp.concatenate(starts, axis=1))

    o_ref[0] = _gla_finish(oi_scr[...], g, nw_ref[...]).astype(o_ref.dtype)

    @pl.when(t_idx == pl.num_programs(1) - 1)
    def _():
        for p in range(GLA_HEADS // 2):
            st = st_scr[p]
            sout_ref[0, 2 * p] = st[:GLA_DV, :GLA_DK].T
            sout_ref[0, 2 * p + 1] = st[GLA_DV:, GLA_DK:].T


def _gla_select_matrix():
    j = jnp.arange(GLA_CHUNK)[:, None, None]
    hl = jnp.arange(2)[None, :, None]
    rows_j = jnp.broadcast_to(j, (GLA_CHUNK, 2, GLA_DK)).reshape(-1)
    rows_h = jnp.broadcast_to(hl, (GLA_CHUNK, 2, GLA_DK)).reshape(-1)
    cols = jnp.arange(256)
    sel = (rows_j[:, None] == cols[None, :] % GLA_CHUNK) & (rows_h[:, None] == cols[None, :] // 128)
    return sel.astype(BF16)


def _gla_seq(z3, s0, gl, tb):
    b, l, _ = z3.shape
    const = lambda shape: pl.BlockSpec(shape, lambda i, j: (0,) * len(shape))
    return pl.pallas_call(
        _gla_seq_kernel,
        grid=(b, l // tb),
        in_specs=[
            pl.BlockSpec((1, tb, GLA_PROJ_PAD), lambda i, j: (i, j, 0)),
            pl.BlockSpec((1, GLA_HEADS, GLA_DK, GLA_DV), lambda i, j: (i, 0, 0, 0)),
            const((GLA_LORA_PAD, GLA_QK)), const((1, GLA_QK)), const((1, GLA_DV)),
            const((GLA_CHUNK * 128, 256)),
        ],
        out_specs=[
            pl.BlockSpec((1, tb, GLA_WIDTH), lambda i, j: (i, j, 0)),
            pl.BlockSpec((1, GLA_HEADS, GLA_DK, GLA_DV), lambda i, j: (i, 0, 0, 0)),
        ],
        out_shape=[
            jax.ShapeDtypeStruct((b, l, GLA_WIDTH), BF16),
            jax.ShapeDtypeStruct((b, GLA_HEADS, GLA_DK, GLA_DV), F32),
        ],
        scratch_shapes=[
            pltpu.VMEM((GLA_HEADS // 2, 2 * GLA_DV, 2 * GLA_DK), F32),
            pltpu.VMEM((tb, GLA_CHUNK * 128), BF16),
            pltpu.VMEM((tb, GLA_QK), F32), pltpu.VMEM((tb, GLA_WIDTH), F32),
        ],
        compiler_params=_params(("parallel", "arbitrary")),
    )(z3, s0, gl["gkw"], gl["gkb"], gl["norm_w"], _gla_select_matrix())


def _gla_step_prep_kernel(z_ref, gkw_ref, gkb_ref, q_ref, k_ref, v_ref, g_ref, dec_ref):
    q, k, v, g, gk = _gla_features(z_ref[...], gkw_ref[...], gkb_ref[...])
    q_ref[...] = q
    k_ref[...] = k
    v_ref[...] = v
    g_ref[...] = g
    dec_ref[...] = jnp.exp(gk)


def _gla_step_prep(z, gl):
    n = z.shape[0]
    qk = jax.ShapeDtypeStruct((n, GLA_QK), F32)
    wide = jax.ShapeDtypeStruct((n, GLA_WIDTH), F32)
    return pl.pallas_call(
        _gla_step_prep_kernel,
        out_shape=[qk, qk, wide, wide, qk],
        compiler_params=pltpu.CompilerParams(vmem_limit_bytes=VMEM_LIMIT),
    )(z, gl["gkw"], gl["gkb"])


def _gla_step_kernel(s_ref, q_ref, k_ref, dec_ref, v_ref, snew_ref, o_ref):
    bb = s_ref.shape[0]
    rows = lax.broadcasted_iota(jnp.int32, (bb, bb * GLA_DK), 0)
    cols = lax.broadcasted_iota(jnp.int32, (bb, bb * GLA_DK), 1)
    own = rows == cols // GLA_DK
    ones = jnp.ones((bb, GLA_DV), F32)
    zeros = jnp.zeros((bb, GLA_DV), F32)

    def seq_diag(x):
        return jnp.where(own, jnp.concatenate([x] * bb, axis=1), 0.0)

    for h in range(GLA_HEADS):
        ks = slice(h * GLA_DK, (h + 1) * GLA_DK)
        vs = slice(h * GLA_DV, (h + 1) * GLA_DV)
        s = s_ref[:, h].reshape(bb * GLA_DK, GLA_DV)
        v = v_ref[:, vs]
        lhs_t = jnp.concatenate([seq_diag(k_ref[:, ks]), seq_diag(dec_ref[:, ks])], axis=0)
        rhs = jnp.concatenate([jnp.concatenate([v, zeros], axis=1), jnp.concatenate([zeros, ones], axis=1)], axis=0)
        both = _dot_tn(lhs_t, rhs, precision=HIGHEST)
        s_new = s * both[:, GLA_DV:] + both[:, :GLA_DV]
        snew_ref[:, h] = s_new.reshape(bb, GLA_DK, GLA_DV)
        o_ref[:, vs] = _dot(seq_diag(q_ref[:, ks]), s_new)


def _gla_step(s0, q, k, dec, v, bb):
    n = s0.shape[0]
    s_spec = pl.BlockSpec((bb, GLA_HEADS, GLA_DK, GLA_DV), lambda i: (i, 0, 0, 0))
    qk_spec = pl.BlockSpec((bb, GLA_QK), lambda i: (i, 0))
    v_spec = pl.BlockSpec((bb, GLA_WIDTH), lambda i: (i, 0))
    return pl.pallas_call(
        _gla_step_kernel,
        grid=(n // bb,),
        in_specs=[s_spec, qk_spec, qk_spec, qk_spec, v_spec],
        out_specs=[s_spec, v_spec],
        out_shape=[jax.ShapeDtypeStruct(s0.shape, F32), jax.ShapeDtypeStruct((n, GLA_WIDTH), F32)],
        compiler_params=_params(("parallel",)),
    )(s0, q, k, dec, v)


def _step_post_kernel(orw_ref, bon_ref, grw_ref, lnw_ref, lnb_ref, ogl_ref, ggl_ref, nw_ref, o_rw_ref, o_gl_ref):
    o = orw_ref[...]
    for h in range(RW_HEADS):
        hs = slice(h * RW_N, (h + 1) * RW_N)
        o_h = o[:, hs]
        mean = jnp.mean(o_h, axis=-1, keepdims=True)
        cen = o_h - mean
        var = jnp.mean(cen * cen, axis=-1, keepdims=True)
        on = cen * lax.rsqrt(var + RW_GN_EPS)
        res = (on * lnw_ref[:, hs] + lnb_ref[:, hs] + bon_ref[:, hs]) * grw_ref[:, hs]
        o_rw_ref[:, hs] = res.astype(o_rw_ref.dtype)
    o_gl_ref[...] = _gla_finish(ogl_ref[...], ggl_ref[...], nw_ref[...]).astype(o_gl_ref.dtype)


def _step_post(o_rw, bonus, g_rw, rw, o_gl, g_gl, gl):
    n = o_rw.shape[0]
    return pl.pallas_call(
        _step_post_kernel,
        out_shape=[jax.ShapeDtypeStruct((n, RW_WIDTH), BF16), jax.ShapeDtypeStruct((n, GLA_WIDTH), BF16)],
        compiler_params=pltpu.CompilerParams(vmem_limit_bytes=VMEM_LIMIT),
    )(o_rw, bonus, g_rw, rw["ln_w"], rw["ln_b"], o_gl, g_gl, gl["norm_w"])


def _outproj_router_kernel(x_ref, orw_ref, ogl_ref, wo_ref, gain_ref, wrt_ref, br_ref, before_ref,
                           h_ref, xn_ref, idx_ref, gate_ref, rank_ref, cnt_ref):
    mix = jnp.concatenate([orw_ref[...], ogl_ref[...]], axis=-1)
    h = x_ref[...] + _dot(mix, wo_ref[...])
    h_ref[...] = h
    xn = h * lax.rsqrt(jnp.mean(h * h, axis=-1, keepdims=True) + NORM_EPS) * gain_ref[...]
    _store_row_tiles(xn_ref, xn)
    logits = _dot_nt(wrt_ref[...], xn) + br_ref[...]
    eidx = lax.broadcasted_iota(jnp.int32, logits.shape, 0)
    vals, idxs = [], []
    work = logits
    chosen = jnp.zeros(logits.shape, F32)
    for _ in range(TOP_K):
        m = jnp.max(work, axis=0, keepdims=True)
        sel = jnp.min(jnp.where(work == m, eidx, N_EXPERTS), axis=0, keepdims=True)
        hit = eidx == sel
        work = jnp.where(hit, -jnp.inf, work)
        chosen = chosen + hit.astype(F32)
        vals.append(m)
        idxs.append(sel)
    prefix = _dot(chosen.astype(BF16), before_ref[...])
    exps = [jnp.exp(v - vals[0]) for v in vals]
    denom = exps[0] + exps[1] + exps[2] + exps[3]
    for j in range(TOP_K):
        idx_ref[0, j : j + 1, :] = idxs[j]
        gate_ref[0, j : j + 1, :] = exps[j] / denom
        rank = jnp.sum(jnp.where(eidx == idxs[j], prefix, 0.0), axis=0, keepdims=True)
        rank_ref[0, j : j + 1, :] = rank.astype(jnp.int32)
    cnt = jnp.sum(chosen, axis=1, keepdims=True)
    cnt_ref[0] = jnp.broadcast_to(cnt, (N_EXPERTS, 128)).astype(jnp.int32)


def _outproj_router(x, o_rw, o_gl, w_out, gain, w_router_t, b_router, tm):
    n = x.shape[0]
    nt = n // tm
    const = lambda shape: pl.BlockSpec(shape, lambda i: (0,) * len(shape))
    tok = lambda width: pl.BlockSpec((tm, width), lambda i: (i, 0))
    lane = pl.BlockSpec((1, TOP_K, tm), lambda i: (i, 0, 0))
    t = jnp.arange(tm, dtype=jnp.int32)
    before = (t[:, None] < t[None, :]).astype(BF16)
    return pl.pallas_call(
        _outproj_router_kernel,
        grid=(nt,),
        in_specs=[
            tok(D_MODEL), tok(RW_WIDTH), tok(GLA_WIDTH),
            const((D_MODEL, D_MODEL)), const((1, D_MODEL)), const((N_EXPERTS, D_MODEL)), const((N_EXPERTS, 1)),
            const((tm, tm)),
        ],
        out_specs=[tok(D_MODEL), pl.BlockSpec((tm * ROW_TILE, 128), lambda i: (i, 0)), lane, lane, lane,
                   pl.BlockSpec((1, N_EXPERTS, 128), lambda i: (i, 0, 0))],
        out_shape=[
            jax.ShapeDtypeStruct((n, D_MODEL), F32),
            jax.ShapeDtypeStruct((n * ROW_TILE, 128), F32),
            jax.ShapeDtypeStruct((nt, TOP_K, tm), jnp.int32),
            jax.ShapeDtypeStruct((nt, TOP_K, tm), F32),
            jax.ShapeDtypeStruct((nt, TOP_K, tm), jnp.int32),
            jax.ShapeDtypeStruct((nt, N_EXPERTS, 128), jnp.int32),
        ],
        compiler_params=_params(("parallel",)),
    )(x, o_rw, o_gl, w_out, gain, w_router_t, b_router, before)


def _moe_kernel(be_ref, nu_ref, epoch_ref, next_ref, rows_ref, parts_ref, xs_ref, wg_hbm, wu_hbm, wd_hbm, bg_ref, bu_ref, bd_ref,
                y_ref, w_f32, wg_b, wu_b, wd_b, sems):
    b = pl.program_id(0)
    prev = be_ref[jnp.maximum(b - 1, 0)]
    new_expert = (b == 0) | (be_ref[b] != prev)

    def fetch(e, slot, i):
        w = (wg_hbm, wu_hbm, wd_hbm)[i]
        return pltpu.make_async_copy(w.at[e], w_f32.at[slot, i], sems.at[slot])

    @pl.when(b == 0)
    def _():
        for i in range(3):
            fetch(be_ref[0], 0, i).start()

    @pl.when(new_expert)
    def _():
        slot = epoch_ref[b] % 2
        for i in range(3):
            fetch(be_ref[b], slot, i).wait()
        wg_b[...] = w_f32[slot, 0].astype(BF16)
        wu_b[...] = w_f32[slot, 1].astype(BF16)
        wd_b[...] = w_f32[slot, 2].astype(BF16)

    first_part, end_part = parts_ref[b] // 4, parts_ref[b] % 4
    for i in range(3):
        @pl.when((next_ref[b] >= 0) & (first_part <= i) & (i < end_part))
        def _():
            fetch(next_ref[b], 1 - epoch_ref[b] % 2, i).start()

    def ffn(m):
        x = _load_row_tiles(xs_ref, m).astype(BF16)
        half = D_MODEL // 2
        acc = None
        for f in range(2):
            fs = slice(f * half, (f + 1) * half)
            gt = _dot(x, wg_b[:, fs]) + bg_ref[0, :, fs]
            up = _dot(x, wu_b[:, fs]) + bu_ref[0, :, fs]
            gt = jnp.minimum(gt, SWIGLU_LIMIT)
            up = jnp.clip(up, -SWIGLU_LIMIT, SWIGLU_LIMIT)
            hid = (up + 1.0) * gt * _sigmoid(SWIGLU_ALPHA * gt)
            part = _dot(hid.astype(BF16), wd_b[fs, :])
            acc = part if acc is None else acc + part
        _store_row_tiles(y_ref, acc + bd_ref[0])

    quarter = MOE_BLOCK // 4
    quarters = (rows_ref[b] + quarter - 1) // quarter
    for q in range(1, 5):
        @pl.when((b < nu_ref[0]) & (quarters == q))
        def _():
            ffn(q * quarter)


def _moe_ffn(block_expert, n_used, block_rows, xs, w_gate, w_up, w_down, b_gate, b_up, b_down):
    n_blocks = block_expert.shape[0]
    pos = jnp.arange(n_blocks, dtype=jnp.int32)
    change = (pos > 0) & (block_expert != jnp.roll(block_expert, 1))
    epoch = jnp.cumsum(change.astype(jnp.int32))
    later = change[None, :] & (pos[None, :] > pos[:, None])
    first = jnp.min(jnp.where(later, pos[None, :], n_blocks), axis=1)
    next_e = jnp.sum(jnp.where(pos[None, :] == first[:, None], block_expert[None, :], 0), axis=1)
    next_e = jnp.where(first < n_blocks, next_e, -1).astype(jnp.int32)
    run_start = jnp.max(jnp.where((pos[None, :] <= pos[:, None]) & (change | (pos == 0))[None, :], pos[None, :], 0),
                        axis=1)
    q = jnp.minimum(pos - run_start, 3)
    used = pos < n_used[0]
    is_last = jnp.roll(change, -1) | (pos == n_used[0] - 1)
    parts = jnp.where(used, q * 4 + jnp.where(is_last, 3, jnp.minimum(q + 1, 3)), 15).astype(jnp.int32)

    row = lambda b, be, nu, ep, nx, br, pt: (jnp.minimum(b, nu[0] - 1), 0)
    bspec = pl.BlockSpec((1, 1, D_MODEL), lambda b, be, nu, ep, nx, br, pt: (be[b], 0, 0))
    wspec = pl.BlockSpec(memory_space=pl.ANY)
    grid_spec = pltpu.PrefetchScalarGridSpec(
        num_scalar_prefetch=6,
        grid=(n_blocks,),
        in_specs=[pl.BlockSpec((MOE_BLOCK * ROW_TILE, 128), row), wspec, wspec, wspec, bspec, bspec, bspec],
        out_specs=pl.BlockSpec((MOE_BLOCK * ROW_TILE, 128), row),
        scratch_shapes=[pltpu.VMEM((2, 3, D_MODEL, D_MODEL), F32)] + [pltpu.VMEM((D_MODEL, D_MODEL), BF16)] * 3
        + [pltpu.SemaphoreType.DMA((2,))],
    )
    return pl.pallas_call(
        _moe_kernel,
        grid_spec=grid_spec,
        out_shape=jax.ShapeDtypeStruct((n_blocks * MOE_BLOCK * ROW_TILE, 128), F32),
        compiler_params=_params(("arbitrary",)),
    )(block_expert, n_used, epoch, next_e, block_rows, parts, xs, w_gate, w_up, w_down,
      b_gate.reshape(N_EXPERTS, 1, D_MODEL), b_up.reshape(N_EXPERTS, 1, D_MODEL),
      b_down.reshape(N_EXPERTS, 1, D_MODEL))


SEG_ALIGN = 8
GROUP_ROWS = SEG_ALIGN * ROW_TILE


def _local_rows(tm):
    return tm * TOP_K + N_EXPERTS * SEG_ALIGN


BIG_COPY = 4


def _copy_tables():
    n_big = _local_rows(TOK_BLOCK) // SEG_ALIGN // BIG_COPY
    n_small = N_EXPERTS * (BIG_COPY - 1)
    return 2 * n_big, 2 * (n_big + n_small)


def _issue_group_copies(cp_ref, hbm, buf, sem, to_hbm):
    small_at, counts_at = _copy_tables()

    def copy(at, n_groups):
        rows = n_groups * GROUP_ROWS
        b = buf.at[pl.ds(pl.multiple_of(cp_ref[0, 0, at] * GROUP_ROWS, GROUP_ROWS), rows)]
        h = hbm.at[pl.ds(pl.multiple_of(cp_ref[0, 0, at + 1] * GROUP_ROWS, GROUP_ROWS), rows)]
        return pltpu.make_async_copy(b, h, sem) if to_hbm else pltpu.make_async_copy(h, b, sem)

    def big(i, carry):
        copy(2 * i, BIG_COPY).start()
        return carry

    def small(i, carry):
        copy(small_at + 2 * i, 1).start()
        return carry

    lax.fori_loop(0, cp_ref[0, 0, counts_at], big, 0)
    lax.fori_loop(0, cp_ref[0, 0, counts_at + 1], small, 0)


def _wait_group_copies(cp_ref, hbm, buf, sem, to_hbm):
    rows = pl.ds(0, pl.multiple_of(cp_ref[0, 0, _copy_tables()[1] + 2] * GROUP_ROWS, GROUP_ROWS))
    b, h = buf.at[rows], hbm.at[rows]
    (pltpu.make_async_copy(b, h, sem) if to_hbm else pltpu.make_async_copy(h, b, sem)).wait()


def _combine_kernel(grp_c, grp_n, lpos_ref, gate_ref, h_ref, gain_ref, y_hbm, o_ref, ybuf, fbuf, sems):
    i = pl.program_id(0)
    nt = pl.num_programs(0)
    tm = h_ref.shape[0]

    @pl.when(i == 0)
    def _():
        _issue_group_copies(grp_c, y_hbm, ybuf.at[0], sems.at[0], False)

    @pl.when(i + 1 < nt)
    def _():
        _issue_group_copies(grp_n, y_hbm, ybuf.at[(i + 1) % 2], sems.at[(i + 1) % 2], False)

    slot = i % 2
    yb = ybuf.at[slot]
    _wait_group_copies(grp_c, y_hbm, yb, sems.at[slot], False)

    def token_body(t, carry):
        acc = None
        for j in range(TOP_K):
            row = pl.multiple_of(lpos_ref[0, 0, t * TOP_K + j], ROW_TILE)
            term = gate_ref[0, 0, t * TOP_K + j] * yb[pl.ds(row, ROW_TILE), :]
            acc = term if acc is None else acc + term
        fbuf[pl.ds(pl.multiple_of(t * ROW_TILE, ROW_TILE), ROW_TILE), :] = acc
        return carry

    lax.fori_loop(0, tm, token_body, 0, unroll=8)
    f = h_ref[...] + _load_row_tiles(fbuf, tm)
    o_ref[...] = f * lax.rsqrt(jnp.mean(f * f, axis=-1, keepdims=True) + NORM_EPS) * gain_ref[...]


def _combine(h, y_rows, grp3, lpos3, gate3, gain, tm):
    n = h.shape[0]
    nt = n // tm
    n_local = _local_rows(tm)
    gw = grp3.shape[-1]
    smem = lambda shape, imap: pl.BlockSpec(shape, imap, memory_space=pltpu.SMEM)
    cur = lambda i: (i, 0, 0)
    nxt = lambda i: (jnp.minimum(i + 1, nt - 1), 0, 0)
    return pl.pallas_call(
        _combine_kernel,
        grid=(nt,),
        in_specs=[
            smem((1, 1, gw), cur), smem((1, 1, gw), nxt),
            smem((1, 1, TOP_K * tm), cur), smem((1, 1, TOP_K * tm), cur),
            pl.BlockSpec((tm, D_MODEL), lambda i: (i, 0)),
            pl.BlockSpec((1, D_MODEL), lambda i: (0, 0)),
            pl.BlockSpec(memory_space=pl.ANY),
        ],
        out_specs=pl.BlockSpec((tm, D_MODEL), lambda i: (i, 0)),
        out_shape=jax.ShapeDtypeStruct((n, D_MODEL), F32),
        scratch_shapes=[pltpu.VMEM((2, n_local * ROW_TILE, 128), F32), pltpu.VMEM((tm * ROW_TILE, 128), F32),
                        pltpu.SemaphoreType.DMA((2,))],
        compiler_params=pltpu.CompilerParams(dimension_semantics=("arbitrary",), vmem_limit_bytes=VMEM_LIMIT,
                                             disable_bounds_checks=True),
    )(grp3, grp3, lpos3, gate3, h, gain, y_rows)


def _dispatch_kernel(*refs, fill):
    if fill:
        grp_ref, grp_prev, lpos_ref, ends_ref, x_ref, xs_hbm, sorted_buf, zero_scr, sems, zsem = refs
    else:
        grp_ref, grp_prev, lpos_ref, x_ref, _, xs_hbm, sorted_buf, sems = refs
    i = pl.program_id(0)
    tm = x_ref.shape[0] // ROW_TILE
    blk = MOE_BLOCK * ROW_TILE
    sorted_scr = sorted_buf.at[i % 2]
    sem = sems.at[i % 2]

    if fill:
        def fill_copy(e):
            start = pl.multiple_of((ends_ref[0, e] - MOE_BLOCK) * ROW_TILE, blk)
            return pltpu.make_async_copy(zero_scr, xs_hbm.at[pl.ds(start, blk)], zsem)

        @pl.when(i == 0)
        def _():
            zero_scr[...] = jnp.zeros(zero_scr.shape, zero_scr.dtype)
            for e in range(N_EXPERTS):
                @pl.when(ends_ref[1, e] > 0)
                def _():
                    fill_copy(e).start()

    sorted_scr[...] = jnp.zeros(sorted_scr.shape, sorted_scr.dtype)

    def move(t, carry):
        row = x_ref[pl.ds(pl.multiple_of(t * ROW_TILE, ROW_TILE), ROW_TILE), :]
        for j in range(TOP_K):
            dst = pl.multiple_of(lpos_ref[0, 0, t * TOP_K + j], ROW_TILE)
            sorted_scr[pl.ds(dst, ROW_TILE), :] = row
        return carry

    lax.fori_loop(0, tm, move, 0, unroll=8)

    if fill:
        @pl.when(i == 0)
        def _():
            for e in range(N_EXPERTS):
                @pl.when(ends_ref[1, e] > 0)
                def _():
                    fill_copy(e).wait()

    @pl.when(i > 0)
    def _():
        _wait_group_copies(grp_prev, xs_hbm, sorted_buf.at[(i + 1) % 2], sems.at[(i + 1) % 2], True)

    _issue_group_copies(grp_ref, xs_hbm, sorted_scr, sem, True)

    @pl.when(i == pl.num_programs(0) - 1)
    def _():
        _wait_group_copies(grp_ref, xs_hbm, sorted_scr, sem, True)


def _dispatch(xn, grp3, lpos3, tm, n_slots, ends=None, xs=None):
    n = xn.shape[0] // ROW_TILE
    fill = xs is None
    smem = lambda shape, imap: pl.BlockSpec(shape, imap, memory_space=pltpu.SMEM)
    in_specs = [smem((1, 1, grp3.shape[-1]), lambda i: (i, 0, 0)),
                smem((1, 1, grp3.shape[-1]), lambda i: (jnp.maximum(i - 1, 0), 0, 0)),
                smem((1, 1, TOP_K * tm), lambda i: (i, 0, 0))]
    args = [grp3, grp3, lpos3]
    scratch = [pltpu.VMEM((2, _local_rows(tm) * ROW_TILE, 128), F32)]
    if fill:
        in_specs.append(smem((2, N_EXPERTS), lambda i: (0, 0)))
        args.append(ends)
        scratch.append(pltpu.VMEM((MOE_BLOCK * ROW_TILE, 128), F32))
    in_specs.append(pl.BlockSpec((tm * ROW_TILE, 128), lambda i: (i, 0)))
    args.append(xn)
    aliases = {}
    if not fill:
        in_specs.append(pl.BlockSpec(memory_space=pl.ANY))
        args.append(xs)
        aliases = {len(args) - 1: 0}
    scratch.append(pltpu.SemaphoreType.DMA((2,)))
    if fill:
        scratch.append(pltpu.SemaphoreType.DMA(()))
    return pl.pallas_call(
        functools.partial(_dispatch_kernel, fill=fill),
        grid=(n // tm,),
        in_specs=in_specs,
        out_specs=pl.BlockSpec(memory_space=pl.ANY),
        out_shape=jax.ShapeDtypeStruct((n_slots * ROW_TILE, 128), F32),
        scratch_shapes=scratch,
        input_output_aliases=aliases,
        compiler_params=pltpu.CompilerParams(dimension_semantics=("arbitrary",), vmem_limit_bytes=VMEM_LIMIT,
                                             disable_bounds_checks=True, has_side_effects=True),
    )(*args)


def _pad_rows(w, rows, offset):
    out = jnp.zeros((rows, w.shape[1]), w.dtype)
    return out.at[offset : offset + w.shape[0]].set(w)


def _routing_tables(counts, n_pairs):
    n_tiles = counts.shape[0]
    n_blocks = (n_pairs + n_tiles * N_EXPERTS * (SEG_ALIGN - 1) + N_EXPERTS * (MOE_BLOCK - 1)
                + MOE_BLOCK - 1) // MOE_BLOCK
    runs = (counts + SEG_ALIGN - 1) // SEG_ALIGN * SEG_ALIGN
    local_start = jnp.cumsum(runs, axis=1) - runs
    total = jnp.sum(runs, axis=0)
    padded = (total + MOE_BLOCK - 1) // MOE_BLOCK * MOE_BLOCK
    pends = jnp.cumsum(padded)
    pstarts = pends - padded
    global_start = pstarts[None, :] + jnp.cumsum(runs, axis=0) - runs
    blocks = jnp.arange(n_blocks, dtype=jnp.int32) * MOE_BLOCK
    n_used = (pends[-1] // MOE_BLOCK).astype(jnp.int32)
    owner = jnp.sum((pends[None, :] <= blocks[:, None]).astype(jnp.int32), axis=1)
    block_expert = jnp.minimum(owner, N_EXPERTS - 1)
    last = jnp.sum(jnp.where(jnp.arange(n_blocks) == n_used - 1, block_expert, 0))
    block_expert = jnp.where(jnp.arange(n_blocks) < n_used, block_expert, last)
    ends = jnp.stack([pends, padded]).astype(jnp.int32)
    row_end = jnp.sum(jnp.where(block_expert[:, None] == jnp.arange(N_EXPERTS), (pstarts + total)[None, :], 0), axis=1)
    block_rows = jnp.clip(row_end - blocks, 0, MOE_BLOCK).astype(jnp.int32)
    return (runs, local_start, global_start, ends, block_expert.astype(jnp.int32), n_used.reshape(1), block_rows,
            n_blocks)


def _local_positions(idx3, rank3, local_start):
    hit = idx3[..., None] == jnp.arange(N_EXPERTS, dtype=jnp.int32)
    lpos = rank3 + jnp.sum(jnp.where(hit, local_start[:, None, None, :], 0), axis=-1)
    return _token_major(lpos.astype(jnp.int32) * ROW_TILE)


def _copy_lists(runs, local_start, global_start):
    experts = jnp.arange(N_EXPERTS, dtype=jnp.int32)
    groups = runs // SEG_ALIGN
    l8, g8 = local_start // SEG_ALIGN, global_start // SEG_ALIGN
    n_big, n_small = groups // BIG_COPY, groups % BIG_COPY

    def copy_list(count, first_local, first_global, step, length):
        ends = jnp.cumsum(count, axis=1)
        o = jnp.arange(length, dtype=jnp.int32)
        owner = jnp.minimum(jnp.sum((ends[:, None, :] <= o[None, :, None]).astype(jnp.int32), axis=-1), N_EXPERTS - 1)
        sel = owner[..., None] == experts
        pick = lambda t: jnp.sum(jnp.where(sel, t[:, None, :], 0), axis=-1)
        k = (o[None, :] - pick(ends - count)) * step
        pairs = jnp.stack([pick(first_local) + k, pick(first_global) + k], axis=-1)
        live = (o[None, :] < ends[:, -1:])[..., None]
        return jnp.where(live, pairs, 0).reshape(count.shape[0], 2 * length)

    small_at, counts_at = _copy_tables()
    big = copy_list(n_big, l8, g8, BIG_COPY, small_at // 2)
    small = copy_list(n_small, l8 + n_big * BIG_COPY, g8 + n_big * BIG_COPY, 1, (counts_at - small_at) // 2)
    counts = jnp.stack([jnp.sum(n_big, axis=1), jnp.sum(n_small, axis=1), jnp.sum(groups, axis=1)], axis=1)
    counts = jnp.pad(counts, ((0, 0), (0, 5)))
    return jnp.concatenate([big, small, counts], axis=1).astype(jnp.int32)[:, None, :]


def _token_major(t3):
    return jnp.swapaxes(t3, 1, 2).reshape(t3.shape[0], 1, -1)


def kernel(x_prompt, x_sample, state_rwkv_shift, state_rwkv_wkv, state_gla, norm_mix, w_in, rw_mu, rw_w0, rw_w2, rw_a0, rw_a2, rw_g2, rw_k_k, rw_k_a, rw_r_k, rw_ln_w, rw_ln_b, gla_gk_w2, gla_gk_b, gla_norm_w, w_out, norm_ffn, w_router, b_router, w_gate, b_gate, w_up, b_up, w_down, b_down, norm_final):
    depth = norm_mix.shape[0]
    assert depth == 1
    bp, lp, d = x_prompt.shape
    bs, ls, _ = x_sample.shape
    assert ls == 1 and lp % SEQ_BLOCK == 0
    l = 0
    row = lambda t: t.reshape(1, -1)

    w_in_b = w_in[l].astype(BF16)
    w_in_r = w_in_b[:, :RW_PROJ]
    w_in_g = jnp.pad(w_in_b[:, RW_PROJ:], ((0, 0), (0, GLA_PROJ_PAD - GLA_PROJ)))
    rw = dict(
        mu=row(rw_mu[l]), w0=row(rw_w0[l]), a0=row(rw_a0[l]),
        w2p=_pad_rows(rw_w2[l].astype(BF16), 128, 0), a2p=_pad_rows(rw_a2[l].astype(BF16), 128, 64),
        g2=rw_g2[l].astype(BF16), k_k=row(rw_k_k[l]), k_a=row(rw_k_a[l]), r_k=row(rw_r_k[l]),
        ln_w=row(rw_ln_w[l]), ln_b=row(rw_ln_b[l]))
    gl = dict(gkw=_pad_rows(gla_gk_w2[l].astype(BF16), GLA_LORA_PAD, 0), gkb=row(gla_gk_b[l]),
              norm_w=row(gla_norm_w[l]))
    gain_mix = row(norm_mix[l])

    n_p = bp * lp
    xp = x_prompt.reshape(n_p, d)
    zr_p, zg_p = _inproj(xp, gain_mix, w_in_r, w_in_g, TOK_BLOCK)
    zr_p3 = zr_p.reshape(bp, lp, RW_PROJ)
    o_rw_p, wkv_p = _rwkv_seq(zr_p3, jnp.zeros((bp, 1, RW_PROJ), F32),
                              jnp.zeros((bp, RW_HEADS, RW_N, RW_N), F32), rw, SEQ_BLOCK)
    o_gl_p, gla_p = _gla_seq(zg_p.reshape(bp, lp, GLA_PROJ_PAD),
                             jnp.zeros((bp, GLA_HEADS, GLA_DK, GLA_DV), F32), gl, SEQ_BLOCK)
    shift_p = zr_p3[:, -1, :]

    xs_ = x_sample.reshape(bs, d)
    zr_s, zg_s = _inproj(xs_, gain_mix, w_in_r, w_in_g, bs)
    r, k, v, al, be, dec, g_rw, bonus = _rwkv_step_prep(zr_s, state_rwkv_shift[l], rw)
    wkv_s, o_rw_s = _rwkv_step(state_rwkv_wkv[l], r, k, al, be, dec, v)
    q, kg, vg, g_gl, dec_g = _gla_step_prep(zg_s, gl)
    gla_s, o_gl_s = _gla_step(state_gla[l], q, kg, dec_g, vg, 16)
    o_rw_s2, o_gl_s2 = _step_post(o_rw_s, bonus, g_rw, rw, o_gl_s, g_gl, gl)
    shift_s = zr_s

    w_out_b = w_out[l].astype(BF16)
    router = (w_out_b, row(norm_ffn[l]), w_router[l].T, b_router[l].reshape(N_EXPERTS, 1))
    h_p, xn_p, idx_p, gate_p, rank_p, cnt_p = _outproj_router(
        xp, o_rw_p.reshape(n_p, RW_WIDTH), o_gl_p.reshape(n_p, GLA_WIDTH), *router, TOK_BLOCK)
    h_s, xn_s, idx_s, gate_s, rank_s, cnt_s = _outproj_router(xs_, o_rw_s2, o_gl_s2, *router, bs)
    nt_p = n_p // TOK_BLOCK
    counts = jnp.concatenate([cnt_p[:, :, 0], cnt_s[:, :, 0]], axis=0)
    runs, lstart, gstart, ends, block_expert, n_used, block_rows, n_blocks = _routing_tables(
        counts, (n_p + bs) * TOP_K)
    n_slots = n_blocks * MOE_BLOCK
    lpos_p = _local_positions(idx_p, rank_p, lstart[:nt_p])
    lpos_s = _local_positions(idx_s, rank_s, lstart[nt_p:])
    grp = _copy_lists(runs, lstart, gstart)
    grp_p, grp_s = grp[:nt_p], grp[nt_p:]
    xs_rows = _dispatch(xn_p, grp_p, lpos_p, TOK_BLOCK, n_slots, ends=ends)
    xs_rows = _dispatch(xn_s, grp_s, lpos_s, bs, n_slots, xs=xs_rows)
    y_rows = _moe_ffn(block_expert, n_used, block_rows, xs_rows, w_gate[l], w_up[l], w_down[l], b_gate[l], b_up[l], b_down[l])
    gain_f = row(norm_final)
    y_p = _combine(h_p, y_rows, grp_p, lpos_p, _token_major(gate_p), gain_f, TOK_BLOCK)
    y_s = _combine(h_s, y_rows, grp_s, lpos_s, _token_major(gate_s), gain_f, bs)

    y_prompt = y_p.reshape(bp, lp, d)
    y_sample = y_s.reshape(bs, ls, d)
    return (y_prompt, y_sample, shift_p[None], wkv_p[None], gla_p[None], shift_s[None], wkv_s[None], gla_s[None])
```

```python
import functools

import jax
import jax.numpy as jnp
from jax import lax
from jax.experimental import pallas as pl
from jax.experimental.pallas import tpu as pltpu

F32 = jnp.float32
BF16 = jnp.bfloat16
HIGHEST = lax.Precision.HIGHEST

D_MODEL = 1024
RW_WIDTH = 512
RW_HEADS = 8
RW_N = 64
RW_PROJ = 1792
RW_GN_EPS = 64e-5
GLA_HEADS = 4
GLA_DK = 64
GLA_DV = 128
GLA_WIDTH = 512
GLA_QK = GLA_HEADS * GLA_DK
GLA_PROJ = 1552
GLA_PROJ_PAD = 1664
GLA_LORA_PAD = 128
GLA_GATE_NORMALIZER = 16.0
N_EXPERTS = 32
TOP_K = 4
SWIGLU_LIMIT = 7.0
SWIGLU_ALPHA = 1.702
NORM_EPS = 1e-5
LOG2_E = 1.4426950408889634

RW_CHUNK = 64
GLA_CHUNK = 16
SEQ_BLOCK = 512
TOK_BLOCK = 512
MOE_BLOCK = 1024
MOE_PIECE = 128
VMEM_LIMIT = 56 * 1024 * 1024


def _dot(a, b, precision=None):
    return jnp.dot(a, b, preferred_element_type=F32, precision=precision)


def _dot_nt(a, b, precision=None):
    return lax.dot_general(a, b, (((1,), (1,)), ((), ())), preferred_element_type=F32, precision=precision)


def _dot_tn(a, b, precision=None):
    return lax.dot_general(a, b, (((0,), (0,)), ((), ())), preferred_element_type=F32, precision=precision)


def _sigmoid(x):
    return 1.0 / (1.0 + jnp.exp(-x))


def _softplus(x):
    return jnp.maximum(x, 0.0) + jnp.log(1.0 + jnp.exp(-jnp.abs(x)))


def _params(sem):
    return pltpu.CompilerParams(dimension_semantics=sem, vmem_limit_bytes=VMEM_LIMIT)


ROW_TILE = D_MODEL // 128


def _store_row_tiles(ref, x):
    m = x.shape[0]
    for c in range(ROW_TILE):
        ref[pl.ds(c, m, stride=ROW_TILE), :] = x[:, c * 128 : (c + 1) * 128]


def _load_row_tiles(ref, m):
    return jnp.concatenate([ref[pl.ds(c, m, stride=ROW_TILE), :] for c in range(ROW_TILE)], axis=-1)


def _inproj_kernel(x_ref, gain_ref, wr_ref, wg_ref, zr_ref, zg_ref):
    x = x_ref[...]
    xn = x * lax.rsqrt(jnp.mean(x * x, axis=-1, keepdims=True) + NORM_EPS) * gain_ref[...]
    xb = xn.astype(BF16)
    zr_ref[...] = _dot(xb, wr_ref[...])
    zg_ref[...] = _dot(xb, wg_ref[...])


def _inproj(x, gain, w_r, w_g, tm):
    n = x.shape[0]
    return pl.pallas_call(
        _inproj_kernel,
        grid=(n // tm,),
        in_specs=[
            pl.BlockSpec((tm, D_MODEL), lambda i: (i, 0)),
            pl.BlockSpec((1, D_MODEL), lambda i: (0, 0)),
            pl.BlockSpec((D_MODEL, RW_PROJ), lambda i: (0, 0)),
            pl.BlockSpec((D_MODEL, GLA_PROJ_PAD), lambda i: (0, 0)),
        ],
        out_specs=[
            pl.BlockSpec((tm, RW_PROJ), lambda i: (i, 0)),
            pl.BlockSpec((tm, GLA_PROJ_PAD), lambda i: (i, 0)),
        ],
        out_shape=[
            jax.ShapeDtypeStruct((n, RW_PROJ), F32),
            jax.ShapeDtypeStruct((n, GLA_PROJ_PAD), F32),
        ],
        compiler_params=_params(("parallel",)),
    )(x, gain, w_r, w_g)


def _rwkv_features(zs, w0, w2p, a0, a2p, g2, k_k, k_a):
    W = RW_WIDTH
    r = zs[:, 0:W]
    k_raw = zs[:, W : 2 * W]
    v = zs[:, 2 * W : 3 * W]
    zwa = zs[:, 3 * W : 3 * W + 128]
    zg = zs[:, 3 * W + 128 :]
    w = -_softplus(-(w0 + _dot(jnp.tanh(zwa).astype(BF16), w2p))) - 0.5
    log_decay = -jnp.exp(w)
    a = _sigmoid(a0 + _dot(zwa.astype(BF16), a2p))
    g = _dot(_sigmoid(zg).astype(BF16), g2)
    kk_raw = k_raw * k_k
    k = k_raw * (1.0 + (a - 1.0) * k_a)
    return r, k, v, kk_raw, a, log_decay, g


def _level_mask(ri, ci, lvl):
    same = (ri >> (lvl + 1)) == (ci >> (lvl + 1))
    return same & (((ri >> lvl) & 1) == 1) & (((ci >> lvl) & 1) == 0)


def _rwkv_seq_kernel(z_ref, shift0_ref, s0_ref, mu_ref, w0_ref, w2_ref, a0_ref, a2_ref, g2_ref, kk_ref, ka_ref,
                     rk_ref, lnw_ref, lnb_ref, o_ref, sout_ref,
                     m_scr, prev_scr, r_scr, k_scr, v_scr, kkr_scr, a_scr, lw_scr, on_scr, bon_scr):
    C = RW_CHUNK
    N = RW_N
    t_idx = pl.program_id(1)
    tb = z_ref.shape[1]
    zero_nn = jnp.zeros((N, N), F32)

    @pl.when(t_idx == 0)
    def _():
        prev_scr[...] = shift0_ref[0]
        for p in range(RW_HEADS // 2):
            top = jnp.concatenate([s0_ref[0, 2 * p].T, zero_nn], axis=1)
            bot = jnp.concatenate([zero_nn, s0_ref[0, 2 * p + 1].T], axis=1)
            m_scr[p] = jnp.concatenate([top, bot], axis=0)

    z = z_ref[0]
    row = lax.broadcasted_iota(jnp.int32, z.shape, 0)
    z_prev = jnp.where(row == 0, prev_scr[...], pltpu.roll(z, 1, axis=0))
    prev_scr[...] = z[tb - 1 : tb, :]
    zs = z + mu_ref[...] * (z_prev - z)
    r, k, v, kk_raw, a, log_decay, g = _rwkv_features(
        zs, w0_ref[...], w2_ref[...], a0_ref[...], a2_ref[...], g2_ref[...], kk_ref[...], ka_ref[...])
    P2 = 2 * N
    left1 = lax.broadcasted_iota(jnp.int32, (1, P2), 1) < N

    def head_sum(x):
        s0 = jnp.sum(jnp.where(left1, x, 0.0), axis=-1, keepdims=True)
        s1 = jnp.sum(jnp.where(left1, 0.0, x), axis=-1, keepdims=True)
        return jnp.where(left1, s0, s1)

    def head_sum_wide(x):
        return jnp.concatenate([head_sum(x[:, p * P2 : (p + 1) * P2]) for p in range(RW_HEADS // 2)], axis=1)

    alpha = kk_raw * lax.rsqrt(jnp.maximum(head_sum_wide(kk_raw * kk_raw), 1e-24))
    r_scr[...] = r
    k_scr[...] = k
    v_scr[...] = v
    kkr_scr[...] = alpha
    a_scr[...] = alpha * a
    lw_scr[...] = log_decay
    bon_scr[...] = head_sum_wide(r * k * rk_ref[...]) * v

    ri = lax.broadcasted_iota(jnp.int32, (C, P2), 0)
    ci = lax.broadcasted_iota(jnp.int32, (C, P2), 1) % N
    left = lax.broadcasted_iota(jnp.int32, (C, P2), 1) < N
    tril = ri >= ci
    stril = ri > ci
    eye_f = (ri == ci).astype(F32)
    rb = lax.broadcasted_iota(jnp.int32, (P2, P2), 0)
    cb = lax.broadcasted_iota(jnp.int32, (P2, P2), 1)
    same_head = (rb < N) == (cb < N)
    eye_b = rb == cb
    rc = lax.broadcasted_iota(jnp.int32, (C, C), 0)
    cc = lax.broadcasted_iota(jnp.int32, (C, C), 1)
    tril_f = (rc >= cc).astype(F32)

    def bdiag(x):
        return jnp.concatenate([jnp.where(left, x, 0.0), jnp.where(left, 0.0, x)], axis=0)

    n_sub = tb // C
    pairs = range(RW_HEADS // 2)

    def chunk_body(it, carry):
        units = [(s, p) for s in range(n_sub) for p in pairs]
        sls = [pl.ds(pl.multiple_of((it * n_sub + s) * C, C), C) for s in range(n_sub)]
        prep = []
        for s in range(n_sub):
            lw = lw_scr[sls[s], :]
            cum = _dot(tril_f, lw, precision=HIGHEST)
            cum_last = cum[C - 1 : C, :]
            prep.append(dict(
                e_incl=jnp.exp(cum), e_excl=jnp.exp(cum - lw), e_neg=jnp.exp(-cum),
                e_tail=jnp.exp(cum_last - cum), p_last=jnp.exp(cum_last),
                r=r_scr[sls[s], :], k=k_scr[sls[s], :], v=v_scr[sls[s], :], kk=kkr_scr[sls[s], :],
                a=a_scr[sls[s], :]))
        lanes = [slice(p * P2, (p + 1) * P2) for p in pairs]
        get = lambda name: [prep[s][name][:, lanes[p]] for s, p in units]
        r2, k2, v2, al, be = get("r"), get("k"), get("v"), get("kk"), get("a")
        e_incl, e_excl, e_neg, e_tail, p_last = get("e_incl"), get("e_excl"), get("e_neg"), get("e_tail"), get("p_last")
        un = range(len(units))
        al_t = [al[u] * e_excl[u] for u in un]
        r_t = [r2[u] * e_incl[u] for u in un]
        be_n = [be[u] * e_neg[u] for u in un]
        k_n = [k2[u] * e_neg[u] for u in un]
        k_et = [(k2[u] * e_tail[u]).T for u in un]
        be_et = [(be[u] * e_tail[u]).T for u in un]
        v_bd = [bdiag(v2[u]) for u in un]
        lhs = [jnp.concatenate([al_t[u], r_t[u]], axis=0) for u in un]
        s_b = [_dot_nt(lhs[u], bdiag(be_n[u])) for u in un]
        s_k = [_dot_nt(lhs[u], bdiag(k_n[u])) for u in un]
        l_ab = [jnp.where(stril, s_b[u][:C], 0.0) for u in un]
        a_rb = [jnp.where(tril, s_b[u][C:], 0.0) for u in un]
        l_ak = [jnp.where(stril, s_k[u][:C], 0.0) for u in un]
        a_rk = [jnp.where(tril, s_k[u][C:], 0.0) for u in un]
        lakv = [_dot(l_ak[u], v_bd[u]) for u in un]
        arkv = [_dot(a_rk[u], v_bd[u]) for u in un]
        kev = [_dot(k_et[u], v2[u]) for u in un]
        t_inv = [eye_f - jnp.where(_level_mask(ri, ci, 0), l_ab[u], 0.0) for u in un]
        lvl = 1
        while (1 << lvl) < C:
            lm = _level_mask(ri, ci, lvl)
            tn = [_dot(t_inv[u], bdiag(jnp.where(lm, l_ab[u], 0.0))) for u in un]
            t_inv = [t_inv[u] - _dot(tn[u], bdiag(t_inv[u])) for u in un]
            lvl += 1
        a_til = [_dot(t_inv[u], bdiag(al_t[u])) for u in un]
        b_til = [_dot(t_inv[u], bdiag(lakv[u])) for u in un]
        r_hat = [r_t[u] - _dot(a_rb[u], bdiag(a_til[u])) for u in un]
        o_hat = [arkv[u] - _dot(a_rb[u], bdiag(b_til[u])) for u in un]
        g_bd = [jnp.where(same_head, jnp.where(eye_b, p_last[u], 0.0) - _dot(be_et[u], a_til[u]), 0.0) for u in un]
        h_bd = [jnp.where(same_head, kev[u] - _dot(be_et[u], b_til[u]), 0.0) for u in un]
        lhs_m = [jnp.concatenate([r_hat[u], g_bd[u]], axis=0) for u in un]
        for u, (s, p) in enumerate(units):
            res = _dot(lhs_m[u], m_scr[p])
            m_scr[p] = res[C:] + h_bd[u]
            o_p = res[:C] + o_hat[u]
            cen = o_p - head_sum(o_p) * (1.0 / N)
            var = head_sum(cen * cen) * (1.0 / N)
            on_scr[sls[s], lanes[p]] = cen * lax.rsqrt(var + RW_GN_EPS)
        return carry

    lax.fori_loop(0, tb // (C * n_sub), chunk_body, 0)
    out = (on_scr[...] * lnw_ref[...] + lnb_ref[...] + bon_scr[...]) * g
    o_ref[0] = out.astype(o_ref.dtype)

    @pl.when(t_idx == pl.num_programs(1) - 1)
    def _():
        for p in range(RW_HEADS // 2):
            m = m_scr[p]
            sout_ref[0, 2 * p] = m[:N, :N].T
            sout_ref[0, 2 * p + 1] = m[N:, N:].T


def _rwkv_seq(z3, shift0, s0, rw, tb):
    b, l, _ = z3.shape
    const = lambda shape: pl.BlockSpec(shape, lambda i, j: (0,) * len(shape))
    wide = lambda: pltpu.VMEM((tb, RW_WIDTH), F32)
    return pl.pallas_call(
        _rwkv_seq_kernel,
        grid=(b, l // tb),
        in_specs=[
            pl.BlockSpec((1, tb, RW_PROJ), lambda i, j: (i, j, 0)),
            pl.BlockSpec((1, 1, RW_PROJ), lambda i, j: (i, 0, 0)),
            pl.BlockSpec((1, RW_HEADS, RW_N, RW_N), lambda i, j: (i, 0, 0, 0)),
            const((1, RW_PROJ)),
            const((1, RW_WIDTH)), const((128, RW_WIDTH)),
            const((1, RW_WIDTH)), const((128, RW_WIDTH)),
            const((128, RW_WIDTH)),
            const((1, RW_WIDTH)), const((1, RW_WIDTH)), const((1, RW_WIDTH)),
            const((1, RW_WIDTH)), const((1, RW_WIDTH)),
        ],
        out_specs=[
            pl.BlockSpec((1, tb, RW_WIDTH), lambda i, j: (i, j, 0)),
            pl.BlockSpec((1, RW_HEADS, RW_N, RW_N), lambda i, j: (i, 0, 0, 0)),
        ],
        out_shape=[
            jax.ShapeDtypeStruct((b, l, RW_WIDTH), BF16),
            jax.ShapeDtypeStruct((b, RW_HEADS, RW_N, RW_N), F32),
        ],
        scratch_shapes=[
            pltpu.VMEM((RW_HEADS // 2, 2 * RW_N, 2 * RW_N), F32),
            pltpu.VMEM((1, RW_PROJ), F32),
            wide(), wide(), wide(), wide(), wide(), wide(), wide(), wide(),
        ],
        compiler_params=_params(("parallel", "arbitrary")),
    )(z3, shift0, s0, rw["mu"], rw["w0"], rw["w2p"], rw["a0"], rw["a2p"], rw["g2"], rw["k_k"], rw["k_a"],
      rw["r_k"], rw["ln_w"], rw["ln_b"])


def _rwkv_step_prep_kernel(z_ref, shift0_ref, mu_ref, w0_ref, w2_ref, a0_ref, a2_ref, g2_ref, kk_ref, ka_ref,
                           rk_ref, r_ref, k_ref, v_ref, al_ref, be_ref, dec_ref, g_ref, bon_ref):
    z = z_ref[...]
    zs = z + mu_ref[...] * (shift0_ref[...] - z)
    r, k, v, kk_raw, a, log_decay, g = _rwkv_features(
        zs, w0_ref[...], w2_ref[...], a0_ref[...], a2_ref[...], g2_ref[...], kk_ref[...], ka_ref[...])
    rk_all = rk_ref[...]
    for h in range(RW_HEADS):
        hs = slice(h * RW_N, (h + 1) * RW_N)
        kk_h = kk_raw[:, hs]
        nrm = jnp.sqrt(jnp.sum(kk_h * kk_h, axis=-1, keepdims=True))
        al = kk_h / jnp.maximum(nrm, 1e-12)
        al_ref[:, hs] = al
        be_ref[:, hs] = al * a[:, hs]
        bon_ref[:, hs] = jnp.sum(r[:, hs] * k[:, hs] * rk_all[:, hs], axis=-1, keepdims=True) * v[:, hs]
    r_ref[...] = r
    k_ref[...] = k
    v_ref[...] = v
    dec_ref[...] = jnp.exp(log_decay)
    g_ref[...] = g


def _rwkv_step_prep(z, shift0, rw):
    n = z.shape[0]
    out = jax.ShapeDtypeStruct((n, RW_WIDTH), F32)
    return pl.pallas_call(
        _rwkv_step_prep_kernel,
        out_shape=[out] * 8,
        compiler_params=pltpu.CompilerParams(vmem_limit_bytes=VMEM_LIMIT),
    )(z, shift0, rw["mu"], rw["w0"], rw["w2p"], rw["a0"], rw["a2p"], rw["g2"], rw["k_k"], rw["k_a"], rw["r_k"])


def _rwkv_step_kernel(s_ref, r_ref, k_ref, al_ref, be_ref, dec_ref, v_ref, snew_ref, o_ref):
    r, k, al, be, dec = r_ref[...], k_ref[...], al_ref[...], be_ref[...], dec_ref[...]

    def body(g, carry):
        rows = pl.ds(pl.multiple_of(g * 8, 8), 8)
        v8 = v_ref[rows, :]
        outs = []
        for j in range(8):
            s = s_ref[0, g * 8 + j]
            sa = -jnp.sum(s * al, axis=0, keepdims=True)
            s_new = s * dec + sa * be + v8[j : j + 1, :] * k
            snew_ref[0, g * 8 + j] = s_new
            outs.append(jnp.sum(s_new * r, axis=0, keepdims=True))
        o_ref[rows, :] = jnp.concatenate(outs, axis=0)
        return carry

    lax.fori_loop(0, RW_N // 8, body, 0)


def _rwkv_step(s0, r, k, al, be, dec, v):
    n = s0.shape[0]
    s_t = jnp.transpose(s0, (1, 2, 3, 0))
    s_spec = pl.BlockSpec((1, RW_N, RW_N, n), lambda h: (h, 0, 0, 0))
    op_spec = pl.BlockSpec((RW_N, n), lambda h: (h, 0))
    s_new_t, o_t = pl.pallas_call(
        _rwkv_step_kernel,
        grid=(RW_HEADS,),
        in_specs=[s_spec] + [op_spec] * 6,
        out_specs=[s_spec, op_spec],
        out_shape=[
            jax.ShapeDtypeStruct(s_t.shape, F32),
            jax.ShapeDtypeStruct((RW_WIDTH, n), F32),
        ],
        compiler_params=_params(("parallel",)),
    )(s_t, r.T, k.T, al.T, be.T, dec.T, v.T)
    return jnp.transpose(s_new_t, (3, 0, 1, 2)), o_t.T


def _gla_features(z, gkw, gkb):
    q = z[:, 0:GLA_QK] * (GLA_DK ** -0.5)
    k = z[:, GLA_QK : 2 * GLA_QK]
    v = z[:, 2 * GLA_QK : 2 * GLA_QK + GLA_WIDTH]
    g = z[:, 2 * GLA_QK + GLA_WIDTH : 2 * GLA_QK + 2 * GLA_WIDTH]
    zgk = z[:, 2 * GLA_QK + 2 * GLA_WIDTH :]
    gk = -_softplus(-(_dot(zgk.astype(BF16), gkw) + gkb)) / GLA_GATE_NORMALIZER
    return q, k, v, g, gk


def _gla_finish(o, g, norm_w):
    outs = []
    for h in range(GLA_HEADS):
        hs = slice(h * GLA_DV, (h + 1) * GLA_DV)
        o_h = o[:, hs]
        o_h = o_h * lax.rsqrt(jnp.mean(o_h * o_h, axis=-1, keepdims=True) + NORM_EPS) * norm_w
        g_h = g[:, hs]
        outs.append(o_h * (g_h * _sigmoid(g_h)))
    return jnp.concatenate(outs, axis=-1)


def _gla_seq_kernel(z_ref, s0_ref, gkw_ref, gkb_ref, nw_ref, wsel_ref, o_ref, sout_ref,
                    st_scr, x_scr, gc_scr, oi_scr):
    C = GLA_CHUNK
    G = 128
    t_idx = pl.program_id(1)
    tb = z_ref.shape[1]
    nc = tb // C
    zero_vk = jnp.zeros((GLA_DV, GLA_DK), F32)

    @pl.when(t_idx == 0)
    def _():
        for p in range(GLA_HEADS // 2):
            top = jnp.concatenate([s0_ref[0, 2 * p].T, zero_vk], axis=1)
            bot = jnp.concatenate([zero_vk, s0_ref[0, 2 * p + 1].T], axis=1)
            st_scr[p] = jnp.concatenate([top, bot], axis=0)

    q, k, v, g, gk = _gla_features(z_ref[0], gkw_ref[...], gkb_ref[...])
    ri = lax.broadcasted_iota(jnp.int32, (G, G), 0)
    ci = lax.broadcasted_iota(jnp.int32, (G, G), 1)
    cum_mat = ((ri // C == ci // C) & (ri >= ci)).astype(F32)
    for m in range(tb // G):
        rows = slice(m * G, (m + 1) * G)
        gc_scr[rows, :] = _dot(cum_mat, gk[rows, :], precision=HIGHEST)
    gcum = gc_scr[...]

    rg = lax.broadcasted_iota(jnp.int32, (tb, 2 * G), 0)
    cg = lax.broadcasted_iota(jnp.int32, (tb, 2 * G), 1)
    blk_mask = ((cg % G) // C == (rg % G) // C) & (cg % C <= rg % C)
    for p in range(GLA_HEADS // 2):
        ls = slice(p * 128, (p + 1) * 128)
        q3 = q[:, ls].reshape(nc, C, 128)
        k3 = k[:, ls].reshape(nc, C, 128)
        g3 = gcum[:, ls].reshape(nc, C, 128) * LOG2_E
        half = C // 2
        for j in range(C):
            lo = 0 if j < half else half
            e = (q3[:, lo:] * jnp.exp2(jnp.minimum(g3[:, lo:] - g3[:, j : j + 1, :], 0.0))) * k3[:, j : j + 1, :]
            if lo:
                e = jnp.concatenate([jnp.zeros((nc, lo, 128), F32), e], axis=1)
            x_scr[:, j * 128 : (j + 1) * 128] = e.reshape(tb, 128).astype(BF16)
        a_t = jnp.where(blk_mask, _dot(x_scr[...], wsel_ref[...]), 0.0).astype(BF16)
        for hl in range(2):
            h = 2 * p + hl
            for m in range(tb // G):
                rows = slice(m * G, (m + 1) * G)
                a_blk = a_t[rows, hl * G : (hl + 1) * G]
                oi_scr[rows, h * GLA_DV : (h + 1) * GLA_DV] = _dot(
                    a_blk, v[rows, h * GLA_DV : (h + 1) * GLA_DV].astype(BF16))

    CG = G // C
    rt = lax.broadcasted_iota(jnp.int32, (G, CG * 128), 0)
    ct = lax.broadcasted_iota(jnp.int32, (G, CG * 128), 1)
    own_chunk = rt // C == ct // 128
    rs = lax.broadcasted_iota(jnp.int32, (2 * GLA_DV, CG * 128), 0)
    cs = lax.broadcasted_iota(jnp.int32, (2 * GLA_DV, CG * 128), 1)
    same_head = rs // GLA_DV == (cs % 128) // GLA_DK

    def chunk_diag(x):
        return jnp.where(own_chunk, jnp.concatenate([x] * CG, axis=1), 0.0)

    for m in range(tb // G):
        rows = slice(m * G, (m + 1) * G)
        for p in range(GLA_HEADS // 2):
            ls = slice(p * 128, (p + 1) * 128)
            vs = slice(p * 2 * GLA_DV, (p + 1) * 2 * GLA_DV)
            g_g = gcum[rows, ls]
            g3 = g_g.reshape(CG, C, 128)
            g_last = jnp.broadcast_to(g3[:, C - 1 : C, :], (CG, C, 128)).reshape(G, 128)
            q_t = q[rows, ls] * jnp.exp(g_g)
            k_t = k[rows, ls] * jnp.exp(g_last - g_g)
            d_s = jnp.where(same_head, _dot_tn(v[rows, vs], chunk_diag(k_t)), 0.0)
            st = st_scr[p]
            starts = []
            for c in range(CG):
                starts.append(st)
                decay = jnp.exp(g_g[c * C + C - 1 : c * C + C, :])
                st = st * decay + d_s[:, c * 128 : (c + 1) * 128]
            st_scr[p] = st
            oi_scr[rows, vs] += _dot_nt(chunk_diag(q_t), jnp.concatenate(starts, axis=1))

    o_ref[0] = _gla_finish(oi_scr[...], g, nw_ref[...]).astype(o_ref.dtype)

    @pl.when(t_idx == pl.num_programs(1) - 1)
    def _():
        for p in range(GLA_HEADS // 2):
            st = st_scr[p]
            sout_ref[0, 2 * p] = st[:GLA_DV, :GLA_DK].T
            sout_ref[0, 2 * p + 1] = st[GLA_DV:, GLA_DK:].T


def _gla_select_matrix():
    j = jnp.arange(GLA_CHUNK)[:, None, None]
    hl = jnp.arange(2)[None, :, None]
    rows_j = jnp.broadcast_to(j, (GLA_CHUNK, 2, GLA_DK)).reshape(-1)
    rows_h = jnp.broadcast_to(hl, (GLA_CHUNK, 2, GLA_DK)).reshape(-1)
    cols = jnp.arange(256)
    sel = (rows_j[:, None] == cols[None, :] % GLA_CHUNK) & (rows_h[:, None] == cols[None, :] // 128)
    return sel.astype(BF16)


def _gla_seq(z3, s0, gl, tb):
    b, l, _ = z3.shape
    const = lambda shape: pl.BlockSpec(shape, lambda i, j: (0,) * len(shape))
    return pl.pallas_call(
        _gla_seq_kernel,
        grid=(b, l // tb),
        in_specs=[
            pl.BlockSpec((1, tb, GLA_PROJ_PAD), lambda i, j: (i, j, 0)),
            pl.BlockSpec((1, GLA_HEADS, GLA_DK, GLA_DV), lambda i, j: (i, 0, 0, 0)),
            const((GLA_LORA_PAD, GLA_QK)), const((1, GLA_QK)), const((1, GLA_DV)),
            const((GLA_CHUNK * 128, 256)),
        ],
        out_specs=[
            pl.BlockSpec((1, tb, GLA_WIDTH), lambda i, j: (i, j, 0)),
            pl.BlockSpec((1, GLA_HEADS, GLA_DK, GLA_DV), lambda i, j: (i, 0, 0, 0)),
        ],
        out_shape=[
            jax.ShapeDtypeStruct((b, l, GLA_WIDTH), BF16),
            jax.ShapeDtypeStruct((b, GLA_HEADS, GLA_DK, GLA_DV), F32),
        ],
        scratch_shapes=[
            pltpu.VMEM((GLA_HEADS // 2, 2 * GLA_DV, 2 * GLA_DK), F32),
            pltpu.VMEM((tb, GLA_CHUNK * 128), BF16),
            pltpu.VMEM((tb, GLA_QK), F32), pltpu.VMEM((tb, GLA_WIDTH), F32),
        ],
        compiler_params=_params(("parallel", "arbitrary")),
    )(z3, s0, gl["gkw"], gl["gkb"], gl["norm_w"], _gla_select_matrix())


def _gla_step_prep_kernel(z_ref, gkw_ref, gkb_ref, q_ref, k_ref, v_ref, g_ref, dec_ref):
    q, k, v, g, gk = _gla_features(z_ref[...], gkw_ref[...], gkb_ref[...])
    q_ref[...] = q
    k_ref[...] = k
    v_ref[...] = v
    g_ref[...] = g
    dec_ref[...] = jnp.exp(gk)


def _gla_step_prep(z, gl):
    n = z.shape[0]
    qk = jax.ShapeDtypeStruct((n, GLA_QK), F32)
    wide = jax.ShapeDtypeStruct((n, GLA_WIDTH), F32)
    return pl.pallas_call(
        _gla_step_prep_kernel,
        out_shape=[qk, qk, wide, wide, qk],
        compiler_params=pltpu.CompilerParams(vmem_limit_bytes=VMEM_LIMIT),
    )(z, gl["gkw"], gl["gkb"])


def _gla_step_kernel(s_ref, q_ref, k_ref, dec_ref, v_ref, snew_ref, o_ref):
    bb = s_ref.shape[0]
    rows = lax.broadcasted_iota(jnp.int32, (bb, bb * GLA_DK), 0)
    cols = lax.broadcasted_iota(jnp.int32, (bb, bb * GLA_DK), 1)
    own = rows == cols // GLA_DK
    ones = jnp.ones((bb, GLA_DV), F32)
    zeros = jnp.zeros((bb, GLA_DV), F32)

    def seq_diag(x):
        return jnp.where(own, jnp.concatenate([x] * bb, axis=1), 0.0)

    for h in range(GLA_HEADS):
        ks = slice(h * GLA_DK, (h + 1) * GLA_DK)
        vs = slice(h * GLA_DV, (h + 1) * GLA_DV)
        s = s_ref[:, h].reshape(bb * GLA_DK, GLA_DV)
        v = v_ref[:, vs]
        lhs_t = jnp.concatenate([seq_diag(k_ref[:, ks]), seq_diag(dec_ref[:, ks])], axis=0)
        rhs = jnp.concatenate([jnp.concatenate([v, zeros], axis=1), jnp.concatenate([zeros, ones], axis=1)], axis=0)
        both = _dot_tn(lhs_t, rhs, precision=HIGHEST)
        s_new = s * both[:, GLA_DV:] + both[:, :GLA_DV]
        snew_ref[:, h] = s_new.reshape(bb, GLA_DK, GLA_DV)
        o_ref[:, vs] = _dot(seq_diag(q_ref[:, ks]), s_new)


def _gla_step(s0, q, k, dec, v, bb):
    n = s0.shape[0]
    s_spec = pl.BlockSpec((bb, GLA_HEADS, GLA_DK, GLA_DV), lambda i: (i, 0, 0, 0))
    qk_spec = pl.BlockSpec((bb, GLA_QK), lambda i: (i, 0))
    v_spec = pl.BlockSpec((bb, GLA_WIDTH), lambda i: (i, 0))
    return pl.pallas_call(
        _gla_step_kernel,
        grid=(n // bb,),
        in_specs=[s_spec, qk_spec, qk_spec, qk_spec, v_spec],
        out_specs=[s_spec, v_spec],
        out_shape=[jax.ShapeDtypeStruct(s0.shape, F32), jax.ShapeDtypeStruct((n, GLA_WIDTH), F32)],
        compiler_params=_params(("parallel",)),
    )(s0, q, k, dec, v)


def _step_post_kernel(orw_ref, bon_ref, grw_ref, lnw_ref, lnb_ref, ogl_ref, ggl_ref, nw_ref, o_rw_ref, o_gl_ref):
    o = orw_ref[...]
    for h in range(RW_HEADS):
        hs = slice(h * RW_N, (h + 1) * RW_N)
        o_h = o[:, hs]
        mean = jnp.mean(o_h, axis=-1, keepdims=True)
        cen = o_h - mean
        var = jnp.mean(cen * cen, axis=-1, keepdims=True)
        on = cen * lax.rsqrt(var + RW_GN_EPS)
        res = (on * lnw_ref[:, hs] + lnb_ref[:, hs] + bon_ref[:, hs]) * grw_ref[:, hs]
        o_rw_ref[:, hs] = res.astype(o_rw_ref.dtype)
    o_gl_ref[...] = _gla_finish(ogl_ref[...], ggl_ref[...], nw_ref[...]).astype(o_gl_ref.dtype)


def _step_post(o_rw, bonus, g_rw, rw, o_gl, g_gl, gl):
    n = o_rw.shape[0]
    return pl.pallas_call(
        _step_post_kernel,
        out_shape=[jax.ShapeDtypeStruct((n, RW_WIDTH), BF16), jax.ShapeDtypeStruct((n, GLA_WIDTH), BF16)],
        compiler_params=pltpu.CompilerParams(vmem_limit_bytes=VMEM_LIMIT),
    )(o_rw, bonus, g_rw, rw["ln_w"], rw["ln_b"], o_gl, g_gl, gl["norm_w"])


def _outproj_router_kernel(x_ref, orw_ref, ogl_ref, wo_ref, gain_ref, wrt_ref, br_ref, before_ref,
                           h_ref, xn_ref, idx_ref, gate_ref, rank_ref, cnt_ref):
    mix = jnp.concatenate([orw_ref[...], ogl_ref[...]], axis=-1)
    h = x_ref[...] + _dot(mix, wo_ref[...])
    h_ref[...] = h
    xn = h * lax.rsqrt(jnp.mean(h * h, axis=-1, keepdims=True) + NORM_EPS) * gain_ref[...]
    _store_row_tiles(xn_ref, xn)
    logits = _dot_nt(wrt_ref[...], xn) + br_ref[...]
    eidx = lax.broadcasted_iota(jnp.int32, logits.shape, 0)
    vals, idxs = [], []
    work = logits
    chosen = jnp.zeros(logits.shape, F32)
    for _ in range(TOP_K):
        m = jnp.max(work, axis=0, keepdims=True)
        sel = jnp.min(jnp.where(work == m, eidx, N_EXPERTS), axis=0, keepdims=True)
        hit = eidx == sel
        work = jnp.where(hit, -jnp.inf, work)
        chosen = chosen + hit.astype(F32)
        vals.append(m)
        idxs.append(sel)
    prefix = _dot(chosen.astype(BF16), before_ref[...])
    exps = [jnp.exp(v - vals[0]) for v in vals]
    denom = exps[0] + exps[1] + exps[2] + exps[3]
    for j in range(TOP_K):
        idx_ref[0, j : j + 1, :] = idxs[j]
        gate_ref[0, j : j + 1, :] = exps[j] / denom
        rank = jnp.sum(jnp.where(eidx == idxs[j], prefix, 0.0), axis=0, keepdims=True)
        rank_ref[0, j : j + 1, :] = rank.astype(jnp.int32)
    cnt = jnp.sum(chosen, axis=1, keepdims=True)
    cnt_ref[0] = jnp.broadcast_to(cnt, (N_EXPERTS, 128)).astype(jnp.int32)


def _outproj_router(x, o_rw, o_gl, w_out, gain, w_router_t, b_router, tm):
    n = x.shape[0]
    nt = n // tm
    const = lambda shape: pl.BlockSpec(shape, lambda i: (0,) * len(shape))
    tok = lambda width: pl.BlockSpec((tm, width), lambda i: (i, 0))
    lane = pl.BlockSpec((1, TOP_K, tm), lambda i: (i, 0, 0))
    t = jnp.arange(tm, dtype=jnp.int32)
    before = (t[:, None] < t[None, :]).astype(BF16)
    return pl.pallas_call(
        _outproj_router_kernel,
        grid=(nt,),
        in_specs=[
            tok(D_MODEL), tok(RW_WIDTH), tok(GLA_WIDTH),
            const((D_MODEL, D_MODEL)), const((1, D_MODEL)), const((N_EXPERTS, D_MODEL)), const((N_EXPERTS, 1)),
            const((tm, tm)),
        ],
        out_specs=[tok(D_MODEL), pl.BlockSpec((tm * ROW_TILE, 128), lambda i: (i, 0)), lane, lane, lane,
                   pl.BlockSpec((1, N_EXPERTS, 128), lambda i: (i, 0, 0))],
        out_shape=[
            jax.ShapeDtypeStruct((n, D_MODEL), F32),
            jax.ShapeDtypeStruct((n * ROW_TILE, 128), F32),
            jax.ShapeDtypeStruct((nt, TOP_K, tm), jnp.int32),
            jax.ShapeDtypeStruct((nt, TOP_K, tm), F32),
            jax.ShapeDtypeStruct((nt, TOP_K, tm), jnp.int32),
            jax.ShapeDtypeStruct((nt, N_EXPERTS, 128), jnp.int32),
        ],
        compiler_params=_params(("parallel",)),
    )(x, o_rw, o_gl, w_out, gain, w_router_t, b_router, before)


def _moe_kernel(be_ref, nu_ref, epoch_ref, next_ref, rows_ref, parts_ref, xs_ref, wg_hbm, wu_hbm, wd_hbm, bg_ref, bu_ref, bd_ref,
                y_ref, w_f32, wg_b, wu_b, wd_b, sems):
    b = pl.program_id(0)
    prev = be_ref[jnp.maximum(b - 1, 0)]
    new_expert = (b == 0) | (be_ref[b] != prev)

    def fetch(e, slot, i):
        w = (wg_hbm, wu_hbm, wd_hbm)[i]
        return pltpu.make_async_copy(w.at[e], w_f32.at[slot, i], sems.at[slot])

    @pl.when(b == 0)
    def _():
        for i in range(3):
            fetch(be_ref[0], 0, i).start()

    @pl.when(new_expert)
    def _():
        slot = epoch_ref[b] % 2
        for i in range(3):
            fetch(be_ref[b], slot, i).wait()
        wg_b[...] = w_f32[slot, 0].astype(BF16)
        wu_b[...] = w_f32[slot, 1].astype(BF16)
        wd_b[...] = w_f32[slot, 2].astype(BF16)

    first_part, end_part = parts_ref[b] // 4, parts_ref[b] % 4
    for i in range(3):
        @pl.when((next_ref[b] >= 0) & (first_part <= i) & (i < end_part))
        def _():
            fetch(next_ref[b], 1 - epoch_ref[b] % 2, i).start()

    def ffn(m):
        x = _load_row_tiles(xs_ref, m).astype(BF16)
        half = D_MODEL // 2
        acc = None
        for f in range(2):
            fs = slice(f * half, (f + 1) * half)
            gt = _dot(x, wg_b[:, fs]) + bg_ref[0, :, fs]
            up = _dot(x, wu_b[:, fs]) + bu_ref[0, :, fs]
            gt = jnp.minimum(gt, SWIGLU_LIMIT)
            up = jnp.clip(up, -SWIGLU_LIMIT, SWIGLU_LIMIT)
            hid = (up + 1.0) * gt * _sigmoid(SWIGLU_ALPHA * gt)
            part = _dot(hid.astype(BF16), wd_b[fs, :])
            acc = part if acc is None else acc + part
        _store_row_tiles(y_ref, acc + bd_ref[0])

    pieces = (rows_ref[b] + MOE_PIECE - 1) // MOE_PIECE
    for q in range(1, MOE_BLOCK // MOE_PIECE + 1):
        @pl.when((b < nu_ref[0]) & (pieces == q))
        def _():
            ffn(q * MOE_PIECE)


def _moe_ffn(block_expert, n_used, block_rows, xs, w_gate, w_up, w_down, b_gate, b_up, b_down):
    n_blocks = block_expert.shape[0]
    pos = jnp.arange(n_blocks, dtype=jnp.int32)
    change = (pos > 0) & (block_expert != jnp.roll(block_expert, 1))
    epoch = jnp.cumsum(change.astype(jnp.int32))
    later = change[None, :] & (pos[None, :] > pos[:, None])
    first = jnp.min(jnp.where(later, pos[None, :], n_blocks), axis=1)
    next_e = jnp.sum(jnp.where(pos[None, :] == first[:, None], block_expert[None, :], 0), axis=1)
    next_e = jnp.where(first < n_blocks, next_e, -1).astype(jnp.int32)
    run_start = jnp.max(jnp.where((pos[None, :] <= pos[:, None]) & (change | (pos == 0))[None, :], pos[None, :], 0),
                        axis=1)
    q = jnp.minimum(pos - run_start, 3)
    used = pos < n_used[0]
    is_last = jnp.roll(change, -1) | (pos == n_used[0] - 1)
    parts = jnp.where(used, q * 4 + jnp.where(is_last, 3, jnp.minimum(q + 1, 3)), 15).astype(jnp.int32)

    row = lambda b, be, nu, ep, nx, br, pt: (jnp.minimum(b, nu[0] - 1), 0)
    bspec = pl.BlockSpec((1, 1, D_MODEL), lambda b, be, nu, ep, nx, br, pt: (be[b], 0, 0))
    wspec = pl.BlockSpec(memory_space=pl.ANY)
    grid_spec = pltpu.PrefetchScalarGridSpec(
        num_scalar_prefetch=6,
        grid=(n_blocks,),
        in_specs=[pl.BlockSpec((MOE_BLOCK * ROW_TILE, 128), row), wspec, wspec, wspec, bspec, bspec, bspec],
        out_specs=pl.BlockSpec((MOE_BLOCK * ROW_TILE, 128), row),
        scratch_shapes=[pltpu.VMEM((2, 3, D_MODEL, D_MODEL), F32)] + [pltpu.VMEM((D_MODEL, D_MODEL), BF16)] * 3
        + [pltpu.SemaphoreType.DMA((2,))],
    )
    return pl.pallas_call(
        _moe_kernel,
        grid_spec=grid_spec,
        out_shape=jax.ShapeDtypeStruct((n_blocks * MOE_BLOCK * ROW_TILE, 128), F32),
        compiler_params=_params(("arbitrary",)),
    )(block_expert, n_used, epoch, next_e, block_rows, parts, xs, w_gate, w_up, w_down,
      b_gate.reshape(N_EXPERTS, 1, D_MODEL), b_up.reshape(N_EXPERTS, 1, D_MODEL),
      b_down.reshape(N_EXPERTS, 1, D_MODEL))


SEG_ALIGN = 8
GROUP_ROWS = SEG_ALIGN * ROW_TILE


def _local_rows(tm):
    return tm * TOP_K + N_EXPERTS * SEG_ALIGN


BIG_COPY = 4


def _copy_tables():
    n_big = _local_rows(TOK_BLOCK) // SEG_ALIGN // BIG_COPY
    n_small = N_EXPERTS * (BIG_COPY - 1)
    return 2 * n_big, 2 * (n_big + n_small)


def _issue_group_copies(cp_ref, hbm, buf, sem, to_hbm):
    small_at, counts_at = _copy_tables()

    def copy(at, n_groups):
        rows = n_groups * GROUP_ROWS
        b = buf.at[pl.ds(pl.multiple_of(cp_ref[0, 0, at] * GROUP_ROWS, GROUP_ROWS), rows)]
        h = hbm.at[pl.ds(pl.multiple_of(cp_ref[0, 0, at + 1] * GROUP_ROWS, GROUP_ROWS), rows)]
        return pltpu.make_async_copy(b, h, sem) if to_hbm else pltpu.make_async_copy(h, b, sem)

    def big(i, carry):
        copy(2 * i, BIG_COPY).start()
        return carry

    def small(i, carry):
        copy(small_at + 2 * i, 1).start()
        return carry

    lax.fori_loop(0, cp_ref[0, 0, counts_at], big, 0)
    lax.fori_loop(0, cp_ref[0, 0, counts_at + 1], small, 0)


def _wait_group_copies(cp_ref, hbm, buf, sem, to_hbm):
    rows = pl.ds(0, pl.multiple_of(cp_ref[0, 0, _copy_tables()[1] + 2] * GROUP_ROWS, GROUP_ROWS))
    b, h = buf.at[rows], hbm.at[rows]
    (pltpu.make_async_copy(b, h, sem) if to_hbm else pltpu.make_async_copy(h, b, sem)).wait()


def _combine_kernel(grp_c, grp_n, lpos_ref, gate_ref, h_ref, gain_ref, y_hbm, o_ref, ybuf, fbuf, sems):
    i = pl.program_id(0)
    nt = pl.num_programs(0)
    tm = h_ref.shape[0]

    @pl.when(i == 0)
    def _():
        _issue_group_copies(grp_c, y_hbm, ybuf.at[0], sems.at[0], False)

    @pl.when(i + 1 < nt)
    def _():
        _issue_group_copies(grp_n, y_hbm, ybuf.at[(i + 1) % 2], sems.at[(i + 1) % 2], False)

    slot = i % 2
    yb = ybuf.at[slot]
    _wait_group_copies(grp_c, y_hbm, yb, sems.at[slot], False)

    def token_body(t, carry):
        acc = None
        for j in range(TOP_K):
            row = pl.multiple_of(lpos_ref[0, 0, t * TOP_K + j], ROW_TILE)
            term = gate_ref[0, 0, t * TOP_K + j] * yb[pl.ds(row, ROW_TILE), :]
            acc = term if acc is None else acc + term
        fbuf[pl.ds(pl.multiple_of(t * ROW_TILE, ROW_TILE), ROW_TILE), :] = acc
        return carry

    lax.fori_loop(0, tm, token_body, 0, unroll=8)
    f = h_ref[...] + _load_row_tiles(fbuf, tm)
    o_ref[...] = f * lax.rsqrt(jnp.mean(f * f, axis=-1, keepdims=True) + NORM_EPS) * gain_ref[...]


def _combine(h, y_rows, grp3, lpos3, gate3, gain, tm):
    n = h.shape[0]
    nt = n // tm
    n_local = _local_rows(tm)
    gw = grp3.shape[-1]
    smem = lambda shape, imap: pl.BlockSpec(shape, imap, memory_space=pltpu.SMEM)
    cur = lambda i: (i, 0, 0)
    nxt = lambda i: (jnp.minimum(i + 1, nt - 1), 0, 0)
    return pl.pallas_call(
        _combine_kernel,
        grid=(nt,),
        in_specs=[
            smem((1, 1, gw), cur), smem((1, 1, gw), nxt),
            smem((1, 1, TOP_K * tm), cur), smem((1, 1, TOP_K * tm), cur),
            pl.BlockSpec((tm, D_MODEL), lambda i: (i, 0)),
            pl.BlockSpec((1, D_MODEL), lambda i: (0, 0)),
            pl.BlockSpec(memory_space=pl.ANY),
        ],
        out_specs=pl.BlockSpec((tm, D_MODEL), lambda i: (i, 0)),
        out_shape=jax.ShapeDtypeStruct((n, D_MODEL), F32),
        scratch_shapes=[pltpu.VMEM((2, n_local * ROW_TILE, 128), F32), pltpu.VMEM((tm * ROW_TILE, 128), F32),
                        pltpu.SemaphoreType.DMA((2,))],
        compiler_params=pltpu.CompilerParams(dimension_semantics=("arbitrary",), vmem_limit_bytes=VMEM_LIMIT,
                                             disable_bounds_checks=True),
    )(grp3, grp3, lpos3, gate3, h, gain, y_rows)


def _dispatch_kernel(*refs, fill):
    if fill:
        grp_ref, grp_prev, lpos_ref, ends_ref, x_ref, xs_hbm, sorted_buf, zero_scr, sems, zsem = refs
    else:
        grp_ref, grp_prev, lpos_ref, x_ref, _, xs_hbm, sorted_buf, sems = refs
    i = pl.program_id(0)
    tm = x_ref.shape[0] // ROW_TILE
    blk = MOE_PIECE * ROW_TILE
    sorted_scr = sorted_buf.at[i % 2]
    sem = sems.at[i % 2]

    if fill:
        def fill_copy(e):
            start = pl.multiple_of(ends_ref[0, e] * ROW_TILE, blk)
            return pltpu.make_async_copy(zero_scr, xs_hbm.at[pl.ds(start, blk)], zsem)

        @pl.when(i == 0)
        def _():
            zero_scr[...] = jnp.zeros(zero_scr.shape, zero_scr.dtype)
            for e in range(N_EXPERTS):
                @pl.when(ends_ref[1, e] > 0)
                def _():
                    fill_copy(e).start()

    sorted_scr[...] = jnp.zeros(sorted_scr.shape, sorted_scr.dtype)

    def move(t, carry):
        row = x_ref[pl.ds(pl.multiple_of(t * ROW_TILE, ROW_TILE), ROW_TILE), :]
        for j in range(TOP_K):
            dst = pl.multiple_of(lpos_ref[0, 0, t * TOP_K + j], ROW_TILE)
            sorted_scr[pl.ds(dst, ROW_TILE), :] = row
        return carry

    lax.fori_loop(0, tm, move, 0, unroll=8)

    if fill:
        @pl.when(i == 0)
        def _():
            for e in range(N_EXPERTS):
                @pl.when(ends_ref[1, e] > 0)
                def _():
                    fill_copy(e).wait()

    @pl.when(i > 0)
    def _():
        _wait_group_copies(grp_prev, xs_hbm, sorted_buf.at[(i + 1) % 2], sems.at[(i + 1) % 2], True)

    _issue_group_copies(grp_ref, xs_hbm, sorted_scr, sem, True)

    @pl.when(i == pl.num_programs(0) - 1)
    def _():
        _wait_group_copies(grp_ref, xs_hbm, sorted_scr, sem, True)


def _dispatch(xn, grp3, lpos3, tm, n_slots, ends=None, xs=None):
    n = xn.shape[0] // ROW_TILE
    fill = xs is None
    smem = lambda shape, imap: pl.BlockSpec(shape, imap, memory_space=pltpu.SMEM)
    in_specs = [smem((1, 1, grp3.shape[-1]), lambda i: (i, 0, 0)),
                smem((1, 1, grp3.shape[-1]), lambda i: (jnp.maximum(i - 1, 0), 0, 0)),
                smem((1, 1, TOP_K * tm), lambda i: (i, 0, 0))]
    args = [grp3, grp3, lpos3]
    scratch = [pltpu.VMEM((2, _local_rows(tm) * ROW_TILE, 128), F32)]
    if fill:
        in_specs.append(smem((2, N_EXPERTS), lambda i: (0, 0)))
        args.append(ends)
        scratch.append(pltpu.VMEM((MOE_PIECE * ROW_TILE, 128), F32))
    in_specs.append(pl.BlockSpec((tm * ROW_TILE, 128), lambda i: (i, 0)))
    args.append(xn)
    aliases = {}
    if not fill:
        in_specs.append(pl.BlockSpec(memory_space=pl.ANY))
        args.append(xs)
        aliases = {len(args) - 1: 0}
    scratch.append(pltpu.SemaphoreType.DMA((2,)))
    if fill:
        scratch.append(pltpu.SemaphoreType.DMA(()))
    return pl.pallas_call(
        functools.partial(_dispatch_kernel, fill=fill),
        grid=(n // tm,),
        in_specs=in_specs,
        out_specs=pl.BlockSpec(memory_space=pl.ANY),
        out_shape=jax.ShapeDtypeStruct((n_slots * ROW_TILE, 128), F32),
        scratch_shapes=scratch,
        input_output_aliases=aliases,
        compiler_params=pltpu.CompilerParams(dimension_semantics=("arbitrary",), vmem_limit_bytes=VMEM_LIMIT,
                                             disable_bounds_checks=True, has_side_effects=True),
    )(*args)


def _pad_rows(w, rows, offset):
    out = jnp.zeros((rows, w.shape[1]), w.dtype)
    return out.at[offset : offset + w.shape[0]].set(w)


def _routing_tables(counts, n_pairs):
    n_tiles = counts.shape[0]
    n_blocks = (n_pairs + n_tiles * N_EXPERTS * (SEG_ALIGN - 1) + N_EXPERTS * (MOE_BLOCK - 1)
                + MOE_BLOCK - 1) // MOE_BLOCK
    runs = (counts + SEG_ALIGN - 1) // SEG_ALIGN * SEG_ALIGN
    local_start = jnp.cumsum(runs, axis=1) - runs
    total = jnp.sum(runs, axis=0)
    padded = (total + MOE_BLOCK - 1) // MOE_BLOCK * MOE_BLOCK
    pends = jnp.cumsum(padded)
    pstarts = pends - padded
    global_start = pstarts[None, :] + jnp.cumsum(runs, axis=0) - runs
    blocks = jnp.arange(n_blocks, dtype=jnp.int32) * MOE_BLOCK
    n_used = (pends[-1] // MOE_BLOCK).astype(jnp.int32)
    owner = jnp.sum((pends[None, :] <= blocks[:, None]).astype(jnp.int32), axis=1)
    block_expert = jnp.minimum(owner, N_EXPERTS - 1)
    last = jnp.sum(jnp.where(jnp.arange(n_blocks) == n_used - 1, block_expert, 0))
    block_expert = jnp.where(jnp.arange(n_blocks) < n_used, block_expert, last)
    real_end = pstarts + total
    ends = jnp.stack([real_end // MOE_PIECE * MOE_PIECE, real_end % MOE_PIECE]).astype(jnp.int32)
    row_end = jnp.sum(jnp.where(block_expert[:, None] == jnp.arange(N_EXPERTS), (pstarts + total)[None, :], 0), axis=1)
    block_rows = jnp.clip(row_end - blocks, 0, MOE_BLOCK).astype(jnp.int32)
    return (runs, local_start, global_start, ends, block_expert.astype(jnp.int32), n_used.reshape(1), block_rows,
            n_blocks)


def _local_positions(idx3, rank3, local_start):
    hit = idx3[..., None] == jnp.arange(N_EXPERTS, dtype=jnp.int32)
    lpos = rank3 + jnp.sum(jnp.where(hit, local_start[:, None, None, :], 0), axis=-1)
    return _token_major(lpos.astype(jnp.int32) * ROW_TILE)


def _copy_lists(runs, local_start, global_start):
    experts = jnp.arange(N_EXPERTS, dtype=jnp.int32)
    groups = runs // SEG_ALIGN
    l8, g8 = local_start // SEG_ALIGN, global_start // SEG_ALIGN
    n_big, n_small = groups // BIG_COPY, groups % BIG_COPY

    def copy_list(count, first_local, first_global, step, length):
        ends = jnp.cumsum(count, axis=1)
        o = jnp.arange(length, dtype=jnp.int32)
        owner = jnp.minimum(jnp.sum((ends[:, None, :] <= o[None, :, None]).astype(jnp.int32), axis=-1), N_EXPERTS - 1)
        sel = owner[..., None] == experts
        pick = lambda t: jnp.sum(jnp.where(sel, t[:, None, :], 0), axis=-1)
        k = (o[None, :] - pick(ends - count)) * step
        pairs = jnp.stack([pick(first_local) + k, pick(first_global) + k], axis=-1)
        live = (o[None, :] < ends[:, -1:])[..., None]
        return jnp.where(live, pairs, 0).reshape(count.shape[0], 2 * length)

    small_at, counts_at = _copy_tables()
    big = copy_list(n_big, l8, g8, BIG_COPY, small_at // 2)
    small = copy_list(n_small, l8 + n_big * BIG_COPY, g8 + n_big * BIG_COPY, 1, (counts_at - small_at) // 2)
    counts = jnp.stack([jnp.sum(n_big, axis=1), jnp.sum(n_small, axis=1), jnp.sum(groups, axis=1)], axis=1)
    counts = jnp.pad(counts, ((0, 0), (0, 5)))
    return jnp.concatenate([big, small, counts], axis=1).astype(jnp.int32)[:, None, :]


def _token_major(t3):
    return jnp.swapaxes(t3, 1, 2).reshape(t3.shape[0], 1, -1)


def kernel(x_prompt, x_sample, state_rwkv_shift, state_rwkv_wkv, state_gla, norm_mix, w_in, rw_mu, rw_w0, rw_w2, rw_a0, rw_a2, rw_g2, rw_k_k, rw_k_a, rw_r_k, rw_ln_w, rw_ln_b, gla_gk_w2, gla_gk_b, gla_norm_w, w_out, norm_ffn, w_router, b_router, w_gate, b_gate, w_up, b_up, w_down, b_down, norm_final):
    depth = norm_mix.shape[0]
    assert depth == 1
    bp, lp, d = x_prompt.shape
    bs, ls, _ = x_sample.shape
    assert ls == 1 and lp % SEQ_BLOCK == 0
    l = 0
    row = lambda t: t.reshape(1, -1)

    w_in_b = w_in[l].astype(BF16)
    w_in_r = w_in_b[:, :RW_PROJ]
    w_in_g = jnp.pad(w_in_b[:, RW_PROJ:], ((0, 0), (0, GLA_PROJ_PAD - GLA_PROJ)))
    rw = dict(
        mu=row(rw_mu[l]), w0=row(rw_w0[l]), a0=row(rw_a0[l]),
        w2p=_pad_rows(rw_w2[l].astype(BF16), 128, 0), a2p=_pad_rows(rw_a2[l].astype(BF16), 128, 64),
        g2=rw_g2[l].astype(BF16), k_k=row(rw_k_k[l]), k_a=row(rw_k_a[l]), r_k=row(rw_r_k[l]),
        ln_w=row(rw_ln_w[l]), ln_b=row(rw_ln_b[l]))
    gl = dict(gkw=_pad_rows(gla_gk_w2[l].astype(BF16), GLA_LORA_PAD, 0), gkb=row(gla_gk_b[l]),
              norm_w=row(gla_norm_w[l]))
    gain_mix = row(norm_mix[l])

    n_p = bp * lp
    xp = x_prompt.reshape(n_p, d)
    zr_p, zg_p = _inproj(xp, gain_mix, w_in_r, w_in_g, TOK_BLOCK)
    zr_p3 = zr_p.reshape(bp, lp, RW_PROJ)
    o_rw_p, wkv_p = _rwkv_seq(zr_p3, jnp.zeros((bp, 1, RW_PROJ), F32),
                              jnp.zeros((bp, RW_HEADS, RW_N, RW_N), F32), rw, SEQ_BLOCK)
    o_gl_p, gla_p = _gla_seq(zg_p.reshape(bp, lp, GLA_PROJ_PAD),
                             jnp.zeros((bp, GLA_HEADS, GLA_DK, GLA_DV), F32), gl, SEQ_BLOCK)
    shift_p = zr_p3[:, -1, :]

    xs_ = x_sample.reshape(bs, d)
    zr_s, zg_s = _inproj(xs_, gain_mix, w_in_r, w_in_g, bs)
    r, k, v, al, be, dec, g_rw, bonus = _rwkv_step_prep(zr_s, state_rwkv_shift[l], rw)
    wkv_s, o_rw_s = _rwkv_step(state_rwkv_wkv[l], r, k, al, be, dec, v)
    q, kg, vg, g_gl, dec_g = _gla_step_prep(zg_s, gl)
    gla_s, o_gl_s = _gla_step(state_gla[l], q, kg, dec_g, vg, 16)
    o_rw_s2, o_gl_s2 = _step_post(o_rw_s, bonus, g_rw, rw, o_gl_s, g_gl, gl)
    shift_s = zr_s

    w_out_b = w_out[l].astype(BF16)
    router = (w_out_b, row(norm_ffn[l]), w_router[l].T, b_router[l].reshape(N_EXPERTS, 1))
    h_p, xn_p, idx_p, gate_p, rank_p, cnt_p = _outproj_router(
        xp, o_rw_p.reshape(n_p, RW_WIDTH), o_gl_p.reshape(n_p, GLA_WIDTH), *router, TOK_BLOCK)
    h_s, xn_s, idx_s, gate_s, rank_s, cnt_s = _outproj_router(xs_, o_rw_s2, o_gl_s2, *router, bs)
    nt_p = n_p // TOK_BLOCK
    counts = jnp.concatenate([cnt_p[:, :, 0], cnt_s[:, :, 0]], axis=0)
    runs, lstart, gstart, ends, block_expert, n_used, block_rows, n_blocks = _routing_tables(
        counts, (n_p + bs) * TOP_K)
    n_slots = n_blocks * MOE_BLOCK
    lpos_p = _local_positions(idx_p, rank_p, lstart[:nt_p])
    lpos_s = _local_positions(idx_s, rank_s, lstart[nt_p:])
    grp = _copy_lists(runs, lstart, gstart)
    grp_p, grp_s = grp[:nt_p], grp[nt_p:]
    xs_rows = _dispatch(xn_p, grp_p, lpos_p, TOK_BLOCK, n_slots, ends=ends)
    xs_rows = _dispatch(xn_s, grp_s, lpos_s, bs, n_slots, xs=xs_rows)
    y_rows = _moe_ffn(block_expert, n_used, block_rows, xs_rows, w_gate[l], w_up[l], w_down[l], b_gate[l], b_up[l], b_down[l])
    gain_f = row(norm_final)
    y_p = _combine(h_p, y_rows, grp_p, lpos_p, _token_major(gate_p), gain_f, TOK_BLOCK)
    y_s = _combine(h_s, y_rows, grp_s, lpos_s, _token_major(gate_s), gain_f, bs)

    y_prompt = y_p.reshape(bp, lp, d)
    y_sample = y_s.reshape(bs, ls, d)
    return (y_prompt, y_sample, shift_p[None], wkv_p[None], gla_p[None], shift_s[None], wkv_s[None], gla_s[None])
```

```python
import functools

import jax
import jax.numpy as jnp
from jax import lax
from jax.experimental import pallas as pl
from jax.experimental.pallas import tpu as pltpu

F32 = jnp.float32
BF16 = jnp.bfloat16
HIGHEST = lax.Precision.HIGHEST

D_MODEL = 1024
RW_WIDTH = 512
RW_HEADS = 8
RW_N = 64
RW_PROJ = 1792
RW_GN_EPS = 64e-5
GLA_HEADS = 4
GLA_DK = 64
GLA_DV = 128
GLA_WIDTH = 512
GLA_QK = GLA_HEADS * GLA_DK
GLA_PROJ = 1552
GLA_PROJ_PAD = 1664
GLA_LORA_PAD = 128
GLA_GATE_NORMALIZER = 16.0
N_EXPERTS = 32
TOP_K = 4
SWIGLU_LIMIT = 7.0
SWIGLU_ALPHA = 1.702
NORM_EPS = 1e-5
LOG2_E = 1.4426950408889634

RW_CHUNK = 64
GLA_CHUNK = 16
SEQ_BLOCK = 512
TOK_BLOCK = 512
MOE_BLOCK = 512
MOE_PIECE = 128
VMEM_LIMIT = 56 * 1024 * 1024


def _dot(a, b, precision=None):
    return jnp.dot(a, b, preferred_element_type=F32, precision=precision)


def _dot_nt(a, b, precision=None):
    return lax.dot_general(a, b, (((1,), (1,)), ((), ())), preferred_element_type=F32, precision=precision)


def _dot_tn(a, b, precision=None):
    return lax.dot_general(a, b, (((0,), (0,)), ((), ())), preferred_element_type=F32, precision=precision)


def _sigmoid(x):
    return 1.0 / (1.0 + jnp.exp(-x))


def _softplus(x):
    return jnp.maximum(x, 0.0) + jnp.log(1.0 + jnp.exp(-jnp.abs(x)))


def _params(sem):
    return pltpu.CompilerParams(dimension_semantics=sem, vmem_limit_bytes=VMEM_LIMIT)


ROW_TILE = D_MODEL // 128


def _store_row_tiles(ref, x):
    m = x.shape[0]
    for c in range(ROW_TILE):
        ref[pl.ds(c, m, stride=ROW_TILE), :] = x[:, c * 128 : (c + 1) * 128]


def _load_row_tiles(ref, m):
    return jnp.concatenate([ref[pl.ds(c, m, stride=ROW_TILE), :] for c in range(ROW_TILE)], axis=-1)


def _inproj_kernel(x_ref, gain_ref, wr_ref, wg_ref, zr_ref, zg_ref):
    x = x_ref[...]
    xn = x * lax.rsqrt(jnp.mean(x * x, axis=-1, keepdims=True) + NORM_EPS) * gain_ref[...]
    xb = xn.astype(BF16)
    zr_ref[...] = _dot(xb, wr_ref[...])
    zg_ref[...] = _dot(xb, wg_ref[...])


def _inproj(x, gain, w_r, w_g, tm):
    n = x.shape[0]
    return pl.pallas_call(
        _inproj_kernel,
        grid=(n // tm,),
        in_specs=[
            pl.BlockSpec((tm, D_MODEL), lambda i: (i, 0)),
            pl.BlockSpec((1, D_MODEL), lambda i: (0, 0)),
            pl.BlockSpec((D_MODEL, RW_PROJ), lambda i: (0, 0)),
            pl.BlockSpec((D_MODEL, GLA_PROJ_PAD), lambda i: (0, 0)),
        ],
        out_specs=[
            pl.BlockSpec((tm, RW_PROJ), lambda i: (i, 0)),
            pl.BlockSpec((tm, GLA_PROJ_PAD), lambda i: (i, 0)),
        ],
        out_shape=[
            jax.ShapeDtypeStruct((n, RW_PROJ), F32),
            jax.ShapeDtypeStruct((n, GLA_PROJ_PAD), F32),
        ],
        compiler_params=_params(("parallel",)),
    )(x, gain, w_r, w_g)


def _rwkv_features(zs, w0, w2p, a0, a2p, g2, k_k, k_a):
    W = RW_WIDTH
    r = zs[:, 0:W]
    k_raw = zs[:, W : 2 * W]
    v = zs[:, 2 * W : 3 * W]
    zwa = zs[:, 3 * W : 3 * W + 128]
    zg = zs[:, 3 * W + 128 :]
    w = -_softplus(-(w0 + _dot(jnp.tanh(zwa).astype(BF16), w2p))) - 0.5
    log_decay = -jnp.exp(w)
    a = _sigmoid(a0 + _dot(zwa.astype(BF16), a2p))
    g = _dot(_sigmoid(zg).astype(BF16), g2)
    kk_raw = k_raw * k_k
    k = k_raw * (1.0 + (a - 1.0) * k_a)
    return r, k, v, kk_raw, a, log_decay, g


def _level_mask(ri, ci, lvl):
    same = (ri >> (lvl + 1)) == (ci >> (lvl + 1))
    return same & (((ri >> lvl) & 1) == 1) & (((ci >> lvl) & 1) == 0)


def _rwkv_seq_kernel(z_ref, shift0_ref, s0_ref, mu_ref, w0_ref, w2_ref, a0_ref, a2_ref, g2_ref, kk_ref, ka_ref,
                     rk_ref, lnw_ref, lnb_ref, o_ref, sout_ref,
                     m_scr, prev_scr, r_scr, k_scr, v_scr, kkr_scr, a_scr, lw_scr, on_scr, bon_scr):
    C = RW_CHUNK
    N = RW_N
    t_idx = pl.program_id(1)
    tb = z_ref.shape[1]
    zero_nn = jnp.zeros((N, N), F32)

    @pl.when(t_idx == 0)
    def _():
        prev_scr[...] = shift0_ref[0]
        for p in range(RW_HEADS // 2):
            top = jnp.concatenate([s0_ref[0, 2 * p].T, zero_nn], axis=1)
            bot = jnp.concatenate([zero_nn, s0_ref[0, 2 * p + 1].T], axis=1)
            m_scr[p] = jnp.concatenate([top, bot], axis=0)

    z = z_ref[0]
    row = lax.broadcasted_iota(jnp.int32, z.shape, 0)
    z_prev = jnp.where(row == 0, prev_scr[...], pltpu.roll(z, 1, axis=0))
    prev_scr[...] = z[tb - 1 : tb, :]
    zs = z + mu_ref[...] * (z_prev - z)
    r, k, v, kk_raw, a, log_decay, g = _rwkv_features(
        zs, w0_ref[...], w2_ref[...], a0_ref[...], a2_ref[...], g2_ref[...], kk_ref[...], ka_ref[...])
    P2 = 2 * N
    left1 = lax.broadcasted_iota(jnp.int32, (1, P2), 1) < N

    def head_sum(x):
        s0 = jnp.sum(jnp.where(left1, x, 0.0), axis=-1, keepdims=True)
        s1 = jnp.sum(jnp.where(left1, 0.0, x), axis=-1, keepdims=True)
        return jnp.where(left1, s0, s1)

    def head_sum_wide(x):
        return jnp.concatenate([head_sum(x[:, p * P2 : (p + 1) * P2]) for p in range(RW_HEADS // 2)], axis=1)

    alpha = kk_raw * lax.rsqrt(jnp.maximum(head_sum_wide(kk_raw * kk_raw), 1e-24))
    r_scr[...] = r
    k_scr[...] = k
    v_scr[...] = v
    kkr_scr[...] = alpha
    a_scr[...] = alpha * a
    lw_scr[...] = log_decay
    bon_scr[...] = head_sum_wide(r * k * rk_ref[...]) * v

    ri = lax.broadcasted_iota(jnp.int32, (C, P2), 0)
    ci = lax.broadcasted_iota(jnp.int32, (C, P2), 1) % N
    left = lax.broadcasted_iota(jnp.int32, (C, P2), 1) < N
    tril = ri >= ci
    stril = ri > ci
    eye_f = (ri == ci).astype(F32)
    rb = lax.broadcasted_iota(jnp.int32, (P2, P2), 0)
    cb = lax.broadcasted_iota(jnp.int32, (P2, P2), 1)
    same_head = (rb < N) == (cb < N)
    eye_b = rb == cb
    rc = lax.broadcasted_iota(jnp.int32, (C, C), 0)
    cc = lax.broadcasted_iota(jnp.int32, (C, C), 1)
    tril_f = (rc >= cc).astype(F32)

    def bdiag(x):
        return jnp.concatenate([jnp.where(left, x, 0.0), jnp.where(left, 0.0, x)], axis=0)

    n_sub = tb // C
    pairs = range(RW_HEADS // 2)

    def chunk_body(it, carry):
        units = [(s, p) for s in range(n_sub) for p in pairs]
        sls = [pl.ds(pl.multiple_of((it * n_sub + s) * C, C), C) for s in range(n_sub)]
        prep = []
        for s in range(n_sub):
            lw = lw_scr[sls[s], :]
            cum = _dot(tril_f, lw, precision=HIGHEST)
            cum_last = cum[C - 1 : C, :]
            prep.append(dict(
                e_incl=jnp.exp(cum), e_excl=jnp.exp(cum - lw), e_neg=jnp.exp(-cum),
                e_tail=jnp.exp(cum_last - cum), p_last=jnp.exp(cum_last),
                r=r_scr[sls[s], :], k=k_scr[sls[s], :], v=v_scr[sls[s], :], kk=kkr_scr[sls[s], :],
                a=a_scr[sls[s], :]))
        lanes = [slice(p * P2, (p + 1) * P2) for p in pairs]
        get = lambda name: [prep[s][name][:, lanes[p]] for s, p in units]
        r2, k2, v2, al, be = get("r"), get("k"), get("v"), get("kk"), get("a")
        e_incl, e_excl, e_neg, e_tail, p_last = get("e_incl"), get("e_excl"), get("e_neg"), get("e_tail"), get("p_last")
        un = range(len(units))
        al_t = [al[u] * e_excl[u] for u in un]
        r_t = [r2[u] * e_incl[u] for u in un]
        be_n = [be[u] * e_neg[u] for u in un]
        k_n = [k2[u] * e_neg[u] for u in un]
        k_et = [(k2[u] * e_tail[u]).T for u in un]
        be_et = [(be[u] * e_tail[u]).T for u in un]
        v_bd = [bdiag(v2[u]) for u in un]
        lhs = [jnp.concatenate([al_t[u], r_t[u]], axis=0) for u in un]
        s_b = [_dot_nt(lhs[u], bdiag(be_n[u])) for u in un]
        s_k = [_dot_nt(lhs[u], bdiag(k_n[u])) for u in un]
        l_ab = [jnp.where(stril, s_b[u][:C], 0.0) for u in un]
        a_rb = [jnp.where(tril, s_b[u][C:], 0.0) for u in un]
        l_ak = [jnp.where(stril, s_k[u][:C], 0.0) for u in un]
        a_rk = [jnp.where(tril, s_k[u][C:], 0.0) for u in un]
        lakv = [_dot(l_ak[u], v_bd[u]) for u in un]
        arkv = [_dot(a_rk[u], v_bd[u]) for u in un]
        kev = [_dot(k_et[u], v2[u]) for u in un]
        t_inv = [eye_f - jnp.where(_level_mask(ri, ci, 0), l_ab[u], 0.0) for u in un]
        lvl = 1
        while (1 << lvl) < C:
            lm = _level_mask(ri, ci, lvl)
            tn = [_dot(t_inv[u], bdiag(jnp.where(lm, l_ab[u], 0.0))) for u in un]
            t_inv = [t_inv[u] - _dot(tn[u], bdiag(t_inv[u])) for u in un]
            lvl += 1
        a_til = [_dot(t_inv[u], bdiag(al_t[u])) for u in un]
        b_til = [_dot(t_inv[u], bdiag(lakv[u])) for u in un]
        r_hat = [r_t[u] - _dot(a_rb[u], bdiag(a_til[u])) for u in un]
        o_hat = [arkv[u] - _dot(a_rb[u], bdiag(b_til[u])) for u in un]
        g_bd = [jnp.where(same_head, jnp.where(eye_b, p_last[u], 0.0) - _dot(be_et[u], a_til[u]), 0.0) for u in un]
        h_bd = [jnp.where(same_head, kev[u] - _dot(be_et[u], b_til[u]), 0.0) for u in un]
        lhs_m = [jnp.concatenate([r_hat[u], g_bd[u]], axis=0) for u in un]
        for u, (s, p) in enumerate(units):
            res = _dot(lhs_m[u], m_scr[p])
            m_scr[p] = res[C:] + h_bd[u]
            o_p = res[:C] + o_hat[u]
            cen = o_p - head_sum(o_p) * (1.0 / N)
            var = head_sum(cen * cen) * (1.0 / N)
            on_scr[sls[s], lanes[p]] = cen * lax.rsqrt(var + RW_GN_EPS)
        return carry

    lax.fori_loop(0, tb // (C * n_sub), chunk_body, 0)
    out = (on_scr[...] * lnw_ref[...] + lnb_ref[...] + bon_scr[...]) * g
    o_ref[0] = out.astype(o_ref.dtype)

    @pl.when(t_idx == pl.num_programs(1) - 1)
    def _():
        for p in range(RW_HEADS // 2):
            m = m_scr[p]
            sout_ref[0, 2 * p] = m[:N, :N].T
            sout_ref[0, 2 * p + 1] = m[N:, N:].T


def _rwkv_seq(z3, shift0, s0, rw, tb):
    b, l, _ = z3.shape
    const = lambda shape: pl.BlockSpec(shape, lambda i, j: (0,) * len(shape))
    wide = lambda: pltpu.VMEM((tb, RW_WIDTH), F32)
    return pl.pallas_call(
        _rwkv_seq_kernel,
        grid=(b, l // tb),
        in_specs=[
            pl.BlockSpec((1, tb, RW_PROJ), lambda i, j: (i, j, 0)),
            pl.BlockSpec((1, 1, RW_PROJ), lambda i, j: (i, 0, 0)),
            pl.BlockSpec((1, RW_HEADS, RW_N, RW_N), lambda i, j: (i, 0, 0, 0)),
            const((1, RW_PROJ)),
            const((1, RW_WIDTH)), const((128, RW_WIDTH)),
            const((1, RW_WIDTH)), const((128, RW_WIDTH)),
            const((128, RW_WIDTH)),
            const((1, RW_WIDTH)), const((1, RW_WIDTH)), const((1, RW_WIDTH)),
            const((1, RW_WIDTH)), const((1, RW_WIDTH)),
        ],
        out_specs=[
            pl.BlockSpec((1, tb, RW_WIDTH), lambda i, j: (i, j, 0)),
            pl.BlockSpec((1, RW_HEADS, RW_N, RW_N), lambda i, j: (i, 0, 0, 0)),
        ],
        out_shape=[
            jax.ShapeDtypeStruct((b, l, RW_WIDTH), BF16),
            jax.ShapeDtypeStruct((b, RW_HEADS, RW_N, RW_N), F32),
        ],
        scratch_shapes=[
            pltpu.VMEM((RW_HEADS // 2, 2 * RW_N, 2 * RW_N), F32),
            pltpu.VMEM((1, RW_PROJ), F32),
            wide(), wide(), wide(), wide(), wide(), wide(), wide(), wide(),
        ],
        compiler_params=_params(("parallel", "arbitrary")),
    )(z3, shift0, s0, rw["mu"], rw["w0"], rw["w2p"], rw["a0"], rw["a2p"], rw["g2"], rw["k_k"], rw["k_a"],
      rw["r_k"], rw["ln_w"], rw["ln_b"])


def _rwkv_step_prep_kernel(z_ref, shift0_ref, mu_ref, w0_ref, w2_ref, a0_ref, a2_ref, g2_ref, kk_ref, ka_ref,
                           rk_ref, r_ref, k_ref, v_ref, al_ref, be_ref, dec_ref, g_ref, bon_ref):
    z = z_ref[...]
    zs = z + mu_ref[...] * (shift0_ref[...] - z)
    r, k, v, kk_raw, a, log_decay, g = _rwkv_features(
        zs, w0_ref[...], w2_ref[...], a0_ref[...], a2_ref[...], g2_ref[...], kk_ref[...], ka_ref[...])
    rk_all = rk_ref[...]
    for h in range(RW_HEADS):
        hs = slice(h * RW_N, (h + 1) * RW_N)
        kk_h = kk_raw[:, hs]
        nrm = jnp.sqrt(jnp.sum(kk_h * kk_h, axis=-1, keepdims=True))
        al = kk_h / jnp.maximum(nrm, 1e-12)
        al_ref[:, hs] = al
        be_ref[:, hs] = al * a[:, hs]
        bon_ref[:, hs] = jnp.sum(r[:, hs] * k[:, hs] * rk_all[:, hs], axis=-1, keepdims=True) * v[:, hs]
    r_ref[...] = r
    k_ref[...] = k
    v_ref[...] = v
    dec_ref[...] = jnp.exp(log_decay)
    g_ref[...] = g


def _rwkv_step_prep(z, shift0, rw):
    n = z.shape[0]
    out = jax.ShapeDtypeStruct((n, RW_WIDTH), F32)
    return pl.pallas_call(
        _rwkv_step_prep_kernel,
        out_shape=[out] * 8,
        compiler_params=pltpu.CompilerParams(vmem_limit_bytes=VMEM_LIMIT),
    )(z, shift0, rw["mu"], rw["w0"], rw["w2p"], rw["a0"], rw["a2p"], rw["g2"], rw["k_k"], rw["k_a"], rw["r_k"])


def _rwkv_step_kernel(s_ref, r_ref, k_ref, al_ref, be_ref, dec_ref, v_ref, snew_ref, o_ref):
    r, k, al, be, dec = r_ref[...], k_ref[...], al_ref[...], be_ref[...], dec_ref[...]

    def body(g, carry):
        rows = pl.ds(pl.multiple_of(g * 8, 8), 8)
        v8 = v_ref[rows, :]
        outs = []
        for j in range(8):
            s = s_ref[0, g * 8 + j]
            sa = -jnp.sum(s * al, axis=0, keepdims=True)
            s_new = s * dec + sa * be + v8[j : j + 1, :] * k
            snew_ref[0, g * 8 + j] = s_new
            outs.append(jnp.sum(s_new * r, axis=0, keepdims=True))
        o_ref[rows, :] = jnp.concatenate(outs, axis=0)
        return carry

    lax.fori_loop(0, RW_N // 8, body, 0)


def _rwkv_step(s0, r, k, al, be, dec, v):
    n = s0.shape[0]
    s_t = jnp.transpose(s0, (1, 2, 3, 0))
    s_spec = pl.BlockSpec((1, RW_N, RW_N, n), lambda h: (h, 0, 0, 0))
    op_spec = pl.BlockSpec((RW_N, n), lambda h: (h, 0))
    s_new_t, o_t = pl.pallas_call(
        _rwkv_step_kernel,
        grid=(RW_HEADS,),
        in_specs=[s_spec] + [op_spec] * 6,
        out_specs=[s_spec, op_spec],
        out_shape=[
            jax.ShapeDtypeStruct(s_t.shape, F32),
            jax.ShapeDtypeStruct((RW_WIDTH, n), F32),
        ],
        compiler_params=_params(("parallel",)),
    )(s_t, r.T, k.T, al.T, be.T, dec.T, v.T)
    return jnp.transpose(s_new_t, (3, 0, 1, 2)), o_t.T


def _gla_features(z, gkw, gkb):
    q = z[:, 0:GLA_QK] * (GLA_DK ** -0.5)
    k = z[:, GLA_QK : 2 * GLA_QK]
    v = z[:, 2 * GLA_QK : 2 * GLA_QK + GLA_WIDTH]
    g = z[:, 2 * GLA_QK + GLA_WIDTH : 2 * GLA_QK + 2 * GLA_WIDTH]
    zgk = z[:, 2 * GLA_QK + 2 * GLA_WIDTH :]
    gk = -_softplus(-(_dot(zgk.astype(BF16), gkw) + gkb)) / GLA_GATE_NORMALIZER
    return q, k, v, g, gk


def _gla_finish(o, g, norm_w):
    outs = []
    for h in range(GLA_HEADS):
        hs = slice(h * GLA_DV, (h + 1) * GLA_DV)
        o_h = o[:, hs]
        o_h = o_h * lax.rsqrt(jnp.mean(o_h * o_h, axis=-1, keepdims=True) + NORM_EPS) * norm_w
        g_h = g[:, hs]
        outs.append(o_h * (g_h * _sigmoid(g_h)))
    return jnp.concatenate(outs, axis=-1)


def _gla_seq_kernel(z_ref, s0_ref, gkw_ref, gkb_ref, nw_ref, wsel_ref, o_ref, sout_ref,
                    st_scr, x_scr, gc_scr, oi_scr):
    C = GLA_CHUNK
    G = 128
    t_idx = pl.program_id(1)
    tb = z_ref.shape[1]
    nc = tb // C
    zero_vk = jnp.zeros((GLA_DV, GLA_DK), F32)

    @pl.when(t_idx == 0)
    def _():
        for p in range(GLA_HEADS // 2):
            top = jnp.concatenate([s0_ref[0, 2 * p].T, zero_vk], axis=1)
            bot = jnp.concatenate([zero_vk, s0_ref[0, 2 * p + 1].T], axis=1)
            st_scr[p] = jnp.concatenate([top, bot], axis=0)

    q, k, v, g, gk = _gla_features(z_ref[0], gkw_ref[...], gkb_ref[...])
    ri = lax.broadcasted_iota(jnp.int32, (G, G), 0)
    ci = lax.broadcasted_iota(jnp.int32, (G, G), 1)
    cum_mat = ((ri // C == ci // C) & (ri >= ci)).astype(F32)
    for m in range(tb // G):
        rows = slice(m * G, (m + 1) * G)
        gc_scr[rows, :] = _dot(cum_mat, gk[rows, :], precision=HIGHEST)
    gcum = gc_scr[...]

    rg = lax.broadcasted_iota(jnp.int32, (tb, 2 * G), 0)
    cg = lax.broadcasted_iota(jnp.int32, (tb, 2 * G), 1)
    blk_mask = ((cg % G) // C == (rg % G) // C) & (cg % C <= rg % C)
    for p in range(GLA_HEADS // 2):
        ls = slice(p * 128, (p + 1) * 128)
        q3 = q[:, ls].reshape(nc, C, 128)
        k3 = k[:, ls].reshape(nc, C, 128)
        g3 = gcum[:, ls].reshape(nc, C, 128) * LOG2_E
        half = C // 2
        for j in range(C):
            lo = 0 if j < half else half
            e = (q3[:, lo:] * jnp.exp2(jnp.minimum(g3[:, lo:] - g3[:, j : j + 1, :], 0.0))) * k3[:, j : j + 1, :]
            if lo:
                e = jnp.concatenate([jnp.zeros((nc, lo, 128), F32), e], axis=1)
            x_scr[:, j * 128 : (j + 1) * 128] = e.reshape(tb, 128).astype(BF16)
        a_t = jnp.where(blk_mask, _dot(x_scr[...], wsel_ref[...]), 0.0).astype(BF16)
        for hl in range(2):
            h = 2 * p + hl
            for m in range(tb // G):
                rows = slice(m * G, (m + 1) * G)
                a_blk = a_t[rows, hl * G : (hl + 1) * G]
                oi_scr[rows, h * GLA_DV : (h + 1) * GLA_DV] = _dot(
                    a_blk, v[rows, h * GLA_DV : (h + 1) * GLA_DV].astype(BF16))

    CG = G // C
    rt = lax.broadcasted_iota(jnp.int32, (G, CG * 128), 0)
    ct = lax.broadcasted_iota(jnp.int32, (G, CG * 128), 1)
    own_chunk = rt // C == ct // 128
    rs = lax.broadcasted_iota(jnp.int32, (2 * GLA_DV, CG * 128), 0)
    cs = lax.broadcasted_iota(jnp.int32, (2 * GLA_DV, CG * 128), 1)
    same_head = rs // GLA_DV == (cs % 128) // GLA_DK

    def chunk_diag(x):
        return jnp.where(own_chunk, jnp.concatenate([x] * CG, axis=1), 0.0)

    for m in range(tb // G):
        rows = slice(m * G, (m + 1) * G)
        for p in range(GLA_HEADS // 2):
            ls = slice(p * 128, (p + 1) * 128)
            vs = slice(p * 2 * GLA_DV, (p + 1) * 2 * GLA_DV)
            g_g = gcum[rows, ls]
            g3 = g_g.reshape(CG, C, 128)
            g_last = jnp.broadcast_to(g3[:, C - 1 : C, :], (CG, C, 128)).reshape(G, 128)
            q_t = q[rows, ls] * jnp.exp(g_g)
            k_t = k[rows, ls] * jnp.exp(g_last - g_g)
            d_s = jnp.where(same_head, _dot_tn(v[rows, vs], chunk_diag(k_t)), 0.0)
            st = st_scr[p]
            starts = []
            for c in range(CG):
                starts.append(st)
                decay = jnp.exp(g_g[c * C + C - 1 : c * C + C, :])
                st = st * decay + d_s[:, c * 128 : (c + 1) * 128]
            st_scr[p] = st
            oi_scr[rows, vs] += _dot_nt(chunk_diag(q_t), jnp.concatenate(starts, axis=1))

    o_ref[0] = _gla_finish(oi_scr[...], g, nw_ref[...]).astype(o_ref.dtype)

    @pl.when(t_idx == pl.num_programs(1) - 1)
    def _():
        for p in range(GLA_HEADS // 2):
            st = st_scr[p]
            sout_ref[0, 2 * p] = st[:GLA_DV, :GLA_DK].T
            sout_ref[0, 2 * p + 1] = st[GLA_DV:, GLA_DK:].T


def _gla_select_matrix():
    j = jnp.arange(GLA_CHUNK)[:, None, None]
    hl = jnp.arange(2)[None, :, None]
    rows_j = jnp.broadcast_to(j, (GLA_CHUNK, 2, GLA_DK)).reshape(-1)
    rows_h = jnp.broadcast_to(hl, (GLA_CHUNK, 2, GLA_DK)).reshape(-1)
    cols = jnp.arange(256)
    sel = (rows_j[:, None] == cols[None, :] % GLA_CHUNK) & (rows_h[:, None] == cols[None, :] // 128)
    return sel.astype(BF16)


def _gla_seq(z3, s0, gl, tb):
    b, l, _ = z3.shape
    const = lambda shape: pl.BlockSpec(shape, lambda i, j: (0,) * len(shape))
    return pl.pallas_call(
        _gla_seq_kernel,
        grid=(b, l // tb),
        in_specs=[
            pl.BlockSpec((1, tb, GLA_PROJ_PAD), lambda i, j: (i, j, 0)),
            pl.BlockSpec((1, GLA_HEADS, GLA_DK, GLA_DV), lambda i, j: (i, 0, 0, 0)),
            const((GLA_LORA_PAD, GLA_QK)), const((1, GLA_QK)), const((1, GLA_DV)),
            const((GLA_CHUNK * 128, 256)),
        ],
        out_specs=[
            pl.BlockSpec((1, tb, GLA_WIDTH), lambda i, j: (i, j, 0)),
            pl.BlockSpec((1, GLA_HEADS, GLA_DK, GLA_DV), lambda i, j: (i, 0, 0, 0)),
        ],
        out_shape=[
            jax.ShapeDtypeStruct((b, l, GLA_WIDTH), BF16),
            jax.ShapeDtypeStruct((b, GLA_HEADS, GLA_DK, GLA_DV), F32),
        ],
        scratch_shapes=[
            pltpu.VMEM((GLA_HEADS // 2, 2 * GLA_DV, 2 * GLA_DK), F32),
            pltpu.VMEM((tb, GLA_CHUNK * 128), BF16),
            pltpu.VMEM((tb, GLA_QK), F32), pltpu.VMEM((tb, GLA_WIDTH), F32),
        ],
        compiler_params=_params(("parallel", "arbitrary")),
    )(z3, s0, gl["gkw"], gl["gkb"], gl["norm_w"], _gla_select_matrix())


def _gla_step_prep_kernel(z_ref, gkw_ref, gkb_ref, q_ref, k_ref, v_ref, g_ref, dec_ref):
    q, k, v, g, gk = _gla_features(z_ref[...], gkw_ref[...], gkb_ref[...])
    q_ref[...] = q
    k_ref[...] = k
    v_ref[...] = v
    g_ref[...] = g
    dec_ref[...] = jnp.exp(gk)


def _gla_step_prep(z, gl):
    n = z.shape[0]
    qk = jax.ShapeDtypeStruct((n, GLA_QK), F32)
    wide = jax.ShapeDtypeStruct((n, GLA_WIDTH), F32)
    return pl.pallas_call(
        _gla_step_prep_kernel,
        out_shape=[qk, qk, wide, wide, qk],
        compiler_params=pltpu.CompilerParams(vmem_limit_bytes=VMEM_LIMIT),
    )(z, gl["gkw"], gl["gkb"])


def _gla_step_kernel(s_ref, q_ref, k_ref, dec_ref, v_ref, snew_ref, o_ref):
    bb = s_ref.shape[0]
    rows = lax.broadcasted_iota(jnp.int32, (bb, bb * GLA_DK), 0)
    cols = lax.broadcasted_iota(jnp.int32, (bb, bb * GLA_DK), 1)
    own = rows == cols // GLA_DK
    ones = jnp.ones((bb, GLA_DV), F32)
    zeros = jnp.zeros((bb, GLA_DV), F32)

    def seq_diag(x):
        return jnp.where(own, jnp.concatenate([x] * bb, axis=1), 0.0)

    for h in range(GLA_HEADS):
        ks = slice(h * GLA_DK, (h + 1) * GLA_DK)
        vs = slice(h * GLA_DV, (h + 1) * GLA_DV)
        s = s_ref[:, h].reshape(bb * GLA_DK, GLA_DV)
        v = v_ref[:, vs]
        lhs_t = jnp.concatenate([seq_diag(k_ref[:, ks]), seq_diag(dec_ref[:, ks])], axis=0)
        rhs = jnp.concatenate([jnp.concatenate([v, zeros], axis=1), jnp.concatenate([zeros, ones], axis=1)], axis=0)
        both = _dot_tn(lhs_t, rhs, precision=HIGHEST)
        s_new = s * both[:, GLA_DV:] + both[:, :GLA_DV]
        snew_ref[:, h] = s_new.reshape(bb, GLA_DK, GLA_DV)
        o_ref[:, vs] = _dot(seq_diag(q_ref[:, ks]), s_new)


def _gla_step(s0, q, k, dec, v, bb):
    n = s0.shape[0]
    s_spec = pl.BlockSpec((bb, GLA_HEADS, GLA_DK, GLA_DV), lambda i: (i, 0, 0, 0))
    qk_spec = pl.BlockSpec((bb, GLA_QK), lambda i: (i, 0))
    v_spec = pl.BlockSpec((bb, GLA_WIDTH), lambda i: (i, 0))
    return pl.pallas_call(
        _gla_step_kernel,
        grid=(n // bb,),
        in_specs=[s_spec, qk_spec, qk_spec, qk_spec, v_spec],
        out_specs=[s_spec, v_spec],
        out_shape=[jax.ShapeDtypeStruct(s0.shape, F32), jax.ShapeDtypeStruct((n, GLA_WIDTH), F32)],
        compiler_params=_params(("parallel",)),
    )(s0, q, k, dec, v)


def _step_post_kernel(orw_ref, bon_ref, grw_ref, lnw_ref, lnb_ref, ogl_ref, ggl_ref, nw_ref, o_rw_ref, o_gl_ref):
    o = orw_ref[...]
    for h in range(RW_HEADS):
        hs = slice(h * RW_N, (h + 1) * RW_N)
        o_h = o[:, hs]
        mean = jnp.mean(o_h, axis=-1, keepdims=True)
        cen = o_h - mean
        var = jnp.mean(cen * cen, axis=-1, keepdims=True)
        on = cen * lax.rsqrt(var + RW_GN_EPS)
        res = (on * lnw_ref[:, hs] + lnb_ref[:, hs] + bon_ref[:, hs]) * grw_ref[:, hs]
        o_rw_ref[:, hs] = res.astype(o_rw_ref.dtype)
    o_gl_ref[...] = _gla_finish(ogl_ref[...], ggl_ref[...], nw_ref[...]).astype(o_gl_ref.dtype)


def _step_post(o_rw, bonus, g_rw, rw, o_gl, g_gl, gl):
    n = o_rw.shape[0]
    return pl.pallas_call(
        _step_post_kernel,
        out_shape=[jax.ShapeDtypeStruct((n, RW_WIDTH), BF16), jax.ShapeDtypeStruct((n, GLA_WIDTH), BF16)],
        compiler_params=pltpu.CompilerParams(vmem_limit_bytes=VMEM_LIMIT),
    )(o_rw, bonus, g_rw, rw["ln_w"], rw["ln_b"], o_gl, g_gl, gl["norm_w"])


def _outproj_router_kernel(x_ref, orw_ref, ogl_ref, wo_ref, gain_ref, wrt_ref, br_ref, before_ref,
                           h_ref, xn_ref, idx_ref, gate_ref, rank_ref, cnt_ref):
    mix = jnp.concatenate([orw_ref[...], ogl_ref[...]], axis=-1)
    h = x_ref[...] + _dot(mix, wo_ref[...])
    h_ref[...] = h
    xn = h * lax.rsqrt(jnp.mean(h * h, axis=-1, keepdims=True) + NORM_EPS) * gain_ref[...]
    _store_row_tiles(xn_ref, xn)
    logits = _dot_nt(wrt_ref[...], xn) + br_ref[...]
    eidx = lax.broadcasted_iota(jnp.int32, logits.shape, 0)
    vals, idxs = [], []
    work = logits
    chosen = jnp.zeros(logits.shape, F32)
    for _ in range(TOP_K):
        m = jnp.max(work, axis=0, keepdims=True)
        sel = jnp.min(jnp.where(work == m, eidx, N_EXPERTS), axis=0, keepdims=True)
        hit = eidx == sel
        work = jnp.where(hit, -jnp.inf, work)
        chosen = chosen + hit.astype(F32)
        vals.append(m)
        idxs.append(sel)
    prefix = _dot(chosen.astype(BF16), before_ref[...])
    exps = [jnp.exp(v - vals[0]) for v in vals]
    denom = exps[0] + exps[1] + exps[2] + exps[3]
    for j in range(TOP_K):
        idx_ref[0, j : j + 1, :] = idxs[j]
        gate_ref[0, j : j + 1, :] = exps[j] / denom
        rank = jnp.sum(jnp.where(eidx == idxs[j], prefix, 0.0), axis=0, keepdims=True)
        rank_ref[0, j : j + 1, :] = rank.astype(jnp.int32)
    cnt = jnp.sum(chosen, axis=1, keepdims=True)
    cnt_ref[0] = jnp.broadcast_to(cnt, (N_EXPERTS, 128)).astype(jnp.int32)


def _outproj_router(x, o_rw, o_gl, w_out, gain, w_router_t, b_router, tm):
    n = x.shape[0]
    nt = n // tm
    const = lambda shape: pl.BlockSpec(shape, lambda i: (0,) * len(shape))
    tok = lambda width: pl.BlockSpec((tm, width), lambda i: (i, 0))
    lane = pl.BlockSpec((1, TOP_K, tm), lambda i: (i, 0, 0))
    t = jnp.arange(tm, dtype=jnp.int32)
    before = (t[:, None] < t[None, :]).astype(BF16)
    return pl.pallas_call(
        _outproj_router_kernel,
        grid=(nt,),
        in_specs=[
            tok(D_MODEL), tok(RW_WIDTH), tok(GLA_WIDTH),
            const((D_MODEL, D_MODEL)), const((1, D_MODEL)), const((N_EXPERTS, D_MODEL)), const((N_EXPERTS, 1)),
            const((tm, tm)),
        ],
        out_specs=[tok(D_MODEL), pl.BlockSpec((tm * ROW_TILE, 128), lambda i: (i, 0)), lane, lane, lane,
                   pl.BlockSpec((1, N_EXPERTS, 128), lambda i: (i, 0, 0))],
        out_shape=[
            jax.ShapeDtypeStruct((n, D_MODEL), F32),
            jax.ShapeDtypeStruct((n * ROW_TILE, 128), F32),
            jax.ShapeDtypeStruct((nt, TOP_K, tm), jnp.int32),
            jax.ShapeDtypeStruct((nt, TOP_K, tm), F32),
            jax.ShapeDtypeStruct((nt, TOP_K, tm), jnp.int32),
            jax.ShapeDtypeStruct((nt, N_EXPERTS, 128), jnp.int32),
        ],
        compiler_params=_params(("parallel",)),
    )(x, o_rw, o_gl, w_out, gain, w_router_t, b_router, before)


def _moe_kernel(be_ref, nu_ref, epoch_ref, next_ref, rows_ref, parts_ref, xs_ref, wg_hbm, wu_hbm, wd_hbm, bg_ref, bu_ref, bd_ref,
                y_ref, w_f32, wg_b, wu_b, wd_b, sems):
    b = pl.program_id(0)
    prev = be_ref[jnp.maximum(b - 1, 0)]
    new_expert = (b == 0) | (be_ref[b] != prev)

    def fetch(e, slot, i):
        w = (wg_hbm, wu_hbm, wd_hbm)[i]
        return pltpu.make_async_copy(w.at[e], w_f32.at[slot, i], sems.at[slot])

    @pl.when(b == 0)
    def _():
        for i in range(3):
            fetch(be_ref[0], 0, i).start()

    @pl.when(new_expert)
    def _():
        slot = epoch_ref[b] % 2
        for i in range(3):
            fetch(be_ref[b], slot, i).wait()
        wg_b[...] = w_f32[slot, 0].astype(BF16)
        wu_b[...] = w_f32[slot, 1].astype(BF16)
        wd_b[...] = w_f32[slot, 2].astype(BF16)

    first_part, end_part = parts_ref[b] // 4, parts_ref[b] % 4
    for i in range(3):
        @pl.when((next_ref[b] >= 0) & (first_part <= i) & (i < end_part))
        def _():
            fetch(next_ref[b], 1 - epoch_ref[b] % 2, i).start()

    def ffn(m):
        x = _load_row_tiles(xs_ref, m).astype(BF16)
        half = D_MODEL // 2
        acc = None
        for f in range(2):
            fs = slice(f * half, (f + 1) * half)
            gt = _dot(x, wg_b[:, fs]) + bg_ref[0, :, fs]
            up = _dot(x, wu_b[:, fs]) + bu_ref[0, :, fs]
            gt = jnp.minimum(gt, SWIGLU_LIMIT)
            up = jnp.clip(up, -SWIGLU_LIMIT, SWIGLU_LIMIT)
            hid = (up + 1.0) * gt * _sigmoid(SWIGLU_ALPHA * gt)
            part = _dot(hid.astype(BF16), wd_b[fs, :])
            acc = part if acc is None else acc + part
        _store_row_tiles(y_ref, acc + bd_ref[0])

    pieces = (rows_ref[b] + MOE_PIECE - 1) // MOE_PIECE
    for q in range(1, MOE_BLOCK // MOE_PIECE + 1):
        @pl.when((b < nu_ref[0]) & (pieces == q))
        def _():
            ffn(q * MOE_PIECE)


def _moe_ffn(block_expert, n_used, block_rows, xs, w_gate, w_up, w_down, b_gate, b_up, b_down):
    n_blocks = block_expert.shape[0]
    pos = jnp.arange(n_blocks, dtype=jnp.int32)
    change = (pos > 0) & (block_expert != jnp.roll(block_expert, 1))
    epoch = jnp.cumsum(change.astype(jnp.int32))
    later = change[None, :] & (pos[None, :] > pos[:, None])
    first = jnp.min(jnp.where(later, pos[None, :], n_blocks), axis=1)
    next_e = jnp.sum(jnp.where(pos[None, :] == first[:, None], block_expert[None, :], 0), axis=1)
    next_e = jnp.where(first < n_blocks, next_e, -1).astype(jnp.int32)
    run_start = jnp.max(jnp.where((pos[None, :] <= pos[:, None]) & (change | (pos == 0))[None, :], pos[None, :], 0),
                        axis=1)
    q = jnp.minimum(pos - run_start, 3)
    used = pos < n_used[0]
    is_last = jnp.roll(change, -1) | (pos == n_used[0] - 1)
    parts = jnp.where(used, q * 4 + jnp.where(is_last, 3, jnp.minimum(q + 1, 3)), 15).astype(jnp.int32)

    row = lambda b, be, nu, ep, nx, br, pt: (jnp.minimum(b, nu[0] - 1), 0)
    bspec = pl.BlockSpec((1, 1, D_MODEL), lambda b, be, nu, ep, nx, br, pt: (be[b], 0, 0))
    wspec = pl.BlockSpec(memory_space=pl.ANY)
    grid_spec = pltpu.PrefetchScalarGridSpec(
        num_scalar_prefetch=6,
        grid=(n_blocks,),
        in_specs=[pl.BlockSpec((MOE_BLOCK * ROW_TILE, 128), row), wspec, wspec, wspec, bspec, bspec, bspec],
        out_specs=pl.BlockSpec((MOE_BLOCK * ROW_TILE, 128), row),
        scratch_shapes=[pltpu.VMEM((2, 3, D_MODEL, D_MODEL), F32)] + [pltpu.VMEM((D_MODEL, D_MODEL), BF16)] * 3
        + [pltpu.SemaphoreType.DMA((2,))],
    )
    return pl.pallas_call(
        _moe_kernel,
        grid_spec=grid_spec,
        out_shape=jax.ShapeDtypeStruct((n_blocks * MOE_BLOCK * ROW_TILE, 128), F32),
        compiler_params=_params(("arbitrary",)),
    )(block_expert, n_used, epoch, next_e, block_rows, parts, xs, w_gate, w_up, w_down,
      b_gate.reshape(N_EXPERTS, 1, D_MODEL), b_up.reshape(N_EXPERTS, 1, D_MODEL),
      b_down.reshape(N_EXPERTS, 1, D_MODEL))


SEG_ALIGN = 8
GROUP_ROWS = SEG_ALIGN * ROW_TILE


def _local_rows(tm):
    return tm * TOP_K + N_EXPERTS * SEG_ALIGN


BIG_COPY = 4


def _copy_tables():
    n_big = _local_rows(TOK_BLOCK) // SEG_ALIGN // BIG_COPY
    n_small = N_EXPERTS * (BIG_COPY - 1)
    return 2 * n_big, 2 * (n_big + n_small)


def _issue_group_copies(cp_ref, hbm, buf, sem, to_hbm):
    small_at, counts_at = _copy_tables()

    def copy(at, n_groups):
        rows = n_groups * GROUP_ROWS
        b = buf.at[pl.ds(pl.multiple_of(cp_ref[0, 0, at] * GROUP_ROWS, GROUP_ROWS), rows)]
        h = hbm.at[pl.ds(pl.multiple_of(cp_ref[0, 0, at + 1] * GROUP_ROWS, GROUP_ROWS), rows)]
        return pltpu.make_async_copy(b, h, sem) if to_hbm else pltpu.make_async_copy(h, b, sem)

    def big(i, carry):
        copy(2 * i, BIG_COPY).start()
        return carry

    def small(i, carry):
        copy(small_at + 2 * i, 1).start()
        return carry

    lax.fori_loop(0, cp_ref[0, 0, counts_at], big, 0)
    lax.fori_loop(0, cp_ref[0, 0, counts_at + 1], small, 0)


def _wait_group_copies(cp_ref, hbm, buf, sem, to_hbm):
    rows = pl.ds(0, pl.multiple_of(cp_ref[0, 0, _copy_tables()[1] + 2] * GROUP_ROWS, GROUP_ROWS))
    b, h = buf.at[rows], hbm.at[rows]
    (pltpu.make_async_copy(b, h, sem) if to_hbm else pltpu.make_async_copy(h, b, sem)).wait()


def _combine_kernel(grp_c, grp_n, lpos_ref, gate_ref, h_ref, gain_ref, y_hbm, o_ref, ybuf, fbuf, sems):
    i = pl.program_id(0)
    nt = pl.num_programs(0)
    tm = h_ref.shape[0]

    @pl.when(i == 0)
    def _():
        _issue_group_copies(grp_c, y_hbm, ybuf.at[0], sems.at[0], False)

    @pl.when(i + 1 < nt)
    def _():
        _issue_group_copies(grp_n, y_hbm, ybuf.at[(i + 1) % 2], sems.at[(i + 1) % 2], False)

    slot = i % 2
    yb = ybuf.at[slot]
    _wait_group_copies(grp_c, y_hbm, yb, sems.at[slot], False)

    def token_body(t, carry):
        acc = None
        for j in range(TOP_K):
            row = pl.multiple_of(lpos_ref[0, 0, t * TOP_K + j], ROW_TILE)
            term = gate_ref[0, 0, t * TOP_K + j] * yb[pl.ds(row, ROW_TILE), :]
            acc = term if acc is None else acc + term
        fbuf[pl.ds(pl.multiple_of(t * ROW_TILE, ROW_TILE), ROW_TILE), :] = acc
        return carry

    lax.fori_loop(0, tm, token_body, 0, unroll=8)
    f = h_ref[...] + _load_row_tiles(fbuf, tm)
    o_ref[...] = f * lax.rsqrt(jnp.mean(f * f, axis=-1, keepdims=True) + NORM_EPS) * gain_ref[...]


def _combine(h, y_rows, grp3, lpos3, gate3, gain, tm):
    n = h.shape[0]
    nt = n // tm
    n_local = _local_rows(tm)
    gw = grp3.shape[-1]
    smem = lambda shape, imap: pl.BlockSpec(shape, imap, memory_space=pltpu.SMEM)
    cur = lambda i: (i, 0, 0)
    nxt = lambda i: (jnp.minimum(i + 1, nt - 1), 0, 0)
    return pl.pallas_call(
        _combine_kernel,
        grid=(nt,),
        in_specs=[
            smem((1, 1, gw), cur), smem((1, 1, gw), nxt),
            smem((1, 1, TOP_K * tm), cur), smem((1, 1, TOP_K * tm), cur),
            pl.BlockSpec((tm, D_MODEL), lambda i: (i, 0)),
            pl.BlockSpec((1, D_MODEL), lambda i: (0, 0)),
            pl.BlockSpec(memory_space=pl.ANY),
        ],
        out_specs=pl.BlockSpec((tm, D_MODEL), lambda i: (i, 0)),
        out_shape=jax.ShapeDtypeStruct((n, D_MODEL), F32),
        scratch_shapes=[pltpu.VMEM((2, n_local * ROW_TILE, 128), F32), pltpu.VMEM((tm * ROW_TILE, 128), F32),
                        pltpu.SemaphoreType.DMA((2,))],
        compiler_params=pltpu.CompilerParams(dimension_semantics=("arbitrary",), vmem_limit_bytes=VMEM_LIMIT,
                                             disable_bounds_checks=True),
    )(grp3, grp3, lpos3, gate3, h, gain, y_rows)


def _dispatch_kernel(*refs, fill):
    if fill:
        grp_ref, grp_prev, lpos_ref, ends_ref, x_ref, xs_hbm, sorted_buf, zero_scr, sems, zsem = refs
    else:
        grp_ref, grp_prev, lpos_ref, x_ref, _, xs_hbm, sorted_buf, sems = refs
    i = pl.program_id(0)
    tm = x_ref.shape[0] // ROW_TILE
    blk = MOE_PIECE * ROW_TILE
    sorted_scr = sorted_buf.at[i % 2]
    sem = sems.at[i % 2]

    if fill:
        def fill_copy(e):
            start = pl.multiple_of(ends_ref[0, e] * ROW_TILE, blk)
            return pltpu.make_async_copy(zero_scr, xs_hbm.at[pl.ds(start, blk)], zsem)

        @pl.when(i == 0)
        def _():
            zero_scr[...] = jnp.zeros(zero_scr.shape, zero_scr.dtype)
            for e in range(N_EXPERTS):
                @pl.when(ends_ref[1, e] > 0)
                def _():
                    fill_copy(e).start()

    sorted_scr[...] = jnp.zeros(sorted_scr.shape, sorted_scr.dtype)

    def move(t, carry):
        row = x_ref[pl.ds(pl.multiple_of(t * ROW_TILE, ROW_TILE), ROW_TILE), :]
        for j in range(TOP_K):
            dst = pl.multiple_of(lpos_ref[0, 0, t * TOP_K + j], ROW_TILE)
            sorted_scr[pl.ds(dst, ROW_TILE), :] = row
        return carry

    lax.fori_loop(0, tm, move, 0, unroll=8)

    if fill:
        @pl.when(i == 0)
        def _():
            for e in range(N_EXPERTS):
                @pl.when(ends_ref[1, e] > 0)
                def _():
                    fill_copy(e).wait()

    @pl.when(i > 0)
    def _():
        _wait_group_copies(grp_prev, xs_hbm, sorted_buf.at[(i + 1) % 2], sems.at[(i + 1) % 2], True)

    _issue_group_copies(grp_ref, xs_hbm, sorted_scr, sem, True)

    @pl.when(i == pl.num_programs(0) - 1)
    def _():
        _wait_group_copies(grp_ref, xs_hbm, sorted_scr, sem, True)


def _dispatch(xn, grp3, lpos3, tm, n_slots, ends=None, xs=None):
    n = xn.shape[0] // ROW_TILE
    fill = xs is None
    smem = lambda shape, imap: pl.BlockSpec(shape, imap, memory_space=pltpu.SMEM)
    in_specs = [smem((1, 1, grp3.shape[-1]), lambda i: (i, 0, 0)),
                smem((1, 1, grp3.shape[-1]), lambda i: (jnp.maximum(i - 1, 0), 0, 0)),
                smem((1, 1, TOP_K * tm), lambda i: (i, 0, 0))]
    args = [grp3, grp3, lpos3]
    scratch = [pltpu.VMEM((2, _local_rows(tm) * ROW_TILE, 128), F32)]
    if fill:
        in_specs.append(smem((2, N_EXPERTS), lambda i: (0, 0)))
        args.append(ends)
        scratch.append(pltpu.VMEM((MOE_PIECE * ROW_TILE, 128), F32))
    in_specs.append(pl.BlockSpec((tm * ROW_TILE, 128), lambda i: (i, 0)))
    args.append(xn)
    aliases = {}
    if not fill:
        in_specs.append(pl.BlockSpec(memory_space=pl.ANY))
        args.append(xs)
        aliases = {len(args) - 1: 0}
    scratch.append(pltpu.SemaphoreType.DMA((2,)))
    if fill:
        scratch.append(pltpu.SemaphoreType.DMA(()))
    return pl.pallas_call(
        functools.partial(_dispatch_kernel, fill=fill),
        grid=(n // tm,),
        in_specs=in_specs,
        out_specs=pl.BlockSpec(memory_space=pl.ANY),
        out_shape=jax.ShapeDtypeStruct((n_slots * ROW_TILE, 128), F32),
        scratch_shapes=scratch,
        input_output_aliases=aliases,
        compiler_params=pltpu.CompilerParams(dimension_semantics=("arbitrary",), vmem_limit_bytes=VMEM_LIMIT,
                                             disable_bounds_checks=True, has_side_effects=True),
    )(*args)


def _pad_rows(w, rows, offset):
    out = jnp.zeros((rows, w.shape[1]), w.dtype)
    return out.at[offset : offset + w.shape[0]].set(w)


def _routing_tables(counts, n_pairs):
    n_tiles = counts.shape[0]
    n_blocks = (n_pairs + n_tiles * N_EXPERTS * (SEG_ALIGN - 1) + N_EXPERTS * (MOE_BLOCK - 1)
                + MOE_BLOCK - 1) // MOE_BLOCK
    runs = (counts + SEG_ALIGN - 1) // SEG_ALIGN * SEG_ALIGN
    local_start = jnp.cumsum(runs, axis=1) - runs
    total = jnp.sum(runs, axis=0)
    padded = (total + MOE_BLOCK - 1) // MOE_BLOCK * MOE_BLOCK
    pends = jnp.cumsum(padded)
    pstarts = pends - padded
    global_start = pstarts[None, :] + jnp.cumsum(runs, axis=0) - runs
    blocks = jnp.arange(n_blocks, dtype=jnp.int32) * MOE_BLOCK
    n_used = (pends[-1] // MOE_BLOCK).astype(jnp.int32)
    owner = jnp.sum((pends[None, :] <= blocks[:, None]).astype(jnp.int32), axis=1)
    block_expert = jnp.minimum(owner, N_EXPERTS - 1)
    last = jnp.sum(jnp.where(jnp.arange(n_blocks) == n_used - 1, block_expert, 0))
    block_expert = jnp.where(jnp.arange(n_blocks) < n_used, block_expert, last)
    real_end = pstarts + total
    ends = jnp.stack([real_end // MOE_PIECE * MOE_PIECE, real_end % MOE_PIECE]).astype(jnp.int32)
    row_end = jnp.sum(jnp.where(block_expert[:, None] == jnp.arange(N_EXPERTS), (pstarts + total)[None, :], 0), axis=1)
    block_rows = jnp.clip(row_end - blocks, 0, MOE_BLOCK).astype(jnp.int32)
    return (runs, local_start, global_start, ends, block_expert.astype(jnp.int32), n_used.reshape(1), block_rows,
            n_blocks)


def _local_positions(idx3, rank3, local_start):
    hit = idx3[..., None] == jnp.arange(N_EXPERTS, dtype=jnp.int32)
    lpos = rank3 + jnp.sum(jnp.where(hit, local_start[:, None, None, :], 0), axis=-1)
    return _token_major(lpos.astype(jnp.int32) * ROW_TILE)


def _copy_lists(runs, local_start, global_start):
    experts = jnp.arange(N_EXPERTS, dtype=jnp.int32)
    groups = runs // SEG_ALIGN
    l8, g8 = local_start // SEG_ALIGN, global_start // SEG_ALIGN
    n_big, n_small = groups // BIG_COPY, groups % BIG_COPY

    def copy_list(count, first_local, first_global, step, length):
        ends = jnp.cumsum(count, axis=1)
        o = jnp.arange(length, dtype=jnp.int32)
        owner = jnp.minimum(jnp.sum((ends[:, None, :] <= o[None, :, None]).astype(jnp.int32), axis=-1), N_EXPERTS - 1)
        sel = owner[..., None] == experts
        pick = lambda t: jnp.sum(jnp.where(sel, t[:, None, :], 0), axis=-1)
        k = (o[None, :] - pick(ends - count)) * step
        pairs = jnp.stack([pick(first_local) + k, pick(first_global) + k], axis=-1)
        live = (o[None, :] < ends[:, -1:])[..., None]
        return jnp.where(live, pairs, 0).reshape(count.shape[0], 2 * length)

    small_at, counts_at = _copy_tables()
    big = copy_list(n_big, l8, g8, BIG_COPY, small_at // 2)
    small = copy_list(n_small, l8 + n_big * BIG_COPY, g8 + n_big * BIG_COPY, 1, (counts_at - small_at) // 2)
    counts = jnp.stack([jnp.sum(n_big, axis=1), jnp.sum(n_small, axis=1), jnp.sum(groups, axis=1)], axis=1)
    counts = jnp.pad(counts, ((0, 0), (0, 5)))
    return jnp.concatenate([big, small, counts], axis=1).astype(jnp.int32)[:, None, :]


def _token_major(t3):
    return jnp.swapaxes(t3, 1, 2).reshape(t3.shape[0], 1, -1)


def kernel(x_prompt, x_sample, state_rwkv_shift, state_rwkv_wkv, state_gla, norm_mix, w_in, rw_mu, rw_w0, rw_w2, rw_a0, rw_a2, rw_g2, rw_k_k, rw_k_a, rw_r_k, rw_ln_w, rw_ln_b, gla_gk_w2, gla_gk_b, gla_norm_w, w_out, norm_ffn, w_router, b_router, w_gate, b_gate, w_up, b_up, w_down, b_down, norm_final):
    depth = norm_mix.shape[0]
    assert depth == 1
    bp, lp, d = x_prompt.shape
    bs, ls, _ = x_sample.shape
    assert ls == 1 and lp % SEQ_BLOCK == 0
    l = 0
    row = lambda t: t.reshape(1, -1)

    w_in_b = w_in[l].astype(BF16)
    w_in_r = w_in_b[:, :RW_PROJ]
    w_in_g = jnp.pad(w_in_b[:, RW_PROJ:], ((0, 0), (0, GLA_PROJ_PAD - GLA_PROJ)))
    rw = dict(
        mu=row(rw_mu[l]), w0=row(rw_w0[l]), a0=row(rw_a0[l]),
        w2p=_pad_rows(rw_w2[l].astype(BF16), 128, 0), a2p=_pad_rows(rw_a2[l].astype(BF16), 128, 64),
        g2=rw_g2[l].astype(BF16), k_k=row(rw_k_k[l]), k_a=row(rw_k_a[l]), r_k=row(rw_r_k[l]),
        ln_w=row(rw_ln_w[l]), ln_b=row(rw_ln_b[l]))
    gl = dict(gkw=_pad_rows(gla_gk_w2[l].astype(BF16), GLA_LORA_PAD, 0), gkb=row(gla_gk_b[l]),
              norm_w=row(gla_norm_w[l]))
    gain_mix = row(norm_mix[l])

    n_p = bp * lp
    xp = x_prompt.reshape(n_p, d)
    zr_p, zg_p = _inproj(xp, gain_mix, w_in_r, w_in_g, TOK_BLOCK)
    zr_p3 = zr_p.reshape(bp, lp, RW_PROJ)
    o_rw_p, wkv_p = _rwkv_seq(zr_p3, jnp.zeros((bp, 1, RW_PROJ), F32),
                              jnp.zeros((bp, RW_HEADS, RW_N, RW_N), F32), rw, SEQ_BLOCK)
    o_gl_p, gla_p = _gla_seq(zg_p.reshape(bp, lp, GLA_PROJ_PAD),
                             jnp.zeros((bp, GLA_HEADS, GLA_DK, GLA_DV), F32), gl, SEQ_BLOCK)
    shift_p = zr_p3[:, -1, :]

    xs_ = x_sample.reshape(bs, d)
    zr_s, zg_s = _inproj(xs_, gain_mix, w_in_r, w_in_g, bs)
    r, k, v, al, be, dec, g_rw, bonus = _rwkv_step_prep(zr_s, state_rwkv_shift[l], rw)
    wkv_s, o_rw_s = _rwkv_step(state_rwkv_wkv[l], r, k, al, be, dec, v)
    q, kg, vg, g_gl, dec_g = _gla_step_prep(zg_s, gl)
    gla_s, o_gl_s = _gla_step(state_gla[l], q, kg, dec_g, vg, 16)
    o_rw_s2, o_gl_s2 = _step_post(o_rw_s, bonus, g_rw, rw, o_gl_s, g_gl, gl)
    shift_s = zr_s

    w_out_b = w_out[l].astype(BF16)
    router = (w_out_b, row(norm_ffn[l]), w_router[l].T, b_router[l].reshape(N_EXPERTS, 1))
    h_p, xn_p, idx_p, gate_p, rank_p, cnt_p = _outproj_router(
        xp, o_rw_p.reshape(n_p, RW_WIDTH), o_gl_p.reshape(n_p, GLA_WIDTH), *router, TOK_BLOCK)
    h_s, xn_s, idx_s, gate_s, rank_s, cnt_s = _outproj_router(xs_, o_rw_s2, o_gl_s2, *router, bs)
    nt_p = n_p // TOK_BLOCK
    counts = jnp.concatenate([cnt_p[:, :, 0], cnt_s[:, :, 0]], axis=0)
    runs, lstart, gstart, ends, block_expert, n_used, block_rows, n_blocks = _routing_tables(
        counts, (n_p + bs) * TOP_K)
    n_slots = n_blocks * MOE_BLOCK
    lpos_p = _local_positions(idx_p, rank_p, lstart[:nt_p])
    lpos_s = _local_positions(idx_s, rank_s, lstart[nt_p:])
    grp = _copy_lists(runs, lstart, gstart)
    grp_p, grp_s = grp[:nt_p], grp[nt_p:]
    xs_rows = _dispatch(xn_p, grp_p, lpos_p, TOK_BLOCK, n_slots, ends=ends)
    xs_rows = _dispatch(xn_s, grp_s, lpos_s, bs, n_slots, xs=xs_rows)
    y_rows = _moe_ffn(block_expert, n_used, block_rows, xs_rows, w_gate[l], w_up[l], w_down[l], b_gate[l], b_up[l], b_down[l])
    gain_f = row(norm_final)
    y_p = _combine(h_p, y_rows, grp_p, lpos_p, _token_major(gate_p), gain_f, TOK_BLOCK)
    y_s = _combine(h_s, y_rows, grp_s, lpos_s, _token_major(gate_s), gain_f, bs)

    y_prompt = y_p.reshape(bp, lp, d)
    y_sample = y_s.reshape(bs, ls, d)
    return (y_prompt, y_sample, shift_p[None], wkv_p[None], gla_p[None], shift_s[None], wkv_s[None], gla_s[None])
```

```python
import functools

import jax
import jax.numpy as jnp
from jax import lax
from jax.experimental import pallas as pl
from jax.experimental.pallas import tpu as pltpu

F32 = jnp.float32
BF16 = jnp.bfloat16
HIGHEST = lax.Precision.HIGHEST

D_MODEL = 1024
RW_WIDTH = 512
RW_HEADS = 8
RW_N = 64
RW_PROJ = 1792
RW_GN_EPS = 64e-5
GLA_HEADS = 4
GLA_DK = 64
GLA_DV = 128
GLA_WIDTH = 512
GLA_QK = GLA_HEADS * GLA_DK
GLA_PROJ = 1552
GLA_PROJ_PAD = 1664
GLA_LORA_PAD = 128
GLA_GATE_NORMALIZER = 16.0
N_EXPERTS = 32
TOP_K = 4
SWIGLU_LIMIT = 7.0
SWIGLU_ALPHA = 1.702
NORM_EPS = 1e-5
LOG2_E = 1.4426950408889634

RW_CHUNK = 64
GLA_CHUNK = 16
SEQ_BLOCK = 512
TOK_BLOCK = 512
MOE_BLOCK = 512
MOE_PIECE = 128
VMEM_LIMIT = 56 * 1024 * 1024


def _dot(a, b, precision=None):
    return jnp.dot(a, b, preferred_element_type=F32, precision=precision)


def _dot_nt(a, b, precision=None):
    return lax.dot_general(a, b, (((1,), (1,)), ((), ())), preferred_element_type=F32, precision=precision)


def _dot_tn(a, b, precision=None):
    return lax.dot_general(a, b, (((0,), (0,)), ((), ())), preferred_element_type=F32, precision=precision)


def _sigmoid(x):
    return 1.0 / (1.0 + jnp.exp(-x))


def _softplus(x):
    return jnp.maximum(x, 0.0) + jnp.log(1.0 + jnp.exp(-jnp.abs(x)))


def _params(sem):
    return pltpu.CompilerParams(dimension_semantics=sem, vmem_limit_bytes=VMEM_LIMIT)


ROW_TILE = D_MODEL // 128


def _store_row_tiles(ref, x):
    m = x.shape[0]
    for c in range(ROW_TILE):
        ref[pl.ds(c, m, stride=ROW_TILE), :] = x[:, c * 128 : (c + 1) * 128]


def _load_row_tiles(ref, m):
    return jnp.concatenate([ref[pl.ds(c, m, stride=ROW_TILE), :] for c in range(ROW_TILE)], axis=-1)


def _inproj_kernel(x_ref, gain_ref, wr_ref, wg_ref, zr_ref, zg_ref):
    x = x_ref[...]
    xn = x * lax.rsqrt(jnp.mean(x * x, axis=-1, keepdims=True) + NORM_EPS) * gain_ref[...]
    xb = xn.astype(BF16)
    zr_ref[...] = _dot(xb, wr_ref[...])
    zg_ref[...] = _dot(xb, wg_ref[...])


def _inproj(x, gain, w_r, w_g, tm):
    n = x.shape[0]
    return pl.pallas_call(
        _inproj_kernel,
        grid=(n // tm,),
        in_specs=[
            pl.BlockSpec((tm, D_MODEL), lambda i: (i, 0)),
            pl.BlockSpec((1, D_MODEL), lambda i: (0, 0)),
            pl.BlockSpec((D_MODEL, RW_PROJ), lambda i: (0, 0)),
            pl.BlockSpec((D_MODEL, GLA_PROJ_PAD), lambda i: (0, 0)),
        ],
        out_specs=[
            pl.BlockSpec((tm, RW_PROJ), lambda i: (i, 0)),
            pl.BlockSpec((tm, GLA_PROJ_PAD), lambda i: (i, 0)),
        ],
        out_shape=[
            jax.ShapeDtypeStruct((n, RW_PROJ), F32),
            jax.ShapeDtypeStruct((n, GLA_PROJ_PAD), F32),
        ],
        compiler_params=_params(("parallel",)),
    )(x, gain, w_r, w_g)


def _rwkv_features(zs, w0, w2p, a0, a2p, g2, k_k, k_a):
    W = RW_WIDTH
    r = zs[:, 0:W]
    k_raw = zs[:, W : 2 * W]
    v = zs[:, 2 * W : 3 * W]
    zwa = zs[:, 3 * W : 3 * W + 128]
    zg = zs[:, 3 * W + 128 :]
    w = -_softplus(-(w0 + _dot(jnp.tanh(zwa).astype(BF16), w2p))) - 0.5
    log_decay = -jnp.exp(w)
    a = _sigmoid(a0 + _dot(zwa.astype(BF16), a2p))
    g = _dot(_sigmoid(zg).astype(BF16), g2)
    kk_raw = k_raw * k_k
    k = k_raw * (1.0 + (a - 1.0) * k_a)
    return r, k, v, kk_raw, a, log_decay, g


def _level_mask(ri, ci, lvl):
    same = (ri >> (lvl + 1)) == (ci >> (lvl + 1))
    return same & (((ri >> lvl) & 1) == 1) & (((ci >> lvl) & 1) == 0)


def _rwkv_seq_kernel(z_ref, shift0_ref, s0_ref, mu_ref, w0_ref, w2_ref, a0_ref, a2_ref, g2_ref, kk_ref, ka_ref,
                     rk_ref, lnw_ref, lnb_ref, o_ref, sout_ref,
                     m_scr, prev_scr, r_scr, k_scr, v_scr, kkr_scr, a_scr, lw_scr, on_scr, bon_scr):
    C = RW_CHUNK
    N = RW_N
    t_idx = pl.program_id(1)
    tb = z_ref.shape[1]
    zero_nn = jnp.zeros((N, N), F32)

    @pl.when(t_idx == 0)
    def _():
        prev_scr[...] = shift0_ref[0]
        for p in range(RW_HEADS // 2):
            top = jnp.concatenate([s0_ref[0, 2 * p].T, zero_nn], axis=1)
            bot = jnp.concatenate([zero_nn, s0_ref[0, 2 * p + 1].T], axis=1)
            m_scr[p] = jnp.concatenate([top, bot], axis=0)

    z = z_ref[0]
    row = lax.broadcasted_iota(jnp.int32, z.shape, 0)
    z_prev = jnp.where(row == 0, prev_scr[...], pltpu.roll(z, 1, axis=0))
    prev_scr[...] = z[tb - 1 : tb, :]
    zs = z + mu_ref[...] * (z_prev - z)
    r, k, v, kk_raw, a, log_decay, g = _rwkv_features(
        zs, w0_ref[...], w2_ref[...], a0_ref[...], a2_ref[...], g2_ref[...], kk_ref[...], ka_ref[...])
    P2 = 2 * N
    left1 = lax.broadcasted_iota(jnp.int32, (1, P2), 1) < N

    def head_sum(x):
        s0 = jnp.sum(jnp.where(left1, x, 0.0), axis=-1, keepdims=True)
        s1 = jnp.sum(jnp.where(left1, 0.0, x), axis=-1, keepdims=True)
        return jnp.where(left1, s0, s1)

    def head_sum_wide(x):
        return jnp.concatenate([head_sum(x[:, p * P2 : (p + 1) * P2]) for p in range(RW_HEADS // 2)], axis=1)

    alpha = kk_raw * lax.rsqrt(jnp.maximum(head_sum_wide(kk_raw * kk_raw), 1e-24))
    r_scr[...] = r
    k_scr[...] = k
    v_scr[...] = v
    kkr_scr[...] = alpha
    a_scr[...] = alpha * a
    lw_scr[...] = log_decay
    bon_scr[...] = head_sum_wide(r * k * rk_ref[...]) * v

    ri = lax.broadcasted_iota(jnp.int32, (C, P2), 0)
    ci = lax.broadcasted_iota(jnp.int32, (C, P2), 1) % N
    left = lax.broadcasted_iota(jnp.int32, (C, P2), 1) < N
    tril = ri >= ci
    stril = ri > ci
    eye_f = (ri == ci).astype(F32)
    rb = lax.broadcasted_iota(jnp.int32, (P2, P2), 0)
    cb = lax.broadcasted_iota(jnp.int32, (P2, P2), 1)
    same_head = (rb < N) == (cb < N)
    eye_b = rb == cb
    rc = lax.broadcasted_iota(jnp.int32, (C, C), 0)
    cc = lax.broadcasted_iota(jnp.int32, (C, C), 1)
    tril_f = (rc >= cc).astype(F32)

    def bdiag(x):
        return jnp.concatenate([jnp.where(left, x, 0.0), jnp.where(left, 0.0, x)], axis=0)

    n_sub = tb // C
    pairs = range(RW_HEADS // 2)

    def chunk_body(it, carry):
        units = [(s, p) for s in range(n_sub) for p in pairs]
        sls = [pl.ds(pl.multiple_of((it * n_sub + s) * C, C), C) for s in range(n_sub)]
        prep = []
        for s in range(n_sub):
            lw = lw_scr[sls[s], :]
            cum = _dot(tril_f, lw, precision=HIGHEST)
            cum_last = cum[C - 1 : C, :]
            prep.append(dict(
                e_incl=jnp.exp(cum), e_excl=jnp.exp(cum - lw), e_neg=jnp.exp(-cum),
                e_tail=jnp.exp(cum_last - cum), p_last=jnp.exp(cum_last),
                r=r_scr[sls[s], :], k=k_scr[sls[s], :], v=v_scr[sls[s], :], kk=kkr_scr[sls[s], :],
                a=a_scr[sls[s], :]))
        lanes = [slice(p * P2, (p + 1) * P2) for p in pairs]
        get = lambda name: [prep[s][name][:, lanes[p]] for s, p in units]
        r2, k2, v2, al, be = get("r"), get("k"), get("v"), get("kk"), get("a")
        e_incl, e_excl, e_neg, e_tail, p_last = get("e_incl"), get("e_excl"), get("e_neg"), get("e_tail"), get("p_last")
        un = range(len(units))
        al_t = [al[u] * e_excl[u] for u in un]
        r_t = [r2[u] * e_incl[u] for u in un]
        be_n = [be[u] * e_neg[u] for u in un]
        k_n = [k2[u] * e_neg[u] for u in un]
        k_et = [(k2[u] * e_tail[u]).T for u in un]
        be_et = [(be[u] * e_tail[u]).T for u in un]
        v_bd = [bdiag(v2[u]) for u in un]
        lhs = [jnp.concatenate([al_t[u], r_t[u]], axis=0) for u in un]
        s_b = [_dot_nt(lhs[u], bdiag(be_n[u])) for u in un]
        s_k = [_dot_nt(lhs[u], bdiag(k_n[u])) for u in un]
        l_ab = [jnp.where(stril, s_b[u][:C], 0.0) for u in un]
        a_rb = [jnp.where(tril, s_b[u][C:], 0.0) for u in un]
        l_ak = [jnp.where(stril, s_k[u][:C], 0.0) for u in un]
        a_rk = [jnp.where(tril, s_k[u][C:], 0.0) for u in un]
        lakv = [_dot(l_ak[u], v_bd[u]) for u in un]
        arkv = [_dot(a_rk[u], v_bd[u]) for u in un]
        kev = [_dot(k_et[u], v2[u]) for u in un]
        t_inv = [eye_f - jnp.where(_level_mask(ri, ci, 0), l_ab[u], 0.0) for u in un]
        lvl = 1
        while (1 << lvl) < C:
            lm = _level_mask(ri, ci, lvl)
            tn = [_dot(t_inv[u], bdiag(jnp.where(lm, l_ab[u], 0.0))) for u in un]
            t_inv = [t_inv[u] - _dot(tn[u], bdiag(t_inv[u])) for u in un]
            lvl += 1
        a_til = [_dot(t_inv[u], bdiag(al_t[u])) for u in un]
        b_til = [_dot(t_inv[u], bdiag(lakv[u])) for u in un]
        r_hat = [r_t[u] - _dot(a_rb[u], bdiag(a_til[u])) for u in un]
        o_hat = [arkv[u] - _dot(a_rb[u], bdiag(b_til[u])) for u in un]
        g_bd = [jnp.where(same_head, jnp.where(eye_b, p_last[u], 0.0) - _dot(be_et[u], a_til[u]), 0.0) for u in un]
        h_bd = [jnp.where(same_head, kev[u] - _dot(be_et[u], b_til[u]), 0.0) for u in un]
        lhs_m = [jnp.concatenate([r_hat[u], g_bd[u]], axis=0) for u in un]
        for u, (s, p) in enumerate(units):
            res = _dot(lhs_m[u], m_scr[p])
            m_scr[p] = res[C:] + h_bd[u]
            o_p = res[:C] + o_hat[u]
            cen = o_p - head_sum(o_p) * (1.0 / N)
            var = head_sum(cen * cen) * (1.0 / N)
            on_scr[sls[s], lanes[p]] = cen * lax.rsqrt(var + RW_GN_EPS)
        return carry

    lax.fori_loop(0, tb // (C * n_sub), chunk_body, 0)
    out = (on_scr[...] * lnw_ref[...] + lnb_ref[...] + bon_scr[...]) * g
    o_ref[0] = out.astype(o_ref.dtype)

    @pl.when(t_idx == pl.num_programs(1) - 1)
    def _():
        for p in range(RW_HEADS // 2):
            m = m_scr[p]
            sout_ref[0, 2 * p] = m[:N, :N].T
            sout_ref[0, 2 * p + 1] = m[N:, N:].T


def _rwkv_seq(z3, shift0, s0, rw, tb):
    b, l, _ = z3.shape
    const = lambda shape: pl.BlockSpec(shape, lambda i, j: (0,) * len(shape))
    wide = lambda: pltpu.VMEM((tb, RW_WIDTH), F32)
    return pl.pallas_call(
        _rwkv_seq_kernel,
        grid=(b, l // tb),
        in_specs=[
            pl.BlockSpec((1, tb, RW_PROJ), lambda i, j: (i, j, 0)),
            pl.BlockSpec((1, 1, RW_PROJ), lambda i, j: (i, 0, 0)),
            pl.BlockSpec((1, RW_HEADS, RW_N, RW_N), lambda i, j: (i, 0, 0, 0)),
            const((1, RW_PROJ)),
            const((1, RW_WIDTH)), const((128, RW_WIDTH)),
            const((1, RW_WIDTH)), const((128, RW_WIDTH)),
            const((128, RW_WIDTH)),
            const((1, RW_WIDTH)), const((1, RW_WIDTH)), const((1, RW_WIDTH)),
            const((1, RW_WIDTH)), const((1, RW_WIDTH)),
        ],
        out_specs=[
            pl.BlockSpec((1, tb, RW_WIDTH), lambda i, j: (i, j, 0)),
            pl.BlockSpec((1, RW_HEADS, RW_N, RW_N), lambda i, j: (i, 0, 0, 0)),
        ],
        out_shape=[
            jax.ShapeDtypeStruct((b, l, RW_WIDTH), BF16),
            jax.ShapeDtypeStruct((b, RW_HEADS, RW_N, RW_N), F32),
        ],
        scratch_shapes=[
            pltpu.VMEM((RW_HEADS // 2, 2 * RW_N, 2 * RW_N), F32),
            pltpu.VMEM((1, RW_PROJ), F32),
            wide(), wide(), wide(), wide(), wide(), wide(), wide(), wide(),
        ],
        compiler_params=_params(("parallel", "arbitrary")),
    )(z3, shift0, s0, rw["mu"], rw["w0"], rw["w2p"], rw["a0"], rw["a2p"], rw["g2"], rw["k_k"], rw["k_a"],
      rw["r_k"], rw["ln_w"], rw["ln_b"])


def _rwkv_step_prep_kernel(z_ref, shift0_ref, mu_ref, w0_ref, w2_ref, a0_ref, a2_ref, g2_ref, kk_ref, ka_ref,
                           rk_ref, r_ref, k_ref, v_ref, al_ref, be_ref, dec_ref, g_ref, bon_ref):
    z = z_ref[...]
    zs = z + mu_ref[...] * (shift0_ref[...] - z)
    r, k, v, kk_raw, a, log_decay, g = _rwkv_features(
        zs, w0_ref[...], w2_ref[...], a0_ref[...], a2_ref[...], g2_ref[...], kk_ref[...], ka_ref[...])
    rk_all = rk_ref[...]
    for h in range(RW_HEADS):
        hs = slice(h * RW_N, (h + 1) * RW_N)
        kk_h = kk_raw[:, hs]
        nrm = jnp.sqrt(jnp.sum(kk_h * kk_h, axis=-1, keepdims=True))
        al = kk_h / jnp.maximum(nrm, 1e-12)
        al_ref[:, hs] = al
        be_ref[:, hs] = al * a[:, hs]
        bon_ref[:, hs] = jnp.sum(r[:, hs] * k[:, hs] * rk_all[:, hs], axis=-1, keepdims=True) * v[:, hs]
    r_ref[...] = r
    k_ref[...] = k
    v_ref[...] = v
    dec_ref[...] = jnp.exp(log_decay)
    g_ref[...] = g


def _rwkv_step_prep(z, shift0, rw):
    n = z.shape[0]
    out = jax.ShapeDtypeStruct((n, RW_WIDTH), F32)
    return pl.pallas_call(
        _rwkv_step_prep_kernel,
        out_shape=[out] * 8,
        compiler_params=pltpu.CompilerParams(vmem_limit_bytes=VMEM_LIMIT),
    )(z, shift0, rw["mu"], rw["w0"], rw["w2p"], rw["a0"], rw["a2p"], rw["g2"], rw["k_k"], rw["k_a"], rw["r_k"])


def _rwkv_step_kernel(s_ref, r_ref, k_ref, al_ref, be_ref, dec_ref, v_ref, snew_ref, o_ref):
    r, k, al, be, dec = r_ref[...], k_ref[...], al_ref[...], be_ref[...], dec_ref[...]

    def body(g, carry):
        rows = pl.ds(pl.multiple_of(g * 8, 8), 8)
        v8 = v_ref[rows, :]
        outs = []
        for j in range(8):
            s = s_ref[0, g * 8 + j]
            sa = -jnp.sum(s * al, axis=0, keepdims=True)
            s_new = s * dec + sa * be + v8[j : j + 1, :] * k
            snew_ref[0, g * 8 + j] = s_new
            outs.append(jnp.sum(s_new * r, axis=0, keepdims=True))
        o_ref[rows, :] = jnp.concatenate(outs, axis=0)
        return carry

    lax.fori_loop(0, RW_N // 8, body, 0)


def _rwkv_step(s0, r, k, al, be, dec, v):
    n = s0.shape[0]
    s_t = jnp.transpose(s0, (1, 2, 3, 0))
    s_spec = pl.BlockSpec((1, RW_N, RW_N, n), lambda h: (h, 0, 0, 0))
    op_spec = pl.BlockSpec((RW_N, n), lambda h: (h, 0))
    s_new_t, o_t = pl.pallas_call(
        _rwkv_step_kernel,
        grid=(RW_HEADS,),
        in_specs=[s_spec] + [op_spec] * 6,
        out_specs=[s_spec, op_spec],
        out_shape=[
            jax.ShapeDtypeStruct(s_t.shape, F32),
            jax.ShapeDtypeStruct((RW_WIDTH, n), F32),
        ],
        compiler_params=_params(("parallel",)),
    )(s_t, r.T, k.T, al.T, be.T, dec.T, v.T)
    return jnp.transpose(s_new_t, (3, 0, 1, 2)), o_t.T


def _gla_features(z, gkw, gkb):
    q = z[:, 0:GLA_QK] * (GLA_DK ** -0.5)
    k = z[:, GLA_QK : 2 * GLA_QK]
    v = z[:, 2 * GLA_QK : 2 * GLA_QK + GLA_WIDTH]
    g = z[:, 2 * GLA_QK + GLA_WIDTH : 2 * GLA_QK + 2 * GLA_WIDTH]
    zgk = z[:, 2 * GLA_QK + 2 * GLA_WIDTH :]
    gk = -_softplus(-(_dot(zgk.astype(BF16), gkw) + gkb)) / GLA_GATE_NORMALIZER
    return q, k, v, g, gk


def _gla_finish(o, g, norm_w):
    outs = []
    for h in range(GLA_HEADS):
        hs = slice(h * GLA_DV, (h + 1) * GLA_DV)
        o_h = o[:, hs]
        o_h = o_h * lax.rsqrt(jnp.mean(o_h * o_h, axis=-1, keepdims=True) + NORM_EPS) * norm_w
        g_h = g[:, hs]
        outs.append(o_h * (g_h * _sigmoid(g_h)))
    return jnp.concatenate(outs, axis=-1)


def _gla_seq_kernel(z_ref, s0_ref, gkw_ref, gkb_ref, nw_ref, wsel_ref, o_ref, sout_ref,
                    st_scr, x_scr, gc_scr, oi_scr):
    C = GLA_CHUNK
    G = 128
    t_idx = pl.program_id(1)
    tb = z_ref.shape[1]
    nc = tb // C
    zero_vk = jnp.zeros((GLA_DV, GLA_DK), F32)

    @pl.when(t_idx == 0)
    def _():
        for p in range(GLA_HEADS // 2):
            top = jnp.concatenate([s0_ref[0, 2 * p].T, zero_vk], axis=1)
            bot = jnp.concatenate([zero_vk, s0_ref[0, 2 * p + 1].T], axis=1)
            st_scr[p] = jnp.concatenate([top, bot], axis=0)

    q, k, v, g, gk = _gla_features(z_ref[0], gkw_ref[...], gkb_ref[...])
    ri = lax.broadcasted_iota(jnp.int32, (G, G), 0)
    ci = lax.broadcasted_iota(jnp.int32, (G, G), 1)
    cum_mat = ((ri // C == ci // C) & (ri >= ci)).astype(F32)
    for m in range(tb // G):
        rows = slice(m * G, (m + 1) * G)
        gc_scr[rows, :] = _dot(cum_mat, gk[rows, :], precision=HIGHEST)
    gcum = gc_scr[...]

    rg = lax.broadcasted_iota(jnp.int32, (tb, 2 * G), 0)
    cg = lax.broadcasted_iota(jnp.int32, (tb, 2 * G), 1)
    blk_mask = ((cg % G) // C == (rg % G) // C) & (cg % C <= rg % C)
    for p in range(GLA_HEADS // 2):
        ls = slice(p * 128, (p + 1) * 128)
        q3 = q[:, ls].reshape(nc, C, 128)
        k3 = k[:, ls].reshape(nc, C, 128)
        g3 = gcum[:, ls].reshape(nc, C, 128) * LOG2_E
        half = C // 2
        for j in range(C):
            lo = 0 if j < half else half
            e = (q3[:, lo:] * jnp.exp2(jnp.minimum(g3[:, lo:] - g3[:, j : j + 1, :], 0.0))) * k3[:, j : j + 1, :]
            if lo:
                e = jnp.concatenate([jnp.zeros((nc, lo, 128), F32), e], axis=1)
            x_scr[:, j * 128 : (j + 1) * 128] = e.reshape(tb, 128).astype(BF16)
        a_t = jnp.where(blk_mask, _dot(x_scr[...], wsel_ref[...]), 0.0).astype(BF16)
        for hl in range(2):
            h = 2 * p + hl
            for m in range(tb // G):
                rows = slice(m * G, (m + 1) * G)
                a_blk = a_t[rows, hl * G : (hl + 1) * G]
                oi_scr[rows, h * GLA_DV : (h + 1) * GLA_DV] = _dot(
                    a_blk, v[rows, h * GLA_DV : (h + 1) * GLA_DV].astype(BF16))

    CG = G // C
    rt = lax.broadcasted_iota(jnp.int32, (G, CG * 128), 0)
    ct = lax.broadcasted_iota(jnp.int32, (G, CG * 128), 1)
    own_chunk = rt // C == ct // 128
    rs = lax.broadcasted_iota(jnp.int32, (2 * GLA_DV, CG * 128), 0)
    cs = lax.broadcasted_iota(jnp.int32, (2 * GLA_DV, CG * 128), 1)
    same_head = rs // GLA_DV == (cs % 128) // GLA_DK

    def chunk_diag(x):
        return jnp.where(own_chunk, jnp.concatenate([x] * CG, axis=1), 0.0)

    for m in range(tb // G):
        rows = slice(m * G, (m + 1) * G)
        for p in range(GLA_HEADS // 2):
            ls = slice(p * 128, (p + 1) * 128)
            vs = slice(p * 2 * GLA_DV, (p + 1) * 2 * GLA_DV)
            g_g = gcum[rows, ls]
            g3 = g_g.reshape(CG, C, 128)
            g_last = jnp.broadcast_to(g3[:, C - 1 : C, :], (CG, C, 128)).reshape(G, 128)
            q_t = q[rows, ls] * jnp.exp(g_g)
            k_t = k[rows, ls] * jnp.exp(g_last - g_g)
            d_s = jnp.where(same_head, _dot_tn(v[rows, vs], chunk_diag(k_t)), 0.0)
            st = st_scr[p]
            starts = []
            for c in range(CG):
                starts.append(st)
                decay = jnp.exp(g_g[c * C + C - 1 : c * C + C, :])
                st = st * decay + d_s[:, c * 128 : (c + 1) * 128]
            st_scr[p] = st
            oi_scr[rows, vs] += _dot_nt(chunk_diag(q_t), jnp.concatenate(starts, axis=1))

    o_ref[0] = _gla_finish(oi_scr[...], g, nw_ref[...]).astype(o_ref.dtype)

    @pl.when(t_idx == pl.num_programs(1) - 1)
    def _():
        for p in range(GLA_HEADS // 2):
            st = st_scr[p]
            sout_ref[0, 2 * p] = st[:GLA_DV, :GLA_DK].T
            sout_ref[0, 2 * p + 1] = st[GLA_DV:, GLA_DK:].T


def _gla_select_matrix():
    j = jnp.arange(GLA_CHUNK)[:, None, None]
    hl = jnp.arange(2)[None, :, None]
    rows_j = jnp.broadcast_to(j, (GLA_CHUNK, 2, GLA_DK)).reshape(-1)
    rows_h = jnp.broadcast_to(hl, (GLA_CHUNK, 2, GLA_DK)).reshape(-1)
    cols = jnp.arange(256)
    sel = (rows_j[:, None] == cols[None, :] % GLA_CHUNK) & (rows_h[:, None] == cols[None, :] // 128)
    return sel.astype(BF16)


def _gla_seq(z3, s0, gl, tb):
    b, l, _ = z3.shape
    const = lambda shape: pl.BlockSpec(shape, lambda i, j: (0,) * len(shape))
    return pl.pallas_call(
        _gla_seq_kernel,
        grid=(b, l // tb),
        in_specs=[
            pl.BlockSpec((1, tb, GLA_PROJ_PAD), lambda i, j: (i, j, 0)),
            pl.BlockSpec((1, GLA_HEADS, GLA_DK, GLA_DV), lambda i, j: (i, 0, 0, 0)),
            const((GLA_LORA_PAD, GLA_QK)), const((1, GLA_QK)), const((1, GLA_DV)),
            const((GLA_CHUNK * 128, 256)),
        ],
        out_specs=[
            pl.BlockSpec((1, tb, GLA_WIDTH), lambda i, j: (i, j, 0)),
            pl.BlockSpec((1, GLA_HEADS, GLA_DK, GLA_DV), lambda i, j: (i, 0, 0, 0)),
        ],
        out_shape=[
            jax.ShapeDtypeStruct((b, l, GLA_WIDTH), BF16),
            jax.ShapeDtypeStruct((b, GLA_HEADS, GLA_DK, GLA_DV), F32),
        ],
        scratch_shapes=[
            pltpu.VMEM((GLA_HEADS // 2, 2 * GLA_DV, 2 * GLA_DK), F32),
            pltpu.VMEM((tb, GLA_CHUNK * 128), BF16),
            pltpu.VMEM((tb, GLA_QK), F32), pltpu.VMEM((tb, GLA_WIDTH), F32),
        ],
        compiler_params=_params(("parallel", "arbitrary")),
    )(z3, s0, gl["gkw"], gl["gkb"], gl["norm_w"], _gla_select_matrix())


def _gla_step_prep_kernel(z_ref, gkw_ref, gkb_ref, q_ref, k_ref, v_ref, g_ref, dec_ref):
    q, k, v, g, gk = _gla_features(z_ref[...], gkw_ref[...], gkb_ref[...])
    q_ref[...] = q
    k_ref[...] = k
    v_ref[...] = v
    g_ref[...] = g
    dec_ref[...] = jnp.exp(gk)


def _gla_step_prep(z, gl):
    n = z.shape[0]
    qk = jax.ShapeDtypeStruct((n, GLA_QK), F32)
    wide = jax.ShapeDtypeStruct((n, GLA_WIDTH), F32)
    return pl.pallas_call(
        _gla_step_prep_kernel,
        out_shape=[qk, qk, wide, wide, qk],
        compiler_params=pltpu.CompilerParams(vmem_limit_bytes=VMEM_LIMIT),
    )(z, gl["gkw"], gl["gkb"])


def _gla_step_kernel(s_ref, q_ref, k_ref, dec_ref, v_ref, snew_ref, o_ref):
    bb = s_ref.shape[0]
    rows = lax.broadcasted_iota(jnp.int32, (bb, bb * GLA_DK), 0)
    cols = lax.broadcasted_iota(jnp.int32, (bb, bb * GLA_DK), 1)
    own = rows == cols // GLA_DK
    ones = jnp.ones((bb, GLA_DV), F32)
    zeros = jnp.zeros((bb, GLA_DV), F32)

    def seq_diag(x):
        return jnp.where(own, jnp.concatenate([x] * bb, axis=1), 0.0)

    for h in range(GLA_HEADS):
        ks = slice(h * GLA_DK, (h + 1) * GLA_DK)
        vs = slice(h * GLA_DV, (h + 1) * GLA_DV)
        s = s_ref[:, h].reshape(bb * GLA_DK, GLA_DV)
        v = v_ref[:, vs]
        lhs_t = jnp.concatenate([seq_diag(k_ref[:, ks]), seq_diag(dec_ref[:, ks])], axis=0)
        rhs = jnp.concatenate([jnp.concatenate([v, zeros], axis=1), jnp.concatenate([zeros, ones], axis=1)], axis=0)
        both = _dot_tn(lhs_t, rhs, precision=HIGHEST)
        s_new = s * both[:, GLA_DV:] + both[:, :GLA_DV]
        snew_ref[:, h] = s_new.reshape(bb, GLA_DK, GLA_DV)
        o_ref[:, vs] = _dot(seq_diag(q_ref[:, ks]), s_new)


def _gla_step(s0, q, k, dec, v, bb):
    n = s0.shape[0]
    s_spec = pl.BlockSpec((bb, GLA_HEADS, GLA_DK, GLA_DV), lambda i: (i, 0, 0, 0))
    qk_spec = pl.BlockSpec((bb, GLA_QK), lambda i: (i, 0))
    v_spec = pl.BlockSpec((bb, GLA_WIDTH), lambda i: (i, 0))
    return pl.pallas_call(
        _gla_step_kernel,
        grid=(n // bb,),
        in_specs=[s_spec, qk_spec, qk_spec, qk_spec, v_spec],
        out_specs=[s_spec, v_spec],
        out_shape=[jax.ShapeDtypeStruct(s0.shape, F32), jax.ShapeDtypeStruct((n, GLA_WIDTH), F32)],
        compiler_params=_params(("parallel",)),
    )(s0, q, k, dec, v)


def _step_post_kernel(orw_ref, bon_ref, grw_ref, lnw_ref, lnb_ref, ogl_ref, ggl_ref, nw_ref, o_rw_ref, o_gl_ref):
    o = orw_ref[...]
    for h in range(RW_HEADS):
        hs = slice(h * RW_N, (h + 1) * RW_N)
        o_h = o[:, hs]
        mean = jnp.mean(o_h, axis=-1, keepdims=True)
        cen = o_h - mean
        var = jnp.mean(cen * cen, axis=-1, keepdims=True)
        on = cen * lax.rsqrt(var + RW_GN_EPS)
        res = (on * lnw_ref[:, hs] + lnb_ref[:, hs] + bon_ref[:, hs]) * grw_ref[:, hs]
        o_rw_ref[:, hs] = res.astype(o_rw_ref.dtype)
    o_gl_ref[...] = _gla_finish(ogl_ref[...], ggl_ref[...], nw_ref[...]).astype(o_gl_ref.dtype)


def _step_post(o_rw, bonus, g_rw, rw, o_gl, g_gl, gl):
    n = o_rw.shape[0]
    return pl.pallas_call(
        _step_post_kernel,
        out_shape=[jax.ShapeDtypeStruct((n, RW_WIDTH), BF16), jax.ShapeDtypeStruct((n, GLA_WIDTH), BF16)],
        compiler_params=pltpu.CompilerParams(vmem_limit_bytes=VMEM_LIMIT),
    )(o_rw, bonus, g_rw, rw["ln_w"], rw["ln_b"], o_gl, g_gl, gl["norm_w"])


def _outproj_router_kernel(x_ref, orw_ref, ogl_ref, wo_ref, gain_ref, wrt_ref, br_ref, before_ref,
                           h_ref, xn_ref, idx_ref, gate_ref, rank_ref, cnt_ref):
    mix = jnp.concatenate([orw_ref[...], ogl_ref[...]], axis=-1)
    h = x_ref[...] + _dot(mix, wo_ref[...])
    h_ref[...] = h
    xn = h * lax.rsqrt(jnp.mean(h * h, axis=-1, keepdims=True) + NORM_EPS) * gain_ref[...]
    _store_row_tiles(xn_ref, xn)
    logits = _dot_nt(wrt_ref[...], xn) + br_ref[...]
    eidx = lax.broadcasted_iota(jnp.int32, logits.shape, 0)
    vals, idxs = [], []
    work = logits
    chosen = jnp.zeros(logits.shape, F32)
    for _ in range(TOP_K):
        m = jnp.max(work, axis=0, keepdims=True)
        sel = jnp.min(jnp.where(work == m, eidx, N_EXPERTS), axis=0, keepdims=True)
        hit = eidx == sel
        work = jnp.where(hit, -jnp.inf, work)
        chosen = chosen + hit.astype(F32)
        vals.append(m)
        idxs.append(sel)
    prefix = _dot(chosen.astype(BF16), before_ref[...])
    exps = [jnp.exp(v - vals[0]) for v in vals]
    denom = exps[0] + exps[1] + exps[2] + exps[3]
    for j in range(TOP_K):
        idx_ref[0, j : j + 1, :] = idxs[j]
        gate_ref[0, j : j + 1, :] = exps[j] / denom
        rank = jnp.sum(jnp.where(eidx == idxs[j], prefix, 0.0), axis=0, keepdims=True)
        rank_ref[0, j : j + 1, :] = rank.astype(jnp.int32)
    cnt = jnp.sum(chosen, axis=1, keepdims=True)
    cnt_ref[0] = jnp.broadcast_to(cnt, (N_EXPERTS, 128)).astype(jnp.int32)


def _outproj_router(x, o_rw, o_gl, w_out, gain, w_router_t, b_router, tm):
    n = x.shape[0]
    nt = n // tm
    const = lambda shape: pl.BlockSpec(shape, lambda i: (0,) * len(shape))
    tok = lambda width: pl.BlockSpec((tm, width), lambda i: (i, 0))
    lane = pl.BlockSpec((1, TOP_K, tm), lambda i: (i, 0, 0))
    t = jnp.arange(tm, dtype=jnp.int32)
    before = (t[:, None] < t[None, :]).astype(BF16)
    return pl.pallas_call(
        _outproj_router_kernel,
        grid=(nt,),
        in_specs=[
            tok(D_MODEL), tok(RW_WIDTH), tok(GLA_WIDTH),
            const((D_MODEL, D_MODEL)), const((1, D_MODEL)), const((N_EXPERTS, D_MODEL)), const((N_EXPERTS, 1)),
            const((tm, tm)),
        ],
        out_specs=[tok(D_MODEL), pl.BlockSpec((tm * ROW_TILE, 128), lambda i: (i, 0)), lane, lane, lane,
                   pl.BlockSpec((1, N_EXPERTS, 128), lambda i: (i, 0, 0))],
        out_shape=[
            jax.ShapeDtypeStruct((n, D_MODEL), F32),
            jax.ShapeDtypeStruct((n * ROW_TILE, 128), F32),
            jax.ShapeDtypeStruct((nt, TOP_K, tm), jnp.int32),
            jax.ShapeDtypeStruct((nt, TOP_K, tm), F32),
            jax.ShapeDtypeStruct((nt, TOP_K, tm), jnp.int32),
            jax.ShapeDtypeStruct((nt, N_EXPERTS, 128), jnp.int32),
        ],
        compiler_params=_params(("parallel",)),
    )(x, o_rw, o_gl, w_out, gain, w_router_t, b_router, before)


def _moe_kernel(be_ref, nu_ref, epoch_ref, next_ref, rows_ref, parts_ref, xs_ref, wg_hbm, wu_hbm, wd_hbm, bg_ref, bu_ref, bd_ref,
                y_ref, w_f32, wg_b, wu_b, wd_b, sems):
    b = pl.program_id(0)
    prev = be_ref[jnp.maximum(b - 1, 0)]
    new_expert = (b == 0) | (be_ref[b] != prev)

    def fetch(e, slot, i):
        w = (wg_hbm, wu_hbm, wd_hbm)[i]
        return pltpu.make_async_copy(w.at[e], w_f32.at[slot, i], sems.at[slot])

    @pl.when(b == 0)
    def _():
        for i in range(3):
            fetch(be_ref[0], 0, i).start()

    @pl.when(new_expert)
    def _():
        slot = epoch_ref[b] % 2
        for i in range(3):
            fetch(be_ref[b], slot, i).wait()
        wg_b[...] = w_f32[slot, 0].astype(BF16)
        wu_b[...] = w_f32[slot, 1].astype(BF16)
        wd_b[...] = w_f32[slot, 2].astype(BF16)

    first_part, end_part = parts_ref[b] // 4, parts_ref[b] % 4
    for i in range(3):
        @pl.when((next_ref[b] >= 0) & (first_part <= i) & (i < end_part))
        def _():
            fetch(next_ref[b], 1 - epoch_ref[b] % 2, i).start()

    def ffn(m):
        x = _load_row_tiles(xs_ref, m).astype(BF16)
        half = D_MODEL // 2
        acc = None
        for f in range(2):
            fs = slice(f * half, (f + 1) * half)
            gt = _dot(x, wg_b[:, fs]) + bg_ref[0, :, fs]
            up = _dot(x, wu_b[:, fs]) + bu_ref[0, :, fs]
            gt = jnp.minimum(gt, SWIGLU_LIMIT)
            up = jnp.clip(up, -SWIGLU_LIMIT, SWIGLU_LIMIT)
            hid = (up + 1.0) * gt * _sigmoid(SWIGLU_ALPHA * gt)
            part = _dot(hid.astype(BF16), wd_b[fs, :])
            acc = part if acc is None else acc + part
        _store_row_tiles(y_ref, acc + bd_ref[0])

    pieces = (rows_ref[b] + MOE_PIECE - 1) // MOE_PIECE
    for q in range(1, MOE_BLOCK // MOE_PIECE + 1):
        @pl.when((b < nu_ref[0]) & (pieces == q))
        def _():
            ffn(q * MOE_PIECE)


def _moe_ffn(block_expert, n_used, block_rows, xs, w_gate, w_up, w_down, b_gate, b_up, b_down):
    n_blocks = block_expert.shape[0]
    pos = jnp.arange(n_blocks, dtype=jnp.int32)
    change = (pos > 0) & (block_expert != jnp.roll(block_expert, 1))
    epoch = jnp.cumsum(change.astype(jnp.int32))
    later = change[None, :] & (pos[None, :] > pos[:, None])
    first = jnp.min(jnp.where(later, pos[None, :], n_blocks), axis=1)
    next_e = jnp.sum(jnp.where(pos[None, :] == first[:, None], block_expert[None, :], 0), axis=1)
    next_e = jnp.where(first < n_blocks, next_e, -1).astype(jnp.int32)
    run_start = jnp.max(jnp.where((pos[None, :] <= pos[:, None]) & (change | (pos == 0))[None, :], pos[None, :], 0),
                        axis=1)
    q = jnp.minimum(pos - run_start, 3)
    used = pos < n_used[0]
    is_last = jnp.roll(change, -1) | (pos == n_used[0] - 1)
    parts = jnp.where(used, q * 4 + jnp.where(is_last, 3, jnp.minimum(q + 1, 3)), 15).astype(jnp.int32)

    row = lambda b, be, nu, ep, nx, br, pt: (jnp.minimum(b, nu[0] - 1), 0)
    bspec = pl.BlockSpec((1, 1, D_MODEL), lambda b, be, nu, ep, nx, br, pt: (be[b], 0, 0))
    wspec = pl.BlockSpec(memory_space=pl.ANY)
    grid_spec = pltpu.PrefetchScalarGridSpec(
        num_scalar_prefetch=6,
        grid=(n_blocks,),
        in_specs=[pl.BlockSpec((MOE_BLOCK * ROW_TILE, 128), row), wspec, wspec, wspec, bspec, bspec, bspec],
        out_specs=pl.BlockSpec((MOE_BLOCK * ROW_TILE, 128), row),
        scratch_shapes=[pltpu.VMEM((2, 3, D_MODEL, D_MODEL), F32)] + [pltpu.VMEM((D_MODEL, D_MODEL), BF16)] * 3
        + [pltpu.SemaphoreType.DMA((2,))],
    )
    return pl.pallas_call(
        _moe_kernel,
        grid_spec=grid_spec,
        out_shape=jax.ShapeDtypeStruct((n_blocks * MOE_BLOCK * ROW_TILE, 128), F32),
        compiler_params=_params(("arbitrary",)),
    )(block_expert, n_used, epoch, next_e, block_rows, parts, xs, w_gate, w_up, w_down,
      b_gate.reshape(N_EXPERTS, 1, D_MODEL), b_up.reshape(N_EXPERTS, 1, D_MODEL),
      b_down.reshape(N_EXPERTS, 1, D_MODEL))


SEG_ALIGN = 8
GROUP_ROWS = SEG_ALIGN * ROW_TILE


def _local_rows(tm):
    return tm * TOP_K + N_EXPERTS * SEG_ALIGN


BIG_COPY = 4


def _copy_tables():
    n_big = _local_rows(TOK_BLOCK) // SEG_ALIGN // BIG_COPY
    n_small = N_EXPERTS * (BIG_COPY - 1)
    return 2 * n_big, 2 * (n_big + n_small)


def _issue_group_copies(cp_ref, hbm, buf, sem, to_hbm):
    small_at, counts_at = _copy_tables()

    def copy(at, n_groups):
        rows = n_groups * GROUP_ROWS
        b = buf.at[pl.ds(pl.multiple_of(cp_ref[0, 0, at] * GROUP_ROWS, GROUP_ROWS), rows)]
        h = hbm.at[pl.ds(pl.multiple_of(cp_ref[0, 0, at + 1] * GROUP_ROWS, GROUP_ROWS), rows)]
        return pltpu.make_async_copy(b, h, sem) if to_hbm else pltpu.make_async_copy(h, b, sem)

    def big(i, carry):
        copy(2 * i, BIG_COPY).start()
        return carry

    def small(i, carry):
        copy(small_at + 2 * i, 1).start()
        return carry

    lax.fori_loop(0, cp_ref[0, 0, counts_at], big, 0)
    lax.fori_loop(0, cp_ref[0, 0, counts_at + 1], small, 0)


def _wait_group_copies(cp_ref, hbm, buf, sem, to_hbm):
    rows = pl.ds(0, pl.multiple_of(cp_ref[0, 0, _copy_tables()[1] + 2] * GROUP_ROWS, GROUP_ROWS))
    b, h = buf.at[rows], hbm.at[rows]
    (pltpu.make_async_copy(b, h, sem) if to_hbm else pltpu.make_async_copy(h, b, sem)).wait()


def _combine_kernel(grp_c, grp_n, lpos_ref, gate_ref, h_ref, gain_ref, y_hbm, o_ref, ybuf, fbuf, sems):
    i = pl.program_id(0)
    nt = pl.num_programs(0)
    tm = h_ref.shape[0]

    @pl.when(i == 0)
    def _():
        _issue_group_copies(grp_c, y_hbm, ybuf.at[0], sems.at[0], False)

    @pl.when(i + 1 < nt)
    def _():
        _issue_group_copies(grp_n, y_hbm, ybuf.at[(i + 1) % 2], sems.at[(i + 1) % 2], False)

    slot = i % 2
    yb = ybuf.at[slot]
    _wait_group_copies(grp_c, y_hbm, yb, sems.at[slot], False)

    def token_body(t, carry):
        acc = None
        for j in range(TOP_K):
            row = pl.multiple_of(lpos_ref[0, 0, t * TOP_K + j], ROW_TILE)
            term = gate_ref[0, 0, t * TOP_K + j] * yb[pl.ds(row, ROW_TILE), :]
            acc = term if acc is None else acc + term
        fbuf[pl.ds(pl.multiple_of(t * ROW_TILE, ROW_TILE), ROW_TILE), :] = acc
        return carry

    lax.fori_loop(0, tm, token_body, 0, unroll=8)
    f = h_ref[...] + _load_row_tiles(fbuf, tm)
    o_ref[...] = f * lax.rsqrt(jnp.mean(f * f, axis=-1, keepdims=True) + NORM_EPS) * gain_ref[...]


def _combine(h, y_rows, grp3, lpos3, gate3, gain, tm):
    n = h.shape[0]
    nt = n // tm
    n_local = _local_rows(tm)
    gw = grp3.shape[-1]
    smem = lambda shape, imap: pl.BlockSpec(shape, imap, memory_space=pltpu.SMEM)
    cur = lambda i: (i, 0, 0)
    nxt = lambda i: (jnp.minimum(i + 1, nt - 1), 0, 0)
    return pl.pallas_call(
        _combine_kernel,
        grid=(nt,),
        in_specs=[
            smem((1, 1, gw), cur), smem((1, 1, gw), nxt),
            smem((1, 1, TOP_K * tm), cur), smem((1, 1, TOP_K * tm), cur),
            pl.BlockSpec((tm, D_MODEL), lambda i: (i, 0)),
            pl.BlockSpec((1, D_MODEL), lambda i: (0, 0)),
            pl.BlockSpec(memory_space=pl.ANY),
        ],
        out_specs=pl.BlockSpec((tm, D_MODEL), lambda i: (i, 0)),
        out_shape=jax.ShapeDtypeStruct((n, D_MODEL), F32),
        scratch_shapes=[pltpu.VMEM((2, n_local * ROW_TILE, 128), F32), pltpu.VMEM((tm * ROW_TILE, 128), F32),
                        pltpu.SemaphoreType.DMA((2,))],
        compiler_params=pltpu.CompilerParams(dimension_semantics=("arbitrary",), vmem_limit_bytes=VMEM_LIMIT,
                                             disable_bounds_checks=True),
    )(grp3, grp3, lpos3, gate3, h, gain, y_rows)


def _dispatch_kernel(*refs, fill):
    if fill:
        grp_ref, grp_prev, lpos_ref, ends_ref, x_ref, xs_hbm, sorted_buf, zero_scr, sems, zsem = refs
    else:
        grp_ref, grp_prev, lpos_ref, x_ref, _, xs_hbm, sorted_buf, sems = refs
    i = pl.program_id(0)
    tm = x_ref.shape[0] // ROW_TILE
    blk = MOE_PIECE * ROW_TILE
    sorted_scr = sorted_buf.at[i % 2]
    sem = sems.at[i % 2]

    if fill:
        def fill_copy(e):
            start = pl.multiple_of(ends_ref[0, e] * ROW_TILE, blk)
            return pltpu.make_async_copy(zero_scr, xs_hbm.at[pl.ds(start, blk)], zsem)

        @pl.when(i == 0)
        def _():
            zero_scr[...] = jnp.zeros(zero_scr.shape, zero_scr.dtype)
            for e in range(N_EXPERTS):
                @pl.when(ends_ref[1, e] > 0)
                def _():
                    fill_copy(e).start()

    tails_at = _copy_tables()[1] + 8
    for e in range(N_EXPERTS):
        tail = pl.multiple_of(grp_ref[0, 0, tails_at + e] * GROUP_ROWS, GROUP_ROWS)
        sorted_scr[pl.ds(tail, GROUP_ROWS), :] = jnp.zeros((GROUP_ROWS, 128), F32)

    def move(t, carry):
        row = x_ref[pl.ds(pl.multiple_of(t * ROW_TILE, ROW_TILE), ROW_TILE), :]
        for j in range(TOP_K):
            dst = pl.multiple_of(lpos_ref[0, 0, t * TOP_K + j], ROW_TILE)
            sorted_scr[pl.ds(dst, ROW_TILE), :] = row
        return carry

    lax.fori_loop(0, tm, move, 0, unroll=8)

    if fill:
        @pl.when(i == 0)
        def _():
            for e in range(N_EXPERTS):
                @pl.when(ends_ref[1, e] > 0)
                def _():
                    fill_copy(e).wait()

    @pl.when(i > 0)
    def _():
        _wait_group_copies(grp_prev, xs_hbm, sorted_buf.at[(i + 1) % 2], sems.at[(i + 1) % 2], True)

    _issue_group_copies(grp_ref, xs_hbm, sorted_scr, sem, True)

    @pl.when(i == pl.num_programs(0) - 1)
    def _():
        _wait_group_copies(grp_ref, xs_hbm, sorted_scr, sem, True)


def _dispatch(xn, grp3, lpos3, tm, n_slots, ends=None, xs=None):
    n = xn.shape[0] // ROW_TILE
    fill = xs is None
    smem = lambda shape, imap: pl.BlockSpec(shape, imap, memory_space=pltpu.SMEM)
    in_specs = [smem((1, 1, grp3.shape[-1]), lambda i: (i, 0, 0)),
                smem((1, 1, grp3.shape[-1]), lambda i: (jnp.maximum(i - 1, 0), 0, 0)),
                smem((1, 1, TOP_K * tm), lambda i: (i, 0, 0))]
    args = [grp3, grp3, lpos3]
    scratch = [pltpu.VMEM((2, _local_rows(tm) * ROW_TILE, 128), F32)]
    if fill:
        in_specs.append(smem((2, N_EXPERTS), lambda i: (0, 0)))
        args.append(ends)
        scratch.append(pltpu.VMEM((MOE_PIECE * ROW_TILE, 128), F32))
    in_specs.append(pl.BlockSpec((tm * ROW_TILE, 128), lambda i: (i, 0)))
    args.append(xn)
    aliases = {}
    if not fill:
        in_specs.append(pl.BlockSpec(memory_space=pl.ANY))
        args.append(xs)
        aliases = {len(args) - 1: 0}
    scratch.append(pltpu.SemaphoreType.DMA((2,)))
    if fill:
        scratch.append(pltpu.SemaphoreType.DMA(()))
    return pl.pallas_call(
        functools.partial(_dispatch_kernel, fill=fill),
        grid=(n // tm,),
        in_specs=in_specs,
        out_specs=pl.BlockSpec(memory_space=pl.ANY),
        out_shape=jax.ShapeDtypeStruct((n_slots * ROW_TILE, 128), F32),
        scratch_shapes=scratch,
        input_output_aliases=aliases,
        compiler_params=pltpu.CompilerParams(dimension_semantics=("arbitrary",), vmem_limit_bytes=VMEM_LIMIT,
                                             disable_bounds_checks=True, has_side_effects=True),
    )(*args)


def _pad_rows(w, rows, offset):
    out = jnp.zeros((rows, w.shape[1]), w.dtype)
    return out.at[offset : offset + w.shape[0]].set(w)


def _routing_tables(counts, n_pairs):
    n_tiles = counts.shape[0]
    n_blocks = (n_pairs + n_tiles * N_EXPERTS * (SEG_ALIGN - 1) + N_EXPERTS * (MOE_BLOCK - 1)
                + MOE_BLOCK - 1) // MOE_BLOCK
    runs = (counts + SEG_ALIGN - 1) // SEG_ALIGN * SEG_ALIGN
    local_start = jnp.cumsum(runs, axis=1) - runs
    total = jnp.sum(runs, axis=0)
    padded = (total + MOE_BLOCK - 1) // MOE_BLOCK * MOE_BLOCK
    pends = jnp.cumsum(padded)
    pstarts = pends - padded
    global_start = pstarts[None, :] + jnp.cumsum(runs, axis=0) - runs
    blocks = jnp.arange(n_blocks, dtype=jnp.int32) * MOE_BLOCK
    n_used = (pends[-1] // MOE_BLOCK).astype(jnp.int32)
    owner = jnp.sum((pends[None, :] <= blocks[:, None]).astype(jnp.int32), axis=1)
    block_expert = jnp.minimum(owner, N_EXPERTS - 1)
    last = jnp.sum(jnp.where(jnp.arange(n_blocks) == n_used - 1, block_expert, 0))
    block_expert = jnp.where(jnp.arange(n_blocks) < n_used, block_expert, last)
    real_end = pstarts + total
    ends = jnp.stack([real_end // MOE_PIECE * MOE_PIECE, real_end % MOE_PIECE]).astype(jnp.int32)
    row_end = jnp.sum(jnp.where(block_expert[:, None] == jnp.arange(N_EXPERTS), (pstarts + total)[None, :], 0), axis=1)
    block_rows = jnp.clip(row_end - blocks, 0, MOE_BLOCK).astype(jnp.int32)
    return (runs, local_start, global_start, ends, block_expert.astype(jnp.int32), n_used.reshape(1), block_rows,
            n_blocks)


def _local_positions(idx3, rank3, local_start):
    hit = idx3[..., None] == jnp.arange(N_EXPERTS, dtype=jnp.int32)
    lpos = rank3 + jnp.sum(jnp.where(hit, local_start[:, None, None, :], 0), axis=-1)
    return _token_major(lpos.astype(jnp.int32) * ROW_TILE)


def _copy_lists(runs, local_start, global_start):
    experts = jnp.arange(N_EXPERTS, dtype=jnp.int32)
    groups = runs // SEG_ALIGN
    l8, g8 = local_start // SEG_ALIGN, global_start // SEG_ALIGN
    n_big, n_small = groups // BIG_COPY, groups % BIG_COPY

    def copy_list(count, first_local, first_global, step, length):
        ends = jnp.cumsum(count, axis=1)
        o = jnp.arange(length, dtype=jnp.int32)
        owner = jnp.minimum(jnp.sum((ends[:, None, :] <= o[None, :, None]).astype(jnp.int32), axis=-1), N_EXPERTS - 1)
        sel = owner[..., None] == experts
        pick = lambda t: jnp.sum(jnp.where(sel, t[:, None, :], 0), axis=-1)
        k = (o[None, :] - pick(ends - count)) * step
        pairs = jnp.stack([pick(first_local) + k, pick(first_global) + k], axis=-1)
        live = (o[None, :] < ends[:, -1:])[..., None]
        return jnp.where(live, pairs, 0).reshape(count.shape[0], 2 * length)

    small_at, counts_at = _copy_tables()
    big = copy_list(n_big, l8, g8, BIG_COPY, small_at // 2)
    small = copy_list(n_small, l8 + n_big * BIG_COPY, g8 + n_big * BIG_COPY, 1, (counts_at - small_at) // 2)
    counts = jnp.stack([jnp.sum(n_big, axis=1), jnp.sum(n_small, axis=1), jnp.sum(groups, axis=1)], axis=1)
    counts = jnp.pad(counts, ((0, 0), (0, 5)))
    tails = l8 + jnp.maximum(groups - 1, 0)
    return jnp.concatenate([big, small, counts, tails], axis=1).astype(jnp.int32)[:, None, :]


def _token_major(t3):
    return jnp.swapaxes(t3, 1, 2).reshape(t3.shape[0], 1, -1)


def kernel(x_prompt, x_sample, state_rwkv_shift, state_rwkv_wkv, state_gla, norm_mix, w_in, rw_mu, rw_w0, rw_w2, rw_a0, rw_a2, rw_g2, rw_k_k, rw_k_a, rw_r_k, rw_ln_w, rw_ln_b, gla_gk_w2, gla_gk_b, gla_norm_w, w_out, norm_ffn, w_router, b_router, w_gate, b_gate, w_up, b_up, w_down, b_down, norm_final):
    depth = norm_mix.shape[0]
    assert depth == 1
    bp, lp, d = x_prompt.shape
    bs, ls, _ = x_sample.shape
    assert ls == 1 and lp % SEQ_BLOCK == 0
    l = 0
    row = lambda t: t.reshape(1, -1)

    w_in_b = w_in[l].astype(BF16)
    w_in_r = w_in_b[:, :RW_PROJ]
    w_in_g = jnp.pad(w_in_b[:, RW_PROJ:], ((0, 0), (0, GLA_PROJ_PAD - GLA_PROJ)))
    rw = dict(
        mu=row(rw_mu[l]), w0=row(rw_w0[l]), a0=row(rw_a0[l]),
        w2p=_pad_rows(rw_w2[l].astype(BF16), 128, 0), a2p=_pad_rows(rw_a2[l].astype(BF16), 128, 64),
        g2=rw_g2[l].astype(BF16), k_k=row(rw_k_k[l]), k_a=row(rw_k_a[l]), r_k=row(rw_r_k[l]),
        ln_w=row(rw_ln_w[l]), ln_b=row(rw_ln_b[l]))
    gl = dict(gkw=_pad_rows(gla_gk_w2[l].astype(BF16), GLA_LORA_PAD, 0), gkb=row(gla_gk_b[l]),
              norm_w=row(gla_norm_w[l]))
    gain_mix = row(norm_mix[l])

    n_p = bp * lp
    xp = x_prompt.reshape(n_p, d)
    zr_p, zg_p = _inproj(xp, gain_mix, w_in_r, w_in_g, TOK_BLOCK)
    zr_p3 = zr_p.reshape(bp, lp, RW_PROJ)
    o_rw_p, wkv_p = _rwkv_seq(zr_p3, jnp.zeros((bp, 1, RW_PROJ), F32),
                              jnp.zeros((bp, RW_HEADS, RW_N, RW_N), F32), rw, SEQ_BLOCK)
    o_gl_p, gla_p = _gla_seq(zg_p.reshape(bp, lp, GLA_PROJ_PAD),
                             jnp.zeros((bp, GLA_HEADS, GLA_DK, GLA_DV), F32), gl, SEQ_BLOCK)
    shift_p = zr_p3[:, -1, :]

    xs_ = x_sample.reshape(bs, d)
    zr_s, zg_s = _inproj(xs_, gain_mix, w_in_r, w_in_g, bs)
    r, k, v, al, be, dec, g_rw, bonus = _rwkv_step_prep(zr_s, state_rwkv_shift[l], rw)
    wkv_s, o_rw_s = _rwkv_step(state_rwkv_wkv[l], r, k, al, be, dec, v)
    q, kg, vg, g_gl, dec_g = _gla_step_prep(zg_s, gl)
    gla_s, o_gl_s = _gla_step(state_gla[l], q, kg, dec_g, vg, 16)
    o_rw_s2, o_gl_s2 = _step_post(o_rw_s, bonus, g_rw, rw, o_gl_s, g_gl, gl)
    shift_s = zr_s

    w_out_b = w_out[l].astype(BF16)
    router = (w_out_b, row(norm_ffn[l]), w_router[l].T, b_router[l].reshape(N_EXPERTS, 1))
    h_p, xn_p, idx_p, gate_p, rank_p, cnt_p = _outproj_router(
        xp, o_rw_p.reshape(n_p, RW_WIDTH), o_gl_p.reshape(n_p, GLA_WIDTH), *router, TOK_BLOCK)
    h_s, xn_s, idx_s, gate_s, rank_s, cnt_s = _outproj_router(xs_, o_rw_s2, o_gl_s2, *router, bs)
    nt_p = n_p // TOK_BLOCK
    counts = jnp.concatenate([cnt_p[:, :, 0], cnt_s[:, :, 0]], axis=0)
    runs, lstart, gstart, ends, block_expert, n_used, block_rows, n_blocks = _routing_tables(
        counts, (n_p + bs) * TOP_K)
    n_slots = n_blocks * MOE_BLOCK
    lpos_p = _local_positions(idx_p, rank_p, lstart[:nt_p])
    lpos_s = _local_positions(idx_s, rank_s, lstart[nt_p:])
    grp = _copy_lists(runs, lstart, gstart)
    grp_p, grp_s = grp[:nt_p], grp[nt_p:]
    xs_rows = _dispatch(xn_p, grp_p, lpos_p, TOK_BLOCK, n_slots, ends=ends)
    xs_rows = _dispatch(xn_s, grp_s, lpos_s, bs, n_slots, xs=xs_rows)
    y_rows = _moe_ffn(block_expert, n_used, block_rows, xs_rows, w_gate[l], w_up[l], w_down[l], b_gate[l], b_up[l], b_down[l])
    gain_f = row(norm_final)
    y_p = _combine(h_p, y_rows, grp_p, lpos_p, _token_major(gate_p), gain_f, TOK_BLOCK)
    y_s = _combine(h_s, y_rows, grp_s, lpos_s, _token_major(gate_s), gain_f, bs)

    y_prompt = y_p.reshape(bp, lp, d)
    y_sample = y_s.reshape(bs, ls, d)
    return (y_prompt, y_sample, shift_p[None], wkv_p[None], gla_p[None], shift_s[None], wkv_s[None], gla_s[None])
```

```python
import functools

import jax
import jax.numpy as jnp
from jax import lax
from jax.experimental import pallas as pl
from jax.experimental.pallas import tpu as pltpu

F32 = jnp.float32
BF16 = jnp.bfloat16
HIGHEST = lax.Precision.HIGHEST

D_MODEL = 1024
RW_WIDTH = 512
RW_HEADS = 8
RW_N = 64
RW_PROJ = 1792
RW_GN_EPS = 64e-5
GLA_HEADS = 4
GLA_DK = 64
GLA_DV = 128
GLA_WIDTH = 512
GLA_QK = GLA_HEADS * GLA_DK
GLA_PROJ = 1552
GLA_PROJ_PAD = 1664
GLA_LORA_PAD = 128
GLA_GATE_NORMALIZER = 16.0
N_EXPERTS = 32
TOP_K = 4
SWIGLU_LIMIT = 7.0
SWIGLU_ALPHA = 1.702
NORM_EPS = 1e-5
LOG2_E = 1.4426950408889634

RW_CHUNK = 64
GLA_CHUNK = 16
SEQ_BLOCK = 512
RW_SEQ_BLOCK = 1024
TOK_BLOCK = 512
MOE_BLOCK = 512
MOE_PIECE = 128
VMEM_LIMIT = 56 * 1024 * 1024


def _dot(a, b, precision=None):
    return jnp.dot(a, b, preferred_element_type=F32, precision=precision)


def _dot_nt(a, b, precision=None):
    return lax.dot_general(a, b, (((1,), (1,)), ((), ())), preferred_element_type=F32, precision=precision)


def _dot_tn(a, b, precision=None):
    return lax.dot_general(a, b, (((0,), (0,)), ((), ())), preferred_element_type=F32, precision=precision)


def _sigmoid(x):
    return 1.0 / (1.0 + jnp.exp(-x))


def _softplus(x):
    return jnp.maximum(x, 0.0) + jnp.log(1.0 + jnp.exp(-jnp.abs(x)))


def _params(sem):
    return pltpu.CompilerParams(dimension_semantics=sem, vmem_limit_bytes=VMEM_LIMIT)


ROW_TILE = D_MODEL // 128


def _store_row_tiles(ref, x):
    m = x.shape[0]
    for c in range(ROW_TILE):
        ref[pl.ds(c, m, stride=ROW_TILE), :] = x[:, c * 128 : (c + 1) * 128]


def _load_row_tiles(ref, m):
    return jnp.concatenate([ref[pl.ds(c, m, stride=ROW_TILE), :] for c in range(ROW_TILE)], axis=-1)


def _inproj_kernel(x_ref, gain_ref, wr_ref, wg_ref, zr_ref, zg_ref):
    x = x_ref[...]
    xn = x * lax.rsqrt(jnp.mean(x * x, axis=-1, keepdims=True) + NORM_EPS) * gain_ref[...]
    xb = xn.astype(BF16)
    zr_ref[...] = _dot(xb, wr_ref[...])
    zg_ref[...] = _dot(xb, wg_ref[...])


def _inproj(x, gain, w_r, w_g, tm):
    n = x.shape[0]
    return pl.pallas_call(
        _inproj_kernel,
        grid=(n // tm,),
        in_specs=[
            pl.BlockSpec((tm, D_MODEL), lambda i: (i, 0)),
            pl.BlockSpec((1, D_MODEL), lambda i: (0, 0)),
            pl.BlockSpec((D_MODEL, RW_PROJ), lambda i: (0, 0)),
            pl.BlockSpec((D_MODEL, GLA_PROJ_PAD), lambda i: (0, 0)),
        ],
        out_specs=[
            pl.BlockSpec((tm, RW_PROJ), lambda i: (i, 0)),
            pl.BlockSpec((tm, GLA_PROJ_PAD), lambda i: (i, 0)),
        ],
        out_shape=[
            jax.ShapeDtypeStruct((n, RW_PROJ), F32),
            jax.ShapeDtypeStruct((n, GLA_PROJ_PAD), F32),
        ],
        compiler_params=_params(("parallel",)),
    )(x, gain, w_r, w_g)


def _rwkv_features(zs, w0, w2p, a0, a2p, g2, k_k, k_a):
    W = RW_WIDTH
    r = zs[:, 0:W]
    k_raw = zs[:, W : 2 * W]
    v = zs[:, 2 * W : 3 * W]
    zwa = zs[:, 3 * W : 3 * W + 128]
    zg = zs[:, 3 * W + 128 :]
    w = -_softplus(-(w0 + _dot(jnp.tanh(zwa).astype(BF16), w2p))) - 0.5
    log_decay = -jnp.exp(w)
    a = _sigmoid(a0 + _dot(zwa.astype(BF16), a2p))
    g = _dot(_sigmoid(zg).astype(BF16), g2)
    kk_raw = k_raw * k_k
    k = k_raw * (1.0 + (a - 1.0) * k_a)
    return r, k, v, kk_raw, a, log_decay, g


def _level_mask(ri, ci, lvl):
    same = (ri >> (lvl + 1)) == (ci >> (lvl + 1))
    return same & (((ri >> lvl) & 1) == 1) & (((ci >> lvl) & 1) == 0)


def _rwkv_seq_kernel(z_ref, shift0_ref, s0_ref, mu_ref, w0_ref, w2_ref, a0_ref, a2_ref, g2_ref, kk_ref, ka_ref,
                     rk_ref, lnw_ref, lnb_ref, o_ref, sout_ref,
                     m_scr, prev_scr, r_scr, k_scr, v_scr, kkr_scr, a_scr, lw_scr, on_scr, bon_scr):
    C = RW_CHUNK
    N = RW_N
    t_idx = pl.program_id(1)
    tb = z_ref.shape[1]
    zero_nn = jnp.zeros((N, N), F32)

    @pl.when(t_idx == 0)
    def _():
        prev_scr[...] = shift0_ref[0]
        for p in range(RW_HEADS // 2):
            top = jnp.concatenate([s0_ref[0, 2 * p].T, zero_nn], axis=1)
            bot = jnp.concatenate([zero_nn, s0_ref[0, 2 * p + 1].T], axis=1)
            m_scr[p] = jnp.concatenate([top, bot], axis=0)

    z = z_ref[0]
    row = lax.broadcasted_iota(jnp.int32, z.shape, 0)
    z_prev = jnp.where(row == 0, prev_scr[...], pltpu.roll(z, 1, axis=0))
    prev_scr[...] = z[tb - 1 : tb, :]
    zs = z + mu_ref[...] * (z_prev - z)
    r, k, v, kk_raw, a, log_decay, g = _rwkv_features(
        zs, w0_ref[...], w2_ref[...], a0_ref[...], a2_ref[...], g2_ref[...], kk_ref[...], ka_ref[...])
    P2 = 2 * N
    left1 = lax.broadcasted_iota(jnp.int32, (1, P2), 1) < N

    def head_sum(x):
        s0 = jnp.sum(jnp.where(left1, x, 0.0), axis=-1, keepdims=True)
        s1 = jnp.sum(jnp.where(left1, 0.0, x), axis=-1, keepdims=True)
        return jnp.where(left1, s0, s1)

    def head_sum_wide(x):
        return jnp.concatenate([head_sum(x[:, p * P2 : (p + 1) * P2]) for p in range(RW_HEADS // 2)], axis=1)

    alpha = kk_raw * lax.rsqrt(jnp.maximum(head_sum_wide(kk_raw * kk_raw), 1e-24))
    r_scr[...] = r
    k_scr[...] = k
    v_scr[...] = v
    kkr_scr[...] = alpha
    a_scr[...] = alpha * a
    lw_scr[...] = log_decay
    bon_scr[...] = head_sum_wide(r * k * rk_ref[...]) * v

    ri = lax.broadcasted_iota(jnp.int32, (C, P2), 0)
    ci = lax.broadcasted_iota(jnp.int32, (C, P2), 1) % N
    left = lax.broadcasted_iota(jnp.int32, (C, P2), 1) < N
    tril = ri >= ci
    stril = ri > ci
    eye_f = (ri == ci).astype(F32)
    rb = lax.broadcasted_iota(jnp.int32, (P2, P2), 0)
    cb = lax.broadcasted_iota(jnp.int32, (P2, P2), 1)
    same_head = (rb < N) == (cb < N)
    eye_b = rb == cb
    rc = lax.broadcasted_iota(jnp.int32, (C, C), 0)
    cc = lax.broadcasted_iota(jnp.int32, (C, C), 1)
    tril_f = (rc >= cc).astype(F32)

    def bdiag(x):
        return jnp.concatenate([jnp.where(left, x, 0.0), jnp.where(left, 0.0, x)], axis=0)

    n_sub = tb // C
    pairs = range(RW_HEADS // 2)

    def chunk_body(it, carry):
        units = [(s, p) for s in range(n_sub) for p in pairs]
        sls = [pl.ds(pl.multiple_of((it * n_sub + s) * C, C), C) for s in range(n_sub)]
        prep = []
        for s in range(n_sub):
            lw = lw_scr[sls[s], :]
            cum = _dot(tril_f, lw, precision=HIGHEST)
            cum_last = cum[C - 1 : C, :]
            prep.append(dict(
                e_incl=jnp.exp(cum), e_excl=jnp.exp(cum - lw), e_neg=jnp.exp(-cum),
                e_tail=jnp.exp(cum_last - cum), p_last=jnp.exp(cum_last),
                r=r_scr[sls[s], :], k=k_scr[sls[s], :], v=v_scr[sls[s], :], kk=kkr_scr[sls[s], :],
                a=a_scr[sls[s], :]))
        lanes = [slice(p * P2, (p + 1) * P2) for p in pairs]
        get = lambda name: [prep[s][name][:, lanes[p]] for s, p in units]
        r2, k2, v2, al, be = get("r"), get("k"), get("v"), get("kk"), get("a")
        e_incl, e_excl, e_neg, e_tail, p_last = get("e_incl"), get("e_excl"), get("e_neg"), get("e_tail"), get("p_last")
        un = range(len(units))
        al_t = [al[u] * e_excl[u] for u in un]
        r_t = [r2[u] * e_incl[u] for u in un]
        be_n = [be[u] * e_neg[u] for u in un]
        k_n = [k2[u] * e_neg[u] for u in un]
        k_et = [(k2[u] * e_tail[u]).T for u in un]
        be_et = [(be[u] * e_tail[u]).T for u in un]
        v_bd = [bdiag(v2[u]) for u in un]
        lhs = [jnp.concatenate([al_t[u], r_t[u]], axis=0) for u in un]
        s_b = [_dot_nt(lhs[u], bdiag(be_n[u])) for u in un]
        s_k = [_dot_nt(lhs[u], bdiag(k_n[u])) for u in un]
        l_ab = [jnp.where(stril, s_b[u][:C], 0.0) for u in un]
        a_rb = [jnp.where(tril, s_b[u][C:], 0.0) for u in un]
        l_ak = [jnp.where(stril, s_k[u][:C], 0.0) for u in un]
        a_rk = [jnp.where(tril, s_k[u][C:], 0.0) for u in un]
        lakv = [_dot(l_ak[u], v_bd[u]) for u in un]
        arkv = [_dot(a_rk[u], v_bd[u]) for u in un]
        kev = [_dot(k_et[u], v2[u]) for u in un]
        t_inv = [eye_f - jnp.where(_level_mask(ri, ci, 0), l_ab[u], 0.0) for u in un]
        lvl = 1
        while (1 << lvl) < C:
            lm = _level_mask(ri, ci, lvl)
            tn = [_dot(t_inv[u], bdiag(jnp.where(lm, l_ab[u], 0.0))) for u in un]
            t_inv = [t_inv[u] - _dot(tn[u], bdiag(t_inv[u])) for u in un]
            lvl += 1
        a_til = [_dot(t_inv[u], bdiag(al_t[u])) for u in un]
        b_til = [_dot(t_inv[u], bdiag(lakv[u])) for u in un]
        r_hat = [r_t[u] - _dot(a_rb[u], bdiag(a_til[u])) for u in un]
        o_hat = [arkv[u] - _dot(a_rb[u], bdiag(b_til[u])) for u in un]
        g_bd = [jnp.where(same_head, jnp.where(eye_b, p_last[u], 0.0) - _dot(be_et[u], a_til[u]), 0.0) for u in un]
        h_bd = [jnp.where(same_head, kev[u] - _dot(be_et[u], b_til[u]), 0.0) for u in un]
        lhs_m = [jnp.concatenate([r_hat[u], g_bd[u]], axis=0) for u in un]
        for u, (s, p) in enumerate(units):
            res = _dot(lhs_m[u], m_scr[p])
            m_scr[p] = res[C:] + h_bd[u]
            o_p = res[:C] + o_hat[u]
            cen = o_p - head_sum(o_p) * (1.0 / N)
            var = head_sum(cen * cen) * (1.0 / N)
            on_scr[sls[s], lanes[p]] = cen * lax.rsqrt(var + RW_GN_EPS)
        return carry

    lax.fori_loop(0, tb // (C * n_sub), chunk_body, 0)
    out = (on_scr[...] * lnw_ref[...] + lnb_ref[...] + bon_scr[...]) * g
    o_ref[0] = out.astype(o_ref.dtype)

    @pl.when(t_idx == pl.num_programs(1) - 1)
    def _():
        for p in range(RW_HEADS // 2):
            m = m_scr[p]
            sout_ref[0, 2 * p] = m[:N, :N].T
            sout_ref[0, 2 * p + 1] = m[N:, N:].T


def _rwkv_seq(z3, shift0, s0, rw, tb):
    b, l, _ = z3.shape
    const = lambda shape: pl.BlockSpec(shape, lambda i, j: (0,) * len(shape))
    wide = lambda: pltpu.VMEM((tb, RW_WIDTH), F32)
    return pl.pallas_call(
        _rwkv_seq_kernel,
        grid=(b, l // tb),
        in_specs=[
            pl.BlockSpec((1, tb, RW_PROJ), lambda i, j: (i, j, 0)),
            pl.BlockSpec((1, 1, RW_PROJ), lambda i, j: (i, 0, 0)),
            pl.BlockSpec((1, RW_HEADS, RW_N, RW_N), lambda i, j: (i, 0, 0, 0)),
            const((1, RW_PROJ)),
            const((1, RW_WIDTH)), const((128, RW_WIDTH)),
            const((1, RW_WIDTH)), const((128, RW_WIDTH)),
            const((128, RW_WIDTH)),
            const((1, RW_WIDTH)), const((1, RW_WIDTH)), const((1, RW_WIDTH)),
            const((1, RW_WIDTH)), const((1, RW_WIDTH)),
        ],
        out_specs=[
            pl.BlockSpec((1, tb, RW_WIDTH), lambda i, j: (i, j, 0)),
            pl.BlockSpec((1, RW_HEADS, RW_N, RW_N), lambda i, j: (i, 0, 0, 0)),
        ],
        out_shape=[
            jax.ShapeDtypeStruct((b, l, RW_WIDTH), BF16),
            jax.ShapeDtypeStruct((b, RW_HEADS, RW_N, RW_N), F32),
        ],
        scratch_shapes=[
            pltpu.VMEM((RW_HEADS // 2, 2 * RW_N, 2 * RW_N), F32),
            pltpu.VMEM((1, RW_PROJ), F32),
            wide(), wide(), wide(), wide(), wide(), wide(), wide(), wide(),
        ],
        compiler_params=_params(("parallel", "arbitrary")),
    )(z3, shift0, s0, rw["mu"], rw["w0"], rw["w2p"], rw["a0"], rw["a2p"], rw["g2"], rw["k_k"], rw["k_a"],
      rw["r_k"], rw["ln_w"], rw["ln_b"])


def _rwkv_step_prep_kernel(z_ref, shift0_ref, mu_ref, w0_ref, w2_ref, a0_ref, a2_ref, g2_ref, kk_ref, ka_ref,
                           rk_ref, r_ref, k_ref, v_ref, al_ref, be_ref, dec_ref, g_ref, bon_ref):
    z = z_ref[...]
    zs = z + mu_ref[...] * (shift0_ref[...] - z)
    r, k, v, kk_raw, a, log_decay, g = _rwkv_features(
        zs, w0_ref[...], w2_ref[...], a0_ref[...], a2_ref[...], g2_ref[...], kk_ref[...], ka_ref[...])
    rk_all = rk_ref[...]
    for h in range(RW_HEADS):
        hs = slice(h * RW_N, (h + 1) * RW_N)
        kk_h = kk_raw[:, hs]
        nrm = jnp.sqrt(jnp.sum(kk_h * kk_h, axis=-1, keepdims=True))
        al = kk_h / jnp.maximum(nrm, 1e-12)
        al_ref[:, hs] = al
        be_ref[:, hs] = al * a[:, hs]
        bon_ref[:, hs] = jnp.sum(r[:, hs] * k[:, hs] * rk_all[:, hs], axis=-1, keepdims=True) * v[:, hs]
    r_ref[...] = r
    k_ref[...] = k
    v_ref[...] = v
    dec_ref[...] = jnp.exp(log_decay)
    g_ref[...] = g


def _rwkv_step_prep(z, shift0, rw):
    n = z.shape[0]
    out = jax.ShapeDtypeStruct((n, RW_WIDTH), F32)
    return pl.pallas_call(
        _rwkv_step_prep_kernel,
        out_shape=[out] * 8,
        compiler_params=pltpu.CompilerParams(vmem_limit_bytes=VMEM_LIMIT),
    )(z, shift0, rw["mu"], rw["w0"], rw["w2p"], rw["a0"], rw["a2p"], rw["g2"], rw["k_k"], rw["k_a"], rw["r_k"])


def _rwkv_step_kernel(s_ref, r_ref, k_ref, al_ref, be_ref, dec_ref, v_ref, snew_ref, o_ref):
    r, k, al, be, dec = r_ref[...], k_ref[...], al_ref[...], be_ref[...], dec_ref[...]

    def body(g, carry):
        rows = pl.ds(pl.multiple_of(g * 8, 8), 8)
        v8 = v_ref[rows, :]
        outs = []
        for j in range(8):
            s = s_ref[0, g * 8 + j]
            sa = -jnp.sum(s * al, axis=0, keepdims=True)
            s_new = s * dec + sa * be + v8[j : j + 1, :] * k
            snew_ref[0, g * 8 + j] = s_new
            outs.append(jnp.sum(s_new * r, axis=0, keepdims=True))
        o_ref[rows, :] = jnp.concatenate(outs, axis=0)
        return carry

    lax.fori_loop(0, RW_N // 8, body, 0)


def _rwkv_step(s0, r, k, al, be, dec, v):
    n = s0.shape[0]
    s_t = jnp.transpose(s0, (1, 2, 3, 0))
    s_spec = pl.BlockSpec((1, RW_N, RW_N, n), lambda h: (h, 0, 0, 0))
    op_spec = pl.BlockSpec((RW_N, n), lambda h: (h, 0))
    s_new_t, o_t = pl.pallas_call(
        _rwkv_step_kernel,
        grid=(RW_HEADS,),
        in_specs=[s_spec] + [op_spec] * 6,
        out_specs=[s_spec, op_spec],
        out_shape=[
            jax.ShapeDtypeStruct(s_t.shape, F32),
            jax.ShapeDtypeStruct((RW_WIDTH, n), F32),
        ],
        compiler_params=_params(("parallel",)),
    )(s_t, r.T, k.T, al.T, be.T, dec.T, v.T)
    return jnp.transpose(s_new_t, (3, 0, 1, 2)), o_t.T


def _gla_features(z, gkw, gkb):
    q = z[:, 0:GLA_QK] * (GLA_DK ** -0.5)
    k = z[:, GLA_QK : 2 * GLA_QK]
    v = z[:, 2 * GLA_QK : 2 * GLA_QK + GLA_WIDTH]
    g = z[:, 2 * GLA_QK + GLA_WIDTH : 2 * GLA_QK + 2 * GLA_WIDTH]
    zgk = z[:, 2 * GLA_QK + 2 * GLA_WIDTH :]
    gk = -_softplus(-(_dot(zgk.astype(BF16), gkw) + gkb)) / GLA_GATE_NORMALIZER
    return q, k, v, g, gk


def _gla_finish(o, g, norm_w):
    outs = []
    for h in range(GLA_HEADS):
        hs = slice(h * GLA_DV, (h + 1) * GLA_DV)
        o_h = o[:, hs]
        o_h = o_h * lax.rsqrt(jnp.mean(o_h * o_h, axis=-1, keepdims=True) + NORM_EPS) * norm_w
        g_h = g[:, hs]
        outs.append(o_h * (g_h * _sigmoid(g_h)))
    return jnp.concatenate(outs, axis=-1)


def _gla_seq_kernel(z_ref, s0_ref, gkw_ref, gkb_ref, nw_ref, wsel_ref, o_ref, sout_ref,
                    st_scr, x_scr, gc_scr, oi_scr):
    C = GLA_CHUNK
    G = 128
    t_idx = pl.program_id(1)
    tb = z_ref.shape[1]
    nc = tb // C
    zero_vk = jnp.zeros((GLA_DV, GLA_DK), F32)

    @pl.when(t_idx == 0)
    def _():
        for p in range(GLA_HEADS // 2):
            top = jnp.concatenate([s0_ref[0, 2 * p].T, zero_vk], axis=1)
            bot = jnp.concatenate([zero_vk, s0_ref[0, 2 * p + 1].T], axis=1)
            st_scr[p] = jnp.concatenate([top, bot], axis=0)

    q, k, v, g, gk = _gla_features(z_ref[0], gkw_ref[...], gkb_ref[...])
    ri = lax.broadcasted_iota(jnp.int32, (G, G), 0)
    ci = lax.broadcasted_iota(jnp.int32, (G, G), 1)
    cum_mat = ((ri // C == ci // C) & (ri >= ci)).astype(F32)
    for m in range(tb // G):
        rows = slice(m * G, (m + 1) * G)
        gc_scr[rows, :] = _dot(cum_mat, gk[rows, :], precision=HIGHEST)
    gcum = gc_scr[...]

    rg = lax.broadcasted_iota(jnp.int32, (tb, 2 * G), 0)
    cg = lax.broadcasted_iota(jnp.int32, (tb, 2 * G), 1)
    blk_mask = ((cg % G) // C == (rg % G) // C) & (cg % C <= rg % C)
    for p in range(GLA_HEADS // 2):
        ls = slice(p * 128, (p + 1) * 128)
        q3 = q[:, ls].reshape(nc, C, 128)
        k3 = k[:, ls].reshape(nc, C, 128)
        g3 = gcum[:, ls].reshape(nc, C, 128) * LOG2_E
        half = C // 2
        for j in range(C):
            lo = 0 if j < half else half
            e = (q3[:, lo:] * jnp.exp2(jnp.minimum(g3[:, lo:] - g3[:, j : j + 1, :], 0.0))) * k3[:, j : j + 1, :]
            if lo:
                e = jnp.concatenate([jnp.zeros((nc, lo, 128), F32), e], axis=1)
            x_scr[:, j * 128 : (j + 1) * 128] = e.reshape(tb, 128).astype(BF16)
        a_t = jnp.where(blk_mask, _dot(x_scr[...], wsel_ref[...]), 0.0).astype(BF16)
        for hl in range(2):
            h = 2 * p + hl
            for m in range(tb // G):
                rows = slice(m * G, (m + 1) * G)
                a_blk = a_t[rows, hl * G : (hl + 1) * G]
                oi_scr[rows, h * GLA_DV : (h + 1) * GLA_DV] = _dot(
                    a_blk, v[rows, h * GLA_DV : (h + 1) * GLA_DV].astype(BF16))

    CG = G // C
    rt = lax.broadcasted_iota(jnp.int32, (G, CG * 128), 0)
    ct = lax.broadcasted_iota(jnp.int32, (G, CG * 128), 1)
    own_chunk = rt // C == ct // 128
    rs = lax.broadcasted_iota(jnp.int32, (2 * GLA_DV, CG * 128), 0)
    cs = lax.broadcasted_iota(jnp.int32, (2 * GLA_DV, CG * 128), 1)
    same_head = rs // GLA_DV == (cs % 128) // GLA_DK

    def chunk_diag(x):
        return jnp.where(own_chunk, jnp.concatenate([x] * CG, axis=1), 0.0)

    for m in range(tb // G):
        rows = slice(m * G, (m + 1) * G)
        for p in range(GLA_HEADS // 2):
            ls = slice(p * 128, (p + 1) * 128)
            vs = slice(p * 2 * GLA_DV, (p + 1) * 2 * GLA_DV)
            g_g = gcum[rows, ls]
            g3 = g_g.reshape(CG, C, 128)
            g_last = jnp.broadcast_to(g3[:, C - 1 : C, :], (CG, C, 128)).reshape(G, 128)
            q_t = q[rows, ls] * jnp.exp(g_g)
            k_t = k[rows, ls] * jnp.exp(g_last - g_g)
            d_s = jnp.where(same_head, _dot_tn(v[rows, vs], chunk_diag(k_t)), 0.0)
            st = st_scr[p]
            starts = []
            for c in range(CG):
                starts.append(st)
                decay = jnp.exp(g_g[c * C + C - 1 : c * C + C, :])
                st = st * decay + d_s[:, c * 128 : (c + 1) * 128]
            st_scr[p] = st
            oi_scr[rows, vs] += _dot_nt(chunk_diag(q_t), jnp.concatenate(starts, axis=1))

    o_ref[0] = _gla_finish(oi_scr[...], g, nw_ref[...]).astype(o_ref.dtype)

    @pl.when(t_idx == pl.num_programs(1) - 1)
    def _():
        for p in range(GLA_HEADS // 2):
            st = st_scr[p]
            sout_ref[0, 2 * p] = st[:GLA_DV, :GLA_DK].T
            sout_ref[0, 2 * p + 1] = st[GLA_DV:, GLA_DK:].T


def _gla_select_matrix():
    j = jnp.arange(GLA_CHUNK)[:, None, None]
    hl = jnp.arange(2)[None, :, None]
    rows_j = jnp.broadcast_to(j, (GLA_CHUNK, 2, GLA_DK)).reshape(-1)
    rows_h = jnp.broadcast_to(hl, (GLA_CHUNK, 2, GLA_DK)).reshape(-1)
    cols = jnp.arange(256)
    sel = (rows_j[:, None] == cols[None, :] % GLA_CHUNK) & (rows_h[:, None] == cols[None, :] // 128)
    return sel.astype(BF16)


def _gla_seq(z3, s0, gl, tb):
    b, l, _ = z3.shape
    const = lambda shape: pl.BlockSpec(shape, lambda i, j: (0,) * len(shape))
    return pl.pallas_call(
        _gla_seq_kernel,
        grid=(b, l // tb),
        in_specs=[
            pl.BlockSpec((1, tb, GLA_PROJ_PAD), lambda i, j: (i, j, 0)),
            pl.BlockSpec((1, GLA_HEADS, GLA_DK, GLA_DV), lambda i, j: (i, 0, 0, 0)),
            const((GLA_LORA_PAD, GLA_QK)), const((1, GLA_QK)), const((1, GLA_DV)),
            const((GLA_CHUNK * 128, 256)),
        ],
        out_specs=[
            pl.BlockSpec((1, tb, GLA_WIDTH), lambda i, j: (i, j, 0)),
            pl.BlockSpec((1, GLA_HEADS, GLA_DK, GLA_DV), lambda i, j: (i, 0, 0, 0)),
        ],
        out_shape=[
            jax.ShapeDtypeStruct((b, l, GLA_WIDTH), BF16),
            jax.ShapeDtypeStruct((b, GLA_HEADS, GLA_DK, GLA_DV), F32),
        ],
        scratch_shapes=[
            pltpu.VMEM((GLA_HEADS // 2, 2 * GLA_DV, 2 * GLA_DK), F32),
            pltpu.VMEM((tb, GLA_CHUNK * 128), BF16),
            pltpu.VMEM((tb, GLA_QK), F32), pltpu.VMEM((tb, GLA_WIDTH), F32),
        ],
        compiler_params=_params(("parallel", "arbitrary")),
    )(z3, s0, gl["gkw"], gl["gkb"], gl["norm_w"], _gla_select_matrix())


def _gla_step_prep_kernel(z_ref, gkw_ref, gkb_ref, q_ref, k_ref, v_ref, g_ref, dec_ref):
    q, k, v, g, gk = _gla_features(z_ref[...], gkw_ref[...], gkb_ref[...])
    q_ref[...] = q
    k_ref[...] = k
    v_ref[...] = v
    g_ref[...] = g
    dec_ref[...] = jnp.exp(gk)


def _gla_step_prep(z, gl):
    n = z.shape[0]
    qk = jax.ShapeDtypeStruct((n, GLA_QK), F32)
    wide = jax.ShapeDtypeStruct((n, GLA_WIDTH), F32)
    return pl.pallas_call(
        _gla_step_prep_kernel,
        out_shape=[qk, qk, wide, wide, qk],
        compiler_params=pltpu.CompilerParams(vmem_limit_bytes=VMEM_LIMIT),
    )(z, gl["gkw"], gl["gkb"])


def _gla_step_kernel(s_ref, q_ref, k_ref, dec_ref, v_ref, snew_ref, o_ref):
    bb = s_ref.shape[0]
    rows = lax.broadcasted_iota(jnp.int32, (bb, bb * GLA_DK), 0)
    cols = lax.broadcasted_iota(jnp.int32, (bb, bb * GLA_DK), 1)
    own = rows == cols // GLA_DK
    ones = jnp.ones((bb, GLA_DV), F32)
    zeros = jnp.zeros((bb, GLA_DV), F32)

    def seq_diag(x):
        return jnp.where(own, jnp.concatenate([x] * bb, axis=1), 0.0)

    for h in range(GLA_HEADS):
        ks = slice(h * GLA_DK, (h + 1) * GLA_DK)
        vs = slice(h * GLA_DV, (h + 1) * GLA_DV)
        s = s_ref[:, h].reshape(bb * GLA_DK, GLA_DV)
        v = v_ref[:, vs]
        lhs_t = jnp.concatenate([seq_diag(k_ref[:, ks]), seq_diag(dec_ref[:, ks])], axis=0)
        rhs = jnp.concatenate([jnp.concatenate([v, zeros], axis=1), jnp.concatenate([zeros, ones], axis=1)], axis=0)
        both = _dot_tn(lhs_t, rhs, precision=HIGHEST)
        s_new = s * both[:, GLA_DV:] + both[:, :GLA_DV]
        snew_ref[:, h] = s_new.reshape(bb, GLA_DK, GLA_DV)
        o_ref[:, vs] = _dot(seq_diag(q_ref[:, ks]), s_new)


def _gla_step(s0, q, k, dec, v, bb):
    n = s0.shape[0]
    s_spec = pl.BlockSpec((bb, GLA_HEADS, GLA_DK, GLA_DV), lambda i: (i, 0, 0, 0))
    qk_spec = pl.BlockSpec((bb, GLA_QK), lambda i: (i, 0))
    v_spec = pl.BlockSpec((bb, GLA_WIDTH), lambda i: (i, 0))
    return pl.pallas_call(
        _gla_step_kernel,
        grid=(n // bb,),
        in_specs=[s_spec, qk_spec, qk_spec, qk_spec, v_spec],
        out_specs=[s_spec, v_spec],
        out_shape=[jax.ShapeDtypeStruct(s0.shape, F32), jax.ShapeDtypeStruct((n, GLA_WIDTH), F32)],
        compiler_params=_params(("parallel",)),
    )(s0, q, k, dec, v)


def _step_post_kernel(orw_ref, bon_ref, grw_ref, lnw_ref, lnb_ref, ogl_ref, ggl_ref, nw_ref, o_rw_ref, o_gl_ref):
    o = orw_ref[...]
    for h in range(RW_HEADS):
        hs = slice(h * RW_N, (h + 1) * RW_N)
        o_h = o[:, hs]
        mean = jnp.mean(o_h, axis=-1, keepdims=True)
        cen = o_h - mean
        var = jnp.mean(cen * cen, axis=-1, keepdims=True)
        on = cen * lax.rsqrt(var + RW_GN_EPS)
        res = (on * lnw_ref[:, hs] + lnb_ref[:, hs] + bon_ref[:, hs]) * grw_ref[:, hs]
        o_rw_ref[:, hs] = res.astype(o_rw_ref.dtype)
    o_gl_ref[...] = _gla_finish(ogl_ref[...], ggl_ref[...], nw_ref[...]).astype(o_gl_ref.dtype)


def _step_post(o_rw, bonus, g_rw, rw, o_gl, g_gl, gl):
    n = o_rw.shape[0]
    return pl.pallas_call(
        _step_post_kernel,
        out_shape=[jax.ShapeDtypeStruct((n, RW_WIDTH), BF16), jax.ShapeDtypeStruct((n, GLA_WIDTH), BF16)],
        compiler_params=pltpu.CompilerParams(vmem_limit_bytes=VMEM_LIMIT),
    )(o_rw, bonus, g_rw, rw["ln_w"], rw["ln_b"], o_gl, g_gl, gl["norm_w"])


def _outproj_router_kernel(x_ref, orw_ref, ogl_ref, wo_ref, gain_ref, wrt_ref, br_ref, before_ref,
                           h_ref, xn_ref, idx_ref, gate_ref, rank_ref, cnt_ref):
    mix = jnp.concatenate([orw_ref[...], ogl_ref[...]], axis=-1)
    h = x_ref[...] + _dot(mix, wo_ref[...])
    h_ref[...] = h
    xn = h * lax.rsqrt(jnp.mean(h * h, axis=-1, keepdims=True) + NORM_EPS) * gain_ref[...]
    _store_row_tiles(xn_ref, xn)
    logits = _dot_nt(wrt_ref[...], xn) + br_ref[...]
    eidx = lax.broadcasted_iota(jnp.int32, logits.shape, 0)
    vals, idxs = [], []
    work = logits
    chosen = jnp.zeros(logits.shape, F32)
    for _ in range(TOP_K):
        m = jnp.max(work, axis=0, keepdims=True)
        sel = jnp.min(jnp.where(work == m, eidx, N_EXPERTS), axis=0, keepdims=True)
        hit = eidx == sel
        work = jnp.where(hit, -jnp.inf, work)
        chosen = chosen + hit.astype(F32)
        vals.append(m)
        idxs.append(sel)
    prefix = _dot(chosen.astype(BF16), before_ref[...])
    exps = [jnp.exp(v - vals[0]) for v in vals]
    denom = exps[0] + exps[1] + exps[2] + exps[3]
    for j in range(TOP_K):
        idx_ref[0, j : j + 1, :] = idxs[j]
        gate_ref[0, j : j + 1, :] = exps[j] / denom
        rank = jnp.sum(jnp.where(eidx == idxs[j], prefix, 0.0), axis=0, keepdims=True)
        rank_ref[0, j : j + 1, :] = rank.astype(jnp.int32)
    cnt = jnp.sum(chosen, axis=1, keepdims=True)
    cnt_ref[0] = jnp.broadcast_to(cnt, (N_EXPERTS, 128)).astype(jnp.int32)


def _outproj_router(x, o_rw, o_gl, w_out, gain, w_router_t, b_router, tm):
    n = x.shape[0]
    nt = n // tm
    const = lambda shape: pl.BlockSpec(shape, lambda i: (0,) * len(shape))
    tok = lambda width: pl.BlockSpec((tm, width), lambda i: (i, 0))
    lane = pl.BlockSpec((1, TOP_K, tm), lambda i: (i, 0, 0))
    t = jnp.arange(tm, dtype=jnp.int32)
    before = (t[:, None] < t[None, :]).astype(BF16)
    return pl.pallas_call(
        _outproj_router_kernel,
        grid=(nt,),
        in_specs=[
            tok(D_MODEL), tok(RW_WIDTH), tok(GLA_WIDTH),
            const((D_MODEL, D_MODEL)), const((1, D_MODEL)), const((N_EXPERTS, D_MODEL)), const((N_EXPERTS, 1)),
            const((tm, tm)),
        ],
        out_specs=[tok(D_MODEL), pl.BlockSpec((tm * ROW_TILE, 128), lambda i: (i, 0)), lane, lane, lane,
                   pl.BlockSpec((1, N_EXPERTS, 128), lambda i: (i, 0, 0))],
        out_shape=[
            jax.ShapeDtypeStruct((n, D_MODEL), F32),
            jax.ShapeDtypeStruct((n * ROW_TILE, 128), F32),
            jax.ShapeDtypeStruct((nt, TOP_K, tm), jnp.int32),
            jax.ShapeDtypeStruct((nt, TOP_K, tm), F32),
            jax.ShapeDtypeStruct((nt, TOP_K, tm), jnp.int32),
            jax.ShapeDtypeStruct((nt, N_EXPERTS, 128), jnp.int32),
        ],
        compiler_params=_params(("parallel",)),
    )(x, o_rw, o_gl, w_out, gain, w_router_t, b_router, before)


def _moe_kernel(be_ref, nu_ref, epoch_ref, next_ref, rows_ref, parts_ref, xs_ref, wg_hbm, wu_hbm, wd_hbm, bg_ref, bu_ref, bd_ref,
                y_ref, w_f32, wg_b, wu_b, wd_b, sems):
    b = pl.program_id(0)
    prev = be_ref[jnp.maximum(b - 1, 0)]
    new_expert = (b == 0) | (be_ref[b] != prev)

    def fetch(e, slot, i):
        w = (wg_hbm, wu_hbm, wd_hbm)[i]
        return pltpu.make_async_copy(w.at[e], w_f32.at[slot, i], sems.at[slot])

    @pl.when(b == 0)
    def _():
        for i in range(3):
            fetch(be_ref[0], 0, i).start()

    @pl.when(new_expert)
    def _():
        slot = epoch_ref[b] % 2
        for i in range(3):
            fetch(be_ref[b], slot, i).wait()
        wg_b[...] = w_f32[slot, 0].astype(BF16)
        wu_b[...] = w_f32[slot, 1].astype(BF16)
        wd_b[...] = w_f32[slot, 2].astype(BF16)

    first_part, end_part = parts_ref[b] // 4, parts_ref[b] % 4
    for i in range(3):
        @pl.when((next_ref[b] >= 0) & (first_part <= i) & (i < end_part))
        def _():
            fetch(next_ref[b], 1 - epoch_ref[b] % 2, i).start()

    def ffn(m):
        x = _load_row_tiles(xs_ref, m).astype(BF16)
        half = D_MODEL // 2
        acc = None
        for f in range(2):
            fs = slice(f * half, (f + 1) * half)
            gt = _dot(x, wg_b[:, fs]) + bg_ref[0, :, fs]
            up = _dot(x, wu_b[:, fs]) + bu_ref[0, :, fs]
            gt = jnp.minimum(gt, SWIGLU_LIMIT)
            up = jnp.clip(up, -SWIGLU_LIMIT, SWIGLU_LIMIT)
            hid = (up + 1.0) * gt * _sigmoid(SWIGLU_ALPHA * gt)
            part = _dot(hid.astype(BF16), wd_b[fs, :])
            acc = part if acc is None else acc + part
        _store_row_tiles(y_ref, acc + bd_ref[0])

    pieces = (rows_ref[b] + MOE_PIECE - 1) // MOE_PIECE
    for q in range(1, MOE_BLOCK // MOE_PIECE + 1):
        @pl.when((b < nu_ref[0]) & (pieces == q))
        def _():
            ffn(q * MOE_PIECE)


def _moe_ffn(block_expert, n_used, block_rows, xs, w_gate, w_up, w_down, b_gate, b_up, b_down):
    n_blocks = block_expert.shape[0]
    pos = jnp.arange(n_blocks, dtype=jnp.int32)
    change = (pos > 0) & (block_expert != jnp.roll(block_expert, 1))
    epoch = jnp.cumsum(change.astype(jnp.int32))
    later = change[None, :] & (pos[None, :] > pos[:, None])
    first = jnp.min(jnp.where(later, pos[None, :], n_blocks), axis=1)
    next_e = jnp.sum(jnp.where(pos[None, :] == first[:, None], block_expert[None, :], 0), axis=1)
    next_e = jnp.where(first < n_blocks, next_e, -1).astype(jnp.int32)
    run_start = jnp.max(jnp.where((pos[None, :] <= pos[:, None]) & (change | (pos == 0))[None, :], pos[None, :], 0),
                        axis=1)
    q = jnp.minimum(pos - run_start, 3)
    used = pos < n_used[0]
    is_last = jnp.roll(change, -1) | (pos == n_used[0] - 1)
    parts = jnp.where(used, q * 4 + jnp.where(is_last, 3, jnp.minimum(q + 1, 3)), 15).astype(jnp.int32)

    row = lambda b, be, nu, ep, nx, br, pt: (jnp.minimum(b, nu[0] - 1), 0)
    bspec = pl.BlockSpec((1, 1, D_MODEL), lambda b, be, nu, ep, nx, br, pt: (be[b], 0, 0))
    wspec = pl.BlockSpec(memory_space=pl.ANY)
    grid_spec = pltpu.PrefetchScalarGridSpec(
        num_scalar_prefetch=6,
        grid=(n_blocks,),
        in_specs=[pl.BlockSpec((MOE_BLOCK * ROW_TILE, 128), row), wspec, wspec, wspec, bspec, bspec, bspec],
        out_specs=pl.BlockSpec((MOE_BLOCK * ROW_TILE, 128), row),
        scratch_shapes=[pltpu.VMEM((2, 3, D_MODEL, D_MODEL), F32)] + [pltpu.VMEM((D_MODEL, D_MODEL), BF16)] * 3
        + [pltpu.SemaphoreType.DMA((2,))],
    )
    return pl.pallas_call(
        _moe_kernel,
        grid_spec=grid_spec,
        out_shape=jax.ShapeDtypeStruct((n_blocks * MOE_BLOCK * ROW_TILE, 128), F32),
        compiler_params=_params(("arbitrary",)),
    )(block_expert, n_used, epoch, next_e, block_rows, parts, xs, w_gate, w_up, w_down,
      b_gate.reshape(N_EXPERTS, 1, D_MODEL), b_up.reshape(N_EXPERTS, 1, D_MODEL),
      b_down.reshape(N_EXPERTS, 1, D_MODEL))


SEG_ALIGN = 8
GROUP_ROWS = SEG_ALIGN * ROW_TILE


def _local_rows(tm):
    return tm * TOP_K + N_EXPERTS * SEG_ALIGN


BIG_COPY = 4


def _copy_tables():
    n_big = _local_rows(TOK_BLOCK) // SEG_ALIGN // BIG_COPY
    n_small = N_EXPERTS * (BIG_COPY - 1)
    return 2 * n_big, 2 * (n_big + n_small)


def _issue_group_copies(cp_ref, hbm, buf, sem, to_hbm):
    small_at, counts_at = _copy_tables()

    def copy(at, n_groups):
        rows = n_groups * GROUP_ROWS
        b = buf.at[pl.ds(pl.multiple_of(cp_ref[0, 0, at] * GROUP_ROWS, GROUP_ROWS), rows)]
        h = hbm.at[pl.ds(pl.multiple_of(cp_ref[0, 0, at + 1] * GROUP_ROWS, GROUP_ROWS), rows)]
        return pltpu.make_async_copy(b, h, sem) if to_hbm else pltpu.make_async_copy(h, b, sem)

    def big(i, carry):
        copy(2 * i, BIG_COPY).start()
        return carry

    def small(i, carry):
        copy(small_at + 2 * i, 1).start()
        return carry

    lax.fori_loop(0, cp_ref[0, 0, counts_at], big, 0)
    lax.fori_loop(0, cp_ref[0, 0, counts_at + 1], small, 0)


def _wait_group_copies(cp_ref, hbm, buf, sem, to_hbm):
    rows = pl.ds(0, pl.multiple_of(cp_ref[0, 0, _copy_tables()[1] + 2] * GROUP_ROWS, GROUP_ROWS))
    b, h = buf.at[rows], hbm.at[rows]
    (pltpu.make_async_copy(b, h, sem) if to_hbm else pltpu.make_async_copy(h, b, sem)).wait()


def _combine_kernel(grp_c, grp_n, lpos_ref, gate_ref, h_ref, gain_ref, y_hbm, o_ref, ybuf, fbuf, sems):
    i = pl.program_id(0)
    nt = pl.num_programs(0)
    tm = h_ref.shape[0]

    @pl.when(i == 0)
    def _():
        _issue_group_copies(grp_c, y_hbm, ybuf.at[0], sems.at[0], False)

    @pl.when(i + 1 < nt)
    def _():
        _issue_group_copies(grp_n, y_hbm, ybuf.at[(i + 1) % 2], sems.at[(i + 1) % 2], False)

    slot = i % 2
    yb = ybuf.at[slot]
    _wait_group_copies(grp_c, y_hbm, yb, sems.at[slot], False)

    def token_body(t, carry):
        acc = None
        for j in range(TOP_K):
            row = pl.multiple_of(lpos_ref[0, 0, t * TOP_K + j], ROW_TILE)
            term = gate_ref[0, 0, t * TOP_K + j] * yb[pl.ds(row, ROW_TILE), :]
            acc = term if acc is None else acc + term
        fbuf[pl.ds(pl.multiple_of(t * ROW_TILE, ROW_TILE), ROW_TILE), :] = acc
        return carry

    lax.fori_loop(0, tm, token_body, 0, unroll=8)
    f = h_ref[...] + _load_row_tiles(fbuf, tm)
    o_ref[...] = f * lax.rsqrt(jnp.mean(f * f, axis=-1, keepdims=True) + NORM_EPS) * gain_ref[...]


def _combine(h, y_rows, grp3, lpos3, gate3, gain, tm):
    n = h.shape[0]
    nt = n // tm
    n_local = _local_rows(tm)
    gw = grp3.shape[-1]
    smem = lambda shape, imap: pl.BlockSpec(shape, imap, memory_space=pltpu.SMEM)
    cur = lambda i: (i, 0, 0)
    nxt = lambda i: (jnp.minimum(i + 1, nt - 1), 0, 0)
    return pl.pallas_call(
        _combine_kernel,
        grid=(nt,),
        in_specs=[
            smem((1, 1, gw), cur), smem((1, 1, gw), nxt),
            smem((1, 1, TOP_K * tm), cur), smem((1, 1, TOP_K * tm), cur),
            pl.BlockSpec((tm, D_MODEL), lambda i: (i, 0)),
            pl.BlockSpec((1, D_MODEL), lambda i: (0, 0)),
            pl.BlockSpec(memory_space=pl.ANY),
        ],
        out_specs=pl.BlockSpec((tm, D_MODEL), lambda i: (i, 0)),
        out_shape=jax.ShapeDtypeStruct((n, D_MODEL), F32),
        scratch_shapes=[pltpu.VMEM((2, n_local * ROW_TILE, 128), F32), pltpu.VMEM((tm * ROW_TILE, 128), F32),
                        pltpu.SemaphoreType.DMA((2,))],
        compiler_params=pltpu.CompilerParams(dimension_semantics=("arbitrary",), vmem_limit_bytes=VMEM_LIMIT,
                                             disable_bounds_checks=True),
    )(grp3, grp3, lpos3, gate3, h, gain, y_rows)


def _dispatch_kernel(*refs, fill):
    if fill:
        grp_ref, grp_prev, lpos_ref, ends_ref, x_ref, xs_hbm, sorted_buf, zero_scr, sems, zsem = refs
    else:
        grp_ref, grp_prev, lpos_ref, x_ref, _, xs_hbm, sorted_buf, sems = refs
    i = pl.program_id(0)
    tm = x_ref.shape[0] // ROW_TILE
    blk = MOE_PIECE * ROW_TILE
    sorted_scr = sorted_buf.at[i % 2]
    sem = sems.at[i % 2]

    if fill:
        def fill_copy(e):
            start = pl.multiple_of(ends_ref[0, e] * ROW_TILE, blk)
            return pltpu.make_async_copy(zero_scr, xs_hbm.at[pl.ds(start, blk)], zsem)

        @pl.when(i == 0)
        def _():
            zero_scr[...] = jnp.zeros(zero_scr.shape, zero_scr.dtype)
            for e in range(N_EXPERTS):
                @pl.when(ends_ref[1, e] > 0)
                def _():
                    fill_copy(e).start()

    sorted_scr[...] = jnp.zeros(sorted_scr.shape, sorted_scr.dtype)

    def move(t, carry):
        row = x_ref[pl.ds(pl.multiple_of(t * ROW_TILE, ROW_TILE), ROW_TILE), :]
        for j in range(TOP_K):
            dst = pl.multiple_of(lpos_ref[0, 0, t * TOP_K + j], ROW_TILE)
            sorted_scr[pl.ds(dst, ROW_TILE), :] = row
        return carry

    lax.fori_loop(0, tm, move, 0, unroll=8)

    if fill:
        @pl.when(i == 0)
        def _():
            for e in range(N_EXPERTS):
                @pl.when(ends_ref[1, e] > 0)
                def _():
                    fill_copy(e).wait()

    @pl.when(i > 0)
    def _():
        _wait_group_copies(grp_prev, xs_hbm, sorted_buf.at[(i + 1) % 2], sems.at[(i + 1) % 2], True)

    _issue_group_copies(grp_ref, xs_hbm, sorted_scr, sem, True)

    @pl.when(i == pl.num_programs(0) - 1)
    def _():
        _wait_group_copies(grp_ref, xs_hbm, sorted_scr, sem, True)


def _dispatch(xn, grp3, lpos3, tm, n_slots, ends=None, xs=None):
    n = xn.shape[0] // ROW_TILE
    fill = xs is None
    smem = lambda shape, imap: pl.BlockSpec(shape, imap, memory_space=pltpu.SMEM)
    in_specs = [smem((1, 1, grp3.shape[-1]), lambda i: (i, 0, 0)),
                smem((1, 1, grp3.shape[-1]), lambda i: (jnp.maximum(i - 1, 0), 0, 0)),
                smem((1, 1, TOP_K * tm), lambda i: (i, 0, 0))]
    args = [grp3, grp3, lpos3]
    scratch = [pltpu.VMEM((2, _local_rows(tm) * ROW_TILE, 128), F32)]
    if fill:
        in_specs.append(smem((2, N_EXPERTS), lambda i: (0, 0)))
        args.append(ends)
        scratch.append(pltpu.VMEM((MOE_PIECE * ROW_TILE, 128), F32))
    in_specs.append(pl.BlockSpec((tm * ROW_TILE, 128), lambda i: (i, 0)))
    args.append(xn)
    aliases = {}
    if not fill:
        in_specs.append(pl.BlockSpec(memory_space=pl.ANY))
        args.append(xs)
        aliases = {len(args) - 1: 0}
    scratch.append(pltpu.SemaphoreType.DMA((2,)))
    if fill:
        scratch.append(pltpu.SemaphoreType.DMA(()))
    return pl.pallas_call(
        functools.partial(_dispatch_kernel, fill=fill),
        grid=(n // tm,),
        in_specs=in_specs,
        out_specs=pl.BlockSpec(memory_space=pl.ANY),
        out_shape=jax.ShapeDtypeStruct((n_slots * ROW_TILE, 128), F32),
        scratch_shapes=scratch,
        input_output_aliases=aliases,
        compiler_params=pltpu.CompilerParams(dimension_semantics=("arbitrary",), vmem_limit_bytes=VMEM_LIMIT,
                                             disable_bounds_checks=True, has_side_effects=True),
    )(*args)


def _pad_rows(w, rows, offset):
    out = jnp.zeros((rows, w.shape[1]), w.dtype)
    return out.at[offset : offset + w.shape[0]].set(w)


def _routing_tables(counts, n_pairs):
    n_tiles = counts.shape[0]
    n_blocks = (n_pairs + n_tiles * N_EXPERTS * (SEG_ALIGN - 1) + N_EXPERTS * (MOE_BLOCK - 1)
                + MOE_BLOCK - 1) // MOE_BLOCK
    runs = (counts + SEG_ALIGN - 1) // SEG_ALIGN * SEG_ALIGN
    local_start = jnp.cumsum(runs, axis=1) - runs
    total = jnp.sum(runs, axis=0)
    padded = (total + MOE_BLOCK - 1) // MOE_BLOCK * MOE_BLOCK
    pends = jnp.cumsum(padded)
    pstarts = pends - padded
    global_start = pstarts[None, :] + jnp.cumsum(runs, axis=0) - runs
    blocks = jnp.arange(n_blocks, dtype=jnp.int32) * MOE_BLOCK
    n_used = (pends[-1] // MOE_BLOCK).astype(jnp.int32)
    owner = jnp.sum((pends[None, :] <= blocks[:, None]).astype(jnp.int32), axis=1)
    block_expert = jnp.minimum(owner, N_EXPERTS - 1)
    last = jnp.sum(jnp.where(jnp.arange(n_blocks) == n_used - 1, block_expert, 0))
    block_expert = jnp.where(jnp.arange(n_blocks) < n_used, block_expert, last)
    real_end = pstarts + total
    ends = jnp.stack([real_end // MOE_PIECE * MOE_PIECE, real_end % MOE_PIECE]).astype(jnp.int32)
    row_end = jnp.sum(jnp.where(block_expert[:, None] == jnp.arange(N_EXPERTS), (pstarts + total)[None, :], 0), axis=1)
    block_rows = jnp.clip(row_end - blocks, 0, MOE_BLOCK).astype(jnp.int32)
    return (runs, local_start, global_start, ends, block_expert.astype(jnp.int32), n_used.reshape(1), block_rows,
            n_blocks)


def _local_positions(idx3, rank3, local_start):
    hit = idx3[..., None] == jnp.arange(N_EXPERTS, dtype=jnp.int32)
    lpos = rank3 + jnp.sum(jnp.where(hit, local_start[:, None, None, :], 0), axis=-1)
    return _token_major(lpos.astype(jnp.int32) * ROW_TILE)


def _copy_lists(runs, local_start, global_start):
    experts = jnp.arange(N_EXPERTS, dtype=jnp.int32)
    groups = runs // SEG_ALIGN
    l8, g8 = local_start // SEG_ALIGN, global_start // SEG_ALIGN
    n_big, n_small = groups // BIG_COPY, groups % BIG_COPY

    def copy_list(count, first_local, first_global, step, length):
        ends = jnp.cumsum(count, axis=1)
        o = jnp.arange(length, dtype=jnp.int32)
        owner = jnp.minimum(jnp.sum((ends[:, None, :] <= o[None, :, None]).astype(jnp.int32), axis=-1), N_EXPERTS - 1)
        sel = owner[..., None] == experts
        pick = lambda t: jnp.sum(jnp.where(sel, t[:, None, :], 0), axis=-1)
        k = (o[None, :] - pick(ends - count)) * step
        pairs = jnp.stack([pick(first_local) + k, pick(first_global) + k], axis=-1)
        live = (o[None, :] < ends[:, -1:])[..., None]
        return jnp.where(live, pairs, 0).reshape(count.shape[0], 2 * length)

    small_at, counts_at = _copy_tables()
    big = copy_list(n_big, l8, g8, BIG_COPY, small_at // 2)
    small = copy_list(n_small, l8 + n_big * BIG_COPY, g8 + n_big * BIG_COPY, 1, (counts_at - small_at) // 2)
    counts = jnp.stack([jnp.sum(n_big, axis=1), jnp.sum(n_small, axis=1), jnp.sum(groups, axis=1)], axis=1)
    counts = jnp.pad(counts, ((0, 0), (0, 5)))
    return jnp.concatenate([big, small, counts], axis=1).astype(jnp.int32)[:, None, :]


def _token_major(t3):
    return jnp.swapaxes(t3, 1, 2).reshape(t3.shape[0], 1, -1)


def kernel(x_prompt, x_sample, state_rwkv_shift, state_rwkv_wkv, state_gla, norm_mix, w_in, rw_mu, rw_w0, rw_w2, rw_a0, rw_a2, rw_g2, rw_k_k, rw_k_a, rw_r_k, rw_ln_w, rw_ln_b, gla_gk_w2, gla_gk_b, gla_norm_w, w_out, norm_ffn, w_router, b_router, w_gate, b_gate, w_up, b_up, w_down, b_down, norm_final):
    depth = norm_mix.shape[0]
    assert depth == 1
    bp, lp, d = x_prompt.shape
    bs, ls, _ = x_sample.shape
    assert ls == 1 and lp % SEQ_BLOCK == 0 and lp % RW_SEQ_BLOCK == 0
    l = 0
    row = lambda t: t.reshape(1, -1)

    w_in_b = w_in[l].astype(BF16)
    w_in_r = w_in_b[:, :RW_PROJ]
    w_in_g = jnp.pad(w_in_b[:, RW_PROJ:], ((0, 0), (0, GLA_PROJ_PAD - GLA_PROJ)))
    rw = dict(
        mu=row(rw_mu[l]), w0=row(rw_w0[l]), a0=row(rw_a0[l]),
        w2p=_pad_rows(rw_w2[l].astype(BF16), 128, 0), a2p=_pad_rows(rw_a2[l].astype(BF16), 128, 64),
        g2=rw_g2[l].astype(BF16), k_k=row(rw_k_k[l]), k_a=row(rw_k_a[l]), r_k=row(rw_r_k[l]),
        ln_w=row(rw_ln_w[l]), ln_b=row(rw_ln_b[l]))
    gl = dict(gkw=_pad_rows(gla_gk_w2[l].astype(BF16), GLA_LORA_PAD, 0), gkb=row(gla_gk_b[l]),
              norm_w=row(gla_norm_w[l]))
    gain_mix = row(norm_mix[l])

    n_p = bp * lp
    xp = x_prompt.reshape(n_p, d)
    zr_p, zg_p = _inproj(xp, gain_mix, w_in_r, w_in_g, TOK_BLOCK)
    zr_p3 = zr_p.reshape(bp, lp, RW_PROJ)
    o_rw_p, wkv_p = _rwkv_seq(zr_p3, jnp.zeros((bp, 1, RW_PROJ), F32),
                              jnp.zeros((bp, RW_HEADS, RW_N, RW_N), F32), rw, RW_SEQ_BLOCK)
    o_gl_p, gla_p = _gla_seq(zg_p.reshape(bp, lp, GLA_PROJ_PAD),
                             jnp.zeros((bp, GLA_HEADS, GLA_DK, GLA_DV), F32), gl, SEQ_BLOCK)
    shift_p = zr_p3[:, -1, :]

    xs_ = x_sample.reshape(bs, d)
    zr_s, zg_s = _inproj(xs_, gain_mix, w_in_r, w_in_g, bs)
    r, k, v, al, be, dec, g_rw, bonus = _rwkv_step_prep(zr_s, state_rwkv_shift[l], rw)
    wkv_s, o_rw_s = _rwkv_step(state_rwkv_wkv[l], r, k, al, be, dec, v)
    q, kg, vg, g_gl, dec_g = _gla_step_prep(zg_s, gl)
    gla_s, o_gl_s = _gla_step(state_gla[l], q, kg, dec_g, vg, 16)
    o_rw_s2, o_gl_s2 = _step_post(o_rw_s, bonus, g_rw, rw, o_gl_s, g_gl, gl)
    shift_s = zr_s

    w_out_b = w_out[l].astype(BF16)
    router = (w_out_b, row(norm_ffn[l]), w_router[l].T, b_router[l].reshape(N_EXPERTS, 1))
    h_p, xn_p, idx_p, gate_p, rank_p, cnt_p = _outproj_router(
        xp, o_rw_p.reshape(n_p, RW_WIDTH), o_gl_p.reshape(n_p, GLA_WIDTH), *router, TOK_BLOCK)
    h_s, xn_s, idx_s, gate_s, rank_s, cnt_s = _outproj_router(xs_, o_rw_s2, o_gl_s2, *router, bs)
    nt_p = n_p // TOK_BLOCK
    counts = jnp.concatenate([cnt_p[:, :, 0], cnt_s[:, :, 0]], axis=0)
    runs, lstart, gstart, ends, block_expert, n_used, block_rows, n_blocks = _routing_tables(
        counts, (n_p + bs) * TOP_K)
    n_slots = n_blocks * MOE_BLOCK
    lpos_p = _local_positions(idx_p, rank_p, lstart[:nt_p])
    lpos_s = _local_positions(idx_s, rank_s, lstart[nt_p:])
    grp = _copy_lists(runs, lstart, gstart)
    grp_p, grp_s = grp[:nt_p], grp[nt_p:]
    xs_rows = _dispatch(xn_p, grp_p, lpos_p, TOK_BLOCK, n_slots, ends=ends)
    xs_rows = _dispatch(xn_s, grp_s, lpos_s, bs, n_slots, xs=xs_rows)
    y_rows = _moe_ffn(block_expert, n_used, block_rows, xs_rows, w_gate[l], w_up[l], w_down[l], b_gate[l], b_up[l], b_down[l])
    gain_f = row(norm_final)
    y_p = _combine(h_p, y_rows, grp_p, lpos_p, _token_major(gate_p), gain_f, TOK_BLOCK)
    y_s = _combine(h_s, y_rows, grp_s, lpos_s, _token_major(gate_s), gain_f, bs)

    y_prompt = y_p.reshape(bp, lp, d)
    y_sample = y_s.reshape(bs, ls, d)
    return (y_prompt, y_sample, shift_p[None], wkv_p[None], gla_p[None], shift_s[None], wkv_s[None], gla_s[None])
```

```python
import functools

import jax
import jax.numpy as jnp
from jax import lax
from jax.experimental import pallas as pl
from jax.experimental.pallas import tpu as pltpu

F32 = jnp.float32
BF16 = jnp.bfloat16
HIGHEST = lax.Precision.HIGHEST

D_MODEL = 1024
RW_WIDTH = 512
RW_HEADS = 8
RW_N = 64
RW_PROJ = 1792
RW_GN_EPS = 64e-5
GLA_HEADS = 4
GLA_DK = 64
GLA_DV = 128
GLA_WIDTH = 512
GLA_QK = GLA_HEADS * GLA_DK
GLA_PROJ = 1552
GLA_PROJ_PAD = 1664
GLA_LORA_PAD = 128
GLA_GATE_NORMALIZER = 16.0
N_EXPERTS = 32
TOP_K = 4
SWIGLU_LIMIT = 7.0
SWIGLU_ALPHA = 1.702
NORM_EPS = 1e-5
LOG2_E = 1.4426950408889634

RW_CHUNK = 64
GLA_CHUNK = 16
SEQ_BLOCK = 512
TOK_BLOCK = 512
MOE_BLOCK = 512
MOE_PIECE = 128
VMEM_LIMIT = 56 * 1024 * 1024


def _dot(a, b, precision=None):
    return jnp.dot(a, b, preferred_element_type=F32, precision=precision)


def _dot_nt(a, b, precision=None):
    return lax.dot_general(a, b, (((1,), (1,)), ((), ())), preferred_element_type=F32, precision=precision)


def _dot_tn(a, b, precision=None):
    return lax.dot_general(a, b, (((0,), (0,)), ((), ())), preferred_element_type=F32, precision=precision)


def _sigmoid(x):
    return 1.0 / (1.0 + jnp.exp(-x))


def _softplus(x):
    return jnp.maximum(x, 0.0) + jnp.log(1.0 + jnp.exp(-jnp.abs(x)))


def _params(sem):
    return pltpu.CompilerParams(dimension_semantics=sem, vmem_limit_bytes=VMEM_LIMIT)


ROW_TILE = D_MODEL // 128


def _store_row_tiles(ref, x):
    m = x.shape[0]
    for c in range(ROW_TILE):
        ref[pl.ds(c, m, stride=ROW_TILE), :] = x[:, c * 128 : (c + 1) * 128]


def _load_row_tiles(ref, m):
    return jnp.concatenate([ref[pl.ds(c, m, stride=ROW_TILE), :] for c in range(ROW_TILE)], axis=-1)


def _inproj_kernel(x_ref, gain_ref, wr_ref, wg_ref, zr_ref, zg_ref):
    x = x_ref[...]
    xn = x * lax.rsqrt(jnp.mean(x * x, axis=-1, keepdims=True) + NORM_EPS) * gain_ref[...]
    xb = xn.astype(BF16)
    zr_ref[...] = _dot(xb, wr_ref[...])
    zg_ref[...] = _dot(xb, wg_ref[...])


def _inproj(x, gain, w_r, w_g, tm):
    n = x.shape[0]
    return pl.pallas_call(
        _inproj_kernel,
        grid=(n // tm,),
        in_specs=[
            pl.BlockSpec((tm, D_MODEL), lambda i: (i, 0)),
            pl.BlockSpec((1, D_MODEL), lambda i: (0, 0)),
            pl.BlockSpec((D_MODEL, RW_PROJ), lambda i: (0, 0)),
            pl.BlockSpec((D_MODEL, GLA_PROJ_PAD), lambda i: (0, 0)),
        ],
        out_specs=[
            pl.BlockSpec((tm, RW_PROJ), lambda i: (i, 0)),
            pl.BlockSpec((tm, GLA_PROJ_PAD), lambda i: (i, 0)),
        ],
        out_shape=[
            jax.ShapeDtypeStruct((n, RW_PROJ), F32),
            jax.ShapeDtypeStruct((n, GLA_PROJ_PAD), F32),
        ],
        compiler_params=_params(("parallel",)),
    )(x, gain, w_r, w_g)


def _rwkv_features(zs, w0, w2p, a0, a2p, g2, k_k, k_a):
    W = RW_WIDTH
    r = zs[:, 0:W]
    k_raw = zs[:, W : 2 * W]
    v = zs[:, 2 * W : 3 * W]
    zwa = zs[:, 3 * W : 3 * W + 128]
    zg = zs[:, 3 * W + 128 :]
    w = -_softplus(-(w0 + _dot(jnp.tanh(zwa).astype(BF16), w2p))) - 0.5
    log_decay = -jnp.exp(w)
    a = _sigmoid(a0 + _dot(zwa.astype(BF16), a2p))
    g = _dot(_sigmoid(zg).astype(BF16), g2)
    kk_raw = k_raw * k_k
    k = k_raw * (1.0 + (a - 1.0) * k_a)
    return r, k, v, kk_raw, a, log_decay, g


def _level_mask(ri, ci, lvl):
    same = (ri >> (lvl + 1)) == (ci >> (lvl + 1))
    return same & (((ri >> lvl) & 1) == 1) & (((ci >> lvl) & 1) == 0)


def _rwkv_seq_kernel(z_ref, shift0_ref, s0_ref, mu_ref, w0_ref, w2_ref, a0_ref, a2_ref, g2_ref, kk_ref, ka_ref,
                     rk_ref, lnw_ref, lnb_ref, o_ref, sout_ref,
                     m_scr, prev_scr, r_scr, k_scr, v_scr, kkr_scr, a_scr, lw_scr, on_scr, bon_scr):
    C = RW_CHUNK
    N = RW_N
    t_idx = pl.program_id(1)
    tb = z_ref.shape[1]
    zero_nn = jnp.zeros((N, N), F32)

    @pl.when(t_idx == 0)
    def _():
        prev_scr[...] = shift0_ref[0]
        for p in range(RW_HEADS // 2):
            top = jnp.concatenate([s0_ref[0, 2 * p].T, zero_nn], axis=1)
            bot = jnp.concatenate([zero_nn, s0_ref[0, 2 * p + 1].T], axis=1)
            m_scr[p] = jnp.concatenate([top, bot], axis=0)

    z = z_ref[0]
    row = lax.broadcasted_iota(jnp.int32, z.shape, 0)
    z_prev = jnp.where(row == 0, prev_scr[...], pltpu.roll(z, 1, axis=0))
    prev_scr[...] = z[tb - 1 : tb, :]
    zs = z + mu_ref[...] * (z_prev - z)
    r, k, v, kk_raw, a, log_decay, g = _rwkv_features(
        zs, w0_ref[...], w2_ref[...], a0_ref[...], a2_ref[...], g2_ref[...], kk_ref[...], ka_ref[...])
    P2 = 2 * N
    left1 = lax.broadcasted_iota(jnp.int32, (1, P2), 1) < N

    def head_sum(x):
        s0 = jnp.sum(jnp.where(left1, x, 0.0), axis=-1, keepdims=True)
        s1 = jnp.sum(jnp.where(left1, 0.0, x), axis=-1, keepdims=True)
        return jnp.where(left1, s0, s1)

    def head_sum_wide(x):
        return jnp.concatenate([head_sum(x[:, p * P2 : (p + 1) * P2]) for p in range(RW_HEADS // 2)], axis=1)

    alpha = kk_raw * lax.rsqrt(jnp.maximum(head_sum_wide(kk_raw * kk_raw), 1e-24))
    r_scr[...] = r
    k_scr[...] = k
    v_scr[...] = v
    kkr_scr[...] = alpha
    a_scr[...] = alpha * a
    lw_scr[...] = log_decay
    bon_scr[...] = head_sum_wide(r * k * rk_ref[...]) * v

    ri = lax.broadcasted_iota(jnp.int32, (C, P2), 0)
    ci = lax.broadcasted_iota(jnp.int32, (C, P2), 1) % N
    left = lax.broadcasted_iota(jnp.int32, (C, P2), 1) < N
    tril = ri >= ci
    stril = ri > ci
    eye_f = (ri == ci).astype(F32)
    rb = lax.broadcasted_iota(jnp.int32, (P2, P2), 0)
    cb = lax.broadcasted_iota(jnp.int32, (P2, P2), 1)
    same_head = (rb < N) == (cb < N)
    eye_b = rb == cb
    rc = lax.broadcasted_iota(jnp.int32, (C, C), 0)
    cc = lax.broadcasted_iota(jnp.int32, (C, C), 1)
    tril_f = (rc >= cc).astype(F32)

    def bdiag(x):
        return jnp.concatenate([jnp.where(left, x, 0.0), jnp.where(left, 0.0, x)], axis=0)

    n_sub = tb // C
    pairs = range(RW_HEADS // 2)

    def chunk_body(it, carry):
        units = [(s, p) for s in range(n_sub) for p in pairs]
        sls = [pl.ds(pl.multiple_of((it * n_sub + s) * C, C), C) for s in range(n_sub)]
        prep = []
        for s in range(n_sub):
            lw = lw_scr[sls[s], :]
            cum = _dot(tril_f, lw, precision=HIGHEST)
            cum_last = cum[C - 1 : C, :]
            prep.append(dict(
                e_incl=jnp.exp(cum), e_excl=jnp.exp(cum - lw), e_neg=jnp.exp(-cum),
                e_tail=jnp.exp(cum_last - cum), p_last=jnp.exp(cum_last),
                r=r_scr[sls[s], :], k=k_scr[sls[s], :], v=v_scr[sls[s], :], kk=kkr_scr[sls[s], :],
                a=a_scr[sls[s], :]))
        lanes = [slice(p * P2, (p + 1) * P2) for p in pairs]
        get = lambda name: [prep[s][name][:, lanes[p]] for s, p in units]
        r2, k2, v2, al, be = get("r"), get("k"), get("v"), get("kk"), get("a")
        e_incl, e_excl, e_neg, e_tail, p_last = get("e_incl"), get("e_excl"), get("e_neg"), get("e_tail"), get("p_last")
        un = range(len(units))
        al_t = [al[u] * e_excl[u] for u in un]
        r_t = [r2[u] * e_incl[u] for u in un]
        be_n = [be[u] * e_neg[u] for u in un]
        k_n = [k2[u] * e_neg[u] for u in un]
        k_et = [(k2[u] * e_tail[u]).T for u in un]
        be_et = [(be[u] * e_tail[u]).T for u in un]
        v_bd = [bdiag(v2[u]) for u in un]
        lhs = [jnp.concatenate([al_t[u], r_t[u]], axis=0) for u in un]
        s_b = [_dot_nt(lhs[u], bdiag(be_n[u])) for u in un]
        s_k = [_dot_nt(lhs[u], bdiag(k_n[u])) for u in un]
        l_ab = [jnp.where(stril, s_b[u][:C], 0.0) for u in un]
        a_rb = [jnp.where(tril, s_b[u][C:], 0.0) for u in un]
        l_ak = [jnp.where(stril, s_k[u][:C], 0.0) for u in un]
        a_rk = [jnp.where(tril, s_k[u][C:], 0.0) for u in un]
        lakv = [_dot(l_ak[u], v_bd[u]) for u in un]
        arkv = [_dot(a_rk[u], v_bd[u]) for u in un]
        kev = [_dot(k_et[u], v2[u]) for u in un]
        t_inv = [eye_f - jnp.where(_level_mask(ri, ci, 0), l_ab[u], 0.0) for u in un]
        lvl = 1
        while (1 << lvl) < C:
            lm = _level_mask(ri, ci, lvl)
            tn = [_dot(t_inv[u], bdiag(jnp.where(lm, l_ab[u], 0.0))) for u in un]
            t_inv = [t_inv[u] - _dot(tn[u], bdiag(t_inv[u])) for u in un]
            lvl += 1
        a_til = [_dot(t_inv[u], bdiag(al_t[u])) for u in un]
        b_til = [_dot(t_inv[u], bdiag(lakv[u])) for u in un]
        r_hat = [r_t[u] - _dot(a_rb[u], bdiag(a_til[u])) for u in un]
        o_hat = [arkv[u] - _dot(a_rb[u], bdiag(b_til[u])) for u in un]
        g_bd = [jnp.where(same_head, jnp.where(eye_b, p_last[u], 0.0) - _dot(be_et[u], a_til[u]), 0.0) for u in un]
        h_bd = [jnp.where(same_head, kev[u] - _dot(be_et[u], b_til[u]), 0.0) for u in un]
        lhs_m = [jnp.concatenate([r_hat[u], g_bd[u]], axis=0) for u in un]
        for u, (s, p) in enumerate(units):
            res = _dot(lhs_m[u], m_scr[p])
            m_scr[p] = res[C:] + h_bd[u]
            o_p = res[:C] + o_hat[u]
            cen = o_p - head_sum(o_p) * (1.0 / N)
            var = head_sum(cen * cen) * (1.0 / N)
            on_scr[sls[s], lanes[p]] = cen * lax.rsqrt(var + RW_GN_EPS)
        return carry

    lax.fori_loop(0, tb // (C * n_sub), chunk_body, 0)
    out = (on_scr[...] * lnw_ref[...] + lnb_ref[...] + bon_scr[...]) * g
    o_ref[0] = out.astype(o_ref.dtype)

    @pl.when(t_idx == pl.num_programs(1) - 1)
    def _():
        for p in range(RW_HEADS // 2):
            m = m_scr[p]
            sout_ref[0, 2 * p] = m[:N, :N].T
            sout_ref[0, 2 * p + 1] = m[N:, N:].T


def _rwkv_seq(z3, shift0, s0, rw, tb):
    b, l, _ = z3.shape
    const = lambda shape: pl.BlockSpec(shape, lambda i, j: (0,) * len(shape))
    wide = lambda: pltpu.VMEM((tb, RW_WIDTH), F32)
    return pl.pallas_call(
        _rwkv_seq_kernel,
        grid=(b, l // tb),
        in_specs=[
            pl.BlockSpec((1, tb, RW_PROJ), lambda i, j: (i, j, 0)),
            pl.BlockSpec((1, 1, RW_PROJ), lambda i, j: (i, 0, 0)),
            pl.BlockSpec((1, RW_HEADS, RW_N, RW_N), lambda i, j: (i, 0, 0, 0)),
            const((1, RW_PROJ)),
            const((1, RW_WIDTH)), const((128, RW_WIDTH)),
            const((1, RW_WIDTH)), const((128, RW_WIDTH)),
            const((128, RW_WIDTH)),
            const((1, RW_WIDTH)), const((1, RW_WIDTH)), const((1, RW_WIDTH)),
            const((1, RW_WIDTH)), const((1, RW_WIDTH)),
        ],
        out_specs=[
            pl.BlockSpec((1, tb, RW_WIDTH), lambda i, j: (i, j, 0)),
            pl.BlockSpec((1, RW_HEADS, RW_N, RW_N), lambda i, j: (i, 0, 0, 0)),
        ],
        out_shape=[
            jax.ShapeDtypeStruct((b, l, RW_WIDTH), BF16),
            jax.ShapeDtypeStruct((b, RW_HEADS, RW_N, RW_N), F32),
        ],
        scratch_shapes=[
            pltpu.VMEM((RW_HEADS // 2, 2 * RW_N, 2 * RW_N), F32),
            pltpu.VMEM((1, RW_PROJ), F32),
            wide(), wide(), wide(), wide(), wide(), wide(), wide(), wide(),
        ],
        compiler_params=_params(("parallel", "arbitrary")),
    )(z3, shift0, s0, rw["mu"], rw["w0"], rw["w2p"], rw["a0"], rw["a2p"], rw["g2"], rw["k_k"], rw["k_a"],
      rw["r_k"], rw["ln_w"], rw["ln_b"])


def _rwkv_step_prep_kernel(z_ref, shift0_ref, mu_ref, w0_ref, w2_ref, a0_ref, a2_ref, g2_ref, kk_ref, ka_ref,
                           rk_ref, r_ref, k_ref, v_ref, al_ref, be_ref, dec_ref, g_ref, bon_ref):
    z = z_ref[...]
    zs = z + mu_ref[...] * (shift0_ref[...] - z)
    r, k, v, kk_raw, a, log_decay, g = _rwkv_features(
        zs, w0_ref[...], w2_ref[...], a0_ref[...], a2_ref[...], g2_ref[...], kk_ref[...], ka_ref[...])
    rk_all = rk_ref[...]
    als = []
    for h in range(RW_HEADS):
        hs = slice(h * RW_N, (h + 1) * RW_N)
        kk_h = kk_raw[:, hs]
        nrm = jnp.sqrt(jnp.sum(kk_h * kk_h, axis=-1, keepdims=True))
        als.append(kk_h / jnp.maximum(nrm, 1e-12))
        bon_ref[:, hs] = jnp.sum(r[:, hs] * k[:, hs] * rk_all[:, hs], axis=-1, keepdims=True) * v[:, hs]
    al = jnp.concatenate(als, axis=1)
    r_ref[...] = r.T
    k_ref[...] = k.T
    v_ref[...] = v.T
    al_ref[...] = al.T
    be_ref[...] = (al * a).T
    dec_ref[...] = jnp.exp(log_decay).T
    g_ref[...] = g


def _rwkv_step_prep(z, shift0, rw):
    n = z.shape[0]
    out = jax.ShapeDtypeStruct((n, RW_WIDTH), F32)
    out_t = jax.ShapeDtypeStruct((RW_WIDTH, n), F32)
    return pl.pallas_call(
        _rwkv_step_prep_kernel,
        out_shape=[out_t] * 6 + [out] * 2,
        compiler_params=pltpu.CompilerParams(vmem_limit_bytes=VMEM_LIMIT),
    )(z, shift0, rw["mu"], rw["w0"], rw["w2p"], rw["a0"], rw["a2p"], rw["g2"], rw["k_k"], rw["k_a"], rw["r_k"])


def _rwkv_step_kernel(s_ref, r_ref, k_ref, al_ref, be_ref, dec_ref, v_ref, snew_ref, o_ref):
    r, k, al, be, dec = r_ref[...], k_ref[...], al_ref[...], be_ref[...], dec_ref[...]

    def body(g, carry):
        rows = pl.ds(pl.multiple_of(g * 8, 8), 8)
        v8 = v_ref[rows, :]
        outs = []
        for j in range(8):
            s = s_ref[0, g * 8 + j]
            sa = -jnp.sum(s * al, axis=0, keepdims=True)
            s_new = s * dec + sa * be + v8[j : j + 1, :] * k
            snew_ref[0, g * 8 + j] = s_new
            outs.append(jnp.sum(s_new * r, axis=0, keepdims=True))
        o_ref[rows, :] = jnp.concatenate(outs, axis=0)
        return carry

    lax.fori_loop(0, RW_N // 8, body, 0)


def _rwkv_step(s0, r, k, al, be, dec, v):
    n = s0.shape[0]
    s_t = jnp.transpose(s0, (1, 2, 3, 0))
    s_spec = pl.BlockSpec((1, RW_N, RW_N, n), lambda h: (h, 0, 0, 0))
    op_spec = pl.BlockSpec((RW_N, n), lambda h: (h, 0))
    s_new_t, o_t = pl.pallas_call(
        _rwkv_step_kernel,
        grid=(RW_HEADS,),
        in_specs=[s_spec] + [op_spec] * 6,
        out_specs=[s_spec, op_spec],
        out_shape=[
            jax.ShapeDtypeStruct(s_t.shape, F32),
            jax.ShapeDtypeStruct((RW_WIDTH, n), F32),
        ],
        compiler_params=_params(("parallel",)),
    )(s_t, r, k, al, be, dec, v)
    return jnp.transpose(s_new_t, (3, 0, 1, 2)), o_t


def _gla_features(z, gkw, gkb):
    q = z[:, 0:GLA_QK] * (GLA_DK ** -0.5)
    k = z[:, GLA_QK : 2 * GLA_QK]
    v = z[:, 2 * GLA_QK : 2 * GLA_QK + GLA_WIDTH]
    g = z[:, 2 * GLA_QK + GLA_WIDTH : 2 * GLA_QK + 2 * GLA_WIDTH]
    zgk = z[:, 2 * GLA_QK + 2 * GLA_WIDTH :]
    gk = -_softplus(-(_dot(zgk.astype(BF16), gkw) + gkb)) / GLA_GATE_NORMALIZER
    return q, k, v, g, gk


def _gla_finish(o, g, norm_w):
    outs = []
    for h in range(GLA_HEADS):
        hs = slice(h * GLA_DV, (h + 1) * GLA_DV)
        o_h = o[:, hs]
        o_h = o_h * lax.rsqrt(jnp.mean(o_h * o_h, axis=-1, keepdims=True) + NORM_EPS) * norm_w
        g_h = g[:, hs]
        outs.append(o_h * (g_h * _sigmoid(g_h)))
    return jnp.concatenate(outs, axis=-1)


def _gla_seq_kernel(z_ref, s0_ref, gkw_ref, gkb_ref, nw_ref, wsel_ref, o_ref, sout_ref,
                    st_scr, x_scr, gc_scr, oi_scr):
    C = GLA_CHUNK
    G = 128
    t_idx = pl.program_id(1)
    tb = z_ref.shape[1]
    nc = tb // C
    zero_vk = jnp.zeros((GLA_DV, GLA_DK), F32)

    @pl.when(t_idx == 0)
    def _():
        for p in range(GLA_HEADS // 2):
            top = jnp.concatenate([s0_ref[0, 2 * p].T, zero_vk], axis=1)
            bot = jnp.concatenate([zero_vk, s0_ref[0, 2 * p + 1].T], axis=1)
            st_scr[p] = jnp.concatenate([top, bot], axis=0)

    q, k, v, g, gk = _gla_features(z_ref[0], gkw_ref[...], gkb_ref[...])
    ri = lax.broadcasted_iota(jnp.int32, (G, G), 0)
    ci = lax.broadcasted_iota(jnp.int32, (G, G), 1)
    cum_mat = ((ri // C == ci // C) & (ri >= ci)).astype(F32)
    for m in range(tb // G):
        rows = slice(m * G, (m + 1) * G)
        gc_scr[rows, :] = _dot(cum_mat, gk[rows, :], precision=HIGHEST)
    gcum = gc_scr[...]

    rg = lax.broadcasted_iota(jnp.int32, (tb, 2 * G), 0)
    cg = lax.broadcasted_iota(jnp.int32, (tb, 2 * G), 1)
    blk_mask = ((cg % G) // C == (rg % G) // C) & (cg % C <= rg % C)
    for p in range(GLA_HEADS // 2):
        ls = slice(p * 128, (p + 1) * 128)
        q3 = q[:, ls].reshape(nc, C, 128)
        k3 = k[:, ls].reshape(nc, C, 128)
        g3 = gcum[:, ls].reshape(nc, C, 128) * LOG2_E
        half = C // 2
        for j in range(C):
            lo = 0 if j < half else half
            e = (q3[:, lo:] * jnp.exp2(jnp.minimum(g3[:, lo:] - g3[:, j : j + 1, :], 0.0))) * k3[:, j : j + 1, :]
            if lo:
                e = jnp.concatenate([jnp.zeros((nc, lo, 128), F32), e], axis=1)
            x_scr[:, j * 128 : (j + 1) * 128] = e.reshape(tb, 128).astype(BF16)
        a_t = jnp.where(blk_mask, _dot(x_scr[...], wsel_ref[...]), 0.0).astype(BF16)
        for hl in range(2):
            h = 2 * p + hl
            for m in range(tb // G):
                rows = slice(m * G, (m + 1) * G)
                a_blk = a_t[rows, hl * G : (hl + 1) * G]
                oi_scr[rows, h * GLA_DV : (h + 1) * GLA_DV] = _dot(
                    a_blk, v[rows, h * GLA_DV : (h + 1) * GLA_DV].astype(BF16))

    CG = G // C
    rt = lax.broadcasted_iota(jnp.int32, (G, CG * 128), 0)
    ct = lax.broadcasted_iota(jnp.int32, (G, CG * 128), 1)
    own_chunk = rt // C == ct // 128
    rs = lax.broadcasted_iota(jnp.int32, (2 * GLA_DV, CG * 128), 0)
    cs = lax.broadcasted_iota(jnp.int32, (2 * GLA_DV, CG * 128), 1)
    same_head = rs // GLA_DV == (cs % 128) // GLA_DK

    def chunk_diag(x):
        return jnp.where(own_chunk, jnp.concatenate([x] * CG, axis=1), 0.0)

    for m in range(tb // G):
        rows = slice(m * G, (m + 1) * G)
        for p in range(GLA_HEADS // 2):
            ls = slice(p * 128, (p + 1) * 128)
            vs = slice(p * 2 * GLA_DV, (p + 1) * 2 * GLA_DV)
            g_g = gcum[rows, ls]
            g3 = g_g.reshape(CG, C, 128)
            g_last = jnp.broadcast_to(g3[:, C - 1 : C, :], (CG, C, 128)).reshape(G, 128)
            q_t = q[rows, ls] * jnp.exp(g_g)
            k_t = k[rows, ls] * jnp.exp(g_last - g_g)
            d_s = jnp.where(same_head, _dot_tn(v[rows, vs], chunk_diag(k_t)), 0.0)
            st = st_scr[p]
            starts = []
            for c in range(CG):
                starts.append(st)
                decay = jnp.exp(g_g[c * C + C - 1 : c * C + C, :])
                st = st * decay + d_s[:, c * 128 : (c + 1) * 128]
            st_scr[p] = st
            oi_scr[rows, vs] += _dot_nt(chunk_diag(q_t), jnp.concatenate(starts, axis=1))

    o_ref[0] = _gla_finish(oi_scr[...], g, nw_ref[...]).astype(o_ref.dtype)

    @pl.when(t_idx == pl.num_programs(1) - 1)
    def _():
        for p in range(GLA_HEADS // 2):
            st = st_scr[p]
            sout_ref[0, 2 * p] = st[:GLA_DV, :GLA_DK].T
            sout_ref[0, 2 * p + 1] = st[GLA_DV:, GLA_DK:].T


def _gla_select_matrix():
    j = jnp.arange(GLA_CHUNK)[:, None, None]
    hl = jnp.arange(2)[None, :, None]
    rows_j = jnp.broadcast_to(j, (GLA_CHUNK, 2, GLA_DK)).reshape(-1)
    rows_h = jnp.broadcast_to(hl, (GLA_CHUNK, 2, GLA_DK)).reshape(-1)
    cols = jnp.arange(256)
    sel = (rows_j[:, None] == cols[None, :] % GLA_CHUNK) & (rows_h[:, None] == cols[None, :] // 128)
    return sel.astype(BF16)


def _gla_seq(z3, s0, gl, tb):
    b, l, _ = z3.shape
    const = lambda shape: pl.BlockSpec(shape, lambda i, j: (0,) * len(shape))
    return pl.pallas_call(
        _gla_seq_kernel,
        grid=(b, l // tb),
        in_specs=[
            pl.BlockSpec((1, tb, GLA_PROJ_PAD), lambda i, j: (i, j, 0)),
            pl.BlockSpec((1, GLA_HEADS, GLA_DK, GLA_DV), lambda i, j: (i, 0, 0, 0)),
            const((GLA_LORA_PAD, GLA_QK)), const((1, GLA_QK)), const((1, GLA_DV)),
            const((GLA_CHUNK * 128, 256)),
        ],
        out_specs=[
            pl.BlockSpec((1, tb, GLA_WIDTH), lambda i, j: (i, j, 0)),
            pl.BlockSpec((1, GLA_HEADS, GLA_DK, GLA_DV), lambda i, j: (i, 0, 0, 0)),
        ],
        out_shape=[
            jax.ShapeDtypeStruct((b, l, GLA_WIDTH), BF16),
            jax.ShapeDtypeStruct((b, GLA_HEADS, GLA_DK, GLA_DV), F32),
        ],
        scratch_shapes=[
            pltpu.VMEM((GLA_HEADS // 2, 2 * GLA_DV, 2 * GLA_DK), F32),
            pltpu.VMEM((tb, GLA_CHUNK * 128), BF16),
            pltpu.VMEM((tb, GLA_QK), F32), pltpu.VMEM((tb, GLA_WIDTH), F32),
        ],
        compiler_params=_params(("parallel", "arbitrary")),
    )(z3, s0, gl["gkw"], gl["gkb"], gl["norm_w"], _gla_select_matrix())


def _gla_step_prep_kernel(z_ref, gkw_ref, gkb_ref, q_ref, k_ref, v_ref, g_ref, dec_ref):
    q, k, v, g, gk = _gla_features(z_ref[...], gkw_ref[...], gkb_ref[...])
    q_ref[...] = q
    k_ref[...] = k
    v_ref[...] = v
    g_ref[...] = g
    dec_ref[...] = jnp.exp(gk)


def _gla_step_prep(z, gl):
    n = z.shape[0]
    qk = jax.ShapeDtypeStruct((n, GLA_QK), F32)
    wide = jax.ShapeDtypeStruct((n, GLA_WIDTH), F32)
    return pl.pallas_call(
        _gla_step_prep_kernel,
        out_shape=[qk, qk, wide, wide, qk],
        compiler_params=pltpu.CompilerParams(vmem_limit_bytes=VMEM_LIMIT),
    )(z, gl["gkw"], gl["gkb"])


def _gla_step_kernel(s_ref, q_ref, k_ref, dec_ref, v_ref, snew_ref, o_ref):
    bb = s_ref.shape[0]
    rows = lax.broadcasted_iota(jnp.int32, (bb, bb * GLA_DK), 0)
    cols = lax.broadcasted_iota(jnp.int32, (bb, bb * GLA_DK), 1)
    own = rows == cols // GLA_DK
    ones = jnp.ones((bb, GLA_DV), F32)
    zeros = jnp.zeros((bb, GLA_DV), F32)

    def seq_diag(x):
        return jnp.where(own, jnp.concatenate([x] * bb, axis=1), 0.0)

    for h in range(GLA_HEADS):
        ks = slice(h * GLA_DK, (h + 1) * GLA_DK)
        vs = slice(h * GLA_DV, (h + 1) * GLA_DV)
        s = s_ref[:, h].reshape(bb * GLA_DK, GLA_DV)
        v = v_ref[:, vs]
        lhs_t = jnp.concatenate([seq_diag(k_ref[:, ks]), seq_diag(dec_ref[:, ks])], axis=0)
        rhs = jnp.concatenate([jnp.concatenate([v, zeros], axis=1), jnp.concatenate([zeros, ones], axis=1)], axis=0)
        both = _dot_tn(lhs_t, rhs, precision=HIGHEST)
        s_new = s * both[:, GLA_DV:] + both[:, :GLA_DV]
        snew_ref[:, h] = s_new.reshape(bb, GLA_DK, GLA_DV)
        o_ref[:, vs] = _dot(seq_diag(q_ref[:, ks]), s_new)


def _gla_step(s0, q, k, dec, v, bb):
    n = s0.shape[0]
    s_spec = pl.BlockSpec((bb, GLA_HEADS, GLA_DK, GLA_DV), lambda i: (i, 0, 0, 0))
    qk_spec = pl.BlockSpec((bb, GLA_QK), lambda i: (i, 0))
    v_spec = pl.BlockSpec((bb, GLA_WIDTH), lambda i: (i, 0))
    return pl.pallas_call(
        _gla_step_kernel,
        grid=(n // bb,),
        in_specs=[s_spec, qk_spec, qk_spec, qk_spec, v_spec],
        out_specs=[s_spec, v_spec],
        out_shape=[jax.ShapeDtypeStruct(s0.shape, F32), jax.ShapeDtypeStruct((n, GLA_WIDTH), F32)],
        compiler_params=_params(("parallel",)),
    )(s0, q, k, dec, v)


def _step_post_kernel(orw_ref, bon_ref, grw_ref, lnw_ref, lnb_ref, ogl_ref, ggl_ref, nw_ref, o_rw_ref, o_gl_ref):
    o = orw_ref[...].T
    for h in range(RW_HEADS):
        hs = slice(h * RW_N, (h + 1) * RW_N)
        o_h = o[:, hs]
        mean = jnp.mean(o_h, axis=-1, keepdims=True)
        cen = o_h - mean
        var = jnp.mean(cen * cen, axis=-1, keepdims=True)
        on = cen * lax.rsqrt(var + RW_GN_EPS)
        res = (on * lnw_ref[:, hs] + lnb_ref[:, hs] + bon_ref[:, hs]) * grw_ref[:, hs]
        o_rw_ref[:, hs] = res.astype(o_rw_ref.dtype)
    o_gl_ref[...] = _gla_finish(ogl_ref[...], ggl_ref[...], nw_ref[...]).astype(o_gl_ref.dtype)


def _step_post(o_rw, bonus, g_rw, rw, o_gl, g_gl, gl):
    n = bonus.shape[0]
    return pl.pallas_call(
        _step_post_kernel,
        out_shape=[jax.ShapeDtypeStruct((n, RW_WIDTH), BF16), jax.ShapeDtypeStruct((n, GLA_WIDTH), BF16)],
        compiler_params=pltpu.CompilerParams(vmem_limit_bytes=VMEM_LIMIT),
    )(o_rw, bonus, g_rw, rw["ln_w"], rw["ln_b"], o_gl, g_gl, gl["norm_w"])


def _outproj_router_kernel(x_ref, orw_ref, ogl_ref, wo_ref, gain_ref, wrt_ref, br_ref, before_ref,
                           h_ref, xn_ref, idx_ref, gate_ref, rank_ref, cnt_ref):
    mix = jnp.concatenate([orw_ref[...], ogl_ref[...]], axis=-1)
    h = x_ref[...] + _dot(mix, wo_ref[...])
    h_ref[...] = h
    xn = h * lax.rsqrt(jnp.mean(h * h, axis=-1, keepdims=True) + NORM_EPS) * gain_ref[...]
    _store_row_tiles(xn_ref, xn)
    logits = _dot_nt(wrt_ref[...], xn) + br_ref[...]
    eidx = lax.broadcasted_iota(jnp.int32, logits.shape, 0)
    vals, idxs = [], []
    work = logits
    chosen = jnp.zeros(logits.shape, F32)
    for _ in range(TOP_K):
        m = jnp.max(work, axis=0, keepdims=True)
        sel = jnp.min(jnp.where(work == m, eidx, N_EXPERTS), axis=0, keepdims=True)
        hit = eidx == sel
        work = jnp.where(hit, -jnp.inf, work)
        chosen = chosen + hit.astype(F32)
        vals.append(m)
        idxs.append(sel)
    prefix = _dot(chosen.astype(BF16), before_ref[...])
    exps = [jnp.exp(v - vals[0]) for v in vals]
    denom = exps[0] + exps[1] + exps[2] + exps[3]
    for j in range(TOP_K):
        idx_ref[0, j : j + 1, :] = idxs[j]
        gate_ref[0, j : j + 1, :] = exps[j] / denom
        rank = jnp.sum(jnp.where(eidx == idxs[j], prefix, 0.0), axis=0, keepdims=True)
        rank_ref[0, j : j + 1, :] = rank.astype(jnp.int32)
    cnt = jnp.sum(chosen, axis=1, keepdims=True)
    cnt_ref[0] = jnp.broadcast_to(cnt, (N_EXPERTS, 128)).astype(jnp.int32)


def _outproj_router(x, o_rw, o_gl, w_out, gain, w_router_t, b_router, tm):
    n = x.shape[0]
    nt = n // tm
    const = lambda shape: pl.BlockSpec(shape, lambda i: (0,) * len(shape))
    tok = lambda width: pl.BlockSpec((tm, width), lambda i: (i, 0))
    lane = pl.BlockSpec((1, TOP_K, tm), lambda i: (i, 0, 0))
    t = jnp.arange(tm, dtype=jnp.int32)
    before = (t[:, None] < t[None, :]).astype(BF16)
    return pl.pallas_call(
        _outproj_router_kernel,
        grid=(nt,),
        in_specs=[
            tok(D_MODEL), tok(RW_WIDTH), tok(GLA_WIDTH),
            const((D_MODEL, D_MODEL)), const((1, D_MODEL)), const((N_EXPERTS, D_MODEL)), const((N_EXPERTS, 1)),
            const((tm, tm)),
        ],
        out_specs=[tok(D_MODEL), pl.BlockSpec((tm * ROW_TILE, 128), lambda i: (i, 0)), lane, lane, lane,
                   pl.BlockSpec((1, N_EXPERTS, 128), lambda i: (i, 0, 0))],
        out_shape=[
            jax.ShapeDtypeStruct((n, D_MODEL), F32),
            jax.ShapeDtypeStruct((n * ROW_TILE, 128), F32),
            jax.ShapeDtypeStruct((nt, TOP_K, tm), jnp.int32),
            jax.ShapeDtypeStruct((nt, TOP_K, tm), F32),
            jax.ShapeDtypeStruct((nt, TOP_K, tm), jnp.int32),
            jax.ShapeDtypeStruct((nt, N_EXPERTS, 128), jnp.int32),
        ],
        compiler_params=_params(("parallel",)),
    )(x, o_rw, o_gl, w_out, gain, w_router_t, b_router, before)


def _moe_kernel(be_ref, nu_ref, epoch_ref, next_ref, rows_ref, parts_ref, xs_ref, wg_hbm, wu_hbm, wd_hbm, bg_ref, bu_ref, bd_ref,
                y_ref, w_f32, wg_b, wu_b, wd_b, sems):
    b = pl.program_id(0)
    prev = be_ref[jnp.maximum(b - 1, 0)]
    new_expert = (b == 0) | (be_ref[b] != prev)

    def fetch(e, slot, i):
        w = (wg_hbm, wu_hbm, wd_hbm)[i]
        return pltpu.make_async_copy(w.at[e], w_f32.at[slot, i], sems.at[slot])

    @pl.when(b == 0)
    def _():
        for i in range(3):
            fetch(be_ref[0], 0, i).start()

    @pl.when(new_expert)
    def _():
        slot = epoch_ref[b] % 2
        for i in range(3):
            fetch(be_ref[b], slot, i).wait()
        wg_b[...] = w_f32[slot, 0].astype(BF16)
        wu_b[...] = w_f32[slot, 1].astype(BF16)
        wd_b[...] = w_f32[slot, 2].astype(BF16)

    first_part, end_part = parts_ref[b] // 4, parts_ref[b] % 4
    for i in range(3):
        @pl.when((next_ref[b] >= 0) & (first_part <= i) & (i < end_part))
        def _():
            fetch(next_ref[b], 1 - epoch_ref[b] % 2, i).start()

    def ffn(m):
        x = _load_row_tiles(xs_ref, m).astype(BF16)
        half = D_MODEL // 2
        acc = None
        for f in range(2):
            fs = slice(f * half, (f + 1) * half)
            gt = _dot(x, wg_b[:, fs]) + bg_ref[0, :, fs]
            up = _dot(x, wu_b[:, fs]) + bu_ref[0, :, fs]
            gt = jnp.minimum(gt, SWIGLU_LIMIT)
            up = jnp.clip(up, -SWIGLU_LIMIT, SWIGLU_LIMIT)
            hid = (up + 1.0) * gt * _sigmoid(SWIGLU_ALPHA * gt)
            part = _dot(hid.astype(BF16), wd_b[fs, :])
            acc = part if acc is None else acc + part
        _store_row_tiles(y_ref, acc + bd_ref[0])

    pieces = (rows_ref[b] + MOE_PIECE - 1) // MOE_PIECE
    for q in range(1, MOE_BLOCK // MOE_PIECE + 1):
        @pl.when((b < nu_ref[0]) & (pieces == q))
        def _():
            ffn(q * MOE_PIECE)


def _moe_ffn(block_expert, n_used, block_rows, xs, w_gate, w_up, w_down, b_gate, b_up, b_down):
    n_blocks = block_expert.shape[0]
    pos = jnp.arange(n_blocks, dtype=jnp.int32)
    change = (pos > 0) & (block_expert != jnp.roll(block_expert, 1))
    epoch = jnp.cumsum(change.astype(jnp.int32))
    later = change[None, :] & (pos[None, :] > pos[:, None])
    first = jnp.min(jnp.where(later, pos[None, :], n_blocks), axis=1)
    next_e = jnp.sum(jnp.where(pos[None, :] == first[:, None], block_expert[None, :], 0), axis=1)
    next_e = jnp.where(first < n_blocks, next_e, -1).astype(jnp.int32)
    run_start = jnp.max(jnp.where((pos[None, :] <= pos[:, None]) & (change | (pos == 0))[None, :], pos[None, :], 0),
                        axis=1)
    q = jnp.minimum(pos - run_start, 3)
    used = pos < n_used[0]
    is_last = jnp.roll(change, -1) | (pos == n_used[0] - 1)
    parts = jnp.where(used, q * 4 + jnp.where(is_last, 3, jnp.minimum(q + 1, 3)), 15).astype(jnp.int32)

    row = lambda b, be, nu, ep, nx, br, pt: (jnp.minimum(b, nu[0] - 1), 0)
    bspec = pl.BlockSpec((1, 1, D_MODEL), lambda b, be, nu, ep, nx, br, pt: (be[b], 0, 0))
    wspec = pl.BlockSpec(memory_space=pl.ANY)
    grid_spec = pltpu.PrefetchScalarGridSpec(
        num_scalar_prefetch=6,
        grid=(n_blocks,),
        in_specs=[pl.BlockSpec((MOE_BLOCK * ROW_TILE, 128), row), wspec, wspec, wspec, bspec, bspec, bspec],
        out_specs=pl.BlockSpec((MOE_BLOCK * ROW_TILE, 128), row),
        scratch_shapes=[pltpu.VMEM((2, 3, D_MODEL, D_MODEL), F32)] + [pltpu.VMEM((D_MODEL, D_MODEL), BF16)] * 3
        + [pltpu.SemaphoreType.DMA((2,))],
    )
    return pl.pallas_call(
        _moe_kernel,
        grid_spec=grid_spec,
        out_shape=jax.ShapeDtypeStruct((n_blocks * MOE_BLOCK * ROW_TILE, 128), F32),
        compiler_params=_params(("arbitrary",)),
    )(block_expert, n_used, epoch, next_e, block_rows, parts, xs, w_gate, w_up, w_down,
      b_gate.reshape(N_EXPERTS, 1, D_MODEL), b_up.reshape(N_EXPERTS, 1, D_MODEL),
      b_down.reshape(N_EXPERTS, 1, D_MODEL))


SEG_ALIGN = 8
GROUP_ROWS = SEG_ALIGN * ROW_TILE


def _local_rows(tm):
    return tm * TOP_K + N_EXPERTS * SEG_ALIGN


BIG_COPY = 4


def _copy_tables():
    n_big = _local_rows(TOK_BLOCK) // SEG_ALIGN // BIG_COPY
    n_small = N_EXPERTS * (BIG_COPY - 1)
    return 2 * n_big, 2 * (n_big + n_small)


def _issue_group_copies(cp_ref, hbm, buf, sem, to_hbm):
    small_at, counts_at = _copy_tables()

    def copy(at, n_groups):
        rows = n_groups * GROUP_ROWS
        b = buf.at[pl.ds(pl.multiple_of(cp_ref[0, 0, at] * GROUP_ROWS, GROUP_ROWS), rows)]
        h = hbm.at[pl.ds(pl.multiple_of(cp_ref[0, 0, at + 1] * GROUP_ROWS, GROUP_ROWS), rows)]
        return pltpu.make_async_copy(b, h, sem) if to_hbm else pltpu.make_async_copy(h, b, sem)

    def big(i, carry):
        copy(2 * i, BIG_COPY).start()
        return carry

    def small(i, carry):
        copy(small_at + 2 * i, 1).start()
        return carry

    lax.fori_loop(0, cp_ref[0, 0, counts_at], big, 0)
    lax.fori_loop(0, cp_ref[0, 0, counts_at + 1], small, 0)


def _wait_group_copies(cp_ref, hbm, buf, sem, to_hbm):
    rows = pl.ds(0, pl.multiple_of(cp_ref[0, 0, _copy_tables()[1] + 2] * GROUP_ROWS, GROUP_ROWS))
    b, h = buf.at[rows], hbm.at[rows]
    (pltpu.make_async_copy(b, h, sem) if to_hbm else pltpu.make_async_copy(h, b, sem)).wait()


def _combine_kernel(grp_c, grp_n, lpos_ref, gate_ref, h_ref, gain_ref, y_hbm, o_ref, ybuf, fbuf, sems):
    i = pl.program_id(0)
    nt = pl.num_programs(0)
    tm = h_ref.shape[0]

    @pl.when(i == 0)
    def _():
        _issue_group_copies(grp_c, y_hbm, ybuf.at[0], sems.at[0], False)

    @pl.when(i + 1 < nt)
    def _():
        _issue_group_copies(grp_n, y_hbm, ybuf.at[(i + 1) % 2], sems.at[(i + 1) % 2], False)

    slot = i % 2
    yb = ybuf.at[slot]
    _wait_group_copies(grp_c, y_hbm, yb, sems.at[slot], False)

    def token_body(t, carry):
        acc = None
        for j in range(TOP_K):
            row = pl.multiple_of(lpos_ref[0, 0, t * TOP_K + j], ROW_TILE)
            term = gate_ref[0, 0, t * TOP_K + j] * yb[pl.ds(row, ROW_TILE), :]
            acc = term if acc is None else acc + term
        fbuf[pl.ds(pl.multiple_of(t * ROW_TILE, ROW_TILE), ROW_TILE), :] = acc
        return carry

    lax.fori_loop(0, tm, token_body, 0, unroll=8)
    f = h_ref[...] + _load_row_tiles(fbuf, tm)
    o_ref[...] = f * lax.rsqrt(jnp.mean(f * f, axis=-1, keepdims=True) + NORM_EPS) * gain_ref[...]


def _combine(h, y_rows, grp3, lpos3, gate3, gain, tm):
    n = h.shape[0]
    nt = n // tm
    n_local = _local_rows(tm)
    gw = grp3.shape[-1]
    smem = lambda shape, imap: pl.BlockSpec(shape, imap, memory_space=pltpu.SMEM)
    cur = lambda i: (i, 0, 0)
    nxt = lambda i: (jnp.minimum(i + 1, nt - 1), 0, 0)
    return pl.pallas_call(
        _combine_kernel,
        grid=(nt,),
        in_specs=[
            smem((1, 1, gw), cur), smem((1, 1, gw), nxt),
            smem((1, 1, TOP_K * tm), cur), smem((1, 1, TOP_K * tm), cur),
            pl.BlockSpec((tm, D_MODEL), lambda i: (i, 0)),
            pl.BlockSpec((1, D_MODEL), lambda i: (0, 0)),
            pl.BlockSpec(memory_space=pl.ANY),
        ],
        out_specs=pl.BlockSpec((tm, D_MODEL), lambda i: (i, 0)),
        out_shape=jax.ShapeDtypeStruct((n, D_MODEL), F32),
        scratch_shapes=[pltpu.VMEM((2, n_local * ROW_TILE, 128), F32), pltpu.VMEM((tm * ROW_TILE, 128), F32),
                        pltpu.SemaphoreType.DMA((2,))],
        compiler_params=pltpu.CompilerParams(dimension_semantics=("arbitrary",), vmem_limit_bytes=VMEM_LIMIT,
                                             disable_bounds_checks=True),
    )(grp3, grp3, lpos3, gate3, h, gain, y_rows)


def _dispatch_kernel(*refs, fill):
    if fill:
        grp_ref, grp_prev, lpos_ref, ends_ref, x_ref, xs_hbm, sorted_buf, zero_scr, sems, zsem = refs
    else:
        grp_ref, grp_prev, lpos_ref, x_ref, _, xs_hbm, sorted_buf, sems = refs
    i = pl.program_id(0)
    tm = x_ref.shape[0] // ROW_TILE
    blk = MOE_PIECE * ROW_TILE
    sorted_scr = sorted_buf.at[i % 2]
    sem = sems.at[i % 2]

    if fill:
        def fill_copy(e):
            start = pl.multiple_of(ends_ref[0, e] * ROW_TILE, blk)
            return pltpu.make_async_copy(zero_scr, xs_hbm.at[pl.ds(start, blk)], zsem)

        @pl.when(i == 0)
        def _():
            zero_scr[...] = jnp.zeros(zero_scr.shape, zero_scr.dtype)
            for e in range(N_EXPERTS):
                @pl.when(ends_ref[1, e] > 0)
                def _():
                    fill_copy(e).start()

    sorted_scr[...] = jnp.zeros(sorted_scr.shape, sorted_scr.dtype)

    def move(t, carry):
        row = x_ref[pl.ds(pl.multiple_of(t * ROW_TILE, ROW_TILE), ROW_TILE), :]
        for j in range(TOP_K):
            dst = pl.multiple_of(lpos_ref[0, 0, t * TOP_K + j], ROW_TILE)
            sorted_scr[pl.ds(dst, ROW_TILE), :] = row
        return carry

    lax.fori_loop(0, tm, move, 0, unroll=8)

    if fill:
        @pl.when(i == 0)
        def _():
            for e in range(N_EXPERTS):
                @pl.when(ends_ref[1, e] > 0)
                def _():
                    fill_copy(e).wait()

    @pl.when(i > 0)
    def _():
        _wait_group_copies(grp_prev, xs_hbm, sorted_buf.at[(i + 1) % 2], sems.at[(i + 1) % 2], True)

    _issue_group_copies(grp_ref, xs_hbm, sorted_scr, sem, True)

    @pl.when(i == pl.num_programs(0) - 1)
    def _():
        _wait_group_copies(grp_ref, xs_hbm, sorted_scr, sem, True)


def _dispatch(xn, grp3, lpos3, tm, n_slots, ends=None, xs=None):
    n = xn.shape[0] // ROW_TILE
    fill = xs is None
    smem = lambda shape, imap: pl.BlockSpec(shape, imap, memory_space=pltpu.SMEM)
    in_specs = [smem((1, 1, grp3.shape[-1]), lambda i: (i, 0, 0)),
                smem((1, 1, grp3.shape[-1]), lambda i: (jnp.maximum(i - 1, 0), 0, 0)),
                smem((1, 1, TOP_K * tm), lambda i: (i, 0, 0))]
    args = [grp3, grp3, lpos3]
    scratch = [pltpu.VMEM((2, _local_rows(tm) * ROW_TILE, 128), F32)]
    if fill:
        in_specs.append(smem((2, N_EXPERTS), lambda i: (0, 0)))
        args.append(ends)
        scratch.append(pltpu.VMEM((MOE_PIECE * ROW_TILE, 128), F32))
    in_specs.append(pl.BlockSpec((tm * ROW_TILE, 128), lambda i: (i, 0)))
    args.append(xn)
    aliases = {}
    if not fill:
        in_specs.append(pl.BlockSpec(memory_space=pl.ANY))
        args.append(xs)
        aliases = {len(args) - 1: 0}
    scratch.append(pltpu.SemaphoreType.DMA((2,)))
    if fill:
        scratch.append(pltpu.SemaphoreType.DMA(()))
    return pl.pallas_call(
        functools.partial(_dispatch_kernel, fill=fill),
        grid=(n // tm,),
        in_specs=in_specs,
        out_specs=pl.BlockSpec(memory_space=pl.ANY),
        out_shape=jax.ShapeDtypeStruct((n_slots * ROW_TILE, 128), F32),
        scratch_shapes=scratch,
        input_output_aliases=aliases,
        compiler_params=pltpu.CompilerParams(dimension_semantics=("arbitrary",), vmem_limit_bytes=VMEM_LIMIT,
                                             disable_bounds_checks=True, has_side_effects=True),
    )(*args)


def _pad_rows(w, rows, offset):
    out = jnp.zeros((rows, w.shape[1]), w.dtype)
    return out.at[offset : offset + w.shape[0]].set(w)


def _routing_tables(counts, n_pairs):
    n_tiles = counts.shape[0]
    n_blocks = (n_pairs + n_tiles * N_EXPERTS * (SEG_ALIGN - 1) + N_EXPERTS * (MOE_BLOCK - 1)
                + MOE_BLOCK - 1) // MOE_BLOCK
    runs = (counts + SEG_ALIGN - 1) // SEG_ALIGN * SEG_ALIGN
    local_start = jnp.cumsum(runs, axis=1) - runs
    total = jnp.sum(runs, axis=0)
    padded = (total + MOE_BLOCK - 1) // MOE_BLOCK * MOE_BLOCK
    pends = jnp.cumsum(padded)
    pstarts = pends - padded
    global_start = pstarts[None, :] + jnp.cumsum(runs, axis=0) - runs
    blocks = jnp.arange(n_blocks, dtype=jnp.int32) * MOE_BLOCK
    n_used = (pends[-1] // MOE_BLOCK).astype(jnp.int32)
    owner = jnp.sum((pends[None, :] <= blocks[:, None]).astype(jnp.int32), axis=1)
    block_expert = jnp.minimum(owner, N_EXPERTS - 1)
    last = jnp.sum(jnp.where(jnp.arange(n_blocks) == n_used - 1, block_expert, 0))
    block_expert = jnp.where(jnp.arange(n_blocks) < n_used, block_expert, last)
    real_end = pstarts + total
    ends = jnp.stack([real_end // MOE_PIECE * MOE_PIECE, real_end % MOE_PIECE]).astype(jnp.int32)
    row_end = jnp.sum(jnp.where(block_expert[:, None] == jnp.arange(N_EXPERTS), (pstarts + total)[None, :], 0), axis=1)
    block_rows = jnp.clip(row_end - blocks, 0, MOE_BLOCK).astype(jnp.int32)
    return (runs, local_start, global_start, ends, block_expert.astype(jnp.int32), n_used.reshape(1), block_rows,
            n_blocks)


def _local_positions(idx3, rank3, local_start):
    hit = idx3[..., None] == jnp.arange(N_EXPERTS, dtype=jnp.int32)
    lpos = rank3 + jnp.sum(jnp.where(hit, local_start[:, None, None, :], 0), axis=-1)
    return _token_major(lpos.astype(jnp.int32) * ROW_TILE)


def _copy_lists(runs, local_start, global_start):
    experts = jnp.arange(N_EXPERTS, dtype=jnp.int32)
    groups = runs // SEG_ALIGN
    l8, g8 = local_start // SEG_ALIGN, global_start // SEG_ALIGN
    n_big, n_small = groups // BIG_COPY, groups % BIG_COPY

    def copy_list(count, first_local, first_global, step, length):
        ends = jnp.cumsum(count, axis=1)
        o = jnp.arange(length, dtype=jnp.int32)
        owner = jnp.minimum(jnp.sum((ends[:, None, :] <= o[None, :, None]).astype(jnp.int32), axis=-1), N_EXPERTS - 1)
        sel = owner[..., None] == experts
        pick = lambda t: jnp.sum(jnp.where(sel, t[:, None, :], 0), axis=-1)
        k = (o[None, :] - pick(ends - count)) * step
        pairs = jnp.stack([pick(first_local) + k, pick(first_global) + k], axis=-1)
        live = (o[None, :] < ends[:, -1:])[..., None]
        return jnp.where(live, pairs, 0).reshape(count.shape[0], 2 * length)

    small_at, counts_at = _copy_tables()
    big = copy_list(n_big, l8, g8, BIG_COPY, small_at // 2)
    small = copy_list(n_small, l8 + n_big * BIG_COPY, g8 + n_big * BIG_COPY, 1, (counts_at - small_at) // 2)
    counts = jnp.stack([jnp.sum(n_big, axis=1), jnp.sum(n_small, axis=1), jnp.sum(groups, axis=1)], axis=1)
    counts = jnp.pad(counts, ((0, 0), (0, 5)))
    return jnp.concatenate([big, small, counts], axis=1).astype(jnp.int32)[:, None, :]


def _token_major(t3):
    return jnp.swapaxes(t3, 1, 2).reshape(t3.shape[0], 1, -1)


def kernel(x_prompt, x_sample, state_rwkv_shift, state_rwkv_wkv, state_gla, norm_mix, w_in, rw_mu, rw_w0, rw_w2, rw_a0, rw_a2, rw_g2, rw_k_k, rw_k_a, rw_r_k, rw_ln_w, rw_ln_b, gla_gk_w2, gla_gk_b, gla_norm_w, w_out, norm_ffn, w_router, b_router, w_gate, b_gate, w_up, b_up, w_down, b_down, norm_final):
    depth = norm_mix.shape[0]
    assert depth == 1
    bp, lp, d = x_prompt.shape
    bs, ls, _ = x_sample.shape
    assert ls == 1 and lp % SEQ_BLOCK == 0
    l = 0
    row = lambda t: t.reshape(1, -1)

    w_in_b = w_in[l].astype(BF16)
    w_in_r = w_in_b[:, :RW_PROJ]
    w_in_g = jnp.pad(w_in_b[:, RW_PROJ:], ((0, 0), (0, GLA_PROJ_PAD - GLA_PROJ)))
    rw = dict(
        mu=row(rw_mu[l]), w0=row(rw_w0[l]), a0=row(rw_a0[l]),
        w2p=_pad_rows(rw_w2[l].astype(BF16), 128, 0), a2p=_pad_rows(rw_a2[l].astype(BF16), 128, 64),
        g2=rw_g2[l].astype(BF16), k_k=row(rw_k_k[l]), k_a=row(rw_k_a[l]), r_k=row(rw_r_k[l]),
        ln_w=row(rw_ln_w[l]), ln_b=row(rw_ln_b[l]))
    gl = dict(gkw=_pad_rows(gla_gk_w2[l].astype(BF16), GLA_LORA_PAD, 0), gkb=row(gla_gk_b[l]),
              norm_w=row(gla_norm_w[l]))
    gain_mix = row(norm_mix[l])

    n_p = bp * lp
    xp = x_prompt.reshape(n_p, d)
    zr_p, zg_p = _inproj(xp, gain_mix, w_in_r, w_in_g, TOK_BLOCK)
    zr_p3 = zr_p.reshape(bp, lp, RW_PROJ)
    o_rw_p, wkv_p = _rwkv_seq(zr_p3, jnp.zeros((bp, 1, RW_PROJ), F32),
                              jnp.zeros((bp, RW_HEADS, RW_N, RW_N), F32), rw, SEQ_BLOCK)
    o_gl_p, gla_p = _gla_seq(zg_p.reshape(bp, lp, GLA_PROJ_PAD),
                             jnp.zeros((bp, GLA_HEADS, GLA_DK, GLA_DV), F32), gl, SEQ_BLOCK)
    shift_p = zr_p3[:, -1, :]

    xs_ = x_sample.reshape(bs, d)
    zr_s, zg_s = _inproj(xs_, gain_mix, w_in_r, w_in_g, bs)
    r, k, v, al, be, dec, g_rw, bonus = _rwkv_step_prep(zr_s, state_rwkv_shift[l], rw)
    wkv_s, o_rw_s = _rwkv_step(state_rwkv_wkv[l], r, k, al, be, dec, v)
    q, kg, vg, g_gl, dec_g = _gla_step_prep(zg_s, gl)
    gla_s, o_gl_s = _gla_step(state_gla[l], q, kg, dec_g, vg, 16)
    o_rw_s2, o_gl_s2 = _step_post(o_rw_s, bonus, g_rw, rw, o_gl_s, g_gl, gl)
    shift_s = zr_s

    w_out_b = w_out[l].astype(BF16)
    router = (w_out_b, row(norm_ffn[l]), w_router[l].T, b_router[l].reshape(N_EXPERTS, 1))
    h_p, xn_p, idx_p, gate_p, rank_p, cnt_p = _outproj_router(
        xp, o_rw_p.reshape(n_p, RW_WIDTH), o_gl_p.reshape(n_p, GLA_WIDTH), *router, TOK_BLOCK)
    h_s, xn_s, idx_s, gate_s, rank_s, cnt_s = _outproj_router(xs_, o_rw_s2, o_gl_s2, *router, bs)
    nt_p = n_p // TOK_BLOCK
    counts = jnp.concatenate([cnt_p[:, :, 0], cnt_s[:, :, 0]], axis=0)
    runs, lstart, gstart, ends, block_expert, n_used, block_rows, n_blocks = _routing_tables(
        counts, (n_p + bs) * TOP_K)
    n_slots = n_blocks * MOE_BLOCK
    lpos_p = _local_positions(idx_p, rank_p, lstart[:nt_p])
    lpos_s = _local_positions(idx_s, rank_s, lstart[nt_p:])
    grp = _copy_lists(runs, lstart, gstart)
    grp_p, grp_s = grp[:nt_p], grp[nt_p:]
    xs_rows = _dispatch(xn_p, grp_p, lpos_p, TOK_BLOCK, n_slots, ends=ends)
    xs_rows = _dispatch(xn_s, grp_s, lpos_s, bs, n_slots, xs=xs_rows)
    y_rows = _moe_ffn(block_expert, n_used, block_rows, xs_rows, w_gate[l], w_up[l], w_down[l], b_gate[l], b_up[l], b_down[l])
    gain_f = row(norm_final)
    y_p = _combine(h_p, y_rows, grp_p, lpos_p, _token_major(gate_p), gain_f, TOK_BLOCK)
    y_s = _combine(h_s, y_rows, grp_s, lpos_s, _token_major(gate_s), gain_f, bs)

    y_prompt = y_p.reshape(bp, lp, d)
    y_sample = y_s.reshape(bs, ls, d)
    return (y_prompt, y_sample, shift_p[None], wkv_p[None], gla_p[None], shift_s[None], wkv_s[None], gla_s[None])
```

```python
import functools

import jax
import jax.numpy as jnp
from jax import lax
from jax.experimental import pallas as pl
from jax.experimental.pallas import tpu as pltpu

F32 = jnp.float32
BF16 = jnp.bfloat16
HIGHEST = lax.Precision.HIGHEST

D_MODEL = 1024
RW_WIDTH = 512
RW_HEADS = 8
RW_N = 64
RW_PROJ = 1792
RW_GN_EPS = 64e-5
GLA_HEADS = 4
GLA_DK = 64
GLA_DV = 128
GLA_WIDTH = 512
GLA_QK = GLA_HEADS * GLA_DK
GLA_PROJ = 1552
GLA_PROJ_PAD = 1664
GLA_LORA_PAD = 128
GLA_GATE_NORMALIZER = 16.0
N_EXPERTS = 32
TOP_K = 4
SWIGLU_LIMIT = 7.0
SWIGLU_ALPHA = 1.702
NORM_EPS = 1e-5
LOG2_E = 1.4426950408889634

RW_CHUNK = 64
GLA_CHUNK = 16
SEQ_BLOCK = 512
TOK_BLOCK = 512
MOE_BLOCK = 512
MOE_PIECE = 128
VMEM_LIMIT = 56 * 1024 * 1024


def _dot(a, b, precision=None):
    return jnp.dot(a, b, preferred_element_type=F32, precision=precision)


def _dot_nt(a, b, precision=None):
    return lax.dot_general(a, b, (((1,), (1,)), ((), ())), preferred_element_type=F32, precision=precision)


def _dot_tn(a, b, precision=None):
    return lax.dot_general(a, b, (((0,), (0,)), ((), ())), preferred_element_type=F32, precision=precision)


def _sigmoid(x):
    return 1.0 / (1.0 + jnp.exp(-x))


def _softplus(x):
    return jnp.maximum(x, 0.0) + jnp.log(1.0 + jnp.exp(-jnp.abs(x)))


def _params(sem):
    return pltpu.CompilerParams(dimension_semantics=sem, vmem_limit_bytes=VMEM_LIMIT)


ROW_TILE = D_MODEL // 128


def _store_row_tiles(ref, x):
    m = x.shape[0]
    for c in range(ROW_TILE):
        ref[pl.ds(c, m, stride=ROW_TILE), :] = x[:, c * 128 : (c + 1) * 128]


def _load_row_tiles(ref, m):
    return jnp.concatenate([ref[pl.ds(c, m, stride=ROW_TILE), :] for c in range(ROW_TILE)], axis=-1)


def _inproj_kernel(x_ref, gain_ref, wr_ref, wg_ref, zr_ref, zg_ref):
    x = x_ref[...]
    xn = x * lax.rsqrt(jnp.mean(x * x, axis=-1, keepdims=True) + NORM_EPS) * gain_ref[...]
    xb = xn.astype(BF16)
    zr_ref[...] = _dot(xb, wr_ref[...])
    zg_ref[...] = _dot(xb, wg_ref[...])


def _inproj(x, gain, w_r, w_g, tm):
    n = x.shape[0]
    return pl.pallas_call(
        _inproj_kernel,
        grid=(n // tm,),
        in_specs=[
            pl.BlockSpec((tm, D_MODEL), lambda i: (i, 0)),
            pl.BlockSpec((1, D_MODEL), lambda i: (0, 0)),
            pl.BlockSpec((D_MODEL, RW_PROJ), lambda i: (0, 0)),
            pl.BlockSpec((D_MODEL, GLA_PROJ_PAD), lambda i: (0, 0)),
        ],
        out_specs=[
            pl.BlockSpec((tm, RW_PROJ), lambda i: (i, 0)),
            pl.BlockSpec((tm, GLA_PROJ_PAD), lambda i: (i, 0)),
        ],
        out_shape=[
            jax.ShapeDtypeStruct((n, RW_PROJ), F32),
            jax.ShapeDtypeStruct((n, GLA_PROJ_PAD), F32),
        ],
        compiler_params=_params(("parallel",)),
    )(x, gain, w_r, w_g)


def _rwkv_features(zs, w0, w2p, a0, a2p, g2, k_k, k_a):
    W = RW_WIDTH
    r = zs[:, 0:W]
    k_raw = zs[:, W : 2 * W]
    v = zs[:, 2 * W : 3 * W]
    zwa = zs[:, 3 * W : 3 * W + 128]
    zg = zs[:, 3 * W + 128 :]
    w = -_softplus(-(w0 + _dot(jnp.tanh(zwa).astype(BF16), w2p))) - 0.5
    log_decay = -jnp.exp(w)
    a = _sigmoid(a0 + _dot(zwa.astype(BF16), a2p))
    g = _dot(_sigmoid(zg).astype(BF16), g2)
    kk_raw = k_raw * k_k
    k = k_raw * (1.0 + (a - 1.0) * k_a)
    return r, k, v, kk_raw, a, log_decay, g


def _level_mask(ri, ci, lvl):
    same = (ri >> (lvl + 1)) == (ci >> (lvl + 1))
    return same & (((ri >> lvl) & 1) == 1) & (((ci >> lvl) & 1) == 0)


def _rwkv_seq_kernel(z_ref, shift0_ref, s0_ref, mu_ref, w0_ref, w2_ref, a0_ref, a2_ref, g2_ref, kk_ref, ka_ref,
                     rk_ref, lnw_ref, lnb_ref, o_ref, sout_ref,
                     m_scr, prev_scr, r_scr, k_scr, v_scr, kkr_scr, a_scr, lw_scr, on_scr, bon_scr):
    C = RW_CHUNK
    N = RW_N
    t_idx = pl.program_id(1)
    tb = z_ref.shape[1]
    zero_nn = jnp.zeros((N, N), F32)

    @pl.when(t_idx == 0)
    def _():
        prev_scr[...] = shift0_ref[0]
        for p in range(RW_HEADS // 2):
            top = jnp.concatenate([s0_ref[0, 2 * p].T, zero_nn], axis=1)
            bot = jnp.concatenate([zero_nn, s0_ref[0, 2 * p + 1].T], axis=1)
            m_scr[p] = jnp.concatenate([top, bot], axis=0)

    z = z_ref[0]
    row = lax.broadcasted_iota(jnp.int32, z.shape, 0)
    z_prev = jnp.where(row == 0, prev_scr[...], pltpu.roll(z, 1, axis=0))
    prev_scr[...] = z[tb - 1 : tb, :]
    zs = z + mu_ref[...] * (z_prev - z)
    r, k, v, kk_raw, a, log_decay, g = _rwkv_features(
        zs, w0_ref[...], w2_ref[...], a0_ref[...], a2_ref[...], g2_ref[...], kk_ref[...], ka_ref[...])
    P2 = 2 * N
    left1 = lax.broadcasted_iota(jnp.int32, (1, P2), 1) < N

    def head_sum(x):
        s0 = jnp.sum(jnp.where(left1, x, 0.0), axis=-1, keepdims=True)
        s1 = jnp.sum(jnp.where(left1, 0.0, x), axis=-1, keepdims=True)
        return jnp.where(left1, s0, s1)

    def head_sum_wide(x):
        return jnp.concatenate([head_sum(x[:, p * P2 : (p + 1) * P2]) for p in range(RW_HEADS // 2)], axis=1)

    alpha = kk_raw * lax.rsqrt(jnp.maximum(head_sum_wide(kk_raw * kk_raw), 1e-24))
    r_scr[...] = r
    k_scr[...] = k
    v_scr[...] = v
    kkr_scr[...] = alpha
    a_scr[...] = alpha * a
    lw_scr[...] = log_decay
    bon_scr[...] = head_sum_wide(r * k * rk_ref[...]) * v

    ri = lax.broadcasted_iota(jnp.int32, (C, P2), 0)
    ci = lax.broadcasted_iota(jnp.int32, (C, P2), 1) % N
    left = lax.broadcasted_iota(jnp.int32, (C, P2), 1) < N
    tril = ri >= ci
    stril = ri > ci
    eye_f = (ri == ci).astype(F32)
    rb = lax.broadcasted_iota(jnp.int32, (P2, P2), 0)
    cb = lax.broadcasted_iota(jnp.int32, (P2, P2), 1)
    same_head = (rb < N) == (cb < N)
    eye_b = rb == cb
    rc = lax.broadcasted_iota(jnp.int32, (C, C), 0)
    cc = lax.broadcasted_iota(jnp.int32, (C, C), 1)
    tril_f = (rc >= cc).astype(F32)

    def bdiag(x):
        return jnp.concatenate([jnp.where(left, x, 0.0), jnp.where(left, 0.0, x)], axis=0)

    n_sub = tb // C
    pairs = range(RW_HEADS // 2)

    def chunk_body(it, carry):
        units = [(s, p) for s in range(n_sub) for p in pairs]
        sls = [pl.ds(pl.multiple_of((it * n_sub + s) * C, C), C) for s in range(n_sub)]
        prep = []
        for s in range(n_sub):
            lw = lw_scr[sls[s], :]
            cum = _dot(tril_f, lw, precision=HIGHEST)
            cum_last = cum[C - 1 : C, :]
            prep.append(dict(
                e_incl=jnp.exp(cum), e_excl=jnp.exp(cum - lw), e_neg=jnp.exp(-cum),
                e_tail=jnp.exp(cum_last - cum), p_last=jnp.exp(cum_last),
                r=r_scr[sls[s], :], k=k_scr[sls[s], :], v=v_scr[sls[s], :], kk=kkr_scr[sls[s], :],
                a=a_scr[sls[s], :]))
        lanes = [slice(p * P2, (p + 1) * P2) for p in pairs]
        get = lambda name: [prep[s][name][:, lanes[p]] for s, p in units]
        r2, k2, v2, al, be = get("r"), get("k"), get("v"), get("kk"), get("a")
        e_incl, e_excl, e_neg, e_tail, p_last = get("e_incl"), get("e_excl"), get("e_neg"), get("e_tail"), get("p_last")
        un = range(len(units))
        al_t = [al[u] * e_excl[u] for u in un]
        r_t = [r2[u] * e_incl[u] for u in un]
        be_n = [be[u] * e_neg[u] for u in un]
        k_n = [k2[u] * e_neg[u] for u in un]
        k_et = [(k2[u] * e_tail[u]).T for u in un]
        be_et = [(be[u] * e_tail[u]).T for u in un]
        v_bd = [bdiag(v2[u]) for u in un]
        lhs = [jnp.concatenate([al_t[u], r_t[u]], axis=0) for u in un]
        s_b = [_dot_nt(lhs[u], bdiag(be_n[u])) for u in un]
        s_k = [_dot_nt(lhs[u], bdiag(k_n[u])) for u in un]
        l_ab = [jnp.where(stril, s_b[u][:C], 0.0) for u in un]
        a_rb = [jnp.where(tril, s_b[u][C:], 0.0) for u in un]
        l_ak = [jnp.where(stril, s_k[u][:C], 0.0) for u in un]
        a_rk = [jnp.where(tril, s_k[u][C:], 0.0) for u in un]
        lakv = [_dot(l_ak[u], v_bd[u]) for u in un]
        arkv = [_dot(a_rk[u], v_bd[u]) for u in un]
        kev = [_dot(k_et[u], v2[u]) for u in un]
        t_inv = [eye_f - jnp.where(_level_mask(ri, ci, 0), l_ab[u], 0.0) for u in un]
        lvl = 1
        while (1 << lvl) < C:
            lm = _level_mask(ri, ci, lvl)
            tn = [_dot(t_inv[u], bdiag(jnp.where(lm, l_ab[u], 0.0))) for u in un]
            t_inv = [t_inv[u] - _dot(tn[u], bdiag(t_inv[u])) for u in un]
            lvl += 1
        a_til = [_dot(t_inv[u], bdiag(al_t[u])) for u in un]
        b_til = [_dot(t_inv[u], bdiag(lakv[u])) for u in un]
        r_hat = [r_t[u] - _dot(a_rb[u], bdiag(a_til[u])) for u in un]
        o_hat = [arkv[u] - _dot(a_rb[u], bdiag(b_til[u])) for u in un]
        g_bd = [jnp.where(same_head, jnp.where(eye_b, p_last[u], 0.0) - _dot(be_et[u], a_til[u]), 0.0) for u in un]
        h_bd = [jnp.where(same_head, kev[u] - _dot(be_et[u], b_til[u]), 0.0) for u in un]
        lhs_m = [jnp.concatenate([r_hat[u], g_bd[u]], axis=0) for u in un]
        for u, (s, p) in enumerate(units):
            res = _dot(lhs_m[u], m_scr[p])
            m_scr[p] = res[C:] + h_bd[u]
            o_p = res[:C] + o_hat[u]
            cen = o_p - head_sum(o_p) * (1.0 / N)
            var = head_sum(cen * cen) * (1.0 / N)
            on_scr[sls[s], lanes[p]] = cen * lax.rsqrt(var + RW_GN_EPS)
        return carry

    lax.fori_loop(0, tb // (C * n_sub), chunk_body, 0)
    out = (on_scr[...] * lnw_ref[...] + lnb_ref[...] + bon_scr[...]) * g
    o_ref[0] = out.astype(o_ref.dtype)

    @pl.when(t_idx == pl.num_programs(1) - 1)
    def _():
        for p in range(RW_HEADS // 2):
            m = m_scr[p]
            sout_ref[0, 2 * p] = m[:N, :N].T
            sout_ref[0, 2 * p + 1] = m[N:, N:].T


def _rwkv_seq(z3, shift0, s0, rw, tb):
    b, l, _ = z3.shape
    const = lambda shape: pl.BlockSpec(shape, lambda i, j: (0,) * len(shape))
    wide = lambda: pltpu.VMEM((tb, RW_WIDTH), F32)
    return pl.pallas_call(
        _rwkv_seq_kernel,
        grid=(b, l // tb),
        in_specs=[
            pl.BlockSpec((1, tb, RW_PROJ), lambda i, j: (i, j, 0)),
            pl.BlockSpec((1, 1, RW_PROJ), lambda i, j: (i, 0, 0)),
            pl.BlockSpec((1, RW_HEADS, RW_N, RW_N), lambda i, j: (i, 0, 0, 0)),
            const((1, RW_PROJ)),
            const((1, RW_WIDTH)), const((128, RW_WIDTH)),
            const((1, RW_WIDTH)), const((128, RW_WIDTH)),
            const((128, RW_WIDTH)),
            const((1, RW_WIDTH)), const((1, RW_WIDTH)), const((1, RW_WIDTH)),
            const((1, RW_WIDTH)), const((1, RW_WIDTH)),
        ],
        out_specs=[
            pl.BlockSpec((1, tb, RW_WIDTH), lambda i, j: (i, j, 0)),
            pl.BlockSpec((1, RW_HEADS, RW_N, RW_N), lambda i, j: (i, 0, 0, 0)),
        ],
        out_shape=[
            jax.ShapeDtypeStruct((b, l, RW_WIDTH), BF16),
            jax.ShapeDtypeStruct((b, RW_HEADS, RW_N, RW_N), F32),
        ],
        scratch_shapes=[
            pltpu.VMEM((RW_HEADS // 2, 2 * RW_N, 2 * RW_N), F32),
            pltpu.VMEM((1, RW_PROJ), F32),
            wide(), wide(), wide(), wide(), wide(), wide(), wide(), wide(),
        ],
        compiler_params=_params(("parallel", "arbitrary")),
    )(z3, shift0, s0, rw["mu"], rw["w0"], rw["w2p"], rw["a0"], rw["a2p"], rw["g2"], rw["k_k"], rw["k_a"],
      rw["r_k"], rw["ln_w"], rw["ln_b"])


def _rwkv_step_prep_kernel(z_ref, shift0_ref, mu_ref, w0_ref, w2_ref, a0_ref, a2_ref, g2_ref, kk_ref, ka_ref,
                           rk_ref, r_ref, k_ref, v_ref, al_ref, be_ref, dec_ref, g_ref, bon_ref):
    z = z_ref[...]
    zs = z + mu_ref[...] * (shift0_ref[...] - z)
    r, k, v, kk_raw, a, log_decay, g = _rwkv_features(
        zs, w0_ref[...], w2_ref[...], a0_ref[...], a2_ref[...], g2_ref[...], kk_ref[...], ka_ref[...])
    rk_all = rk_ref[...]
    als = []
    for h in range(RW_HEADS):
        hs = slice(h * RW_N, (h + 1) * RW_N)
        kk_h = kk_raw[:, hs]
        nrm = jnp.sqrt(jnp.sum(kk_h * kk_h, axis=-1, keepdims=True))
        als.append(kk_h / jnp.maximum(nrm, 1e-12))
        bon_ref[:, hs] = jnp.sum(r[:, hs] * k[:, hs] * rk_all[:, hs], axis=-1, keepdims=True) * v[:, hs]
    al = jnp.concatenate(als, axis=1)
    r_ref[...] = r.T
    k_ref[...] = k.T
    v_ref[...] = v.T
    al_ref[...] = al.T
    be_ref[...] = (al * a).T
    dec_ref[...] = jnp.exp(log_decay).T
    g_ref[...] = g


def _rwkv_step_prep(z, shift0, rw):
    n = z.shape[0]
    out = jax.ShapeDtypeStruct((n, RW_WIDTH), F32)
    out_t = jax.ShapeDtypeStruct((RW_WIDTH, n), F32)
    return pl.pallas_call(
        _rwkv_step_prep_kernel,
        out_shape=[out_t] * 6 + [out] * 2,
        compiler_params=pltpu.CompilerParams(vmem_limit_bytes=VMEM_LIMIT),
    )(z, shift0, rw["mu"], rw["w0"], rw["w2p"], rw["a0"], rw["a2p"], rw["g2"], rw["k_k"], rw["k_a"], rw["r_k"])


def _rwkv_step_kernel(s_ref, r_ref, k_ref, al_ref, be_ref, dec_ref, v_ref, snew_ref, o_ref):
    r, k, al, be, dec = r_ref[...], k_ref[...], al_ref[...], be_ref[...], dec_ref[...]

    def body(g, carry):
        rows = pl.ds(pl.multiple_of(g * 8, 8), 8)
        v8 = v_ref[rows, :]
        outs = []
        for j in range(8):
            s = s_ref[0, g * 8 + j]
            sa = -jnp.sum(s * al, axis=0, keepdims=True)
            s_new = s * dec + sa * be + v8[j : j + 1, :] * k
            snew_ref[0, g * 8 + j] = s_new
            outs.append(jnp.sum(s_new * r, axis=0, keepdims=True))
        o_ref[rows, :] = jnp.concatenate(outs, axis=0)
        return carry

    lax.fori_loop(0, RW_N // 8, body, 0)


def _rwkv_step(s0, r, k, al, be, dec, v):
    n = s0.shape[0]
    s_t = jnp.transpose(s0, (1, 2, 3, 0))
    s_spec = pl.BlockSpec((1, RW_N, RW_N, n), lambda h: (h, 0, 0, 0))
    op_spec = pl.BlockSpec((RW_N, n), lambda h: (h, 0))
    s_new_t, o_t = pl.pallas_call(
        _rwkv_step_kernel,
        grid=(RW_HEADS,),
        in_specs=[s_spec] + [op_spec] * 6,
        out_specs=[s_spec, op_spec],
        out_shape=[
            jax.ShapeDtypeStruct(s_t.shape, F32),
            jax.ShapeDtypeStruct((RW_WIDTH, n), F32),
        ],
        compiler_params=_params(("parallel",)),
    )(s_t, r, k, al, be, dec, v)
    return jnp.transpose(s_new_t, (3, 0, 1, 2)), o_t


def _gla_features(z, gkw, gkb):
    q = z[:, 0:GLA_QK] * (GLA_DK ** -0.5)
    k = z[:, GLA_QK : 2 * GLA_QK]
    v = z[:, 2 * GLA_QK : 2 * GLA_QK + GLA_WIDTH]
    g = z[:, 2 * GLA_QK + GLA_WIDTH : 2 * GLA_QK + 2 * GLA_WIDTH]
    zgk = z[:, 2 * GLA_QK + 2 * GLA_WIDTH :]
    gk = -_softplus(-(_dot(zgk.astype(BF16), gkw) + gkb)) / GLA_GATE_NORMALIZER
    return q, k, v, g, gk


def _gla_finish(o, g, norm_w):
    outs = []
    for h in range(GLA_HEADS):
        hs = slice(h * GLA_DV, (h + 1) * GLA_DV)
        o_h = o[:, hs]
        o_h = o_h * lax.rsqrt(jnp.mean(o_h * o_h, axis=-1, keepdims=True) + NORM_EPS) * norm_w
        g_h = g[:, hs]
        outs.append(o_h * (g_h * _sigmoid(g_h)))
    return jnp.concatenate(outs, axis=-1)


def _gla_seq_kernel(z_ref, s0_ref, gkw_ref, gkb_ref, nw_ref, wsel_ref, o_ref, sout_ref,
                    st_scr, x_scr, gc_scr, oi_scr):
    C = GLA_CHUNK
    G = 128
    t_idx = pl.program_id(1)
    tb = z_ref.shape[1]
    nc = tb // C
    zero_vk = jnp.zeros((GLA_DV, GLA_DK), F32)

    @pl.when(t_idx == 0)
    def _():
        for p in range(GLA_HEADS // 2):
            top = jnp.concatenate([s0_ref[0, 2 * p].T, zero_vk], axis=1)
            bot = jnp.concatenate([zero_vk, s0_ref[0, 2 * p + 1].T], axis=1)
            st_scr[p] = jnp.concatenate([top, bot], axis=0)

    q, k, v, g, gk = _gla_features(z_ref[0], gkw_ref[...], gkb_ref[...])
    ri = lax.broadcasted_iota(jnp.int32, (G, G), 0)
    ci = lax.broadcasted_iota(jnp.int32, (G, G), 1)
    cum_mat = ((ri // C == ci // C) & (ri >= ci)).astype(F32)
    for m in range(tb // G):
        rows = slice(m * G, (m + 1) * G)
        gc_scr[rows, :] = _dot(cum_mat, gk[rows, :], precision=HIGHEST)
    gcum = gc_scr[...]

    rg = lax.broadcasted_iota(jnp.int32, (tb, 2 * G), 0)
    cg = lax.broadcasted_iota(jnp.int32, (tb, 2 * G), 1)
    blk_mask = ((cg % G) // C == (rg % G) // C) & (cg % C <= rg % C)
    for p in range(GLA_HEADS // 2):
        ls = slice(p * 128, (p + 1) * 128)
        q3 = q[:, ls].reshape(nc, C, 128)
        k3 = k[:, ls].reshape(nc, C, 128)
        g3 = gcum[:, ls].reshape(nc, C, 128) * LOG2_E
        half = C // 2
        for j in range(C):
            lo = 0 if j < half else half
            e = (q3[:, lo:] * jnp.exp2(jnp.minimum(g3[:, lo:] - g3[:, j : j + 1, :], 0.0))) * k3[:, j : j + 1, :]
            if lo:
                e = jnp.concatenate([jnp.zeros((nc, lo, 128), F32), e], axis=1)
            x_scr[:, j * 128 : (j + 1) * 128] = e.reshape(tb, 128).astype(BF16)
        a_t = jnp.where(blk_mask, _dot(x_scr[...], wsel_ref[...]), 0.0).astype(BF16)
        for hl in range(2):
            h = 2 * p + hl
            for m in range(tb // G):
                rows = slice(m * G, (m + 1) * G)
                a_blk = a_t[rows, hl * G : (hl + 1) * G]
                oi_scr[rows, h * GLA_DV : (h + 1) * GLA_DV] = _dot(
                    a_blk, v[rows, h * GLA_DV : (h + 1) * GLA_DV].astype(BF16))

    CG = G // C
    rt = lax.broadcasted_iota(jnp.int32, (G, CG * 128), 0)
    ct = lax.broadcasted_iota(jnp.int32, (G, CG * 128), 1)
    own_chunk = rt // C == ct // 128
    rs = lax.broadcasted_iota(jnp.int32, (2 * GLA_DV, CG * 128), 0)
    cs = lax.broadcasted_iota(jnp.int32, (2 * GLA_DV, CG * 128), 1)
    same_head = rs // GLA_DV == (cs % 128) // GLA_DK

    def chunk_diag(x):
        return jnp.where(own_chunk, jnp.concatenate([x] * CG, axis=1), 0.0)

    for m in range(tb // G):
        rows = slice(m * G, (m + 1) * G)
        for p in range(GLA_HEADS // 2):
            ls = slice(p * 128, (p + 1) * 128)
            vs = slice(p * 2 * GLA_DV, (p + 1) * 2 * GLA_DV)
            g_g = gcum[rows, ls]
            g3 = g_g.reshape(CG, C, 128)
            g_last = jnp.broadcast_to(g3[:, C - 1 : C, :], (CG, C, 128)).reshape(G, 128)
            q_t = q[rows, ls] * jnp.exp(g_g)
            k_t = k[rows, ls] * jnp.exp(g_last - g_g)
            d_s = jnp.where(same_head, _dot_tn(v[rows, vs], chunk_diag(k_t)), 0.0)
            st = st_scr[p]
            starts = []
            for c in range(CG):
                starts.append(st)
                decay = jnp.exp(g_g[c * C + C - 1 : c * C + C, :])
                st = st * decay + d_s[:, c * 128 : (c + 1) * 128]
            st_scr[p] = st
            oi_scr[rows, vs] += _dot_nt(chunk_diag(q_t), jnp.concatenate(starts, axis=1))

    o_ref[0] = _gla_finish(oi_scr[...], g, nw_ref[...]).astype(o_ref.dtype)

    @pl.when(t_idx == pl.num_programs(1) - 1)
    def _():
        for p in range(GLA_HEADS // 2):
            st = st_scr[p]
            sout_ref[0, 2 * p] = st[:GLA_DV, :GLA_DK].T
            sout_ref[0, 2 * p + 1] = st[GLA_DV:, GLA_DK:].T


def _gla_select_matrix():
    j = jnp.arange(GLA_CHUNK)[:, None, None]
    hl = jnp.arange(2)[None, :, None]
    rows_j = jnp.broadcast_to(j, (GLA_CHUNK, 2, GLA_DK)).reshape(-1)
    rows_h = jnp.broadcast_to(hl, (GLA_CHUNK, 2, GLA_DK)).reshape(-1)
    cols = jnp.arange(256)
    sel = (rows_j[:, None] == cols[None, :] % GLA_CHUNK) & (rows_h[:, None] == cols[None, :] // 128)
    return sel.astype(BF16)


def _gla_seq(z3, s0, gl, tb):
    b, l, _ = z3.shape
    const = lambda shape: pl.BlockSpec(shape, lambda i, j: (0,) * len(shape))
    return pl.pallas_call(
        _gla_seq_kernel,
        grid=(b, l // tb),
        in_specs=[
            pl.BlockSpec((1, tb, GLA_PROJ_PAD), lambda i, j: (i, j, 0)),
            pl.BlockSpec((1, GLA_HEADS, GLA_DK, GLA_DV), lambda i, j: (i, 0, 0, 0)),
            const((GLA_LORA_PAD, GLA_QK)), const((1, GLA_QK)), const((1, GLA_DV)),
            const((GLA_CHUNK * 128, 256)),
        ],
        out_specs=[
            pl.BlockSpec((1, tb, GLA_WIDTH), lambda i, j: (i, j, 0)),
            pl.BlockSpec((1, GLA_HEADS, GLA_DK, GLA_DV), lambda i, j: (i, 0, 0, 0)),
        ],
        out_shape=[
            jax.ShapeDtypeStruct((b, l, GLA_WIDTH), BF16),
            jax.ShapeDtypeStruct((b, GLA_HEADS, GLA_DK, GLA_DV), F32),
        ],
        scratch_shapes=[
            pltpu.VMEM((GLA_HEADS // 2, 2 * GLA_DV, 2 * GLA_DK), F32),
            pltpu.VMEM((tb, GLA_CHUNK * 128), BF16),
            pltpu.VMEM((tb, GLA_QK), F32), pltpu.VMEM((tb, GLA_WIDTH), F32),
        ],
        compiler_params=_params(("parallel", "arbitrary")),
    )(z3, s0, gl["gkw"], gl["gkb"], gl["norm_w"], _gla_select_matrix())


def _gla_step_prep_kernel(z_ref, gkw_ref, gkb_ref, q_ref, k_ref, v_ref, g_ref, dec_ref):
    q, k, v, g, gk = _gla_features(z_ref[...], gkw_ref[...], gkb_ref[...])
    q_ref[...] = q
    k_ref[...] = k
    v_ref[...] = v
    g_ref[...] = g
    dec_ref[...] = jnp.exp(gk)


def _gla_step_prep(z, gl):
    n = z.shape[0]
    qk = jax.ShapeDtypeStruct((n, GLA_QK), F32)
    wide = jax.ShapeDtypeStruct((n, GLA_WIDTH), F32)
    return pl.pallas_call(
        _gla_step_prep_kernel,
        out_shape=[qk, qk, wide, wide, qk],
        compiler_params=pltpu.CompilerParams(vmem_limit_bytes=VMEM_LIMIT),
    )(z, gl["gkw"], gl["gkb"])


def _gla_step_kernel(s_ref, q_ref, k_ref, dec_ref, v_ref, snew_ref, o_ref):
    bb = s_ref.shape[0]
    rows = lax.broadcasted_iota(jnp.int32, (bb, bb * GLA_DK), 0)
    cols = lax.broadcasted_iota(jnp.int32, (bb, bb * GLA_DK), 1)
    own = rows == cols // GLA_DK
    ones = jnp.ones((bb, GLA_DV), F32)
    zeros = jnp.zeros((bb, GLA_DV), F32)

    def seq_diag(x):
        return jnp.where(own, jnp.concatenate([x] * bb, axis=1), 0.0)

    for h in range(GLA_HEADS):
        ks = slice(h * GLA_DK, (h + 1) * GLA_DK)
        vs = slice(h * GLA_DV, (h + 1) * GLA_DV)
        s = s_ref[:, h].reshape(bb * GLA_DK, GLA_DV)
        v = v_ref[:, vs]
        lhs_t = jnp.concatenate([seq_diag(k_ref[:, ks]), seq_diag(dec_ref[:, ks])], axis=0)
        rhs = jnp.concatenate([jnp.concatenate([v, zeros], axis=1), jnp.concatenate([zeros, ones], axis=1)], axis=0)
        both = _dot_tn(lhs_t, rhs, precision=HIGHEST)
        s_new = s * both[:, GLA_DV:] + both[:, :GLA_DV]
        snew_ref[:, h] = s_new.reshape(bb, GLA_DK, GLA_DV)
        o_ref[:, vs] = _dot(seq_diag(q_ref[:, ks]), s_new)


def _gla_step(s0, q, k, dec, v, bb):
    n = s0.shape[0]
    s_spec = pl.BlockSpec((bb, GLA_HEADS, GLA_DK, GLA_DV), lambda i: (i, 0, 0, 0))
    qk_spec = pl.BlockSpec((bb, GLA_QK), lambda i: (i, 0))
    v_spec = pl.BlockSpec((bb, GLA_WIDTH), lambda i: (i, 0))
    return pl.pallas_call(
        _gla_step_kernel,
        grid=(n // bb,),
        in_specs=[s_spec, qk_spec, qk_spec, qk_spec, v_spec],
        out_specs=[s_spec, v_spec],
        out_shape=[jax.ShapeDtypeStruct(s0.shape, F32), jax.ShapeDtypeStruct((n, GLA_WIDTH), F32)],
        compiler_params=_params(("parallel",)),
    )(s0, q, k, dec, v)


def _step_post_kernel(orw_ref, bon_ref, grw_ref, lnw_ref, lnb_ref, ogl_ref, ggl_ref, nw_ref, o_rw_ref, o_gl_ref):
    o = orw_ref[...].T
    for h in range(RW_HEADS):
        hs = slice(h * RW_N, (h + 1) * RW_N)
        o_h = o[:, hs]
        mean = jnp.mean(o_h, axis=-1, keepdims=True)
        cen = o_h - mean
        var = jnp.mean(cen * cen, axis=-1, keepdims=True)
        on = cen * lax.rsqrt(var + RW_GN_EPS)
        res = (on * lnw_ref[:, hs] + lnb_ref[:, hs] + bon_ref[:, hs]) * grw_ref[:, hs]
        o_rw_ref[:, hs] = res.astype(o_rw_ref.dtype)
    o_gl_ref[...] = _gla_finish(ogl_ref[...], ggl_ref[...], nw_ref[...]).astype(o_gl_ref.dtype)


def _step_post(o_rw, bonus, g_rw, rw, o_gl, g_gl, gl):
    n = bonus.shape[0]
    return pl.pallas_call(
        _step_post_kernel,
        out_shape=[jax.ShapeDtypeStruct((n, RW_WIDTH), BF16), jax.ShapeDtypeStruct((n, GLA_WIDTH), BF16)],
        compiler_params=pltpu.CompilerParams(vmem_limit_bytes=VMEM_LIMIT),
    )(o_rw, bonus, g_rw, rw["ln_w"], rw["ln_b"], o_gl, g_gl, gl["norm_w"])


def _outproj_router_kernel(x_ref, orw_ref, ogl_ref, wo_ref, gain_ref, wrt_ref, br_ref, before_ref,
                           h_ref, xn_ref, idx_ref, gate_ref, rank_ref, cnt_ref):
    mix = jnp.concatenate([orw_ref[...], ogl_ref[...]], axis=-1)
    h = x_ref[...] + _dot(mix, wo_ref[...])
    h_ref[...] = h
    xn = h * lax.rsqrt(jnp.mean(h * h, axis=-1, keepdims=True) + NORM_EPS) * gain_ref[...]
    _store_row_tiles(xn_ref, xn)
    logits = _dot_nt(wrt_ref[...], xn) + br_ref[...]
    eidx = lax.broadcasted_iota(jnp.int32, logits.shape, 0)
    vals, idxs = [], []
    work = logits
    chosen = jnp.zeros(logits.shape, F32)
    for _ in range(TOP_K):
        m = jnp.max(work, axis=0, keepdims=True)
        sel = jnp.min(jnp.where(work == m, eidx, N_EXPERTS), axis=0, keepdims=True)
        hit = eidx == sel
        work = jnp.where(hit, -jnp.inf, work)
        chosen = chosen + hit.astype(F32)
        vals.append(m)
        idxs.append(sel)
    prefix = _dot(chosen.astype(BF16), before_ref[...])
    exps = [jnp.exp(v - vals[0]) for v in vals]
    denom = exps[0] + exps[1] + exps[2] + exps[3]
    for j in range(TOP_K):
        idx_ref[0, j : j + 1, :] = idxs[j]
        gate_ref[0, j : j + 1, :] = exps[j] / denom
        rank = jnp.sum(jnp.where(eidx == idxs[j], prefix, 0.0), axis=0, keepdims=True)
        rank_ref[0, j : j + 1, :] = rank.astype(jnp.int32)
    cnt = jnp.sum(chosen, axis=1, keepdims=True)
    cnt_ref[0] = jnp.broadcast_to(cnt, (N_EXPERTS, 128)).astype(jnp.int32)


def _outproj_router(x, o_rw, o_gl, w_out, gain, w_router_t, b_router, tm):
    n = x.shape[0]
    nt = n // tm
    const = lambda shape: pl.BlockSpec(shape, lambda i: (0,) * len(shape))
    tok = lambda width: pl.BlockSpec((tm, width), lambda i: (i, 0))
    lane = pl.BlockSpec((1, TOP_K, tm), lambda i: (i, 0, 0))
    t = jnp.arange(tm, dtype=jnp.int32)
    before = (t[:, None] < t[None, :]).astype(BF16)
    return pl.pallas_call(
        _outproj_router_kernel,
        grid=(nt,),
        in_specs=[
            tok(D_MODEL), tok(RW_WIDTH), tok(GLA_WIDTH),
            const((D_MODEL, D_MODEL)), const((1, D_MODEL)), const((N_EXPERTS, D_MODEL)), const((N_EXPERTS, 1)),
            const((tm, tm)),
        ],
        out_specs=[tok(D_MODEL), pl.BlockSpec((tm * ROW_TILE, 128), lambda i: (i, 0)), lane, lane, lane,
                   pl.BlockSpec((1, N_EXPERTS, 128), lambda i: (i, 0, 0))],
        out_shape=[
            jax.ShapeDtypeStruct((n, D_MODEL), F32),
            jax.ShapeDtypeStruct((n * ROW_TILE, 128), F32),
            jax.ShapeDtypeStruct((nt, TOP_K, tm), jnp.int32),
            jax.ShapeDtypeStruct((nt, TOP_K, tm), F32),
            jax.ShapeDtypeStruct((nt, TOP_K, tm), jnp.int32),
            jax.ShapeDtypeStruct((nt, N_EXPERTS, 128), jnp.int32),
        ],
        compiler_params=_params(("parallel",)),
    )(x, o_rw, o_gl, w_out, gain, w_router_t, b_router, before)


def _moe_kernel(be_ref, nu_ref, epoch_ref, next_ref, rows_ref, parts_ref, xs_ref, wg_hbm, wu_hbm, wd_hbm, bg_ref, bu_ref, bd_ref,
                y_ref, w_f32, wg_b, wu_b, wd_b, sems):
    b = pl.program_id(0)
    prev = be_ref[jnp.maximum(b - 1, 0)]
    new_expert = (b == 0) | (be_ref[b] != prev)

    def fetch(e, slot, i):
        w = (wg_hbm, wu_hbm, wd_hbm)[i]
        return pltpu.make_async_copy(w.at[e], w_f32.at[slot, i], sems.at[slot])

    @pl.when(b == 0)
    def _():
        for i in range(3):
            fetch(be_ref[0], 0, i).start()

    @pl.when(new_expert)
    def _():
        slot = epoch_ref[b] % 2
        for i in range(3):
            fetch(be_ref[b], slot, i).wait()
        wg_b[...] = w_f32[slot, 0].astype(BF16)
        wu_b[...] = w_f32[slot, 1].astype(BF16)
        wd_b[...] = w_f32[slot, 2].astype(BF16)

    first_part, end_part = parts_ref[b] // 4, parts_ref[b] % 4
    for i in range(3):
        @pl.when((next_ref[b] >= 0) & (first_part <= i) & (i < end_part))
        def _():
            fetch(next_ref[b], 1 - epoch_ref[b] % 2, i).start(priority=1)

    def ffn(m):
        x = _load_row_tiles(xs_ref, m).astype(BF16)
        half = D_MODEL // 2
        acc = None
        for f in range(2):
            fs = slice(f * half, (f + 1) * half)
            gt = _dot(x, wg_b[:, fs]) + bg_ref[0, :, fs]
            up = _dot(x, wu_b[:, fs]) + bu_ref[0, :, fs]
            gt = jnp.minimum(gt, SWIGLU_LIMIT)
            up = jnp.clip(up, -SWIGLU_LIMIT, SWIGLU_LIMIT)
            hid = (up + 1.0) * gt * _sigmoid(SWIGLU_ALPHA * gt)
            part = _dot(hid.astype(BF16), wd_b[fs, :])
            acc = part if acc is None else acc + part
        _store_row_tiles(y_ref, acc + bd_ref[0])

    pieces = (rows_ref[b] + MOE_PIECE - 1) // MOE_PIECE
    for q in range(1, MOE_BLOCK // MOE_PIECE + 1):
        @pl.when((b < nu_ref[0]) & (pieces == q))
        def _():
            ffn(q * MOE_PIECE)


def _moe_ffn(block_expert, n_used, block_rows, xs, w_gate, w_up, w_down, b_gate, b_up, b_down):
    n_blocks = block_expert.shape[0]
    pos = jnp.arange(n_blocks, dtype=jnp.int32)
    change = (pos > 0) & (block_expert != jnp.roll(block_expert, 1))
    epoch = jnp.cumsum(change.astype(jnp.int32))
    later = change[None, :] & (pos[None, :] > pos[:, None])
    first = jnp.min(jnp.where(later, pos[None, :], n_blocks), axis=1)
    next_e = jnp.sum(jnp.where(pos[None, :] == first[:, None], block_expert[None, :], 0), axis=1)
    next_e = jnp.where(first < n_blocks, next_e, -1).astype(jnp.int32)
    run_start = jnp.max(jnp.where((pos[None, :] <= pos[:, None]) & (change | (pos == 0))[None, :], pos[None, :], 0),
                        axis=1)
    q = jnp.minimum(pos - run_start, 3)
    used = pos < n_used[0]
    is_last = jnp.roll(change, -1) | (pos == n_used[0] - 1)
    parts = jnp.where(used, q * 4 + jnp.where(is_last, 3, jnp.minimum(q + 1, 3)), 15).astype(jnp.int32)

    row = lambda b, be, nu, ep, nx, br, pt: (jnp.minimum(b, nu[0] - 1), 0)
    bspec = pl.BlockSpec((1, 1, D_MODEL), lambda b, be, nu, ep, nx, br, pt: (be[b], 0, 0))
    wspec = pl.BlockSpec(memory_space=pl.ANY)
    grid_spec = pltpu.PrefetchScalarGridSpec(
        num_scalar_prefetch=6,
        grid=(n_blocks,),
        in_specs=[pl.BlockSpec((MOE_BLOCK * ROW_TILE, 128), row), wspec, wspec, wspec, bspec, bspec, bspec],
        out_specs=pl.BlockSpec((MOE_BLOCK * ROW_TILE, 128), row),
        scratch_shapes=[pltpu.VMEM((2, 3, D_MODEL, D_MODEL), F32)] + [pltpu.VMEM((D_MODEL, D_MODEL), BF16)] * 3
        + [pltpu.SemaphoreType.DMA((2,))],
    )
    return pl.pallas_call(
        _moe_kernel,
        grid_spec=grid_spec,
        out_shape=jax.ShapeDtypeStruct((n_blocks * MOE_BLOCK * ROW_TILE, 128), F32),
        compiler_params=_params(("arbitrary",)),
    )(block_expert, n_used, epoch, next_e, block_rows, parts, xs, w_gate, w_up, w_down,
      b_gate.reshape(N_EXPERTS, 1, D_MODEL), b_up.reshape(N_EXPERTS, 1, D_MODEL),
      b_down.reshape(N_EXPERTS, 1, D_MODEL))


SEG_ALIGN = 8
GROUP_ROWS = SEG_ALIGN * ROW_TILE


def _local_rows(tm):
    return tm * TOP_K + N_EXPERTS * SEG_ALIGN


BIG_COPY = 4


def _copy_tables():
    n_big = _local_rows(TOK_BLOCK) // SEG_ALIGN // BIG_COPY
    n_small = N_EXPERTS * (BIG_COPY - 1)
    return 2 * n_big, 2 * (n_big + n_small)


def _issue_group_copies(cp_ref, hbm, buf, sem, to_hbm):
    small_at, counts_at = _copy_tables()

    def copy(at, n_groups):
        rows = n_groups * GROUP_ROWS
        b = buf.at[pl.ds(pl.multiple_of(cp_ref[0, 0, at] * GROUP_ROWS, GROUP_ROWS), rows)]
        h = hbm.at[pl.ds(pl.multiple_of(cp_ref[0, 0, at + 1] * GROUP_ROWS, GROUP_ROWS), rows)]
        return pltpu.make_async_copy(b, h, sem) if to_hbm else pltpu.make_async_copy(h, b, sem)

    def big(i, carry):
        copy(2 * i, BIG_COPY).start()
        return carry

    def small(i, carry):
        copy(small_at + 2 * i, 1).start(priority=1)
        return carry

    lax.fori_loop(0, cp_ref[0, 0, counts_at], big, 0)
    lax.fori_loop(0, cp_ref[0, 0, counts_at + 1], small, 0)


def _wait_group_copies(cp_ref, hbm, buf, sem, to_hbm):
    rows = pl.ds(0, pl.multiple_of(cp_ref[0, 0, _copy_tables()[1] + 2] * GROUP_ROWS, GROUP_ROWS))
    b, h = buf.at[rows], hbm.at[rows]
    (pltpu.make_async_copy(b, h, sem) if to_hbm else pltpu.make_async_copy(h, b, sem)).wait()


def _combine_kernel(grp_c, grp_n, lpos_ref, gate_ref, h_ref, gain_ref, y_hbm, o_ref, ybuf, fbuf, sems):
    i = pl.program_id(0)
    nt = pl.num_programs(0)
    tm = h_ref.shape[0]

    @pl.when(i == 0)
    def _():
        _issue_group_copies(grp_c, y_hbm, ybuf.at[0], sems.at[0], False)

    @pl.when(i + 1 < nt)
    def _():
        _issue_group_copies(grp_n, y_hbm, ybuf.at[(i + 1) % 2], sems.at[(i + 1) % 2], False)

    slot = i % 2
    yb = ybuf.at[slot]
    _wait_group_copies(grp_c, y_hbm, yb, sems.at[slot], False)

    def token_body(t, carry):
        acc = None
        for j in range(TOP_K):
            row = pl.multiple_of(lpos_ref[0, 0, t * TOP_K + j], ROW_TILE)
            term = gate_ref[0, 0, t * TOP_K + j] * yb[pl.ds(row, ROW_TILE), :]
            acc = term if acc is None else acc + term
        fbuf[pl.ds(pl.multiple_of(t * ROW_TILE, ROW_TILE), ROW_TILE), :] = acc
        return carry

    lax.fori_loop(0, tm, token_body, 0, unroll=8)
    f = h_ref[...] + _load_row_tiles(fbuf, tm)
    o_ref[...] = f * lax.rsqrt(jnp.mean(f * f, axis=-1, keepdims=True) + NORM_EPS) * gain_ref[...]


def _combine(h, y_rows, grp3, lpos3, gate3, gain, tm):
    n = h.shape[0]
    nt = n // tm
    n_local = _local_rows(tm)
    gw = grp3.shape[-1]
    smem = lambda shape, imap: pl.BlockSpec(shape, imap, memory_space=pltpu.SMEM)
    cur = lambda i: (i, 0, 0)
    nxt = lambda i: (jnp.minimum(i + 1, nt - 1), 0, 0)
    return pl.pallas_call(
        _combine_kernel,
        grid=(nt,),
        in_specs=[
            smem((1, 1, gw), cur), smem((1, 1, gw), nxt),
            smem((1, 1, TOP_K * tm), cur), smem((1, 1, TOP_K * tm), cur),
            pl.BlockSpec((tm, D_MODEL), lambda i: (i, 0)),
            pl.BlockSpec((1, D_MODEL), lambda i: (0, 0)),
            pl.BlockSpec(memory_space=pl.ANY),
        ],
        out_specs=pl.BlockSpec((tm, D_MODEL), lambda i: (i, 0)),
        out_shape=jax.ShapeDtypeStruct((n, D_MODEL), F32),
        scratch_shapes=[pltpu.VMEM((2, n_local * ROW_TILE, 128), F32), pltpu.VMEM((tm * ROW_TILE, 128), F32),
                        pltpu.SemaphoreType.DMA((2,))],
        compiler_params=pltpu.CompilerParams(dimension_semantics=("arbitrary",), vmem_limit_bytes=VMEM_LIMIT,
                                             disable_bounds_checks=True),
    )(grp3, grp3, lpos3, gate3, h, gain, y_rows)


def _dispatch_kernel(*refs, fill):
    if fill:
        grp_ref, grp_prev, lpos_ref, ends_ref, x_ref, xs_hbm, sorted_buf, zero_scr, sems, zsem = refs
    else:
        grp_ref, grp_prev, lpos_ref, x_ref, _, xs_hbm, sorted_buf, sems = refs
    i = pl.program_id(0)
    tm = x_ref.shape[0] // ROW_TILE
    blk = MOE_PIECE * ROW_TILE
    sorted_scr = sorted_buf.at[i % 2]
    sem = sems.at[i % 2]

    if fill:
        def fill_copy(e):
            start = pl.multiple_of(ends_ref[0, e] * ROW_TILE, blk)
            return pltpu.make_async_copy(zero_scr, xs_hbm.at[pl.ds(start, blk)], zsem)

        @pl.when(i == 0)
        def _():
            zero_scr[...] = jnp.zeros(zero_scr.shape, zero_scr.dtype)
            for e in range(N_EXPERTS):
                @pl.when(ends_ref[1, e] > 0)
                def _():
                    fill_copy(e).start()

    sorted_scr[...] = jnp.zeros(sorted_scr.shape, sorted_scr.dtype)

    def move(t, carry):
        row = x_ref[pl.ds(pl.multiple_of(t * ROW_TILE, ROW_TILE), ROW_TILE), :]
        for j in range(TOP_K):
            dst = pl.multiple_of(lpos_ref[0, 0, t * TOP_K + j], ROW_TILE)
            sorted_scr[pl.ds(dst, ROW_TILE), :] = row
        return carry

    lax.fori_loop(0, tm, move, 0, unroll=8)

    if fill:
        @pl.when(i == 0)
        def _():
            for e in range(N_EXPERTS):
                @pl.when(ends_ref[1, e] > 0)
                def _():
                    fill_copy(e).wait()

    @pl.when(i > 0)
    def _():
        _wait_group_copies(grp_prev, xs_hbm, sorted_buf.at[(i + 1) % 2], sems.at[(i + 1) % 2], True)

    _issue_group_copies(grp_ref, xs_hbm, sorted_scr, sem, True)

    @pl.when(i == pl.num_programs(0) - 1)
    def _():
        _wait_group_copies(grp_ref, xs_hbm, sorted_scr, sem, True)


def _dispatch(xn, grp3, lpos3, tm, n_slots, ends=None, xs=None):
    n = xn.shape[0] // ROW_TILE
    fill = xs is None
    smem = lambda shape, imap: pl.BlockSpec(shape, imap, memory_space=pltpu.SMEM)
    in_specs = [smem((1, 1, grp3.shape[-1]), lambda i: (i, 0, 0)),
                smem((1, 1, grp3.shape[-1]), lambda i: (jnp.maximum(i - 1, 0), 0, 0)),
                smem((1, 1, TOP_K * tm), lambda i: (i, 0, 0))]
    args = [grp3, grp3, lpos3]
    scratch = [pltpu.VMEM((2, _local_rows(tm) * ROW_TILE, 128), F32)]
    if fill:
        in_specs.append(smem((2, N_EXPERTS), lambda i: (0, 0)))
        args.append(ends)
        scratch.append(pltpu.VMEM((MOE_PIECE * ROW_TILE, 128), F32))
    in_specs.append(pl.BlockSpec((tm * ROW_TILE, 128), lambda i: (i, 0)))
    args.append(xn)
    aliases = {}
    if not fill:
        in_specs.append(pl.BlockSpec(memory_space=pl.ANY))
        args.append(xs)
        aliases = {len(args) - 1: 0}
    scratch.append(pltpu.SemaphoreType.DMA((2,)))
    if fill:
        scratch.append(pltpu.SemaphoreType.DMA(()))
    return pl.pallas_call(
        functools.partial(_dispatch_kernel, fill=fill),
        grid=(n // tm,),
        in_specs=in_specs,
        out_specs=pl.BlockSpec(memory_space=pl.ANY),
        out_shape=jax.ShapeDtypeStruct((n_slots * ROW_TILE, 128), F32),
        scratch_shapes=scratch,
        input_output_aliases=aliases,
        compiler_params=pltpu.CompilerParams(dimension_semantics=("arbitrary",), vmem_limit_bytes=VMEM_LIMIT,
                                             disable_bounds_checks=True, has_side_effects=True),
    )(*args)


def _pad_rows(w, rows, offset):
    out = jnp.zeros((rows, w.shape[1]), w.dtype)
    return out.at[offset : offset + w.shape[0]].set(w)


def _routing_tables(counts, n_pairs):
    n_tiles = counts.shape[0]
    n_blocks = (n_pairs + n_tiles * N_EXPERTS * (SEG_ALIGN - 1) + N_EXPERTS * (MOE_BLOCK - 1)
                + MOE_BLOCK - 1) // MOE_BLOCK
    runs = (counts + SEG_ALIGN - 1) // SEG_ALIGN * SEG_ALIGN
    local_start = jnp.cumsum(runs, axis=1) - runs
    total = jnp.sum(runs, axis=0)
    padded = (total + MOE_BLOCK - 1) // MOE_BLOCK * MOE_BLOCK
    pends = jnp.cumsum(padded)
    pstarts = pends - padded
    global_start = pstarts[None, :] + jnp.cumsum(runs, axis=0) - runs
    blocks = jnp.arange(n_blocks, dtype=jnp.int32) * MOE_BLOCK
    n_used = (pends[-1] // MOE_BLOCK).astype(jnp.int32)
    owner = jnp.sum((pends[None, :] <= blocks[:, None]).astype(jnp.int32), axis=1)
    block_expert = jnp.minimum(owner, N_EXPERTS - 1)
    last = jnp.sum(jnp.where(jnp.arange(n_blocks) == n_used - 1, block_expert, 0))
    block_expert = jnp.where(jnp.arange(n_blocks) < n_used, block_expert, last)
    real_end = pstarts + total
    ends = jnp.stack([real_end // MOE_PIECE * MOE_PIECE, real_end % MOE_PIECE]).astype(jnp.int32)
    row_end = jnp.sum(jnp.where(block_expert[:, None] == jnp.arange(N_EXPERTS), (pstarts + total)[None, :], 0), axis=1)
    block_rows = jnp.clip(row_end - blocks, 0, MOE_BLOCK).astype(jnp.int32)
    return (runs, local_start, global_start, ends, block_expert.astype(jnp.int32), n_used.reshape(1), block_rows,
            n_blocks)


def _local_positions(idx3, rank3, local_start):
    hit = idx3[..., None] == jnp.arange(N_EXPERTS, dtype=jnp.int32)
    lpos = rank3 + jnp.sum(jnp.where(hit, local_start[:, None, None, :], 0), axis=-1)
    return _token_major(lpos.astype(jnp.int32) * ROW_TILE)


def _copy_lists(runs, local_start, global_start):
    experts = jnp.arange(N_EXPERTS, dtype=jnp.int32)
    groups = runs // SEG_ALIGN
    l8, g8 = local_start // SEG_ALIGN, global_start // SEG_ALIGN
    n_big, n_small = groups // BIG_COPY, groups % BIG_COPY

    def copy_list(count, first_local, first_global, step, length):
        ends = jnp.cumsum(count, axis=1)
        o = jnp.arange(length, dtype=jnp.int32)
        owner = jnp.minimum(jnp.sum((ends[:, None, :] <= o[None, :, None]).astype(jnp.int32), axis=-1), N_EXPERTS - 1)
        sel = owner[..., None] == experts
        pick = lambda t: jnp.sum(jnp.where(sel, t[:, None, :], 0), axis=-1)
        k = (o[None, :] - pick(ends - count)) * step
        pairs = jnp.stack([pick(first_local) + k, pick(first_global) + k], axis=-1)
        live = (o[None, :] < ends[:, -1:])[..., None]
        return jnp.where(live, pairs, 0).reshape(count.shape[0], 2 * length)

    small_at, counts_at = _copy_tables()
    big = copy_list(n_big, l8, g8, BIG_COPY, small_at // 2)
    small = copy_list(n_small, l8 + n_big * BIG_COPY, g8 + n_big * BIG_COPY, 1, (counts_at - small_at) // 2)
    counts = jnp.stack([jnp.sum(n_big, axis=1), jnp.sum(n_small, axis=1), jnp.sum(groups, axis=1)], axis=1)
    counts = jnp.pad(counts, ((0, 0), (0, 5)))
    return jnp.concatenate([big, small, counts], axis=1).astype(jnp.int32)[:, None, :]


def _token_major(t3):
    return jnp.swapaxes(t3, 1, 2).reshape(t3.shape[0], 1, -1)


def kernel(x_prompt, x_sample, state_rwkv_shift, state_rwkv_wkv, state_gla, norm_mix, w_in, rw_mu, rw_w0, rw_w2, rw_a0, rw_a2, rw_g2, rw_k_k, rw_k_a, rw_r_k, rw_ln_w, rw_ln_b, gla_gk_w2, gla_gk_b, gla_norm_w, w_out, norm_ffn, w_router, b_router, w_gate, b_gate, w_up, b_up, w_down, b_down, norm_final):
    depth = norm_mix.shape[0]
    assert depth == 1
    bp, lp, d = x_prompt.shape
    bs, ls, _ = x_sample.shape
    assert ls == 1 and lp % SEQ_BLOCK == 0
    l = 0
    row = lambda t: t.reshape(1, -1)

    w_in_b = w_in[l].astype(BF16)
    w_in_r = w_in_b[:, :RW_PROJ]
    w_in_g = jnp.pad(w_in_b[:, RW_PROJ:], ((0, 0), (0, GLA_PROJ_PAD - GLA_PROJ)))
    rw = dict(
        mu=row(rw_mu[l]), w0=row(rw_w0[l]), a0=row(rw_a0[l]),
        w2p=_pad_rows(rw_w2[l].astype(BF16), 128, 0), a2p=_pad_rows(rw_a2[l].astype(BF16), 128, 64),
        g2=rw_g2[l].astype(BF16), k_k=row(rw_k_k[l]), k_a=row(rw_k_a[l]), r_k=row(rw_r_k[l]),
        ln_w=row(rw_ln_w[l]), ln_b=row(rw_ln_b[l]))
    gl = dict(gkw=_pad_rows(gla_gk_w2[l].astype(BF16), GLA_LORA_PAD, 0), gkb=row(gla_gk_b[l]),
              norm_w=row(gla_norm_w[l]))
    gain_mix = row(norm_mix[l])

    n_p = bp * lp
    xp = x_prompt.reshape(n_p, d)
    zr_p, zg_p = _inproj(xp, gain_mix, w_in_r, w_in_g, TOK_BLOCK)
    zr_p3 = zr_p.reshape(bp, lp, RW_PROJ)
    o_rw_p, wkv_p = _rwkv_seq(zr_p3, jnp.zeros((bp, 1, RW_PROJ), F32),
                              jnp.zeros((bp, RW_HEADS, RW_N, RW_N), F32), rw, SEQ_BLOCK)
    o_gl_p, gla_p = _gla_seq(zg_p.reshape(bp, lp, GLA_PROJ_PAD),
                             jnp.zeros((bp, GLA_HEADS, GLA_DK, GLA_DV), F32), gl, SEQ_BLOCK)
    shift_p = zr_p3[:, -1, :]

    xs_ = x_sample.reshape(bs, d)
    zr_s, zg_s = _inproj(xs_, gain_mix, w_in_r, w_in_g, bs)
    r, k, v, al, be, dec, g_rw, bonus = _rwkv_step_prep(zr_s, state_rwkv_shift[l], rw)
    wkv_s, o_rw_s = _rwkv_step(state_rwkv_wkv[l], r, k, al, be, dec, v)
    q, kg, vg, g_gl, dec_g = _gla_step_prep(zg_s, gl)
    gla_s, o_gl_s = _gla_step(state_gla[l], q, kg, dec_g, vg, 16)
    o_rw_s2, o_gl_s2 = _step_post(o_rw_s, bonus, g_rw, rw, o_gl_s, g_gl, gl)
    shift_s = zr_s

    w_out_b = w_out[l].astype(BF16)
    router = (w_out_b, row(norm_ffn[l]), w_router[l].T, b_router[l].reshape(N_EXPERTS, 1))
    h_p, xn_p, idx_p, gate_p, rank_p, cnt_p = _outproj_router(
        xp, o_rw_p.reshape(n_p, RW_WIDTH), o_gl_p.reshape(n_p, GLA_WIDTH), *router, TOK_BLOCK)
    h_s, xn_s, idx_s, gate_s, rank_s, cnt_s = _outproj_router(xs_, o_rw_s2, o_gl_s2, *router, bs)
    nt_p = n_p // TOK_BLOCK
    counts = jnp.concatenate([cnt_p[:, :, 0], cnt_s[:, :, 0]], axis=0)
    runs, lstart, gstart, ends, block_expert, n_used, block_rows, n_blocks = _routing_tables(
        counts, (n_p + bs) * TOP_K)
    n_slots = n_blocks * MOE_BLOCK
    lpos_p = _local_positions(idx_p, rank_p, lstart[:nt_p])
    lpos_s = _local_positions(idx_s, rank_s, lstart[nt_p:])
    grp = _copy_lists(runs, lstart, gstart)
    grp_p, grp_s = grp[:nt_p], grp[nt_p:]
    xs_rows = _dispatch(xn_p, grp_p, lpos_p, TOK_BLOCK, n_slots, ends=ends)
    xs_rows = _dispatch(xn_s, grp_s, lpos_s, bs, n_slots, xs=xs_rows)
    y_rows = _moe_ffn(block_expert, n_used, block_rows, xs_rows, w_gate[l], w_up[l], w_down[l], b_gate[l], b_up[l], b_down[l])
    gain_f = row(norm_final)
    y_p = _combine(h_p, y_rows, grp_p, lpos_p, _token_major(gate_p), gain_f, TOK_BLOCK)
    y_s = _combine(h_s, y_rows, grp_s, lpos_s, _token_major(gate_s), gain_f, bs)

    y_prompt = y_p.reshape(bp, lp, d)
    y_sample = y_s.reshape(bs, ls, d)
    return (y_prompt, y_sample, shift_p[None], wkv_p[None], gla_p[None], shift_s[None], wkv_s[None], gla_s[None])
```
